```python
import jax, jax.numpy as jnp
from jax import lax
import numpy as np

D_MODEL = 1024
BATCH = 8
SEQ = 8192
DEPTH = 2

PLE_DIM = 256
N_MIXERS = 2
N_FOX_LAYERS = (DEPTH + 1) // 2
N_DIL_LAYERS = DEPTH // 2

FOX_HEADS = 16
FOX_HEAD_DIM = D_MODEL // FOX_HEADS
FOX_WIDTH = FOX_HEADS * FOX_HEAD_DIM
FOX_IN = 4 * FOX_WIDTH + FOX_HEADS
QUERY_BLOCK = 128
FORGET_BIAS_CENTER = 2.0

DIL_PATTERN = ((128, 1), (512, 4), (2048, 16))
DIL_GROUPS = len(DIL_PATTERN)
DIL_HEADS_PER_GROUP = 8
DIL_HEAD_DIM = D_MODEL // DIL_HEADS_PER_GROUP
DIL_HEADS = DIL_GROUPS * DIL_HEADS_PER_GROUP
DIL_QKV = DIL_HEADS * DIL_HEAD_DIM
DIL_WIDTH = DIL_HEADS_PER_GROUP * DIL_HEAD_DIM
DIL_IN = 3 * DIL_QKV + DIL_WIDTH
ALIBI_MAX_EXP = 8.0

RMS_EPS = 1e-6

kernel_name = "fox_dilated_hybrid_trunk"


def rms_norm(x, g):
    xf = x.astype(jnp.float32)
    y = xf * lax.rsqrt(jnp.mean(xf * xf, axis=-1, keepdims=True) + RMS_EPS)
    return (y * g.astype(jnp.float32)).astype(x.dtype)


def alibi_slopes(n):
    return 2.0 ** (-ALIBI_MAX_EXP * jnp.arange(1, n + 1, dtype=jnp.float32) / n)


def fox_mixer(h, w_in, b_f, w_out):
    B, S, _ = h.shape
    proj = h @ w_in
    q = proj[..., :FOX_WIDTH].reshape(B, S, FOX_HEADS, FOX_HEAD_DIM)
    k = proj[..., FOX_WIDTH:2 * FOX_WIDTH].reshape(B, S, FOX_HEADS, FOX_HEAD_DIM)
    v = proj[..., 2 * FOX_WIDTH:3 * FOX_WIDTH].reshape(B, S, FOX_HEADS, FOX_HEAD_DIM)
    z = proj[..., 3 * FOX_WIDTH:4 * FOX_WIDTH]
    f_logit = proj[..., 4 * FOX_WIDTH:]
    log_f = jax.nn.log_sigmoid((f_logit + b_f).astype(jnp.float32))
    c = jnp.cumsum(log_f, axis=1)
    nb = S // QUERY_BLOCK
    qb = q.reshape(B, nb, QUERY_BLOCK, FOX_HEADS, FOX_HEAD_DIM).transpose(1, 0, 2, 3, 4)
    cqb = c.reshape(B, nb, QUERY_BLOCK, FOX_HEADS).transpose(1, 0, 3, 2)
    qpos = jnp.arange(S).reshape(nb, QUERY_BLOCK)
    kpos = jnp.arange(S)
    ck = c.transpose(0, 2, 1)
    scale = FOX_HEAD_DIM ** -0.5

    def block(args):
        qi, ci, pi = args
        s = jnp.einsum('bqhd,bkhd->bhqk', qi, k).astype(jnp.float32) * scale
        s = s + ci[..., None] - ck[:, :, None, :]
        s = jnp.where((kpos[None, :] <= pi[:, None])[None, None], s, -jnp.inf)
        pr = jax.nn.softmax(s, axis=-1)
        return jnp.einsum('bhqk,bkhd->bqhd', pr.astype(v.dtype), v)

    o = lax.map(block, (qb, cqb, qpos))
    o = o.transpose(1, 0, 2, 3, 4).reshape(B, S, FOX_WIDTH)
    return (o * jax.nn.silu(z)) @ w_out


def dilated_window_attention(q, k, v, slopes, window, dilation):
    B, S, Hg, hd = q.shape
    L = S // dilation
    nW = window // dilation
    nb = -(-L // nW)
    Lp = nb * nW
    Bd = B * dilation

    def to_blocks(t):
        t = t.reshape(B, L, dilation, Hg, hd).transpose(0, 2, 1, 3, 4).reshape(Bd, L, Hg, hd)
        t = jnp.pad(t, ((0, 0), (0, Lp - L), (0, 0), (0, 0)))
        return t.reshape(Bd, nb, nW, Hg, hd)

    def with_prev(t):
        prev = jnp.pad(t, ((0, 0), (1, 0), (0, 0), (0, 0), (0, 0)))[:, :-1]
        return jnp.concatenate([prev, t], axis=2)

    qb = to_blocks(q)
    kk = with_prev(to_blocks(k))
    vv = with_prev(to_blocks(v))
    i = jnp.arange(nW)[:, None]
    j = jnp.arange(2 * nW)[None, :]
    dist = nW + i - j
    key_pos = (jnp.arange(nb)[:, None, None] - 1) * nW + j[None]
    valid = (dist >= 0)[None] & (dist <= nW)[None] & (key_pos >= 0)
    bias = -slopes.astype(jnp.float32)[:, None, None] * (dist * dilation).astype(jnp.float32)
    scale = hd ** -0.5
    s = jnp.einsum('znqhd,znkhd->znhqk', qb, kk).astype(jnp.float32) * scale + bias[None, None]
    s = jnp.where(valid[None, :, None], s, -jnp.inf)
    lse = jax.nn.logsumexp(s, axis=-1, keepdims=True)
    pr = jnp.exp(s - lse)
    o = jnp.einsum('znhqk,znkhd->znqhd', pr.astype(v.dtype), vv)
    o = o.reshape(Bd, Lp, Hg, hd)[:, :L]
    o = o.reshape(B, dilation, L, Hg, hd).transpose(0, 2, 1, 3, 4).reshape(B, S, Hg, hd)
    lse = lse[..., 0].transpose(0, 1, 3, 2).reshape(Bd, Lp, Hg)[:, :L]
    lse = lse.reshape(B, dilation, L, Hg).transpose(0, 2, 1, 3).reshape(B, S, Hg)
    return o, lse


def dilated_mixer(h, w_in, w_out):
    B, S, _ = h.shape
    proj = h @ w_in
    shp = (B, S, DIL_GROUPS, DIL_HEADS_PER_GROUP, DIL_HEAD_DIM)
    q = proj[..., :DIL_QKV].reshape(shp)
    k = proj[..., DIL_QKV:2 * DIL_QKV].reshape(shp)
    v = proj[..., 2 * DIL_QKV:3 * DIL_QKV].reshape(shp)
    z = proj[..., 3 * DIL_QKV:]
    slopes = alibi_slopes(DIL_HEADS).reshape(DIL_GROUPS, DIL_HEADS_PER_GROUP)
    outs, lses = [], []
    for g, (window, dilation) in enumerate(DIL_PATTERN):
        o, l = dilated_window_attention(q[:, :, g], k[:, :, g], v[:, :, g], slopes[g], window, dilation)
        outs.append(o)
        lses.append(l)
    wts = jax.nn.softmax(jnp.stack(lses), axis=0)
    o = jnp.sum(wts[..., None] * jnp.stack(outs).astype(jnp.float32), axis=0)
    o = o.astype(h.dtype).reshape(B, S, DIL_WIDTH)
    return (o * jax.nn.silu(z)) @ w_out


def _fwd_setup_inputs(seed: int = 0) -> dict:
    key = jax.random.key(seed)
    ks = jax.random.split(key, 14)
    f32 = jnp.float32
    nrm = lambda k, shape, fan_in: jax.random.normal(k, shape, f32) * fan_in ** -0.5
    return {
        "x": jax.random.normal(ks[0], (BATCH, SEQ, D_MODEL), f32),
        "p": jax.random.normal(ks[1], (DEPTH, BATCH, SEQ, PLE_DIM), f32),
        "fox_norm": 1.0 + 0.02 * jax.random.normal(ks[2], (N_FOX_LAYERS, D_MODEL), f32),
        "fox_w_in": nrm(ks[3], (N_FOX_LAYERS, D_MODEL, FOX_IN), D_MODEL),
        "fox_b_f": FORGET_BIAS_CENTER + 0.5 * jax.random.normal(ks[4], (N_FOX_LAYERS, FOX_HEADS), f32),
        "fox_w_out": nrm(ks[5], (N_FOX_LAYERS, FOX_WIDTH, D_MODEL), FOX_WIDTH),
        "dil_norm": 1.0 + 0.02 * jax.random.normal(ks[6], (N_DIL_LAYERS, D_MODEL), f32),
        "dil_w_in": nrm(ks[7], (N_DIL_LAYERS, D_MODEL, DIL_IN), D_MODEL),
        "dil_w_out": nrm(ks[8], (N_DIL_LAYERS, DIL_WIDTH, D_MODEL), DIL_WIDTH),
        "ple_w_up": nrm(ks[9], (DEPTH, PLE_DIM, D_MODEL), PLE_DIM),
        "ple_w_gate": nrm(ks[10], (DEPTH, D_MODEL, D_MODEL), D_MODEL),
        "final_norm": 1.0 + 0.02 * jax.random.normal(ks[11], (D_MODEL,), f32),
    }


def _fwd_reference(x, p, fox_norm, fox_w_in, fox_b_f, fox_w_out, dil_norm, dil_w_in, dil_w_out,
              ple_w_up, ple_w_gate, final_norm):
    h = x
    for i in range(DEPTH):
        j = i // N_MIXERS
        if i % N_MIXERS == 0:
            h = h + fox_mixer(rms_norm(h, fox_norm[j]), fox_w_in[j], fox_b_f[j], fox_w_out[j])
        else:
            h = h + dilated_mixer(rms_norm(h, dil_norm[j]), dil_w_in[j], dil_w_out[j])
        h = h + (p[i] @ ple_w_up[i]) * jax.nn.sigmoid(h @ ple_w_gate[i])
    return rms_norm(h, final_norm)


import jax as _jax
import jax.numpy as _jnp

TWIN_FORMAT = 'train_step'
FWD_PARAMS = ['x', 'p', 'fox_norm', 'fox_w_in', 'fox_b_f', 'fox_w_out', 'dil_norm', 'dil_w_in', 'dil_w_out', 'ple_w_up', 'ple_w_gate', 'final_norm']
TWIN_WEIGHTS = ['fox_norm', 'fox_w_in', 'fox_b_f', 'fox_w_out', 'dil_norm', 'dil_w_in', 'dil_w_out', 'ple_w_up', 'ple_w_gate', 'final_norm']
TWIN_DIFF_INPUT = 'x'
TWIN_INPUTS = ['x', 'p', 'fox_norm', 'fox_w_in', 'fox_b_f', 'fox_w_out', 'dil_norm', 'dil_w_in', 'dil_w_out', 'ple_w_up', 'ple_w_gate', 'final_norm', 'loss_target', 'm_fox_norm', 'm_fox_w_in', 'm_fox_b_f', 'm_fox_w_out', 'm_dil_norm', 'm_dil_w_in', 'm_dil_w_out', 'm_ple_w_up', 'm_ple_w_gate', 'm_final_norm', 'v_fox_norm', 'v_fox_w_in', 'v_fox_b_f', 'v_fox_w_out', 'v_dil_norm', 'v_dil_w_in', 'v_dil_w_out', 'v_ple_w_up', 'v_ple_w_gate', 'v_final_norm']
TWIN_OUTPUTS = ['loss', 'grad_x', 'grad_fox_norm', 'grad_fox_w_in', 'grad_fox_b_f', 'grad_fox_w_out', 'grad_dil_norm', 'grad_dil_w_in', 'grad_dil_w_out', 'grad_ple_w_up', 'grad_ple_w_gate', 'grad_final_norm', 'delta_fox_norm', 'delta_fox_w_in', 'delta_fox_b_f', 'delta_fox_w_out', 'delta_dil_norm', 'delta_dil_w_in', 'delta_dil_w_out', 'delta_ple_w_up', 'delta_ple_w_gate', 'delta_final_norm', 'new_m_fox_norm', 'new_m_fox_w_in', 'new_m_fox_b_f', 'new_m_fox_w_out', 'new_m_dil_norm', 'new_m_dil_w_in', 'new_m_dil_w_out', 'new_m_ple_w_up', 'new_m_ple_w_gate', 'new_m_final_norm', 'new_v_fox_norm', 'new_v_fox_w_in', 'new_v_fox_b_f', 'new_v_fox_w_out', 'new_v_dil_norm', 'new_v_dil_w_in', 'new_v_dil_w_out', 'new_v_ple_w_up', 'new_v_ple_w_gate', 'new_v_final_norm']
TWIN_LEAF_KINDS = {'loss': 'loss', 'grad_x': 'grad_x', 'grad_fox_norm': 'grad_w', 'grad_fox_w_in': 'grad_w', 'grad_fox_b_f': 'grad_w', 'grad_fox_w_out': 'grad_w', 'grad_dil_norm': 'grad_w', 'grad_dil_w_in': 'grad_w', 'grad_dil_w_out': 'grad_w', 'grad_ple_w_up': 'grad_w', 'grad_ple_w_gate': 'grad_w', 'grad_final_norm': 'grad_w', 'delta_fox_norm': 'delta_w', 'delta_fox_w_in': 'delta_w', 'delta_fox_b_f': 'delta_w', 'delta_fox_w_out': 'delta_w', 'delta_dil_norm': 'delta_w', 'delta_dil_w_in': 'delta_w', 'delta_dil_w_out': 'delta_w', 'delta_ple_w_up': 'delta_w', 'delta_ple_w_gate': 'delta_w', 'delta_final_norm': 'delta_w', 'new_m_fox_norm': 'new_m', 'new_m_fox_w_in': 'new_m', 'new_m_fox_b_f': 'new_m', 'new_m_fox_w_out': 'new_m', 'new_m_dil_norm': 'new_m', 'new_m_dil_w_in': 'new_m', 'new_m_dil_w_out': 'new_m', 'new_m_ple_w_up': 'new_m', 'new_m_ple_w_gate': 'new_m', 'new_m_final_norm': 'new_m', 'new_v_fox_norm': 'new_v', 'new_v_fox_w_in': 'new_v', 'new_v_fox_b_f': 'new_v', 'new_v_fox_w_out': 'new_v', 'new_v_dil_norm': 'new_v', 'new_v_dil_w_in': 'new_v', 'new_v_dil_w_out': 'new_v', 'new_v_ple_w_up': 'new_v', 'new_v_ple_w_gate': 'new_v', 'new_v_final_norm': 'new_v'}


def _forward(args):
    return _fwd_reference(*[args[k] for k in FWD_PARAMS])


def _output_shape():
    def fwd():
        inp = _fwd_setup_inputs(0)
        return _fwd_reference(*[inp[k] for k in FWD_PARAMS])
    out = _jax.eval_shape(fwd)
    return out.shape, out.dtype

N_MICROBATCH = 1
ADAM_LR = 0.001
ADAM_B1 = 0.9
ADAM_B2 = 0.999
ADAM_EPS = 1e-08
ADAM_WD = 0.01
ADAM_STEP = 10
PER_EXAMPLE_BATCH_AXIS = {'x': 0, 'p': 1, 'loss_target': 0}
SHARED_INPUTS = []
_WEIGHT_DTYPES = {'fox_norm': _jnp.float32, 'fox_w_in': _jnp.float32, 'fox_b_f': _jnp.float32, 'fox_w_out': _jnp.float32, 'dil_norm': _jnp.float32, 'dil_w_in': _jnp.float32, 'dil_w_out': _jnp.float32, 'ple_w_up': _jnp.float32, 'ple_w_gate': _jnp.float32, 'final_norm': _jnp.float32}
MOMENT_SCALE = {'fox_norm': 9.697801e-02, 'fox_w_in': 4.861164e-02, 'fox_b_f': 3.025040e-01, 'fox_w_out': 5.446748e-02, 'dil_norm': 7.420386e-02, 'dil_w_in': 2.292519e-02, 'dil_w_out': 3.939331e-02, 'ple_w_up': 1.088862e-01, 'ple_w_gate': 4.514757e-02, 'final_norm': 6.404767e+01}


def _to_microbatches(a, axis):
    t = _jnp.moveaxis(a, axis, 0)
    t = t.reshape((N_MICROBATCH, t.shape[0] // N_MICROBATCH) + t.shape[1:])
    return _jnp.moveaxis(t, 1, axis + 1)


def setup_inputs(seed: int = 0) -> dict:
    inp = _fwd_setup_inputs(seed)
    key = _jax.random.fold_in(_jax.random.key(seed), 7919)
    shape, _ = _output_shape()
    out = dict(inp)
    out["loss_target"] = _jax.random.normal(_jax.random.fold_in(key, 0), shape, _jnp.float32)
    for i, name in enumerate(TWIN_WEIGHTS):
        w = inp[name].astype(_jnp.float32)
        if MOMENT_SCALE is None:
            s = _jnp.sqrt(_jnp.mean(_jnp.square(w)) + 1e-30)
        else:
            s = MOMENT_SCALE[name]
        km, kv = _jax.random.split(_jax.random.fold_in(key, i + 1))
        out[name] = w
        out["m_" + name] = s * _jax.random.normal(km, w.shape, _jnp.float32)
        out["v_" + name] = (s * s) * _jax.random.uniform(kv, w.shape, _jnp.float32, 0.5, 1.5)
    if N_MICROBATCH > 1:
        for name, axis in PER_EXAMPLE_BATCH_AXIS.items():
            out[name] = _to_microbatches(out[name], axis)
    return {'x': out['x'], 'p': out['p'], 'fox_norm': out['fox_norm'], 'fox_w_in': out['fox_w_in'], 'fox_b_f': out['fox_b_f'], 'fox_w_out': out['fox_w_out'], 'dil_norm': out['dil_norm'], 'dil_w_in': out['dil_w_in'], 'dil_w_out': out['dil_w_out'], 'ple_w_up': out['ple_w_up'], 'ple_w_gate': out['ple_w_gate'], 'final_norm': out['final_norm'], 'loss_target': out['loss_target'], 'm_fox_norm': out['m_fox_norm'], 'm_fox_w_in': out['m_fox_w_in'], 'm_fox_b_f': out['m_fox_b_f'], 'm_fox_w_out': out['m_fox_w_out'], 'm_dil_norm': out['m_dil_norm'], 'm_dil_w_in': out['m_dil_w_in'], 'm_dil_w_out': out['m_dil_w_out'], 'm_ple_w_up': out['m_ple_w_up'], 'm_ple_w_gate': out['m_ple_w_gate'], 'm_final_norm': out['m_final_norm'], 'v_fox_norm': out['v_fox_norm'], 'v_fox_w_in': out['v_fox_w_in'], 'v_fox_b_f': out['v_fox_b_f'], 'v_fox_w_out': out['v_fox_w_out'], 'v_dil_norm': out['v_dil_norm'], 'v_dil_w_in': out['v_dil_w_in'], 'v_dil_w_out': out['v_dil_w_out'], 'v_ple_w_up': out['v_ple_w_up'], 'v_ple_w_gate': out['v_ple_w_gate'], 'v_final_norm': out['v_final_norm']}


def _loss(weights, diff, rest, loss_target):
    with _jax.named_scope("forward"):
        args = {**rest, TWIN_DIFF_INPUT: diff, **{k: w.astype(_WEIGHT_DTYPES[k]) for k, w in weights.items()}}
        y = _forward(args)
    with _jax.named_scope("loss_head"):
        err = _jnp.square(y.astype(_jnp.float32) - loss_target)
        return 0.5 * _jnp.sum(_jnp.mean(err, axis=-1)) if err.ndim else 0.5 * err


def _adamw(w, g, m, v):
    m = ADAM_B1 * m + (1.0 - ADAM_B1) * g
    v = ADAM_B2 * v + (1.0 - ADAM_B2) * _jnp.square(g)
    m_hat = m / (1.0 - ADAM_B1 ** ADAM_STEP)
    v_hat = v / (1.0 - ADAM_B2 ** ADAM_STEP)
    delta = -ADAM_LR * (m_hat / (_jnp.sqrt(v_hat) + ADAM_EPS) + ADAM_WD * w)
    return delta, m, v


def reference(x, p, fox_norm, fox_w_in, fox_b_f, fox_w_out, dil_norm, dil_w_in, dil_w_out, ple_w_up, ple_w_gate, final_norm, loss_target, m_fox_norm, m_fox_w_in, m_fox_b_f, m_fox_w_out, m_dil_norm, m_dil_w_in, m_dil_w_out, m_ple_w_up, m_ple_w_gate, m_final_norm, v_fox_norm, v_fox_w_in, v_fox_b_f, v_fox_w_out, v_dil_norm, v_dil_w_in, v_dil_w_out, v_ple_w_up, v_ple_w_gate, v_final_norm):
    given = dict(x=x, p=p, fox_norm=fox_norm, fox_w_in=fox_w_in, fox_b_f=fox_b_f, fox_w_out=fox_w_out, dil_norm=dil_norm, dil_w_in=dil_w_in, dil_w_out=dil_w_out, ple_w_up=ple_w_up, ple_w_gate=ple_w_gate, final_norm=final_norm, loss_target=loss_target, m_fox_norm=m_fox_norm, m_fox_w_in=m_fox_w_in, m_fox_b_f=m_fox_b_f, m_fox_w_out=m_fox_w_out, m_dil_norm=m_dil_norm, m_dil_w_in=m_dil_w_in, m_dil_w_out=m_dil_w_out, m_ple_w_up=m_ple_w_up, m_ple_w_gate=m_ple_w_gate, m_final_norm=m_final_norm, v_fox_norm=v_fox_norm, v_fox_w_in=v_fox_w_in, v_fox_b_f=v_fox_b_f, v_fox_w_out=v_fox_w_out, v_dil_norm=v_dil_norm, v_dil_w_in=v_dil_w_in, v_dil_w_out=v_dil_w_out, v_ple_w_up=v_ple_w_up, v_ple_w_gate=v_ple_w_gate, v_final_norm=v_final_norm)
    weights = {n: given[n] for n in TWIN_WEIGHTS}
    shared = {n: given[n] for n in SHARED_INPUTS}
    per_example = {n: given[n] for n in ['x', 'p']}
    grad_fn = _jax.value_and_grad(_loss, argnums=(0, 1))

    def one_microbatch(ex, loss_target):
        ex = dict(ex)
        diff = ex.pop(TWIN_DIFF_INPUT)
        return grad_fn(weights, diff, {**shared, **ex}, loss_target)

    if N_MICROBATCH == 1:
        loss, (grad_w, grad_x) = one_microbatch(per_example, given["loss_target"])
    else:
        def body(carry, xs):
            loss_sum, grad_sum = carry
            l_k, (gw_k, gx_k) = one_microbatch(xs[0], xs[1])
            with _jax.named_scope("update"):
                return (loss_sum + l_k, _jax.tree.map(_jnp.add, grad_sum, gw_k)), gx_k

        init = (_jnp.zeros((), _jnp.float32), _jax.tree.map(_jnp.zeros_like, weights))
        (loss, grad_w), grad_x = _jax.lax.scan(body, init, (per_example, given["loss_target"]))
    with _jax.named_scope("update"):
        delta_w, new_m, new_v = {}, {}, {}
        for n in TWIN_WEIGHTS:
            delta_w[n], new_m[n], new_v[n] = _adamw(weights[n], grad_w[n], given["m_" + n], given["v_" + n])
    return (loss, grad_x, *[grad_w[n] for n in TWIN_WEIGHTS], *[delta_w[n] for n in TWIN_WEIGHTS],
            *[new_m[n] for n in TWIN_WEIGHTS], *[new_v[n] for n in TWIN_WEIGHTS])
```

```python
import functools

import jax
import jax.numpy as jnp
from jax import lax
from jax.experimental import pallas as pl
from jax.experimental.pallas import tpu as pltpu

F32 = jnp.float32
BF16 = jnp.bfloat16
SDS = jax.ShapeDtypeStruct

D_MODEL = 1024
N_DEV = 8
LANES = 128
FOX_HEADS = 16
FOX_HEAD_DIM = 64
FOX_PAIRS = FOX_HEADS // 2
DIL_HEADS = 8
DIL_BLOCK = 128
DIL_PATTERN = ((128, 1), (512, 4), (2048, 16))
ALIBI_MAX_EXP = 8.0
RMS_EPS = 1e-6
ADAM_LR, ADAM_B1, ADAM_B2, ADAM_EPS, ADAM_WD, ADAM_STEP = 0.001, 0.9, 0.999, 1e-08, 0.01, 10
VMEM_LIMIT = 48 * 1024 * 1024
NEG_INF = float("-inf")

NN = (((1,), (0,)), ((), ()))
NT = (((1,), (1,)), ((), ()))
TN = (((0,), (0,)), ((), ()))
MESH = pl.DeviceIdType.MESH


def _pcall(body, **kw):
    return pl.pallas_call(body, **kw)


def _params(**kw):
    return pltpu.CompilerParams(vmem_limit_bytes=VMEM_LIMIT, **kw)


def _dot(a, b, dims):
    return lax.dot_general(a, b, dims, preferred_element_type=F32)


def _sigmoid(x):
    return 1.0 / (1.0 + jnp.exp(-x))


def _mm(a_parts, b_parts, mode, name, tiles=(512, 512, 512), extras=(), outs=None, epilogue=None, out_dtype=BF16):
    na, nb = len(a_parts), len(b_parts)
    if mode == "tn":
        kdim, m = a_parts[0].shape
        n_part = b_parts[0].shape[1]
        n = nb * n_part
        k_part = kdim
    else:
        m, k_part = a_parts[0].shape
        n = b_parts[0].shape[1] if mode == "nn" else b_parts[0].shape[0]
        n_part = n
    tm, tn, tk = min(tiles[0], m), min(tiles[1], n_part), min(tiles[2], k_part)
    kb = k_part // tk
    jb = n_part // tn
    nk = na * kb if mode != "tn" else kb
    grid = (m // tm, n // tn, nk)

    def part_idx(idx, s, per):
        return jnp.clip(idx - s * per, 0, per - 1)

    in_specs = []
    for s in range(na):
        if mode == "tn":
            in_specs.append(pl.BlockSpec((tk, tm), lambda i, j, k: (k, i)))
        else:
            in_specs.append(pl.BlockSpec((tm, tk), lambda i, j, k, s=s: (i, part_idx(k, s, kb))))
    for s in range(nb):
        if mode == "nn":
            in_specs.append(pl.BlockSpec((tk, tn), lambda i, j, k: (k, j)))
        elif mode == "nt":
            in_specs.append(pl.BlockSpec((tn, tk), lambda i, j, k: (j, k)))
        else:
            in_specs.append(pl.BlockSpec((tk, tn), lambda i, j, k, s=s: (k, part_idx(j, s, jb))))
    for _, blk, imap in extras:
        in_specs.append(pl.BlockSpec(blk, imap))
    if outs is None:
        outs = [(SDS((m, n), out_dtype), (tm, tn), lambda i, j, k: (i, j))]
    out_specs = [pl.BlockSpec(blk, imap) for _, blk, imap in outs]
    ne, no = len(extras), len(outs)
    dims = {"nn": NN, "nt": NT, "tn": TN}[mode]

    def body(*refs):
        a_refs = refs[:na]
        b_refs = refs[na:na + nb]
        e_refs = refs[na + nb:na + nb + ne]
        o_refs = refs[na + nb + ne:na + nb + ne + no]
        acc = refs[-1]
        i, j, k = pl.program_id(0), pl.program_id(1), pl.program_id(2)

        @pl.when(k == 0)
        def _():
            acc[...] = jnp.zeros_like(acc)

        def step(a_ref, b_ref):
            acc[...] += _dot(a_ref[...].astype(BF16), b_ref[...].astype(BF16), dims)

        if mode == "tn":
            for s in range(nb):
                if nb == 1:
                    step(a_refs[0], b_refs[0])
                else:
                    pl.when((j >= s * jb) & (j < (s + 1) * jb))(functools.partial(step, a_refs[0], b_refs[s]))
        else:
            for s in range(na):
                if na == 1:
                    step(a_refs[0], b_refs[0])
                else:
                    pl.when((k >= s * kb) & (k < (s + 1) * kb))(functools.partial(step, a_refs[s], b_refs[0]))

        @pl.when(k == nk - 1)
        def _():
            if epilogue is None:
                o_refs[0][...] = acc[...].astype(o_refs[0].dtype)
            else:
                epilogue(acc[...], e_refs, o_refs, i)

    res = _pcall(
        body, name=name, grid=grid, in_specs=in_specs, out_specs=out_specs,
        out_shape=[o[0] for o in outs], scratch_shapes=[pltpu.VMEM((tm, tn), F32)],
        compiler_params=_params(dimension_semantics=("arbitrary", "arbitrary", "arbitrary")),
    )(*a_parts, *b_parts, *[e[0] for e in extras])
    return res[0] if len(res) == 1 else res


def _add_extra_epilogue(acc, e_refs, o_refs, i):
    o_refs[0][...] = acc + e_refs[0][...]


def _mm_residual(a, b, mode, res, name):
    m = a.shape[0]
    n = b.shape[1] if mode == "nn" else b.shape[0]
    tm, tn = 512, 512
    return _mm([a], [b], mode, name, tiles=(tm, tn, 512),
               extras=[(res, (tm, tn), lambda i, j, k: (i, j))],
               outs=[(SDS((m, n), F32), (tm, tn), lambda i, j, k: (i, j))],
               epilogue=_add_extra_epilogue)


def _rms_fwd(h, g, name):
    t, d = h.shape
    tm = 512

    def body(h_ref, g_ref, n_ref, r_ref):
        x = h_ref[...]
        r = lax.rsqrt(jnp.mean(x * x, axis=-1, keepdims=True) + RMS_EPS)
        n_ref[...] = ((x * r) * g_ref[...]).astype(BF16)
        r_ref[...] = r

    return _pcall(
        body, name=name, grid=(t // tm,),
        in_specs=[pl.BlockSpec((tm, d), lambda i: (i, 0)), pl.BlockSpec((1, d), lambda i: (0, 0))],
        out_specs=[pl.BlockSpec((tm, d), lambda i: (i, 0)), pl.BlockSpec((tm, 1), lambda i: (i, 0))],
        out_shape=[SDS((t, d), BF16), SDS((t, 1), F32)],
        compiler_params=_params(),
    )(h, g)


def _rms_bwd_rows(dn, x, g, r):
    xhat = x * r
    dxhat = dn * g
    dx = r * (dxhat - xhat * jnp.mean(dxhat * xhat, axis=-1, keepdims=True))
    dg = jnp.sum(dn * xhat, axis=0, keepdims=True)
    return dx, dg


def _mm_in_bwd(d_parts, w, h, g, r, dres, name, more=None):
    t = h.shape[0]
    tm = 256
    row = lambda i, j, k: (i, 0)
    extras = [(h, (tm, D_MODEL), row), (g, (1, D_MODEL), lambda i, j, k: (0, 0)), (r, (tm, 1), row),
              (dres, (tm, D_MODEL), row)]
    if more is not None:
        extras.append((more, (tm, D_MODEL), row))

    def epilogue(acc, e_refs, o_refs, i):
        dn = acc if more is None else acc + e_refs[4][...]
        dx, dg = _rms_bwd_rows(dn, e_refs[0][...], e_refs[1][...], e_refs[2][...])
        o_refs[0][...] = e_refs[3][...] + dx

        @pl.when(i == 0)
        def _():
            o_refs[1][...] = dg

        @pl.when(i > 0)
        def _():
            o_refs[1][...] += dg

    return _mm(d_parts, [w], "nt", name, tiles=(tm, D_MODEL, 512), extras=extras,
               outs=[(SDS((t, D_MODEL), F32), (tm, D_MODEL), row),
                     (SDS((1, D_MODEL), F32), (1, D_MODEL), lambda i, j, k: (0, 0))],
               epilogue=epilogue)


def _final_bwd(h, g, tgt):
    t, d = h.shape
    tm = 256

    def body(h_ref, g_ref, t_ref, dh_ref, dg_ref, loss_ref):
        i = pl.program_id(0)
        x = h_ref[...]
        gg = g_ref[...]
        r = lax.rsqrt(jnp.mean(x * x, axis=-1, keepdims=True) + RMS_EPS)
        err = (x * r) * gg - t_ref[...]
        part = 0.5 * jnp.sum(jnp.mean(err * err, axis=-1, keepdims=True), axis=0, keepdims=True)
        dx, dg = _rms_bwd_rows(err * (1.0 / d), x, gg, r)
        dh_ref[...] = dx

        @pl.when(i == 0)
        def _():
            dg_ref[...] = dg
            loss_ref[...] = jnp.broadcast_to(part, loss_ref.shape)

        @pl.when(i > 0)
        def _():
            dg_ref[...] += dg
            loss_ref[...] += jnp.broadcast_to(part, loss_ref.shape)

    return _pcall(
        body, name="final_norm_loss", grid=(t // tm,),
        in_specs=[pl.BlockSpec((tm, d), lambda i: (i, 0)), pl.BlockSpec((1, d), lambda i: (0, 0)),
                  pl.BlockSpec((tm, d), lambda i: (i, 0))],
        out_specs=[pl.BlockSpec((tm, d), lambda i: (i, 0)), pl.BlockSpec((1, d), lambda i: (0, 0)),
                   pl.BlockSpec((1, LANES), lambda i: (0, 0))],
        out_shape=[SDS((t, d), F32), SDS((1, d), F32), SDS((1, LANES), F32)],
        compiler_params=_params(),
    )(h, g, tgt)


GATE_ROWS = 256


def _split3(x):
    hi = x.astype(BF16)
    r1 = x - hi.astype(F32)
    mid = r1.astype(BF16)
    lo = (r1 - mid.astype(F32)).astype(BF16)
    return hi, mid, lo


def _tri_sum(x, upper):
    rows = x.shape[0]
    ri = lax.broadcasted_iota(jnp.int32, (rows, rows), 0)
    ci = lax.broadcasted_iota(jnp.int32, (rows, rows), 1)
    tri = jnp.where((ri <= ci) if upper else (ri >= ci), 1.0, 0.0).astype(BF16)
    hi, mid, lo = _split3(x)
    return _dot(tri, hi, NN) + _dot(tri, mid, NN) + _dot(tri, lo, NN)


def _log_sigmoid(x):
    return jnp.minimum(x, 0.0) - jnp.log1p(jnp.exp(-jnp.abs(x)))


def _fox_gate_fwd(projf, bpad):
    t = projf.shape[0]
    tb = GATE_ROWS

    def body(x_ref, b_ref, c_ref, carry):
        i = pl.program_id(0)

        @pl.when(i == 0)
        def _():
            carry[...] = jnp.zeros_like(carry)

        c_ref[...] = _tri_sum(_log_sigmoid(x_ref[...] + b_ref[...]), upper=False) + carry[...]
        carry[...] = c_ref[pl.ds(tb - 1, 1), :]

    return _pcall(
        body, name="fox_gate_fwd", grid=(t // tb,),
        in_specs=[pl.BlockSpec((tb, LANES), lambda i: (i, 0)), pl.BlockSpec((1, LANES), lambda i: (0, 0))],
        out_specs=pl.BlockSpec((tb, LANES), lambda i: (i, 0)),
        out_shape=SDS((t, LANES), F32), scratch_shapes=[pltpu.VMEM((1, LANES), F32)],
        compiler_params=_params(),
    )(projf, bpad)


def _fox_gate_bwd(projf, bpad, dc_query, dc_key):
    t = projf.shape[0]
    tb = GATE_ROWS
    nb = t // tb

    def body(x_ref, b_ref, dcq_ref, dck_ref, df_ref, db_ref, carry, buf):
        i = pl.program_id(0)

        @pl.when(i == 0)
        def _():
            carry[...] = jnp.zeros_like(carry)

        buf[...] = _tri_sum(dcq_ref[...] - dck_ref[...], upper=True) + carry[...]
        carry[...] = buf[pl.ds(0, 1), :]
        df = buf[...] * _sigmoid(-(x_ref[...] + b_ref[...]))
        df_ref[...] = df.astype(BF16)
        part = jnp.sum(df, axis=0, keepdims=True)

        @pl.when(i == 0)
        def _():
            db_ref[...] = part

        @pl.when(i > 0)
        def _():
            db_ref[...] += part

    rev = lambda i: (nb - 1 - i, 0)
    return _pcall(
        body, name="fox_gate_bwd", grid=(nb,),
        in_specs=[pl.BlockSpec((tb, LANES), rev), pl.BlockSpec((1, LANES), lambda i: (0, 0)),
                  pl.BlockSpec((tb, LANES), rev), pl.BlockSpec((tb, LANES), rev)],
        out_specs=[pl.BlockSpec((tb, LANES), rev), pl.BlockSpec((1, LANES), lambda i: (0, 0))],
        out_shape=[SDS((t, LANES), BF16), SDS((1, LANES), F32)],
        scratch_shapes=[pltpu.VMEM((1, LANES), F32), pltpu.VMEM((tb, LANES), F32)],
        compiler_params=_params(),
    )(projf, bpad, dc_query, dc_key)


FOX_TQ = 512
FOX_SCALE = FOX_HEAD_DIM ** -0.5


def _low_lanes(shape):
    return lax.broadcasted_iota(jnp.int32, shape, len(shape) - 1) < FOX_HEAD_DIM


def _fox_fwd(proj, cq, ck):
    t = proj.shape[0]
    tq = tk = min(FOX_TQ, t)
    nq = t // tq
    cb = D_MODEL // LANES

    def body(q_ref, k_ref, v_ref, z_ref, cq_ref, ck_ref, o_ref, g_ref, lse_ref, m_s, l_s, acc_s):
        i, j = pl.program_id(1), pl.program_id(2)

        @pl.when(j == 0)
        def _():
            m_s[...] = jnp.full_like(m_s, NEG_INF)
            l_s[...] = jnp.zeros_like(l_s)
            acc_s[...] = jnp.zeros_like(acc_s)

        low = _low_lanes((tq, LANES))

        @pl.when(j <= i)
        def _():
            q = q_ref[...] * FOX_SCALE
            k = k_ref[...]
            v = v_ref[...]
            causal = (j * tk + lax.broadcasted_iota(jnp.int32, (tq, tk), 1)
                      <= i * tq + lax.broadcasted_iota(jnp.int32, (tq, tk), 0))
            upd = []
            for hh in range(2):
                qm = jnp.where(low if hh == 0 else jnp.logical_not(low), q, jnp.zeros_like(q))
                s = _dot(qm, k, NT) + cq_ref[hh] - ck_ref[hh]
                s = jnp.where(causal, s, NEG_INF)
                m_prev = m_s[hh]
                m_new = jnp.maximum(m_prev, jnp.max(s, axis=-1, keepdims=True))
                alpha = jnp.exp(m_prev - m_new)
                p = jnp.exp(s - m_new)
                l_s[hh] = alpha * l_s[hh] + jnp.sum(p, axis=-1, keepdims=True)
                m_s[hh] = m_new
                upd.append((alpha, _dot(p.astype(BF16), v, NN)))
            acc = acc_s[...]
            acc_s[...] = jnp.where(low, acc * upd[0][0] + upd[0][1], acc * upd[1][0] + upd[1][1])

        @pl.when(j == i)
        def _():
            o = acc_s[...] / jnp.where(low, l_s[0], l_s[1])
            z = z_ref[...].astype(F32)
            o_ref[...] = o.astype(BF16)
            g_ref[...] = (o * (z * _sigmoid(z))).astype(BF16)
            for hh in range(2):
                lse_ref[hh] = m_s[hh] + jnp.log(l_s[hh])

    kv = lambda h, i, j: jnp.minimum(j, i)
    return _pcall(
        body, name="fox_attn_fwd", grid=(FOX_PAIRS, nq, nq),
        in_specs=[pl.BlockSpec((tq, LANES), lambda h, i, j: (i, h)),
                  pl.BlockSpec((tk, LANES), lambda h, i, j: (kv(h, i, j), cb + h)),
                  pl.BlockSpec((tk, LANES), lambda h, i, j: (kv(h, i, j), 2 * cb + h)),
                  pl.BlockSpec((tq, LANES), lambda h, i, j: (i, 3 * cb + h)),
                  pl.BlockSpec((2, tq, 1), lambda h, i, j: (h, i, 0)),
                  pl.BlockSpec((2, 1, tk), lambda h, i, j: (h, 0, kv(h, i, j)))],
        out_specs=[pl.BlockSpec((tq, LANES), lambda h, i, j: (i, h)),
                   pl.BlockSpec((tq, LANES), lambda h, i, j: (i, h)),
                   pl.BlockSpec((2, tq, 1), lambda h, i, j: (h, i, 0))],
        out_shape=[SDS((t, D_MODEL), BF16), SDS((t, D_MODEL), BF16), SDS((FOX_HEADS, t, 1), F32)],
        scratch_shapes=[pltpu.VMEM((2, tq, 1), F32), pltpu.VMEM((2, tq, 1), F32), pltpu.VMEM((tq, LANES), F32)],
        compiler_params=_params(dimension_semantics=("arbitrary", "arbitrary", "arbitrary")),
    )(proj, proj, proj, proj, cq, ck)


def _fox_bwd(proj, do, lse_row, delta_row, c_row, c_col):
    t = proj.shape[0]
    tq = tk = min(FOX_TQ, t)
    nq = t // tq
    cb = D_MODEL // LANES

    def body(q_ref, k_ref, v_ref, do_ref, lse_ref, dl_ref, cr_ref, cc_ref,
             dq_ref, dk_ref, dv_ref, dck_ref, dcq_ref, dq_acc, dk_acc, dv_acc, dck_acc, dcq_acc):
        j, i = pl.program_id(1), pl.program_id(2)
        low = _low_lanes((tq, LANES))

        @pl.when(i == j)
        def _():
            dk_acc[...] = jnp.zeros_like(dk_acc)
            dv_acc[...] = jnp.zeros_like(dv_acc)
            dck_acc[...] = jnp.zeros_like(dck_acc)

        @pl.when(i >= j)
        def _():
            k = k_ref[...]
            v = v_ref[...]
            dout = do_ref[...]
            qs = q_ref[...] * FOX_SCALE
            causal = (j * tk + lax.broadcasted_iota(jnp.int32, (tk, tq), 0)
                      <= i * tq + lax.broadcasted_iota(jnp.int32, (tk, tq), 1))
            parts = []
            for hh in range(2):
                sel = low if hh == 0 else jnp.logical_not(low)
                qm = jnp.where(sel, qs, jnp.zeros_like(qs))
                dom = jnp.where(sel, dout, jnp.zeros_like(dout))
                s = _dot(k, qm, NT) + cr_ref[hh] - cc_ref[hh]
                p = jnp.exp(jnp.where(causal, s, NEG_INF) - lse_ref[hh])
                ds = p * (_dot(v, dom, NT) - dl_ref[hh])
                dck_acc[hh] += jnp.sum(ds, axis=1, keepdims=True)
                over_keys = jnp.sum(ds, axis=0, keepdims=True)

                @pl.when(j == 0)
                def _():
                    dcq_acc[hh, i] = over_keys

                @pl.when(j > 0)
                def _():
                    dcq_acc[hh, i] += over_keys

                pb = p.astype(BF16)
                dsb = ds.astype(BF16)
                parts.append((_dot(pb, dout, NN), _dot(dsb, qs, NN), _dot(dsb, k, TN)))
            dv_acc[...] += jnp.where(low, parts[0][0], parts[1][0])
            dk_acc[...] += jnp.where(low, parts[0][1], parts[1][1])
            dq_blk = jnp.where(low, parts[0][2], parts[1][2]) * FOX_SCALE
            rows = pl.ds(pl.multiple_of(i * tq, tq), tq)

            @pl.when(j == 0)
            def _():
                dq_acc[rows, :] = dq_blk

            @pl.when(j > 0)
            def _():
                dq_acc[rows, :] += dq_blk

        @pl.when(i == nq - 1)
        def _():
            dk_ref[...] = dk_acc[...].astype(BF16)
            dv_ref[...] = dv_acc[...].astype(BF16)
            dck_ref[...] = dck_acc[...]

        @pl.when((i == nq - 1) & (j == nq - 1))
        def _():
            dq_ref[...] = dq_acc[...].astype(BF16)
            dcq_ref[...] = dcq_acc[...]

    qi = lambda h, j, i: jnp.maximum(i, j)
    qrow = pl.BlockSpec((2, 1, tq), lambda h, j, i: (h, 0, qi(h, j, i)))
    return _pcall(
        body, name="fox_attn_bwd", grid=(FOX_PAIRS, nq, nq),
        in_specs=[pl.BlockSpec((tq, LANES), lambda h, j, i: (qi(h, j, i), h)),
                  pl.BlockSpec((tk, LANES), lambda h, j, i: (j, cb + h)),
                  pl.BlockSpec((tk, LANES), lambda h, j, i: (j, 2 * cb + h)),
                  pl.BlockSpec((tq, LANES), lambda h, j, i: (qi(h, j, i), h)),
                  qrow, qrow, qrow,
                  pl.BlockSpec((2, tk, 1), lambda h, j, i: (h, j, 0))],
        out_specs=[pl.BlockSpec((t, LANES), lambda h, j, i: (0, h)),
                   pl.BlockSpec((tk, LANES), lambda h, j, i: (j, h)),
                   pl.BlockSpec((tk, LANES), lambda h, j, i: (j, h)),
                   pl.BlockSpec((2, tk, 1), lambda h, j, i: (h, j, 0)),
                   pl.BlockSpec((2, nq, 1, tq), lambda h, j, i: (h, 0, 0, 0))],
        out_shape=[SDS((t, D_MODEL), BF16), SDS((t, D_MODEL), BF16), SDS((t, D_MODEL), BF16),
                   SDS((FOX_HEADS, t, 1), F32), SDS((FOX_HEADS, nq, 1, tq), F32)],
        scratch_shapes=[pltpu.VMEM((t, LANES), F32), pltpu.VMEM((tk, LANES), F32), pltpu.VMEM((tk, LANES), F32),
                        pltpu.VMEM((2, tk, 1), F32), pltpu.VMEM((2, nq, 1, tq), F32)],
        compiler_params=_params(dimension_semantics=("arbitrary", "arbitrary", "arbitrary")),
    )(proj, proj, proj, do, lse_row, delta_row, c_row, c_col)


def _mm_gate_bwd(dh, w_out, z_src, z_col0, o, heads, name):
    t = dh.shape[0]
    tm = 256
    row = lambda i, j, k: (i, 0)
    zcb = z_col0 // D_MODEL

    def epilogue(acc, e_refs, o_refs, i):
        z = e_refs[0][...].astype(F32)
        ov = e_refs[1][...].astype(F32)
        sg = _sigmoid(z)
        dout = acc * (z * sg)
        o_refs[0][...] = dout.astype(BF16)
        o_refs[1][...] = (acc * ov * (sg * (1.0 + z * (1.0 - sg)))).astype(BF16)
        prod = dout * ov
        for cbk in range(D_MODEL // LANES):
            seg = prod[:, cbk * LANES:(cbk + 1) * LANES]
            tot = jnp.sum(seg, axis=-1, keepdims=True)
            if heads == D_MODEL // LANES:
                o_refs[2][cbk] = tot
            else:
                lo = jnp.sum(jnp.where(_low_lanes(seg.shape), seg, 0.0), axis=-1, keepdims=True)
                o_refs[2][2 * cbk] = lo
                o_refs[2][2 * cbk + 1] = tot - lo

    return _mm([dh], [w_out], "nt", name, tiles=(tm, D_MODEL, 512),
               extras=[(z_src, (tm, D_MODEL), lambda i, j, k: (i, zcb)), (o, (tm, D_MODEL), row)],
               outs=[(SDS((t, D_MODEL), BF16), (tm, D_MODEL), row), (SDS((t, D_MODEL), BF16), (tm, D_MODEL), row),
                     (SDS((heads, t, 1), F32), (heads, tm, 1), lambda i, j, k: (0, i, 0))],
               epilogue=epilogue)


def _ple_fwd(h, pin, w_up, w_gate, name):
    t = h.shape[0]
    tm, tn = 512, 512
    pd = pin.shape[1]

    def body(h_ref, p_ref, wu_ref, wg_ref, ht_ref, hn_ref, u_ref, a_ref):
        u = _dot(p_ref[...].astype(BF16), wu_ref[...], NN)
        a = _dot(h_ref[...].astype(BF16), wg_ref[...], NN)
        hn_ref[...] = ht_ref[...] + u * _sigmoid(a)
        u_ref[...] = u.astype(BF16)
        a_ref[...] = a.astype(BF16)

    tile = pl.BlockSpec((tm, tn), lambda i, j: (i, j))
    return _pcall(
        body, name=name, grid=(t // tm, D_MODEL // tn),
        in_specs=[pl.BlockSpec((tm, D_MODEL), lambda i, j: (i, 0)), pl.BlockSpec((tm, pd), lambda i, j: (i, 0)),
                  pl.BlockSpec((pd, tn), lambda i, j: (0, j)), pl.BlockSpec((D_MODEL, tn), lambda i, j: (0, j)), tile],
        out_specs=[tile, tile, tile],
        out_shape=[SDS((t, D_MODEL), F32), SDS((t, D_MODEL), BF16), SDS((t, D_MODEL), BF16)],
        compiler_params=_params(),
    )(h, pin, w_up, w_gate, h)


def _ple_bwd_elem(dh, u, a, name):
    t = dh.shape[0]
    tm = 512

    def body(dh_ref, u_ref, a_ref, du_ref, da_ref):
        g = dh_ref[...]
        s = _sigmoid(a_ref[...].astype(F32))
        du_ref[...] = (g * s).astype(BF16)
        da_ref[...] = (g * u_ref[...].astype(F32) * (s * (1.0 - s))).astype(BF16)

    blk = pl.BlockSpec((tm, D_MODEL), lambda i: (i, 0))
    return _pcall(
        body, name=name, grid=(t // tm,), in_specs=[blk, blk, blk], out_specs=[blk, blk],
        out_shape=[SDS((t, D_MODEL), BF16), SDS((t, D_MODEL), BF16)], compiler_params=_params(),
    )(dh, u, a)


DIL_SCALE = LANES ** -0.5


def _dil_masks():
    ii = lax.broadcasted_iota(jnp.int32, (DIL_BLOCK, DIL_BLOCK), 0)
    jj = lax.broadcasted_iota(jnp.int32, (DIL_BLOCK, DIL_BLOCK), 1)
    return ii, jj


def _dil_scores(q, kp, kc, slope, has_prev):
    ii, jj = _dil_masks()
    dist_p = (DIL_BLOCK + ii - jj).astype(F32)
    dist_c = (ii - jj).astype(F32)
    sp = _dot(q, kp, NT) * DIL_SCALE - slope * dist_p
    sc = _dot(q, kc, NT) * DIL_SCALE - slope * dist_c
    sp = jnp.where((jj >= ii) & has_prev, sp, NEG_INF)
    sc = jnp.where(jj <= ii, sc, NEG_INF)
    return sp, sc


def _dil_fwd(proj, slopes, grp, dil, name):
    t = proj.shape[0]
    rows = DIL_BLOCK * dil
    nsb = t // rows
    qc, kc_, vc_ = grp * DIL_HEADS, 3 * DIL_HEADS + grp * DIL_HEADS, 6 * DIL_HEADS + grp * DIL_HEADS

    def body(q_ref, kp_ref, kc_ref, vp_ref, vc_ref, sl_ref, o_ref, lse_ref, qf, kpf, kcf, vpf, vcf, of, lf):
        m = pl.program_id(1)
        for src, dst in ((q_ref, qf), (kp_ref, kpf), (kc_ref, kcf), (vp_ref, vpf), (vc_ref, vcf)):
            dst[...] = src[...].astype(F32)
        slope = sl_ref[0]
        has_prev = m > 0
        for r in range(dil):
            ph = pl.ds(r, DIL_BLOCK, stride=dil)
            q = qf[ph, :].astype(BF16)
            sp, sc = _dil_scores(q, kpf[ph, :].astype(BF16), kcf[ph, :].astype(BF16), slope, has_prev)
            mx = jnp.maximum(jnp.max(sp, axis=-1, keepdims=True), jnp.max(sc, axis=-1, keepdims=True))
            pp = jnp.exp(sp - mx)
            pc = jnp.exp(sc - mx)
            l = jnp.sum(pp, axis=-1, keepdims=True) + jnp.sum(pc, axis=-1, keepdims=True)
            o = _dot(pp.astype(BF16), vpf[ph, :].astype(BF16), NN) + _dot(pc.astype(BF16), vcf[ph, :].astype(BF16), NN)
            of[ph, :] = o / l
            lf[ph, :] = mx + jnp.log(l)
        o_ref[...] = of[...].astype(BF16)
        lse_ref[0] = lf[...]

    prev = lambda h, m: jnp.maximum(m - 1, 0)
    blk = lambda col, rowmap: pl.BlockSpec((rows, LANES), lambda h, m: (rowmap(h, m), col + h))
    cur = lambda h, m: m
    return _pcall(
        body, name=name, grid=(DIL_HEADS, nsb),
        in_specs=[blk(qc, cur), blk(kc_, prev), blk(kc_, cur), blk(vc_, prev), blk(vc_, cur),
                  pl.BlockSpec((1, 1, 1), lambda h, m: (h, 0, 0))],
        out_specs=[pl.BlockSpec((rows, LANES), lambda h, m: (m, h)), pl.BlockSpec((1, rows, 1), lambda h, m: (h, m, 0))],
        out_shape=[SDS((t, D_MODEL), BF16), SDS((DIL_HEADS, t, 1), F32)],
        scratch_shapes=[pltpu.VMEM((rows, LANES), F32)] * 6 + [pltpu.VMEM((rows, 1), F32)],
        compiler_params=_params(),
    )(proj, proj, proj, proj, proj, slopes)


def _dil_mix(outs, lses, proj, z_col0):
    t = proj.shape[0]
    tm = 512
    zcb = z_col0 // LANES
    ng = len(outs)

    def body(*refs):
        o_refs, l_refs, z_ref = refs[:ng], refs[ng:2 * ng], refs[2 * ng]
        om_ref, g_ref, lse_ref = refs[2 * ng + 1:]
        ls = [r[0] for r in l_refs]
        mx = functools.reduce(jnp.maximum, ls)
        es = [jnp.exp(l - mx) for l in ls]
        tot = functools.reduce(jnp.add, es)
        o = functools.reduce(jnp.add, [(e / tot) * r[...].astype(F32) for e, r in zip(es, o_refs)])
        z = z_ref[...].astype(F32)
        om_ref[...] = o.astype(BF16)
        g_ref[...] = (o * (z * _sigmoid(z))).astype(BF16)
        lse_ref[0] = mx + jnp.log(tot)

    tile = pl.BlockSpec((tm, LANES), lambda i, h: (i, h))
    col = pl.BlockSpec((1, tm, 1), lambda i, h: (h, i, 0))
    return _pcall(
        body, name="dil_mix", grid=(t // tm, DIL_HEADS),
        in_specs=[tile] * ng + [col] * ng + [pl.BlockSpec((tm, LANES), lambda i, h: (i, zcb + h))],
        out_specs=[tile, tile, col],
        out_shape=[SDS((t, D_MODEL), BF16), SDS((t, D_MODEL), BF16), SDS((DIL_HEADS, t, 1), F32)],
        compiler_params=_params(),
    )(*outs, *lses, proj)


def _dil_bwd(proj, do, lse, delta, slopes, grp, dil, name):
    t = proj.shape[0]
    rows = DIL_BLOCK * dil
    nsb = t // rows
    qc, kc_, vc_ = grp * DIL_HEADS, 3 * DIL_HEADS + grp * DIL_HEADS, 6 * DIL_HEADS + grp * DIL_HEADS

    def body(q_ref, qn_ref, kp_ref, kc_ref, vp_ref, vc_ref, do_ref, don_ref, l_ref, ln_ref, d_ref, dn_ref, sl_ref,
             dq_ref, dk_ref, dv_ref, qf, qnf, kpf, kcf, vpf, vcf, dof, donf, dqf, dkf, dvf):
        m = pl.program_id(1)
        for src, dst in ((q_ref, qf), (qn_ref, qnf), (kp_ref, kpf), (kc_ref, kcf), (vp_ref, vpf), (vc_ref, vcf),
                         (do_ref, dof), (don_ref, donf)):
            dst[...] = src[...].astype(F32)
        slope = sl_ref[0]
        has_prev = m > 0
        has_next = m < nsb - 1
        ii, jj = _dil_masks()
        for r in range(dil):
            ph = pl.ds(r, DIL_BLOCK, stride=dil)
            q, kp, kc, vp, vc, dout = (x[ph, :].astype(BF16) for x in (qf, kpf, kcf, vpf, vcf, dof))
            sp, sc = _dil_scores(q, kp, kc, slope, has_prev)
            lrow = l_ref[0, ph, :]
            drow = d_ref[0, ph, :]
            pp = jnp.exp(sp - lrow)
            pc = jnp.exp(sc - lrow)
            dsp = pp * (_dot(dout, vp, NT) - drow)
            dsc = pc * (_dot(dout, vc, NT) - drow)
            dspb, dscb = dsp.astype(BF16), dsc.astype(BF16)
            dqf[ph, :] = (_dot(dspb, kp, NN) + _dot(dscb, kc, NN)) * DIL_SCALE
            qn = qnf[ph, :].astype(BF16)
            don = donf[ph, :].astype(BF16)
            sn = _dot(qn, kc, NT) * DIL_SCALE - slope * (DIL_BLOCK + ii - jj).astype(F32)
            pn = jnp.exp(jnp.where((jj >= ii) & has_next, sn, NEG_INF) - ln_ref[0, ph, :])
            dsn = (pn * (_dot(don, vc, NT) - dn_ref[0, ph, :])).astype(BF16)
            dkf[ph, :] = (_dot(dscb, q, TN) + _dot(dsn, qn, TN)) * DIL_SCALE
            dvf[ph, :] = _dot(pc.astype(BF16), dout, TN) + _dot(pn.astype(BF16), don, TN)
        dq_ref[...] = dqf[...].astype(BF16)
        dk_ref[...] = dkf[...].astype(BF16)
        dv_ref[...] = dvf[...].astype(BF16)

    prev = lambda h, m: jnp.maximum(m - 1, 0)
    cur = lambda h, m: m
    nxt = lambda h, m: jnp.minimum(m + 1, nsb - 1)
    blk = lambda col, rowmap: pl.BlockSpec((rows, LANES), lambda h, m: (rowmap(h, m), col + h))
    colblk = lambda rowmap: pl.BlockSpec((1, rows, 1), lambda h, m: (h, rowmap(h, m), 0))
    out_blk = pl.BlockSpec((rows, LANES), lambda h, m: (m, h))
    return _pcall(
        body, name=name, grid=(DIL_HEADS, nsb),
        in_specs=[blk(qc, cur), blk(qc, nxt), blk(kc_, prev), blk(kc_, cur), blk(vc_, prev), blk(vc_, cur),
                  blk(0, cur), blk(0, nxt), colblk(cur), colblk(nxt), colblk(cur), colblk(nxt),
                  pl.BlockSpec((1, 1, 1), lambda h, m: (h, 0, 0))],
        out_specs=[out_blk, out_blk, out_blk],
        out_shape=[SDS((t, D_MODEL), BF16)] * 3,
        scratch_shapes=[pltpu.VMEM((rows, LANES), F32)] * 11,
        compiler_params=_params(),
    )(proj, proj, proj, proj, proj, proj, do, do, lse, lse, delta, delta, slopes)


def _mesh_pos():
    x, y, c = lax.axis_index("x"), lax.axis_index("y"), lax.axis_index("c")
    return x, y, c


def _peer(pos, k):
    x, y, c = pos
    px = 1 - x if k & 4 else x
    py = 1 - y if k & 2 else y
    pc = 1 - c if k & 1 else c
    return (px, py, pc), 4 * px + 2 * py + pc


def _exchange(arrays, name, gather):
    n = len(arrays)
    hbm = pl.BlockSpec(memory_space=pltpu.HBM)

    def body(*refs):
        ins, outs = refs[:n], refs[n:2 * n]
        send_sems, recv_sems, local_sems = refs[2 * n:]
        pos = _mesh_pos()
        me = 4 * pos[0] + 2 * pos[1] + pos[2]
        local, sends, recvs = [], [], []
        for w in range(n):
            own = ins[w] if gather else ins[w].at[me]
            cp = pltpu.make_async_copy(own, outs[w].at[me], local_sems.at[w])
            cp.start()
            local.append(cp)
            for k in range(1, N_DEV):
                peer, peer_idx = _peer(pos, k)
                sem = w * (N_DEV - 1) + k - 1
                src = ins[w] if gather else ins[w].at[peer_idx]
                cp = pltpu.make_async_remote_copy(src_ref=src, dst_ref=outs[w].at[me], send_sem=send_sems.at[sem],
                                                  recv_sem=recv_sems.at[sem], device_id=peer, device_id_type=MESH)
                cp.start()
                sends.append(cp)
                recvs.append(pltpu.make_async_remote_copy(
                    src_ref=src, dst_ref=outs[w].at[peer_idx], send_sem=send_sems.at[sem],
                    recv_sem=recv_sems.at[sem], device_id=peer, device_id_type=MESH))
        for cp in recvs:
            cp.wait_recv()
        for cp in sends:
            cp.wait_send()
        for cp in local:
            cp.wait()

    out_shape = [SDS((N_DEV,) + (a.shape if gather else a.shape[1:]), a.dtype) for a in arrays]
    return _pcall(
        body, name=name, in_specs=[hbm] * n, out_specs=[hbm] * n, out_shape=out_shape,
        scratch_shapes=[pltpu.SemaphoreType.DMA((n * (N_DEV - 1),)), pltpu.SemaphoreType.DMA((n * (N_DEV - 1),)),
                        pltpu.SemaphoreType.DMA((n,))],
        compiler_params=pltpu.CompilerParams(has_side_effects=True),
    )(*arrays)


def _adam_math(w, g, m, v):
    m = ADAM_B1 * m + (1.0 - ADAM_B1) * g
    v = ADAM_B2 * v + (1.0 - ADAM_B2) * (g * g)
    m_hat = m / (1.0 - ADAM_B1 ** ADAM_STEP)
    v_hat = v / (1.0 - ADAM_B2 ** ADAM_STEP)
    delta = -ADAM_LR * (m_hat / (jnp.sqrt(v_hat) + ADAM_EPS) + ADAM_WD * w)
    return delta, m, v


def _adamw(recv, w, m, v, name):
    r, c = w.shape
    tr = min(r, 128)

    def body(g_ref, w_ref, m_ref, v_ref, go_ref, d_ref, mo_ref, vo_ref):
        g = g_ref[0].astype(F32)
        for s in range(1, N_DEV):
            g = g + g_ref[s].astype(F32)
        delta, mn, vn = _adam_math(w_ref[...], g, m_ref[...], v_ref[...])
        go_ref[...] = g
        d_ref[...] = delta
        mo_ref[...] = mn
        vo_ref[...] = vn

    blk = pl.BlockSpec((tr, c), lambda i: (i, 0))
    return _pcall(
        body, name=name, grid=(r // tr,),
        in_specs=[pl.BlockSpec((N_DEV, tr, c), lambda i: (0, i, 0)), blk, blk, blk],
        out_specs=[blk] * 4, out_shape=[SDS((r, c), F32)] * 4, compiler_params=_params(),
    )(recv, w, m, v)


VEC_ROWS = 32


def _small_allreduce_adamw(vec, w, m, v):
    def body(vec_ref, w_ref, m_ref, v_ref, g_ref, d_ref, mo_ref, vo_ref, gath, send_sems, recv_sems):
        pos = _mesh_pos()
        me = 4 * pos[0] + 2 * pos[1] + pos[2]
        sends, recvs = [], []
        for k in range(1, N_DEV):
            peer, peer_idx = _peer(pos, k)
            cp = pltpu.make_async_remote_copy(src_ref=vec_ref, dst_ref=gath.at[me], send_sem=send_sems.at[k - 1],
                                              recv_sem=recv_sems.at[k - 1], device_id=peer, device_id_type=MESH)
            cp.start()
            sends.append(cp)
            recvs.append(pltpu.make_async_remote_copy(src_ref=vec_ref, dst_ref=gath.at[peer_idx],
                                                      send_sem=send_sems.at[k - 1], recv_sem=recv_sems.at[k - 1],
                                                      device_id=peer, device_id_type=MESH))
        gath[me] = vec_ref[...]
        for cp in recvs:
            cp.wait_recv()
        for cp in sends:
            cp.wait_send()
        tot = gath[0]
        for s in range(1, N_DEV):
            tot = tot + gath[s]
        rowi = lax.broadcasted_iota(jnp.int32, (8, LANES), 0)
        mine = jnp.sum(jnp.where(rowi == me, tot[16:24, :], 0.0), axis=0, keepdims=True)
        g = jnp.concatenate([tot[0:16, :], jnp.broadcast_to(mine, (8, LANES)), tot[24:32, :]], axis=0)
        delta, mn, vn = _adam_math(w_ref[...], g, m_ref[...], v_ref[...])
        g_ref[...] = g
        d_ref[...] = delta
        mo_ref[...] = mn
        vo_ref[...] = vn

    vm = pl.BlockSpec(memory_space=pltpu.VMEM)
    return _pcall(
        body, name="small_allreduce_adamw", in_specs=[vm] * 4, out_specs=[vm] * 4,
        out_shape=[SDS((VEC_ROWS, LANES), F32)] * 4,
        scratch_shapes=[pltpu.VMEM((N_DEV, VEC_ROWS, LANES), F32), pltpu.SemaphoreType.DMA((N_DEV - 1,)),
                        pltpu.SemaphoreType.DMA((N_DEV - 1,))],
        compiler_params=pltpu.CompilerParams(has_side_effects=True),
    )(vec, w, m, v)


def _cols_to_slabs(a):
    r, c8 = a.shape
    return a.reshape(r, N_DEV, c8 // N_DEV).transpose(1, 0, 2)


def _slabs_to_cols(a):
    n, r, c = a.shape
    return a.transpose(1, 0, 2).reshape(r, n * c)


def _rows8(vec):
    return vec.reshape(-1, LANES)


def _pad_rows(a, rows):
    return jnp.pad(a, ((0, rows - a.shape[0]), (0, LANES - a.shape[1])))


def kernel(x, p, fox_norm, fox_w_in, fox_b_f, fox_w_out, dil_norm, dil_w_in, dil_w_out, ple_w_up, ple_w_gate, final_norm, loss_target, m_fox_norm, m_fox_w_in, m_fox_b_f, m_fox_w_out, m_dil_norm, m_dil_w_in, m_dil_w_out, m_ple_w_up, m_ple_w_gate, m_final_norm, v_fox_norm, v_fox_w_in, v_fox_b_f, v_fox_w_out, v_dil_norm, v_dil_w_in, v_dil_w_out, v_ple_w_up, v_ple_w_gate, v_final_norm):
    t = x.shape[1]
    d = D_MODEL
    xs, tgt = x[0], loss_target[0]
    p0, p1 = p[0, 0], p[1, 0]
    fox_cols = fox_w_in.shape[2]
    ple_dim = ple_w_up.shape[1]

    shards = [fox_w_in[0].astype(BF16), fox_w_out[0].astype(BF16), dil_w_in[0].astype(BF16),
              dil_w_out[0].astype(BF16), ple_w_up.reshape(-1, LANES).astype(BF16),
              ple_w_gate.reshape(-1, d).astype(BF16), dil_norm]
    gw = _exchange(shards, "all_gather_weights", gather=True)
    w_fox_in = _slabs_to_cols(gw[0])
    w_fox_main = w_fox_in[:, :4 * d]
    w_fox_f = jnp.pad(w_fox_in[:, 4 * d:], ((0, 0), (0, LANES - FOX_HEADS)))
    w_fox_out = gw[1].reshape(d, d)
    w_dil_in = _slabs_to_cols(gw[2])
    w_dil_out = gw[3].reshape(d, d)
    w_up = gw[4].reshape(N_DEV, 2, ple_dim, LANES).transpose(1, 2, 0, 3).reshape(2, ple_dim, d)
    w_gate = gw[5].reshape(N_DEV, 2, d // N_DEV, d).transpose(1, 0, 2, 3).reshape(2, d, d)
    dil_norm_full = gw[6].reshape(1, d)
    b_pad = jnp.pad(fox_b_f, ((0, 0), (0, LANES - FOX_HEADS)))

    n0, r0 = _rms_fwd(xs, fox_norm, "rms_fox")
    proj0 = _mm([n0], [w_fox_main], "nn", "fox_in_proj", tiles=(512, 1024, 512))
    projf = _mm([n0], [w_fox_f], "nn", "fox_gate_proj", out_dtype=F32)
    c_all = _fox_gate_fwd(projf, b_pad)
    c_t = c_all[:, :FOX_HEADS].T
    cq, ck = c_t.reshape(FOX_HEADS, t, 1), c_t.reshape(FOX_HEADS, 1, t)
    o0, g0, lse0 = _fox_fwd(proj0, cq, ck)
    h1 = _mm_residual(g0, w_fox_out, "nn", xs, "fox_out_proj")
    h2, u0, a0 = _ple_fwd(h1, p0, w_up[0], w_gate[0], "ple0_fwd")

    n1, r1 = _rms_fwd(h2, dil_norm_full, "rms_dil")
    proj1 = _mm([n1], [w_dil_in], "nn", "dil_in_proj", tiles=(512, 1024, 512))
    n_heads = len(DIL_PATTERN) * DIL_HEADS
    slopes = 2.0 ** (-ALIBI_MAX_EXP * jnp.arange(1, n_heads + 1, dtype=F32) / n_heads)
    dil_o, dil_lse, dil_slopes = [], [], []
    for grp, (_, dil) in enumerate(DIL_PATTERN):
        sl = (slopes[grp * DIL_HEADS:(grp + 1) * DIL_HEADS] * dil).reshape(DIL_HEADS, 1, 1)
        og, lg = _dil_fwd(proj1, sl, grp, dil, f"dil_attn_fwd_{grp}")
        dil_o.append(og)
        dil_lse.append(lg)
        dil_slopes.append(sl)
    z1_col0 = 9 * d
    o1, g1, lse1 = _dil_mix(dil_o, dil_lse, proj1, z1_col0)
    h3 = _mm_residual(g1, w_dil_out, "nn", h2, "dil_out_proj")
    h4, u1, a1 = _ple_fwd(h3, p1, w_up[1], w_gate[1], "ple1_fwd")

    dh4, d_final_norm, loss_part = _final_bwd(h4, final_norm.reshape(1, d), tgt)

    du1, da1 = _ple_bwd_elem(dh4, u1, a1, "ple1_bwd_elem")
    dw_up1 = _mm([p1], [du1], "tn", "ple1_dw_up")
    dw_gate1 = _mm([h3], [da1], "tn", "ple1_dw_gate")
    dh3 = _mm_residual(da1, w_gate[1], "nt", dh4, "ple1_dh")

    dw_dil_out = _mm([g1], [dh3], "tn", "dil_dw_out")
    do1, dz1, delta1 = _mm_gate_bwd(dh3, w_dil_out, proj1, z1_col0, o1, DIL_HEADS, "dil_dgate")
    dqs, dks, dvs = [], [], []
    for grp, (_, dil) in enumerate(DIL_PATTERN):
        dq, dk, dv = _dil_bwd(proj1, do1, lse1, delta1, dil_slopes[grp], grp, dil, f"dil_attn_bwd_{grp}")
        dqs.append(dq)
        dks.append(dk)
        dvs.append(dv)
    dproj1 = dqs + dks + dvs + [dz1]
    dw_dil_in = _mm([n1], dproj1, "tn", "dil_dw_in", tiles=(512, 1024, 512))
    dh2, d_dil_norm = _mm_in_bwd(dproj1, w_dil_in, h2, dil_norm_full, r1, dh3, "dil_dx")

    du0, da0 = _ple_bwd_elem(dh2, u0, a0, "ple0_bwd_elem")
    dw_up0 = _mm([p0], [du0], "tn", "ple0_dw_up")
    dw_gate0 = _mm([h1], [da0], "tn", "ple0_dw_gate")
    dh1 = _mm_residual(da0, w_gate[0], "nt", dh2, "ple0_dh")

    dw_fox_out = _mm([g0], [dh1], "tn", "fox_dw_out")
    do0, dz0, delta0 = _mm_gate_bwd(dh1, w_fox_out, proj0, 3 * d, o0, FOX_HEADS, "fox_dgate")
    as_row = lambda a: a.reshape(FOX_HEADS, 1, t)
    dq0, dk0, dv0, dck, dcq = _fox_bwd(proj0, do0, as_row(lse0), as_row(delta0), ck, cq)
    head_cols = lambda a: jnp.pad(a.reshape(FOX_HEADS, t).T, ((0, 0), (0, LANES - FOX_HEADS)))
    df, d_b_f = _fox_gate_bwd(projf, b_pad, head_cols(dcq), head_cols(dck))
    dproj0 = [dq0, dk0, dv0, dz0]
    dw_fox_main = _mm([n0], dproj0, "tn", "fox_dw_in", tiles=(512, 1024, 512))
    dw_fox_f = _mm([n0], [df], "tn", "fox_dw_gate")
    dn0_f = _mm([df], [w_fox_f], "nt", "fox_dx_gate", out_dtype=F32)
    grad_x, d_fox_norm = _mm_in_bwd(dproj0, w_fox_main, xs, fox_norm, r0, dh1, "fox_dx", more=dn0_f)

    dw_fox_in = jnp.concatenate([dw_fox_main, dw_fox_f[:, :FOX_HEADS]], axis=1)
    slabs = [_cols_to_slabs(dw_fox_in), dw_fox_out.reshape(N_DEV, d // N_DEV, d), _cols_to_slabs(dw_dil_in),
             dw_dil_out.reshape(N_DEV, d // N_DEV, d),
             jnp.stack([dw_up0, dw_up1]).reshape(2, ple_dim, N_DEV, LANES).transpose(2, 0, 1, 3).reshape(N_DEV, -1, LANES),
             jnp.stack([dw_gate0, dw_gate1]).reshape(2, N_DEV, d // N_DEV, d).transpose(1, 0, 2, 3).reshape(N_DEV, -1, d)]
    recv = _exchange(slabs, "reduce_scatter_grads", gather=False)
    big = [(fox_w_in, m_fox_w_in, v_fox_w_in), (fox_w_out, m_fox_w_out, v_fox_w_out),
           (dil_w_in, m_dil_w_in, v_dil_w_in), (dil_w_out, m_dil_w_out, v_dil_w_out),
           (ple_w_up, m_ple_w_up, v_ple_w_up), (ple_w_gate, m_ple_w_gate, v_ple_w_gate)]
    names = ["fox_w_in", "fox_w_out", "dil_w_in", "dil_w_out", "ple_w_up", "ple_w_gate"]
    upd = {}
    for rv, (w, m, v), nm in zip(recv, big, names):
        shp2 = rv.shape[1:]
        res = _adamw(rv, w.reshape(shp2), m.reshape(shp2), v.reshape(shp2), "adamw_" + nm)
        upd[nm] = [a.reshape(w.shape) for a in res]

    loss_row = jnp.where(jnp.arange(LANES) == 0, loss_part, 0.0)
    vec = jnp.concatenate([_rows8(d_fox_norm), _rows8(d_final_norm), _rows8(d_dil_norm), d_b_f, loss_row,
                           jnp.zeros((VEC_ROWS - 26, LANES), F32)], axis=0)

    def small_pack(a_fox_norm, a_final_norm, a_dil_norm, a_b_f):
        return jnp.concatenate([_rows8(a_fox_norm), _rows8(a_final_norm), _pad_rows(a_dil_norm, 8),
                                _pad_rows(a_b_f, 8)], axis=0)

    sg, sd, sm, sv = _small_allreduce_adamw(
        vec, small_pack(fox_norm, final_norm, dil_norm, fox_b_f),
        small_pack(m_fox_norm, m_final_norm, m_dil_norm, m_fox_b_f),
        small_pack(v_fox_norm, v_final_norm, v_dil_norm, v_fox_b_f))

    def small_unpack(a):
        return {"fox_norm": a[0:8].reshape(1, d), "final_norm": a[8:16].reshape(d), "dil_norm": a[16:17],
                "fox_b_f": a[24:25, :FOX_HEADS]}

    loss = sg[25, 0]
    order = ["fox_norm", "fox_w_in", "fox_b_f", "fox_w_out", "dil_norm", "dil_w_in", "dil_w_out", "ple_w_up",
             "ple_w_gate", "final_norm"]
    out = [loss, grad_x[None]]
    for idx, small in enumerate((sg, sd, sm, sv)):
        sp = small_unpack(small)
        out += [sp[nm] if nm in sp else upd[nm][idx] for nm in order]
    return tuple(out)
```

```python
import functools

import jax
import jax.numpy as jnp
from jax import lax
from jax.experimental import pallas as pl
from jax.experimental.pallas import tpu as pltpu

F32 = jnp.float32
BF16 = jnp.bfloat16
SDS = jax.ShapeDtypeStruct

D_MODEL = 1024
N_DEV = 8
LANES = 128
FOX_HEADS = 16
FOX_HEAD_DIM = 64
FOX_PAIRS = FOX_HEADS // 2
DIL_HEADS = 8
DIL_BLOCK = 128
DIL_PATTERN = ((128, 1), (512, 4), (2048, 16))
ALIBI_MAX_EXP = 8.0
RMS_EPS = 1e-6
ADAM_LR, ADAM_B1, ADAM_B2, ADAM_EPS, ADAM_WD, ADAM_STEP = 0.001, 0.9, 0.999, 1e-08, 0.01, 10
VMEM_LIMIT = 48 * 1024 * 1024
NEG_INF = float("-inf")

NN = (((1,), (0,)), ((), ()))
NT = (((1,), (1,)), ((), ()))
TN = (((0,), (0,)), ((), ()))
MESH = pl.DeviceIdType.MESH


def _pcall(body, **kw):
    return pl.pallas_call(body, **kw)


def _params(**kw):
    return pltpu.CompilerParams(vmem_limit_bytes=VMEM_LIMIT, **kw)


def _dot(a, b, dims):
    return lax.dot_general(a, b, dims, preferred_element_type=F32)


def _sigmoid(x):
    return 1.0 / (1.0 + jnp.exp(-x))


def _mm(a_parts, b, mode, name, tiles=(512, 1024, 1024), extras=(), outs=None, epilogue=None, out_dtype=BF16):
    na = len(a_parts)
    if mode == "tn":
        k_part, m = a_parts[0].shape
        n = b.shape[1]
    else:
        m, k_part = a_parts[0].shape
        n = b.shape[1] if mode == "nn" else b.shape[0]
    tm, tn, tk = min(tiles[0], m), min(tiles[1], n), min(tiles[2], k_part)
    kb = k_part // tk
    nk = na * kb
    grid = (m // tm, n // tn, nk)

    in_specs = []
    for s in range(na):
        if mode == "tn":
            in_specs.append(pl.BlockSpec((tk, tm), lambda i, j, k: (k, i)))
        else:
            in_specs.append(pl.BlockSpec((tm, tk), lambda i, j, k, s=s: (i, jnp.clip(k - s * kb, 0, kb - 1))))
    if mode == "nt":
        in_specs.append(pl.BlockSpec((tn, tk), lambda i, j, k: (j, k)))
    else:
        in_specs.append(pl.BlockSpec((tk, tn), lambda i, j, k: (k, j)))
    for _, blk, imap in extras:
        in_specs.append(pl.BlockSpec(blk, imap))
    if outs is None:
        outs = [(SDS((m, n), out_dtype), (tm, tn), lambda i, j, k: (i, j))]
    out_specs = [pl.BlockSpec(blk, imap) for _, blk, imap in outs]
    ne, no = len(extras), len(outs)
    dims = {"nn": NN, "nt": NT, "tn": TN}[mode]

    def finish(res, e_refs, o_refs, i):
        if epilogue is None:
            o_refs[0][...] = res.astype(o_refs[0].dtype)
        else:
            epilogue(res, e_refs, o_refs, i)

    def body(*refs):
        a_refs = refs[:na]
        b_ref = refs[na]
        e_refs = refs[na + 1:na + 1 + ne]
        o_refs = refs[na + 1 + ne:na + 1 + ne + no]
        i, k = pl.program_id(0), pl.program_id(2)
        if nk == 1:
            finish(_dot(a_refs[0][...].astype(BF16), b_ref[...].astype(BF16), dims), e_refs, o_refs, i)
            return
        acc = refs[-1]

        @pl.when(k == 0)
        def _():
            acc[...] = jnp.zeros_like(acc)

        def step(a_ref):
            acc[...] += _dot(a_ref[...].astype(BF16), b_ref[...].astype(BF16), dims)

        for s in range(na):
            if na == 1:
                step(a_refs[0])
            else:
                pl.when((k >= s * kb) & (k < (s + 1) * kb))(functools.partial(step, a_refs[s]))

        @pl.when(k == nk - 1)
        def _():
            finish(acc[...], e_refs, o_refs, i)

    res = _pcall(
        body, name=name, grid=grid, in_specs=in_specs, out_specs=out_specs,
        out_shape=[o[0] for o in outs], scratch_shapes=[] if nk == 1 else [pltpu.VMEM((tm, tn), F32)],
        compiler_params=_params(dimension_semantics=("arbitrary", "arbitrary", "arbitrary")),
    )(*a_parts, b, *[e[0] for e in extras])
    return res[0] if len(res) == 1 else res


IN_PROJ_TILES = (1024, 1024, 1024)
DW_TILES = (1024, 1024, 512)


def _dw(x, dy, name):
    return _mm([x], dy, "tn", name, tiles=DW_TILES)


def _add_extra_epilogue(acc, e_refs, o_refs, i):
    o_refs[0][...] = acc + e_refs[0][...]


def _mm_residual(a, b, mode, res, name):
    m = a.shape[0]
    n = b.shape[1] if mode == "nn" else b.shape[0]
    tm, tn = 512, 1024
    return _mm([a], b, mode, name, tiles=(tm, tn, 1024),
               extras=[(res, (tm, tn), lambda i, j, k: (i, j))],
               outs=[(SDS((m, n), F32), (tm, tn), lambda i, j, k: (i, j))],
               epilogue=_add_extra_epilogue)


def _rms_fwd(h, g, name):
    t, d = h.shape
    tm = 512

    def body(h_ref, g_ref, n_ref, r_ref):
        x = h_ref[...]
        r = lax.rsqrt(jnp.mean(x * x, axis=-1, keepdims=True) + RMS_EPS)
        n_ref[...] = ((x * r) * g_ref[...]).astype(BF16)
        r_ref[...] = r

    return _pcall(
        body, name=name, grid=(t // tm,),
        in_specs=[pl.BlockSpec((tm, d), lambda i: (i, 0)), pl.BlockSpec((1, d), lambda i: (0, 0))],
        out_specs=[pl.BlockSpec((tm, d), lambda i: (i, 0)), pl.BlockSpec((tm, 1), lambda i: (i, 0))],
        out_shape=[SDS((t, d), BF16), SDS((t, 1), F32)],
        compiler_params=_params(),
    )(h, g)


def _rms_bwd_rows(dn, x, g, r):
    xhat = x * r
    dxhat = dn * g
    dx = r * (dxhat - xhat * jnp.mean(dxhat * xhat, axis=-1, keepdims=True))
    dg = jnp.sum(dn * xhat, axis=0, keepdims=True)
    return dx, dg


def _mm_in_bwd(d_parts, w, h, g, r, dres, name, more=None):
    t = h.shape[0]
    tm = 512
    tk = 1024 if len(d_parts) <= 4 else 512
    row = lambda i, j, k: (i, 0)
    extras = [(h, (tm, D_MODEL), row), (g, (1, D_MODEL), lambda i, j, k: (0, 0)), (r, (tm, 1), row),
              (dres, (tm, D_MODEL), row)]
    if more is not None:
        extras.append((more, (tm, D_MODEL), row))

    def epilogue(acc, e_refs, o_refs, i):
        dn = acc if more is None else acc + e_refs[4][...]
        dx, dg = _rms_bwd_rows(dn, e_refs[0][...], e_refs[1][...], e_refs[2][...])
        o_refs[0][...] = e_refs[3][...] + dx

        @pl.when(i == 0)
        def _():
            o_refs[1][...] = dg

        @pl.when(i > 0)
        def _():
            o_refs[1][...] += dg

    return _mm(d_parts, w, "nt", name, tiles=(tm, D_MODEL, tk), extras=extras,
               outs=[(SDS((t, D_MODEL), F32), (tm, D_MODEL), row),
                     (SDS((1, D_MODEL), F32), (1, D_MODEL), lambda i, j, k: (0, 0))],
               epilogue=epilogue)


def _final_bwd(h, g, tgt):
    t, d = h.shape
    tm = 256

    def body(h_ref, g_ref, t_ref, dh_ref, dg_ref, loss_ref):
        i = pl.program_id(0)
        x = h_ref[...]
        gg = g_ref[...]
        r = lax.rsqrt(jnp.mean(x * x, axis=-1, keepdims=True) + RMS_EPS)
        err = (x * r) * gg - t_ref[...]
        part = 0.5 * jnp.sum(jnp.mean(err * err, axis=-1, keepdims=True), axis=0, keepdims=True)
        dx, dg = _rms_bwd_rows(err * (1.0 / d), x, gg, r)
        dh_ref[...] = dx

        @pl.when(i == 0)
        def _():
            dg_ref[...] = dg
            loss_ref[...] = jnp.broadcast_to(part, loss_ref.shape)

        @pl.when(i > 0)
        def _():
            dg_ref[...] += dg
            loss_ref[...] += jnp.broadcast_to(part, loss_ref.shape)

    return _pcall(
        body, name="final_norm_loss", grid=(t // tm,),
        in_specs=[pl.BlockSpec((tm, d), lambda i: (i, 0)), pl.BlockSpec((1, d), lambda i: (0, 0)),
                  pl.BlockSpec((tm, d), lambda i: (i, 0))],
        out_specs=[pl.BlockSpec((tm, d), lambda i: (i, 0)), pl.BlockSpec((1, d), lambda i: (0, 0)),
                   pl.BlockSpec((1, LANES), lambda i: (0, 0))],
        out_shape=[SDS((t, d), F32), SDS((1, d), F32), SDS((1, LANES), F32)],
        compiler_params=_params(),
    )(h, g, tgt)


GATE_ROWS = 256


def _split3(x):
    hi = x.astype(BF16)
    r1 = x - hi.astype(F32)
    mid = r1.astype(BF16)
    lo = (r1 - mid.astype(F32)).astype(BF16)
    return hi, mid, lo


def _tri_sum(x, upper):
    rows = x.shape[0]
    ri = lax.broadcasted_iota(jnp.int32, (rows, rows), 0)
    ci = lax.broadcasted_iota(jnp.int32, (rows, rows), 1)
    tri = jnp.where((ri <= ci) if upper else (ri >= ci), 1.0, 0.0).astype(BF16)
    hi, mid, lo = _split3(x)
    return _dot(tri, hi, NN) + _dot(tri, mid, NN) + _dot(tri, lo, NN)


def _log_sigmoid(x):
    return jnp.minimum(x, 0.0) - jnp.log1p(jnp.exp(-jnp.abs(x)))


def _fox_gate_fwd(projf, bpad):
    t = projf.shape[0]
    tb = GATE_ROWS

    def body(x_ref, b_ref, c_ref, carry):
        i = pl.program_id(0)

        @pl.when(i == 0)
        def _():
            carry[...] = jnp.zeros_like(carry)

        c_ref[...] = _tri_sum(_log_sigmoid(x_ref[...] + b_ref[...]), upper=False) + carry[...]
        carry[...] = c_ref[pl.ds(tb - 1, 1), :]

    return _pcall(
        body, name="fox_gate_fwd", grid=(t // tb,),
        in_specs=[pl.BlockSpec((tb, LANES), lambda i: (i, 0)), pl.BlockSpec((1, LANES), lambda i: (0, 0))],
        out_specs=pl.BlockSpec((tb, LANES), lambda i: (i, 0)),
        out_shape=SDS((t, LANES), F32), scratch_shapes=[pltpu.VMEM((1, LANES), F32)],
        compiler_params=_params(),
    )(projf, bpad)


def _fox_gate_bwd(projf, bpad, dc_query, dc_key):
    t = projf.shape[0]
    tb = GATE_ROWS
    nb = t // tb

    def body(x_ref, b_ref, dcq_ref, dck_ref, df_ref, db_ref, carry, buf):
        i = pl.program_id(0)

        @pl.when(i == 0)
        def _():
            carry[...] = jnp.zeros_like(carry)

        buf[...] = _tri_sum(dcq_ref[...] - dck_ref[...], upper=True) + carry[...]
        carry[...] = buf[pl.ds(0, 1), :]
        df = buf[...] * _sigmoid(-(x_ref[...] + b_ref[...]))
        df_ref[...] = df.astype(BF16)
        part = jnp.sum(df, axis=0, keepdims=True)

        @pl.when(i == 0)
        def _():
            db_ref[...] = part

        @pl.when(i > 0)
        def _():
            db_ref[...] += part

    rev = lambda i: (nb - 1 - i, 0)
    return _pcall(
        body, name="fox_gate_bwd", grid=(nb,),
        in_specs=[pl.BlockSpec((tb, LANES), rev), pl.BlockSpec((1, LANES), lambda i: (0, 0)),
                  pl.BlockSpec((tb, LANES), rev), pl.BlockSpec((tb, LANES), rev)],
        out_specs=[pl.BlockSpec((tb, LANES), rev), pl.BlockSpec((1, LANES), lambda i: (0, 0))],
        out_shape=[SDS((t, LANES), BF16), SDS((1, LANES), F32)],
        scratch_shapes=[pltpu.VMEM((1, LANES), F32), pltpu.VMEM((tb, LANES), F32)],
        compiler_params=_params(),
    )(projf, bpad, dc_query, dc_key)


FOX_TQ = 512
FOX_SCALE = FOX_HEAD_DIM ** -0.5


def _low_lanes(shape):
    return lax.broadcasted_iota(jnp.int32, shape, len(shape) - 1) < FOX_HEAD_DIM


def _fox_fwd(proj, cq, ck):
    t = proj.shape[0]
    tq = tk = min(FOX_TQ, t)
    nq = t // tq
    cb = D_MODEL // LANES

    def body(q_ref, k_ref, v_ref, z_ref, cq_ref, ck_ref, o_ref, g_ref, lse_ref, m_s, l_s, acc_s):
        i, j = pl.program_id(1), pl.program_id(2)

        @pl.when(j == 0)
        def _():
            m_s[...] = jnp.full_like(m_s, NEG_INF)
            l_s[...] = jnp.zeros_like(l_s)
            acc_s[...] = jnp.zeros_like(acc_s)

        low = _low_lanes((tq, LANES))

        @pl.when(j <= i)
        def _():
            q = q_ref[...] * FOX_SCALE
            k = k_ref[...]
            v = v_ref[...]
            causal = (j * tk + lax.broadcasted_iota(jnp.int32, (tq, tk), 1)
                      <= i * tq + lax.broadcasted_iota(jnp.int32, (tq, tk), 0))
            upd = []
            for hh in range(2):
                qm = jnp.where(low if hh == 0 else jnp.logical_not(low), q, jnp.zeros_like(q))
                s = _dot(qm, k, NT) + cq_ref[hh] - ck_ref[hh]
                s = jnp.where(causal, s, NEG_INF)
                m_prev = m_s[hh]
                m_new = jnp.maximum(m_prev, jnp.max(s, axis=-1, keepdims=True))
                alpha = jnp.exp(m_prev - m_new)
                p = jnp.exp(s - m_new)
                l_s[hh] = alpha * l_s[hh] + jnp.sum(p, axis=-1, keepdims=True)
                m_s[hh] = m_new
                upd.append((alpha, _dot(p.astype(BF16), v, NN)))
            acc = acc_s[...]
            acc_s[...] = jnp.where(low, acc * upd[0][0] + upd[0][1], acc * upd[1][0] + upd[1][1])

        @pl.when(j == i)
        def _():
            o = acc_s[...] / jnp.where(low, l_s[0], l_s[1])
            z = z_ref[...].astype(F32)
            o_ref[...] = o.astype(BF16)
            g_ref[...] = (o * (z * _sigmoid(z))).astype(BF16)
            for hh in range(2):
                lse_ref[hh] = m_s[hh] + jnp.log(l_s[hh])

    kv = lambda h, i, j: jnp.minimum(j, i)
    return _pcall(
        body, name="fox_attn_fwd", grid=(FOX_PAIRS, nq, nq),
        in_specs=[pl.BlockSpec((tq, LANES), lambda h, i, j: (i, h)),
                  pl.BlockSpec((tk, LANES), lambda h, i, j: (kv(h, i, j), cb + h)),
                  pl.BlockSpec((tk, LANES), lambda h, i, j: (kv(h, i, j), 2 * cb + h)),
                  pl.BlockSpec((tq, LANES), lambda h, i, j: (i, 3 * cb + h)),
                  pl.BlockSpec((2, tq, 1), lambda h, i, j: (h, i, 0)),
                  pl.BlockSpec((2, 1, tk), lambda h, i, j: (h, 0, kv(h, i, j)))],
        out_specs=[pl.BlockSpec((tq, LANES), lambda h, i, j: (i, h)),
                   pl.BlockSpec((tq, LANES), lambda h, i, j: (i, h)),
                   pl.BlockSpec((2, tq, 1), lambda h, i, j: (h, i, 0))],
        out_shape=[SDS((t, D_MODEL), BF16), SDS((t, D_MODEL), BF16), SDS((FOX_HEADS, t, 1), F32)],
        scratch_shapes=[pltpu.VMEM((2, tq, 1), F32), pltpu.VMEM((2, tq, 1), F32), pltpu.VMEM((tq, LANES), F32)],
        compiler_params=_params(dimension_semantics=("arbitrary", "arbitrary", "arbitrary")),
    )(proj, proj, proj, proj, cq, ck)


def _fox_bwd(proj, do, lse_row, delta_row, c_row, c_col):
    t = proj.shape[0]
    tq = tk = min(FOX_TQ, t)
    nq = t // tq
    cb = D_MODEL // LANES

    def body(q_ref, k_ref, v_ref, do_ref, lse_ref, dl_ref, cr_ref, cc_ref,
             dq_ref, dk_ref, dv_ref, dck_ref, dcq_ref, dq_acc, dk_acc, dv_acc, dck_acc, dcq_acc):
        j, i = pl.program_id(1), pl.program_id(2)
        low = _low_lanes((tq, LANES))

        @pl.when(i == j)
        def _():
            dk_acc[...] = jnp.zeros_like(dk_acc)
            dv_acc[...] = jnp.zeros_like(dv_acc)
            dck_acc[...] = jnp.zeros_like(dck_acc)

        @pl.when(i >= j)
        def _():
            k = k_ref[...]
            v = v_ref[...]
            dout = do_ref[...]
            qs = q_ref[...] * FOX_SCALE
            causal = (j * tk + lax.broadcasted_iota(jnp.int32, (tk, tq), 0)
                      <= i * tq + lax.broadcasted_iota(jnp.int32, (tk, tq), 1))
            parts = []
            for hh in range(2):
                sel = low if hh == 0 else jnp.logical_not(low)
                qm = jnp.where(sel, qs, jnp.zeros_like(qs))
                dom = jnp.where(sel, dout, jnp.zeros_like(dout))
                s = _dot(k, qm, NT) + cr_ref[hh] - cc_ref[hh]
                p = jnp.exp(jnp.where(causal, s, NEG_INF) - lse_ref[hh])
                ds = p * (_dot(v, dom, NT) - dl_ref[hh])
                dck_acc[hh] += jnp.sum(ds, axis=1, keepdims=True)
                over_keys = jnp.sum(ds, axis=0, keepdims=True)

                @pl.when(j == 0)
                def _():
                    dcq_acc[hh, i] = over_keys

                @pl.when(j > 0)
                def _():
                    dcq_acc[hh, i] += over_keys

                pb = p.astype(BF16)
                dsb = ds.astype(BF16)
                parts.append((_dot(pb, dout, NN), _dot(dsb, qs, NN), _dot(dsb, k, TN)))
            dv_acc[...] += jnp.where(low, parts[0][0], parts[1][0])
            dk_acc[...] += jnp.where(low, parts[0][1], parts[1][1])
            dq_blk = jnp.where(low, parts[0][2], parts[1][2]) * FOX_SCALE
            rows = pl.ds(pl.multiple_of(i * tq, tq), tq)

            @pl.when(j == 0)
            def _():
                dq_acc[rows, :] = dq_blk

            @pl.when(j > 0)
            def _():
                dq_acc[rows, :] += dq_blk

        @pl.when(i == nq - 1)
        def _():
            dk_ref[...] = dk_acc[...].astype(BF16)
            dv_ref[...] = dv_acc[...].astype(BF16)
            dck_ref[...] = dck_acc[...]

        @pl.when((i == nq - 1) & (j == nq - 1))
        def _():
            dq_ref[...] = dq_acc[...].astype(BF16)
            dcq_ref[...] = dcq_acc[...]

    qi = lambda h, j, i: jnp.maximum(i, j)
    qrow = pl.BlockSpec((2, 1, tq), lambda h, j, i: (h, 0, qi(h, j, i)))
    return _pcall(
        body, name="fox_attn_bwd", grid=(FOX_PAIRS, nq, nq),
        in_specs=[pl.BlockSpec((tq, LANES), lambda h, j, i: (qi(h, j, i), h)),
                  pl.BlockSpec((tk, LANES), lambda h, j, i: (j, cb + h)),
                  pl.BlockSpec((tk, LANES), lambda h, j, i: (j, 2 * cb + h)),
                  pl.BlockSpec((tq, LANES), lambda h, j, i: (qi(h, j, i), h)),
                  qrow, qrow, qrow,
                  pl.BlockSpec((2, tk, 1), lambda h, j, i: (h, j, 0))],
        out_specs=[pl.BlockSpec((t, LANES), lambda h, j, i: (0, h)),
                   pl.BlockSpec((tk, LANES), lambda h, j, i: (j, h)),
                   pl.BlockSpec((tk, LANES), lambda h, j, i: (j, h)),
                   pl.BlockSpec((2, tk, 1), lambda h, j, i: (h, j, 0)),
                   pl.BlockSpec((2, nq, 1, tq), lambda h, j, i: (h, 0, 0, 0))],
        out_shape=[SDS((t, D_MODEL), BF16), SDS((t, D_MODEL), BF16), SDS((t, D_MODEL), BF16),
                   SDS((FOX_HEADS, t, 1), F32), SDS((FOX_HEADS, nq, 1, tq), F32)],
        scratch_shapes=[pltpu.VMEM((t, LANES), F32), pltpu.VMEM((tk, LANES), F32), pltpu.VMEM((tk, LANES), F32),
                        pltpu.VMEM((2, tk, 1), F32), pltpu.VMEM((2, nq, 1, tq), F32)],
        compiler_params=_params(dimension_semantics=("arbitrary", "arbitrary", "arbitrary")),
    )(proj, proj, proj, do, lse_row, delta_row, c_row, c_col)


def _mm_gate_bwd(dh, w_out, z_src, z_col0, o, heads, name):
    t = dh.shape[0]
    tm = 512
    row = lambda i, j, k: (i, 0)
    zcb = z_col0 // D_MODEL

    def epilogue(acc, e_refs, o_refs, i):
        z = e_refs[0][...].astype(F32)
        ov = e_refs[1][...].astype(F32)
        sg = _sigmoid(z)
        dout = acc * (z * sg)
        o_refs[0][...] = dout.astype(BF16)
        o_refs[1][...] = (acc * ov * (sg * (1.0 + z * (1.0 - sg)))).astype(BF16)
        prod = dout * ov
        for cbk in range(D_MODEL // LANES):
            seg = prod[:, cbk * LANES:(cbk + 1) * LANES]
            tot = jnp.sum(seg, axis=-1, keepdims=True)
            if heads == D_MODEL // LANES:
                o_refs[2][cbk] = tot
            else:
                lo = jnp.sum(jnp.where(_low_lanes(seg.shape), seg, 0.0), axis=-1, keepdims=True)
                o_refs[2][2 * cbk] = lo
                o_refs[2][2 * cbk + 1] = tot - lo

    return _mm([dh], w_out, "nt", name, tiles=(tm, D_MODEL, D_MODEL),
               extras=[(z_src, (tm, D_MODEL), lambda i, j, k: (i, zcb)), (o, (tm, D_MODEL), row)],
               outs=[(SDS((t, D_MODEL), BF16), (tm, D_MODEL), row), (SDS((t, D_MODEL), BF16), (tm, D_MODEL), row),
                     (SDS((heads, t, 1), F32), (heads, tm, 1), lambda i, j, k: (0, i, 0))],
               epilogue=epilogue)


def _ple_fwd(h, pin, w_up, w_gate, name):
    t = h.shape[0]
    tm = 512
    pd = pin.shape[1]

    def body(h_ref, p_ref, wu_ref, wg_ref, hn_ref, u_ref, a_ref):
        h = h_ref[...]
        u = _dot(p_ref[...].astype(BF16), wu_ref[...], NN)
        a = _dot(h.astype(BF16), wg_ref[...], NN)
        hn_ref[...] = h + u * _sigmoid(a)
        u_ref[...] = u.astype(BF16)
        a_ref[...] = a.astype(BF16)

    rows = pl.BlockSpec((tm, D_MODEL), lambda i: (i, 0))
    return _pcall(
        body, name=name, grid=(t // tm,),
        in_specs=[rows, pl.BlockSpec((tm, pd), lambda i: (i, 0)),
                  pl.BlockSpec((pd, D_MODEL), lambda i: (0, 0)), pl.BlockSpec((D_MODEL, D_MODEL), lambda i: (0, 0))],
        out_specs=[rows, rows, rows],
        out_shape=[SDS((t, D_MODEL), F32), SDS((t, D_MODEL), BF16), SDS((t, D_MODEL), BF16)],
        compiler_params=_params(),
    )(h, pin, w_up, w_gate)


def _ple_bwd_elem(dh, u, a, name):
    t = dh.shape[0]
    tm = 512

    def body(dh_ref, u_ref, a_ref, du_ref, da_ref):
        g = dh_ref[...]
        s = _sigmoid(a_ref[...].astype(F32))
        du_ref[...] = (g * s).astype(BF16)
        da_ref[...] = (g * u_ref[...].astype(F32) * (s * (1.0 - s))).astype(BF16)

    blk = pl.BlockSpec((tm, D_MODEL), lambda i: (i, 0))
    return _pcall(
        body, name=name, grid=(t // tm,), in_specs=[blk, blk, blk], out_specs=[blk, blk],
        out_shape=[SDS((t, D_MODEL), BF16), SDS((t, D_MODEL), BF16)], compiler_params=_params(),
    )(dh, u, a)


DIL_SCALE = LANES ** -0.5


def _dil_masks():
    ii = lax.broadcasted_iota(jnp.int32, (DIL_BLOCK, DIL_BLOCK), 0)
    jj = lax.broadcasted_iota(jnp.int32, (DIL_BLOCK, DIL_BLOCK), 1)
    return ii, jj


def _dil_scores(q, kp, kc, slope, has_prev):
    ii, jj = _dil_masks()
    dist_p = (DIL_BLOCK + ii - jj).astype(F32)
    dist_c = (ii - jj).astype(F32)
    sp = _dot(q, kp, NT) * DIL_SCALE - slope * dist_p
    sc = _dot(q, kc, NT) * DIL_SCALE - slope * dist_c
    sp = jnp.where((jj >= ii) & has_prev, sp, NEG_INF)
    sc = jnp.where(jj <= ii, sc, NEG_INF)
    return sp, sc


def _dil_fwd(proj, slopes, grp, dil, name):
    t = proj.shape[0]
    rows = DIL_BLOCK * dil
    nsb = t // rows
    qc, kc_, vc_ = grp * DIL_HEADS, 3 * DIL_HEADS + grp * DIL_HEADS, 6 * DIL_HEADS + grp * DIL_HEADS

    def body(q_ref, kp_ref, kc_ref, vp_ref, vc_ref, sl_ref, o_ref, lse_ref, qf, kpf, kcf, vpf, vcf, of, lf):
        m = pl.program_id(1)
        for src, dst in ((q_ref, qf), (kp_ref, kpf), (kc_ref, kcf), (vp_ref, vpf), (vc_ref, vcf)):
            dst[...] = src[...].astype(F32)
        slope = sl_ref[0]
        has_prev = m > 0
        for r in range(dil):
            ph = pl.ds(r, DIL_BLOCK, stride=dil)
            q = qf[ph, :].astype(BF16)
            sp, sc = _dil_scores(q, kpf[ph, :].astype(BF16), kcf[ph, :].astype(BF16), slope, has_prev)
            mx = jnp.maximum(jnp.max(sp, axis=-1, keepdims=True), jnp.max(sc, axis=-1, keepdims=True))
            pp = jnp.exp(sp - mx)
            pc = jnp.exp(sc - mx)
            l = jnp.sum(pp, axis=-1, keepdims=True) + jnp.sum(pc, axis=-1, keepdims=True)
            o = _dot(pp.astype(BF16), vpf[ph, :].astype(BF16), NN) + _dot(pc.astype(BF16), vcf[ph, :].astype(BF16), NN)
            of[ph, :] = o / l
            lf[ph, :] = mx + jnp.log(l)
        o_ref[...] = of[...].astype(BF16)
        lse_ref[0] = lf[...]

    prev = lambda h, m: jnp.maximum(m - 1, 0)
    blk = lambda col, rowmap: pl.BlockSpec((rows, LANES), lambda h, m: (rowmap(h, m), col + h))
    cur = lambda h, m: m
    return _pcall(
        body, name=name, grid=(DIL_HEADS, nsb),
        in_specs=[blk(qc, cur), blk(kc_, prev), blk(kc_, cur), blk(vc_, prev), blk(vc_, cur),
                  pl.BlockSpec((1, 1, 1), lambda h, m: (h, 0, 0))],
        out_specs=[pl.BlockSpec((rows, LANES), lambda h, m: (m, h)), pl.BlockSpec((1, rows, 1), lambda h, m: (h, m, 0))],
        out_shape=[SDS((t, D_MODEL), BF16), SDS((DIL_HEADS, t, 1), F32)],
        scratch_shapes=[pltpu.VMEM((rows, LANES), F32)] * 6 + [pltpu.VMEM((rows, 1), F32)],
        compiler_params=_params(),
    )(proj, proj, proj, proj, proj, slopes)


def _dil_mix(outs, lses, proj, z_col0):
    t = proj.shape[0]
    tm = 512
    zcb = z_col0 // LANES
    ng = len(outs)

    def body(*refs):
        o_refs, l_refs, z_ref = refs[:ng], refs[ng:2 * ng], refs[2 * ng]
        om_ref, g_ref, lse_ref = refs[2 * ng + 1:]
        ls = [r[0] for r in l_refs]
        mx = functools.reduce(jnp.maximum, ls)
        es = [jnp.exp(l - mx) for l in ls]
        tot = functools.reduce(jnp.add, es)
        o = functools.reduce(jnp.add, [(e / tot) * r[...].astype(F32) for e, r in zip(es, o_refs)])
        z = z_ref[...].astype(F32)
        om_ref[...] = o.astype(BF16)
        g_ref[...] = (o * (z * _sigmoid(z))).astype(BF16)
        lse_ref[0] = mx + jnp.log(tot)

    tile = pl.BlockSpec((tm, LANES), lambda i, h: (i, h))
    col = pl.BlockSpec((1, tm, 1), lambda i, h: (h, i, 0))
    return _pcall(
        body, name="dil_mix", grid=(t // tm, DIL_HEADS),
        in_specs=[tile] * ng + [col] * ng + [pl.BlockSpec((tm, LANES), lambda i, h: (i, zcb + h))],
        out_specs=[tile, tile, col],
        out_shape=[SDS((t, D_MODEL), BF16), SDS((t, D_MODEL), BF16), SDS((DIL_HEADS, t, 1), F32)],
        compiler_params=_params(),
    )(*outs, *lses, proj)


def _dil_bwd(proj, do, lse, delta, slopes, grp, dil, name):
    t = proj.shape[0]
    rows = DIL_BLOCK * dil
    nsb = t // rows
    qc, kc_, vc_ = grp * DIL_HEADS, 3 * DIL_HEADS + grp * DIL_HEADS, 6 * DIL_HEADS + grp * DIL_HEADS

    def body(q_ref, qn_ref, kp_ref, kc_ref, vp_ref, vc_ref, do_ref, don_ref, l_ref, ln_ref, d_ref, dn_ref, sl_ref,
             dq_ref, dk_ref, dv_ref, qf, qnf, kpf, kcf, vpf, vcf, dof, donf, dqf, dkf, dvf):
        m = pl.program_id(1)
        for src, dst in ((q_ref, qf), (qn_ref, qnf), (kp_ref, kpf), (kc_ref, kcf), (vp_ref, vpf), (vc_ref, vcf),
                         (do_ref, dof), (don_ref, donf)):
            dst[...] = src[...].astype(F32)
        slope = sl_ref[0]
        has_prev = m > 0
        has_next = m < nsb - 1
        ii, jj = _dil_masks()
        for r in range(dil):
            ph = pl.ds(r, DIL_BLOCK, stride=dil)
            q, kp, kc, vp, vc, dout = (x[ph, :].astype(BF16) for x in (qf, kpf, kcf, vpf, vcf, dof))
            sp, sc = _dil_scores(q, kp, kc, slope, has_prev)
            lrow = l_ref[0, ph, :]
            drow = d_ref[0, ph, :]
            pp = jnp.exp(sp - lrow)
            pc = jnp.exp(sc - lrow)
            dsp = pp * (_dot(dout, vp, NT) - drow)
            dsc = pc * (_dot(dout, vc, NT) - drow)
            dspb, dscb = dsp.astype(BF16), dsc.astype(BF16)
            dqf[ph, :] = (_dot(dspb, kp, NN) + _dot(dscb, kc, NN)) * DIL_SCALE
            qn = qnf[ph, :].astype(BF16)
            don = donf[ph, :].astype(BF16)
            sn = _dot(qn, kc, NT) * DIL_SCALE - slope * (DIL_BLOCK + ii - jj).astype(F32)
            pn = jnp.exp(jnp.where((jj >= ii) & has_next, sn, NEG_INF) - ln_ref[0, ph, :])
            dsn = (pn * (_dot(don, vc, NT) - dn_ref[0, ph, :])).astype(BF16)
            dkf[ph, :] = (_dot(dscb, q, TN) + _dot(dsn, qn, TN)) * DIL_SCALE
            dvf[ph, :] = _dot(pc.astype(BF16), dout, TN) + _dot(pn.astype(BF16), don, TN)
        dq_ref[...] = dqf[...].astype(BF16)
        dk_ref[...] = dkf[...].astype(BF16)
        dv_ref[...] = dvf[...].astype(BF16)

    prev = lambda h, m: jnp.maximum(m - 1, 0)
    cur = lambda h, m: m
    nxt = lambda h, m: jnp.minimum(m + 1, nsb - 1)
    blk = lambda col, rowmap: pl.BlockSpec((rows, LANES), lambda h, m: (rowmap(h, m), col + h))
    colblk = lambda rowmap: pl.BlockSpec((1, rows, 1), lambda h, m: (h, rowmap(h, m), 0))
    out_blk = pl.BlockSpec((rows, LANES), lambda h, m: (m, h))
    return _pcall(
        body, name=name, grid=(DIL_HEADS, nsb),
        in_specs=[blk(qc, cur), blk(qc, nxt), blk(kc_, prev), blk(kc_, cur), blk(vc_, prev), blk(vc_, cur),
                  blk(0, cur), blk(0, nxt), colblk(cur), colblk(nxt), colblk(cur), colblk(nxt),
                  pl.BlockSpec((1, 1, 1), lambda h, m: (h, 0, 0))],
        out_specs=[out_blk, out_blk, out_blk],
        out_shape=[SDS((t, D_MODEL), BF16)] * 3,
        scratch_shapes=[pltpu.VMEM((rows, LANES), F32)] * 11,
        compiler_params=_params(),
    )(proj, proj, proj, proj, proj, proj, do, do, lse, lse, delta, delta, slopes)


def _mesh_pos():
    x, y, c = lax.axis_index("x"), lax.axis_index("y"), lax.axis_index("c")
    return x, y, c


def _peer(pos, k):
    x, y, c = pos
    px = 1 - x if k & 4 else x
    py = 1 - y if k & 2 else y
    pc = 1 - c if k & 1 else c
    return (px, py, pc), 4 * px + 2 * py + pc


def _exchange(arrays, name, gather):
    n = len(arrays)
    hbm = pl.BlockSpec(memory_space=pltpu.HBM)

    def body(*refs):
        ins, outs = refs[:n], refs[n:2 * n]
        send_sems, recv_sems, local_sems = refs[2 * n:]
        pos = _mesh_pos()
        me = 4 * pos[0] + 2 * pos[1] + pos[2]
        local, sends, recvs = [], [], []
        for w in range(n):
            own = ins[w] if gather else ins[w].at[me]
            cp = pltpu.make_async_copy(own, outs[w].at[me], local_sems.at[w])
            cp.start()
            local.append(cp)
            for k in range(1, N_DEV):
                peer, peer_idx = _peer(pos, k)
                sem = w * (N_DEV - 1) + k - 1
                src = ins[w] if gather else ins[w].at[peer_idx]
                cp = pltpu.make_async_remote_copy(src_ref=src, dst_ref=outs[w].at[me], send_sem=send_sems.at[sem],
                                                  recv_sem=recv_sems.at[sem], device_id=peer, device_id_type=MESH)
                cp.start()
                sends.append(cp)
                recvs.append(pltpu.make_async_remote_copy(
                    src_ref=src, dst_ref=outs[w].at[peer_idx], send_sem=send_sems.at[sem],
                    recv_sem=recv_sems.at[sem], device_id=peer, device_id_type=MESH))
        for cp in recvs:
            cp.wait_recv()
        for cp in sends:
            cp.wait_send()
        for cp in local:
            cp.wait()

    out_shape = [SDS((N_DEV,) + (a.shape if gather else a.shape[1:]), a.dtype) for a in arrays]
    return _pcall(
        body, name=name, in_specs=[hbm] * n, out_specs=[hbm] * n, out_shape=out_shape,
        scratch_shapes=[pltpu.SemaphoreType.DMA((n * (N_DEV - 1),)), pltpu.SemaphoreType.DMA((n * (N_DEV - 1),)),
                        pltpu.SemaphoreType.DMA((n,))],
        compiler_params=pltpu.CompilerParams(has_side_effects=True),
    )(*arrays)


def _adam_math(w, g, m, v):
    m = ADAM_B1 * m + (1.0 - ADAM_B1) * g
    v = ADAM_B2 * v + (1.0 - ADAM_B2) * (g * g)
    m_hat = m / (1.0 - ADAM_B1 ** ADAM_STEP)
    v_hat = v / (1.0 - ADAM_B2 ** ADAM_STEP)
    delta = -ADAM_LR * (m_hat / (jnp.sqrt(v_hat) + ADAM_EPS) + ADAM_WD * w)
    return delta, m, v


def _adamw(recv, w, m, v, name):
    r, c = w.shape
    tr = min(r, 128)

    def body(g_ref, w_ref, m_ref, v_ref, go_ref, d_ref, mo_ref, vo_ref):
        g = g_ref[0].astype(F32)
        for s in range(1, N_DEV):
            g = g + g_ref[s].astype(F32)
        delta, mn, vn = _adam_math(w_ref[...], g, m_ref[...], v_ref[...])
        go_ref[...] = g
        d_ref[...] = delta
        mo_ref[...] = mn
        vo_ref[...] = vn

    blk = pl.BlockSpec((tr, c), lambda i: (i, 0))
    return _pcall(
        body, name=name, grid=(r // tr,),
        in_specs=[pl.BlockSpec((N_DEV, tr, c), lambda i: (0, i, 0)), blk, blk, blk],
        out_specs=[blk] * 4, out_shape=[SDS((r, c), F32)] * 4, compiler_params=_params(),
    )(recv, w, m, v)


VEC_ROWS = 32


def _small_allreduce_adamw(vec, w, m, v):
    def body(vec_ref, w_ref, m_ref, v_ref, g_ref, d_ref, mo_ref, vo_ref, gath, send_sems, recv_sems):
        pos = _mesh_pos()
        me = 4 * pos[0] + 2 * pos[1] + pos[2]
        sends, recvs = [], []
        for k in range(1, N_DEV):
            peer, peer_idx = _peer(pos, k)
            cp = pltpu.make_async_remote_copy(src_ref=vec_ref, dst_ref=gath.at[me], send_sem=send_sems.at[k - 1],
                                              recv_sem=recv_sems.at[k - 1], device_id=peer, device_id_type=MESH)
            cp.start()
            sends.append(cp)
            recvs.append(pltpu.make_async_remote_copy(src_ref=vec_ref, dst_ref=gath.at[peer_idx],
                                                      send_sem=send_sems.at[k - 1], recv_sem=recv_sems.at[k - 1],
                                                      device_id=peer, device_id_type=MESH))
        gath[me] = vec_ref[...]
        for cp in recvs:
            cp.wait_recv()
        for cp in sends:
            cp.wait_send()
        tot = gath[0]
        for s in range(1, N_DEV):
            tot = tot + gath[s]
        rowi = lax.broadcasted_iota(jnp.int32, (8, LANES), 0)
        mine = jnp.sum(jnp.where(rowi == me, tot[16:24, :], 0.0), axis=0, keepdims=True)
        g = jnp.concatenate([tot[0:16, :], jnp.broadcast_to(mine, (8, LANES)), tot[24:32, :]], axis=0)
        delta, mn, vn = _adam_math(w_ref[...], g, m_ref[...], v_ref[...])
        g_ref[...] = g
        d_ref[...] = delta
        mo_ref[...] = mn
        vo_ref[...] = vn

    vm = pl.BlockSpec(memory_space=pltpu.VMEM)
    return _pcall(
        body, name="small_allreduce_adamw", in_specs=[vm] * 4, out_specs=[vm] * 4,
        out_shape=[SDS((VEC_ROWS, LANES), F32)] * 4,
        scratch_shapes=[pltpu.VMEM((N_DEV, VEC_ROWS, LANES), F32), pltpu.SemaphoreType.DMA((N_DEV - 1,)),
                        pltpu.SemaphoreType.DMA((N_DEV - 1,))],
        compiler_params=pltpu.CompilerParams(has_side_effects=True),
    )(vec, w, m, v)


def _cols_to_slabs(a):
    r, c8 = a.shape
    return a.reshape(r, N_DEV, c8 // N_DEV).transpose(1, 0, 2)


def _slabs_to_cols(a):
    n, r, c = a.shape
    return a.transpose(1, 0, 2).reshape(r, n * c)


def _rows8(vec):
    return vec.reshape(-1, LANES)


def _pad_rows(a, rows):
    return jnp.pad(a, ((0, rows - a.shape[0]), (0, LANES - a.shape[1])))


def kernel(x, p, fox_norm, fox_w_in, fox_b_f, fox_w_out, dil_norm, dil_w_in, dil_w_out, ple_w_up, ple_w_gate, final_norm, loss_target, m_fox_norm, m_fox_w_in, m_fox_b_f, m_fox_w_out, m_dil_norm, m_dil_w_in, m_dil_w_out, m_ple_w_up, m_ple_w_gate, m_final_norm, v_fox_norm, v_fox_w_in, v_fox_b_f, v_fox_w_out, v_dil_norm, v_dil_w_in, v_dil_w_out, v_ple_w_up, v_ple_w_gate, v_final_norm):
    t = x.shape[1]
    d = D_MODEL
    xs, tgt = x[0], loss_target[0]
    p0, p1 = p[0, 0], p[1, 0]
    fox_cols = fox_w_in.shape[2]
    ple_dim = ple_w_up.shape[1]

    shards = [fox_w_in[0].astype(BF16), fox_w_out[0].astype(BF16), dil_w_in[0].astype(BF16),
              dil_w_out[0].astype(BF16), ple_w_up.reshape(-1, LANES).astype(BF16),
              ple_w_gate.reshape(-1, d).astype(BF16), dil_norm]
    gw = _exchange(shards, "all_gather_weights", gather=True)
    w_fox_in = _slabs_to_cols(gw[0])
    w_fox_main = w_fox_in[:, :4 * d]
    w_fox_f = jnp.pad(w_fox_in[:, 4 * d:], ((0, 0), (0, LANES - FOX_HEADS)))
    w_fox_out = gw[1].reshape(d, d)
    w_dil_in = _slabs_to_cols(gw[2])
    w_dil_out = gw[3].reshape(d, d)
    w_up = gw[4].reshape(N_DEV, 2, ple_dim, LANES).transpose(1, 2, 0, 3).reshape(2, ple_dim, d)
    w_gate = gw[5].reshape(N_DEV, 2, d // N_DEV, d).transpose(1, 0, 2, 3).reshape(2, d, d)
    dil_norm_full = gw[6].reshape(1, d)
    b_pad = jnp.pad(fox_b_f, ((0, 0), (0, LANES - FOX_HEADS)))

    n0, r0 = _rms_fwd(xs, fox_norm, "rms_fox")
    proj0 = _mm([n0], w_fox_main, "nn", "fox_in_proj", tiles=IN_PROJ_TILES)
    projf = _mm([n0], w_fox_f, "nn", "fox_gate_proj", tiles=IN_PROJ_TILES, out_dtype=F32)
    c_all = _fox_gate_fwd(projf, b_pad)
    c_t = c_all[:, :FOX_HEADS].T
    cq, ck = c_t.reshape(FOX_HEADS, t, 1), c_t.reshape(FOX_HEADS, 1, t)
    o0, g0, lse0 = _fox_fwd(proj0, cq, ck)
    h1 = _mm_residual(g0, w_fox_out, "nn", xs, "fox_out_proj")
    h2, u0, a0 = _ple_fwd(h1, p0, w_up[0], w_gate[0], "ple0_fwd")

    n1, r1 = _rms_fwd(h2, dil_norm_full, "rms_dil")
    proj1 = _mm([n1], w_dil_in, "nn", "dil_in_proj", tiles=IN_PROJ_TILES)
    n_heads = len(DIL_PATTERN) * DIL_HEADS
    slopes = 2.0 ** (-ALIBI_MAX_EXP * jnp.arange(1, n_heads + 1, dtype=F32) / n_heads)
    dil_o, dil_lse, dil_slopes = [], [], []
    for grp, (_, dil) in enumerate(DIL_PATTERN):
        sl = (slopes[grp * DIL_HEADS:(grp + 1) * DIL_HEADS] * dil).reshape(DIL_HEADS, 1, 1)
        og, lg = _dil_fwd(proj1, sl, grp, dil, f"dil_attn_fwd_{grp}")
        dil_o.append(og)
        dil_lse.append(lg)
        dil_slopes.append(sl)
    z1_col0 = 9 * d
    o1, g1, lse1 = _dil_mix(dil_o, dil_lse, proj1, z1_col0)
    h3 = _mm_residual(g1, w_dil_out, "nn", h2, "dil_out_proj")
    h4, u1, a1 = _ple_fwd(h3, p1, w_up[1], w_gate[1], "ple1_fwd")

    dh4, d_final_norm, loss_part = _final_bwd(h4, final_norm.reshape(1, d), tgt)

    du1, da1 = _ple_bwd_elem(dh4, u1, a1, "ple1_bwd_elem")
    dw_up1 = _dw(p1, du1, "ple1_dw_up")
    dw_gate1 = _dw(h3, da1, "ple1_dw_gate")
    dh3 = _mm_residual(da1, w_gate[1], "nt", dh4, "ple1_dh")

    dw_dil_out = _dw(g1, dh3, "dil_dw_out")
    do1, dz1, delta1 = _mm_gate_bwd(dh3, w_dil_out, proj1, z1_col0, o1, DIL_HEADS, "dil_dgate")
    dqs, dks, dvs = [], [], []
    for grp, (_, dil) in enumerate(DIL_PATTERN):
        dq, dk, dv = _dil_bwd(proj1, do1, lse1, delta1, dil_slopes[grp], grp, dil, f"dil_attn_bwd_{grp}")
        dqs.append(dq)
        dks.append(dk)
        dvs.append(dv)
    dproj1 = dqs + dks + dvs + [dz1]
    dw_dil_in = jnp.concatenate([_dw(n1, dpart, f"dil_dw_in_{s}") for s, dpart in enumerate(dproj1)], axis=1)
    dh2, d_dil_norm = _mm_in_bwd(dproj1, w_dil_in, h2, dil_norm_full, r1, dh3, "dil_dx")

    du0, da0 = _ple_bwd_elem(dh2, u0, a0, "ple0_bwd_elem")
    dw_up0 = _dw(p0, du0, "ple0_dw_up")
    dw_gate0 = _dw(h1, da0, "ple0_dw_gate")
    dh1 = _mm_residual(da0, w_gate[0], "nt", dh2, "ple0_dh")

    dw_fox_out = _dw(g0, dh1, "fox_dw_out")
    do0, dz0, delta0 = _mm_gate_bwd(dh1, w_fox_out, proj0, 3 * d, o0, FOX_HEADS, "fox_dgate")
    as_row = lambda a: a.reshape(FOX_HEADS, 1, t)
    dq0, dk0, dv0, dck, dcq = _fox_bwd(proj0, do0, as_row(lse0), as_row(delta0), ck, cq)
    head_cols = lambda a: jnp.pad(a.reshape(FOX_HEADS, t).T, ((0, 0), (0, LANES - FOX_HEADS)))
    df, d_b_f = _fox_gate_bwd(projf, b_pad, head_cols(dcq), head_cols(dck))
    dproj0 = [dq0, dk0, dv0, dz0]
    dw_fox_parts = [_dw(n0, dpart, f"fox_dw_in_{s}") for s, dpart in enumerate(dproj0)]
    dw_fox_f = _dw(n0, df, "fox_dw_gate")
    dn0_f = _mm([df], w_fox_f, "nt", "fox_dx_gate", out_dtype=F32)
    grad_x, d_fox_norm = _mm_in_bwd(dproj0, w_fox_main, xs, fox_norm, r0, dh1, "fox_dx", more=dn0_f)

    dw_fox_in = jnp.concatenate(dw_fox_parts + [dw_fox_f[:, :FOX_HEADS]], axis=1)
    slabs = [_cols_to_slabs(dw_fox_in), dw_fox_out.reshape(N_DEV, d // N_DEV, d), _cols_to_slabs(dw_dil_in),
             dw_dil_out.reshape(N_DEV, d // N_DEV, d),
             jnp.stack([dw_up0, dw_up1]).reshape(2, ple_dim, N_DEV, LANES).transpose(2, 0, 1, 3).reshape(N_DEV, -1, LANES),
             jnp.stack([dw_gate0, dw_gate1]).reshape(2, N_DEV, d // N_DEV, d).transpose(1, 0, 2, 3).reshape(N_DEV, -1, d)]
    recv = _exchange(slabs, "reduce_scatter_grads", gather=False)
    big = [(fox_w_in, m_fox_w_in, v_fox_w_in), (fox_w_out, m_fox_w_out, v_fox_w_out),
           (dil_w_in, m_dil_w_in, v_dil_w_in), (dil_w_out, m_dil_w_out, v_dil_w_out),
           (ple_w_up, m_ple_w_up, v_ple_w_up), (ple_w_gate, m_ple_w_gate, v_ple_w_gate)]
    names = ["fox_w_in", "fox_w_out", "dil_w_in", "dil_w_out", "ple_w_up", "ple_w_gate"]
    upd = {}
    for rv, (w, m, v), nm in zip(recv, big, names):
        shp2 = rv.shape[1:]
        res = _adamw(rv, w.reshape(shp2), m.reshape(shp2), v.reshape(shp2), "adamw_" + nm)
        upd[nm] = [a.reshape(w.shape) for a in res]

    loss_row = jnp.where(jnp.arange(LANES) == 0, loss_part, 0.0)
    vec = jnp.concatenate([_rows8(d_fox_norm), _rows8(d_final_norm), _rows8(d_dil_norm), d_b_f, loss_row,
                           jnp.zeros((VEC_ROWS - 26, LANES), F32)], axis=0)

    def small_pack(a_fox_norm, a_final_norm, a_dil_norm, a_b_f):
        return jnp.concatenate([_rows8(a_fox_norm), _rows8(a_final_norm), _pad_rows(a_dil_norm, 8),
                                _pad_rows(a_b_f, 8)], axis=0)

    sg, sd, sm, sv = _small_allreduce_adamw(
        vec, small_pack(fox_norm, final_norm, dil_norm, fox_b_f),
        small_pack(m_fox_norm, m_final_norm, m_dil_norm, m_fox_b_f),
        small_pack(v_fox_norm, v_final_norm, v_dil_norm, v_fox_b_f))

    def small_unpack(a):
        return {"fox_norm": a[0:8].reshape(1, d), "final_norm": a[8:16].reshape(d), "dil_norm": a[16:17],
                "fox_b_f": a[24:25, :FOX_HEADS]}

    loss = sg[25, 0]
    order = ["fox_norm", "fox_w_in", "fox_b_f", "fox_w_out", "dil_norm", "dil_w_in", "dil_w_out", "ple_w_up",
             "ple_w_gate", "final_norm"]
    out = [loss, grad_x[None]]
    for idx, small in enumerate((sg, sd, sm, sv)):
        sp = small_unpack(small)
        out += [sp[nm] if nm in sp else upd[nm][idx] for nm in order]
    return tuple(out)
```

```python
import functools

import jax
import jax.numpy as jnp
from jax import lax
from jax.experimental import pallas as pl
from jax.experimental.pallas import tpu as pltpu

F32 = jnp.float32
BF16 = jnp.bfloat16
SDS = jax.ShapeDtypeStruct

D_MODEL = 1024
N_DEV = 8
LANES = 128
FOX_HEADS = 16
FOX_HEAD_DIM = 64
FOX_PAIRS = FOX_HEADS // 2
DIL_HEADS = 8
DIL_BLOCK = 128
DIL_PATTERN = ((128, 1), (512, 4), (2048, 16))
ALIBI_MAX_EXP = 8.0
RMS_EPS = 1e-6
ADAM_LR, ADAM_B1, ADAM_B2, ADAM_EPS, ADAM_WD, ADAM_STEP = 0.001, 0.9, 0.999, 1e-08, 0.01, 10
VMEM_LIMIT = 48 * 1024 * 1024
NEG_INF = float("-inf")

NN = (((1,), (0,)), ((), ()))
NT = (((1,), (1,)), ((), ()))
TN = (((0,), (0,)), ((), ()))
MESH = pl.DeviceIdType.MESH


def _pcall(body, **kw):
    return pl.pallas_call(body, **kw)


def _params(**kw):
    return pltpu.CompilerParams(vmem_limit_bytes=VMEM_LIMIT, **kw)


def _dot(a, b, dims):
    return lax.dot_general(a, b, dims, preferred_element_type=F32)


def _sigmoid(x):
    return 1.0 / (1.0 + jnp.exp(-x))


def _mm(a_parts, b, mode, name, tiles=(512, 1024, 1024), extras=(), outs=None, epilogue=None, out_dtype=BF16):
    na = len(a_parts)
    if mode == "tn":
        k_part, m = a_parts[0].shape
        n = b.shape[1]
    else:
        m, k_part = a_parts[0].shape
        n = b.shape[1] if mode == "nn" else b.shape[0]
    tm, tn, tk = min(tiles[0], m), min(tiles[1], n), min(tiles[2], k_part)
    kb = k_part // tk
    nk = na * kb
    grid = (m // tm, n // tn, nk)

    in_specs = []
    for s in range(na):
        if mode == "tn":
            in_specs.append(pl.BlockSpec((tk, tm), lambda i, j, k: (k, i)))
        else:
            in_specs.append(pl.BlockSpec((tm, tk), lambda i, j, k, s=s: (i, jnp.clip(k - s * kb, 0, kb - 1))))
    if mode == "nt":
        in_specs.append(pl.BlockSpec((tn, tk), lambda i, j, k: (j, k)))
    else:
        in_specs.append(pl.BlockSpec((tk, tn), lambda i, j, k: (k, j)))
    for _, blk, imap in extras:
        in_specs.append(pl.BlockSpec(blk, imap))
    if outs is None:
        outs = [(SDS((m, n), out_dtype), (tm, tn), lambda i, j, k: (i, j))]
    out_specs = [pl.BlockSpec(blk, imap) for _, blk, imap in outs]
    ne, no = len(extras), len(outs)
    dims = {"nn": NN, "nt": NT, "tn": TN}[mode]

    def finish(res, e_refs, o_refs, i):
        if epilogue is None:
            o_refs[0][...] = res.astype(o_refs[0].dtype)
        else:
            epilogue(res, e_refs, o_refs, i)

    def body(*refs):
        a_refs = refs[:na]
        b_ref = refs[na]
        e_refs = refs[na + 1:na + 1 + ne]
        o_refs = refs[na + 1 + ne:na + 1 + ne + no]
        i, k = pl.program_id(0), pl.program_id(2)
        if nk == 1:
            finish(_dot(a_refs[0][...].astype(BF16), b_ref[...].astype(BF16), dims), e_refs, o_refs, i)
            return
        acc = refs[-1]

        @pl.when(k == 0)
        def _():
            acc[...] = jnp.zeros_like(acc)

        def step(a_ref):
            acc[...] += _dot(a_ref[...].astype(BF16), b_ref[...].astype(BF16), dims)

        for s in range(na):
            if na == 1:
                step(a_refs[0])
            else:
                pl.when((k >= s * kb) & (k < (s + 1) * kb))(functools.partial(step, a_refs[s]))

        @pl.when(k == nk - 1)
        def _():
            finish(acc[...], e_refs, o_refs, i)

    res = _pcall(
        body, name=name, grid=grid, in_specs=in_specs, out_specs=out_specs,
        out_shape=[o[0] for o in outs], scratch_shapes=[] if nk == 1 else [pltpu.VMEM((tm, tn), F32)],
        compiler_params=_params(dimension_semantics=("arbitrary", "arbitrary", "arbitrary")),
    )(*a_parts, b, *[e[0] for e in extras])
    return res[0] if len(res) == 1 else res


IN_PROJ_TILES = (1024, 1024, 1024)
DW_TILES = (1024, 1024, 512)


def _dw(x, dy, name):
    return _mm([x], dy, "tn", name, tiles=DW_TILES)


def _add_extra_epilogue(acc, e_refs, o_refs, i):
    o_refs[0][...] = acc + e_refs[0][...]


def _mm_residual(a, b, mode, res, name):
    m = a.shape[0]
    n = b.shape[1] if mode == "nn" else b.shape[0]
    tm, tn = 512, 1024
    return _mm([a], b, mode, name, tiles=(tm, tn, 1024),
               extras=[(res, (tm, tn), lambda i, j, k: (i, j))],
               outs=[(SDS((m, n), F32), (tm, tn), lambda i, j, k: (i, j))],
               epilogue=_add_extra_epilogue)


def _rms_fwd(h, g, name):
    t, d = h.shape
    tm = 512

    def body(h_ref, g_ref, n_ref, r_ref):
        x = h_ref[...]
        r = lax.rsqrt(jnp.mean(x * x, axis=-1, keepdims=True) + RMS_EPS)
        n_ref[...] = ((x * r) * g_ref[...]).astype(BF16)
        r_ref[...] = r

    return _pcall(
        body, name=name, grid=(t // tm,),
        in_specs=[pl.BlockSpec((tm, d), lambda i: (i, 0)), pl.BlockSpec((1, d), lambda i: (0, 0))],
        out_specs=[pl.BlockSpec((tm, d), lambda i: (i, 0)), pl.BlockSpec((tm, 1), lambda i: (i, 0))],
        out_shape=[SDS((t, d), BF16), SDS((t, 1), F32)],
        compiler_params=_params(),
    )(h, g)


def _rms_bwd_rows(dn, x, g, r):
    xhat = x * r
    dxhat = dn * g
    dx = r * (dxhat - xhat * jnp.mean(dxhat * xhat, axis=-1, keepdims=True))
    dg = jnp.sum(dn * xhat, axis=0, keepdims=True)
    return dx, dg


def _mm_in_bwd(d_parts, w, h, g, r, dres, name, more=None):
    t = h.shape[0]
    tm = 512
    tk = 1024 if len(d_parts) <= 4 else 512
    row = lambda i, j, k: (i, 0)
    extras = [(h, (tm, D_MODEL), row), (g, (1, D_MODEL), lambda i, j, k: (0, 0)), (r, (tm, 1), row),
              (dres, (tm, D_MODEL), row)]
    if more is not None:
        extras.append((more, (tm, D_MODEL), row))

    def epilogue(acc, e_refs, o_refs, i):
        dn = acc if more is None else acc + e_refs[4][...]
        dx, dg = _rms_bwd_rows(dn, e_refs[0][...], e_refs[1][...], e_refs[2][...])
        o_refs[0][...] = e_refs[3][...] + dx

        @pl.when(i == 0)
        def _():
            o_refs[1][...] = dg

        @pl.when(i > 0)
        def _():
            o_refs[1][...] += dg

    return _mm(d_parts, w, "nt", name, tiles=(tm, D_MODEL, tk), extras=extras,
               outs=[(SDS((t, D_MODEL), F32), (tm, D_MODEL), row),
                     (SDS((1, D_MODEL), F32), (1, D_MODEL), lambda i, j, k: (0, 0))],
               epilogue=epilogue)


def _final_bwd(h, g, tgt):
    t, d = h.shape
    tm = 256

    def body(h_ref, g_ref, t_ref, dh_ref, dg_ref, loss_ref):
        i = pl.program_id(0)
        x = h_ref[...]
        gg = g_ref[...]
        r = lax.rsqrt(jnp.mean(x * x, axis=-1, keepdims=True) + RMS_EPS)
        err = (x * r) * gg - t_ref[...]
        part = 0.5 * jnp.sum(jnp.mean(err * err, axis=-1, keepdims=True), axis=0, keepdims=True)
        dx, dg = _rms_bwd_rows(err * (1.0 / d), x, gg, r)
        dh_ref[...] = dx

        @pl.when(i == 0)
        def _():
            dg_ref[...] = dg
            loss_ref[...] = jnp.broadcast_to(part, loss_ref.shape)

        @pl.when(i > 0)
        def _():
            dg_ref[...] += dg
            loss_ref[...] += jnp.broadcast_to(part, loss_ref.shape)

    return _pcall(
        body, name="final_norm_loss", grid=(t // tm,),
        in_specs=[pl.BlockSpec((tm, d), lambda i: (i, 0)), pl.BlockSpec((1, d), lambda i: (0, 0)),
                  pl.BlockSpec((tm, d), lambda i: (i, 0))],
        out_specs=[pl.BlockSpec((tm, d), lambda i: (i, 0)), pl.BlockSpec((1, d), lambda i: (0, 0)),
                   pl.BlockSpec((1, LANES), lambda i: (0, 0))],
        out_shape=[SDS((t, d), F32), SDS((1, d), F32), SDS((1, LANES), F32)],
        compiler_params=_params(),
    )(h, g, tgt)


GATE_ROWS = 256


def _split3(x):
    hi = x.astype(BF16)
    r1 = x - hi.astype(F32)
    mid = r1.astype(BF16)
    lo = (r1 - mid.astype(F32)).astype(BF16)
    return hi, mid, lo


def _tri_sum(x, upper):
    rows = x.shape[0]
    ri = lax.broadcasted_iota(jnp.int32, (rows, rows), 0)
    ci = lax.broadcasted_iota(jnp.int32, (rows, rows), 1)
    tri = jnp.where((ri <= ci) if upper else (ri >= ci), 1.0, 0.0).astype(BF16)
    hi, mid, lo = _split3(x)
    return _dot(tri, hi, NN) + _dot(tri, mid, NN) + _dot(tri, lo, NN)


def _log_sigmoid(x):
    return jnp.minimum(x, 0.0) - jnp.log1p(jnp.exp(-jnp.abs(x)))


def _fox_gate_fwd(projf, bpad):
    t = projf.shape[0]
    tb = GATE_ROWS

    def body(x_ref, b_ref, c_ref, carry):
        i = pl.program_id(0)

        @pl.when(i == 0)
        def _():
            carry[...] = jnp.zeros_like(carry)

        c_ref[...] = _tri_sum(_log_sigmoid(x_ref[...] + b_ref[...]), upper=False) + carry[...]
        carry[...] = c_ref[pl.ds(tb - 1, 1), :]

    return _pcall(
        body, name="fox_gate_fwd", grid=(t // tb,),
        in_specs=[pl.BlockSpec((tb, LANES), lambda i: (i, 0)), pl.BlockSpec((1, LANES), lambda i: (0, 0))],
        out_specs=pl.BlockSpec((tb, LANES), lambda i: (i, 0)),
        out_shape=SDS((t, LANES), F32), scratch_shapes=[pltpu.VMEM((1, LANES), F32)],
        compiler_params=_params(),
    )(projf, bpad)


def _fox_gate_bwd(projf, bpad, dc_query, dc_key):
    t = projf.shape[0]
    tb = GATE_ROWS
    nb = t // tb

    def body(x_ref, b_ref, dcq_ref, dck_ref, df_ref, db_ref, carry, buf):
        i = pl.program_id(0)

        @pl.when(i == 0)
        def _():
            carry[...] = jnp.zeros_like(carry)

        buf[...] = _tri_sum(dcq_ref[...] - dck_ref[...], upper=True) + carry[...]
        carry[...] = buf[pl.ds(0, 1), :]
        df = buf[...] * _sigmoid(-(x_ref[...] + b_ref[...]))
        df_ref[...] = df.astype(BF16)
        part = jnp.sum(df, axis=0, keepdims=True)

        @pl.when(i == 0)
        def _():
            db_ref[...] = part

        @pl.when(i > 0)
        def _():
            db_ref[...] += part

    rev = lambda i: (nb - 1 - i, 0)
    return _pcall(
        body, name="fox_gate_bwd", grid=(nb,),
        in_specs=[pl.BlockSpec((tb, LANES), rev), pl.BlockSpec((1, LANES), lambda i: (0, 0)),
                  pl.BlockSpec((tb, LANES), rev), pl.BlockSpec((tb, LANES), rev)],
        out_specs=[pl.BlockSpec((tb, LANES), rev), pl.BlockSpec((1, LANES), lambda i: (0, 0))],
        out_shape=[SDS((t, LANES), BF16), SDS((1, LANES), F32)],
        scratch_shapes=[pltpu.VMEM((1, LANES), F32), pltpu.VMEM((tb, LANES), F32)],
        compiler_params=_params(),
    )(projf, bpad, dc_query, dc_key)


FOX_TQ = 512
FOX_SCALE = FOX_HEAD_DIM ** -0.5


def _low_lanes(shape):
    return lax.broadcasted_iota(jnp.int32, shape, len(shape) - 1) < FOX_HEAD_DIM


FOX_AUG = 3


def _top_rows(shape):
    return lax.broadcasted_iota(jnp.int32, shape, 0) < FOX_HEAD_DIM


def _fox_aug(a, b, sign_a, sign_b, piece_entry, ones_entry, name):
    t = a.shape[0]
    tb = 512

    def body(a_ref, b_ref, o_ref):
        x = sign_a * a_ref[...]
        if sign_b != 0.0:
            x = x + sign_b * b_ref[...]
        head = lax.broadcasted_iota(jnp.int32, (LANES, D_MODEL), 0)
        col = lax.broadcasted_iota(jnp.int32, (LANES, D_MODEL), 1)
        base = (head // 2) * LANES + (1 - head % 2) * FOX_HEAD_DIM + piece_entry
        acc = jnp.zeros((tb, D_MODEL), F32)
        for e, piece in enumerate(_split3(x)):
            place = jnp.where((head < FOX_HEADS) & (col == base + e), 1.0, 0.0).astype(BF16)
            acc = acc + _dot(piece, place, NN)
        if ones_entry is not None:
            ent = lax.broadcasted_iota(jnp.int32, (1, D_MODEL), 1) % FOX_HEAD_DIM
            acc = acc + jnp.where((ent >= ones_entry) & (ent < ones_entry + FOX_AUG), 1.0, 0.0)
        o_ref[...] = acc.astype(BF16)

    blk = pl.BlockSpec((tb, LANES), lambda i: (i, 0))
    return _pcall(
        body, name=name, grid=(t // tb,), in_specs=[blk, blk],
        out_specs=pl.BlockSpec((tb, D_MODEL), lambda i: (i, 0)), out_shape=SDS((t, D_MODEL), BF16),
        compiler_params=_params(),
    )(a, b)


def _fox_unpack_dc(dck_wide, dcq):
    t = dck_wide.shape[0]
    dck = dck_wide.reshape(t, FOX_PAIRS, 2, FOX_HEAD_DIM)[:, :, ::-1, 0].reshape(t, FOX_HEADS)
    return dcq[:, :, 0, :].reshape(FOX_HEADS, t).T, dck


def _pair_operand(low, own, other, hh):
    return jnp.where(low, own, other) if hh == 0 else jnp.where(low, other, own)


def _fox_fwd(proj, qaug, kaug):
    t = proj.shape[0]
    tq = tk = min(FOX_TQ, t)
    nq = t // tq
    cb = D_MODEL // LANES

    def body(q_ref, k_ref, v_ref, z_ref, qa_ref, ka_ref, o_ref, g_ref, lse_ref, m_s, l_s, acc_s):
        i, j = pl.program_id(1), pl.program_id(2)

        @pl.when(j == 0)
        def _():
            m_s[...] = jnp.full_like(m_s, NEG_INF)
            l_s[...] = jnp.zeros_like(l_s)
            acc_s[...] = jnp.zeros_like(acc_s)

        low = _low_lanes((tq, LANES))
        top = _top_rows((LANES, tq))

        def update(masked):
            qs = q_ref[...] * FOX_SCALE
            qa, k, ka, v = qa_ref[...], k_ref[...], ka_ref[...], v_ref[...]
            if masked:
                causal = (lax.broadcasted_iota(jnp.int32, (tk, tq), 0) <= lax.broadcasted_iota(jnp.int32, (tk, tq), 1))
            upd = []
            for hh in range(2):
                s = _dot(_pair_operand(low, k, ka, hh), _pair_operand(low, qs, qa, hh), NT)
                if masked:
                    s = jnp.where(causal, s, NEG_INF)
                m_prev = m_s[hh]
                m_new = jnp.maximum(m_prev, jnp.max(s, axis=0, keepdims=True))
                alpha = jnp.exp(m_prev - m_new)
                p = jnp.exp(s - m_new)
                l_s[hh] = alpha * l_s[hh] + jnp.sum(p, axis=0, keepdims=True)
                m_s[hh] = m_new
                upd.append((alpha, _dot(v, p.astype(BF16), TN)))
            acc = acc_s[...]
            acc_s[...] = jnp.where(top, acc * upd[0][0] + upd[0][1], acc * upd[1][0] + upd[1][1])

        pl.when(j < i)(functools.partial(update, False))
        pl.when(j == i)(functools.partial(update, True))

        @pl.when(j == i)
        def _():
            o = (acc_s[...] / jnp.where(top, l_s[0], l_s[1])).T
            z = z_ref[...].astype(F32)
            o_ref[...] = o.astype(BF16)
            g_ref[...] = (o * (z * _sigmoid(z))).astype(BF16)
            for hh in range(2):
                lse_ref[hh] = m_s[hh] + jnp.log(l_s[hh])

    kv = lambda h, i, j: jnp.minimum(j, i)
    qblk = lambda col: pl.BlockSpec((tq, LANES), lambda h, i, j: (i, col + h))
    kblk = lambda col: pl.BlockSpec((tk, LANES), lambda h, i, j: (kv(h, i, j), col + h))
    return _pcall(
        body, name="fox_attn_fwd", grid=(FOX_PAIRS, nq, nq),
        in_specs=[qblk(0), kblk(cb), kblk(2 * cb), qblk(3 * cb), qblk(0), kblk(0)],
        out_specs=[qblk(0), qblk(0), pl.BlockSpec((2, 1, tq), lambda h, i, j: (h, 0, i))],
        out_shape=[SDS((t, D_MODEL), BF16), SDS((t, D_MODEL), BF16), SDS((FOX_HEADS, 1, t), F32)],
        scratch_shapes=[pltpu.VMEM((2, 1, tq), F32), pltpu.VMEM((2, 1, tq), F32), pltpu.VMEM((LANES, tq), F32)],
        compiler_params=_params(dimension_semantics=("arbitrary", "arbitrary", "arbitrary")),
    )(proj, proj, proj, proj, qaug, kaug)


FOX_SUM_ROWS = 8


def _fox_bwd(proj, do, qaug, kaug, doaug):
    t = proj.shape[0]
    tq = tk = min(FOX_TQ, t)
    nq = t // tq
    cb = D_MODEL // LANES

    def body(q_ref, k_ref, v_ref, do_ref, qa_ref, ka_ref, da_ref,
             dq_ref, dk_ref, dv_ref, dck_ref, dcq_ref, dq_acc, dcq_acc, dk_acc, dks_acc, dv_acc):
        j, i = pl.program_id(1), pl.program_id(2)
        low = _low_lanes((tq, LANES))
        top = _top_rows((LANES, tq))

        @pl.when(i == j)
        def _():
            dk_acc[...] = jnp.zeros_like(dk_acc)
            dks_acc[...] = jnp.zeros_like(dks_acc)
            dv_acc[...] = jnp.zeros_like(dv_acc)

        def update(masked):
            qs = q_ref[...] * FOX_SCALE
            k, v, dout = k_ref[...], v_ref[...], do_ref[...]
            qa, ka, da = qa_ref[...], ka_ref[...], da_ref[...]
            lane = lax.broadcasted_iota(jnp.int32, (tk, LANES), 1)
            vone = jnp.where((lane & (FOX_HEAD_DIM - 1)) < FOX_AUG, 1.0, 0.0).astype(v.dtype)
            one = jnp.ones_like(k)
            if masked:
                causal = (lax.broadcasted_iota(jnp.int32, (tk, tq), 0) <= lax.broadcasted_iota(jnp.int32, (tk, tq), 1))
            parts = []
            for hh in range(2):
                s = _dot(_pair_operand(low, k, ka, hh), _pair_operand(low, qs, qa, hh), NT)
                if masked:
                    s = jnp.where(causal, s, NEG_INF)
                p = jnp.exp(s)
                ds = p * _dot(_pair_operand(low, v, vone, hh), _pair_operand(low, dout, da, hh), NT)
                pb = p.astype(BF16)
                dsb = ds.astype(BF16)
                parts.append((_dot(pb, dout, NN),
                              _dot(dsb, _pair_operand(low, qs, one, hh), NN),
                              _dot(_pair_operand(low, k, one, hh), dsb, TN)))
            dv_acc[...] += jnp.where(low, parts[0][0], parts[1][0])
            dk_acc[...] += jnp.where(low, parts[0][1], parts[1][1])
            dks_acc[...] += jnp.where(low, parts[1][1], parts[0][1])
            dq_t = jnp.where(top, parts[0][2], parts[1][2]) * FOX_SCALE
            sum_a = parts[0][2][FOX_HEAD_DIM:FOX_HEAD_DIM + FOX_SUM_ROWS, :]
            sum_b = parts[1][2][0:FOX_SUM_ROWS, :]

            @pl.when(j == 0)
            def _():
                dq_acc[i] = dq_t
                dcq_acc[0, i] = sum_a
                dcq_acc[1, i] = sum_b

            @pl.when(j > 0)
            def _():
                dq_acc[i] += dq_t
                dcq_acc[0, i] += sum_a
                dcq_acc[1, i] += sum_b

        pl.when(i > j)(functools.partial(update, False))
        pl.when(i == j)(functools.partial(update, True))

        @pl.when(i == nq - 1)
        def _():
            dk_ref[...] = dk_acc[...].astype(BF16)
            dv_ref[...] = dv_acc[...].astype(BF16)
            dck_ref[...] = dks_acc[...]

        @pl.when((i == nq - 1) & (j == nq - 1))
        def _():
            for blk in range(nq):
                dq_ref[blk * tq:(blk + 1) * tq, :] = dq_acc[blk].T.astype(BF16)
            dcq_ref[...] = dcq_acc[...]

    qi = lambda h, j, i: jnp.maximum(i, j)
    qblk = lambda col: pl.BlockSpec((tq, LANES), lambda h, j, i: (qi(h, j, i), col + h))
    kblk = lambda col: pl.BlockSpec((tk, LANES), lambda h, j, i: (j, col + h))
    return _pcall(
        body, name="fox_attn_bwd", grid=(FOX_PAIRS, nq, nq),
        in_specs=[qblk(0), kblk(cb), kblk(2 * cb), qblk(0), qblk(0), kblk(0), qblk(0)],
        out_specs=[pl.BlockSpec((t, LANES), lambda h, j, i: (0, h)), kblk(0), kblk(0), kblk(0),
                   pl.BlockSpec((2, nq, FOX_SUM_ROWS, tq), lambda h, j, i: (h, 0, 0, 0))],
        out_shape=[SDS((t, D_MODEL), BF16), SDS((t, D_MODEL), BF16), SDS((t, D_MODEL), BF16),
                   SDS((t, D_MODEL), F32), SDS((FOX_HEADS, nq, FOX_SUM_ROWS, tq), F32)],
        scratch_shapes=[pltpu.VMEM((nq, LANES, tq), F32), pltpu.VMEM((2, nq, FOX_SUM_ROWS, tq), F32),
                        pltpu.VMEM((tk, LANES), F32), pltpu.VMEM((tk, LANES), F32), pltpu.VMEM((tk, LANES), F32)],
        compiler_params=_params(dimension_semantics=("arbitrary", "arbitrary", "arbitrary")),
    )(proj, proj, proj, do, qaug, kaug, doaug)


def _mm_gate_bwd(dh, w_out, z_src, z_col0, o, heads, name):
    t = dh.shape[0]
    tm = 512
    row = lambda i, j, k: (i, 0)
    zcb = z_col0 // D_MODEL

    def epilogue(acc, e_refs, o_refs, i):
        z = e_refs[0][...].astype(F32)
        ov = e_refs[1][...].astype(F32)
        sg = _sigmoid(z)
        dout = acc * (z * sg)
        o_refs[0][...] = dout.astype(BF16)
        o_refs[1][...] = (acc * ov * (sg * (1.0 + z * (1.0 - sg)))).astype(BF16)
        prod = dout * ov
        for cbk in range(D_MODEL // LANES):
            seg = prod[:, cbk * LANES:(cbk + 1) * LANES]
            tot = jnp.sum(seg, axis=-1, keepdims=True)
            if heads == D_MODEL // LANES:
                o_refs[2][cbk] = tot
            else:
                lo = jnp.sum(jnp.where(_low_lanes(seg.shape), seg, 0.0), axis=-1, keepdims=True)
                o_refs[2][2 * cbk] = lo
                o_refs[2][2 * cbk + 1] = tot - lo

    return _mm([dh], w_out, "nt", name, tiles=(tm, D_MODEL, D_MODEL),
               extras=[(z_src, (tm, D_MODEL), lambda i, j, k: (i, zcb)), (o, (tm, D_MODEL), row)],
               outs=[(SDS((t, D_MODEL), BF16), (tm, D_MODEL), row), (SDS((t, D_MODEL), BF16), (tm, D_MODEL), row),
                     (SDS((heads, t, 1), F32), (heads, tm, 1), lambda i, j, k: (0, i, 0))],
               epilogue=epilogue)


def _ple_fwd(h, pin, w_up, w_gate, name):
    t = h.shape[0]
    tm = 512
    pd = pin.shape[1]

    def body(h_ref, p_ref, wu_ref, wg_ref, hn_ref, u_ref, a_ref):
        h = h_ref[...]
        u = _dot(p_ref[...].astype(BF16), wu_ref[...], NN)
        a = _dot(h.astype(BF16), wg_ref[...], NN)
        hn_ref[...] = h + u * _sigmoid(a)
        u_ref[...] = u.astype(BF16)
        a_ref[...] = a.astype(BF16)

    rows = pl.BlockSpec((tm, D_MODEL), lambda i: (i, 0))
    return _pcall(
        body, name=name, grid=(t // tm,),
        in_specs=[rows, pl.BlockSpec((tm, pd), lambda i: (i, 0)),
                  pl.BlockSpec((pd, D_MODEL), lambda i: (0, 0)), pl.BlockSpec((D_MODEL, D_MODEL), lambda i: (0, 0))],
        out_specs=[rows, rows, rows],
        out_shape=[SDS((t, D_MODEL), F32), SDS((t, D_MODEL), BF16), SDS((t, D_MODEL), BF16)],
        compiler_params=_params(),
    )(h, pin, w_up, w_gate)


def _ple_bwd_elem(dh, u, a, name):
    t = dh.shape[0]
    tm = 512

    def body(dh_ref, u_ref, a_ref, du_ref, da_ref):
        g = dh_ref[...]
        s = _sigmoid(a_ref[...].astype(F32))
        du_ref[...] = (g * s).astype(BF16)
        da_ref[...] = (g * u_ref[...].astype(F32) * (s * (1.0 - s))).astype(BF16)

    blk = pl.BlockSpec((tm, D_MODEL), lambda i: (i, 0))
    return _pcall(
        body, name=name, grid=(t // tm,), in_specs=[blk, blk, blk], out_specs=[blk, blk],
        out_shape=[SDS((t, D_MODEL), BF16), SDS((t, D_MODEL), BF16)], compiler_params=_params(),
    )(dh, u, a)


DIL_SCALE = LANES ** -0.5


def _dil_masks():
    ii = lax.broadcasted_iota(jnp.int32, (DIL_BLOCK, DIL_BLOCK), 0)
    jj = lax.broadcasted_iota(jnp.int32, (DIL_BLOCK, DIL_BLOCK), 1)
    return ii, jj


def _dil_scores(q, kp, kc, slope, has_prev):
    ii, jj = _dil_masks()
    dist_p = (DIL_BLOCK + ii - jj).astype(F32)
    dist_c = (ii - jj).astype(F32)
    sp = _dot(q, kp, NT) * DIL_SCALE - slope * dist_p
    sc = _dot(q, kc, NT) * DIL_SCALE - slope * dist_c
    sp = jnp.where((jj >= ii) & has_prev, sp, NEG_INF)
    sc = jnp.where(jj <= ii, sc, NEG_INF)
    return sp, sc


def _dil_fwd(proj, slopes, grp, dil, name):
    t = proj.shape[0]
    rows = DIL_BLOCK * dil
    nsb = t // rows
    qc, kc_, vc_ = grp * DIL_HEADS, 3 * DIL_HEADS + grp * DIL_HEADS, 6 * DIL_HEADS + grp * DIL_HEADS

    def body(q_ref, kp_ref, kc_ref, vp_ref, vc_ref, sl_ref, o_ref, lse_ref, qf, kpf, kcf, vpf, vcf, of, lf):
        m = pl.program_id(1)
        for src, dst in ((q_ref, qf), (kp_ref, kpf), (kc_ref, kcf), (vp_ref, vpf), (vc_ref, vcf)):
            dst[...] = src[...].astype(F32)
        slope = sl_ref[0]
        has_prev = m > 0
        for r in range(dil):
            ph = pl.ds(r, DIL_BLOCK, stride=dil)
            q = qf[ph, :].astype(BF16)
            sp, sc = _dil_scores(q, kpf[ph, :].astype(BF16), kcf[ph, :].astype(BF16), slope, has_prev)
            mx = jnp.maximum(jnp.max(sp, axis=-1, keepdims=True), jnp.max(sc, axis=-1, keepdims=True))
            pp = jnp.exp(sp - mx)
            pc = jnp.exp(sc - mx)
            l = jnp.sum(pp, axis=-1, keepdims=True) + jnp.sum(pc, axis=-1, keepdims=True)
            o = _dot(pp.astype(BF16), vpf[ph, :].astype(BF16), NN) + _dot(pc.astype(BF16), vcf[ph, :].astype(BF16), NN)
            of[ph, :] = o / l
            lf[ph, :] = mx + jnp.log(l)
        o_ref[...] = of[...].astype(BF16)
        lse_ref[0] = lf[...]

    prev = lambda h, m: jnp.maximum(m - 1, 0)
    blk = lambda col, rowmap: pl.BlockSpec((rows, LANES), lambda h, m: (rowmap(h, m), col + h))
    cur = lambda h, m: m
    return _pcall(
        body, name=name, grid=(DIL_HEADS, nsb),
        in_specs=[blk(qc, cur), blk(kc_, prev), blk(kc_, cur), blk(vc_, prev), blk(vc_, cur),
                  pl.BlockSpec((1, 1, 1), lambda h, m: (h, 0, 0))],
        out_specs=[pl.BlockSpec((rows, LANES), lambda h, m: (m, h)), pl.BlockSpec((1, rows, 1), lambda h, m: (h, m, 0))],
        out_shape=[SDS((t, D_MODEL), BF16), SDS((DIL_HEADS, t, 1), F32)],
        scratch_shapes=[pltpu.VMEM((rows, LANES), F32)] * 6 + [pltpu.VMEM((rows, 1), F32)],
        compiler_params=_params(),
    )(proj, proj, proj, proj, proj, slopes)


def _dil_mix(outs, lses, proj, z_col0):
    t = proj.shape[0]
    tm = 512
    zcb = z_col0 // LANES
    ng = len(outs)

    def body(*refs):
        o_refs, l_refs, z_ref = refs[:ng], refs[ng:2 * ng], refs[2 * ng]
        om_ref, g_ref, lse_ref = refs[2 * ng + 1:]
        ls = [r[0] for r in l_refs]
        mx = functools.reduce(jnp.maximum, ls)
        es = [jnp.exp(l - mx) for l in ls]
        tot = functools.reduce(jnp.add, es)
        o = functools.reduce(jnp.add, [(e / tot) * r[...].astype(F32) for e, r in zip(es, o_refs)])
        z = z_ref[...].astype(F32)
        om_ref[...] = o.astype(BF16)
        g_ref[...] = (o * (z * _sigmoid(z))).astype(BF16)
        lse_ref[0] = mx + jnp.log(tot)

    tile = pl.BlockSpec((tm, LANES), lambda i, h: (i, h))
    col = pl.BlockSpec((1, tm, 1), lambda i, h: (h, i, 0))
    return _pcall(
        body, name="dil_mix", grid=(t // tm, DIL_HEADS),
        in_specs=[tile] * ng + [col] * ng + [pl.BlockSpec((tm, LANES), lambda i, h: (i, zcb + h))],
        out_specs=[tile, tile, col],
        out_shape=[SDS((t, D_MODEL), BF16), SDS((t, D_MODEL), BF16), SDS((DIL_HEADS, t, 1), F32)],
        compiler_params=_params(),
    )(*outs, *lses, proj)


def _dil_bwd(proj, do, lse, delta, slopes, grp, dil, name):
    t = proj.shape[0]
    rows = DIL_BLOCK * dil
    nsb = t // rows
    qc, kc_, vc_ = grp * DIL_HEADS, 3 * DIL_HEADS + grp * DIL_HEADS, 6 * DIL_HEADS + grp * DIL_HEADS

    def body(q_ref, qn_ref, kp_ref, kc_ref, vp_ref, vc_ref, do_ref, don_ref, l_ref, ln_ref, d_ref, dn_ref, sl_ref,
             dq_ref, dk_ref, dv_ref, qf, qnf, kpf, kcf, vpf, vcf, dof, donf, dqf, dkf, dvf):
        m = pl.program_id(1)
        for src, dst in ((q_ref, qf), (qn_ref, qnf), (kp_ref, kpf), (kc_ref, kcf), (vp_ref, vpf), (vc_ref, vcf),
                         (do_ref, dof), (don_ref, donf)):
            dst[...] = src[...].astype(F32)
        slope = sl_ref[0]
        has_prev = m > 0
        has_next = m < nsb - 1
        ii, jj = _dil_masks()
        for r in range(dil):
            ph = pl.ds(r, DIL_BLOCK, stride=dil)
            q, kp, kc, vp, vc, dout = (x[ph, :].astype(BF16) for x in (qf, kpf, kcf, vpf, vcf, dof))
            sp, sc = _dil_scores(q, kp, kc, slope, has_prev)
            lrow = l_ref[0, ph, :]
            drow = d_ref[0, ph, :]
            pp = jnp.exp(sp - lrow)
            pc = jnp.exp(sc - lrow)
            dsp = pp * (_dot(dout, vp, NT) - drow)
            dsc = pc * (_dot(dout, vc, NT) - drow)
            dspb, dscb = dsp.astype(BF16), dsc.astype(BF16)
            dqf[ph, :] = (_dot(dspb, kp, NN) + _dot(dscb, kc, NN)) * DIL_SCALE
            qn = qnf[ph, :].astype(BF16)
            don = donf[ph, :].astype(BF16)
            sn = _dot(qn, kc, NT) * DIL_SCALE - slope * (DIL_BLOCK + ii - jj).astype(F32)
            pn = jnp.exp(jnp.where((jj >= ii) & has_next, sn, NEG_INF) - ln_ref[0, ph, :])
            dsn = (pn * (_dot(don, vc, NT) - dn_ref[0, ph, :])).astype(BF16)
            dkf[ph, :] = (_dot(dscb, q, TN) + _dot(dsn, qn, TN)) * DIL_SCALE
            dvf[ph, :] = _dot(pc.astype(BF16), dout, TN) + _dot(pn.astype(BF16), don, TN)
        dq_ref[...] = dqf[...].astype(BF16)
        dk_ref[...] = dkf[...].astype(BF16)
        dv_ref[...] = dvf[...].astype(BF16)

    prev = lambda h, m: jnp.maximum(m - 1, 0)
    cur = lambda h, m: m
    nxt = lambda h, m: jnp.minimum(m + 1, nsb - 1)
    blk = lambda col, rowmap: pl.BlockSpec((rows, LANES), lambda h, m: (rowmap(h, m), col + h))
    colblk = lambda rowmap: pl.BlockSpec((1, rows, 1), lambda h, m: (h, rowmap(h, m), 0))
    out_blk = pl.BlockSpec((rows, LANES), lambda h, m: (m, h))
    return _pcall(
        body, name=name, grid=(DIL_HEADS, nsb),
        in_specs=[blk(qc, cur), blk(qc, nxt), blk(kc_, prev), blk(kc_, cur), blk(vc_, prev), blk(vc_, cur),
                  blk(0, cur), blk(0, nxt), colblk(cur), colblk(nxt), colblk(cur), colblk(nxt),
                  pl.BlockSpec((1, 1, 1), lambda h, m: (h, 0, 0))],
        out_specs=[out_blk, out_blk, out_blk],
        out_shape=[SDS((t, D_MODEL), BF16)] * 3,
        scratch_shapes=[pltpu.VMEM((rows, LANES), F32)] * 11,
        compiler_params=_params(),
    )(proj, proj, proj, proj, proj, proj, do, do, lse, lse, delta, delta, slopes)


def _mesh_pos():
    x, y, c = lax.axis_index("x"), lax.axis_index("y"), lax.axis_index("c")
    return x, y, c


def _peer(pos, k):
    x, y, c = pos
    px = 1 - x if k & 4 else x
    py = 1 - y if k & 2 else y
    pc = 1 - c if k & 1 else c
    return (px, py, pc), 4 * px + 2 * py + pc


def _exchange(arrays, name, gather):
    n = len(arrays)
    hbm = pl.BlockSpec(memory_space=pltpu.HBM)

    def body(*refs):
        ins, outs = refs[:n], refs[n:2 * n]
        send_sems, recv_sems, local_sems = refs[2 * n:]
        pos = _mesh_pos()
        me = 4 * pos[0] + 2 * pos[1] + pos[2]
        local, sends, recvs = [], [], []
        for w in range(n):
            own = ins[w] if gather else ins[w].at[me]
            cp = pltpu.make_async_copy(own, outs[w].at[me], local_sems.at[w])
            cp.start()
            local.append(cp)
            for k in range(1, N_DEV):
                peer, peer_idx = _peer(pos, k)
                sem = w * (N_DEV - 1) + k - 1
                src = ins[w] if gather else ins[w].at[peer_idx]
                cp = pltpu.make_async_remote_copy(src_ref=src, dst_ref=outs[w].at[me], send_sem=send_sems.at[sem],
                                                  recv_sem=recv_sems.at[sem], device_id=peer, device_id_type=MESH)
                cp.start()
                sends.append(cp)
                recvs.append(pltpu.make_async_remote_copy(
                    src_ref=src, dst_ref=outs[w].at[peer_idx], send_sem=send_sems.at[sem],
                    recv_sem=recv_sems.at[sem], device_id=peer, device_id_type=MESH))
        for cp in recvs:
            cp.wait_recv()
        for cp in sends:
            cp.wait_send()
        for cp in local:
            cp.wait()

    out_shape = [SDS((N_DEV,) + (a.shape if gather else a.shape[1:]), a.dtype) for a in arrays]
    return _pcall(
        body, name=name, in_specs=[hbm] * n, out_specs=[hbm] * n, out_shape=out_shape,
        scratch_shapes=[pltpu.SemaphoreType.DMA((n * (N_DEV - 1),)), pltpu.SemaphoreType.DMA((n * (N_DEV - 1),)),
                        pltpu.SemaphoreType.DMA((n,))],
        compiler_params=pltpu.CompilerParams(has_side_effects=True),
    )(*arrays)


def _adam_math(w, g, m, v):
    m = ADAM_B1 * m + (1.0 - ADAM_B1) * g
    v = ADAM_B2 * v + (1.0 - ADAM_B2) * (g * g)
    m_hat = m / (1.0 - ADAM_B1 ** ADAM_STEP)
    v_hat = v / (1.0 - ADAM_B2 ** ADAM_STEP)
    delta = -ADAM_LR * (m_hat / (jnp.sqrt(v_hat) + ADAM_EPS) + ADAM_WD * w)
    return delta, m, v


def _adamw(recv, w, m, v, name):
    r, c = w.shape
    tr = min(r, 128)

    def body(g_ref, w_ref, m_ref, v_ref, go_ref, d_ref, mo_ref, vo_ref):
        g = g_ref[0].astype(F32)
        for s in range(1, N_DEV):
            g = g + g_ref[s].astype(F32)
        delta, mn, vn = _adam_math(w_ref[...], g, m_ref[...], v_ref[...])
        go_ref[...] = g
        d_ref[...] = delta
        mo_ref[...] = mn
        vo_ref[...] = vn

    blk = pl.BlockSpec((tr, c), lambda i: (i, 0))
    return _pcall(
        body, name=name, grid=(r // tr,),
        in_specs=[pl.BlockSpec((N_DEV, tr, c), lambda i: (0, i, 0)), blk, blk, blk],
        out_specs=[blk] * 4, out_shape=[SDS((r, c), F32)] * 4, compiler_params=_params(),
    )(recv, w, m, v)


VEC_ROWS = 32


def _small_allreduce_adamw(vec, w, m, v):
    def body(vec_ref, w_ref, m_ref, v_ref, g_ref, d_ref, mo_ref, vo_ref, gath, send_sems, recv_sems):
        pos = _mesh_pos()
        me = 4 * pos[0] + 2 * pos[1] + pos[2]
        sends, recvs = [], []
        for k in range(1, N_DEV):
            peer, peer_idx = _peer(pos, k)
            cp = pltpu.make_async_remote_copy(src_ref=vec_ref, dst_ref=gath.at[me], send_sem=send_sems.at[k - 1],
                                              recv_sem=recv_sems.at[k - 1], device_id=peer, device_id_type=MESH)
            cp.start()
            sends.append(cp)
            recvs.append(pltpu.make_async_remote_copy(src_ref=vec_ref, dst_ref=gath.at[peer_idx],
                                                      send_sem=send_sems.at[k - 1], recv_sem=recv_sems.at[k - 1],
                                                      device_id=peer, device_id_type=MESH))
        gath[me] = vec_ref[...]
        for cp in recvs:
            cp.wait_recv()
        for cp in sends:
            cp.wait_send()
        tot = gath[0]
        for s in range(1, N_DEV):
            tot = tot + gath[s]
        rowi = lax.broadcasted_iota(jnp.int32, (8, LANES), 0)
        mine = jnp.sum(jnp.where(rowi == me, tot[16:24, :], 0.0), axis=0, keepdims=True)
        g = jnp.concatenate([tot[0:16, :], jnp.broadcast_to(mine, (8, LANES)), tot[24:32, :]], axis=0)
        delta, mn, vn = _adam_math(w_ref[...], g, m_ref[...], v_ref[...])
        g_ref[...] = g
        d_ref[...] = delta
        mo_ref[...] = mn
        vo_ref[...] = vn

    vm = pl.BlockSpec(memory_space=pltpu.VMEM)
    return _pcall(
        body, name="small_allreduce_adamw", in_specs=[vm] * 4, out_specs=[vm] * 4,
        out_shape=[SDS((VEC_ROWS, LANES), F32)] * 4,
        scratch_shapes=[pltpu.VMEM((N_DEV, VEC_ROWS, LANES), F32), pltpu.SemaphoreType.DMA((N_DEV - 1,)),
                        pltpu.SemaphoreType.DMA((N_DEV - 1,))],
        compiler_params=pltpu.CompilerParams(has_side_effects=True),
    )(vec, w, m, v)


def _cols_to_slabs(a):
    r, c8 = a.shape
    return a.reshape(r, N_DEV, c8 // N_DEV).transpose(1, 0, 2)


def _slabs_to_cols(a):
    n, r, c = a.shape
    return a.transpose(1, 0, 2).reshape(r, n * c)


def _rows8(vec):
    return vec.reshape(-1, LANES)


def _pad_rows(a, rows):
    return jnp.pad(a, ((0, rows - a.shape[0]), (0, LANES - a.shape[1])))


def kernel(x, p, fox_norm, fox_w_in, fox_b_f, fox_w_out, dil_norm, dil_w_in, dil_w_out, ple_w_up, ple_w_gate, final_norm, loss_target, m_fox_norm, m_fox_w_in, m_fox_b_f, m_fox_w_out, m_dil_norm, m_dil_w_in, m_dil_w_out, m_ple_w_up, m_ple_w_gate, m_final_norm, v_fox_norm, v_fox_w_in, v_fox_b_f, v_fox_w_out, v_dil_norm, v_dil_w_in, v_dil_w_out, v_ple_w_up, v_ple_w_gate, v_final_norm):
    t = x.shape[1]
    d = D_MODEL
    xs, tgt = x[0], loss_target[0]
    p0, p1 = p[0, 0], p[1, 0]
    fox_cols = fox_w_in.shape[2]
    ple_dim = ple_w_up.shape[1]

    shards = [fox_w_in[0].astype(BF16), fox_w_out[0].astype(BF16), dil_w_in[0].astype(BF16),
              dil_w_out[0].astype(BF16), ple_w_up.reshape(-1, LANES).astype(BF16),
              ple_w_gate.reshape(-1, d).astype(BF16), dil_norm]
    gw = _exchange(shards, "all_gather_weights", gather=True)
    w_fox_in = _slabs_to_cols(gw[0])
    w_fox_main = w_fox_in[:, :4 * d]
    w_fox_f = jnp.pad(w_fox_in[:, 4 * d:], ((0, 0), (0, LANES - FOX_HEADS)))
    w_fox_out = gw[1].reshape(d, d)
    w_dil_in = _slabs_to_cols(gw[2])
    w_dil_out = gw[3].reshape(d, d)
    w_up = gw[4].reshape(N_DEV, 2, ple_dim, LANES).transpose(1, 2, 0, 3).reshape(2, ple_dim, d)
    w_gate = gw[5].reshape(N_DEV, 2, d // N_DEV, d).transpose(1, 0, 2, 3).reshape(2, d, d)
    dil_norm_full = gw[6].reshape(1, d)
    b_pad = jnp.pad(fox_b_f, ((0, 0), (0, LANES - FOX_HEADS)))

    n0, r0 = _rms_fwd(xs, fox_norm, "rms_fox")
    proj0 = _mm([n0], w_fox_main, "nn", "fox_in_proj", tiles=IN_PROJ_TILES)
    projf = _mm([n0], w_fox_f, "nn", "fox_gate_proj", tiles=IN_PROJ_TILES, out_dtype=F32)
    c_all = _fox_gate_fwd(projf, b_pad)
    qaug_fwd = _fox_aug(c_all, c_all, 1.0, 0.0, 0, FOX_AUG, "fox_aug_q_fwd")
    kaug = _fox_aug(c_all, c_all, -1.0, 0.0, FOX_AUG, 0, "fox_aug_k")
    o0, g0, lse0 = _fox_fwd(proj0, qaug_fwd, kaug)
    h1 = _mm_residual(g0, w_fox_out, "nn", xs, "fox_out_proj")
    h2, u0, a0 = _ple_fwd(h1, p0, w_up[0], w_gate[0], "ple0_fwd")

    n1, r1 = _rms_fwd(h2, dil_norm_full, "rms_dil")
    proj1 = _mm([n1], w_dil_in, "nn", "dil_in_proj", tiles=IN_PROJ_TILES)
    n_heads = len(DIL_PATTERN) * DIL_HEADS
    slopes = 2.0 ** (-ALIBI_MAX_EXP * jnp.arange(1, n_heads + 1, dtype=F32) / n_heads)
    dil_o, dil_lse, dil_slopes = [], [], []
    for grp, (_, dil) in enumerate(DIL_PATTERN):
        sl = (slopes[grp * DIL_HEADS:(grp + 1) * DIL_HEADS] * dil).reshape(DIL_HEADS, 1, 1)
        og, lg = _dil_fwd(proj1, sl, grp, dil, f"dil_attn_fwd_{grp}")
        dil_o.append(og)
        dil_lse.append(lg)
        dil_slopes.append(sl)
    z1_col0 = 9 * d
    o1, g1, lse1 = _dil_mix(dil_o, dil_lse, proj1, z1_col0)
    h3 = _mm_residual(g1, w_dil_out, "nn", h2, "dil_out_proj")
    h4, u1, a1 = _ple_fwd(h3, p1, w_up[1], w_gate[1], "ple1_fwd")

    dh4, d_final_norm, loss_part = _final_bwd(h4, final_norm.reshape(1, d), tgt)

    du1, da1 = _ple_bwd_elem(dh4, u1, a1, "ple1_bwd_elem")
    dw_up1 = _dw(p1, du1, "ple1_dw_up")
    dw_gate1 = _dw(h3, da1, "ple1_dw_gate")
    dh3 = _mm_residual(da1, w_gate[1], "nt", dh4, "ple1_dh")

    dw_dil_out = _dw(g1, dh3, "dil_dw_out")
    do1, dz1, delta1 = _mm_gate_bwd(dh3, w_dil_out, proj1, z1_col0, o1, DIL_HEADS, "dil_dgate")
    dqs, dks, dvs = [], [], []
    for grp, (_, dil) in enumerate(DIL_PATTERN):
        dq, dk, dv = _dil_bwd(proj1, do1, lse1, delta1, dil_slopes[grp], grp, dil, f"dil_attn_bwd_{grp}")
        dqs.append(dq)
        dks.append(dk)
        dvs.append(dv)
    dproj1 = dqs + dks + dvs + [dz1]
    dw_dil_in = jnp.concatenate([_dw(n1, dpart, f"dil_dw_in_{s}") for s, dpart in enumerate(dproj1)], axis=1)
    dh2, d_dil_norm = _mm_in_bwd(dproj1, w_dil_in, h2, dil_norm_full, r1, dh3, "dil_dx")

    du0, da0 = _ple_bwd_elem(dh2, u0, a0, "ple0_bwd_elem")
    dw_up0 = _dw(p0, du0, "ple0_dw_up")
    dw_gate0 = _dw(h1, da0, "ple0_dw_gate")
    dh1 = _mm_residual(da0, w_gate[0], "nt", dh2, "ple0_dh")

    dw_fox_out = _dw(g0, dh1, "fox_dw_out")
    do0, dz0, delta0 = _mm_gate_bwd(dh1, w_fox_out, proj0, 3 * d, o0, FOX_HEADS, "fox_dgate")
    head_cols = lambda a: jnp.pad(a, ((0, 0), (0, LANES - FOX_HEADS)))
    lse_cols = head_cols(lse0.reshape(FOX_HEADS, t).T)
    delta_cols = head_cols(delta0.reshape(FOX_HEADS, t).T)
    qaug_bwd = _fox_aug(c_all, lse_cols, 1.0, -1.0, 0, FOX_AUG, "fox_aug_q_bwd")
    doaug = _fox_aug(delta_cols, delta_cols, -1.0, 0.0, 0, None, "fox_aug_do")
    dq0, dk0, dv0, dck_wide, dcq = _fox_bwd(proj0, do0, qaug_bwd, kaug, doaug)
    dc_query, dc_key = _fox_unpack_dc(dck_wide, dcq)
    df, d_b_f = _fox_gate_bwd(projf, b_pad, head_cols(dc_query), head_cols(dc_key))
    dproj0 = [dq0, dk0, dv0, dz0]
    dw_fox_parts = [_dw(n0, dpart, f"fox_dw_in_{s}") for s, dpart in enumerate(dproj0)]
    dw_fox_f = _dw(n0, df, "fox_dw_gate")
    dn0_f = _mm([df], w_fox_f, "nt", "fox_dx_gate", out_dtype=F32)
    grad_x, d_fox_norm = _mm_in_bwd(dproj0, w_fox_main, xs, fox_norm, r0, dh1, "fox_dx", more=dn0_f)

    dw_fox_in = jnp.concatenate(dw_fox_parts + [dw_fox_f[:, :FOX_HEADS]], axis=1)
    slabs = [_cols_to_slabs(dw_fox_in), dw_fox_out.reshape(N_DEV, d // N_DEV, d), _cols_to_slabs(dw_dil_in),
             dw_dil_out.reshape(N_DEV, d // N_DEV, d),
             jnp.stack([dw_up0, dw_up1]).reshape(2, ple_dim, N_DEV, LANES).transpose(2, 0, 1, 3).reshape(N_DEV, -1, LANES),
             jnp.stack([dw_gate0, dw_gate1]).reshape(2, N_DEV, d // N_DEV, d).transpose(1, 0, 2, 3).reshape(N_DEV, -1, d)]
    recv = _exchange(slabs, "reduce_scatter_grads", gather=False)
    big = [(fox_w_in, m_fox_w_in, v_fox_w_in), (fox_w_out, m_fox_w_out, v_fox_w_out),
           (dil_w_in, m_dil_w_in, v_dil_w_in), (dil_w_out, m_dil_w_out, v_dil_w_out),
           (ple_w_up, m_ple_w_up, v_ple_w_up), (ple_w_gate, m_ple_w_gate, v_ple_w_gate)]
    names = ["fox_w_in", "fox_w_out", "dil_w_in", "dil_w_out", "ple_w_up", "ple_w_gate"]
    upd = {}
    for rv, (w, m, v), nm in zip(recv, big, names):
        shp2 = rv.shape[1:]
        res = _adamw(rv, w.reshape(shp2), m.reshape(shp2), v.reshape(shp2), "adamw_" + nm)
        upd[nm] = [a.reshape(w.shape) for a in res]

    loss_row = jnp.where(jnp.arange(LANES) == 0, loss_part, 0.0)
    vec = jnp.concatenate([_rows8(d_fox_norm), _rows8(d_final_norm), _rows8(d_dil_norm), d_b_f, loss_row,
                           jnp.zeros((VEC_ROWS - 26, LANES), F32)], axis=0)

    def small_pack(a_fox_norm, a_final_norm, a_dil_norm, a_b_f):
        return jnp.concatenate([_rows8(a_fox_norm), _rows8(a_final_norm), _pad_rows(a_dil_norm, 8),
                                _pad_rows(a_b_f, 8)], axis=0)

    sg, sd, sm, sv = _small_allreduce_adamw(
        vec, small_pack(fox_norm, final_norm, dil_norm, fox_b_f),
        small_pack(m_fox_norm, m_final_norm, m_dil_norm, m_fox_b_f),
        small_pack(v_fox_norm, v_final_norm, v_dil_norm, v_fox_b_f))

    def small_unpack(a):
        return {"fox_norm": a[0:8].reshape(1, d), "final_norm": a[8:16].reshape(d), "dil_norm": a[16:17],
                "fox_b_f": a[24:25, :FOX_HEADS]}

    loss = sg[25, 0]
    order = ["fox_norm", "fox_w_in", "fox_b_f", "fox_w_out", "dil_norm", "dil_w_in", "dil_w_out", "ple_w_up",
             "ple_w_gate", "final_norm"]
    out = [loss, grad_x[None]]
    for idx, small in enumerate((sg, sd, sm, sv)):
        sp = small_unpack(small)
        out += [sp[nm] if nm in sp else upd[nm][idx] for nm in order]
    return tuple(out)
```

```python
import functools

import jax
import jax.numpy as jnp
from jax import lax
from jax.experimental import pallas as pl
from jax.experimental.pallas import tpu as pltpu

F32 = jnp.float32
BF16 = jnp.bfloat16
SDS = jax.ShapeDtypeStruct

D_MODEL = 1024
N_DEV = 8
LANES = 128
FOX_HEADS = 16
FOX_HEAD_DIM = 64
FOX_PAIRS = FOX_HEADS // 2
DIL_HEADS = 8
DIL_BLOCK = 128
DIL_PATTERN = ((128, 1), (512, 4), (2048, 16))
ALIBI_MAX_EXP = 8.0
RMS_EPS = 1e-6
ADAM_LR, ADAM_B1, ADAM_B2, ADAM_EPS, ADAM_WD, ADAM_STEP = 0.001, 0.9, 0.999, 1e-08, 0.01, 10
VMEM_LIMIT = 48 * 1024 * 1024
NEG_INF = float("-inf")

NN = (((1,), (0,)), ((), ()))
NT = (((1,), (1,)), ((), ()))
TN = (((0,), (0,)), ((), ()))
MESH = pl.DeviceIdType.MESH


def _pcall(body, **kw):
    return pl.pallas_call(body, **kw)


def _params(**kw):
    return pltpu.CompilerParams(vmem_limit_bytes=VMEM_LIMIT, **kw)


def _dot(a, b, dims):
    return lax.dot_general(a, b, dims, preferred_element_type=F32)


def _sigmoid(x):
    return 1.0 / (1.0 + jnp.exp(-x))


def _mm(a_parts, b, mode, name, tiles=(512, 1024, 1024), extras=(), outs=None, epilogue=None, out_dtype=BF16):
    na = len(a_parts)
    if mode == "tn":
        k_part, m = a_parts[0].shape
        n = b.shape[1]
    else:
        m, k_part = a_parts[0].shape
        n = b.shape[1] if mode == "nn" else b.shape[0]
    tm, tn, tk = min(tiles[0], m), min(tiles[1], n), min(tiles[2], k_part)
    kb = k_part // tk
    nk = na * kb
    grid = (m // tm, n // tn, nk)

    in_specs = []
    for s in range(na):
        if mode == "tn":
            in_specs.append(pl.BlockSpec((tk, tm), lambda i, j, k: (k, i)))
        else:
            in_specs.append(pl.BlockSpec((tm, tk), lambda i, j, k, s=s: (i, jnp.clip(k - s * kb, 0, kb - 1))))
    if mode == "nt":
        in_specs.append(pl.BlockSpec((tn, tk), lambda i, j, k: (j, k)))
    else:
        in_specs.append(pl.BlockSpec((tk, tn), lambda i, j, k: (k, j)))
    for _, blk, imap in extras:
        in_specs.append(pl.BlockSpec(blk, imap))
    if outs is None:
        outs = [(SDS((m, n), out_dtype), (tm, tn), lambda i, j, k: (i, j))]
    out_specs = [pl.BlockSpec(blk, imap) for _, blk, imap in outs]
    ne, no = len(extras), len(outs)
    dims = {"nn": NN, "nt": NT, "tn": TN}[mode]

    def finish(res, e_refs, o_refs, i):
        if epilogue is None:
            o_refs[0][...] = res.astype(o_refs[0].dtype)
        else:
            epilogue(res, e_refs, o_refs, i)

    def body(*refs):
        a_refs = refs[:na]
        b_ref = refs[na]
        e_refs = refs[na + 1:na + 1 + ne]
        o_refs = refs[na + 1 + ne:na + 1 + ne + no]
        i, k = pl.program_id(0), pl.program_id(2)
        if nk == 1:
            finish(_dot(a_refs[0][...].astype(BF16), b_ref[...].astype(BF16), dims), e_refs, o_refs, i)
            return
        acc = refs[-1]

        @pl.when(k == 0)
        def _():
            acc[...] = jnp.zeros_like(acc)

        def step(a_ref):
            acc[...] += _dot(a_ref[...].astype(BF16), b_ref[...].astype(BF16), dims)

        for s in range(na):
            if na == 1:
                step(a_refs[0])
            else:
                pl.when((k >= s * kb) & (k < (s + 1) * kb))(functools.partial(step, a_refs[s]))

        @pl.when(k == nk - 1)
        def _():
            finish(acc[...], e_refs, o_refs, i)

    res = _pcall(
        body, name=name, grid=grid, in_specs=in_specs, out_specs=out_specs,
        out_shape=[o[0] for o in outs], scratch_shapes=[] if nk == 1 else [pltpu.VMEM((tm, tn), F32)],
        compiler_params=_params(dimension_semantics=("arbitrary", "arbitrary", "arbitrary")),
    )(*a_parts, b, *[e[0] for e in extras])
    return res[0] if len(res) == 1 else res


IN_PROJ_TILES = (1024, 1024, 1024)
DW_TILES = (1024, 1024, 512)


def _dw(x, dy, name):
    return _mm([x], dy, "tn", name, tiles=DW_TILES)


def _add_extra_epilogue(acc, e_refs, o_refs, i):
    o_refs[0][...] = acc + e_refs[0][...]


def _mm_residual(a, b, mode, res, name):
    m = a.shape[0]
    n = b.shape[1] if mode == "nn" else b.shape[0]
    tm, tn = 512, 1024
    return _mm([a], b, mode, name, tiles=(tm, tn, 1024),
               extras=[(res, (tm, tn), lambda i, j, k: (i, j))],
               outs=[(SDS((m, n), F32), (tm, tn), lambda i, j, k: (i, j))],
               epilogue=_add_extra_epilogue)


def _rms_fwd(h, g, name):
    t, d = h.shape
    tm = 512

    def body(h_ref, g_ref, n_ref, r_ref):
        x = h_ref[...]
        r = lax.rsqrt(jnp.mean(x * x, axis=-1, keepdims=True) + RMS_EPS)
        n_ref[...] = ((x * r) * g_ref[...]).astype(BF16)
        r_ref[...] = r

    return _pcall(
        body, name=name, grid=(t // tm,),
        in_specs=[pl.BlockSpec((tm, d), lambda i: (i, 0)), pl.BlockSpec((1, d), lambda i: (0, 0))],
        out_specs=[pl.BlockSpec((tm, d), lambda i: (i, 0)), pl.BlockSpec((tm, 1), lambda i: (i, 0))],
        out_shape=[SDS((t, d), BF16), SDS((t, 1), F32)],
        compiler_params=_params(),
    )(h, g)


def _rms_bwd_rows(dn, x, g, r):
    xhat = x * r
    dxhat = dn * g
    dx = r * (dxhat - xhat * jnp.mean(dxhat * xhat, axis=-1, keepdims=True))
    dg = jnp.sum(dn * xhat, axis=0, keepdims=True)
    return dx, dg


def _mm_in_bwd(d_parts, w, h, g, r, dres, name, more=None):
    t = h.shape[0]
    tm = 512
    tk = 1024 if len(d_parts) <= 4 else 512
    row = lambda i, j, k: (i, 0)
    extras = [(h, (tm, D_MODEL), row), (g, (1, D_MODEL), lambda i, j, k: (0, 0)), (r, (tm, 1), row),
              (dres, (tm, D_MODEL), row)]
    if more is not None:
        extras.append((more, (tm, D_MODEL), row))

    def epilogue(acc, e_refs, o_refs, i):
        dn = acc if more is None else acc + e_refs[4][...]
        dx, dg = _rms_bwd_rows(dn, e_refs[0][...], e_refs[1][...], e_refs[2][...])
        o_refs[0][...] = e_refs[3][...] + dx

        @pl.when(i == 0)
        def _():
            o_refs[1][...] = dg

        @pl.when(i > 0)
        def _():
            o_refs[1][...] += dg

    return _mm(d_parts, w, "nt", name, tiles=(tm, D_MODEL, tk), extras=extras,
               outs=[(SDS((t, D_MODEL), F32), (tm, D_MODEL), row),
                     (SDS((1, D_MODEL), F32), (1, D_MODEL), lambda i, j, k: (0, 0))],
               epilogue=epilogue)


def _final_bwd(h, g, tgt):
    t, d = h.shape
    tm = 256

    def body(h_ref, g_ref, t_ref, dh_ref, dg_ref, loss_ref):
        i = pl.program_id(0)
        x = h_ref[...]
        gg = g_ref[...]
        r = lax.rsqrt(jnp.mean(x * x, axis=-1, keepdims=True) + RMS_EPS)
        err = (x * r) * gg - t_ref[...]
        part = 0.5 * jnp.sum(jnp.mean(err * err, axis=-1, keepdims=True), axis=0, keepdims=True)
        dx, dg = _rms_bwd_rows(err * (1.0 / d), x, gg, r)
        dh_ref[...] = dx

        @pl.when(i == 0)
        def _():
            dg_ref[...] = dg
            loss_ref[...] = jnp.broadcast_to(part, loss_ref.shape)

        @pl.when(i > 0)
        def _():
            dg_ref[...] += dg
            loss_ref[...] += jnp.broadcast_to(part, loss_ref.shape)

    return _pcall(
        body, name="final_norm_loss", grid=(t // tm,),
        in_specs=[pl.BlockSpec((tm, d), lambda i: (i, 0)), pl.BlockSpec((1, d), lambda i: (0, 0)),
                  pl.BlockSpec((tm, d), lambda i: (i, 0))],
        out_specs=[pl.BlockSpec((tm, d), lambda i: (i, 0)), pl.BlockSpec((1, d), lambda i: (0, 0)),
                   pl.BlockSpec((1, LANES), lambda i: (0, 0))],
        out_shape=[SDS((t, d), F32), SDS((1, d), F32), SDS((1, LANES), F32)],
        compiler_params=_params(),
    )(h, g, tgt)


GATE_ROWS = 256


def _split3(x):
    hi = x.astype(BF16)
    r1 = x - hi.astype(F32)
    mid = r1.astype(BF16)
    lo = (r1 - mid.astype(F32)).astype(BF16)
    return hi, mid, lo


def _tri_sum(x, upper):
    rows = x.shape[0]
    ri = lax.broadcasted_iota(jnp.int32, (rows, rows), 0)
    ci = lax.broadcasted_iota(jnp.int32, (rows, rows), 1)
    tri = jnp.where((ri <= ci) if upper else (ri >= ci), 1.0, 0.0).astype(BF16)
    hi, mid, lo = _split3(x)
    return _dot(tri, hi, NN) + _dot(tri, mid, NN) + _dot(tri, lo, NN)


def _log_sigmoid(x):
    return jnp.minimum(x, 0.0) - jnp.log1p(jnp.exp(-jnp.abs(x)))


def _fox_gate_fwd(projf, bpad):
    t = projf.shape[0]
    tb = GATE_ROWS

    def body(x_ref, b_ref, c_ref, carry):
        i = pl.program_id(0)

        @pl.when(i == 0)
        def _():
            carry[...] = jnp.zeros_like(carry)

        c_ref[...] = _tri_sum(_log_sigmoid(x_ref[...] + b_ref[...]), upper=False) + carry[...]
        carry[...] = c_ref[pl.ds(tb - 1, 1), :]

    return _pcall(
        body, name="fox_gate_fwd", grid=(t // tb,),
        in_specs=[pl.BlockSpec((tb, LANES), lambda i: (i, 0)), pl.BlockSpec((1, LANES), lambda i: (0, 0))],
        out_specs=pl.BlockSpec((tb, LANES), lambda i: (i, 0)),
        out_shape=SDS((t, LANES), F32), scratch_shapes=[pltpu.VMEM((1, LANES), F32)],
        compiler_params=_params(),
    )(projf, bpad)


def _fox_gate_bwd(projf, bpad, dc_query, dc_key):
    t = projf.shape[0]
    tb = GATE_ROWS
    nb = t // tb

    def body(x_ref, b_ref, dcq_ref, dck_ref, df_ref, db_ref, carry, buf):
        i = pl.program_id(0)

        @pl.when(i == 0)
        def _():
            carry[...] = jnp.zeros_like(carry)

        buf[...] = _tri_sum(dcq_ref[...] - dck_ref[...], upper=True) + carry[...]
        carry[...] = buf[pl.ds(0, 1), :]
        df = buf[...] * _sigmoid(-(x_ref[...] + b_ref[...]))
        df_ref[...] = df.astype(BF16)
        part = jnp.sum(df, axis=0, keepdims=True)

        @pl.when(i == 0)
        def _():
            db_ref[...] = part

        @pl.when(i > 0)
        def _():
            db_ref[...] += part

    rev = lambda i: (nb - 1 - i, 0)
    return _pcall(
        body, name="fox_gate_bwd", grid=(nb,),
        in_specs=[pl.BlockSpec((tb, LANES), rev), pl.BlockSpec((1, LANES), lambda i: (0, 0)),
                  pl.BlockSpec((tb, LANES), rev), pl.BlockSpec((tb, LANES), rev)],
        out_specs=[pl.BlockSpec((tb, LANES), rev), pl.BlockSpec((1, LANES), lambda i: (0, 0))],
        out_shape=[SDS((t, LANES), BF16), SDS((1, LANES), F32)],
        scratch_shapes=[pltpu.VMEM((1, LANES), F32), pltpu.VMEM((tb, LANES), F32)],
        compiler_params=_params(),
    )(projf, bpad, dc_query, dc_key)


FOX_TQ = 512
FOX_SCALE = FOX_HEAD_DIM ** -0.5


def _low_lanes(shape):
    return lax.broadcasted_iota(jnp.int32, shape, len(shape) - 1) < FOX_HEAD_DIM


FOX_AUG = 3


def _top_rows(shape):
    return lax.broadcasted_iota(jnp.int32, shape, 0) < FOX_HEAD_DIM


def _fox_aug(a, b, sign_a, sign_b, piece_entry, ones_entry, name):
    t = a.shape[0]
    tb = 512

    def body(a_ref, b_ref, o_ref):
        x = sign_a * a_ref[...]
        if sign_b != 0.0:
            x = x + sign_b * b_ref[...]
        head = lax.broadcasted_iota(jnp.int32, (LANES, D_MODEL), 0)
        col = lax.broadcasted_iota(jnp.int32, (LANES, D_MODEL), 1)
        base = (head // 2) * LANES + (1 - head % 2) * FOX_HEAD_DIM + piece_entry
        acc = jnp.zeros((tb, D_MODEL), F32)
        for e, piece in enumerate(_split3(x)):
            place = jnp.where((head < FOX_HEADS) & (col == base + e), 1.0, 0.0).astype(BF16)
            acc = acc + _dot(piece, place, NN)
        if ones_entry is not None:
            ent = lax.broadcasted_iota(jnp.int32, (1, D_MODEL), 1) % FOX_HEAD_DIM
            acc = acc + jnp.where((ent >= ones_entry) & (ent < ones_entry + FOX_AUG), 1.0, 0.0)
        o_ref[...] = acc.astype(BF16)

    blk = pl.BlockSpec((tb, LANES), lambda i: (i, 0))
    return _pcall(
        body, name=name, grid=(t // tb,), in_specs=[blk, blk],
        out_specs=pl.BlockSpec((tb, D_MODEL), lambda i: (i, 0)), out_shape=SDS((t, D_MODEL), BF16),
        compiler_params=_params(),
    )(a, b)


def _fox_unpack_dc(dck_wide, dcq):
    t = dck_wide.shape[0]
    dck = dck_wide.reshape(t, FOX_PAIRS, 2, FOX_HEAD_DIM)[:, :, ::-1, 0].reshape(t, FOX_HEADS)
    return dcq[:, :, 0, :].reshape(FOX_HEADS, t).T, dck


def _causal_steps(nq, key_major):
    if key_major:
        pairs = [(i, j) for j in range(nq) for i in range(j, nq)]
    else:
        pairs = [(i, j) for i in range(nq) for j in range(i + 1)]
    return (jnp.asarray([p[0] for p in pairs], jnp.int32), jnp.asarray([p[1] for p in pairs], jnp.int32))


def _pair_operand(low, own, other, hh):
    return jnp.where(low, own, other) if hh == 0 else jnp.where(low, other, own)


def _fox_fwd(proj, qaug, kaug):
    t = proj.shape[0]
    tq = tk = min(FOX_TQ, t)
    nq = t // tq
    cb = D_MODEL // LANES

    i_tab, j_tab = _causal_steps(nq, key_major=False)

    def body(i_ref, j_ref, q_ref, k_ref, v_ref, z_ref, qa_ref, ka_ref, o_ref, g_ref, lse_ref, m_s, l_s, acc_s):
        step = pl.program_id(1)
        i, j = i_ref[step], j_ref[step]

        @pl.when(j == 0)
        def _():
            m_s[...] = jnp.full_like(m_s, NEG_INF)
            l_s[...] = jnp.zeros_like(l_s)
            acc_s[...] = jnp.zeros_like(acc_s)

        low = _low_lanes((tq, LANES))
        top = _top_rows((LANES, tq))

        def update(masked):
            qs = q_ref[...] * FOX_SCALE
            qa, k, ka, v = qa_ref[...], k_ref[...], ka_ref[...], v_ref[...]
            if masked:
                causal = (lax.broadcasted_iota(jnp.int32, (tk, tq), 0) <= lax.broadcasted_iota(jnp.int32, (tk, tq), 1))
            upd = []
            for hh in range(2):
                s = _dot(_pair_operand(low, k, ka, hh), _pair_operand(low, qs, qa, hh), NT)
                if masked:
                    s = jnp.where(causal, s, NEG_INF)
                m_prev = m_s[hh]
                m_new = jnp.maximum(m_prev, jnp.max(s, axis=0, keepdims=True))
                alpha = jnp.exp(m_prev - m_new)
                p = jnp.exp(s - m_new)
                l_s[hh] = alpha * l_s[hh] + jnp.sum(p, axis=0, keepdims=True)
                m_s[hh] = m_new
                upd.append((alpha, _dot(v, p.astype(BF16), TN)))
            acc = acc_s[...]
            acc_s[...] = jnp.where(top, acc * upd[0][0] + upd[0][1], acc * upd[1][0] + upd[1][1])

        pl.when(j < i)(functools.partial(update, False))
        pl.when(j == i)(functools.partial(update, True))

        @pl.when(j == i)
        def _():
            o = (acc_s[...] / jnp.where(top, l_s[0], l_s[1])).T
            z = z_ref[...].astype(F32)
            o_ref[...] = o.astype(BF16)
            g_ref[...] = (o * (z * _sigmoid(z))).astype(BF16)
            for hh in range(2):
                lse_ref[hh] = m_s[hh] + jnp.log(l_s[hh])

    qblk = lambda col: pl.BlockSpec((tq, LANES), lambda h, s, it, jt: (it[s], col + h))
    kblk = lambda col: pl.BlockSpec((tk, LANES), lambda h, s, it, jt: (jt[s], col + h))
    return _pcall(
        body, name="fox_attn_fwd",
        grid_spec=pltpu.PrefetchScalarGridSpec(
            num_scalar_prefetch=2, grid=(FOX_PAIRS, i_tab.shape[0]),
            in_specs=[qblk(0), kblk(cb), kblk(2 * cb), qblk(3 * cb), qblk(0), kblk(0)],
            out_specs=[qblk(0), qblk(0), pl.BlockSpec((2, 1, tq), lambda h, s, it, jt: (h, 0, it[s]))],
            scratch_shapes=[pltpu.VMEM((2, 1, tq), F32), pltpu.VMEM((2, 1, tq), F32), pltpu.VMEM((LANES, tq), F32)]),
        out_shape=[SDS((t, D_MODEL), BF16), SDS((t, D_MODEL), BF16), SDS((FOX_HEADS, 1, t), F32)],
        compiler_params=_params(dimension_semantics=("arbitrary", "arbitrary")),
    )(i_tab, j_tab, proj, proj, proj, proj, qaug, kaug)


FOX_SUM_ROWS = 8


def _fox_bwd(proj, do, qaug, kaug, doaug):
    t = proj.shape[0]
    tq = tk = min(FOX_TQ, t)
    nq = t // tq
    cb = D_MODEL // LANES

    i_tab, j_tab = _causal_steps(nq, key_major=True)

    def body(i_ref, j_ref, q_ref, k_ref, v_ref, do_ref, qa_ref, ka_ref, da_ref,
             dq_ref, dk_ref, dv_ref, dck_ref, dcq_ref, dq_acc, dcq_acc, dk_acc, dks_acc, dv_acc):
        step = pl.program_id(1)
        i, j = i_ref[step], j_ref[step]
        low = _low_lanes((tq, LANES))
        top = _top_rows((LANES, tq))

        @pl.when(i == j)
        def _():
            dk_acc[...] = jnp.zeros_like(dk_acc)
            dks_acc[...] = jnp.zeros_like(dks_acc)
            dv_acc[...] = jnp.zeros_like(dv_acc)

        def update(masked):
            qs = q_ref[...] * FOX_SCALE
            k, v, dout = k_ref[...], v_ref[...], do_ref[...]
            qa, ka, da = qa_ref[...], ka_ref[...], da_ref[...]
            lane = lax.broadcasted_iota(jnp.int32, (tk, LANES), 1)
            vone = jnp.where((lane & (FOX_HEAD_DIM - 1)) < FOX_AUG, 1.0, 0.0).astype(v.dtype)
            one = jnp.ones_like(k)
            if masked:
                causal = (lax.broadcasted_iota(jnp.int32, (tk, tq), 0) <= lax.broadcasted_iota(jnp.int32, (tk, tq), 1))
            parts = []
            for hh in range(2):
                s = _dot(_pair_operand(low, k, ka, hh), _pair_operand(low, qs, qa, hh), NT)
                if masked:
                    s = jnp.where(causal, s, NEG_INF)
                p = jnp.exp(s)
                ds = p * _dot(_pair_operand(low, v, vone, hh), _pair_operand(low, dout, da, hh), NT)
                pb = p.astype(BF16)
                dsb = ds.astype(BF16)
                parts.append((_dot(pb, dout, NN),
                              _dot(dsb, _pair_operand(low, qs, one, hh), NN),
                              _dot(_pair_operand(low, k, one, hh), dsb, TN)))
            dv_acc[...] += jnp.where(low, parts[0][0], parts[1][0])
            dk_acc[...] += jnp.where(low, parts[0][1], parts[1][1])
            dks_acc[...] += jnp.where(low, parts[1][1], parts[0][1])
            dq_t = jnp.where(top, parts[0][2], parts[1][2]) * FOX_SCALE
            sum_a = parts[0][2][FOX_HEAD_DIM:FOX_HEAD_DIM + FOX_SUM_ROWS, :]
            sum_b = parts[1][2][0:FOX_SUM_ROWS, :]

            @pl.when(j == 0)
            def _():
                dq_acc[i] = dq_t
                dcq_acc[0, i] = sum_a
                dcq_acc[1, i] = sum_b

            @pl.when(j > 0)
            def _():
                dq_acc[i] += dq_t
                dcq_acc[0, i] += sum_a
                dcq_acc[1, i] += sum_b

        pl.when(i > j)(functools.partial(update, False))
        pl.when(i == j)(functools.partial(update, True))

        @pl.when(i == nq - 1)
        def _():
            dk_ref[...] = dk_acc[...].astype(BF16)
            dv_ref[...] = dv_acc[...].astype(BF16)
            dck_ref[...] = dks_acc[...]

        @pl.when((i == nq - 1) & (j == nq - 1))
        def _():
            for blk in range(nq):
                dq_ref[blk * tq:(blk + 1) * tq, :] = dq_acc[blk].T.astype(BF16)
            dcq_ref[...] = dcq_acc[...]

    qblk = lambda col: pl.BlockSpec((tq, LANES), lambda h, s, it, jt: (it[s], col + h))
    kblk = lambda col: pl.BlockSpec((tk, LANES), lambda h, s, it, jt: (jt[s], col + h))
    return _pcall(
        body, name="fox_attn_bwd",
        grid_spec=pltpu.PrefetchScalarGridSpec(
            num_scalar_prefetch=2, grid=(FOX_PAIRS, i_tab.shape[0]),
            in_specs=[qblk(0), kblk(cb), kblk(2 * cb), qblk(0), qblk(0), kblk(0), qblk(0)],
            out_specs=[pl.BlockSpec((t, LANES), lambda h, s, it, jt: (0, h)), kblk(0), kblk(0), kblk(0),
                       pl.BlockSpec((2, nq, FOX_SUM_ROWS, tq), lambda h, s, it, jt: (h, 0, 0, 0))],
            scratch_shapes=[pltpu.VMEM((nq, LANES, tq), F32), pltpu.VMEM((2, nq, FOX_SUM_ROWS, tq), F32),
                            pltpu.VMEM((tk, LANES), F32), pltpu.VMEM((tk, LANES), F32), pltpu.VMEM((tk, LANES), F32)]),
        out_shape=[SDS((t, D_MODEL), BF16), SDS((t, D_MODEL), BF16), SDS((t, D_MODEL), BF16),
                   SDS((t, D_MODEL), F32), SDS((FOX_HEADS, nq, FOX_SUM_ROWS, tq), F32)],
        compiler_params=_params(dimension_semantics=("arbitrary", "arbitrary")),
    )(i_tab, j_tab, proj, proj, proj, do, qaug, kaug, doaug)


def _mm_gate_bwd(dh, w_out, z_src, z_col0, o, heads, name):
    t = dh.shape[0]
    tm = 512
    row = lambda i, j, k: (i, 0)
    zcb = z_col0 // D_MODEL

    def epilogue(acc, e_refs, o_refs, i):
        z = e_refs[0][...].astype(F32)
        ov = e_refs[1][...].astype(F32)
        sg = _sigmoid(z)
        dout = acc * (z * sg)
        o_refs[0][...] = dout.astype(BF16)
        o_refs[1][...] = (acc * ov * (sg * (1.0 + z * (1.0 - sg)))).astype(BF16)
        prod = dout * ov
        for cbk in range(D_MODEL // LANES):
            seg = prod[:, cbk * LANES:(cbk + 1) * LANES]
            tot = jnp.sum(seg, axis=-1, keepdims=True)
            if heads == D_MODEL // LANES:
                o_refs[2][cbk] = tot
            else:
                lo = jnp.sum(jnp.where(_low_lanes(seg.shape), seg, 0.0), axis=-1, keepdims=True)
                o_refs[2][2 * cbk] = lo
                o_refs[2][2 * cbk + 1] = tot - lo

    return _mm([dh], w_out, "nt", name, tiles=(tm, D_MODEL, D_MODEL),
               extras=[(z_src, (tm, D_MODEL), lambda i, j, k: (i, zcb)), (o, (tm, D_MODEL), row)],
               outs=[(SDS((t, D_MODEL), BF16), (tm, D_MODEL), row), (SDS((t, D_MODEL), BF16), (tm, D_MODEL), row),
                     (SDS((heads, t, 1), F32), (heads, tm, 1), lambda i, j, k: (0, i, 0))],
               epilogue=epilogue)


def _ple_fwd(h, pin, w_up, w_gate, name):
    t = h.shape[0]
    tm = 512
    pd = pin.shape[1]

    def body(h_ref, p_ref, wu_ref, wg_ref, hn_ref, u_ref, a_ref):
        h = h_ref[...]
        u = _dot(p_ref[...].astype(BF16), wu_ref[...], NN)
        a = _dot(h.astype(BF16), wg_ref[...], NN)
        hn_ref[...] = h + u * _sigmoid(a)
        u_ref[...] = u.astype(BF16)
        a_ref[...] = a.astype(BF16)

    rows = pl.BlockSpec((tm, D_MODEL), lambda i: (i, 0))
    return _pcall(
        body, name=name, grid=(t // tm,),
        in_specs=[rows, pl.BlockSpec((tm, pd), lambda i: (i, 0)),
                  pl.BlockSpec((pd, D_MODEL), lambda i: (0, 0)), pl.BlockSpec((D_MODEL, D_MODEL), lambda i: (0, 0))],
        out_specs=[rows, rows, rows],
        out_shape=[SDS((t, D_MODEL), F32), SDS((t, D_MODEL), BF16), SDS((t, D_MODEL), BF16)],
        compiler_params=_params(),
    )(h, pin, w_up, w_gate)


def _ple_bwd_elem(dh, u, a, name):
    t = dh.shape[0]
    tm = 512

    def body(dh_ref, u_ref, a_ref, du_ref, da_ref):
        g = dh_ref[...]
        s = _sigmoid(a_ref[...].astype(F32))
        du_ref[...] = (g * s).astype(BF16)
        da_ref[...] = (g * u_ref[...].astype(F32) * (s * (1.0 - s))).astype(BF16)

    blk = pl.BlockSpec((tm, D_MODEL), lambda i: (i, 0))
    return _pcall(
        body, name=name, grid=(t // tm,), in_specs=[blk, blk, blk], out_specs=[blk, blk],
        out_shape=[SDS((t, D_MODEL), BF16), SDS((t, D_MODEL), BF16)], compiler_params=_params(),
    )(dh, u, a)


DIL_SCALE = LANES ** -0.5


def _dil_masks():
    ii = lax.broadcasted_iota(jnp.int32, (DIL_BLOCK, DIL_BLOCK), 0)
    jj = lax.broadcasted_iota(jnp.int32, (DIL_BLOCK, DIL_BLOCK), 1)
    return ii, jj


def _dil_scores(q, kp, kc, slope, has_prev):
    ii, jj = _dil_masks()
    dist_p = (DIL_BLOCK + ii - jj).astype(F32)
    dist_c = (ii - jj).astype(F32)
    sp = _dot(q, kp, NT) * DIL_SCALE - slope * dist_p
    sc = _dot(q, kc, NT) * DIL_SCALE - slope * dist_c
    sp = jnp.where((jj >= ii) & has_prev, sp, NEG_INF)
    sc = jnp.where(jj <= ii, sc, NEG_INF)
    return sp, sc


def _dil_fwd(proj, slopes, grp, dil, name):
    t = proj.shape[0]
    rows = DIL_BLOCK * dil
    nsb = t // rows
    qc, kc_, vc_ = grp * DIL_HEADS, 3 * DIL_HEADS + grp * DIL_HEADS, 6 * DIL_HEADS + grp * DIL_HEADS

    def body(q_ref, kp_ref, kc_ref, vp_ref, vc_ref, sl_ref, o_ref, lse_ref, qf, kpf, kcf, vpf, vcf, of, lf):
        m = pl.program_id(1)
        for src, dst in ((q_ref, qf), (kp_ref, kpf), (kc_ref, kcf), (vp_ref, vpf), (vc_ref, vcf)):
            dst[...] = src[...].astype(F32)
        slope = sl_ref[0]
        has_prev = m > 0
        for r in range(dil):
            ph = pl.ds(r, DIL_BLOCK, stride=dil)
            q = qf[ph, :].astype(BF16)
            sp, sc = _dil_scores(q, kpf[ph, :].astype(BF16), kcf[ph, :].astype(BF16), slope, has_prev)
            mx = jnp.maximum(jnp.max(sp, axis=-1, keepdims=True), jnp.max(sc, axis=-1, keepdims=True))
            pp = jnp.exp(sp - mx)
            pc = jnp.exp(sc - mx)
            l = jnp.sum(pp, axis=-1, keepdims=True) + jnp.sum(pc, axis=-1, keepdims=True)
            o = _dot(pp.astype(BF16), vpf[ph, :].astype(BF16), NN) + _dot(pc.astype(BF16), vcf[ph, :].astype(BF16), NN)
            of[ph, :] = o / l
            lf[ph, :] = mx + jnp.log(l)
        o_ref[...] = of[...].astype(BF16)
        lse_ref[0] = lf[...]

    prev = lambda h, m: jnp.maximum(m - 1, 0)
    blk = lambda col, rowmap: pl.BlockSpec((rows, LANES), lambda h, m: (rowmap(h, m), col + h))
    cur = lambda h, m: m
    return _pcall(
        body, name=name, grid=(DIL_HEADS, nsb),
        in_specs=[blk(qc, cur), blk(kc_, prev), blk(kc_, cur), blk(vc_, prev), blk(vc_, cur),
                  pl.BlockSpec((1, 1, 1), lambda h, m: (h, 0, 0))],
        out_specs=[pl.BlockSpec((rows, LANES), lambda h, m: (m, h)), pl.BlockSpec((1, rows, 1), lambda h, m: (h, m, 0))],
        out_shape=[SDS((t, D_MODEL), BF16), SDS((DIL_HEADS, t, 1), F32)],
        scratch_shapes=[pltpu.VMEM((rows, LANES), F32)] * 6 + [pltpu.VMEM((rows, 1), F32)],
        compiler_params=_params(),
    )(proj, proj, proj, proj, proj, slopes)


def _dil_mix(outs, lses, proj, z_col0):
    t = proj.shape[0]
    tm = 512
    zcb = z_col0 // LANES
    ng = len(outs)

    def body(*refs):
        o_refs, l_refs, z_ref = refs[:ng], refs[ng:2 * ng], refs[2 * ng]
        om_ref, g_ref, lse_ref = refs[2 * ng + 1:]
        ls = [r[0] for r in l_refs]
        mx = functools.reduce(jnp.maximum, ls)
        es = [jnp.exp(l - mx) for l in ls]
        tot = functools.reduce(jnp.add, es)
        o = functools.reduce(jnp.add, [(e / tot) * r[...].astype(F32) for e, r in zip(es, o_refs)])
        z = z_ref[...].astype(F32)
        om_ref[...] = o.astype(BF16)
        g_ref[...] = (o * (z * _sigmoid(z))).astype(BF16)
        lse_ref[0] = mx + jnp.log(tot)

    tile = pl.BlockSpec((tm, LANES), lambda i, h: (i, h))
    col = pl.BlockSpec((1, tm, 1), lambda i, h: (h, i, 0))
    return _pcall(
        body, name="dil_mix", grid=(t // tm, DIL_HEADS),
        in_specs=[tile] * ng + [col] * ng + [pl.BlockSpec((tm, LANES), lambda i, h: (i, zcb + h))],
        out_specs=[tile, tile, col],
        out_shape=[SDS((t, D_MODEL), BF16), SDS((t, D_MODEL), BF16), SDS((DIL_HEADS, t, 1), F32)],
        compiler_params=_params(),
    )(*outs, *lses, proj)


def _dil_bwd(proj, do, lse, delta, slopes, grp, dil, name):
    t = proj.shape[0]
    rows = DIL_BLOCK * dil
    nsb = t // rows
    qc, kc_, vc_ = grp * DIL_HEADS, 3 * DIL_HEADS + grp * DIL_HEADS, 6 * DIL_HEADS + grp * DIL_HEADS

    def body(q_ref, qn_ref, kp_ref, kc_ref, vp_ref, vc_ref, do_ref, don_ref, l_ref, ln_ref, d_ref, dn_ref, sl_ref,
             dq_ref, dk_ref, dv_ref, qf, qnf, kpf, kcf, vpf, vcf, dof, donf, dqf, dkf, dvf):
        m = pl.program_id(1)
        for src, dst in ((q_ref, qf), (qn_ref, qnf), (kp_ref, kpf), (kc_ref, kcf), (vp_ref, vpf), (vc_ref, vcf),
                         (do_ref, dof), (don_ref, donf)):
            dst[...] = src[...].astype(F32)
        slope = sl_ref[0]
        has_prev = m > 0
        has_next = m < nsb - 1
        ii, jj = _dil_masks()
        for r in range(dil):
            ph = pl.ds(r, DIL_BLOCK, stride=dil)
            q, kp, kc, vp, vc, dout = (x[ph, :].astype(BF16) for x in (qf, kpf, kcf, vpf, vcf, dof))
            sp, sc = _dil_scores(q, kp, kc, slope, has_prev)
            lrow = l_ref[0, ph, :]
            drow = d_ref[0, ph, :]
            pp = jnp.exp(sp - lrow)
            pc = jnp.exp(sc - lrow)
            dsp = pp * (_dot(dout, vp, NT) - drow)
            dsc = pc * (_dot(dout, vc, NT) - drow)
            dspb, dscb = dsp.astype(BF16), dsc.astype(BF16)
            dqf[ph, :] = (_dot(dspb, kp, NN) + _dot(dscb, kc, NN)) * DIL_SCALE
            qn = qnf[ph, :].astype(BF16)
            don = donf[ph, :].astype(BF16)
            sn = _dot(qn, kc, NT) * DIL_SCALE - slope * (DIL_BLOCK + ii - jj).astype(F32)
            pn = jnp.exp(jnp.where((jj >= ii) & has_next, sn, NEG_INF) - ln_ref[0, ph, :])
            dsn = (pn * (_dot(don, vc, NT) - dn_ref[0, ph, :])).astype(BF16)
            dkf[ph, :] = (_dot(dscb, q, TN) + _dot(dsn, qn, TN)) * DIL_SCALE
            dvf[ph, :] = _dot(pc.astype(BF16), dout, TN) + _dot(pn.astype(BF16), don, TN)
        dq_ref[...] = dqf[...].astype(BF16)
        dk_ref[...] = dkf[...].astype(BF16)
        dv_ref[...] = dvf[...].astype(BF16)

    prev = lambda h, m: jnp.maximum(m - 1, 0)
    cur = lambda h, m: m
    nxt = lambda h, m: jnp.minimum(m + 1, nsb - 1)
    blk = lambda col, rowmap: pl.BlockSpec((rows, LANES), lambda h, m: (rowmap(h, m), col + h))
    colblk = lambda rowmap: pl.BlockSpec((1, rows, 1), lambda h, m: (h, rowmap(h, m), 0))
    out_blk = pl.BlockSpec((rows, LANES), lambda h, m: (m, h))
    return _pcall(
        body, name=name, grid=(DIL_HEADS, nsb),
        in_specs=[blk(qc, cur), blk(qc, nxt), blk(kc_, prev), blk(kc_, cur), blk(vc_, prev), blk(vc_, cur),
                  blk(0, cur), blk(0, nxt), colblk(cur), colblk(nxt), colblk(cur), colblk(nxt),
                  pl.BlockSpec((1, 1, 1), lambda h, m: (h, 0, 0))],
        out_specs=[out_blk, out_blk, out_blk],
        out_shape=[SDS((t, D_MODEL), BF16)] * 3,
        scratch_shapes=[pltpu.VMEM((rows, LANES), F32)] * 11,
        compiler_params=_params(),
    )(proj, proj, proj, proj, proj, proj, do, do, lse, lse, delta, delta, slopes)


def _mesh_pos():
    x, y, c = lax.axis_index("x"), lax.axis_index("y"), lax.axis_index("c")
    return x, y, c


def _peer(pos, k):
    x, y, c = pos
    px = 1 - x if k & 4 else x
    py = 1 - y if k & 2 else y
    pc = 1 - c if k & 1 else c
    return (px, py, pc), 4 * px + 2 * py + pc


N_CHIPS = 4
CHIP_FLIPS = ((1, 0), (0, 1), (1, 1))


def _other_chips(x, y):
    return [(1 - x if fx else x, 1 - y if fy else y) for fx, fy in CHIP_FLIPS]


def _all_gather(arrays):
    n = len(arrays)
    per = 2 * N_CHIPS - 1
    hbm = pl.BlockSpec(memory_space=pltpu.HBM)

    def body(*refs):
        ins, outs = refs[:n], refs[n:2 * n]
        send_sems, recv_sems, local_sems = refs[2 * n:]
        x, y, c = _mesh_pos()
        sibling = (x, y, 1 - c)
        chips = _other_chips(x, y)
        block = lambda px, py, pc: 4 * px + 2 * py + pc

        def copy(w, k, src, blk, to):
            return pltpu.make_async_remote_copy(
                src_ref=src, dst_ref=outs[w].at[blk], send_sem=send_sems.at[w * per + k],
                recv_sem=recv_sems.at[w * per + k], device_id=to, device_id_type=MESH)

        local, started = [], []
        for w in range(n):
            cp = pltpu.make_async_copy(ins[w], outs[w].at[block(x, y, c)], local_sems.at[w])
            cp.start()
            local.append(cp)
            started.append(copy(w, 0, ins[w], block(x, y, c), sibling))
            for j, (px, py) in enumerate(chips):
                started.append(copy(w, 1 + j, ins[w], block(x, y, c), (px, py, c)))
        for cp in started:
            cp.start()
        for j, (px, py) in enumerate(chips):
            for w in range(n):
                copy(w, 1 + j, ins[w], block(px, py, c), sibling).wait_recv()
                cp = copy(w, 4 + j, outs[w].at[block(px, py, c)], block(px, py, c), sibling)
                cp.start()
                started.append(cp)
        for w in range(n):
            copy(w, 0, ins[w], block(x, y, 1 - c), sibling).wait_recv()
            for j, (px, py) in enumerate(chips):
                copy(w, 4 + j, ins[w], block(px, py, 1 - c), sibling).wait_recv()
        for cp in started:
            cp.wait_send()
        for cp in local:
            cp.wait()

    return _pcall(
        body, name="all_gather_weights", in_specs=[hbm] * n, out_specs=[hbm] * n,
        out_shape=[SDS((N_DEV,) + a.shape, a.dtype) for a in arrays],
        scratch_shapes=[pltpu.SemaphoreType.DMA((n * per,)), pltpu.SemaphoreType.DMA((n * per,)),
                        pltpu.SemaphoreType.DMA((n,))],
    )(*arrays)


def _core_exchange(slabs):
    n = len(slabs)
    hbm = pl.BlockSpec(memory_space=pltpu.HBM)

    def body(*refs):
        ins, outs = refs[:n], refs[n:2 * n]
        send_sems, recv_sems = refs[2 * n:]
        x, y, c = _mesh_pos()
        copies = [pltpu.make_async_remote_copy(
            src_ref=ins[w].at[pl.ds(0, N_CHIPS), 1 - c], dst_ref=outs[w], send_sem=send_sems.at[w],
            recv_sem=recv_sems.at[w], device_id=(x, y, 1 - c), device_id_type=MESH) for w in range(n)]
        for cp in copies:
            cp.start()
        for cp in copies:
            cp.wait_recv()
        for cp in copies:
            cp.wait_send()

    return _pcall(
        body, name="grads_core_exchange", in_specs=[hbm] * n, out_specs=[hbm] * n,
        out_shape=[SDS((N_CHIPS,) + a.shape[2:], a.dtype) for a in slabs],
        scratch_shapes=[pltpu.SemaphoreType.DMA((n,)), pltpu.SemaphoreType.DMA((n,))],
    )(*slabs)


def _chip_exchange(partials):
    n = len(partials)
    per = N_CHIPS - 1
    hbm = pl.BlockSpec(memory_space=pltpu.HBM)

    def body(*refs):
        ins, outs = refs[:n], refs[n:2 * n]
        send_sems, recv_sems, local_sems = refs[2 * n:]
        x, y, c = _mesh_pos()
        mine = 2 * x + y
        local, sends, recvs = [], [], []
        for w in range(n):
            cp = pltpu.make_async_copy(ins[w].at[mine], outs[w].at[mine], local_sems.at[w])
            cp.start()
            local.append(cp)
            for j, (px, py) in enumerate(_other_chips(x, y)):
                theirs = 2 * px + py
                sems = dict(send_sem=send_sems.at[w * per + j], recv_sem=recv_sems.at[w * per + j],
                            device_id=(px, py, c), device_id_type=MESH)
                cp = pltpu.make_async_remote_copy(src_ref=ins[w].at[theirs], dst_ref=outs[w].at[mine], **sems)
                cp.start()
                sends.append(cp)
                recvs.append(pltpu.make_async_remote_copy(src_ref=ins[w].at[theirs], dst_ref=outs[w].at[theirs], **sems))
        for cp in recvs:
            cp.wait_recv()
        for cp in sends:
            cp.wait_send()
        for cp in local:
            cp.wait()

    return _pcall(
        body, name="grads_chip_exchange", in_specs=[hbm] * n, out_specs=[hbm] * n,
        out_shape=[SDS(a.shape, a.dtype) for a in partials],
        scratch_shapes=[pltpu.SemaphoreType.DMA((n * per,)), pltpu.SemaphoreType.DMA((n * per,)),
                        pltpu.SemaphoreType.DMA((n,))],
    )(*partials)


def _core_sum(slabs, from_sibling, core, name):
    _, _, r, c = slabs.shape
    tr = min(r, 256)

    def body(core_ref, a_ref, b_ref, o_ref):
        o_ref[...] = (a_ref[0].astype(F32) + b_ref[...].astype(F32)).astype(BF16)

    return _pcall(
        body, name=name,
        grid_spec=pltpu.PrefetchScalarGridSpec(
            num_scalar_prefetch=1, grid=(N_CHIPS, r // tr),
            in_specs=[pl.BlockSpec((1, 1, tr, c), lambda g, i, core: (g, core[0], i, 0)),
                      pl.BlockSpec((1, tr, c), lambda g, i, core: (g, i, 0))],
            out_specs=pl.BlockSpec((1, tr, c), lambda g, i, core: (g, i, 0))),
        out_shape=SDS((N_CHIPS, r, c), BF16), compiler_params=_params(),
    )(core, slabs, from_sibling)


def _adam_math(w, g, m, v):
    m = ADAM_B1 * m + (1.0 - ADAM_B1) * g
    v = ADAM_B2 * v + (1.0 - ADAM_B2) * (g * g)
    m_hat = m / (1.0 - ADAM_B1 ** ADAM_STEP)
    v_hat = v / (1.0 - ADAM_B2 ** ADAM_STEP)
    delta = -ADAM_LR * (m_hat / (jnp.sqrt(v_hat) + ADAM_EPS) + ADAM_WD * w)
    return delta, m, v


def _adamw(recv, w, m, v, name):
    r, c = w.shape
    tr = min(r, 128)
    n_parts = recv.shape[0]

    def body(g_ref, w_ref, m_ref, v_ref, go_ref, d_ref, mo_ref, vo_ref):
        g = g_ref[0].astype(F32)
        for s in range(1, n_parts):
            g = g + g_ref[s].astype(F32)
        delta, mn, vn = _adam_math(w_ref[...], g, m_ref[...], v_ref[...])
        go_ref[...] = g
        d_ref[...] = delta
        mo_ref[...] = mn
        vo_ref[...] = vn

    blk = pl.BlockSpec((tr, c), lambda i: (i, 0))
    return _pcall(
        body, name=name, grid=(r // tr,),
        in_specs=[pl.BlockSpec((n_parts, tr, c), lambda i: (0, i, 0)), blk, blk, blk],
        out_specs=[blk] * 4, out_shape=[SDS((r, c), F32)] * 4, compiler_params=_params(),
    )(recv, w, m, v)


VEC_ROWS = 32


def _small_allreduce_adamw(vec, w, m, v):
    def body(vec_ref, w_ref, m_ref, v_ref, g_ref, d_ref, mo_ref, vo_ref, gath, send_sems, recv_sems):
        pos = _mesh_pos()
        me = 4 * pos[0] + 2 * pos[1] + pos[2]
        sends, recvs = [], []
        for k in range(1, N_DEV):
            peer, peer_idx = _peer(pos, k)
            cp = pltpu.make_async_remote_copy(src_ref=vec_ref, dst_ref=gath.at[me], send_sem=send_sems.at[k - 1],
                                              recv_sem=recv_sems.at[k - 1], device_id=peer, device_id_type=MESH)
            cp.start()
            sends.append(cp)
            recvs.append(pltpu.make_async_remote_copy(src_ref=vec_ref, dst_ref=gath.at[peer_idx],
                                                      send_sem=send_sems.at[k - 1], recv_sem=recv_sems.at[k - 1],
                                                      device_id=peer, device_id_type=MESH))
        gath[me] = vec_ref[...]
        for cp in recvs:
            cp.wait_recv()
        for cp in sends:
            cp.wait_send()
        tot = gath[0]
        for s in range(1, N_DEV):
            tot = tot + gath[s]
        rowi = lax.broadcasted_iota(jnp.int32, (8, LANES), 0)
        mine = jnp.sum(jnp.where(rowi == me, tot[16:24, :], 0.0), axis=0, keepdims=True)
        g = jnp.concatenate([tot[0:16, :], jnp.broadcast_to(mine, (8, LANES)), tot[24:32, :]], axis=0)
        delta, mn, vn = _adam_math(w_ref[...], g, m_ref[...], v_ref[...])
        g_ref[...] = g
        d_ref[...] = delta
        mo_ref[...] = mn
        vo_ref[...] = vn

    vm = pl.BlockSpec(memory_space=pltpu.VMEM)
    return _pcall(
        body, name="small_allreduce_adamw", in_specs=[vm] * 4, out_specs=[vm] * 4,
        out_shape=[SDS((VEC_ROWS, LANES), F32)] * 4,
        scratch_shapes=[pltpu.VMEM((N_DEV, VEC_ROWS, LANES), F32), pltpu.SemaphoreType.DMA((N_DEV - 1,)),
                        pltpu.SemaphoreType.DMA((N_DEV - 1,))],
        compiler_params=pltpu.CompilerParams(has_side_effects=True),
    )(vec, w, m, v)


def _cols_to_slabs(a):
    r, c8 = a.shape
    return a.reshape(r, N_DEV, c8 // N_DEV).transpose(1, 0, 2)


def _slabs_to_cols(a):
    n, r, c = a.shape
    return a.transpose(1, 0, 2).reshape(r, n * c)


def _rows8(vec):
    return vec.reshape(-1, LANES)


def _pad_rows(a, rows):
    return jnp.pad(a, ((0, rows - a.shape[0]), (0, LANES - a.shape[1])))


def kernel(x, p, fox_norm, fox_w_in, fox_b_f, fox_w_out, dil_norm, dil_w_in, dil_w_out, ple_w_up, ple_w_gate, final_norm, loss_target, m_fox_norm, m_fox_w_in, m_fox_b_f, m_fox_w_out, m_dil_norm, m_dil_w_in, m_dil_w_out, m_ple_w_up, m_ple_w_gate, m_final_norm, v_fox_norm, v_fox_w_in, v_fox_b_f, v_fox_w_out, v_dil_norm, v_dil_w_in, v_dil_w_out, v_ple_w_up, v_ple_w_gate, v_final_norm):
    t = x.shape[1]
    d = D_MODEL
    xs, tgt = x[0], loss_target[0]
    p0, p1 = p[0, 0], p[1, 0]
    fox_cols = fox_w_in.shape[2]
    ple_dim = ple_w_up.shape[1]

    shards = [fox_w_in[0].astype(BF16), fox_w_out[0].astype(BF16), dil_w_in[0].astype(BF16),
              dil_w_out[0].astype(BF16), ple_w_up.reshape(-1, LANES).astype(BF16),
              ple_w_gate.reshape(-1, d).astype(BF16), dil_norm]
    gw = _all_gather(shards)
    w_fox_in = _slabs_to_cols(gw[0])
    w_fox_main = w_fox_in[:, :4 * d]
    w_fox_f = jnp.pad(w_fox_in[:, 4 * d:], ((0, 0), (0, LANES - FOX_HEADS)))
    w_fox_out = gw[1].reshape(d, d)
    w_dil_in = _slabs_to_cols(gw[2])
    w_dil_out = gw[3].reshape(d, d)
    w_up = gw[4].reshape(N_DEV, 2, ple_dim, LANES).transpose(1, 2, 0, 3).reshape(2, ple_dim, d)
    w_gate = gw[5].reshape(N_DEV, 2, d // N_DEV, d).transpose(1, 0, 2, 3).reshape(2, d, d)
    dil_norm_full = gw[6].reshape(1, d)
    b_pad = jnp.pad(fox_b_f, ((0, 0), (0, LANES - FOX_HEADS)))

    n0, r0 = _rms_fwd(xs, fox_norm, "rms_fox")
    proj0 = _mm([n0], w_fox_main, "nn", "fox_in_proj", tiles=IN_PROJ_TILES)
    projf = _mm([n0], w_fox_f, "nn", "fox_gate_proj", tiles=IN_PROJ_TILES, out_dtype=F32)
    c_all = _fox_gate_fwd(projf, b_pad)
    qaug_fwd = _fox_aug(c_all, c_all, 1.0, 0.0, 0, FOX_AUG, "fox_aug_q_fwd")
    kaug = _fox_aug(c_all, c_all, -1.0, 0.0, FOX_AUG, 0, "fox_aug_k")
    o0, g0, lse0 = _fox_fwd(proj0, qaug_fwd, kaug)
    h1 = _mm_residual(g0, w_fox_out, "nn", xs, "fox_out_proj")
    h2, u0, a0 = _ple_fwd(h1, p0, w_up[0], w_gate[0], "ple0_fwd")

    n1, r1 = _rms_fwd(h2, dil_norm_full, "rms_dil")
    proj1 = _mm([n1], w_dil_in, "nn", "dil_in_proj", tiles=IN_PROJ_TILES)
    n_heads = len(DIL_PATTERN) * DIL_HEADS
    slopes = 2.0 ** (-ALIBI_MAX_EXP * jnp.arange(1, n_heads + 1, dtype=F32) / n_heads)
    dil_o, dil_lse, dil_slopes = [], [], []
    for grp, (_, dil) in enumerate(DIL_PATTERN):
        sl = (slopes[grp * DIL_HEADS:(grp + 1) * DIL_HEADS] * dil).reshape(DIL_HEADS, 1, 1)
        og, lg = _dil_fwd(proj1, sl, grp, dil, f"dil_attn_fwd_{grp}")
        dil_o.append(og)
        dil_lse.append(lg)
        dil_slopes.append(sl)
    z1_col0 = 9 * d
    o1, g1, lse1 = _dil_mix(dil_o, dil_lse, proj1, z1_col0)
    h3 = _mm_residual(g1, w_dil_out, "nn", h2, "dil_out_proj")
    h4, u1, a1 = _ple_fwd(h3, p1, w_up[1], w_gate[1], "ple1_fwd")

    dh4, d_final_norm, loss_part = _final_bwd(h4, final_norm.reshape(1, d), tgt)

    du1, da1 = _ple_bwd_elem(dh4, u1, a1, "ple1_bwd_elem")
    dw_up1 = _dw(p1, du1, "ple1_dw_up")
    dw_gate1 = _dw(h3, da1, "ple1_dw_gate")
    dh3 = _mm_residual(da1, w_gate[1], "nt", dh4, "ple1_dh")

    dw_dil_out = _dw(g1, dh3, "dil_dw_out")
    do1, dz1, delta1 = _mm_gate_bwd(dh3, w_dil_out, proj1, z1_col0, o1, DIL_HEADS, "dil_dgate")
    dqs, dks, dvs = [], [], []
    for grp, (_, dil) in enumerate(DIL_PATTERN):
        dq, dk, dv = _dil_bwd(proj1, do1, lse1, delta1, dil_slopes[grp], grp, dil, f"dil_attn_bwd_{grp}")
        dqs.append(dq)
        dks.append(dk)
        dvs.append(dv)
    dproj1 = dqs + dks + dvs + [dz1]
    dw_dil_in = jnp.concatenate([_dw(n1, dpart, f"dil_dw_in_{s}") for s, dpart in enumerate(dproj1)], axis=1)
    dh2, d_dil_norm = _mm_in_bwd(dproj1, w_dil_in, h2, dil_norm_full, r1, dh3, "dil_dx")

    du0, da0 = _ple_bwd_elem(dh2, u0, a0, "ple0_bwd_elem")
    dw_up0 = _dw(p0, du0, "ple0_dw_up")
    dw_gate0 = _dw(h1, da0, "ple0_dw_gate")
    dh1 = _mm_residual(da0, w_gate[0], "nt", dh2, "ple0_dh")

    dw_fox_out = _dw(g0, dh1, "fox_dw_out")
    do0, dz0, delta0 = _mm_gate_bwd(dh1, w_fox_out, proj0, 3 * d, o0, FOX_HEADS, "fox_dgate")
    head_cols = lambda a: jnp.pad(a, ((0, 0), (0, LANES - FOX_HEADS)))
    lse_cols = head_cols(lse0.reshape(FOX_HEADS, t).T)
    delta_cols = head_cols(delta0.reshape(FOX_HEADS, t).T)
    qaug_bwd = _fox_aug(c_all, lse_cols, 1.0, -1.0, 0, FOX_AUG, "fox_aug_q_bwd")
    doaug = _fox_aug(delta_cols, delta_cols, -1.0, 0.0, 0, None, "fox_aug_do")
    dq0, dk0, dv0, dck_wide, dcq = _fox_bwd(proj0, do0, qaug_bwd, kaug, doaug)
    dc_query, dc_key = _fox_unpack_dc(dck_wide, dcq)
    df, d_b_f = _fox_gate_bwd(projf, b_pad, head_cols(dc_query), head_cols(dc_key))
    dproj0 = [dq0, dk0, dv0, dz0]
    dw_fox_parts = [_dw(n0, dpart, f"fox_dw_in_{s}") for s, dpart in enumerate(dproj0)]
    dw_fox_f = _dw(n0, df, "fox_dw_gate")
    dn0_f = _mm([df], w_fox_f, "nt", "fox_dx_gate", out_dtype=F32)
    grad_x, d_fox_norm = _mm_in_bwd(dproj0, w_fox_main, xs, fox_norm, r0, dh1, "fox_dx", more=dn0_f)

    dw_fox_in = jnp.concatenate(dw_fox_parts + [dw_fox_f[:, :FOX_HEADS]], axis=1)
    slabs = [_cols_to_slabs(dw_fox_in), dw_fox_out.reshape(N_DEV, d // N_DEV, d), _cols_to_slabs(dw_dil_in),
             dw_dil_out.reshape(N_DEV, d // N_DEV, d),
             jnp.stack([dw_up0, dw_up1]).reshape(2, ple_dim, N_DEV, LANES).transpose(2, 0, 1, 3).reshape(N_DEV, -1, LANES),
             jnp.stack([dw_gate0, dw_gate1]).reshape(2, N_DEV, d // N_DEV, d).transpose(1, 0, 2, 3).reshape(N_DEV, -1, d)]
    names = ["fox_w_in", "fox_w_out", "dil_w_in", "dil_w_out", "ple_w_up", "ple_w_gate"]
    slabs = [s.reshape((N_CHIPS, 2) + s.shape[1:]) for s in slabs]
    from_sibling = _core_exchange(slabs)
    core = lax.axis_index("c").astype(jnp.int32).reshape(1)
    chip_sums = [_core_sum(s, f, core, "core_sum_" + nm) for s, f, nm in zip(slabs, from_sibling, names)]
    recv = _chip_exchange(chip_sums)
    big = [(fox_w_in, m_fox_w_in, v_fox_w_in), (fox_w_out, m_fox_w_out, v_fox_w_out),
           (dil_w_in, m_dil_w_in, v_dil_w_in), (dil_w_out, m_dil_w_out, v_dil_w_out),
           (ple_w_up, m_ple_w_up, v_ple_w_up), (ple_w_gate, m_ple_w_gate, v_ple_w_gate)]
    upd = {}
    for rv, (w, m, v), nm in zip(recv, big, names):
        shp2 = rv.shape[1:]
        res = _adamw(rv, w.reshape(shp2), m.reshape(shp2), v.reshape(shp2), "adamw_" + nm)
        upd[nm] = [a.reshape(w.shape) for a in res]

    loss_row = jnp.where(jnp.arange(LANES) == 0, loss_part, 0.0)
    vec = jnp.concatenate([_rows8(d_fox_norm), _rows8(d_final_norm), _rows8(d_dil_norm), d_b_f, loss_row,
                           jnp.zeros((VEC_ROWS - 26, LANES), F32)], axis=0)

    def small_pack(a_fox_norm, a_final_norm, a_dil_norm, a_b_f):
        return jnp.concatenate([_rows8(a_fox_norm), _rows8(a_final_norm), _pad_rows(a_dil_norm, 8),
                                _pad_rows(a_b_f, 8)], axis=0)

    sg, sd, sm, sv = _small_allreduce_adamw(
        vec, small_pack(fox_norm, final_norm, dil_norm, fox_b_f),
        small_pack(m_fox_norm, m_final_norm, m_dil_norm, m_fox_b_f),
        small_pack(v_fox_norm, v_final_norm, v_dil_norm, v_fox_b_f))

    def small_unpack(a):
        return {"fox_norm": a[0:8].reshape(1, d), "final_norm": a[8:16].reshape(d), "dil_norm": a[16:17],
                "fox_b_f": a[24:25, :FOX_HEADS]}

    loss = sg[25, 0]
    order = ["fox_norm", "fox_w_in", "fox_b_f", "fox_w_out", "dil_norm", "dil_w_in", "dil_w_out", "ple_w_up",
             "ple_w_gate", "final_norm"]
    out = [loss, grad_x[None]]
    for idx, small in enumerate((sg, sd, sm, sv)):
        sp = small_unpack(small)
        out += [sp[nm] if nm in sp else upd[nm][idx] for nm in order]
    return tuple(out)
```

```python
import functools

import jax
import jax.numpy as jnp
from jax import lax
from jax.experimental import pallas as pl
from jax.experimental.pallas import tpu as pltpu

F32 = jnp.float32
BF16 = jnp.bfloat16
SDS = jax.ShapeDtypeStruct

D_MODEL = 1024
N_DEV = 8
LANES = 128
FOX_HEADS = 16
FOX_HEAD_DIM = 64
FOX_PAIRS = FOX_HEADS // 2
DIL_HEADS = 8
DIL_BLOCK = 128
DIL_PATTERN = ((128, 1), (512, 4), (2048, 16))
ALIBI_MAX_EXP = 8.0
RMS_EPS = 1e-6
ADAM_LR, ADAM_B1, ADAM_B2, ADAM_EPS, ADAM_WD, ADAM_STEP = 0.001, 0.9, 0.999, 1e-08, 0.01, 10
VMEM_LIMIT = 48 * 1024 * 1024
NEG_INF = float("-inf")

NN = (((1,), (0,)), ((), ()))
NT = (((1,), (1,)), ((), ()))
TN = (((0,), (0,)), ((), ()))
MESH = pl.DeviceIdType.MESH


def _pcall(body, **kw):
    return pl.pallas_call(body, **kw)


def _params(**kw):
    return pltpu.CompilerParams(vmem_limit_bytes=VMEM_LIMIT, **kw)


def _dot(a, b, dims):
    return lax.dot_general(a, b, dims, preferred_element_type=F32)


def _sigmoid(x):
    return 1.0 / (1.0 + jnp.exp(-x))


def _mm(a_parts, b, mode, name, tiles=(512, 1024, 1024), extras=(), outs=None, epilogue=None, out_dtype=BF16):
    na = len(a_parts)
    if mode == "tn":
        k_part, m = a_parts[0].shape
        n = b.shape[1]
    else:
        m, k_part = a_parts[0].shape
        n = b.shape[1] if mode == "nn" else b.shape[0]
    tm, tn, tk = min(tiles[0], m), min(tiles[1], n), min(tiles[2], k_part)
    kb = k_part // tk
    nk = na * kb
    grid = (m // tm, n // tn, nk)

    in_specs = []
    for s in range(na):
        if mode == "tn":
            in_specs.append(pl.BlockSpec((tk, tm), lambda i, j, k: (k, i)))
        else:
            in_specs.append(pl.BlockSpec((tm, tk), lambda i, j, k, s=s: (i, jnp.clip(k - s * kb, 0, kb - 1))))
    if mode == "nt":
        in_specs.append(pl.BlockSpec((tn, tk), lambda i, j, k: (j, k)))
    else:
        in_specs.append(pl.BlockSpec((tk, tn), lambda i, j, k: (k, j)))
    for _, blk, imap in extras:
        in_specs.append(pl.BlockSpec(blk, imap))
    if outs is None:
        outs = [(SDS((m, n), out_dtype), (tm, tn), lambda i, j, k: (i, j))]
    out_specs = [pl.BlockSpec(blk, imap) for _, blk, imap in outs]
    ne, no = len(extras), len(outs)
    dims = {"nn": NN, "nt": NT, "tn": TN}[mode]

    def finish(res, e_refs, o_refs, i):
        if epilogue is None:
            o_refs[0][...] = res.astype(o_refs[0].dtype)
        else:
            epilogue(res, e_refs, o_refs, i)

    def body(*refs):
        a_refs = refs[:na]
        b_ref = refs[na]
        e_refs = refs[na + 1:na + 1 + ne]
        o_refs = refs[na + 1 + ne:na + 1 + ne + no]
        i, k = pl.program_id(0), pl.program_id(2)
        if nk == 1:
            finish(_dot(a_refs[0][...].astype(BF16), b_ref[...].astype(BF16), dims), e_refs, o_refs, i)
            return
        acc = refs[-1]

        @pl.when(k == 0)
        def _():
            acc[...] = jnp.zeros_like(acc)

        def step(a_ref):
            acc[...] += _dot(a_ref[...].astype(BF16), b_ref[...].astype(BF16), dims)

        for s in range(na):
            if na == 1:
                step(a_refs[0])
            else:
                pl.when((k >= s * kb) & (k < (s + 1) * kb))(functools.partial(step, a_refs[s]))

        @pl.when(k == nk - 1)
        def _():
            finish(acc[...], e_refs, o_refs, i)

    res = _pcall(
        body, name=name, grid=grid, in_specs=in_specs, out_specs=out_specs,
        out_shape=[o[0] for o in outs], scratch_shapes=[] if nk == 1 else [pltpu.VMEM((tm, tn), F32)],
        compiler_params=_params(dimension_semantics=("arbitrary", "arbitrary", "arbitrary")),
    )(*a_parts, b, *[e[0] for e in extras])
    return res[0] if len(res) == 1 else res


IN_PROJ_TILES = (1024, 1024, 1024)
DW_TILES = (1024, 1024, 512)


def _dw(x, dy, name):
    return _mm([x], dy, "tn", name, tiles=DW_TILES)


def _add_extra_epilogue(acc, e_refs, o_refs, i):
    o_refs[0][...] = acc + e_refs[0][...]


def _mm_residual(a, b, mode, res, name):
    m = a.shape[0]
    n = b.shape[1] if mode == "nn" else b.shape[0]
    tm, tn = 512, 1024
    return _mm([a], b, mode, name, tiles=(tm, tn, 1024),
               extras=[(res, (tm, tn), lambda i, j, k: (i, j))],
               outs=[(SDS((m, n), F32), (tm, tn), lambda i, j, k: (i, j))],
               epilogue=_add_extra_epilogue)


def _rms_fwd(h, g, name):
    t, d = h.shape
    tm = 512

    def body(h_ref, g_ref, n_ref, r_ref):
        x = h_ref[...]
        r = lax.rsqrt(jnp.mean(x * x, axis=-1, keepdims=True) + RMS_EPS)
        n_ref[...] = ((x * r) * g_ref[...]).astype(BF16)
        r_ref[...] = r

    return _pcall(
        body, name=name, grid=(t // tm,),
        in_specs=[pl.BlockSpec((tm, d), lambda i: (i, 0)), pl.BlockSpec((1, d), lambda i: (0, 0))],
        out_specs=[pl.BlockSpec((tm, d), lambda i: (i, 0)), pl.BlockSpec((tm, 1), lambda i: (i, 0))],
        out_shape=[SDS((t, d), BF16), SDS((t, 1), F32)],
        compiler_params=_params(),
    )(h, g)


def _rms_bwd_rows(dn, x, g, r):
    xhat = x * r
    dxhat = dn * g
    dx = r * (dxhat - xhat * jnp.mean(dxhat * xhat, axis=-1, keepdims=True))
    dg = jnp.sum(dn * xhat, axis=0, keepdims=True)
    return dx, dg


def _mm_in_bwd(d_parts, w, h, g, r, dres, name, more=None):
    t = h.shape[0]
    tm = 512
    tk = 1024 if len(d_parts) <= 4 else 512
    row = lambda i, j, k: (i, 0)
    extras = [(h, (tm, D_MODEL), row), (g, (1, D_MODEL), lambda i, j, k: (0, 0)), (r, (tm, 1), row),
              (dres, (tm, D_MODEL), row)]
    if more is not None:
        extras.append((more, (tm, D_MODEL), row))

    def epilogue(acc, e_refs, o_refs, i):
        dn = acc if more is None else acc + e_refs[4][...]
        dx, dg = _rms_bwd_rows(dn, e_refs[0][...], e_refs[1][...], e_refs[2][...])
        o_refs[0][...] = e_refs[3][...] + dx

        @pl.when(i == 0)
        def _():
            o_refs[1][...] = dg

        @pl.when(i > 0)
        def _():
            o_refs[1][...] += dg

    return _mm(d_parts, w, "nt", name, tiles=(tm, D_MODEL, tk), extras=extras,
               outs=[(SDS((t, D_MODEL), F32), (tm, D_MODEL), row),
                     (SDS((1, D_MODEL), F32), (1, D_MODEL), lambda i, j, k: (0, 0))],
               epilogue=epilogue)


def _final_bwd(h, g, tgt):
    t, d = h.shape
    tm = 256

    def body(h_ref, g_ref, t_ref, dh_ref, dg_ref, loss_ref):
        i = pl.program_id(0)
        x = h_ref[...]
        gg = g_ref[...]
        r = lax.rsqrt(jnp.mean(x * x, axis=-1, keepdims=True) + RMS_EPS)
        err = (x * r) * gg - t_ref[...]
        part = 0.5 * jnp.sum(jnp.mean(err * err, axis=-1, keepdims=True), axis=0, keepdims=True)
        dx, dg = _rms_bwd_rows(err * (1.0 / d), x, gg, r)
        dh_ref[...] = dx

        @pl.when(i == 0)
        def _():
            dg_ref[...] = dg
            loss_ref[...] = jnp.broadcast_to(part, loss_ref.shape)

        @pl.when(i > 0)
        def _():
            dg_ref[...] += dg
            loss_ref[...] += jnp.broadcast_to(part, loss_ref.shape)

    return _pcall(
        body, name="final_norm_loss", grid=(t // tm,),
        in_specs=[pl.BlockSpec((tm, d), lambda i: (i, 0)), pl.BlockSpec((1, d), lambda i: (0, 0)),
                  pl.BlockSpec((tm, d), lambda i: (i, 0))],
        out_specs=[pl.BlockSpec((tm, d), lambda i: (i, 0)), pl.BlockSpec((1, d), lambda i: (0, 0)),
                   pl.BlockSpec((1, LANES), lambda i: (0, 0))],
        out_shape=[SDS((t, d), F32), SDS((1, d), F32), SDS((1, LANES), F32)],
        compiler_params=_params(),
    )(h, g, tgt)


GATE_ROWS = 256


def _split3(x):
    hi = x.astype(BF16)
    r1 = x - hi.astype(F32)
    mid = r1.astype(BF16)
    lo = (r1 - mid.astype(F32)).astype(BF16)
    return hi, mid, lo


def _tri_sum(x, upper):
    rows = x.shape[0]
    ri = lax.broadcasted_iota(jnp.int32, (rows, rows), 0)
    ci = lax.broadcasted_iota(jnp.int32, (rows, rows), 1)
    tri = jnp.where((ri <= ci) if upper else (ri >= ci), 1.0, 0.0).astype(BF16)
    hi, mid, lo = _split3(x)
    return _dot(tri, hi, NN) + _dot(tri, mid, NN) + _dot(tri, lo, NN)


def _log_sigmoid(x):
    return jnp.minimum(x, 0.0) - jnp.log1p(jnp.exp(-jnp.abs(x)))


def _fox_gate_fwd(projf, bpad):
    t = projf.shape[0]
    tb = GATE_ROWS

    def body(x_ref, b_ref, c_ref, carry):
        i = pl.program_id(0)

        @pl.when(i == 0)
        def _():
            carry[...] = jnp.zeros_like(carry)

        c_ref[...] = _tri_sum(_log_sigmoid(x_ref[...] + b_ref[...]), upper=False) + carry[...]
        carry[...] = c_ref[pl.ds(tb - 1, 1), :]

    return _pcall(
        body, name="fox_gate_fwd", grid=(t // tb,),
        in_specs=[pl.BlockSpec((tb, LANES), lambda i: (i, 0)), pl.BlockSpec((1, LANES), lambda i: (0, 0))],
        out_specs=pl.BlockSpec((tb, LANES), lambda i: (i, 0)),
        out_shape=SDS((t, LANES), F32), scratch_shapes=[pltpu.VMEM((1, LANES), F32)],
        compiler_params=_params(),
    )(projf, bpad)


def _fox_gate_bwd(projf, bpad, dc_query, dc_key):
    t = projf.shape[0]
    tb = GATE_ROWS
    nb = t // tb

    def body(x_ref, b_ref, dcq_ref, dck_ref, df_ref, db_ref, carry, buf):
        i = pl.program_id(0)

        @pl.when(i == 0)
        def _():
            carry[...] = jnp.zeros_like(carry)

        buf[...] = _tri_sum(dcq_ref[...] - dck_ref[...], upper=True) + carry[...]
        carry[...] = buf[pl.ds(0, 1), :]
        df = buf[...] * _sigmoid(-(x_ref[...] + b_ref[...]))
        df_ref[...] = df.astype(BF16)
        part = jnp.sum(df, axis=0, keepdims=True)

        @pl.when(i == 0)
        def _():
            db_ref[...] = part

        @pl.when(i > 0)
        def _():
            db_ref[...] += part

    rev = lambda i: (nb - 1 - i, 0)
    return _pcall(
        body, name="fox_gate_bwd", grid=(nb,),
        in_specs=[pl.BlockSpec((tb, LANES), rev), pl.BlockSpec((1, LANES), lambda i: (0, 0)),
                  pl.BlockSpec((tb, LANES), rev), pl.BlockSpec((tb, LANES), rev)],
        out_specs=[pl.BlockSpec((tb, LANES), rev), pl.BlockSpec((1, LANES), lambda i: (0, 0))],
        out_shape=[SDS((t, LANES), BF16), SDS((1, LANES), F32)],
        scratch_shapes=[pltpu.VMEM((1, LANES), F32), pltpu.VMEM((tb, LANES), F32)],
        compiler_params=_params(),
    )(projf, bpad, dc_query, dc_key)


FOX_TQ = 512
FOX_SCALE = FOX_HEAD_DIM ** -0.5


def _low_lanes(shape):
    return lax.broadcasted_iota(jnp.int32, shape, len(shape) - 1) < FOX_HEAD_DIM


FOX_AUG = 3


def _top_rows(shape):
    return lax.broadcasted_iota(jnp.int32, shape, 0) < FOX_HEAD_DIM


def _fox_aug(a, b, sign_a, sign_b, piece_entry, ones_entry, name):
    t = a.shape[0]
    tb = 512

    def body(a_ref, b_ref, o_ref):
        x = sign_a * a_ref[...]
        if sign_b != 0.0:
            x = x + sign_b * b_ref[...]
        head = lax.broadcasted_iota(jnp.int32, (LANES, D_MODEL), 0)
        col = lax.broadcasted_iota(jnp.int32, (LANES, D_MODEL), 1)
        base = (head // 2) * LANES + (1 - head % 2) * FOX_HEAD_DIM + piece_entry
        acc = jnp.zeros((tb, D_MODEL), F32)
        for e, piece in enumerate(_split3(x)):
            place = jnp.where((head < FOX_HEADS) & (col == base + e), 1.0, 0.0).astype(BF16)
            acc = acc + _dot(piece, place, NN)
        if ones_entry is not None:
            ent = lax.broadcasted_iota(jnp.int32, (1, D_MODEL), 1) % FOX_HEAD_DIM
            acc = acc + jnp.where((ent >= ones_entry) & (ent < ones_entry + FOX_AUG), 1.0, 0.0)
        o_ref[...] = acc.astype(BF16)

    blk = pl.BlockSpec((tb, LANES), lambda i: (i, 0))
    return _pcall(
        body, name=name, grid=(t // tb,), in_specs=[blk, blk],
        out_specs=pl.BlockSpec((tb, D_MODEL), lambda i: (i, 0)), out_shape=SDS((t, D_MODEL), BF16),
        compiler_params=_params(),
    )(a, b)


def _fox_unpack_dc(dck_wide, dcq):
    t = dck_wide.shape[0]
    dck = dck_wide.reshape(t, FOX_PAIRS, 2, FOX_HEAD_DIM)[:, :, ::-1, 0].reshape(t, FOX_HEADS)
    return dcq[:, :, 0, :].reshape(FOX_HEADS, t).T, dck


def _causal_steps(nq, key_major):
    if key_major:
        pairs = [(i, j) for j in range(nq) for i in range(j, nq)]
    else:
        pairs = [(i, j) for i in range(nq) for j in range(i + 1)]
    return (jnp.asarray([p[0] for p in pairs], jnp.int32), jnp.asarray([p[1] for p in pairs], jnp.int32))


def _pair_operand(low, own, other, hh):
    return jnp.where(low, own, other) if hh == 0 else jnp.where(low, other, own)


def _fox_fwd(proj, qaug, kaug):
    t = proj.shape[0]
    tq = tk = min(FOX_TQ, t)
    nq = t // tq
    cb = D_MODEL // LANES

    i_tab, j_tab = _causal_steps(nq, key_major=False)

    def body(i_ref, j_ref, q_ref, k_ref, v_ref, z_ref, qa_ref, ka_ref, o_ref, g_ref, lse_ref, m_s, l_s, acc_s):
        step = pl.program_id(1)
        i, j = i_ref[step], j_ref[step]

        @pl.when(j == 0)
        def _():
            m_s[...] = jnp.full_like(m_s, NEG_INF)
            l_s[...] = jnp.zeros_like(l_s)
            acc_s[...] = jnp.zeros_like(acc_s)

        low = _low_lanes((tq, LANES))
        top = _top_rows((LANES, tq))

        def update(masked):
            qs = q_ref[...] * FOX_SCALE
            qa, k, ka, v = qa_ref[...], k_ref[...], ka_ref[...], v_ref[...]
            if masked:
                causal = (lax.broadcasted_iota(jnp.int32, (tk, tq), 0) <= lax.broadcasted_iota(jnp.int32, (tk, tq), 1))
            upd = []
            scores = [_dot(_pair_operand(low, k, ka, hh), _pair_operand(low, qs, qa, hh), NT) for hh in range(2)]
            for hh in range(2):
                s = scores[hh]
                if masked:
                    s = jnp.where(causal, s, NEG_INF)
                m_prev = m_s[hh]
                m_new = jnp.maximum(m_prev, jnp.max(s, axis=0, keepdims=True))
                alpha = jnp.exp(m_prev - m_new)
                p = jnp.exp(s - m_new)
                l_s[hh] = alpha * l_s[hh] + jnp.sum(p, axis=0, keepdims=True)
                m_s[hh] = m_new
                upd.append((alpha, _dot(v, p.astype(BF16), TN)))
            acc = acc_s[...]
            acc_s[...] = jnp.where(top, acc * upd[0][0] + upd[0][1], acc * upd[1][0] + upd[1][1])

        pl.when(j < i)(functools.partial(update, False))
        pl.when(j == i)(functools.partial(update, True))

        @pl.when(j == i)
        def _():
            o = (acc_s[...] / jnp.where(top, l_s[0], l_s[1])).T
            z = z_ref[...].astype(F32)
            o_ref[...] = o.astype(BF16)
            g_ref[...] = (o * (z * _sigmoid(z))).astype(BF16)
            for hh in range(2):
                lse_ref[hh] = m_s[hh] + jnp.log(l_s[hh])

    qblk = lambda col: pl.BlockSpec((tq, LANES), lambda h, s, it, jt: (it[s], col + h))
    kblk = lambda col: pl.BlockSpec((tk, LANES), lambda h, s, it, jt: (jt[s], col + h))
    return _pcall(
        body, name="fox_attn_fwd",
        grid_spec=pltpu.PrefetchScalarGridSpec(
            num_scalar_prefetch=2, grid=(FOX_PAIRS, i_tab.shape[0]),
            in_specs=[qblk(0), kblk(cb), kblk(2 * cb), qblk(3 * cb), qblk(0), kblk(0)],
            out_specs=[qblk(0), qblk(0), pl.BlockSpec((2, 1, tq), lambda h, s, it, jt: (h, 0, it[s]))],
            scratch_shapes=[pltpu.VMEM((2, 1, tq), F32), pltpu.VMEM((2, 1, tq), F32), pltpu.VMEM((LANES, tq), F32)]),
        out_shape=[SDS((t, D_MODEL), BF16), SDS((t, D_MODEL), BF16), SDS((FOX_HEADS, 1, t), F32)],
        compiler_params=_params(dimension_semantics=("arbitrary", "arbitrary")),
    )(i_tab, j_tab, proj, proj, proj, proj, qaug, kaug)


FOX_SUM_ROWS = 8


def _fox_bwd(proj, do, qaug, kaug, doaug):
    t = proj.shape[0]
    tq = tk = min(FOX_TQ, t)
    nq = t // tq
    cb = D_MODEL // LANES

    i_tab, j_tab = _causal_steps(nq, key_major=True)

    def body(i_ref, j_ref, q_ref, k_ref, v_ref, do_ref, qa_ref, ka_ref, da_ref,
             dq_ref, dk_ref, dv_ref, dck_ref, dcq_ref, dq_acc, dcq_acc, dk_acc, dks_acc, dv_acc):
        step = pl.program_id(1)
        i, j = i_ref[step], j_ref[step]
        low = _low_lanes((tq, LANES))
        top = _top_rows((LANES, tq))

        @pl.when(i == j)
        def _():
            dk_acc[...] = jnp.zeros_like(dk_acc)
            dks_acc[...] = jnp.zeros_like(dks_acc)
            dv_acc[...] = jnp.zeros_like(dv_acc)

        def update(masked):
            qs = q_ref[...] * FOX_SCALE
            k, v, dout = k_ref[...], v_ref[...], do_ref[...]
            qa, ka, da = qa_ref[...], ka_ref[...], da_ref[...]
            lane = lax.broadcasted_iota(jnp.int32, (tk, LANES), 1)
            vone = jnp.where((lane & (FOX_HEAD_DIM - 1)) < FOX_AUG, 1.0, 0.0).astype(v.dtype)
            one = jnp.ones_like(k)
            if masked:
                causal = (lax.broadcasted_iota(jnp.int32, (tk, tq), 0) <= lax.broadcasted_iota(jnp.int32, (tk, tq), 1))
            parts = []
            scores = [_dot(_pair_operand(low, k, ka, hh), _pair_operand(low, qs, qa, hh), NT) for hh in range(2)]
            dps = [_dot(_pair_operand(low, v, vone, hh), _pair_operand(low, dout, da, hh), NT) for hh in range(2)]
            for hh in range(2):
                s = scores[hh]
                if masked:
                    s = jnp.where(causal, s, NEG_INF)
                p = jnp.exp(s)
                ds = p * dps[hh]
                pb = p.astype(BF16)
                dsb = ds.astype(BF16)
                parts.append((_dot(pb, dout, NN),
                              _dot(dsb, _pair_operand(low, qs, one, hh), NN),
                              _dot(_pair_operand(low, k, one, hh), dsb, TN)))
            dv_acc[...] += jnp.where(low, parts[0][0], parts[1][0])
            dk_acc[...] += jnp.where(low, parts[0][1], parts[1][1])
            dks_acc[...] += jnp.where(low, parts[1][1], parts[0][1])
            dq_t = jnp.where(top, parts[0][2], parts[1][2]) * FOX_SCALE
            sum_a = parts[0][2][FOX_HEAD_DIM:FOX_HEAD_DIM + FOX_SUM_ROWS, :]
            sum_b = parts[1][2][0:FOX_SUM_ROWS, :]

            @pl.when(j == 0)
            def _():
                dq_acc[i] = dq_t
                dcq_acc[0, i] = sum_a
                dcq_acc[1, i] = sum_b

            @pl.when(j > 0)
            def _():
                dq_acc[i] += dq_t
                dcq_acc[0, i] += sum_a
                dcq_acc[1, i] += sum_b

        pl.when(i > j)(functools.partial(update, False))
        pl.when(i == j)(functools.partial(update, True))

        @pl.when(i == nq - 1)
        def _():
            dk_ref[...] = dk_acc[...].astype(BF16)
            dv_ref[...] = dv_acc[...].astype(BF16)
            dck_ref[...] = dks_acc[...]

        @pl.when((i == nq - 1) & (j == nq - 1))
        def _():
            for blk in range(nq):
                dq_ref[blk * tq:(blk + 1) * tq, :] = dq_acc[blk].T.astype(BF16)
            dcq_ref[...] = dcq_acc[...]

    qblk = lambda col: pl.BlockSpec((tq, LANES), lambda h, s, it, jt: (it[s], col + h))
    kblk = lambda col: pl.BlockSpec((tk, LANES), lambda h, s, it, jt: (jt[s], col + h))
    return _pcall(
        body, name="fox_attn_bwd",
        grid_spec=pltpu.PrefetchScalarGridSpec(
            num_scalar_prefetch=2, grid=(FOX_PAIRS, i_tab.shape[0]),
            in_specs=[qblk(0), kblk(cb), kblk(2 * cb), qblk(0), qblk(0), kblk(0), qblk(0)],
            out_specs=[pl.BlockSpec((t, LANES), lambda h, s, it, jt: (0, h)), kblk(0), kblk(0), kblk(0),
                       pl.BlockSpec((2, nq, FOX_SUM_ROWS, tq), lambda h, s, it, jt: (h, 0, 0, 0))],
            scratch_shapes=[pltpu.VMEM((nq, LANES, tq), F32), pltpu.VMEM((2, nq, FOX_SUM_ROWS, tq), F32),
                            pltpu.VMEM((tk, LANES), F32), pltpu.VMEM((tk, LANES), F32), pltpu.VMEM((tk, LANES), F32)]),
        out_shape=[SDS((t, D_MODEL), BF16), SDS((t, D_MODEL), BF16), SDS((t, D_MODEL), BF16),
                   SDS((t, D_MODEL), F32), SDS((FOX_HEADS, nq, FOX_SUM_ROWS, tq), F32)],
        compiler_params=_params(dimension_semantics=("arbitrary", "arbitrary")),
    )(i_tab, j_tab, proj, proj, proj, do, qaug, kaug, doaug)


def _mm_gate_bwd(dh, w_out, z_src, z_col0, o, heads, name):
    t = dh.shape[0]
    tm = 512
    row = lambda i, j, k: (i, 0)
    zcb = z_col0 // D_MODEL

    def epilogue(acc, e_refs, o_refs, i):
        z = e_refs[0][...].astype(F32)
        ov = e_refs[1][...].astype(F32)
        sg = _sigmoid(z)
        dout = acc * (z * sg)
        o_refs[0][...] = dout.astype(BF16)
        o_refs[1][...] = (acc * ov * (sg * (1.0 + z * (1.0 - sg)))).astype(BF16)
        prod = dout * ov
        for cbk in range(D_MODEL // LANES):
            seg = prod[:, cbk * LANES:(cbk + 1) * LANES]
            tot = jnp.sum(seg, axis=-1, keepdims=True)
            if heads == D_MODEL // LANES:
                o_refs[2][cbk] = tot
            else:
                lo = jnp.sum(jnp.where(_low_lanes(seg.shape), seg, 0.0), axis=-1, keepdims=True)
                o_refs[2][2 * cbk] = lo
                o_refs[2][2 * cbk + 1] = tot - lo

    return _mm([dh], w_out, "nt", name, tiles=(tm, D_MODEL, D_MODEL),
               extras=[(z_src, (tm, D_MODEL), lambda i, j, k: (i, zcb)), (o, (tm, D_MODEL), row)],
               outs=[(SDS((t, D_MODEL), BF16), (tm, D_MODEL), row), (SDS((t, D_MODEL), BF16), (tm, D_MODEL), row),
                     (SDS((heads, t, 1), F32), (heads, tm, 1), lambda i, j, k: (0, i, 0))],
               epilogue=epilogue)


def _ple_fwd(h, pin, w_up, w_gate, name):
    t = h.shape[0]
    tm = 512
    pd = pin.shape[1]

    def body(h_ref, p_ref, wu_ref, wg_ref, hn_ref, u_ref, a_ref):
        h = h_ref[...]
        u = _dot(p_ref[...].astype(BF16), wu_ref[...], NN)
        a = _dot(h.astype(BF16), wg_ref[...], NN)
        hn_ref[...] = h + u * _sigmoid(a)
        u_ref[...] = u.astype(BF16)
        a_ref[...] = a.astype(BF16)

    rows = pl.BlockSpec((tm, D_MODEL), lambda i: (i, 0))
    return _pcall(
        body, name=name, grid=(t // tm,),
        in_specs=[rows, pl.BlockSpec((tm, pd), lambda i: (i, 0)),
                  pl.BlockSpec((pd, D_MODEL), lambda i: (0, 0)), pl.BlockSpec((D_MODEL, D_MODEL), lambda i: (0, 0))],
        out_specs=[rows, rows, rows],
        out_shape=[SDS((t, D_MODEL), F32), SDS((t, D_MODEL), BF16), SDS((t, D_MODEL), BF16)],
        compiler_params=_params(),
    )(h, pin, w_up, w_gate)


def _ple_bwd_elem(dh, u, a, name):
    t = dh.shape[0]
    tm = 512

    def body(dh_ref, u_ref, a_ref, du_ref, da_ref):
        g = dh_ref[...]
        s = _sigmoid(a_ref[...].astype(F32))
        du_ref[...] = (g * s).astype(BF16)
        da_ref[...] = (g * u_ref[...].astype(F32) * (s * (1.0 - s))).astype(BF16)

    blk = pl.BlockSpec((tm, D_MODEL), lambda i: (i, 0))
    return _pcall(
        body, name=name, grid=(t // tm,), in_specs=[blk, blk, blk], out_specs=[blk, blk],
        out_shape=[SDS((t, D_MODEL), BF16), SDS((t, D_MODEL), BF16)], compiler_params=_params(),
    )(dh, u, a)


DIL_SCALE = LANES ** -0.5


def _dil_masks():
    ii = lax.broadcasted_iota(jnp.int32, (DIL_BLOCK, DIL_BLOCK), 0)
    jj = lax.broadcasted_iota(jnp.int32, (DIL_BLOCK, DIL_BLOCK), 1)
    return ii, jj


def _dil_scores(q, kp, kc, slope, has_prev):
    ii, jj = _dil_masks()
    dist_p = (DIL_BLOCK + ii - jj).astype(F32)
    dist_c = (ii - jj).astype(F32)
    sp = _dot(q, kp, NT) * DIL_SCALE - slope * dist_p
    sc = _dot(q, kc, NT) * DIL_SCALE - slope * dist_c
    sp = jnp.where((jj >= ii) & has_prev, sp, NEG_INF)
    sc = jnp.where(jj <= ii, sc, NEG_INF)
    return sp, sc


def _dil_fwd(proj, slopes, grp, dil, name):
    t = proj.shape[0]
    rows = DIL_BLOCK * dil
    nsb = t // rows
    qc, kc_, vc_ = grp * DIL_HEADS, 3 * DIL_HEADS + grp * DIL_HEADS, 6 * DIL_HEADS + grp * DIL_HEADS

    def body(q_ref, kp_ref, kc_ref, vp_ref, vc_ref, sl_ref, o_ref, lse_ref, qf, kpf, kcf, vpf, vcf, of, lf):
        m = pl.program_id(1)
        for src, dst in ((q_ref, qf), (kp_ref, kpf), (kc_ref, kcf), (vp_ref, vpf), (vc_ref, vcf)):
            dst[...] = src[...].astype(F32)
        slope = sl_ref[0]
        has_prev = m > 0
        for r in range(dil):
            ph = pl.ds(r, DIL_BLOCK, stride=dil)
            q = qf[ph, :].astype(BF16)
            sp, sc = _dil_scores(q, kpf[ph, :].astype(BF16), kcf[ph, :].astype(BF16), slope, has_prev)
            mx = jnp.maximum(jnp.max(sp, axis=-1, keepdims=True), jnp.max(sc, axis=-1, keepdims=True))
            pp = jnp.exp(sp - mx)
            pc = jnp.exp(sc - mx)
            l = jnp.sum(pp, axis=-1, keepdims=True) + jnp.sum(pc, axis=-1, keepdims=True)
            o = _dot(pp.astype(BF16), vpf[ph, :].astype(BF16), NN) + _dot(pc.astype(BF16), vcf[ph, :].astype(BF16), NN)
            of[ph, :] = o / l
            lf[ph, :] = mx + jnp.log(l)
        o_ref[...] = of[...].astype(BF16)
        lse_ref[0] = lf[...]

    prev = lambda h, m: jnp.maximum(m - 1, 0)
    blk = lambda col, rowmap: pl.BlockSpec((rows, LANES), lambda h, m: (rowmap(h, m), col + h))
    cur = lambda h, m: m
    return _pcall(
        body, name=name, grid=(DIL_HEADS, nsb),
        in_specs=[blk(qc, cur), blk(kc_, prev), blk(kc_, cur), blk(vc_, prev), blk(vc_, cur),
                  pl.BlockSpec((1, 1, 1), lambda h, m: (h, 0, 0))],
        out_specs=[pl.BlockSpec((rows, LANES), lambda h, m: (m, h)), pl.BlockSpec((1, rows, 1), lambda h, m: (h, m, 0))],
        out_shape=[SDS((t, D_MODEL), BF16), SDS((DIL_HEADS, t, 1), F32)],
        scratch_shapes=[pltpu.VMEM((rows, LANES), F32)] * 6 + [pltpu.VMEM((rows, 1), F32)],
        compiler_params=_params(),
    )(proj, proj, proj, proj, proj, slopes)


def _dil_mix(outs, lses, proj, z_col0):
    t = proj.shape[0]
    tm = 512
    zcb = z_col0 // LANES
    ng = len(outs)

    def body(*refs):
        o_refs, l_refs, z_ref = refs[:ng], refs[ng:2 * ng], refs[2 * ng]
        om_ref, g_ref, lse_ref = refs[2 * ng + 1:]
        ls = [r[0] for r in l_refs]
        mx = functools.reduce(jnp.maximum, ls)
        es = [jnp.exp(l - mx) for l in ls]
        tot = functools.reduce(jnp.add, es)
        o = functools.reduce(jnp.add, [(e / tot) * r[...].astype(F32) for e, r in zip(es, o_refs)])
        z = z_ref[...].astype(F32)
        om_ref[...] = o.astype(BF16)
        g_ref[...] = (o * (z * _sigmoid(z))).astype(BF16)
        lse_ref[0] = mx + jnp.log(tot)

    tile = pl.BlockSpec((tm, LANES), lambda i, h: (i, h))
    col = pl.BlockSpec((1, tm, 1), lambda i, h: (h, i, 0))
    return _pcall(
        body, name="dil_mix", grid=(t // tm, DIL_HEADS),
        in_specs=[tile] * ng + [col] * ng + [pl.BlockSpec((tm, LANES), lambda i, h: (i, zcb + h))],
        out_specs=[tile, tile, col],
        out_shape=[SDS((t, D_MODEL), BF16), SDS((t, D_MODEL), BF16), SDS((DIL_HEADS, t, 1), F32)],
        compiler_params=_params(),
    )(*outs, *lses, proj)


def _dil_bwd(proj, do, lse, delta, slopes, grp, dil, name):
    t = proj.shape[0]
    rows = DIL_BLOCK * dil
    nsb = t // rows
    qc, kc_, vc_ = grp * DIL_HEADS, 3 * DIL_HEADS + grp * DIL_HEADS, 6 * DIL_HEADS + grp * DIL_HEADS

    def body(q_ref, qn_ref, kp_ref, kc_ref, vp_ref, vc_ref, do_ref, don_ref, l_ref, ln_ref, d_ref, dn_ref, sl_ref,
             dq_ref, dk_ref, dv_ref, qf, qnf, kpf, kcf, vpf, vcf, dof, donf, dqf, dkf, dvf):
        m = pl.program_id(1)
        for src, dst in ((q_ref, qf), (qn_ref, qnf), (kp_ref, kpf), (kc_ref, kcf), (vp_ref, vpf), (vc_ref, vcf),
                         (do_ref, dof), (don_ref, donf)):
            dst[...] = src[...].astype(F32)
        slope = sl_ref[0]
        has_prev = m > 0
        has_next = m < nsb - 1
        ii, jj = _dil_masks()
        for r in range(dil):
            ph = pl.ds(r, DIL_BLOCK, stride=dil)
            q, kp, kc, vp, vc, dout = (x[ph, :].astype(BF16) for x in (qf, kpf, kcf, vpf, vcf, dof))
            sp, sc = _dil_scores(q, kp, kc, slope, has_prev)
            lrow = l_ref[0, ph, :]
            drow = d_ref[0, ph, :]
            pp = jnp.exp(sp - lrow)
            pc = jnp.exp(sc - lrow)
            dsp = pp * (_dot(dout, vp, NT) - drow)
            dsc = pc * (_dot(dout, vc, NT) - drow)
            dspb, dscb = dsp.astype(BF16), dsc.astype(BF16)
            dqf[ph, :] = (_dot(dspb, kp, NN) + _dot(dscb, kc, NN)) * DIL_SCALE
            qn = qnf[ph, :].astype(BF16)
            don = donf[ph, :].astype(BF16)
            sn = _dot(qn, kc, NT) * DIL_SCALE - slope * (DIL_BLOCK + ii - jj).astype(F32)
            pn = jnp.exp(jnp.where((jj >= ii) & has_next, sn, NEG_INF) - ln_ref[0, ph, :])
            dsn = (pn * (_dot(don, vc, NT) - dn_ref[0, ph, :])).astype(BF16)
            dkf[ph, :] = (_dot(dscb, q, TN) + _dot(dsn, qn, TN)) * DIL_SCALE
            dvf[ph, :] = _dot(pc.astype(BF16), dout, TN) + _dot(pn.astype(BF16), don, TN)
        dq_ref[...] = dqf[...].astype(BF16)
        dk_ref[...] = dkf[...].astype(BF16)
        dv_ref[...] = dvf[...].astype(BF16)

    prev = lambda h, m: jnp.maximum(m - 1, 0)
    cur = lambda h, m: m
    nxt = lambda h, m: jnp.minimum(m + 1, nsb - 1)
    blk = lambda col, rowmap: pl.BlockSpec((rows, LANES), lambda h, m: (rowmap(h, m), col + h))
    colblk = lambda rowmap: pl.BlockSpec((1, rows, 1), lambda h, m: (h, rowmap(h, m), 0))
    out_blk = pl.BlockSpec((rows, LANES), lambda h, m: (m, h))
    return _pcall(
        body, name=name, grid=(DIL_HEADS, nsb),
        in_specs=[blk(qc, cur), blk(qc, nxt), blk(kc_, prev), blk(kc_, cur), blk(vc_, prev), blk(vc_, cur),
                  blk(0, cur), blk(0, nxt), colblk(cur), colblk(nxt), colblk(cur), colblk(nxt),
                  pl.BlockSpec((1, 1, 1), lambda h, m: (h, 0, 0))],
        out_specs=[out_blk, out_blk, out_blk],
        out_shape=[SDS((t, D_MODEL), BF16)] * 3,
        scratch_shapes=[pltpu.VMEM((rows, LANES), F32)] * 11,
        compiler_params=_params(),
    )(proj, proj, proj, proj, proj, proj, do, do, lse, lse, delta, delta, slopes)


def _mesh_pos():
    x, y, c = lax.axis_index("x"), lax.axis_index("y"), lax.axis_index("c")
    return x, y, c


def _peer(pos, k):
    x, y, c = pos
    px = 1 - x if k & 4 else x
    py = 1 - y if k & 2 else y
    pc = 1 - c if k & 1 else c
    return (px, py, pc), 4 * px + 2 * py + pc


N_CHIPS = 4
CHIP_FLIPS = ((1, 0), (0, 1), (1, 1))


def _other_chips(x, y):
    return [(1 - x if fx else x, 1 - y if fy else y) for fx, fy in CHIP_FLIPS]


def _all_gather(arrays):
    n = len(arrays)
    per = 2 * N_CHIPS - 1
    hbm = pl.BlockSpec(memory_space=pltpu.HBM)

    def body(*refs):
        ins, outs = refs[:n], refs[n:2 * n]
        send_sems, recv_sems, local_sems = refs[2 * n:]
        x, y, c = _mesh_pos()
        sibling = (x, y, 1 - c)
        chips = _other_chips(x, y)
        block = lambda px, py, pc: 4 * px + 2 * py + pc

        def copy(w, k, src, blk, to):
            return pltpu.make_async_remote_copy(
                src_ref=src, dst_ref=outs[w].at[blk], send_sem=send_sems.at[w * per + k],
                recv_sem=recv_sems.at[w * per + k], device_id=to, device_id_type=MESH)

        local, started = [], []
        for w in range(n):
            cp = pltpu.make_async_copy(ins[w], outs[w].at[block(x, y, c)], local_sems.at[w])
            cp.start()
            local.append(cp)
            started.append(copy(w, 0, ins[w], block(x, y, c), sibling))
            for j, (px, py) in enumerate(chips):
                started.append(copy(w, 1 + j, ins[w], block(x, y, c), (px, py, c)))
        for cp in started:
            cp.start()
        for j, (px, py) in enumerate(chips):
            for w in range(n):
                copy(w, 1 + j, ins[w], block(px, py, c), sibling).wait_recv()
                cp = copy(w, 4 + j, outs[w].at[block(px, py, c)], block(px, py, c), sibling)
                cp.start()
                started.append(cp)
        for w in range(n):
            copy(w, 0, ins[w], block(x, y, 1 - c), sibling).wait_recv()
            for j, (px, py) in enumerate(chips):
                copy(w, 4 + j, ins[w], block(px, py, 1 - c), sibling).wait_recv()
        for cp in started:
            cp.wait_send()
        for cp in local:
            cp.wait()

    return _pcall(
        body, name="all_gather_weights", in_specs=[hbm] * n, out_specs=[hbm] * n,
        out_shape=[SDS((N_DEV,) + a.shape, a.dtype) for a in arrays],
        scratch_shapes=[pltpu.SemaphoreType.DMA((n * per,)), pltpu.SemaphoreType.DMA((n * per,)),
                        pltpu.SemaphoreType.DMA((n,))],
    )(*arrays)


def _core_exchange(slabs):
    n = len(slabs)
    hbm = pl.BlockSpec(memory_space=pltpu.HBM)

    def body(*refs):
        ins, outs = refs[:n], refs[n:2 * n]
        send_sems, recv_sems = refs[2 * n:]
        x, y, c = _mesh_pos()
        copies = [pltpu.make_async_remote_copy(
            src_ref=ins[w].at[pl.ds(0, N_CHIPS), 1 - c], dst_ref=outs[w], send_sem=send_sems.at[w],
            recv_sem=recv_sems.at[w], device_id=(x, y, 1 - c), device_id_type=MESH) for w in range(n)]
        for cp in copies:
            cp.start()
        for cp in copies:
            cp.wait_recv()
        for cp in copies:
            cp.wait_send()

    return _pcall(
        body, name="grads_core_exchange", in_specs=[hbm] * n, out_specs=[hbm] * n,
        out_shape=[SDS((N_CHIPS,) + a.shape[2:], a.dtype) for a in slabs],
        scratch_shapes=[pltpu.SemaphoreType.DMA((n,)), pltpu.SemaphoreType.DMA((n,))],
    )(*slabs)


def _chip_exchange(partials):
    n = len(partials)
    per = N_CHIPS - 1
    hbm = pl.BlockSpec(memory_space=pltpu.HBM)

    def body(*refs):
        ins, outs = refs[:n], refs[n:2 * n]
        send_sems, recv_sems, local_sems = refs[2 * n:]
        x, y, c = _mesh_pos()
        mine = 2 * x + y
        local, sends, recvs = [], [], []
        for w in range(n):
            cp = pltpu.make_async_copy(ins[w].at[mine], outs[w].at[mine], local_sems.at[w])
            cp.start()
            local.append(cp)
            for j, (px, py) in enumerate(_other_chips(x, y)):
                theirs = 2 * px + py
                sems = dict(send_sem=send_sems.at[w * per + j], recv_sem=recv_sems.at[w * per + j],
                            device_id=(px, py, c), device_id_type=MESH)
                cp = pltpu.make_async_remote_copy(src_ref=ins[w].at[theirs], dst_ref=outs[w].at[mine], **sems)
                cp.start()
                sends.append(cp)
                recvs.append(pltpu.make_async_remote_copy(src_ref=ins[w].at[theirs], dst_ref=outs[w].at[theirs], **sems))
        for cp in recvs:
            cp.wait_recv()
        for cp in sends:
            cp.wait_send()
        for cp in local:
            cp.wait()

    return _pcall(
        body, name="grads_chip_exchange", in_specs=[hbm] * n, out_specs=[hbm] * n,
        out_shape=[SDS(a.shape, a.dtype) for a in partials],
        scratch_shapes=[pltpu.SemaphoreType.DMA((n * per,)), pltpu.SemaphoreType.DMA((n * per,)),
                        pltpu.SemaphoreType.DMA((n,))],
    )(*partials)


def _core_sum(slabs, from_sibling, core, name):
    _, _, r, c = slabs.shape
    tr = min(r, 256)

    def body(core_ref, a_ref, b_ref, o_ref):
        o_ref[...] = (a_ref[0].astype(F32) + b_ref[...].astype(F32)).astype(BF16)

    return _pcall(
        body, name=name,
        grid_spec=pltpu.PrefetchScalarGridSpec(
            num_scalar_prefetch=1, grid=(N_CHIPS, r // tr),
            in_specs=[pl.BlockSpec((1, 1, tr, c), lambda g, i, core: (g, core[0], i, 0)),
                      pl.BlockSpec((1, tr, c), lambda g, i, core: (g, i, 0))],
            out_specs=pl.BlockSpec((1, tr, c), lambda g, i, core: (g, i, 0))),
        out_shape=SDS((N_CHIPS, r, c), BF16), compiler_params=_params(),
    )(core, slabs, from_sibling)


def _adam_math(w, g, m, v):
    m = ADAM_B1 * m + (1.0 - ADAM_B1) * g
    v = ADAM_B2 * v + (1.0 - ADAM_B2) * (g * g)
    m_hat = m / (1.0 - ADAM_B1 ** ADAM_STEP)
    v_hat = v / (1.0 - ADAM_B2 ** ADAM_STEP)
    delta = -ADAM_LR * (m_hat / (jnp.sqrt(v_hat) + ADAM_EPS) + ADAM_WD * w)
    return delta, m, v


def _adamw(recv, w, m, v, name):
    r, c = w.shape
    tr = min(r, 128)
    n_parts = recv.shape[0]

    def body(g_ref, w_ref, m_ref, v_ref, go_ref, d_ref, mo_ref, vo_ref):
        g = g_ref[0].astype(F32)
        for s in range(1, n_parts):
            g = g + g_ref[s].astype(F32)
        delta, mn, vn = _adam_math(w_ref[...], g, m_ref[...], v_ref[...])
        go_ref[...] = g
        d_ref[...] = delta
        mo_ref[...] = mn
        vo_ref[...] = vn

    blk = pl.BlockSpec((tr, c), lambda i: (i, 0))
    return _pcall(
        body, name=name, grid=(r // tr,),
        in_specs=[pl.BlockSpec((n_parts, tr, c), lambda i: (0, i, 0)), blk, blk, blk],
        out_specs=[blk] * 4, out_shape=[SDS((r, c), F32)] * 4, compiler_params=_params(),
    )(recv, w, m, v)


VEC_ROWS = 32


def _small_allreduce_adamw(vec, w, m, v):
    def body(vec_ref, w_ref, m_ref, v_ref, g_ref, d_ref, mo_ref, vo_ref, gath, send_sems, recv_sems):
        pos = _mesh_pos()
        me = 4 * pos[0] + 2 * pos[1] + pos[2]
        sends, recvs = [], []
        for k in range(1, N_DEV):
            peer, peer_idx = _peer(pos, k)
            cp = pltpu.make_async_remote_copy(src_ref=vec_ref, dst_ref=gath.at[me], send_sem=send_sems.at[k - 1],
                                              recv_sem=recv_sems.at[k - 1], device_id=peer, device_id_type=MESH)
            cp.start()
            sends.append(cp)
            recvs.append(pltpu.make_async_remote_copy(src_ref=vec_ref, dst_ref=gath.at[peer_idx],
                                                      send_sem=send_sems.at[k - 1], recv_sem=recv_sems.at[k - 1],
                                                      device_id=peer, device_id_type=MESH))
        gath[me] = vec_ref[...]
        for cp in recvs:
            cp.wait_recv()
        for cp in sends:
            cp.wait_send()
        tot = gath[0]
        for s in range(1, N_DEV):
            tot = tot + gath[s]
        rowi = lax.broadcasted_iota(jnp.int32, (8, LANES), 0)
        mine = jnp.sum(jnp.where(rowi == me, tot[16:24, :], 0.0), axis=0, keepdims=True)
        g = jnp.concatenate([tot[0:16, :], jnp.broadcast_to(mine, (8, LANES)), tot[24:32, :]], axis=0)
        delta, mn, vn = _adam_math(w_ref[...], g, m_ref[...], v_ref[...])
        g_ref[...] = g
        d_ref[...] = delta
        mo_ref[...] = mn
        vo_ref[...] = vn

    vm = pl.BlockSpec(memory_space=pltpu.VMEM)
    return _pcall(
        body, name="small_allreduce_adamw", in_specs=[vm] * 4, out_specs=[vm] * 4,
        out_shape=[SDS((VEC_ROWS, LANES), F32)] * 4,
        scratch_shapes=[pltpu.VMEM((N_DEV, VEC_ROWS, LANES), F32), pltpu.SemaphoreType.DMA((N_DEV - 1,)),
                        pltpu.SemaphoreType.DMA((N_DEV - 1,))],
        compiler_params=pltpu.CompilerParams(has_side_effects=True),
    )(vec, w, m, v)


def _cols_to_slabs(a):
    r, c8 = a.shape
    return a.reshape(r, N_DEV, c8 // N_DEV).transpose(1, 0, 2)


def _slabs_to_cols(a):
    n, r, c = a.shape
    return a.transpose(1, 0, 2).reshape(r, n * c)


def _rows8(vec):
    return vec.reshape(-1, LANES)


def _pad_rows(a, rows):
    return jnp.pad(a, ((0, rows - a.shape[0]), (0, LANES - a.shape[1])))


def kernel(x, p, fox_norm, fox_w_in, fox_b_f, fox_w_out, dil_norm, dil_w_in, dil_w_out, ple_w_up, ple_w_gate, final_norm, loss_target, m_fox_norm, m_fox_w_in, m_fox_b_f, m_fox_w_out, m_dil_norm, m_dil_w_in, m_dil_w_out, m_ple_w_up, m_ple_w_gate, m_final_norm, v_fox_norm, v_fox_w_in, v_fox_b_f, v_fox_w_out, v_dil_norm, v_dil_w_in, v_dil_w_out, v_ple_w_up, v_ple_w_gate, v_final_norm):
    t = x.shape[1]
    d = D_MODEL
    xs, tgt = x[0], loss_target[0]
    p0, p1 = p[0, 0], p[1, 0]
    fox_cols = fox_w_in.shape[2]
    ple_dim = ple_w_up.shape[1]

    shards = [fox_w_in[0].astype(BF16), fox_w_out[0].astype(BF16), dil_w_in[0].astype(BF16),
              dil_w_out[0].astype(BF16), ple_w_up.reshape(-1, LANES).astype(BF16),
              ple_w_gate.reshape(-1, d).astype(BF16), dil_norm]
    gw = _all_gather(shards)
    w_fox_in = _slabs_to_cols(gw[0])
    w_fox_main = w_fox_in[:, :4 * d]
    w_fox_f = jnp.pad(w_fox_in[:, 4 * d:], ((0, 0), (0, LANES - FOX_HEADS)))
    w_fox_out = gw[1].reshape(d, d)
    w_dil_in = _slabs_to_cols(gw[2])
    w_dil_out = gw[3].reshape(d, d)
    w_up = gw[4].reshape(N_DEV, 2, ple_dim, LANES).transpose(1, 2, 0, 3).reshape(2, ple_dim, d)
    w_gate = gw[5].reshape(N_DEV, 2, d // N_DEV, d).transpose(1, 0, 2, 3).reshape(2, d, d)
    dil_norm_full = gw[6].reshape(1, d)
    b_pad = jnp.pad(fox_b_f, ((0, 0), (0, LANES - FOX_HEADS)))

    n0, r0 = _rms_fwd(xs, fox_norm, "rms_fox")
    proj0 = _mm([n0], w_fox_main, "nn", "fox_in_proj", tiles=IN_PROJ_TILES)
    projf = _mm([n0], w_fox_f, "nn", "fox_gate_proj", tiles=IN_PROJ_TILES, out_dtype=F32)
    c_all = _fox_gate_fwd(projf, b_pad)
    qaug_fwd = _fox_aug(c_all, c_all, 1.0, 0.0, 0, FOX_AUG, "fox_aug_q_fwd")
    kaug = _fox_aug(c_all, c_all, -1.0, 0.0, FOX_AUG, 0, "fox_aug_k")
    o0, g0, lse0 = _fox_fwd(proj0, qaug_fwd, kaug)
    h1 = _mm_residual(g0, w_fox_out, "nn", xs, "fox_out_proj")
    h2, u0, a0 = _ple_fwd(h1, p0, w_up[0], w_gate[0], "ple0_fwd")

    n1, r1 = _rms_fwd(h2, dil_norm_full, "rms_dil")
    proj1 = _mm([n1], w_dil_in, "nn", "dil_in_proj", tiles=IN_PROJ_TILES)
    n_heads = len(DIL_PATTERN) * DIL_HEADS
    slopes = 2.0 ** (-ALIBI_MAX_EXP * jnp.arange(1, n_heads + 1, dtype=F32) / n_heads)
    dil_o, dil_lse, dil_slopes = [], [], []
    for grp, (_, dil) in enumerate(DIL_PATTERN):
        sl = (slopes[grp * DIL_HEADS:(grp + 1) * DIL_HEADS] * dil).reshape(DIL_HEADS, 1, 1)
        og, lg = _dil_fwd(proj1, sl, grp, dil, f"dil_attn_fwd_{grp}")
        dil_o.append(og)
        dil_lse.append(lg)
        dil_slopes.append(sl)
    z1_col0 = 9 * d
    o1, g1, lse1 = _dil_mix(dil_o, dil_lse, proj1, z1_col0)
    h3 = _mm_residual(g1, w_dil_out, "nn", h2, "dil_out_proj")
    h4, u1, a1 = _ple_fwd(h3, p1, w_up[1], w_gate[1], "ple1_fwd")

    dh4, d_final_norm, loss_part = _final_bwd(h4, final_norm.reshape(1, d), tgt)

    du1, da1 = _ple_bwd_elem(dh4, u1, a1, "ple1_bwd_elem")
    dw_up1 = _dw(p1, du1, "ple1_dw_up")
    dw_gate1 = _dw(h3, da1, "ple1_dw_gate")
    dh3 = _mm_residual(da1, w_gate[1], "nt", dh4, "ple1_dh")

    dw_dil_out = _dw(g1, dh3, "dil_dw_out")
    do1, dz1, delta1 = _mm_gate_bwd(dh3, w_dil_out, proj1, z1_col0, o1, DIL_HEADS, "dil_dgate")
    dqs, dks, dvs = [], [], []
    for grp, (_, dil) in enumerate(DIL_PATTERN):
        dq, dk, dv = _dil_bwd(proj1, do1, lse1, delta1, dil_slopes[grp], grp, dil, f"dil_attn_bwd_{grp}")
        dqs.append(dq)
        dks.append(dk)
        dvs.append(dv)
    dproj1 = dqs + dks + dvs + [dz1]
    dw_dil_in = jnp.concatenate([_dw(n1, dpart, f"dil_dw_in_{s}") for s, dpart in enumerate(dproj1)], axis=1)
    dh2, d_dil_norm = _mm_in_bwd(dproj1, w_dil_in, h2, dil_norm_full, r1, dh3, "dil_dx")

    du0, da0 = _ple_bwd_elem(dh2, u0, a0, "ple0_bwd_elem")
    dw_up0 = _dw(p0, du0, "ple0_dw_up")
    dw_gate0 = _dw(h1, da0, "ple0_dw_gate")
    dh1 = _mm_residual(da0, w_gate[0], "nt", dh2, "ple0_dh")

    dw_fox_out = _dw(g0, dh1, "fox_dw_out")
    do0, dz0, delta0 = _mm_gate_bwd(dh1, w_fox_out, proj0, 3 * d, o0, FOX_HEADS, "fox_dgate")
    head_cols = lambda a: jnp.pad(a, ((0, 0), (0, LANES - FOX_HEADS)))
    lse_cols = head_cols(lse0.reshape(FOX_HEADS, t).T)
    delta_cols = head_cols(delta0.reshape(FOX_HEADS, t).T)
    qaug_bwd = _fox_aug(c_all, lse_cols, 1.0, -1.0, 0, FOX_AUG, "fox_aug_q_bwd")
    doaug = _fox_aug(delta_cols, delta_cols, -1.0, 0.0, 0, None, "fox_aug_do")
    dq0, dk0, dv0, dck_wide, dcq = _fox_bwd(proj0, do0, qaug_bwd, kaug, doaug)
    dc_query, dc_key = _fox_unpack_dc(dck_wide, dcq)
    df, d_b_f = _fox_gate_bwd(projf, b_pad, head_cols(dc_query), head_cols(dc_key))
    dproj0 = [dq0, dk0, dv0, dz0]
    dw_fox_parts = [_dw(n0, dpart, f"fox_dw_in_{s}") for s, dpart in enumerate(dproj0)]
    dw_fox_f = _dw(n0, df, "fox_dw_gate")
    dn0_f = _mm([df], w_fox_f, "nt", "fox_dx_gate", out_dtype=F32)
    grad_x, d_fox_norm = _mm_in_bwd(dproj0, w_fox_main, xs, fox_norm, r0, dh1, "fox_dx", more=dn0_f)

    dw_fox_in = jnp.concatenate(dw_fox_parts + [dw_fox_f[:, :FOX_HEADS]], axis=1)
    slabs = [_cols_to_slabs(dw_fox_in), dw_fox_out.reshape(N_DEV, d // N_DEV, d), _cols_to_slabs(dw_dil_in),
             dw_dil_out.reshape(N_DEV, d // N_DEV, d),
             jnp.stack([dw_up0, dw_up1]).reshape(2, ple_dim, N_DEV, LANES).transpose(2, 0, 1, 3).reshape(N_DEV, -1, LANES),
             jnp.stack([dw_gate0, dw_gate1]).reshape(2, N_DEV, d // N_DEV, d).transpose(1, 0, 2, 3).reshape(N_DEV, -1, d)]
    names = ["fox_w_in", "fox_w_out", "dil_w_in", "dil_w_out", "ple_w_up", "ple_w_gate"]
    slabs = [s.reshape((N_CHIPS, 2) + s.shape[1:]) for s in slabs]
    from_sibling = _core_exchange(slabs)
    core = lax.axis_index("c").astype(jnp.int32).reshape(1)
    chip_sums = [_core_sum(s, f, core, "core_sum_" + nm) for s, f, nm in zip(slabs, from_sibling, names)]
    recv = _chip_exchange(chip_sums)
    big = [(fox_w_in, m_fox_w_in, v_fox_w_in), (fox_w_out, m_fox_w_out, v_fox_w_out),
           (dil_w_in, m_dil_w_in, v_dil_w_in), (dil_w_out, m_dil_w_out, v_dil_w_out),
           (ple_w_up, m_ple_w_up, v_ple_w_up), (ple_w_gate, m_ple_w_gate, v_ple_w_gate)]
    upd = {}
    for rv, (w, m, v), nm in zip(recv, big, names):
        shp2 = rv.shape[1:]
        res = _adamw(rv, w.reshape(shp2), m.reshape(shp2), v.reshape(shp2), "adamw_" + nm)
        upd[nm] = [a.reshape(w.shape) for a in res]

    loss_row = jnp.where(jnp.arange(LANES) == 0, loss_part, 0.0)
    vec = jnp.concatenate([_rows8(d_fox_norm), _rows8(d_final_norm), _rows8(d_dil_norm), d_b_f, loss_row,
                           jnp.zeros((VEC_ROWS - 26, LANES), F32)], axis=0)

    def small_pack(a_fox_norm, a_final_norm, a_dil_norm, a_b_f):
        return jnp.concatenate([_rows8(a_fox_norm), _rows8(a_final_norm), _pad_rows(a_dil_norm, 8),
                                _pad_rows(a_b_f, 8)], axis=0)

    sg, sd, sm, sv = _small_allreduce_adamw(
        vec, small_pack(fox_norm, final_norm, dil_norm, fox_b_f),
        small_pack(m_fox_norm, m_final_norm, m_dil_norm, m_fox_b_f),
        small_pack(v_fox_norm, v_final_norm, v_dil_norm, v_fox_b_f))

    def small_unpack(a):
        return {"fox_norm": a[0:8].reshape(1, d), "final_norm": a[8:16].reshape(d), "dil_norm": a[16:17],
                "fox_b_f": a[24:25, :FOX_HEADS]}

    loss = sg[25, 0]
    order = ["fox_norm", "fox_w_in", "fox_b_f", "fox_w_out", "dil_norm", "dil_w_in", "dil_w_out", "ple_w_up",
             "ple_w_gate", "final_norm"]
    out = [loss, grad_x[None]]
    for idx, small in enumerate((sg, sd, sm, sv)):
        sp = small_unpack(small)
        out += [sp[nm] if nm in sp else upd[nm][idx] for nm in order]
    return tuple(out)
```

```python
import functools

import jax
import jax.numpy as jnp
from jax import lax
from jax.experimental import pallas as pl
from jax.experimental.pallas import tpu as pltpu

F32 = jnp.float32
BF16 = jnp.bfloat16
SDS = jax.ShapeDtypeStruct

D_MODEL = 1024
N_DEV = 8
LANES = 128
FOX_HEADS = 16
FOX_HEAD_DIM = 64
FOX_PAIRS = FOX_HEADS // 2
DIL_HEADS = 8
DIL_BLOCK = 128
DIL_PATTERN = ((128, 1), (512, 4), (2048, 16))
ALIBI_MAX_EXP = 8.0
RMS_EPS = 1e-6
ADAM_LR, ADAM_B1, ADAM_B2, ADAM_EPS, ADAM_WD, ADAM_STEP = 0.001, 0.9, 0.999, 1e-08, 0.01, 10
VMEM_LIMIT = 48 * 1024 * 1024
NEG_INF = float("-inf")

NN = (((1,), (0,)), ((), ()))
NT = (((1,), (1,)), ((), ()))
TN = (((0,), (0,)), ((), ()))
MESH = pl.DeviceIdType.MESH


def _pcall(body, **kw):
    return pl.pallas_call(body, **kw)


def _params(**kw):
    return pltpu.CompilerParams(vmem_limit_bytes=VMEM_LIMIT, **kw)


def _dot(a, b, dims):
    return lax.dot_general(a, b, dims, preferred_element_type=F32)


def _sigmoid(x):
    return 1.0 / (1.0 + jnp.exp(-x))


def _mm(a_parts, b, mode, name, tiles=(512, 1024, 1024), extras=(), outs=None, epilogue=None, out_dtype=BF16):
    na = len(a_parts)
    if mode == "tn":
        k_part, m = a_parts[0].shape
        n = b.shape[1]
    else:
        m, k_part = a_parts[0].shape
        n = b.shape[1] if mode == "nn" else b.shape[0]
    tm, tn, tk = min(tiles[0], m), min(tiles[1], n), min(tiles[2], k_part)
    kb = k_part // tk
    nk = na * kb
    grid = (m // tm, n // tn, nk)

    in_specs = []
    for s in range(na):
        if mode == "tn":
            in_specs.append(pl.BlockSpec((tk, tm), lambda i, j, k: (k, i)))
        else:
            in_specs.append(pl.BlockSpec((tm, tk), lambda i, j, k, s=s: (i, jnp.clip(k - s * kb, 0, kb - 1))))
    if mode == "nt":
        in_specs.append(pl.BlockSpec((tn, tk), lambda i, j, k: (j, k)))
    else:
        in_specs.append(pl.BlockSpec((tk, tn), lambda i, j, k: (k, j)))
    for _, blk, imap in extras:
        in_specs.append(pl.BlockSpec(blk, imap))
    if outs is None:
        outs = [(SDS((m, n), out_dtype), (tm, tn), lambda i, j, k: (i, j))]
    out_specs = [pl.BlockSpec(blk, imap) for _, blk, imap in outs]
    ne, no = len(extras), len(outs)
    dims = {"nn": NN, "nt": NT, "tn": TN}[mode]

    def finish(res, e_refs, o_refs, i):
        if epilogue is None:
            o_refs[0][...] = res.astype(o_refs[0].dtype)
        else:
            epilogue(res, e_refs, o_refs, i)

    def body(*refs):
        a_refs = refs[:na]
        b_ref = refs[na]
        e_refs = refs[na + 1:na + 1 + ne]
        o_refs = refs[na + 1 + ne:na + 1 + ne + no]
        i, k = pl.program_id(0), pl.program_id(2)
        if nk == 1:
            finish(_dot(a_refs[0][...].astype(BF16), b_ref[...].astype(BF16), dims), e_refs, o_refs, i)
            return
        acc = refs[-1]

        @pl.when(k == 0)
        def _():
            acc[...] = jnp.zeros_like(acc)

        def step(a_ref):
            acc[...] += _dot(a_ref[...].astype(BF16), b_ref[...].astype(BF16), dims)

        for s in range(na):
            if na == 1:
                step(a_refs[0])
            else:
                pl.when((k >= s * kb) & (k < (s + 1) * kb))(functools.partial(step, a_refs[s]))

        @pl.when(k == nk - 1)
        def _():
            finish(acc[...], e_refs, o_refs, i)

    res = _pcall(
        body, name=name, grid=grid, in_specs=in_specs, out_specs=out_specs,
        out_shape=[o[0] for o in outs], scratch_shapes=[] if nk == 1 else [pltpu.VMEM((tm, tn), F32)],
        compiler_params=_params(dimension_semantics=("arbitrary", "arbitrary", "arbitrary")),
    )(*a_parts, b, *[e[0] for e in extras])
    return res[0] if len(res) == 1 else res


IN_PROJ_TILES = (1024, 1024, 1024)
DW_TILES = (1024, 1024, 512)


def _dw(x, dy, name):
    return _mm([x], dy, "tn", name, tiles=DW_TILES)


def _add_extra_epilogue(acc, e_refs, o_refs, i):
    o_refs[0][...] = acc + e_refs[0][...]


def _mm_residual(a, b, mode, res, name):
    m = a.shape[0]
    n = b.shape[1] if mode == "nn" else b.shape[0]
    tm, tn = 512, 1024
    return _mm([a], b, mode, name, tiles=(tm, tn, 1024),
               extras=[(res, (tm, tn), lambda i, j, k: (i, j))],
               outs=[(SDS((m, n), F32), (tm, tn), lambda i, j, k: (i, j))],
               epilogue=_add_extra_epilogue)


def _rms_fwd(h, g, name):
    t, d = h.shape
    tm = 512

    def body(h_ref, g_ref, n_ref, r_ref):
        x = h_ref[...]
        r = lax.rsqrt(jnp.mean(x * x, axis=-1, keepdims=True) + RMS_EPS)
        n_ref[...] = ((x * r) * g_ref[...]).astype(BF16)
        r_ref[...] = r

    return _pcall(
        body, name=name, grid=(t // tm,),
        in_specs=[pl.BlockSpec((tm, d), lambda i: (i, 0)), pl.BlockSpec((1, d), lambda i: (0, 0))],
        out_specs=[pl.BlockSpec((tm, d), lambda i: (i, 0)), pl.BlockSpec((tm, 1), lambda i: (i, 0))],
        out_shape=[SDS((t, d), BF16), SDS((t, 1), F32)],
        compiler_params=_params(),
    )(h, g)


def _rms_bwd_rows(dn, x, g, r):
    xhat = x * r
    dxhat = dn * g
    dx = r * (dxhat - xhat * jnp.mean(dxhat * xhat, axis=-1, keepdims=True))
    dg = jnp.sum(dn * xhat, axis=0, keepdims=True)
    return dx, dg


def _mm_in_bwd(d_parts, w, h, g, r, dres, name, more=None):
    t = h.shape[0]
    tm = 512
    tk = 1024 if len(d_parts) <= 4 else 512
    row = lambda i, j, k: (i, 0)
    extras = [(h, (tm, D_MODEL), row), (g, (1, D_MODEL), lambda i, j, k: (0, 0)), (r, (tm, 1), row),
              (dres, (tm, D_MODEL), row)]
    if more is not None:
        extras.append((more, (tm, D_MODEL), row))

    def epilogue(acc, e_refs, o_refs, i):
        dn = acc if more is None else acc + e_refs[4][...]
        dx, dg = _rms_bwd_rows(dn, e_refs[0][...], e_refs[1][...], e_refs[2][...])
        o_refs[0][...] = e_refs[3][...] + dx

        @pl.when(i == 0)
        def _():
            o_refs[1][...] = dg

        @pl.when(i > 0)
        def _():
            o_refs[1][...] += dg

    return _mm(d_parts, w, "nt", name, tiles=(tm, D_MODEL, tk), extras=extras,
               outs=[(SDS((t, D_MODEL), F32), (tm, D_MODEL), row),
                     (SDS((1, D_MODEL), F32), (1, D_MODEL), lambda i, j, k: (0, 0))],
               epilogue=epilogue)


def _final_bwd(h, g, tgt):
    t, d = h.shape
    tm = 256

    def body(h_ref, g_ref, t_ref, dh_ref, dg_ref, loss_ref):
        i = pl.program_id(0)
        x = h_ref[...]
        gg = g_ref[...]
        r = lax.rsqrt(jnp.mean(x * x, axis=-1, keepdims=True) + RMS_EPS)
        err = (x * r) * gg - t_ref[...]
        part = 0.5 * jnp.sum(jnp.mean(err * err, axis=-1, keepdims=True), axis=0, keepdims=True)
        dx, dg = _rms_bwd_rows(err * (1.0 / d), x, gg, r)
        dh_ref[...] = dx

        @pl.when(i == 0)
        def _():
            dg_ref[...] = dg
            loss_ref[...] = jnp.broadcast_to(part, loss_ref.shape)

        @pl.when(i > 0)
        def _():
            dg_ref[...] += dg
            loss_ref[...] += jnp.broadcast_to(part, loss_ref.shape)

    return _pcall(
        body, name="final_norm_loss", grid=(t // tm,),
        in_specs=[pl.BlockSpec((tm, d), lambda i: (i, 0)), pl.BlockSpec((1, d), lambda i: (0, 0)),
                  pl.BlockSpec((tm, d), lambda i: (i, 0))],
        out_specs=[pl.BlockSpec((tm, d), lambda i: (i, 0)), pl.BlockSpec((1, d), lambda i: (0, 0)),
                   pl.BlockSpec((1, LANES), lambda i: (0, 0))],
        out_shape=[SDS((t, d), F32), SDS((1, d), F32), SDS((1, LANES), F32)],
        compiler_params=_params(),
    )(h, g, tgt)


GATE_ROWS = 256


def _split3(x):
    hi = x.astype(BF16)
    r1 = x - hi.astype(F32)
    mid = r1.astype(BF16)
    lo = (r1 - mid.astype(F32)).astype(BF16)
    return hi, mid, lo


def _tri_sum(x, upper):
    rows = x.shape[0]
    ri = lax.broadcasted_iota(jnp.int32, (rows, rows), 0)
    ci = lax.broadcasted_iota(jnp.int32, (rows, rows), 1)
    tri = jnp.where((ri <= ci) if upper else (ri >= ci), 1.0, 0.0).astype(BF16)
    hi, mid, lo = _split3(x)
    return _dot(tri, hi, NN) + _dot(tri, mid, NN) + _dot(tri, lo, NN)


def _log_sigmoid(x):
    return jnp.minimum(x, 0.0) - jnp.log1p(jnp.exp(-jnp.abs(x)))


def _fox_gate_fwd(projf, bpad):
    t = projf.shape[0]
    tb = GATE_ROWS

    def body(x_ref, b_ref, c_ref, carry):
        i = pl.program_id(0)

        @pl.when(i == 0)
        def _():
            carry[...] = jnp.zeros_like(carry)

        c_ref[...] = _tri_sum(_log_sigmoid(x_ref[...] + b_ref[...]), upper=False) + carry[...]
        carry[...] = c_ref[pl.ds(tb - 1, 1), :]

    return _pcall(
        body, name="fox_gate_fwd", grid=(t // tb,),
        in_specs=[pl.BlockSpec((tb, LANES), lambda i: (i, 0)), pl.BlockSpec((1, LANES), lambda i: (0, 0))],
        out_specs=pl.BlockSpec((tb, LANES), lambda i: (i, 0)),
        out_shape=SDS((t, LANES), F32), scratch_shapes=[pltpu.VMEM((1, LANES), F32)],
        compiler_params=_params(),
    )(projf, bpad)


def _fox_gate_bwd(projf, bpad, dc_query, dc_key):
    t = projf.shape[0]
    tb = GATE_ROWS
    nb = t // tb

    def body(x_ref, b_ref, dcq_ref, dck_ref, df_ref, db_ref, carry, buf):
        i = pl.program_id(0)

        @pl.when(i == 0)
        def _():
            carry[...] = jnp.zeros_like(carry)

        buf[...] = _tri_sum(dcq_ref[...] - dck_ref[...], upper=True) + carry[...]
        carry[...] = buf[pl.ds(0, 1), :]
        df = buf[...] * _sigmoid(-(x_ref[...] + b_ref[...]))
        df_ref[...] = df.astype(BF16)
        part = jnp.sum(df, axis=0, keepdims=True)

        @pl.when(i == 0)
        def _():
            db_ref[...] = part

        @pl.when(i > 0)
        def _():
            db_ref[...] += part

    rev = lambda i: (nb - 1 - i, 0)
    return _pcall(
        body, name="fox_gate_bwd", grid=(nb,),
        in_specs=[pl.BlockSpec((tb, LANES), rev), pl.BlockSpec((1, LANES), lambda i: (0, 0)),
                  pl.BlockSpec((tb, LANES), rev), pl.BlockSpec((tb, LANES), rev)],
        out_specs=[pl.BlockSpec((tb, LANES), rev), pl.BlockSpec((1, LANES), lambda i: (0, 0))],
        out_shape=[SDS((t, LANES), BF16), SDS((1, LANES), F32)],
        scratch_shapes=[pltpu.VMEM((1, LANES), F32), pltpu.VMEM((tb, LANES), F32)],
        compiler_params=_params(),
    )(projf, bpad, dc_query, dc_key)


FOX_TQ = 512
FOX_SCALE = FOX_HEAD_DIM ** -0.5


def _low_lanes(shape):
    return lax.broadcasted_iota(jnp.int32, shape, len(shape) - 1) < FOX_HEAD_DIM


FOX_AUG = 3


def _top_rows(shape):
    return lax.broadcasted_iota(jnp.int32, shape, 0) < FOX_HEAD_DIM


def _fox_aug(a, b, sign_a, sign_b, piece_entry, ones_entry, name):
    t = a.shape[0]
    tb = 512

    def body(a_ref, b_ref, o_ref):
        x = sign_a * a_ref[...]
        if sign_b != 0.0:
            x = x + sign_b * b_ref[...]
        head = lax.broadcasted_iota(jnp.int32, (LANES, D_MODEL), 0)
        col = lax.broadcasted_iota(jnp.int32, (LANES, D_MODEL), 1)
        base = (head // 2) * LANES + (1 - head % 2) * FOX_HEAD_DIM + piece_entry
        acc = jnp.zeros((tb, D_MODEL), F32)
        for e, piece in enumerate(_split3(x)):
            place = jnp.where((head < FOX_HEADS) & (col == base + e), 1.0, 0.0).astype(BF16)
            acc = acc + _dot(piece, place, NN)
        if ones_entry is not None:
            ent = lax.broadcasted_iota(jnp.int32, (1, D_MODEL), 1) % FOX_HEAD_DIM
            acc = acc + jnp.where((ent >= ones_entry) & (ent < ones_entry + FOX_AUG), 1.0, 0.0)
        o_ref[...] = acc.astype(BF16)

    blk = pl.BlockSpec((tb, LANES), lambda i: (i, 0))
    return _pcall(
        body, name=name, grid=(t // tb,), in_specs=[blk, blk],
        out_specs=pl.BlockSpec((tb, D_MODEL), lambda i: (i, 0)), out_shape=SDS((t, D_MODEL), BF16),
        compiler_params=_params(),
    )(a, b)


def _fox_unpack_dc(dck_wide, dcq):
    t = dck_wide.shape[0]
    dck = dck_wide.reshape(t, FOX_PAIRS, 2, FOX_HEAD_DIM)[:, :, ::-1, 0].reshape(t, FOX_HEADS)
    return dcq[:, :, 0, :].reshape(FOX_HEADS, t).T, dck


def _causal_steps(nq, key_major):
    if key_major:
        pairs = [(i, j) for j in range(nq) for i in range(j, nq)]
    else:
        pairs = [(i, j) for i in range(nq) for j in range(i + 1)]
    return (jnp.asarray([p[0] for p in pairs], jnp.int32), jnp.asarray([p[1] for p in pairs], jnp.int32))


def _pair_operand(low, own, other, hh):
    return jnp.where(low, own, other) if hh == 0 else jnp.where(low, other, own)


def _fox_fwd(proj, qaug, kaug):
    t = proj.shape[0]
    tq = tk = min(FOX_TQ, t)
    nq = t // tq
    cb = D_MODEL // LANES

    i_tab, j_tab = _causal_steps(nq, key_major=False)

    def body(i_ref, j_ref, q_ref, k_ref, v_ref, z_ref, qa_ref, ka_ref, o_ref, g_ref, lse_ref, m_s, l_s, acc_s):
        step = pl.program_id(1)
        i, j = i_ref[step], j_ref[step]

        @pl.when(j == 0)
        def _():
            m_s[...] = jnp.full_like(m_s, NEG_INF)
            l_s[...] = jnp.zeros_like(l_s)
            acc_s[...] = jnp.zeros_like(acc_s)

        low = _low_lanes((tq, LANES))
        top = _top_rows((LANES, tq))

        def update(masked):
            qs = q_ref[...] * FOX_SCALE
            qa, k, ka, v = qa_ref[...], k_ref[...], ka_ref[...], v_ref[...]
            if masked:
                causal = (lax.broadcasted_iota(jnp.int32, (tk, tq), 0) <= lax.broadcasted_iota(jnp.int32, (tk, tq), 1))
            upd = []
            scores = [_dot(_pair_operand(low, k, ka, hh), _pair_operand(low, qs, qa, hh), NT) for hh in range(2)]
            for hh in range(2):
                s = scores[hh]
                if masked:
                    s = jnp.where(causal, s, NEG_INF)
                m_prev = m_s[hh]
                m_new = jnp.maximum(m_prev, jnp.max(s, axis=0, keepdims=True))
                alpha = jnp.exp(m_prev - m_new)
                p = jnp.exp(s - m_new)
                l_s[hh] = alpha * l_s[hh] + jnp.sum(p, axis=0, keepdims=True)
                m_s[hh] = m_new
                upd.append((alpha, _dot(v, p.astype(BF16), TN)))
            acc = acc_s[...]
            acc_s[...] = jnp.where(top, acc * upd[0][0] + upd[0][1], acc * upd[1][0] + upd[1][1])

        pl.when(j < i)(functools.partial(update, False))
        pl.when(j == i)(functools.partial(update, True))

        @pl.when(j == i)
        def _():
            o = (acc_s[...] / jnp.where(top, l_s[0], l_s[1])).T
            z = z_ref[...].astype(F32)
            o_ref[...] = o.astype(BF16)
            g_ref[...] = (o * (z * _sigmoid(z))).astype(BF16)
            for hh in range(2):
                lse_ref[hh] = m_s[hh] + jnp.log(l_s[hh])

    qblk = lambda col: pl.BlockSpec((tq, LANES), lambda h, s, it, jt: (it[s], col + h))
    kblk = lambda col: pl.BlockSpec((tk, LANES), lambda h, s, it, jt: (jt[s], col + h))
    return _pcall(
        body, name="fox_attn_fwd",
        grid_spec=pltpu.PrefetchScalarGridSpec(
            num_scalar_prefetch=2, grid=(FOX_PAIRS, i_tab.shape[0]),
            in_specs=[qblk(0), kblk(cb), kblk(2 * cb), qblk(3 * cb), qblk(0), kblk(0)],
            out_specs=[qblk(0), qblk(0), pl.BlockSpec((2, 1, tq), lambda h, s, it, jt: (h, 0, it[s]))],
            scratch_shapes=[pltpu.VMEM((2, 1, tq), F32), pltpu.VMEM((2, 1, tq), F32), pltpu.VMEM((LANES, tq), F32)]),
        out_shape=[SDS((t, D_MODEL), BF16), SDS((t, D_MODEL), BF16), SDS((FOX_HEADS, 1, t), F32)],
        compiler_params=_params(dimension_semantics=("arbitrary", "arbitrary")),
    )(i_tab, j_tab, proj, proj, proj, proj, qaug, kaug)


FOX_SUM_ROWS = 8


def _fox_bwd(proj, do, qaug, kaug, doaug):
    t = proj.shape[0]
    tq = tk = min(FOX_TQ, t)
    nq = t // tq
    cb = D_MODEL // LANES

    i_tab, j_tab = _causal_steps(nq, key_major=True)

    def body(i_ref, j_ref, q_ref, k_ref, v_ref, do_ref, qa_ref, ka_ref, da_ref,
             dq_ref, dk_ref, dv_ref, dck_ref, dcq_ref, dq_acc, dcq_acc, dk_acc, dks_acc, dv_acc):
        step = pl.program_id(1)
        i, j = i_ref[step], j_ref[step]
        low = _low_lanes((tq, LANES))
        top = _top_rows((LANES, tq))

        @pl.when(i == j)
        def _():
            dk_acc[...] = jnp.zeros_like(dk_acc)
            dks_acc[...] = jnp.zeros_like(dks_acc)
            dv_acc[...] = jnp.zeros_like(dv_acc)

        def update(masked):
            qs = q_ref[...] * FOX_SCALE
            k, v, dout = k_ref[...], v_ref[...], do_ref[...]
            qa, ka, da = qa_ref[...], ka_ref[...], da_ref[...]
            lane = lax.broadcasted_iota(jnp.int32, (tk, LANES), 1)
            vone = jnp.where((lane & (FOX_HEAD_DIM - 1)) < FOX_AUG, 1.0, 0.0).astype(v.dtype)
            one = jnp.ones_like(k)
            if masked:
                causal = (lax.broadcasted_iota(jnp.int32, (tk, tq), 0) <= lax.broadcasted_iota(jnp.int32, (tk, tq), 1))
            parts = []
            scores = [_dot(_pair_operand(low, k, ka, hh), _pair_operand(low, qs, qa, hh), NT) for hh in range(2)]
            dps = [_dot(_pair_operand(low, v, vone, hh), _pair_operand(low, dout, da, hh), NT) for hh in range(2)]
            for hh in range(2):
                s = scores[hh]
                if masked:
                    s = jnp.where(causal, s, NEG_INF)
                p = jnp.exp(s)
                ds = p * dps[hh]
                pb = p.astype(BF16)
                dsb = ds.astype(BF16)
                parts.append((_dot(pb, dout, NN),
                              _dot(dsb, _pair_operand(low, qs, one, hh), NN),
                              _dot(_pair_operand(low, k, one, hh), dsb, TN)))
            dv_acc[...] += jnp.where(low, parts[0][0], parts[1][0])
            dk_acc[...] += jnp.where(low, parts[0][1], parts[1][1])
            dks_acc[...] += jnp.where(low, parts[1][1], parts[0][1])
            dq_t = jnp.where(top, parts[0][2], parts[1][2]) * FOX_SCALE
            sum_a = parts[0][2][FOX_HEAD_DIM:FOX_HEAD_DIM + FOX_SUM_ROWS, :]
            sum_b = parts[1][2][0:FOX_SUM_ROWS, :]

            @pl.when(j == 0)
            def _():
                dq_acc[i] = dq_t
                dcq_acc[0, i] = sum_a
                dcq_acc[1, i] = sum_b

            @pl.when(j > 0)
            def _():
                dq_acc[i] += dq_t
                dcq_acc[0, i] += sum_a
                dcq_acc[1, i] += sum_b

        pl.when(i > j)(functools.partial(update, False))
        pl.when(i == j)(functools.partial(update, True))

        @pl.when(i == nq - 1)
        def _():
            dk_ref[...] = dk_acc[...].astype(BF16)
            dv_ref[...] = dv_acc[...].astype(BF16)
            dck_ref[...] = dks_acc[...]

        @pl.when((i == nq - 1) & (j == nq - 1))
        def _():
            for blk in range(nq):
                dq_ref[blk * tq:(blk + 1) * tq, :] = dq_acc[blk].T.astype(BF16)
            dcq_ref[...] = dcq_acc[...]

    qblk = lambda col: pl.BlockSpec((tq, LANES), lambda h, s, it, jt: (it[s], col + h))
    kblk = lambda col: pl.BlockSpec((tk, LANES), lambda h, s, it, jt: (jt[s], col + h))
    return _pcall(
        body, name="fox_attn_bwd",
        grid_spec=pltpu.PrefetchScalarGridSpec(
            num_scalar_prefetch=2, grid=(FOX_PAIRS, i_tab.shape[0]),
            in_specs=[qblk(0), kblk(cb), kblk(2 * cb), qblk(0), qblk(0), kblk(0), qblk(0)],
            out_specs=[pl.BlockSpec((t, LANES), lambda h, s, it, jt: (0, h)), kblk(0), kblk(0), kblk(0),
                       pl.BlockSpec((2, nq, FOX_SUM_ROWS, tq), lambda h, s, it, jt: (h, 0, 0, 0))],
            scratch_shapes=[pltpu.VMEM((nq, LANES, tq), F32), pltpu.VMEM((2, nq, FOX_SUM_ROWS, tq), F32),
                            pltpu.VMEM((tk, LANES), F32), pltpu.VMEM((tk, LANES), F32), pltpu.VMEM((tk, LANES), F32)]),
        out_shape=[SDS((t, D_MODEL), BF16), SDS((t, D_MODEL), BF16), SDS((t, D_MODEL), BF16),
                   SDS((t, D_MODEL), F32), SDS((FOX_HEADS, nq, FOX_SUM_ROWS, tq), F32)],
        compiler_params=_params(dimension_semantics=("arbitrary", "arbitrary")),
    )(i_tab, j_tab, proj, proj, proj, do, qaug, kaug, doaug)


def _mm_gate_bwd(dh, w_out, z_src, z_col0, o, heads, name):
    t = dh.shape[0]
    tm = 512
    row = lambda i, j, k: (i, 0)
    zcb = z_col0 // D_MODEL

    def epilogue(acc, e_refs, o_refs, i):
        z = e_refs[0][...].astype(F32)
        ov = e_refs[1][...].astype(F32)
        sg = _sigmoid(z)
        dout = acc * (z * sg)
        o_refs[0][...] = dout.astype(BF16)
        o_refs[1][...] = (acc * ov * (sg * (1.0 + z * (1.0 - sg)))).astype(BF16)
        prod = dout * ov
        for cbk in range(D_MODEL // LANES):
            seg = prod[:, cbk * LANES:(cbk + 1) * LANES]
            tot = jnp.sum(seg, axis=-1, keepdims=True)
            if heads == D_MODEL // LANES:
                o_refs[2][cbk] = tot
            else:
                lo = jnp.sum(jnp.where(_low_lanes(seg.shape), seg, 0.0), axis=-1, keepdims=True)
                o_refs[2][2 * cbk] = lo
                o_refs[2][2 * cbk + 1] = tot - lo

    return _mm([dh], w_out, "nt", name, tiles=(tm, D_MODEL, D_MODEL),
               extras=[(z_src, (tm, D_MODEL), lambda i, j, k: (i, zcb)), (o, (tm, D_MODEL), row)],
               outs=[(SDS((t, D_MODEL), BF16), (tm, D_MODEL), row), (SDS((t, D_MODEL), BF16), (tm, D_MODEL), row),
                     (SDS((heads, t, 1), F32), (heads, tm, 1), lambda i, j, k: (0, i, 0))],
               epilogue=epilogue)


def _ple_fwd(h, pin, w_up, w_gate, name):
    t = h.shape[0]
    tm = 512
    pd = pin.shape[1]

    def body(h_ref, p_ref, wu_ref, wg_ref, hn_ref, u_ref, a_ref):
        h = h_ref[...]
        u = _dot(p_ref[...].astype(BF16), wu_ref[...], NN)
        a = _dot(h.astype(BF16), wg_ref[...], NN)
        hn_ref[...] = h + u * _sigmoid(a)
        u_ref[...] = u.astype(BF16)
        a_ref[...] = a.astype(BF16)

    rows = pl.BlockSpec((tm, D_MODEL), lambda i: (i, 0))
    return _pcall(
        body, name=name, grid=(t // tm,),
        in_specs=[rows, pl.BlockSpec((tm, pd), lambda i: (i, 0)),
                  pl.BlockSpec((pd, D_MODEL), lambda i: (0, 0)), pl.BlockSpec((D_MODEL, D_MODEL), lambda i: (0, 0))],
        out_specs=[rows, rows, rows],
        out_shape=[SDS((t, D_MODEL), F32), SDS((t, D_MODEL), BF16), SDS((t, D_MODEL), BF16)],
        compiler_params=_params(),
    )(h, pin, w_up, w_gate)


def _ple_bwd_elem(dh, u, a, name):
    t = dh.shape[0]
    tm = 512

    def body(dh_ref, u_ref, a_ref, du_ref, da_ref):
        g = dh_ref[...]
        s = _sigmoid(a_ref[...].astype(F32))
        du_ref[...] = (g * s).astype(BF16)
        da_ref[...] = (g * u_ref[...].astype(F32) * (s * (1.0 - s))).astype(BF16)

    blk = pl.BlockSpec((tm, D_MODEL), lambda i: (i, 0))
    return _pcall(
        body, name=name, grid=(t // tm,), in_specs=[blk, blk, blk], out_specs=[blk, blk],
        out_shape=[SDS((t, D_MODEL), BF16), SDS((t, D_MODEL), BF16)], compiler_params=_params(),
    )(dh, u, a)


DIL_SCALE = LANES ** -0.5


def _dil_masks():
    ii = lax.broadcasted_iota(jnp.int32, (DIL_BLOCK, DIL_BLOCK), 0)
    jj = lax.broadcasted_iota(jnp.int32, (DIL_BLOCK, DIL_BLOCK), 1)
    return ii, jj


DIL_UNITS = 16
BNT = (((2,), (2,)), ((0,), (0,)))
BNN = (((2,), (1,)), ((0,), (0,)))
BTN = (((1,), (1,)), ((0,), (0,)))


def _dil_units(dil):
    return [(b, r) for b in range(DIL_UNITS // dil) for r in range(dil)]


def _unit_rows(b, r, dil):
    return pl.ds(b * DIL_BLOCK * dil + r, DIL_BLOCK, stride=dil)


def _gather_units(cur, dil, shift=0, edge=None, lead=()):
    nbk = DIL_UNITS // dil
    parts = []
    for b, r in _dil_units(dil):
        bb = b + shift
        if 0 <= bb < nbk:
            parts.append(cur[lead + (_unit_rows(bb, r, dil), slice(None))])
        else:
            parts.append(edge[lead + (pl.ds(r, DIL_BLOCK, stride=dil), slice(None))])
    return jnp.stack(parts)


def _scatter_units(dst, val, dil):
    for u, (b, r) in enumerate(_dil_units(dil)):
        dst[_unit_rows(b, r, dil), :] = val[u]


def _dil_bias(slope, prev):
    ii, jj = _dil_masks()
    dist = (DIL_BLOCK + ii - jj) if prev else (ii - jj)
    return (slope * dist.astype(F32))[None], ((jj >= ii) if prev else (jj <= ii))[None]


def _dil_fwd(proj, slopes, grp, dil, name):
    t = proj.shape[0]
    rows = DIL_BLOCK * DIL_UNITS
    edge_rows = DIL_BLOCK * dil
    nbk = DIL_UNITS // dil
    nsb = t // rows
    qc, kc_, vc_ = grp * DIL_HEADS, 3 * DIL_HEADS + grp * DIL_HEADS, 6 * DIL_HEADS + grp * DIL_HEADS

    def body(q_ref, kp_ref, kc_ref, vp_ref, vc_ref, sl_ref, o_ref, lse_ref, qf, kpf, kcf, vpf, vcf, of, lf):
        m = pl.program_id(1)
        for src, dst in ((q_ref, qf), (kp_ref, kpf), (kc_ref, kcf), (vp_ref, vpf), (vc_ref, vcf)):
            dst[...] = src[...].astype(F32)
        slope = sl_ref[0]
        unit = lax.broadcasted_iota(jnp.int32, (DIL_UNITS, 1, 1), 0)
        has_prev = (unit >= dil) | (m > 0)
        q = _gather_units(qf, dil).astype(BF16)
        kc, vc = _gather_units(kcf, dil).astype(BF16), _gather_units(vcf, dil).astype(BF16)
        kp, vp = _gather_units(kcf, dil, -1, kpf).astype(BF16), _gather_units(vcf, dil, -1, vpf).astype(BF16)
        bias_p, ok_p = _dil_bias(slope, True)
        bias_c, ok_c = _dil_bias(slope, False)
        sp = jnp.where(ok_p & has_prev, _dot(q, kp, BNT) * DIL_SCALE - bias_p, NEG_INF)
        sc = jnp.where(ok_c, _dot(q, kc, BNT) * DIL_SCALE - bias_c, NEG_INF)
        mx = jnp.maximum(jnp.max(sp, axis=-1, keepdims=True), jnp.max(sc, axis=-1, keepdims=True))
        pp = jnp.exp(sp - mx)
        pc = jnp.exp(sc - mx)
        l = jnp.sum(pp, axis=-1, keepdims=True) + jnp.sum(pc, axis=-1, keepdims=True)
        o = (_dot(pp.astype(BF16), vp, BNN) + _dot(pc.astype(BF16), vc, BNN)) / l
        _scatter_units(of, o, dil)
        _scatter_units(lf, mx + jnp.log(l), dil)
        o_ref[...] = of[...].astype(BF16)
        lse_ref[0] = lf[...]

    cur = lambda col: pl.BlockSpec((rows, LANES), lambda h, m: (m, col + h))
    prev = lambda col: pl.BlockSpec((edge_rows, LANES), lambda h, m: (jnp.maximum(m * nbk - 1, 0), col + h))
    return _pcall(
        body, name=name, grid=(DIL_HEADS, nsb),
        in_specs=[cur(qc), prev(kc_), cur(kc_), prev(vc_), cur(vc_), pl.BlockSpec((1, 1, 1), lambda h, m: (h, 0, 0))],
        out_specs=[pl.BlockSpec((rows, LANES), lambda h, m: (m, h)), pl.BlockSpec((1, rows, 1), lambda h, m: (h, m, 0))],
        out_shape=[SDS((t, D_MODEL), BF16), SDS((DIL_HEADS, t, 1), F32)],
        scratch_shapes=[pltpu.VMEM((rows, LANES), F32), pltpu.VMEM((edge_rows, LANES), F32), pltpu.VMEM((rows, LANES), F32),
                        pltpu.VMEM((edge_rows, LANES), F32), pltpu.VMEM((rows, LANES), F32), pltpu.VMEM((rows, LANES), F32),
                        pltpu.VMEM((rows, 1), F32)],
        compiler_params=_params(),
    )(proj, proj, proj, proj, proj, slopes)


def _dil_mix(outs, lses, proj, z_col0):
    t = proj.shape[0]
    tm = 512
    zcb = z_col0 // LANES
    ng = len(outs)

    def body(*refs):
        o_refs, l_refs, z_ref = refs[:ng], refs[ng:2 * ng], refs[2 * ng]
        om_ref, g_ref, lse_ref = refs[2 * ng + 1:]
        ls = [r[0] for r in l_refs]
        mx = functools.reduce(jnp.maximum, ls)
        es = [jnp.exp(l - mx) for l in ls]
        tot = functools.reduce(jnp.add, es)
        o = functools.reduce(jnp.add, [(e / tot) * r[...].astype(F32) for e, r in zip(es, o_refs)])
        z = z_ref[...].astype(F32)
        om_ref[...] = o.astype(BF16)
        g_ref[...] = (o * (z * _sigmoid(z))).astype(BF16)
        lse_ref[0] = mx + jnp.log(tot)

    tile = pl.BlockSpec((tm, LANES), lambda i, h: (i, h))
    col = pl.BlockSpec((1, tm, 1), lambda i, h: (h, i, 0))
    return _pcall(
        body, name="dil_mix", grid=(t // tm, DIL_HEADS),
        in_specs=[tile] * ng + [col] * ng + [pl.BlockSpec((tm, LANES), lambda i, h: (i, zcb + h))],
        out_specs=[tile, tile, col],
        out_shape=[SDS((t, D_MODEL), BF16), SDS((t, D_MODEL), BF16), SDS((DIL_HEADS, t, 1), F32)],
        compiler_params=_params(),
    )(*outs, *lses, proj)


def _dil_bwd(proj, do, lse, delta, slopes, grp, dil, name):
    t = proj.shape[0]
    rows = DIL_BLOCK * DIL_UNITS
    edge_rows = DIL_BLOCK * dil
    nbk = DIL_UNITS // dil
    nsb = t // rows
    last_edge = t // edge_rows - 1
    qc, kc_, vc_ = grp * DIL_HEADS, 3 * DIL_HEADS + grp * DIL_HEADS, 6 * DIL_HEADS + grp * DIL_HEADS

    def body(q_ref, qn_ref, kp_ref, kc_ref, vp_ref, vc_ref, do_ref, don_ref, l_ref, ln_ref, d_ref, dn_ref, sl_ref,
             dq_ref, dk_ref, dv_ref, qf, qnf, kpf, kcf, vpf, vcf, dof, donf, dqf, dkf, dvf):
        m = pl.program_id(1)
        for src, dst in ((q_ref, qf), (qn_ref, qnf), (kp_ref, kpf), (kc_ref, kcf), (vp_ref, vpf), (vc_ref, vcf),
                         (do_ref, dof), (don_ref, donf)):
            dst[...] = src[...].astype(F32)
        slope = sl_ref[0]
        unit = lax.broadcasted_iota(jnp.int32, (DIL_UNITS, 1, 1), 0)
        has_prev = (unit >= dil) | (m > 0)
        has_next = (unit < DIL_UNITS - dil) | (m < nsb - 1)
        b16 = lambda x: x.astype(BF16)
        q, kc, vc, dout = (b16(_gather_units(x, dil)) for x in (qf, kcf, vcf, dof))
        kp, vp = b16(_gather_units(kcf, dil, -1, kpf)), b16(_gather_units(vcf, dil, -1, vpf))
        qn, don = b16(_gather_units(qf, dil, 1, qnf)), b16(_gather_units(dof, dil, 1, donf))
        lrow, drow = _gather_units(l_ref, dil, lead=(0,)), _gather_units(d_ref, dil, lead=(0,))
        lnrow = _gather_units(l_ref, dil, 1, ln_ref, lead=(0,))
        dnrow = _gather_units(d_ref, dil, 1, dn_ref, lead=(0,))
        bias_p, ok_p = _dil_bias(slope, True)
        bias_c, ok_c = _dil_bias(slope, False)
        sp = jnp.where(ok_p & has_prev, _dot(q, kp, BNT) * DIL_SCALE - bias_p, NEG_INF)
        sc = jnp.where(ok_c, _dot(q, kc, BNT) * DIL_SCALE - bias_c, NEG_INF)
        pp = jnp.exp(sp - lrow)
        pc = jnp.exp(sc - lrow)
        dsp = b16(pp * (_dot(dout, vp, BNT) - drow))
        dsc = b16(pc * (_dot(dout, vc, BNT) - drow))
        _scatter_units(dqf, (_dot(dsp, kp, BNN) + _dot(dsc, kc, BNN)) * DIL_SCALE, dil)
        sn = jnp.where(ok_p & has_next, _dot(qn, kc, BNT) * DIL_SCALE - bias_p, NEG_INF)
        pn = jnp.exp(sn - lnrow)
        dsn = b16(pn * (_dot(don, vc, BNT) - dnrow))
        _scatter_units(dkf, (_dot(dsc, q, BTN) + _dot(dsn, qn, BTN)) * DIL_SCALE, dil)
        _scatter_units(dvf, _dot(b16(pc), dout, BTN) + _dot(b16(pn), don, BTN), dil)
        dq_ref[...] = dqf[...].astype(BF16)
        dk_ref[...] = dkf[...].astype(BF16)
        dv_ref[...] = dvf[...].astype(BF16)

    prev_i = lambda m: jnp.maximum(m * nbk - 1, 0)
    next_i = lambda m: jnp.minimum((m + 1) * nbk, last_edge)
    cur = lambda col: pl.BlockSpec((rows, LANES), lambda h, m: (m, col + h))
    edge = lambda col, f: pl.BlockSpec((edge_rows, LANES), lambda h, m: (f(m), col + h))
    colcur = pl.BlockSpec((1, rows, 1), lambda h, m: (h, m, 0))
    colnext = pl.BlockSpec((1, edge_rows, 1), lambda h, m: (h, next_i(m), 0))
    out_blk = pl.BlockSpec((rows, LANES), lambda h, m: (m, h))
    big, small = pltpu.VMEM((rows, LANES), F32), pltpu.VMEM((edge_rows, LANES), F32)
    return _pcall(
        body, name=name, grid=(DIL_HEADS, nsb),
        in_specs=[cur(qc), edge(qc, next_i), edge(kc_, prev_i), cur(kc_), edge(vc_, prev_i), cur(vc_),
                  cur(0), edge(0, next_i), colcur, colnext, colcur, colnext,
                  pl.BlockSpec((1, 1, 1), lambda h, m: (h, 0, 0))],
        out_specs=[out_blk, out_blk, out_blk],
        out_shape=[SDS((t, D_MODEL), BF16)] * 3,
        scratch_shapes=[big, small, small, big, small, big, big, small, big, big, big],
        compiler_params=_params(),
    )(proj, proj, proj, proj, proj, proj, do, do, lse, lse, delta, delta, slopes)


def _mesh_pos():
    x, y, c = lax.axis_index("x"), lax.axis_index("y"), lax.axis_index("c")
    return x, y, c


def _peer(pos, k):
    x, y, c = pos
    px = 1 - x if k & 4 else x
    py = 1 - y if k & 2 else y
    pc = 1 - c if k & 1 else c
    return (px, py, pc), 4 * px + 2 * py + pc


N_CHIPS = 4
CHIP_FLIPS = ((1, 0), (0, 1), (1, 1))


def _other_chips(x, y):
    return [(1 - x if fx else x, 1 - y if fy else y) for fx, fy in CHIP_FLIPS]


def _all_gather(arrays):
    n = len(arrays)
    per = 2 * N_CHIPS - 1
    hbm = pl.BlockSpec(memory_space=pltpu.HBM)

    def body(*refs):
        ins, outs = refs[:n], refs[n:2 * n]
        send_sems, recv_sems, local_sems = refs[2 * n:]
        x, y, c = _mesh_pos()
        sibling = (x, y, 1 - c)
        chips = _other_chips(x, y)
        block = lambda px, py, pc: 4 * px + 2 * py + pc

        def copy(w, k, src, blk, to):
            return pltpu.make_async_remote_copy(
                src_ref=src, dst_ref=outs[w].at[blk], send_sem=send_sems.at[w * per + k],
                recv_sem=recv_sems.at[w * per + k], device_id=to, device_id_type=MESH)

        local, started = [], []
        for w in range(n):
            cp = pltpu.make_async_copy(ins[w], outs[w].at[block(x, y, c)], local_sems.at[w])
            cp.start()
            local.append(cp)
            started.append(copy(w, 0, ins[w], block(x, y, c), sibling))
            for j, (px, py) in enumerate(chips):
                started.append(copy(w, 1 + j, ins[w], block(x, y, c), (px, py, c)))
        for cp in started:
            cp.start()
        for j, (px, py) in enumerate(chips):
            for w in range(n):
                copy(w, 1 + j, ins[w], block(px, py, c), sibling).wait_recv()
                cp = copy(w, 4 + j, outs[w].at[block(px, py, c)], block(px, py, c), sibling)
                cp.start()
                started.append(cp)
        for w in range(n):
            copy(w, 0, ins[w], block(x, y, 1 - c), sibling).wait_recv()
            for j, (px, py) in enumerate(chips):
                copy(w, 4 + j, ins[w], block(px, py, 1 - c), sibling).wait_recv()
        for cp in started:
            cp.wait_send()
        for cp in local:
            cp.wait()

    return _pcall(
        body, name="all_gather_weights", in_specs=[hbm] * n, out_specs=[hbm] * n,
        out_shape=[SDS((N_DEV,) + a.shape, a.dtype) for a in arrays],
        scratch_shapes=[pltpu.SemaphoreType.DMA((n * per,)), pltpu.SemaphoreType.DMA((n * per,)),
                        pltpu.SemaphoreType.DMA((n,))],
    )(*arrays)


def _core_exchange(slabs):
    n = len(slabs)
    hbm = pl.BlockSpec(memory_space=pltpu.HBM)

    def body(*refs):
        ins, outs = refs[:n], refs[n:2 * n]
        send_sems, recv_sems = refs[2 * n:]
        x, y, c = _mesh_pos()
        copies = [pltpu.make_async_remote_copy(
            src_ref=ins[w].at[pl.ds(0, N_CHIPS), 1 - c], dst_ref=outs[w], send_sem=send_sems.at[w],
            recv_sem=recv_sems.at[w], device_id=(x, y, 1 - c), device_id_type=MESH) for w in range(n)]
        for cp in copies:
            cp.start()
        for cp in copies:
            cp.wait_recv()
        for cp in copies:
            cp.wait_send()

    return _pcall(
        body, name="grads_core_exchange", in_specs=[hbm] * n, out_specs=[hbm] * n,
        out_shape=[SDS((N_CHIPS,) + a.shape[2:], a.dtype) for a in slabs],
        scratch_shapes=[pltpu.SemaphoreType.DMA((n,)), pltpu.SemaphoreType.DMA((n,))],
    )(*slabs)


def _chip_exchange(partials):
    n = len(partials)
    per = N_CHIPS - 1
    hbm = pl.BlockSpec(memory_space=pltpu.HBM)

    def body(*refs):
        ins, outs = refs[:n], refs[n:2 * n]
        send_sems, recv_sems, local_sems = refs[2 * n:]
        x, y, c = _mesh_pos()
        mine = 2 * x + y
        local, sends, recvs = [], [], []
        for w in range(n):
            cp = pltpu.make_async_copy(ins[w].at[mine], outs[w].at[mine], local_sems.at[w])
            cp.start()
            local.append(cp)
            for j, (px, py) in enumerate(_other_chips(x, y)):
                theirs = 2 * px + py
                sems = dict(send_sem=send_sems.at[w * per + j], recv_sem=recv_sems.at[w * per + j],
                            device_id=(px, py, c), device_id_type=MESH)
                cp = pltpu.make_async_remote_copy(src_ref=ins[w].at[theirs], dst_ref=outs[w].at[mine], **sems)
                cp.start()
                sends.append(cp)
                recvs.append(pltpu.make_async_remote_copy(src_ref=ins[w].at[theirs], dst_ref=outs[w].at[theirs], **sems))
        for cp in recvs:
            cp.wait_recv()
        for cp in sends:
            cp.wait_send()
        for cp in local:
            cp.wait()

    return _pcall(
        body, name="grads_chip_exchange", in_specs=[hbm] * n, out_specs=[hbm] * n,
        out_shape=[SDS(a.shape, a.dtype) for a in partials],
        scratch_shapes=[pltpu.SemaphoreType.DMA((n * per,)), pltpu.SemaphoreType.DMA((n * per,)),
                        pltpu.SemaphoreType.DMA((n,))],
    )(*partials)


def _core_sum(slabs, from_sibling, core, name):
    _, _, r, c = slabs.shape
    tr = min(r, 256)

    def body(core_ref, a_ref, b_ref, o_ref):
        o_ref[...] = (a_ref[0].astype(F32) + b_ref[...].astype(F32)).astype(BF16)

    return _pcall(
        body, name=name,
        grid_spec=pltpu.PrefetchScalarGridSpec(
            num_scalar_prefetch=1, grid=(N_CHIPS, r // tr),
            in_specs=[pl.BlockSpec((1, 1, tr, c), lambda g, i, core: (g, core[0], i, 0)),
                      pl.BlockSpec((1, tr, c), lambda g, i, core: (g, i, 0))],
            out_specs=pl.BlockSpec((1, tr, c), lambda g, i, core: (g, i, 0))),
        out_shape=SDS((N_CHIPS, r, c), BF16), compiler_params=_params(),
    )(core, slabs, from_sibling)


def _adam_math(w, g, m, v):
    m = ADAM_B1 * m + (1.0 - ADAM_B1) * g
    v = ADAM_B2 * v + (1.0 - ADAM_B2) * (g * g)
    m_hat = m / (1.0 - ADAM_B1 ** ADAM_STEP)
    v_hat = v / (1.0 - ADAM_B2 ** ADAM_STEP)
    delta = -ADAM_LR * (m_hat / (jnp.sqrt(v_hat) + ADAM_EPS) + ADAM_WD * w)
    return delta, m, v


def _adamw(recv, w, m, v, name):
    r, c = w.shape
    tr = min(r, 128)
    n_parts = recv.shape[0]

    def body(g_ref, w_ref, m_ref, v_ref, go_ref, d_ref, mo_ref, vo_ref):
        g = g_ref[0].astype(F32)
        for s in range(1, n_parts):
            g = g + g_ref[s].astype(F32)
        delta, mn, vn = _adam_math(w_ref[...], g, m_ref[...], v_ref[...])
        go_ref[...] = g
        d_ref[...] = delta
        mo_ref[...] = mn
        vo_ref[...] = vn

    blk = pl.BlockSpec((tr, c), lambda i: (i, 0))
    return _pcall(
        body, name=name, grid=(r // tr,),
        in_specs=[pl.BlockSpec((n_parts, tr, c), lambda i: (0, i, 0)), blk, blk, blk],
        out_specs=[blk] * 4, out_shape=[SDS((r, c), F32)] * 4, compiler_params=_params(),
    )(recv, w, m, v)


VEC_ROWS = 32


def _small_allreduce_adamw(vec, w, m, v):
    def body(vec_ref, w_ref, m_ref, v_ref, g_ref, d_ref, mo_ref, vo_ref, gath, send_sems, recv_sems):
        pos = _mesh_pos()
        me = 4 * pos[0] + 2 * pos[1] + pos[2]
        sends, recvs = [], []
        for k in range(1, N_DEV):
            peer, peer_idx = _peer(pos, k)
            cp = pltpu.make_async_remote_copy(src_ref=vec_ref, dst_ref=gath.at[me], send_sem=send_sems.at[k - 1],
                                              recv_sem=recv_sems.at[k - 1], device_id=peer, device_id_type=MESH)
            cp.start()
            sends.append(cp)
            recvs.append(pltpu.make_async_remote_copy(src_ref=vec_ref, dst_ref=gath.at[peer_idx],
                                                      send_sem=send_sems.at[k - 1], recv_sem=recv_sems.at[k - 1],
                                                      device_id=peer, device_id_type=MESH))
        gath[me] = vec_ref[...]
        for cp in recvs:
            cp.wait_recv()
        for cp in sends:
            cp.wait_send()
        tot = gath[0]
        for s in range(1, N_DEV):
            tot = tot + gath[s]
        rowi = lax.broadcasted_iota(jnp.int32, (8, LANES), 0)
        mine = jnp.sum(jnp.where(rowi == me, tot[16:24, :], 0.0), axis=0, keepdims=True)
        g = jnp.concatenate([tot[0:16, :], jnp.broadcast_to(mine, (8, LANES)), tot[24:32, :]], axis=0)
        delta, mn, vn = _adam_math(w_ref[...], g, m_ref[...], v_ref[...])
        g_ref[...] = g
        d_ref[...] = delta
        mo_ref[...] = mn
        vo_ref[...] = vn

    vm = pl.BlockSpec(memory_space=pltpu.VMEM)
    return _pcall(
        body, name="small_allreduce_adamw", in_specs=[vm] * 4, out_specs=[vm] * 4,
        out_shape=[SDS((VEC_ROWS, LANES), F32)] * 4,
        scratch_shapes=[pltpu.VMEM((N_DEV, VEC_ROWS, LANES), F32), pltpu.SemaphoreType.DMA((N_DEV - 1,)),
                        pltpu.SemaphoreType.DMA((N_DEV - 1,))],
        compiler_params=pltpu.CompilerParams(has_side_effects=True),
    )(vec, w, m, v)


def _cols_to_slabs(a):
    r, c8 = a.shape
    return a.reshape(r, N_DEV, c8 // N_DEV).transpose(1, 0, 2)


def _slabs_to_cols(a):
    n, r, c = a.shape
    return a.transpose(1, 0, 2).reshape(r, n * c)


def _rows8(vec):
    return vec.reshape(-1, LANES)


def _pad_rows(a, rows):
    return jnp.pad(a, ((0, rows - a.shape[0]), (0, LANES - a.shape[1])))


def kernel(x, p, fox_norm, fox_w_in, fox_b_f, fox_w_out, dil_norm, dil_w_in, dil_w_out, ple_w_up, ple_w_gate, final_norm, loss_target, m_fox_norm, m_fox_w_in, m_fox_b_f, m_fox_w_out, m_dil_norm, m_dil_w_in, m_dil_w_out, m_ple_w_up, m_ple_w_gate, m_final_norm, v_fox_norm, v_fox_w_in, v_fox_b_f, v_fox_w_out, v_dil_norm, v_dil_w_in, v_dil_w_out, v_ple_w_up, v_ple_w_gate, v_final_norm):
    t = x.shape[1]
    d = D_MODEL
    xs, tgt = x[0], loss_target[0]
    p0, p1 = p[0, 0], p[1, 0]
    fox_cols = fox_w_in.shape[2]
    ple_dim = ple_w_up.shape[1]

    shards = [fox_w_in[0].astype(BF16), fox_w_out[0].astype(BF16), dil_w_in[0].astype(BF16),
              dil_w_out[0].astype(BF16), ple_w_up.reshape(-1, LANES).astype(BF16),
              ple_w_gate.reshape(-1, d).astype(BF16), dil_norm]
    gw = _all_gather(shards)
    w_fox_in = _slabs_to_cols(gw[0])
    w_fox_main = w_fox_in[:, :4 * d]
    w_fox_f = jnp.pad(w_fox_in[:, 4 * d:], ((0, 0), (0, LANES - FOX_HEADS)))
    w_fox_out = gw[1].reshape(d, d)
    w_dil_in = _slabs_to_cols(gw[2])
    w_dil_out = gw[3].reshape(d, d)
    w_up = gw[4].reshape(N_DEV, 2, ple_dim, LANES).transpose(1, 2, 0, 3).reshape(2, ple_dim, d)
    w_gate = gw[5].reshape(N_DEV, 2, d // N_DEV, d).transpose(1, 0, 2, 3).reshape(2, d, d)
    dil_norm_full = gw[6].reshape(1, d)
    b_pad = jnp.pad(fox_b_f, ((0, 0), (0, LANES - FOX_HEADS)))

    n0, r0 = _rms_fwd(xs, fox_norm, "rms_fox")
    proj0 = _mm([n0], w_fox_main, "nn", "fox_in_proj", tiles=IN_PROJ_TILES)
    projf = _mm([n0], w_fox_f, "nn", "fox_gate_proj", tiles=IN_PROJ_TILES, out_dtype=F32)
    c_all = _fox_gate_fwd(projf, b_pad)
    qaug_fwd = _fox_aug(c_all, c_all, 1.0, 0.0, 0, FOX_AUG, "fox_aug_q_fwd")
    kaug = _fox_aug(c_all, c_all, -1.0, 0.0, FOX_AUG, 0, "fox_aug_k")
    o0, g0, lse0 = _fox_fwd(proj0, qaug_fwd, kaug)
    h1 = _mm_residual(g0, w_fox_out, "nn", xs, "fox_out_proj")
    h2, u0, a0 = _ple_fwd(h1, p0, w_up[0], w_gate[0], "ple0_fwd")

    n1, r1 = _rms_fwd(h2, dil_norm_full, "rms_dil")
    proj1 = _mm([n1], w_dil_in, "nn", "dil_in_proj", tiles=IN_PROJ_TILES)
    n_heads = len(DIL_PATTERN) * DIL_HEADS
    slopes = 2.0 ** (-ALIBI_MAX_EXP * jnp.arange(1, n_heads + 1, dtype=F32) / n_heads)
    dil_o, dil_lse, dil_slopes = [], [], []
    for grp, (_, dil) in enumerate(DIL_PATTERN):
        sl = (slopes[grp * DIL_HEADS:(grp + 1) * DIL_HEADS] * dil).reshape(DIL_HEADS, 1, 1)
        og, lg = _dil_fwd(proj1, sl, grp, dil, f"dil_attn_fwd_{grp}")
        dil_o.append(og)
        dil_lse.append(lg)
        dil_slopes.append(sl)
    z1_col0 = 9 * d
    o1, g1, lse1 = _dil_mix(dil_o, dil_lse, proj1, z1_col0)
    h3 = _mm_residual(g1, w_dil_out, "nn", h2, "dil_out_proj")
    h4, u1, a1 = _ple_fwd(h3, p1, w_up[1], w_gate[1], "ple1_fwd")

    dh4, d_final_norm, loss_part = _final_bwd(h4, final_norm.reshape(1, d), tgt)

    du1, da1 = _ple_bwd_elem(dh4, u1, a1, "ple1_bwd_elem")
    dw_up1 = _dw(p1, du1, "ple1_dw_up")
    dw_gate1 = _dw(h3, da1, "ple1_dw_gate")
    dh3 = _mm_residual(da1, w_gate[1], "nt", dh4, "ple1_dh")

    dw_dil_out = _dw(g1, dh3, "dil_dw_out")
    do1, dz1, delta1 = _mm_gate_bwd(dh3, w_dil_out, proj1, z1_col0, o1, DIL_HEADS, "dil_dgate")
    dqs, dks, dvs = [], [], []
    for grp, (_, dil) in enumerate(DIL_PATTERN):
        dq, dk, dv = _dil_bwd(proj1, do1, lse1, delta1, dil_slopes[grp], grp, dil, f"dil_attn_bwd_{grp}")
        dqs.append(dq)
        dks.append(dk)
        dvs.append(dv)
    dproj1 = dqs + dks + dvs + [dz1]
    dw_dil_in = jnp.concatenate([_dw(n1, dpart, f"dil_dw_in_{s}") for s, dpart in enumerate(dproj1)], axis=1)
    dh2, d_dil_norm = _mm_in_bwd(dproj1, w_dil_in, h2, dil_norm_full, r1, dh3, "dil_dx")

    du0, da0 = _ple_bwd_elem(dh2, u0, a0, "ple0_bwd_elem")
    dw_up0 = _dw(p0, du0, "ple0_dw_up")
    dw_gate0 = _dw(h1, da0, "ple0_dw_gate")
    dh1 = _mm_residual(da0, w_gate[0], "nt", dh2, "ple0_dh")

    dw_fox_out = _dw(g0, dh1, "fox_dw_out")
    do0, dz0, delta0 = _mm_gate_bwd(dh1, w_fox_out, proj0, 3 * d, o0, FOX_HEADS, "fox_dgate")
    head_cols = lambda a: jnp.pad(a, ((0, 0), (0, LANES - FOX_HEADS)))
    lse_cols = head_cols(lse0.reshape(FOX_HEADS, t).T)
    delta_cols = head_cols(delta0.reshape(FOX_HEADS, t).T)
    qaug_bwd = _fox_aug(c_all, lse_cols, 1.0, -1.0, 0, FOX_AUG, "fox_aug_q_bwd")
    doaug = _fox_aug(delta_cols, delta_cols, -1.0, 0.0, 0, None, "fox_aug_do")
    dq0, dk0, dv0, dck_wide, dcq = _fox_bwd(proj0, do0, qaug_bwd, kaug, doaug)
    dc_query, dc_key = _fox_unpack_dc(dck_wide, dcq)
    df, d_b_f = _fox_gate_bwd(projf, b_pad, head_cols(dc_query), head_cols(dc_key))
    dproj0 = [dq0, dk0, dv0, dz0]
    dw_fox_parts = [_dw(n0, dpart, f"fox_dw_in_{s}") for s, dpart in enumerate(dproj0)]
    dw_fox_f = _dw(n0, df, "fox_dw_gate")
    dn0_f = _mm([df], w_fox_f, "nt", "fox_dx_gate", out_dtype=F32)
    grad_x, d_fox_norm = _mm_in_bwd(dproj0, w_fox_main, xs, fox_norm, r0, dh1, "fox_dx", more=dn0_f)

    dw_fox_in = jnp.concatenate(dw_fox_parts + [dw_fox_f[:, :FOX_HEADS]], axis=1)
    slabs = [_cols_to_slabs(dw_fox_in), dw_fox_out.reshape(N_DEV, d // N_DEV, d), _cols_to_slabs(dw_dil_in),
             dw_dil_out.reshape(N_DEV, d // N_DEV, d),
             jnp.stack([dw_up0, dw_up1]).reshape(2, ple_dim, N_DEV, LANES).transpose(2, 0, 1, 3).reshape(N_DEV, -1, LANES),
             jnp.stack([dw_gate0, dw_gate1]).reshape(2, N_DEV, d // N_DEV, d).transpose(1, 0, 2, 3).reshape(N_DEV, -1, d)]
    names = ["fox_w_in", "fox_w_out", "dil_w_in", "dil_w_out", "ple_w_up", "ple_w_gate"]
    slabs = [s.reshape((N_CHIPS, 2) + s.shape[1:]) for s in slabs]
    from_sibling = _core_exchange(slabs)
    core = lax.axis_index("c").astype(jnp.int32).reshape(1)
    chip_sums = [_core_sum(s, f, core, "core_sum_" + nm) for s, f, nm in zip(slabs, from_sibling, names)]
    recv = _chip_exchange(chip_sums)
    big = [(fox_w_in, m_fox_w_in, v_fox_w_in), (fox_w_out, m_fox_w_out, v_fox_w_out),
           (dil_w_in, m_dil_w_in, v_dil_w_in), (dil_w_out, m_dil_w_out, v_dil_w_out),
           (ple_w_up, m_ple_w_up, v_ple_w_up), (ple_w_gate, m_ple_w_gate, v_ple_w_gate)]
    upd = {}
    for rv, (w, m, v), nm in zip(recv, big, names):
        shp2 = rv.shape[1:]
        res = _adamw(rv, w.reshape(shp2), m.reshape(shp2), v.reshape(shp2), "adamw_" + nm)
        upd[nm] = [a.reshape(w.shape) for a in res]

    loss_row = jnp.where(jnp.arange(LANES) == 0, loss_part, 0.0)
    vec = jnp.concatenate([_rows8(d_fox_norm), _rows8(d_final_norm), _rows8(d_dil_norm), d_b_f, loss_row,
                           jnp.zeros((VEC_ROWS - 26, LANES), F32)], axis=0)

    def small_pack(a_fox_norm, a_final_norm, a_dil_norm, a_b_f):
        return jnp.concatenate([_rows8(a_fox_norm), _rows8(a_final_norm), _pad_rows(a_dil_norm, 8),
                                _pad_rows(a_b_f, 8)], axis=0)

    sg, sd, sm, sv = _small_allreduce_adamw(
        vec, small_pack(fox_norm, final_norm, dil_norm, fox_b_f),
        small_pack(m_fox_norm, m_final_norm, m_dil_norm, m_fox_b_f),
        small_pack(v_fox_norm, v_final_norm, v_dil_norm, v_fox_b_f))

    def small_unpack(a):
        return {"fox_norm": a[0:8].reshape(1, d), "final_norm": a[8:16].reshape(d), "dil_norm": a[16:17],
                "fox_b_f": a[24:25, :FOX_HEADS]}

    loss = sg[25, 0]
    order = ["fox_norm", "fox_w_in", "fox_b_f", "fox_w_out", "dil_norm", "dil_w_in", "dil_w_out", "ple_w_up",
             "ple_w_gate", "final_norm"]
    out = [loss, grad_x[None]]
    for idx, small in enumerate((sg, sd, sm, sv)):
        sp = small_unpack(small)
        out += [sp[nm] if nm in sp else upd[nm][idx] for nm in order]
    return tuple(out)
```

```python
import functools

import jax
import jax.numpy as jnp
from jax import lax
from jax.experimental import pallas as pl
from jax.experimental.pallas import tpu as pltpu

F32 = jnp.float32
BF16 = jnp.bfloat16
SDS = jax.ShapeDtypeStruct

D_MODEL = 1024
N_DEV = 8
LANES = 128
FOX_HEADS = 16
FOX_HEAD_DIM = 64
FOX_PAIRS = FOX_HEADS // 2
DIL_HEADS = 8
DIL_BLOCK = 128
DIL_PATTERN = ((128, 1), (512, 4), (2048, 16))
ALIBI_MAX_EXP = 8.0
RMS_EPS = 1e-6
ADAM_LR, ADAM_B1, ADAM_B2, ADAM_EPS, ADAM_WD, ADAM_STEP = 0.001, 0.9, 0.999, 1e-08, 0.01, 10
VMEM_LIMIT = 48 * 1024 * 1024
NEG_INF = float("-inf")

NN = (((1,), (0,)), ((), ()))
NT = (((1,), (1,)), ((), ()))
TN = (((0,), (0,)), ((), ()))
MESH = pl.DeviceIdType.MESH


def _pcall(body, **kw):
    return pl.pallas_call(body, **kw)


def _params(**kw):
    return pltpu.CompilerParams(vmem_limit_bytes=VMEM_LIMIT, **kw)


def _dot(a, b, dims):
    return lax.dot_general(a, b, dims, preferred_element_type=F32)


def _sigmoid(x):
    return 1.0 / (1.0 + jnp.exp(-x))


def _mm(a_parts, b, mode, name, tiles=(512, 1024, 1024), extras=(), outs=None, epilogue=None, out_dtype=BF16):
    na = len(a_parts)
    if mode == "tn":
        k_part, m = a_parts[0].shape
        n = b.shape[1]
    else:
        m, k_part = a_parts[0].shape
        n = b.shape[1] if mode == "nn" else b.shape[0]
    tm, tn, tk = min(tiles[0], m), min(tiles[1], n), min(tiles[2], k_part)
    kb = k_part // tk
    nk = na * kb
    grid = (m // tm, n // tn, nk)

    in_specs = []
    for s in range(na):
        if mode == "tn":
            in_specs.append(pl.BlockSpec((tk, tm), lambda i, j, k: (k, i)))
        else:
            in_specs.append(pl.BlockSpec((tm, tk), lambda i, j, k, s=s: (i, jnp.clip(k - s * kb, 0, kb - 1))))
    if mode == "nt":
        in_specs.append(pl.BlockSpec((tn, tk), lambda i, j, k: (j, k)))
    else:
        in_specs.append(pl.BlockSpec((tk, tn), lambda i, j, k: (k, j)))
    for _, blk, imap in extras:
        in_specs.append(pl.BlockSpec(blk, imap))
    if outs is None:
        outs = [(SDS((m, n), out_dtype), (tm, tn), lambda i, j, k: (i, j))]
    out_specs = [pl.BlockSpec(blk, imap) for _, blk, imap in outs]
    ne, no = len(extras), len(outs)
    dims = {"nn": NN, "nt": NT, "tn": TN}[mode]

    def finish(res, e_refs, o_refs, i):
        if epilogue is None:
            o_refs[0][...] = res.astype(o_refs[0].dtype)
        else:
            epilogue(res, e_refs, o_refs, i)

    def body(*refs):
        a_refs = refs[:na]
        b_ref = refs[na]
        e_refs = refs[na + 1:na + 1 + ne]
        o_refs = refs[na + 1 + ne:na + 1 + ne + no]
        i, k = pl.program_id(0), pl.program_id(2)
        if nk == 1:
            finish(_dot(a_refs[0][...].astype(BF16), b_ref[...].astype(BF16), dims), e_refs, o_refs, i)
            return
        acc = refs[-1]

        @pl.when(k == 0)
        def _():
            acc[...] = jnp.zeros_like(acc)

        def step(a_ref):
            acc[...] += _dot(a_ref[...].astype(BF16), b_ref[...].astype(BF16), dims)

        for s in range(na):
            if na == 1:
                step(a_refs[0])
            else:
                pl.when((k >= s * kb) & (k < (s + 1) * kb))(functools.partial(step, a_refs[s]))

        @pl.when(k == nk - 1)
        def _():
            finish(acc[...], e_refs, o_refs, i)

    res = _pcall(
        body, name=name, grid=grid, in_specs=in_specs, out_specs=out_specs,
        out_shape=[o[0] for o in outs], scratch_shapes=[] if nk == 1 else [pltpu.VMEM((tm, tn), F32)],
        compiler_params=_params(dimension_semantics=("arbitrary", "arbitrary", "arbitrary")),
    )(*a_parts, b, *[e[0] for e in extras])
    return res[0] if len(res) == 1 else res


IN_PROJ_TILES = (1024, 1024, 1024)
DW_TILES = (1024, 1024, 512)


def _dw(x, dy, name):
    return _mm([x], dy, "tn", name, tiles=DW_TILES)


def _add_extra_epilogue(acc, e_refs, o_refs, i):
    o_refs[0][...] = acc + e_refs[0][...]


def _mm_residual(a, b, mode, res, name):
    m = a.shape[0]
    n = b.shape[1] if mode == "nn" else b.shape[0]
    tm, tn = 512, 1024
    return _mm([a], b, mode, name, tiles=(tm, tn, 1024),
               extras=[(res, (tm, tn), lambda i, j, k: (i, j))],
               outs=[(SDS((m, n), F32), (tm, tn), lambda i, j, k: (i, j))],
               epilogue=_add_extra_epilogue)


def _rms_fwd(h, g, name):
    t, d = h.shape
    tm = 512

    def body(h_ref, g_ref, n_ref, r_ref):
        x = h_ref[...]
        r = lax.rsqrt(jnp.mean(x * x, axis=-1, keepdims=True) + RMS_EPS)
        n_ref[...] = ((x * r) * g_ref[...]).astype(BF16)
        r_ref[...] = r

    return _pcall(
        body, name=name, grid=(t // tm,),
        in_specs=[pl.BlockSpec((tm, d), lambda i: (i, 0)), pl.BlockSpec((1, d), lambda i: (0, 0))],
        out_specs=[pl.BlockSpec((tm, d), lambda i: (i, 0)), pl.BlockSpec((tm, 1), lambda i: (i, 0))],
        out_shape=[SDS((t, d), BF16), SDS((t, 1), F32)],
        compiler_params=_params(),
    )(h, g)


def _rms_bwd_rows(dn, x, g, r):
    xhat = x * r
    dxhat = dn * g
    dx = r * (dxhat - xhat * jnp.mean(dxhat * xhat, axis=-1, keepdims=True))
    dg = jnp.sum(dn * xhat, axis=0, keepdims=True)
    return dx, dg


def _mm_in_bwd(d_parts, w, h, g, r, dres, name, more=None):
    t = h.shape[0]
    tm = 512
    tk = 1024 if len(d_parts) <= 4 else 512
    row = lambda i, j, k: (i, 0)
    extras = [(h, (tm, D_MODEL), row), (g, (1, D_MODEL), lambda i, j, k: (0, 0)), (r, (tm, 1), row),
              (dres, (tm, D_MODEL), row)]
    if more is not None:
        extras.append((more, (tm, D_MODEL), row))

    def epilogue(acc, e_refs, o_refs, i):
        dn = acc if more is None else acc + e_refs[4][...]
        dx, dg = _rms_bwd_rows(dn, e_refs[0][...], e_refs[1][...], e_refs[2][...])
        o_refs[0][...] = e_refs[3][...] + dx

        @pl.when(i == 0)
        def _():
            o_refs[1][...] = dg

        @pl.when(i > 0)
        def _():
            o_refs[1][...] += dg

    return _mm(d_parts, w, "nt", name, tiles=(tm, D_MODEL, tk), extras=extras,
               outs=[(SDS((t, D_MODEL), F32), (tm, D_MODEL), row),
                     (SDS((1, D_MODEL), F32), (1, D_MODEL), lambda i, j, k: (0, 0))],
               epilogue=epilogue)


def _final_bwd(h, g, tgt):
    t, d = h.shape
    tm = 256

    def body(h_ref, g_ref, t_ref, dh_ref, dg_ref, loss_ref):
        i = pl.program_id(0)
        x = h_ref[...]
        gg = g_ref[...]
        r = lax.rsqrt(jnp.mean(x * x, axis=-1, keepdims=True) + RMS_EPS)
        err = (x * r) * gg - t_ref[...]
        part = 0.5 * jnp.sum(jnp.mean(err * err, axis=-1, keepdims=True), axis=0, keepdims=True)
        dx, dg = _rms_bwd_rows(err * (1.0 / d), x, gg, r)
        dh_ref[...] = dx

        @pl.when(i == 0)
        def _():
            dg_ref[...] = dg
            loss_ref[...] = jnp.broadcast_to(part, loss_ref.shape)

        @pl.when(i > 0)
        def _():
            dg_ref[...] += dg
            loss_ref[...] += jnp.broadcast_to(part, loss_ref.shape)

    return _pcall(
        body, name="final_norm_loss", grid=(t // tm,),
        in_specs=[pl.BlockSpec((tm, d), lambda i: (i, 0)), pl.BlockSpec((1, d), lambda i: (0, 0)),
                  pl.BlockSpec((tm, d), lambda i: (i, 0))],
        out_specs=[pl.BlockSpec((tm, d), lambda i: (i, 0)), pl.BlockSpec((1, d), lambda i: (0, 0)),
                   pl.BlockSpec((1, LANES), lambda i: (0, 0))],
        out_shape=[SDS((t, d), F32), SDS((1, d), F32), SDS((1, LANES), F32)],
        compiler_params=_params(),
    )(h, g, tgt)


GATE_ROWS = 256


def _split3(x):
    hi = x.astype(BF16)
    r1 = x - hi.astype(F32)
    mid = r1.astype(BF16)
    lo = (r1 - mid.astype(F32)).astype(BF16)
    return hi, mid, lo


def _tri_sum(x, upper):
    rows = x.shape[0]
    ri = lax.broadcasted_iota(jnp.int32, (rows, rows), 0)
    ci = lax.broadcasted_iota(jnp.int32, (rows, rows), 1)
    tri = jnp.where((ri <= ci) if upper else (ri >= ci), 1.0, 0.0).astype(BF16)
    hi, mid, lo = _split3(x)
    return _dot(tri, hi, NN) + _dot(tri, mid, NN) + _dot(tri, lo, NN)


def _log_sigmoid(x):
    return jnp.minimum(x, 0.0) - jnp.log1p(jnp.exp(-jnp.abs(x)))


def _fox_gate_fwd(projf, bpad):
    t = projf.shape[0]
    tb = GATE_ROWS

    def body(x_ref, b_ref, c_ref, carry):
        i = pl.program_id(0)

        @pl.when(i == 0)
        def _():
            carry[...] = jnp.zeros_like(carry)

        c_ref[...] = _tri_sum(_log_sigmoid(x_ref[...] + b_ref[...]), upper=False) + carry[...]
        carry[...] = c_ref[pl.ds(tb - 1, 1), :]

    return _pcall(
        body, name="fox_gate_fwd", grid=(t // tb,),
        in_specs=[pl.BlockSpec((tb, LANES), lambda i: (i, 0)), pl.BlockSpec((1, LANES), lambda i: (0, 0))],
        out_specs=pl.BlockSpec((tb, LANES), lambda i: (i, 0)),
        out_shape=SDS((t, LANES), F32), scratch_shapes=[pltpu.VMEM((1, LANES), F32)],
        compiler_params=_params(),
    )(projf, bpad)


def _fox_gate_bwd(projf, bpad, dc_query, dc_key):
    t = projf.shape[0]
    tb = GATE_ROWS
    nb = t // tb

    def body(x_ref, b_ref, dcq_ref, dck_ref, df_ref, db_ref, carry, buf):
        i = pl.program_id(0)

        @pl.when(i == 0)
        def _():
            carry[...] = jnp.zeros_like(carry)

        buf[...] = _tri_sum(dcq_ref[...] - dck_ref[...], upper=True) + carry[...]
        carry[...] = buf[pl.ds(0, 1), :]
        df = buf[...] * _sigmoid(-(x_ref[...] + b_ref[...]))
        df_ref[...] = df.astype(BF16)
        part = jnp.sum(df, axis=0, keepdims=True)

        @pl.when(i == 0)
        def _():
            db_ref[...] = part

        @pl.when(i > 0)
        def _():
            db_ref[...] += part

    rev = lambda i: (nb - 1 - i, 0)
    return _pcall(
        body, name="fox_gate_bwd", grid=(nb,),
        in_specs=[pl.BlockSpec((tb, LANES), rev), pl.BlockSpec((1, LANES), lambda i: (0, 0)),
                  pl.BlockSpec((tb, LANES), rev), pl.BlockSpec((tb, LANES), rev)],
        out_specs=[pl.BlockSpec((tb, LANES), rev), pl.BlockSpec((1, LANES), lambda i: (0, 0))],
        out_shape=[SDS((t, LANES), BF16), SDS((1, LANES), F32)],
        scratch_shapes=[pltpu.VMEM((1, LANES), F32), pltpu.VMEM((tb, LANES), F32)],
        compiler_params=_params(),
    )(projf, bpad, dc_query, dc_key)


FOX_TQ = 1024
FOX_TQ_FWD = 1024
FOX_SCALE = FOX_HEAD_DIM ** -0.5


def _low_lanes(shape):
    return lax.broadcasted_iota(jnp.int32, shape, len(shape) - 1) < FOX_HEAD_DIM


FOX_AUG = 3
FOX_CHAIN = 256
FOX_SUM_ROWS = 8


def _top_rows(shape):
    return lax.broadcasted_iota(jnp.int32, shape, 0) < FOX_HEAD_DIM


def _fox_aug(a, b, sign_a, sign_b, piece_entry, ones_entry, name):
    t = a.shape[0]
    tb = 512

    def body(a_ref, b_ref, o_ref):
        x = sign_a * a_ref[...]
        if sign_b != 0.0:
            x = x + sign_b * b_ref[...]
        head = lax.broadcasted_iota(jnp.int32, (LANES, D_MODEL), 0)
        col = lax.broadcasted_iota(jnp.int32, (LANES, D_MODEL), 1)
        base = (head // 2) * LANES + (1 - head % 2) * FOX_HEAD_DIM + piece_entry
        acc = jnp.zeros((tb, D_MODEL), F32)
        for e, piece in enumerate(_split3(x)):
            place = jnp.where((head < FOX_HEADS) & (col == base + e), 1.0, 0.0).astype(BF16)
            acc = acc + _dot(piece, place, NN)
        if ones_entry is not None:
            ent = lax.broadcasted_iota(jnp.int32, (1, D_MODEL), 1) % FOX_HEAD_DIM
            acc = acc + jnp.where((ent >= ones_entry) & (ent < ones_entry + FOX_AUG), 1.0, 0.0)
        o_ref[...] = acc.astype(BF16)

    blk = pl.BlockSpec((tb, LANES), lambda i: (i, 0))
    return _pcall(
        body, name=name, grid=(t // tb,), in_specs=[blk, blk],
        out_specs=pl.BlockSpec((tb, D_MODEL), lambda i: (i, 0)), out_shape=SDS((t, D_MODEL), BF16),
        compiler_params=_params(),
    )(a, b)


def _fox_unpack_dc(dck_wide, dcq):
    t = dck_wide.shape[0]
    dck = dck_wide.reshape(t, FOX_PAIRS, 2, FOX_HEAD_DIM)[:, :, ::-1, 0].reshape(t, FOX_HEADS)
    return dcq[:, :, 0, :].reshape(FOX_HEADS, t).T, dck


def _causal_steps(nq, key_major):
    if key_major:
        pairs = [(i, j) for j in range(nq) for i in range(j, nq)]
    else:
        pairs = [(i, j) for i in range(nq) for j in range(i + 1)]
    return (jnp.asarray([p[0] for p in pairs], jnp.int32), jnp.asarray([p[1] for p in pairs], jnp.int32))


def _pair_operand(low, own, other, hh):
    return jnp.where(low, own, other) if hh == 0 else jnp.where(low, other, own)


def _fox_fwd(proj, qaug, kaug):
    t = proj.shape[0]
    tq = tk = min(FOX_TQ_FWD, t)
    nq = t // tq
    cb = D_MODEL // LANES
    half = min(FOX_CHAIN, tq)

    i_tab, j_tab = _causal_steps(nq, key_major=False)

    def body(i_ref, j_ref, q_ref, k_ref, v_ref, z_ref, qa_ref, ka_ref, o_ref, g_ref, lse_ref, m_s, l_s, acc_s):
        step = pl.program_id(1)
        i, j = i_ref[step], j_ref[step]

        @pl.when(j == 0)
        def _():
            m_s[...] = jnp.full_like(m_s, NEG_INF)
            l_s[...] = jnp.zeros_like(l_s)
            acc_s[...] = jnp.zeros_like(acc_s)

        low = _low_lanes((tq, LANES))
        top = _top_rows((LANES, tq))

        def update(masked):
            qs = q_ref[...] * FOX_SCALE
            qa, k, ka, v = qa_ref[...], k_ref[...], ka_ref[...], v_ref[...]
            if masked:
                causal = (lax.broadcasted_iota(jnp.int32, (tk, tq), 0) <= lax.broadcasted_iota(jnp.int32, (tk, tq), 1))
            one = jnp.ones_like(v)
            chains = [(hh, slice(c * half, (c + 1) * half)) for hh in range(2) for c in range(tq // half)]
            qh = [_pair_operand(low, qs, qa, hh) for hh in range(2)]
            kh = [_pair_operand(low, k, ka, hh) for hh in range(2)]
            vh = [_pair_operand(low, v, one, hh) for hh in range(2)]
            scores = [_dot(kh[hh], qh[hh][cols, :], NT) for hh, cols in chains]
            for (hh, cols), s in zip(chains, scores):
                if masked:
                    s = jnp.where(causal[:, cols], s, NEG_INF)
                m_prev = m_s[hh, :, cols]
                m_new = jnp.maximum(m_prev, jnp.max(s, axis=0, keepdims=True))
                alpha = jnp.exp(m_prev - m_new)
                pv = _dot(vh[hh], jnp.exp(s - m_new).astype(BF16), TN)
                sums = pv[FOX_HEAD_DIM:FOX_HEAD_DIM + FOX_SUM_ROWS, :] if hh == 0 else pv[0:FOX_SUM_ROWS, :]
                l_s[hh, :, cols] = alpha * l_s[hh, :, cols] + sums
                m_s[hh, :, cols] = m_new
                own = top[:, cols] if hh == 0 else jnp.logical_not(top[:, cols])
                acc_s[:, cols] = jnp.where(own, acc_s[:, cols] * alpha + pv, acc_s[:, cols])

        pl.when(j < i)(functools.partial(update, False))
        pl.when(j == i)(functools.partial(update, True))

        @pl.when(j == i)
        def _():
            o = (acc_s[...] / jnp.where(top, l_s[0, 0:1, :], l_s[1, 0:1, :])).T
            z = z_ref[...].astype(F32)
            o_ref[...] = o.astype(BF16)
            g_ref[...] = (o * (z * _sigmoid(z))).astype(BF16)
            for hh in range(2):
                lse_ref[hh] = m_s[hh] + jnp.log(l_s[hh, 0:1, :])

    qblk = lambda col: pl.BlockSpec((tq, LANES), lambda h, s, it, jt: (it[s], col + h))
    kblk = lambda col: pl.BlockSpec((tk, LANES), lambda h, s, it, jt: (jt[s], col + h))
    return _pcall(
        body, name="fox_attn_fwd",
        grid_spec=pltpu.PrefetchScalarGridSpec(
            num_scalar_prefetch=2, grid=(FOX_PAIRS, i_tab.shape[0]),
            in_specs=[qblk(0), kblk(cb), kblk(2 * cb), qblk(3 * cb), qblk(0), kblk(0)],
            out_specs=[qblk(0), qblk(0), pl.BlockSpec((2, 1, tq), lambda h, s, it, jt: (h, 0, it[s]))],
            scratch_shapes=[pltpu.VMEM((2, 1, tq), F32), pltpu.VMEM((2, FOX_SUM_ROWS, tq), F32),
                            pltpu.VMEM((LANES, tq), F32)]),
        out_shape=[SDS((t, D_MODEL), BF16), SDS((t, D_MODEL), BF16), SDS((FOX_HEADS, 1, t), F32)],
        compiler_params=_params(dimension_semantics=("arbitrary", "arbitrary")),
    )(i_tab, j_tab, proj, proj, proj, proj, qaug, kaug)


def _fox_bwd(proj, do, qaug, kaug, doaug):
    t = proj.shape[0]
    tq = tk = min(FOX_TQ, t)
    nq = t // tq
    cb = D_MODEL // LANES

    i_tab, j_tab = _causal_steps(nq, key_major=True)

    def body(i_ref, j_ref, q_ref, k_ref, v_ref, do_ref, qa_ref, ka_ref, da_ref,
             dq_ref, dk_ref, dv_ref, dck_ref, dcq_ref, dq_acc, dcq_acc, dk_acc, dks_acc, dv_acc):
        step = pl.program_id(1)
        i, j = i_ref[step], j_ref[step]
        low = _low_lanes((tq, LANES))
        top = _top_rows((LANES, tq))

        @pl.when(i == j)
        def _():
            dk_acc[...] = jnp.zeros_like(dk_acc)
            dks_acc[...] = jnp.zeros_like(dks_acc)
            dv_acc[...] = jnp.zeros_like(dv_acc)

        def update(masked):
            qs = q_ref[...] * FOX_SCALE
            k, v, dout = k_ref[...], v_ref[...], do_ref[...]
            qa, ka, da = qa_ref[...], ka_ref[...], da_ref[...]
            lane = lax.broadcasted_iota(jnp.int32, (tk, LANES), 1)
            vone = jnp.where((lane & (FOX_HEAD_DIM - 1)) < FOX_AUG, 1.0, 0.0).astype(v.dtype)
            one = jnp.ones_like(k)
            if masked:
                causal = (lax.broadcasted_iota(jnp.int32, (tk, tq), 0) <= lax.broadcasted_iota(jnp.int32, (tk, tq), 1))
            parts = []
            scores = [_dot(_pair_operand(low, k, ka, hh), _pair_operand(low, qs, qa, hh), NT) for hh in range(2)]
            dps = [_dot(_pair_operand(low, v, vone, hh), _pair_operand(low, dout, da, hh), NT) for hh in range(2)]
            for hh in range(2):
                s = scores[hh]
                if masked:
                    s = jnp.where(causal, s, NEG_INF)
                p = jnp.exp(s)
                ds = p * dps[hh]
                pb = p.astype(BF16)
                dsb = ds.astype(BF16)
                parts.append((_dot(pb, dout, NN),
                              _dot(dsb, _pair_operand(low, qs, one, hh), NN),
                              _dot(_pair_operand(low, k, one, hh), dsb, TN)))
            dv_acc[...] += jnp.where(low, parts[0][0], parts[1][0])
            dk_acc[...] += jnp.where(low, parts[0][1], parts[1][1])
            dks_acc[...] += jnp.where(low, parts[1][1], parts[0][1])
            dq_t = jnp.where(top, parts[0][2], parts[1][2]) * FOX_SCALE
            sum_a = parts[0][2][FOX_HEAD_DIM:FOX_HEAD_DIM + FOX_SUM_ROWS, :]
            sum_b = parts[1][2][0:FOX_SUM_ROWS, :]

            @pl.when(j == 0)
            def _():
                dq_acc[i] = dq_t
                dcq_acc[0, i] = sum_a
                dcq_acc[1, i] = sum_b

            @pl.when(j > 0)
            def _():
                dq_acc[i] += dq_t
                dcq_acc[0, i] += sum_a
                dcq_acc[1, i] += sum_b

        pl.when(i > j)(functools.partial(update, False))
        pl.when(i == j)(functools.partial(update, True))

        @pl.when(i == nq - 1)
        def _():
            dk_ref[...] = dk_acc[...].astype(BF16)
            dv_ref[...] = dv_acc[...].astype(BF16)
            dck_ref[...] = dks_acc[...]

        @pl.when((i == nq - 1) & (j == nq - 1))
        def _():
            for blk in range(nq):
                dq_ref[blk * tq:(blk + 1) * tq, :] = dq_acc[blk].T.astype(BF16)
            dcq_ref[...] = dcq_acc[...]

    qblk = lambda col: pl.BlockSpec((tq, LANES), lambda h, s, it, jt: (it[s], col + h))
    kblk = lambda col: pl.BlockSpec((tk, LANES), lambda h, s, it, jt: (jt[s], col + h))
    return _pcall(
        body, name="fox_attn_bwd",
        grid_spec=pltpu.PrefetchScalarGridSpec(
            num_scalar_prefetch=2, grid=(FOX_PAIRS, i_tab.shape[0]),
            in_specs=[qblk(0), kblk(cb), kblk(2 * cb), qblk(0), qblk(0), kblk(0), qblk(0)],
            out_specs=[pl.BlockSpec((t, LANES), lambda h, s, it, jt: (0, h)), kblk(0), kblk(0), kblk(0),
                       pl.BlockSpec((2, nq, FOX_SUM_ROWS, tq), lambda h, s, it, jt: (h, 0, 0, 0))],
            scratch_shapes=[pltpu.VMEM((nq, LANES, tq), F32), pltpu.VMEM((2, nq, FOX_SUM_ROWS, tq), F32),
                            pltpu.VMEM((tk, LANES), F32), pltpu.VMEM((tk, LANES), F32), pltpu.VMEM((tk, LANES), F32)]),
        out_shape=[SDS((t, D_MODEL), BF16), SDS((t, D_MODEL), BF16), SDS((t, D_MODEL), BF16),
                   SDS((t, D_MODEL), F32), SDS((FOX_HEADS, nq, FOX_SUM_ROWS, tq), F32)],
        compiler_params=_params(dimension_semantics=("arbitrary", "arbitrary")),
    )(i_tab, j_tab, proj, proj, proj, do, qaug, kaug, doaug)


def _mm_gate_bwd(dh, w_out, z_src, z_col0, o, heads, name):
    t = dh.shape[0]
    tm = 512
    row = lambda i, j, k: (i, 0)
    zcb = z_col0 // D_MODEL

    def epilogue(acc, e_refs, o_refs, i):
        z = e_refs[0][...].astype(F32)
        ov = e_refs[1][...].astype(F32)
        sg = _sigmoid(z)
        dout = acc * (z * sg)
        o_refs[0][...] = dout.astype(BF16)
        o_refs[1][...] = (acc * ov * (sg * (1.0 + z * (1.0 - sg)))).astype(BF16)
        prod = dout * ov
        for cbk in range(D_MODEL // LANES):
            seg = prod[:, cbk * LANES:(cbk + 1) * LANES]
            tot = jnp.sum(seg, axis=-1, keepdims=True)
            if heads == D_MODEL // LANES:
                o_refs[2][cbk] = tot
            else:
                lo = jnp.sum(jnp.where(_low_lanes(seg.shape), seg, 0.0), axis=-1, keepdims=True)
                o_refs[2][2 * cbk] = lo
                o_refs[2][2 * cbk + 1] = tot - lo

    return _mm([dh], w_out, "nt", name, tiles=(tm, D_MODEL, D_MODEL),
               extras=[(z_src, (tm, D_MODEL), lambda i, j, k: (i, zcb)), (o, (tm, D_MODEL), row)],
               outs=[(SDS((t, D_MODEL), BF16), (tm, D_MODEL), row), (SDS((t, D_MODEL), BF16), (tm, D_MODEL), row),
                     (SDS((heads, t, 1), F32), (heads, tm, 1), lambda i, j, k: (0, i, 0))],
               epilogue=epilogue)


def _ple_fwd(h, pin, w_up, w_gate, name):
    t = h.shape[0]
    tm = 512
    pd = pin.shape[1]

    def body(h_ref, p_ref, wu_ref, wg_ref, hn_ref, u_ref, a_ref):
        h = h_ref[...]
        u = _dot(p_ref[...].astype(BF16), wu_ref[...], NN)
        a = _dot(h.astype(BF16), wg_ref[...], NN)
        hn_ref[...] = h + u * _sigmoid(a)
        u_ref[...] = u.astype(BF16)
        a_ref[...] = a.astype(BF16)

    rows = pl.BlockSpec((tm, D_MODEL), lambda i: (i, 0))
    return _pcall(
        body, name=name, grid=(t // tm,),
        in_specs=[rows, pl.BlockSpec((tm, pd), lambda i: (i, 0)),
                  pl.BlockSpec((pd, D_MODEL), lambda i: (0, 0)), pl.BlockSpec((D_MODEL, D_MODEL), lambda i: (0, 0))],
        out_specs=[rows, rows, rows],
        out_shape=[SDS((t, D_MODEL), F32), SDS((t, D_MODEL), BF16), SDS((t, D_MODEL), BF16)],
        compiler_params=_params(),
    )(h, pin, w_up, w_gate)


def _ple_bwd_elem(dh, u, a, name):
    t = dh.shape[0]
    tm = 512

    def body(dh_ref, u_ref, a_ref, du_ref, da_ref):
        g = dh_ref[...]
        s = _sigmoid(a_ref[...].astype(F32))
        du_ref[...] = (g * s).astype(BF16)
        da_ref[...] = (g * u_ref[...].astype(F32) * (s * (1.0 - s))).astype(BF16)

    blk = pl.BlockSpec((tm, D_MODEL), lambda i: (i, 0))
    return _pcall(
        body, name=name, grid=(t // tm,), in_specs=[blk, blk, blk], out_specs=[blk, blk],
        out_shape=[SDS((t, D_MODEL), BF16), SDS((t, D_MODEL), BF16)], compiler_params=_params(),
    )(dh, u, a)


DIL_SCALE = LANES ** -0.5


def _dil_masks():
    ii = lax.broadcasted_iota(jnp.int32, (DIL_BLOCK, DIL_BLOCK), 0)
    jj = lax.broadcasted_iota(jnp.int32, (DIL_BLOCK, DIL_BLOCK), 1)
    return ii, jj


DIL_UNITS = 16
BNT = (((2,), (2,)), ((0,), (0,)))
BNN = (((2,), (1,)), ((0,), (0,)))
BTN = (((1,), (1,)), ((0,), (0,)))


def _dil_units(dil):
    return [(b, r) for b in range(DIL_UNITS // dil) for r in range(dil)]


def _unit_rows(b, r, dil):
    return pl.ds(b * DIL_BLOCK * dil + r, DIL_BLOCK, stride=dil)


def _gather_units(cur, dil, shift=0, edge=None, lead=()):
    nbk = DIL_UNITS // dil
    parts = []
    for b, r in _dil_units(dil):
        bb = b + shift
        if 0 <= bb < nbk:
            parts.append(cur[lead + (_unit_rows(bb, r, dil), slice(None))])
        else:
            parts.append(edge[lead + (pl.ds(r, DIL_BLOCK, stride=dil), slice(None))])
    return jnp.stack(parts)


def _scatter_units(dst, val, dil):
    for u, (b, r) in enumerate(_dil_units(dil)):
        dst[_unit_rows(b, r, dil), :] = val[u]


def _dil_bias(slope, prev):
    ii, jj = _dil_masks()
    dist = (DIL_BLOCK + ii - jj) if prev else (ii - jj)
    return (slope * dist.astype(F32))[None], ((jj >= ii) if prev else (jj <= ii))[None]


def _dil_fwd(proj, slopes, grp, dil, name):
    t = proj.shape[0]
    rows = DIL_BLOCK * DIL_UNITS
    edge_rows = DIL_BLOCK * dil
    nbk = DIL_UNITS // dil
    nsb = t // rows
    qc, kc_, vc_ = grp * DIL_HEADS, 3 * DIL_HEADS + grp * DIL_HEADS, 6 * DIL_HEADS + grp * DIL_HEADS

    def body(q_ref, kp_ref, kc_ref, vp_ref, vc_ref, sl_ref, o_ref, lse_ref, qf, kpf, kcf, vpf, vcf, of, lf):
        m = pl.program_id(1)
        for src, dst in ((q_ref, qf), (kp_ref, kpf), (kc_ref, kcf), (vp_ref, vpf), (vc_ref, vcf)):
            dst[...] = src[...].astype(F32)
        slope = sl_ref[0]
        unit = lax.broadcasted_iota(jnp.int32, (DIL_UNITS, 1, 1), 0)
        has_prev = (unit >= dil) | (m > 0)
        q = _gather_units(qf, dil).astype(BF16)
        kc, vc = _gather_units(kcf, dil).astype(BF16), _gather_units(vcf, dil).astype(BF16)
        kp, vp = _gather_units(kcf, dil, -1, kpf).astype(BF16), _gather_units(vcf, dil, -1, vpf).astype(BF16)
        bias_p, ok_p = _dil_bias(slope, True)
        bias_c, ok_c = _dil_bias(slope, False)
        sp = jnp.where(ok_p & has_prev, _dot(q, kp, BNT) * DIL_SCALE - bias_p, NEG_INF)
        sc = jnp.where(ok_c, _dot(q, kc, BNT) * DIL_SCALE - bias_c, NEG_INF)
        mx = jnp.maximum(jnp.max(sp, axis=-1, keepdims=True), jnp.max(sc, axis=-1, keepdims=True))
        pp = jnp.exp(sp - mx)
        pc = jnp.exp(sc - mx)
        l = jnp.sum(pp, axis=-1, keepdims=True) + jnp.sum(pc, axis=-1, keepdims=True)
        o = (_dot(pp.astype(BF16), vp, BNN) + _dot(pc.astype(BF16), vc, BNN)) / l
        _scatter_units(of, o, dil)
        _scatter_units(lf, mx + jnp.log(l), dil)
        o_ref[...] = of[...].astype(BF16)
        lse_ref[0] = lf[...]

    cur = lambda col: pl.BlockSpec((rows, LANES), lambda h, m: (m, col + h))
    prev = lambda col: pl.BlockSpec((edge_rows, LANES), lambda h, m: (jnp.maximum(m * nbk - 1, 0), col + h))
    return _pcall(
        body, name=name, grid=(DIL_HEADS, nsb),
        in_specs=[cur(qc), prev(kc_), cur(kc_), prev(vc_), cur(vc_), pl.BlockSpec((1, 1, 1), lambda h, m: (h, 0, 0))],
        out_specs=[pl.BlockSpec((rows, LANES), lambda h, m: (m, h)), pl.BlockSpec((1, rows, 1), lambda h, m: (h, m, 0))],
        out_shape=[SDS((t, D_MODEL), BF16), SDS((DIL_HEADS, t, 1), F32)],
        scratch_shapes=[pltpu.VMEM((rows, LANES), F32), pltpu.VMEM((edge_rows, LANES), F32), pltpu.VMEM((rows, LANES), F32),
                        pltpu.VMEM((edge_rows, LANES), F32), pltpu.VMEM((rows, LANES), F32), pltpu.VMEM((rows, LANES), F32),
                        pltpu.VMEM((rows, 1), F32)],
        compiler_params=_params(),
    )(proj, proj, proj, proj, proj, slopes)


def _dil_mix(outs, lses, proj, z_col0):
    t = proj.shape[0]
    tm = 512
    zcb = z_col0 // LANES
    ng = len(outs)

    def body(*refs):
        o_refs, l_refs, z_ref = refs[:ng], refs[ng:2 * ng], refs[2 * ng]
        om_ref, g_ref, lse_ref = refs[2 * ng + 1:]
        ls = [r[0] for r in l_refs]
        mx = functools.reduce(jnp.maximum, ls)
        es = [jnp.exp(l - mx) for l in ls]
        tot = functools.reduce(jnp.add, es)
        o = functools.reduce(jnp.add, [(e / tot) * r[...].astype(F32) for e, r in zip(es, o_refs)])
        z = z_ref[...].astype(F32)
        om_ref[...] = o.astype(BF16)
        g_ref[...] = (o * (z * _sigmoid(z))).astype(BF16)
        lse_ref[0] = mx + jnp.log(tot)

    tile = pl.BlockSpec((tm, LANES), lambda i, h: (i, h))
    col = pl.BlockSpec((1, tm, 1), lambda i, h: (h, i, 0))
    return _pcall(
        body, name="dil_mix", grid=(t // tm, DIL_HEADS),
        in_specs=[tile] * ng + [col] * ng + [pl.BlockSpec((tm, LANES), lambda i, h: (i, zcb + h))],
        out_specs=[tile, tile, col],
        out_shape=[SDS((t, D_MODEL), BF16), SDS((t, D_MODEL), BF16), SDS((DIL_HEADS, t, 1), F32)],
        compiler_params=_params(),
    )(*outs, *lses, proj)


def _dil_bwd(proj, do, lse, delta, slopes, grp, dil, name):
    t = proj.shape[0]
    rows = DIL_BLOCK * DIL_UNITS
    edge_rows = DIL_BLOCK * dil
    nbk = DIL_UNITS // dil
    nsb = t // rows
    last_edge = t // edge_rows - 1
    qc, kc_, vc_ = grp * DIL_HEADS, 3 * DIL_HEADS + grp * DIL_HEADS, 6 * DIL_HEADS + grp * DIL_HEADS

    def body(q_ref, qn_ref, kp_ref, kc_ref, vp_ref, vc_ref, do_ref, don_ref, l_ref, ln_ref, d_ref, dn_ref, sl_ref,
             dq_ref, dk_ref, dv_ref, qf, qnf, kpf, kcf, vpf, vcf, dof, donf, dqf, dkf, dvf):
        m = pl.program_id(1)
        for src, dst in ((q_ref, qf), (qn_ref, qnf), (kp_ref, kpf), (kc_ref, kcf), (vp_ref, vpf), (vc_ref, vcf),
                         (do_ref, dof), (don_ref, donf)):
            dst[...] = src[...].astype(F32)
        slope = sl_ref[0]
        unit = lax.broadcasted_iota(jnp.int32, (DIL_UNITS, 1, 1), 0)
        has_prev = (unit >= dil) | (m > 0)
        has_next = (unit < DIL_UNITS - dil) | (m < nsb - 1)
        b16 = lambda x: x.astype(BF16)
        q, kc, vc, dout = (b16(_gather_units(x, dil)) for x in (qf, kcf, vcf, dof))
        kp, vp = b16(_gather_units(kcf, dil, -1, kpf)), b16(_gather_units(vcf, dil, -1, vpf))
        qn, don = b16(_gather_units(qf, dil, 1, qnf)), b16(_gather_units(dof, dil, 1, donf))
        lrow, drow = _gather_units(l_ref, dil, lead=(0,)), _gather_units(d_ref, dil, lead=(0,))
        lnrow = _gather_units(l_ref, dil, 1, ln_ref, lead=(0,))
        dnrow = _gather_units(d_ref, dil, 1, dn_ref, lead=(0,))
        bias_p, ok_p = _dil_bias(slope, True)
        bias_c, ok_c = _dil_bias(slope, False)
        sp = jnp.where(ok_p & has_prev, _dot(q, kp, BNT) * DIL_SCALE - bias_p, NEG_INF)
        sc = jnp.where(ok_c, _dot(q, kc, BNT) * DIL_SCALE - bias_c, NEG_INF)
        pp = jnp.exp(sp - lrow)
        pc = jnp.exp(sc - lrow)
        dsp = b16(pp * (_dot(dout, vp, BNT) - drow))
        dsc = b16(pc * (_dot(dout, vc, BNT) - drow))
        _scatter_units(dqf, (_dot(dsp, kp, BNN) + _dot(dsc, kc, BNN)) * DIL_SCALE, dil)
        sn = jnp.where(ok_p & has_next, _dot(qn, kc, BNT) * DIL_SCALE - bias_p, NEG_INF)
        pn = jnp.exp(sn - lnrow)
        dsn = b16(pn * (_dot(don, vc, BNT) - dnrow))
        _scatter_units(dkf, (_dot(dsc, q, BTN) + _dot(dsn, qn, BTN)) * DIL_SCALE, dil)
        _scatter_units(dvf, _dot(b16(pc), dout, BTN) + _dot(b16(pn), don, BTN), dil)
        dq_ref[...] = dqf[...].astype(BF16)
        dk_ref[...] = dkf[...].astype(BF16)
        dv_ref[...] = dvf[...].astype(BF16)

    prev_i = lambda m: jnp.maximum(m * nbk - 1, 0)
    next_i = lambda m: jnp.minimum((m + 1) * nbk, last_edge)
    cur = lambda col: pl.BlockSpec((rows, LANES), lambda h, m: (m, col + h))
    edge = lambda col, f: pl.BlockSpec((edge_rows, LANES), lambda h, m: (f(m), col + h))
    colcur = pl.BlockSpec((1, rows, 1), lambda h, m: (h, m, 0))
    colnext = pl.BlockSpec((1, edge_rows, 1), lambda h, m: (h, next_i(m), 0))
    out_blk = pl.BlockSpec((rows, LANES), lambda h, m: (m, h))
    big, small = pltpu.VMEM((rows, LANES), F32), pltpu.VMEM((edge_rows, LANES), F32)
    return _pcall(
        body, name=name, grid=(DIL_HEADS, nsb),
        in_specs=[cur(qc), edge(qc, next_i), edge(kc_, prev_i), cur(kc_), edge(vc_, prev_i), cur(vc_),
                  cur(0), edge(0, next_i), colcur, colnext, colcur, colnext,
                  pl.BlockSpec((1, 1, 1), lambda h, m: (h, 0, 0))],
        out_specs=[out_blk, out_blk, out_blk],
        out_shape=[SDS((t, D_MODEL), BF16)] * 3,
        scratch_shapes=[big, small, small, big, small, big, big, small, big, big, big],
        compiler_params=_params(),
    )(proj, proj, proj, proj, proj, proj, do, do, lse, lse, delta, delta, slopes)


def _mesh_pos():
    x, y, c = lax.axis_index("x"), lax.axis_index("y"), lax.axis_index("c")
    return x, y, c


def _peer(pos, k):
    x, y, c = pos
    px = 1 - x if k & 4 else x
    py = 1 - y if k & 2 else y
    pc = 1 - c if k & 1 else c
    return (px, py, pc), 4 * px + 2 * py + pc


N_CHIPS = 4
CHIP_FLIPS = ((1, 0), (0, 1), (1, 1))


def _other_chips(x, y):
    return [(1 - x if fx else x, 1 - y if fy else y) for fx, fy in CHIP_FLIPS]


def _all_gather(arrays):
    n = len(arrays)
    per = 2 * N_CHIPS - 1
    hbm = pl.BlockSpec(memory_space=pltpu.HBM)

    def body(*refs):
        ins, outs = refs[:n], refs[n:2 * n]
        send_sems, recv_sems, local_sems = refs[2 * n:]
        x, y, c = _mesh_pos()
        sibling = (x, y, 1 - c)
        chips = _other_chips(x, y)
        block = lambda px, py, pc: 4 * px + 2 * py + pc

        def copy(w, k, src, blk, to):
            return pltpu.make_async_remote_copy(
                src_ref=src, dst_ref=outs[w].at[blk], send_sem=send_sems.at[w * per + k],
                recv_sem=recv_sems.at[w * per + k], device_id=to, device_id_type=MESH)

        local, started = [], []
        for w in range(n):
            cp = pltpu.make_async_copy(ins[w], outs[w].at[block(x, y, c)], local_sems.at[w])
            cp.start()
            local.append(cp)
            started.append(copy(w, 0, ins[w], block(x, y, c), sibling))
            for j, (px, py) in enumerate(chips):
                started.append(copy(w, 1 + j, ins[w], block(x, y, c), (px, py, c)))
        for cp in started:
            cp.start()
        for j, (px, py) in enumerate(chips):
            for w in range(n):
                copy(w, 1 + j, ins[w], block(px, py, c), sibling).wait_recv()
                cp = copy(w, 4 + j, outs[w].at[block(px, py, c)], block(px, py, c), sibling)
                cp.start()
                started.append(cp)
        for w in range(n):
            copy(w, 0, ins[w], block(x, y, 1 - c), sibling).wait_recv()
            for j, (px, py) in enumerate(chips):
                copy(w, 4 + j, ins[w], block(px, py, 1 - c), sibling).wait_recv()
        for cp in started:
            cp.wait_send()
        for cp in local:
            cp.wait()

    return _pcall(
        body, name="all_gather_weights", in_specs=[hbm] * n, out_specs=[hbm] * n,
        out_shape=[SDS((N_DEV,) + a.shape, a.dtype) for a in arrays],
        scratch_shapes=[pltpu.SemaphoreType.DMA((n * per,)), pltpu.SemaphoreType.DMA((n * per,)),
                        pltpu.SemaphoreType.DMA((n,))],
    )(*arrays)


def _core_exchange(slabs):
    n = len(slabs)
    hbm = pl.BlockSpec(memory_space=pltpu.HBM)

    def body(*refs):
        ins, outs = refs[:n], refs[n:2 * n]
        send_sems, recv_sems = refs[2 * n:]
        x, y, c = _mesh_pos()
        copies = [pltpu.make_async_remote_copy(
            src_ref=ins[w].at[pl.ds(0, N_CHIPS), 1 - c], dst_ref=outs[w], send_sem=send_sems.at[w],
            recv_sem=recv_sems.at[w], device_id=(x, y, 1 - c), device_id_type=MESH) for w in range(n)]
        for cp in copies:
            cp.start()
        for cp in copies:
            cp.wait_recv()
        for cp in copies:
            cp.wait_send()

    return _pcall(
        body, name="grads_core_exchange", in_specs=[hbm] * n, out_specs=[hbm] * n,
        out_shape=[SDS((N_CHIPS,) + a.shape[2:], a.dtype) for a in slabs],
        scratch_shapes=[pltpu.SemaphoreType.DMA((n,)), pltpu.SemaphoreType.DMA((n,))],
    )(*slabs)


def _chip_exchange(partials):
    n = len(partials)
    per = N_CHIPS - 1
    hbm = pl.BlockSpec(memory_space=pltpu.HBM)

    def body(*refs):
        ins, outs = refs[:n], refs[n:2 * n]
        send_sems, recv_sems, local_sems = refs[2 * n:]
        x, y, c = _mesh_pos()
        mine = 2 * x + y
        local, sends, recvs = [], [], []
        for w in range(n):
            cp = pltpu.make_async_copy(ins[w].at[mine], outs[w].at[mine], local_sems.at[w])
            cp.start()
            local.append(cp)
            for j, (px, py) in enumerate(_other_chips(x, y)):
                theirs = 2 * px + py
                sems = dict(send_sem=send_sems.at[w * per + j], recv_sem=recv_sems.at[w * per + j],
                            device_id=(px, py, c), device_id_type=MESH)
                cp = pltpu.make_async_remote_copy(src_ref=ins[w].at[theirs], dst_ref=outs[w].at[mine], **sems)
                cp.start()
                sends.append(cp)
                recvs.append(pltpu.make_async_remote_copy(src_ref=ins[w].at[theirs], dst_ref=outs[w].at[theirs], **sems))
        for cp in recvs:
            cp.wait_recv()
        for cp in sends:
            cp.wait_send()
        for cp in local:
            cp.wait()

    return _pcall(
        body, name="grads_chip_exchange", in_specs=[hbm] * n, out_specs=[hbm] * n,
        out_shape=[SDS(a.shape, a.dtype) for a in partials],
        scratch_shapes=[pltpu.SemaphoreType.DMA((n * per,)), pltpu.SemaphoreType.DMA((n * per,)),
                        pltpu.SemaphoreType.DMA((n,))],
    )(*partials)


def _core_sum(slabs, from_sibling, core, name):
    _, _, r, c = slabs.shape
    tr = min(r, 256)

    def body(core_ref, a_ref, b_ref, o_ref):
        o_ref[...] = (a_ref[0].astype(F32) + b_ref[...].astype(F32)).astype(BF16)

    return _pcall(
        body, name=name,
        grid_spec=pltpu.PrefetchScalarGridSpec(
            num_scalar_prefetch=1, grid=(N_CHIPS, r // tr),
            in_specs=[pl.BlockSpec((1, 1, tr, c), lambda g, i, core: (g, core[0], i, 0)),
                      pl.BlockSpec((1, tr, c), lambda g, i, core: (g, i, 0))],
            out_specs=pl.BlockSpec((1, tr, c), lambda g, i, core: (g, i, 0))),
        out_shape=SDS((N_CHIPS, r, c), BF16), compiler_params=_params(),
    )(core, slabs, from_sibling)


def _adam_math(w, g, m, v):
    m = ADAM_B1 * m + (1.0 - ADAM_B1) * g
    v = ADAM_B2 * v + (1.0 - ADAM_B2) * (g * g)
    m_hat = m / (1.0 - ADAM_B1 ** ADAM_STEP)
    v_hat = v / (1.0 - ADAM_B2 ** ADAM_STEP)
    delta = -ADAM_LR * (m_hat / (jnp.sqrt(v_hat) + ADAM_EPS) + ADAM_WD * w)
    return delta, m, v


def _adamw(recv, w, m, v, name):
    r, c = w.shape
    tr = min(r, 128)
    n_parts = recv.shape[0]

    def body(g_ref, w_ref, m_ref, v_ref, go_ref, d_ref, mo_ref, vo_ref):
        g = g_ref[0].astype(F32)
        for s in range(1, n_parts):
            g = g + g_ref[s].astype(F32)
        delta, mn, vn = _adam_math(w_ref[...], g, m_ref[...], v_ref[...])
        go_ref[...] = g
        d_ref[...] = delta
        mo_ref[...] = mn
        vo_ref[...] = vn

    blk = pl.BlockSpec((tr, c), lambda i: (i, 0))
    return _pcall(
        body, name=name, grid=(r // tr,),
        in_specs=[pl.BlockSpec((n_parts, tr, c), lambda i: (0, i, 0)), blk, blk, blk],
        out_specs=[blk] * 4, out_shape=[SDS((r, c), F32)] * 4, compiler_params=_params(),
    )(recv, w, m, v)


VEC_ROWS = 32


def _small_allreduce_adamw(vec, w, m, v):
    def body(vec_ref, w_ref, m_ref, v_ref, g_ref, d_ref, mo_ref, vo_ref, gath, send_sems, recv_sems):
        pos = _mesh_pos()
        me = 4 * pos[0] + 2 * pos[1] + pos[2]
        sends, recvs = [], []
        for k in range(1, N_DEV):
            peer, peer_idx = _peer(pos, k)
            cp = pltpu.make_async_remote_copy(src_ref=vec_ref, dst_ref=gath.at[me], send_sem=send_sems.at[k - 1],
                                              recv_sem=recv_sems.at[k - 1], device_id=peer, device_id_type=MESH)
            cp.start()
            sends.append(cp)
            recvs.append(pltpu.make_async_remote_copy(src_ref=vec_ref, dst_ref=gath.at[peer_idx],
                                                      send_sem=send_sems.at[k - 1], recv_sem=recv_sems.at[k - 1],
                                                      device_id=peer, device_id_type=MESH))
        gath[me] = vec_ref[...]
        for cp in recvs:
            cp.wait_recv()
        for cp in sends:
            cp.wait_send()
        tot = gath[0]
        for s in range(1, N_DEV):
            tot = tot + gath[s]
        rowi = lax.broadcasted_iota(jnp.int32, (8, LANES), 0)
        mine = jnp.sum(jnp.where(rowi == me, tot[16:24, :], 0.0), axis=0, keepdims=True)
        g = jnp.concatenate([tot[0:16, :], jnp.broadcast_to(mine, (8, LANES)), tot[24:32, :]], axis=0)
        delta, mn, vn = _adam_math(w_ref[...], g, m_ref[...], v_ref[...])
        g_ref[...] = g
        d_ref[...] = delta
        mo_ref[...] = mn
        vo_ref[...] = vn

    vm = pl.BlockSpec(memory_space=pltpu.VMEM)
    return _pcall(
        body, name="small_allreduce_adamw", in_specs=[vm] * 4, out_specs=[vm] * 4,
        out_shape=[SDS((VEC_ROWS, LANES), F32)] * 4,
        scratch_shapes=[pltpu.VMEM((N_DEV, VEC_ROWS, LANES), F32), pltpu.SemaphoreType.DMA((N_DEV - 1,)),
                        pltpu.SemaphoreType.DMA((N_DEV - 1,))],
        compiler_params=pltpu.CompilerParams(has_side_effects=True),
    )(vec, w, m, v)


def _cols_to_slabs(a):
    r, c8 = a.shape
    return a.reshape(r, N_DEV, c8 // N_DEV).transpose(1, 0, 2)


def _slabs_to_cols(a):
    n, r, c = a.shape
    return a.transpose(1, 0, 2).reshape(r, n * c)


def _rows8(vec):
    return vec.reshape(-1, LANES)


def _pad_rows(a, rows):
    return jnp.pad(a, ((0, rows - a.shape[0]), (0, LANES - a.shape[1])))


def kernel(x, p, fox_norm, fox_w_in, fox_b_f, fox_w_out, dil_norm, dil_w_in, dil_w_out, ple_w_up, ple_w_gate, final_norm, loss_target, m_fox_norm, m_fox_w_in, m_fox_b_f, m_fox_w_out, m_dil_norm, m_dil_w_in, m_dil_w_out, m_ple_w_up, m_ple_w_gate, m_final_norm, v_fox_norm, v_fox_w_in, v_fox_b_f, v_fox_w_out, v_dil_norm, v_dil_w_in, v_dil_w_out, v_ple_w_up, v_ple_w_gate, v_final_norm):
    t = x.shape[1]
    d = D_MODEL
    xs, tgt = x[0], loss_target[0]
    p0, p1 = p[0, 0], p[1, 0]
    fox_cols = fox_w_in.shape[2]
    ple_dim = ple_w_up.shape[1]

    shards = [fox_w_in[0].astype(BF16), fox_w_out[0].astype(BF16), dil_w_in[0].astype(BF16),
              dil_w_out[0].astype(BF16), ple_w_up.reshape(-1, LANES).astype(BF16),
              ple_w_gate.reshape(-1, d).astype(BF16), dil_norm]
    gw = _all_gather(shards)
    w_fox_in = _slabs_to_cols(gw[0])
    w_fox_main = w_fox_in[:, :4 * d]
    w_fox_f = jnp.pad(w_fox_in[:, 4 * d:], ((0, 0), (0, LANES - FOX_HEADS)))
    w_fox_out = gw[1].reshape(d, d)
    w_dil_in = _slabs_to_cols(gw[2])
    w_dil_out = gw[3].reshape(d, d)
    w_up = gw[4].reshape(N_DEV, 2, ple_dim, LANES).transpose(1, 2, 0, 3).reshape(2, ple_dim, d)
    w_gate = gw[5].reshape(N_DEV, 2, d // N_DEV, d).transpose(1, 0, 2, 3).reshape(2, d, d)
    dil_norm_full = gw[6].reshape(1, d)
    b_pad = jnp.pad(fox_b_f, ((0, 0), (0, LANES - FOX_HEADS)))

    n0, r0 = _rms_fwd(xs, fox_norm, "rms_fox")
    proj0 = _mm([n0], w_fox_main, "nn", "fox_in_proj", tiles=IN_PROJ_TILES)
    projf = _mm([n0], w_fox_f, "nn", "fox_gate_proj", tiles=IN_PROJ_TILES, out_dtype=F32)
    c_all = _fox_gate_fwd(projf, b_pad)
    qaug_fwd = _fox_aug(c_all, c_all, 1.0, 0.0, 0, FOX_AUG, "fox_aug_q_fwd")
    kaug = _fox_aug(c_all, c_all, -1.0, 0.0, FOX_AUG, 0, "fox_aug_k")
    o0, g0, lse0 = _fox_fwd(proj0, qaug_fwd, kaug)
    h1 = _mm_residual(g0, w_fox_out, "nn", xs, "fox_out_proj")
    h2, u0, a0 = _ple_fwd(h1, p0, w_up[0], w_gate[0], "ple0_fwd")

    n1, r1 = _rms_fwd(h2, dil_norm_full, "rms_dil")
    proj1 = _mm([n1], w_dil_in, "nn", "dil_in_proj", tiles=IN_PROJ_TILES)
    n_heads = len(DIL_PATTERN) * DIL_HEADS
    slopes = 2.0 ** (-ALIBI_MAX_EXP * jnp.arange(1, n_heads + 1, dtype=F32) / n_heads)
    dil_o, dil_lse, dil_slopes = [], [], []
    for grp, (_, dil) in enumerate(DIL_PATTERN):
        sl = (slopes[grp * DIL_HEADS:(grp + 1) * DIL_HEADS] * dil).reshape(DIL_HEADS, 1, 1)
        og, lg = _dil_fwd(proj1, sl, grp, dil, f"dil_attn_fwd_{grp}")
        dil_o.append(og)
        dil_lse.append(lg)
        dil_slopes.append(sl)
    z1_col0 = 9 * d
    o1, g1, lse1 = _dil_mix(dil_o, dil_lse, proj1, z1_col0)
    h3 = _mm_residual(g1, w_dil_out, "nn", h2, "dil_out_proj")
    h4, u1, a1 = _ple_fwd(h3, p1, w_up[1], w_gate[1], "ple1_fwd")

    dh4, d_final_norm, loss_part = _final_bwd(h4, final_norm.reshape(1, d), tgt)

    du1, da1 = _ple_bwd_elem(dh4, u1, a1, "ple1_bwd_elem")
    dw_up1 = _dw(p1, du1, "ple1_dw_up")
    dw_gate1 = _dw(h3, da1, "ple1_dw_gate")
    dh3 = _mm_residual(da1, w_gate[1], "nt", dh4, "ple1_dh")

    dw_dil_out = _dw(g1, dh3, "dil_dw_out")
    do1, dz1, delta1 = _mm_gate_bwd(dh3, w_dil_out, proj1, z1_col0, o1, DIL_HEADS, "dil_dgate")
    dqs, dks, dvs = [], [], []
    for grp, (_, dil) in enumerate(DIL_PATTERN):
        dq, dk, dv = _dil_bwd(proj1, do1, lse1, delta1, dil_slopes[grp], grp, dil, f"dil_attn_bwd_{grp}")
        dqs.append(dq)
        dks.append(dk)
        dvs.append(dv)
    dproj1 = dqs + dks + dvs + [dz1]
    dw_dil_in = jnp.concatenate([_dw(n1, dpart, f"dil_dw_in_{s}") for s, dpart in enumerate(dproj1)], axis=1)
    dh2, d_dil_norm = _mm_in_bwd(dproj1, w_dil_in, h2, dil_norm_full, r1, dh3, "dil_dx")

    du0, da0 = _ple_bwd_elem(dh2, u0, a0, "ple0_bwd_elem")
    dw_up0 = _dw(p0, du0, "ple0_dw_up")
    dw_gate0 = _dw(h1, da0, "ple0_dw_gate")
    dh1 = _mm_residual(da0, w_gate[0], "nt", dh2, "ple0_dh")

    dw_fox_out = _dw(g0, dh1, "fox_dw_out")
    do0, dz0, delta0 = _mm_gate_bwd(dh1, w_fox_out, proj0, 3 * d, o0, FOX_HEADS, "fox_dgate")
    head_cols = lambda a: jnp.pad(a, ((0, 0), (0, LANES - FOX_HEADS)))
    lse_cols = head_cols(lse0.reshape(FOX_HEADS, t).T)
    delta_cols = head_cols(delta0.reshape(FOX_HEADS, t).T)
    qaug_bwd = _fox_aug(c_all, lse_cols, 1.0, -1.0, 0, FOX_AUG, "fox_aug_q_bwd")
    doaug = _fox_aug(delta_cols, delta_cols, -1.0, 0.0, 0, None, "fox_aug_do")
    dq0, dk0, dv0, dck_wide, dcq = _fox_bwd(proj0, do0, qaug_bwd, kaug, doaug)
    dc_query, dc_key = _fox_unpack_dc(dck_wide, dcq)
    df, d_b_f = _fox_gate_bwd(projf, b_pad, head_cols(dc_query), head_cols(dc_key))
    dproj0 = [dq0, dk0, dv0, dz0]
    dw_fox_parts = [_dw(n0, dpart, f"fox_dw_in_{s}") for s, dpart in enumerate(dproj0)]
    dw_fox_f = _dw(n0, df, "fox_dw_gate")
    dn0_f = _mm([df], w_fox_f, "nt", "fox_dx_gate", out_dtype=F32)
    grad_x, d_fox_norm = _mm_in_bwd(dproj0, w_fox_main, xs, fox_norm, r0, dh1, "fox_dx", more=dn0_f)

    dw_fox_in = jnp.concatenate(dw_fox_parts + [dw_fox_f[:, :FOX_HEADS]], axis=1)
    slabs = [_cols_to_slabs(dw_fox_in), dw_fox_out.reshape(N_DEV, d // N_DEV, d), _cols_to_slabs(dw_dil_in),
             dw_dil_out.reshape(N_DEV, d // N_DEV, d),
             jnp.stack([dw_up0, dw_up1]).reshape(2, ple_dim, N_DEV, LANES).transpose(2, 0, 1, 3).reshape(N_DEV, -1, LANES),
             jnp.stack([dw_gate0, dw_gate1]).reshape(2, N_DEV, d // N_DEV, d).transpose(1, 0, 2, 3).reshape(N_DEV, -1, d)]
    names = ["fox_w_in", "fox_w_out", "dil_w_in", "dil_w_out", "ple_w_up", "ple_w_gate"]
    slabs = [s.reshape((N_CHIPS, 2) + s.shape[1:]) for s in slabs]
    from_sibling = _core_exchange(slabs)
    core = lax.axis_index("c").astype(jnp.int32).reshape(1)
    chip_sums = [_core_sum(s, f, core, "core_sum_" + nm) for s, f, nm in zip(slabs, from_sibling, names)]
    recv = _chip_exchange(chip_sums)
    big = [(fox_w_in, m_fox_w_in, v_fox_w_in), (fox_w_out, m_fox_w_out, v_fox_w_out),
           (dil_w_in, m_dil_w_in, v_dil_w_in), (dil_w_out, m_dil_w_out, v_dil_w_out),
           (ple_w_up, m_ple_w_up, v_ple_w_up), (ple_w_gate, m_ple_w_gate, v_ple_w_gate)]
    upd = {}
    for rv, (w, m, v), nm in zip(recv, big, names):
        shp2 = rv.shape[1:]
        res = _adamw(rv, w.reshape(shp2), m.reshape(shp2), v.reshape(shp2), "adamw_" + nm)
        upd[nm] = [a.reshape(w.shape) for a in res]

    loss_row = jnp.where(jnp.arange(LANES) == 0, loss_part, 0.0)
    vec = jnp.concatenate([_rows8(d_fox_norm), _rows8(d_final_norm), _rows8(d_dil_norm), d_b_f, loss_row,
                           jnp.zeros((VEC_ROWS - 26, LANES), F32)], axis=0)

    def small_pack(a_fox_norm, a_final_norm, a_dil_norm, a_b_f):
        return jnp.concatenate([_rows8(a_fox_norm), _rows8(a_final_norm), _pad_rows(a_dil_norm, 8),
                                _pad_rows(a_b_f, 8)], axis=0)

    sg, sd, sm, sv = _small_allreduce_adamw(
        vec, small_pack(fox_norm, final_norm, dil_norm, fox_b_f),
        small_pack(m_fox_norm, m_final_norm, m_dil_norm, m_fox_b_f),
        small_pack(v_fox_norm, v_final_norm, v_dil_norm, v_fox_b_f))

    def small_unpack(a):
        return {"fox_norm": a[0:8].reshape(1, d), "final_norm": a[8:16].reshape(d), "dil_norm": a[16:17],
                "fox_b_f": a[24:25, :FOX_HEADS]}

    loss = sg[25, 0]
    order = ["fox_norm", "fox_w_in", "fox_b_f", "fox_w_out", "dil_norm", "dil_w_in", "dil_w_out", "ple_w_up",
             "ple_w_gate", "final_norm"]
    out = [loss, grad_x[None]]
    for idx, small in enumerate((sg, sd, sm, sv)):
        sp = small_unpack(small)
        out += [sp[nm] if nm in sp else upd[nm][idx] for nm in order]
    return tuple(out)
```

```python
import functools

import jax
import jax.numpy as jnp
from jax import lax
from jax.experimental import pallas as pl
from jax.experimental.pallas import tpu as pltpu

F32 = jnp.float32
BF16 = jnp.bfloat16
SDS = jax.ShapeDtypeStruct

D_MODEL = 1024
N_DEV = 8
LANES = 128
FOX_HEADS = 16
FOX_HEAD_DIM = 64
FOX_PAIRS = FOX_HEADS // 2
DIL_HEADS = 8
DIL_BLOCK = 128
DIL_PATTERN = ((128, 1), (512, 4), (2048, 16))
ALIBI_MAX_EXP = 8.0
RMS_EPS = 1e-6
ADAM_LR, ADAM_B1, ADAM_B2, ADAM_EPS, ADAM_WD, ADAM_STEP = 0.001, 0.9, 0.999, 1e-08, 0.01, 10
VMEM_LIMIT = 48 * 1024 * 1024
NEG_INF = float("-inf")

NN = (((1,), (0,)), ((), ()))
NT = (((1,), (1,)), ((), ()))
TN = (((0,), (0,)), ((), ()))
MESH = pl.DeviceIdType.MESH


def _pcall(body, **kw):
    return pl.pallas_call(body, **kw)


def _params(**kw):
    return pltpu.CompilerParams(vmem_limit_bytes=VMEM_LIMIT, **kw)


def _dot(a, b, dims):
    return lax.dot_general(a, b, dims, preferred_element_type=F32)


def _sigmoid(x):
    return 1.0 / (1.0 + jnp.exp(-x))


def _mm(a_parts, b, mode, name, tiles=(512, 1024, 1024), extras=(), outs=None, epilogue=None, out_dtype=BF16,
        b_kmap=None, b_sub=None):
    na = len(a_parts)
    stack = [a.shape[0] if a.ndim == 3 else 1 for a in a_parts]
    first = [sum(stack[:s]) for s in range(na)]
    if mode == "tn":
        k_part, m = a_parts[0].shape
        n = b.shape[-1]
    else:
        m, k_part = a_parts[0].shape[-2:]
        n = b.shape[1] if mode == "nn" else b.shape[0]
    tm, tn, tk = min(tiles[0], m), min(tiles[1], n), min(tiles[2], k_part)
    kb = k_part // tk
    nk = sum(stack) * kb
    grid = (m // tm, n // tn, nk)
    b_kmap = b_kmap or (lambda k: k)

    in_specs = []
    for s in range(na):
        if mode == "tn":
            in_specs.append(pl.BlockSpec((tk, tm), lambda i, j, k: (k, i)))
            continue

        def rel(k, s=s):
            return jnp.clip(k - first[s] * kb, 0, stack[s] * kb - 1)

        if a_parts[s].ndim == 3:
            in_specs.append(pl.BlockSpec((None, tm, tk), lambda i, j, k, rel=rel: (rel(k) // kb, i, rel(k) % kb)))
        else:
            in_specs.append(pl.BlockSpec((tm, tk), lambda i, j, k, rel=rel: (i, rel(k))))
    if mode == "nt":
        in_specs.append(pl.BlockSpec((tn, tk), lambda i, j, k: (j, b_kmap(k))))
    elif b_sub is not None:
        in_specs.append(pl.BlockSpec((None, tk, tn), lambda i, j, k: (b_sub, k, j)))
    else:
        in_specs.append(pl.BlockSpec((tk, tn), lambda i, j, k: (b_kmap(k), j)))
    for _, blk, imap in extras:
        in_specs.append(pl.BlockSpec(blk, imap))
    if outs is None:
        outs = [(SDS((m, n), out_dtype), (tm, tn), lambda i, j, k: (i, j))]
    out_specs = [pl.BlockSpec(blk, imap) for _, blk, imap in outs]
    ne, no = len(extras), len(outs)
    dims = {"nn": NN, "nt": NT, "tn": TN}[mode]

    def finish(res, e_refs, o_refs, i):
        if epilogue is None:
            o_refs[0][...] = res.astype(o_refs[0].dtype)
        else:
            epilogue(res, e_refs, o_refs, i)

    def body(*refs):
        a_refs = refs[:na]
        b_ref = refs[na]
        e_refs = refs[na + 1:na + 1 + ne]
        o_refs = refs[na + 1 + ne:na + 1 + ne + no]
        i, k = pl.program_id(0), pl.program_id(2)
        if nk == 1:
            finish(_dot(a_refs[0][...].astype(BF16), b_ref[...].astype(BF16), dims), e_refs, o_refs, i)
            return
        acc = refs[-1]

        @pl.when(k == 0)
        def _():
            acc[...] = jnp.zeros_like(acc)

        def step(a_ref):
            acc[...] += _dot(a_ref[...].astype(BF16), b_ref[...].astype(BF16), dims)

        for s in range(na):
            if na == 1:
                step(a_refs[0])
            else:
                in_use = (k >= first[s] * kb) & (k < (first[s] + stack[s]) * kb)
                pl.when(in_use)(functools.partial(step, a_refs[s]))

        @pl.when(k == nk - 1)
        def _():
            finish(acc[...], e_refs, o_refs, i)

    res = _pcall(
        body, name=name, grid=grid, in_specs=in_specs, out_specs=out_specs,
        out_shape=[o[0] for o in outs], scratch_shapes=[] if nk == 1 else [pltpu.VMEM((tm, tn), F32)],
        compiler_params=_params(dimension_semantics=("arbitrary", "arbitrary", "arbitrary")),
    )(*a_parts, b, *[e[0] for e in extras])
    return res[0] if len(res) == 1 else res


IN_PROJ_TILES = (1024, 1024, 1024)
DW_TILES = (1024, 1024, 512)


def _dw(x, dy, name, sub=None):
    return _mm([x], dy, "tn", name, tiles=DW_TILES, b_sub=sub)


def _add_extra_epilogue(acc, e_refs, o_refs, i):
    o_refs[0][...] = acc + e_refs[0][...]


def _mm_residual(a, b, mode, res, name):
    m = a.shape[0]
    n = b.shape[1] if mode == "nn" else b.shape[0]
    tm, tn = 512, 1024
    return _mm([a], b, mode, name, tiles=(tm, tn, 1024),
               extras=[(res, (tm, tn), lambda i, j, k: (i, j))],
               outs=[(SDS((m, n), F32), (tm, tn), lambda i, j, k: (i, j))],
               epilogue=_add_extra_epilogue)


def _rms_fwd(h, g, name):
    t, d = h.shape
    tm = 512

    def body(h_ref, g_ref, n_ref, r_ref):
        x = h_ref[...]
        r = lax.rsqrt(jnp.mean(x * x, axis=-1, keepdims=True) + RMS_EPS)
        n_ref[...] = ((x * r) * g_ref[...]).astype(BF16)
        r_ref[...] = r

    return _pcall(
        body, name=name, grid=(t // tm,),
        in_specs=[pl.BlockSpec((tm, d), lambda i: (i, 0)), pl.BlockSpec((1, d), lambda i: (0, 0))],
        out_specs=[pl.BlockSpec((tm, d), lambda i: (i, 0)), pl.BlockSpec((tm, 1), lambda i: (i, 0))],
        out_shape=[SDS((t, d), BF16), SDS((t, 1), F32)],
        compiler_params=_params(),
    )(h, g)


def _rms_bwd_rows(dn, x, g, r):
    xhat = x * r
    dxhat = dn * g
    dx = r * (dxhat - xhat * jnp.mean(dxhat * xhat, axis=-1, keepdims=True))
    dg = jnp.sum(dn * xhat, axis=0, keepdims=True)
    return dx, dg


def _mm_in_bwd(d_parts, w, h, g, r, dres, name, more=None, w_kmap=None):
    t = h.shape[0]
    tm = 512
    tk = D_MODEL
    row = lambda i, j, k: (i, 0)
    extras = [(h, (tm, D_MODEL), row), (g, (1, D_MODEL), lambda i, j, k: (0, 0)), (r, (tm, 1), row),
              (dres, (tm, D_MODEL), row)]
    if more is not None:
        extras.append((more, (tm, D_MODEL), row))

    def epilogue(acc, e_refs, o_refs, i):
        dn = acc if more is None else acc + e_refs[4][...]
        dx, dg = _rms_bwd_rows(dn, e_refs[0][...], e_refs[1][...], e_refs[2][...])
        o_refs[0][...] = e_refs[3][...] + dx

        @pl.when(i == 0)
        def _():
            o_refs[1][...] = dg

        @pl.when(i > 0)
        def _():
            o_refs[1][...] += dg

    return _mm(d_parts, w, "nt", name, tiles=(tm, D_MODEL, tk), extras=extras,
               outs=[(SDS((t, D_MODEL), F32), (tm, D_MODEL), row),
                     (SDS((1, D_MODEL), F32), (1, D_MODEL), lambda i, j, k: (0, 0))],
               epilogue=epilogue, b_kmap=w_kmap)


def _final_bwd(h, g, tgt):
    t, d = h.shape
    tm = 256

    def body(h_ref, g_ref, t_ref, dh_ref, dg_ref, loss_ref):
        i = pl.program_id(0)
        x = h_ref[...]
        gg = g_ref[...]
        r = lax.rsqrt(jnp.mean(x * x, axis=-1, keepdims=True) + RMS_EPS)
        err = (x * r) * gg - t_ref[...]
        part = 0.5 * jnp.sum(jnp.mean(err * err, axis=-1, keepdims=True), axis=0, keepdims=True)
        dx, dg = _rms_bwd_rows(err * (1.0 / d), x, gg, r)
        dh_ref[...] = dx

        @pl.when(i == 0)
        def _():
            dg_ref[...] = dg
            loss_ref[...] = jnp.broadcast_to(part, loss_ref.shape)

        @pl.when(i > 0)
        def _():
            dg_ref[...] += dg
            loss_ref[...] += jnp.broadcast_to(part, loss_ref.shape)

    return _pcall(
        body, name="final_norm_loss", grid=(t // tm,),
        in_specs=[pl.BlockSpec((tm, d), lambda i: (i, 0)), pl.BlockSpec((1, d), lambda i: (0, 0)),
                  pl.BlockSpec((tm, d), lambda i: (i, 0))],
        out_specs=[pl.BlockSpec((tm, d), lambda i: (i, 0)), pl.BlockSpec((1, d), lambda i: (0, 0)),
                   pl.BlockSpec((1, LANES), lambda i: (0, 0))],
        out_shape=[SDS((t, d), F32), SDS((1, d), F32), SDS((1, LANES), F32)],
        compiler_params=_params(),
    )(h, g, tgt)


GATE_ROWS = 256


def _split3(x):
    hi = x.astype(BF16)
    r1 = x - hi.astype(F32)
    mid = r1.astype(BF16)
    lo = (r1 - mid.astype(F32)).astype(BF16)
    return hi, mid, lo


def _tri_sum(x, upper):
    rows = x.shape[0]
    ri = lax.broadcasted_iota(jnp.int32, (rows, rows), 0)
    ci = lax.broadcasted_iota(jnp.int32, (rows, rows), 1)
    tri = jnp.where((ri <= ci) if upper else (ri >= ci), 1.0, 0.0).astype(BF16)
    hi, mid, lo = _split3(x)
    return _dot(tri, hi, NN) + _dot(tri, mid, NN) + _dot(tri, lo, NN)


def _log_sigmoid(x):
    return jnp.minimum(x, 0.0) - jnp.log1p(jnp.exp(-jnp.abs(x)))


def _fox_gate_fwd(projf, bpad):
    t = projf.shape[0]
    tb = GATE_ROWS

    def body(x_ref, b_ref, c_ref, carry):
        i = pl.program_id(0)

        @pl.when(i == 0)
        def _():
            carry[...] = jnp.zeros_like(carry)

        c_ref[...] = _tri_sum(_log_sigmoid(x_ref[...] + b_ref[...]), upper=False) + carry[...]
        carry[...] = c_ref[pl.ds(tb - 1, 1), :]

    return _pcall(
        body, name="fox_gate_fwd", grid=(t // tb,),
        in_specs=[pl.BlockSpec((tb, LANES), lambda i: (i, 0)), pl.BlockSpec((1, LANES), lambda i: (0, 0))],
        out_specs=pl.BlockSpec((tb, LANES), lambda i: (i, 0)),
        out_shape=SDS((t, LANES), F32), scratch_shapes=[pltpu.VMEM((1, LANES), F32)],
        compiler_params=_params(),
    )(projf, bpad)


def _fox_gate_bwd(projf, bpad, dc_query, dc_key):
    t = projf.shape[0]
    tb = GATE_ROWS
    nb = t // tb

    def body(x_ref, b_ref, dcq_ref, dck_ref, df_ref, db_ref, carry, buf):
        i = pl.program_id(0)

        @pl.when(i == 0)
        def _():
            carry[...] = jnp.zeros_like(carry)

        buf[...] = _tri_sum(dcq_ref[...] - dck_ref[...], upper=True) + carry[...]
        carry[...] = buf[pl.ds(0, 1), :]
        df = buf[...] * _sigmoid(-(x_ref[...] + b_ref[...]))
        df_ref[...] = df.astype(BF16)
        part = jnp.sum(df, axis=0, keepdims=True)

        @pl.when(i == 0)
        def _():
            db_ref[...] = part

        @pl.when(i > 0)
        def _():
            db_ref[...] += part

    rev = lambda i: (nb - 1 - i, 0)
    return _pcall(
        body, name="fox_gate_bwd", grid=(nb,),
        in_specs=[pl.BlockSpec((tb, LANES), rev), pl.BlockSpec((1, LANES), lambda i: (0, 0)),
                  pl.BlockSpec((tb, LANES), rev), pl.BlockSpec((tb, LANES), rev)],
        out_specs=[pl.BlockSpec((tb, LANES), rev), pl.BlockSpec((1, LANES), lambda i: (0, 0))],
        out_shape=[SDS((t, LANES), BF16), SDS((1, LANES), F32)],
        scratch_shapes=[pltpu.VMEM((1, LANES), F32), pltpu.VMEM((tb, LANES), F32)],
        compiler_params=_params(),
    )(projf, bpad, dc_query, dc_key)


FOX_TQ = 1024
FOX_TQ_FWD = 1024
FOX_SCALE = FOX_HEAD_DIM ** -0.5


def _low_lanes(shape):
    return lax.broadcasted_iota(jnp.int32, shape, len(shape) - 1) < FOX_HEAD_DIM


FOX_AUG = 3
FOX_CHAIN = 256
FOX_SUM_ROWS = 8


def _top_rows(shape):
    return lax.broadcasted_iota(jnp.int32, shape, 0) < FOX_HEAD_DIM


def _fox_aug(a, b, sign_a, sign_b, piece_entry, ones_entry, name):
    t = a.shape[0]
    tb = 512

    def body(a_ref, b_ref, o_ref):
        x = sign_a * a_ref[...]
        if sign_b != 0.0:
            x = x + sign_b * b_ref[...]
        head = lax.broadcasted_iota(jnp.int32, (LANES, D_MODEL), 0)
        col = lax.broadcasted_iota(jnp.int32, (LANES, D_MODEL), 1)
        base = (head // 2) * LANES + (1 - head % 2) * FOX_HEAD_DIM + piece_entry
        acc = jnp.zeros((tb, D_MODEL), F32)
        for e, piece in enumerate(_split3(x)):
            place = jnp.where((head < FOX_HEADS) & (col == base + e), 1.0, 0.0).astype(BF16)
            acc = acc + _dot(piece, place, NN)
        if ones_entry is not None:
            ent = lax.broadcasted_iota(jnp.int32, (1, D_MODEL), 1) % FOX_HEAD_DIM
            acc = acc + jnp.where((ent >= ones_entry) & (ent < ones_entry + FOX_AUG), 1.0, 0.0)
        o_ref[...] = acc.astype(BF16)

    blk = pl.BlockSpec((tb, LANES), lambda i: (i, 0))
    return _pcall(
        body, name=name, grid=(t // tb,), in_specs=[blk, blk],
        out_specs=pl.BlockSpec((tb, D_MODEL), lambda i: (i, 0)), out_shape=SDS((t, D_MODEL), BF16),
        compiler_params=_params(),
    )(a, b)


def _fox_unpack_dc(dck_wide, dcq):
    t = dck_wide.shape[0]
    dck = dck_wide.reshape(t, FOX_PAIRS, 2, FOX_HEAD_DIM)[:, :, ::-1, 0].reshape(t, FOX_HEADS)
    return dcq[:, :, 0, :].reshape(FOX_HEADS, t).T, dck


def _causal_steps(nq, key_major):
    if key_major:
        pairs = [(i, j) for j in range(nq) for i in range(j, nq)]
    else:
        pairs = [(i, j) for i in range(nq) for j in range(i + 1)]
    return (jnp.asarray([p[0] for p in pairs], jnp.int32), jnp.asarray([p[1] for p in pairs], jnp.int32))


def _pair_operand(low, own, other, hh):
    return jnp.where(low, own, other) if hh == 0 else jnp.where(low, other, own)


def _fox_fwd(proj, qaug, kaug):
    t = proj.shape[0]
    tq = tk = min(FOX_TQ_FWD, t)
    nq = t // tq
    cb = D_MODEL // LANES
    half = min(FOX_CHAIN, tq)

    i_tab, j_tab = _causal_steps(nq, key_major=False)

    def body(i_ref, j_ref, q_ref, k_ref, v_ref, z_ref, qa_ref, ka_ref, o_ref, g_ref, lse_ref, m_s, l_s, acc_s):
        step = pl.program_id(1)
        i, j = i_ref[step], j_ref[step]

        @pl.when(j == 0)
        def _():
            m_s[...] = jnp.full_like(m_s, NEG_INF)
            l_s[...] = jnp.zeros_like(l_s)
            acc_s[...] = jnp.zeros_like(acc_s)

        low = _low_lanes((tq, LANES))
        top = _top_rows((LANES, tq))

        def update(masked):
            qs = q_ref[...] * FOX_SCALE
            qa, k, ka, v = qa_ref[...], k_ref[...], ka_ref[...], v_ref[...]
            if masked:
                causal = (lax.broadcasted_iota(jnp.int32, (tk, tq), 0) <= lax.broadcasted_iota(jnp.int32, (tk, tq), 1))
            one = jnp.ones_like(v)
            chains = [(hh, slice(c * half, (c + 1) * half)) for hh in range(2) for c in range(tq // half)]
            qh = [_pair_operand(low, qs, qa, hh) for hh in range(2)]
            kh = [_pair_operand(low, k, ka, hh) for hh in range(2)]
            vh = [_pair_operand(low, v, one, hh) for hh in range(2)]
            scores = [_dot(kh[hh], qh[hh][cols, :], NT) for hh, cols in chains]
            for (hh, cols), s in zip(chains, scores):
                if masked:
                    s = jnp.where(causal[:, cols], s, NEG_INF)
                m_prev = m_s[hh, :, cols]
                m_new = jnp.maximum(m_prev, jnp.max(s, axis=0, keepdims=True))
                alpha = jnp.exp(m_prev - m_new)
                pv = _dot(vh[hh], jnp.exp(s - m_new).astype(BF16), TN)
                sums = pv[FOX_HEAD_DIM:FOX_HEAD_DIM + FOX_SUM_ROWS, :] if hh == 0 else pv[0:FOX_SUM_ROWS, :]
                l_s[hh, :, cols] = alpha * l_s[hh, :, cols] + sums
                m_s[hh, :, cols] = m_new
                own = top[:, cols] if hh == 0 else jnp.logical_not(top[:, cols])
                acc_s[:, cols] = jnp.where(own, acc_s[:, cols] * alpha + pv, acc_s[:, cols])

        pl.when(j < i)(functools.partial(update, False))
        pl.when(j == i)(functools.partial(update, True))

        @pl.when(j == i)
        def _():
            o = (acc_s[...] / jnp.where(top, l_s[0, 0:1, :], l_s[1, 0:1, :])).T
            z = z_ref[...].astype(F32)
            o_ref[...] = o.astype(BF16)
            g_ref[...] = (o * (z * _sigmoid(z))).astype(BF16)
            for hh in range(2):
                lse_ref[hh] = m_s[hh] + jnp.log(l_s[hh, 0:1, :])

    qblk = lambda col: pl.BlockSpec((tq, LANES), lambda h, s, it, jt: (it[s], col + h))
    kblk = lambda col: pl.BlockSpec((tk, LANES), lambda h, s, it, jt: (jt[s], col + h))
    return _pcall(
        body, name="fox_attn_fwd",
        grid_spec=pltpu.PrefetchScalarGridSpec(
            num_scalar_prefetch=2, grid=(FOX_PAIRS, i_tab.shape[0]),
            in_specs=[qblk(0), kblk(cb), kblk(2 * cb), qblk(3 * cb), qblk(0), kblk(0)],
            out_specs=[qblk(0), qblk(0), pl.BlockSpec((2, 1, tq), lambda h, s, it, jt: (h, 0, it[s]))],
            scratch_shapes=[pltpu.VMEM((2, 1, tq), F32), pltpu.VMEM((2, FOX_SUM_ROWS, tq), F32),
                            pltpu.VMEM((LANES, tq), F32)]),
        out_shape=[SDS((t, D_MODEL), BF16), SDS((t, D_MODEL), BF16), SDS((FOX_HEADS, 1, t), F32)],
        compiler_params=_params(dimension_semantics=("arbitrary", "arbitrary")),
    )(i_tab, j_tab, proj, proj, proj, proj, qaug, kaug)


def _fox_bwd(proj, do, qaug, kaug, doaug):
    t = proj.shape[0]
    tq = tk = min(FOX_TQ, t)
    nq = t // tq
    cb = D_MODEL // LANES

    i_tab, j_tab = _causal_steps(nq, key_major=True)

    def body(i_ref, j_ref, q_ref, k_ref, v_ref, do_ref, qa_ref, ka_ref, da_ref,
             dq_ref, dk_ref, dv_ref, dck_ref, dcq_ref, dq_acc, dcq_acc, dk_acc, dks_acc, dv_acc):
        step = pl.program_id(1)
        i, j = i_ref[step], j_ref[step]
        low = _low_lanes((tq, LANES))
        top = _top_rows((LANES, tq))

        @pl.when(i == j)
        def _():
            dk_acc[...] = jnp.zeros_like(dk_acc)
            dks_acc[...] = jnp.zeros_like(dks_acc)
            dv_acc[...] = jnp.zeros_like(dv_acc)

        def update(masked):
            qs = q_ref[...] * FOX_SCALE
            k, v, dout = k_ref[...], v_ref[...], do_ref[...]
            qa, ka, da = qa_ref[...], ka_ref[...], da_ref[...]
            lane = lax.broadcasted_iota(jnp.int32, (tk, LANES), 1)
            vone = jnp.where((lane & (FOX_HEAD_DIM - 1)) < FOX_AUG, 1.0, 0.0).astype(v.dtype)
            one = jnp.ones_like(k)
            if masked:
                causal = (lax.broadcasted_iota(jnp.int32, (tk, tq), 0) <= lax.broadcasted_iota(jnp.int32, (tk, tq), 1))
            parts = []
            scores = [_dot(_pair_operand(low, k, ka, hh), _pair_operand(low, qs, qa, hh), NT) for hh in range(2)]
            dps = [_dot(_pair_operand(low, v, vone, hh), _pair_operand(low, dout, da, hh), NT) for hh in range(2)]
            for hh in range(2):
                s = scores[hh]
                if masked:
                    s = jnp.where(causal, s, NEG_INF)
                p = jnp.exp(s)
                ds = p * dps[hh]
                pb = p.astype(BF16)
                dsb = ds.astype(BF16)
                parts.append((_dot(pb, dout, NN),
                              _dot(dsb, _pair_operand(low, qs, one, hh), NN),
                              _dot(_pair_operand(low, k, one, hh), dsb, TN)))
            dv_acc[...] += jnp.where(low, parts[0][0], parts[1][0])
            dk_acc[...] += jnp.where(low, parts[0][1], parts[1][1])
            dks_acc[...] += jnp.where(low, parts[1][1], parts[0][1])
            dq_t = jnp.where(top, parts[0][2], parts[1][2]) * FOX_SCALE
            sum_a = parts[0][2][FOX_HEAD_DIM:FOX_HEAD_DIM + FOX_SUM_ROWS, :]
            sum_b = parts[1][2][0:FOX_SUM_ROWS, :]

            @pl.when(j == 0)
            def _():
                dq_acc[i] = dq_t
                dcq_acc[0, i] = sum_a
                dcq_acc[1, i] = sum_b

            @pl.when(j > 0)
            def _():
                dq_acc[i] += dq_t
                dcq_acc[0, i] += sum_a
                dcq_acc[1, i] += sum_b

        pl.when(i > j)(functools.partial(update, False))
        pl.when(i == j)(functools.partial(update, True))

        @pl.when(i == nq - 1)
        def _():
            dk_ref[...] = dk_acc[...].astype(BF16)
            dv_ref[...] = dv_acc[...].astype(BF16)
            dck_ref[...] = dks_acc[...]

        @pl.when((i == nq - 1) & (j == nq - 1))
        def _():
            for blk in range(nq):
                dq_ref[blk * tq:(blk + 1) * tq, :] = dq_acc[blk].T.astype(BF16)
            dcq_ref[...] = dcq_acc[...]

    qblk = lambda col: pl.BlockSpec((tq, LANES), lambda h, s, it, jt: (it[s], col + h))
    kblk = lambda col: pl.BlockSpec((tk, LANES), lambda h, s, it, jt: (jt[s], col + h))
    return _pcall(
        body, name="fox_attn_bwd",
        grid_spec=pltpu.PrefetchScalarGridSpec(
            num_scalar_prefetch=2, grid=(FOX_PAIRS, i_tab.shape[0]),
            in_specs=[qblk(0), kblk(cb), kblk(2 * cb), qblk(0), qblk(0), kblk(0), qblk(0)],
            out_specs=[pl.BlockSpec((t, LANES), lambda h, s, it, jt: (0, h)), kblk(0), kblk(0), kblk(0),
                       pl.BlockSpec((2, nq, FOX_SUM_ROWS, tq), lambda h, s, it, jt: (h, 0, 0, 0))],
            scratch_shapes=[pltpu.VMEM((nq, LANES, tq), F32), pltpu.VMEM((2, nq, FOX_SUM_ROWS, tq), F32),
                            pltpu.VMEM((tk, LANES), F32), pltpu.VMEM((tk, LANES), F32), pltpu.VMEM((tk, LANES), F32)]),
        out_shape=[SDS((t, D_MODEL), BF16), SDS((t, D_MODEL), BF16), SDS((t, D_MODEL), BF16),
                   SDS((t, D_MODEL), F32), SDS((FOX_HEADS, nq, FOX_SUM_ROWS, tq), F32)],
        compiler_params=_params(dimension_semantics=("arbitrary", "arbitrary")),
    )(i_tab, j_tab, proj, proj, proj, do, qaug, kaug, doaug)


def _mm_gate_bwd(dh, w_out, z_src, z_col0, o, heads, name):
    t = dh.shape[0]
    tm = 512
    row = lambda i, j, k: (i, 0)
    zcb = z_col0 // D_MODEL

    def epilogue(acc, e_refs, o_refs, i):
        z = e_refs[0][...].astype(F32)
        ov = e_refs[1][...].astype(F32)
        sg = _sigmoid(z)
        dout = acc * (z * sg)
        o_refs[0][...] = dout.astype(BF16)
        o_refs[1][...] = (acc * ov * (sg * (1.0 + z * (1.0 - sg)))).astype(BF16)
        prod = dout * ov
        for cbk in range(D_MODEL // LANES):
            seg = prod[:, cbk * LANES:(cbk + 1) * LANES]
            tot = jnp.sum(seg, axis=-1, keepdims=True)
            if heads == D_MODEL // LANES:
                o_refs[2][cbk] = tot
            else:
                lo = jnp.sum(jnp.where(_low_lanes(seg.shape), seg, 0.0), axis=-1, keepdims=True)
                o_refs[2][2 * cbk] = lo
                o_refs[2][2 * cbk + 1] = tot - lo

    return _mm([dh], w_out, "nt", name, tiles=(tm, D_MODEL, D_MODEL),
               extras=[(z_src, (tm, D_MODEL), lambda i, j, k: (i, zcb)), (o, (tm, D_MODEL), row)],
               outs=[(SDS((t, D_MODEL), BF16), (tm, D_MODEL), row), (SDS((t, D_MODEL), BF16), (tm, D_MODEL), row),
                     (SDS((heads, t, 1), F32), (heads, tm, 1), lambda i, j, k: (0, i, 0))],
               epilogue=epilogue)


def _ple_fwd(h, pin, w_up, w_gate, name):
    t = h.shape[0]
    tm = 512
    pd = pin.shape[1]

    def body(h_ref, p_ref, wu_ref, wg_ref, hn_ref, u_ref, a_ref):
        h = h_ref[...]
        u = _dot(p_ref[...].astype(BF16), wu_ref[...], NN)
        a = _dot(h.astype(BF16), wg_ref[...], NN)
        hn_ref[...] = h + u * _sigmoid(a)
        u_ref[...] = u.astype(BF16)
        a_ref[...] = a.astype(BF16)

    rows = pl.BlockSpec((tm, D_MODEL), lambda i: (i, 0))
    return _pcall(
        body, name=name, grid=(t // tm,),
        in_specs=[rows, pl.BlockSpec((tm, pd), lambda i: (i, 0)),
                  pl.BlockSpec((pd, D_MODEL), lambda i: (0, 0)), pl.BlockSpec((D_MODEL, D_MODEL), lambda i: (0, 0))],
        out_specs=[rows, rows, rows],
        out_shape=[SDS((t, D_MODEL), F32), SDS((t, D_MODEL), BF16), SDS((t, D_MODEL), BF16)],
        compiler_params=_params(),
    )(h, pin, w_up, w_gate)


def _ple_bwd_elem(dh, u, a, name):
    t = dh.shape[0]
    tm = 512

    def body(dh_ref, u_ref, a_ref, du_ref, da_ref):
        g = dh_ref[...]
        s = _sigmoid(a_ref[...].astype(F32))
        du_ref[...] = (g * s).astype(BF16)
        da_ref[...] = (g * u_ref[...].astype(F32) * (s * (1.0 - s))).astype(BF16)

    blk = pl.BlockSpec((tm, D_MODEL), lambda i: (i, 0))
    return _pcall(
        body, name=name, grid=(t // tm,), in_specs=[blk, blk, blk], out_specs=[blk, blk],
        out_shape=[SDS((t, D_MODEL), BF16), SDS((t, D_MODEL), BF16)], compiler_params=_params(),
    )(dh, u, a)


DIL_SCALE = LANES ** -0.5


def _dil_masks():
    ii = lax.broadcasted_iota(jnp.int32, (DIL_BLOCK, DIL_BLOCK), 0)
    jj = lax.broadcasted_iota(jnp.int32, (DIL_BLOCK, DIL_BLOCK), 1)
    return ii, jj


DIL_UNITS = 16
BNT = (((2,), (2,)), ((0,), (0,)))
BNN = (((2,), (1,)), ((0,), (0,)))
BTN = (((1,), (1,)), ((0,), (0,)))


def _dil_units(dil):
    return [(b, r) for b in range(DIL_UNITS // dil) for r in range(dil)]


def _unit_rows(b, r, dil):
    return pl.ds(b * DIL_BLOCK * dil + r, DIL_BLOCK, stride=dil)


def _gather_units(cur, dil, shift=0, edge=None, lead=()):
    nbk = DIL_UNITS // dil
    parts = []
    for b, r in _dil_units(dil):
        bb = b + shift
        if 0 <= bb < nbk:
            parts.append(cur[lead + (_unit_rows(bb, r, dil), slice(None))])
        else:
            parts.append(edge[lead + (pl.ds(r, DIL_BLOCK, stride=dil), slice(None))])
    return jnp.stack(parts)


def _scatter_units(dst, val, dil):
    for u, (b, r) in enumerate(_dil_units(dil)):
        dst[_unit_rows(b, r, dil), :] = val[u]


def _dil_bias(slope, prev):
    ii, jj = _dil_masks()
    dist = (DIL_BLOCK + ii - jj) if prev else (ii - jj)
    return (slope * dist.astype(F32))[None], ((jj >= ii) if prev else (jj <= ii))[None]


def _dil_fwd(proj, slopes, grp, dil, name):
    t = proj.shape[0]
    rows = DIL_BLOCK * DIL_UNITS
    edge_rows = DIL_BLOCK * dil
    nbk = DIL_UNITS // dil
    nsb = t // rows
    qc, kc_, vc_ = grp * DIL_HEADS, 3 * DIL_HEADS + grp * DIL_HEADS, 6 * DIL_HEADS + grp * DIL_HEADS

    def body(q_ref, kp_ref, kc_ref, vp_ref, vc_ref, sl_ref, o_ref, lse_ref, qf, kpf, kcf, vpf, vcf, of, lf):
        m = pl.program_id(1)
        for src, dst in ((q_ref, qf), (kp_ref, kpf), (kc_ref, kcf), (vp_ref, vpf), (vc_ref, vcf)):
            dst[...] = src[...].astype(F32)
        slope = sl_ref[0]
        unit = lax.broadcasted_iota(jnp.int32, (DIL_UNITS, 1, 1), 0)
        has_prev = (unit >= dil) | (m > 0)
        q = _gather_units(qf, dil).astype(BF16)
        kc, vc = _gather_units(kcf, dil).astype(BF16), _gather_units(vcf, dil).astype(BF16)
        kp, vp = _gather_units(kcf, dil, -1, kpf).astype(BF16), _gather_units(vcf, dil, -1, vpf).astype(BF16)
        bias_p, ok_p = _dil_bias(slope, True)
        bias_c, ok_c = _dil_bias(slope, False)
        sp = jnp.where(ok_p & has_prev, _dot(q, kp, BNT) * DIL_SCALE - bias_p, NEG_INF)
        sc = jnp.where(ok_c, _dot(q, kc, BNT) * DIL_SCALE - bias_c, NEG_INF)
        mx = jnp.maximum(jnp.max(sp, axis=-1, keepdims=True), jnp.max(sc, axis=-1, keepdims=True))
        pp = jnp.exp(sp - mx)
        pc = jnp.exp(sc - mx)
        l = jnp.sum(pp, axis=-1, keepdims=True) + jnp.sum(pc, axis=-1, keepdims=True)
        o = (_dot(pp.astype(BF16), vp, BNN) + _dot(pc.astype(BF16), vc, BNN)) / l
        _scatter_units(of, o, dil)
        _scatter_units(lf, mx + jnp.log(l), dil)
        o_ref[...] = of[...].astype(BF16)
        lse_ref[0] = lf[...]

    cur = lambda col: pl.BlockSpec((rows, LANES), lambda h, m: (m, col + h))
    prev = lambda col: pl.BlockSpec((edge_rows, LANES), lambda h, m: (jnp.maximum(m * nbk - 1, 0), col + h))
    return _pcall(
        body, name=name, grid=(DIL_HEADS, nsb),
        in_specs=[cur(qc), prev(kc_), cur(kc_), prev(vc_), cur(vc_), pl.BlockSpec((1, 1, 1), lambda h, m: (h, 0, 0))],
        out_specs=[pl.BlockSpec((rows, LANES), lambda h, m: (m, h)), pl.BlockSpec((1, rows, 1), lambda h, m: (h, m, 0))],
        out_shape=[SDS((t, D_MODEL), BF16), SDS((DIL_HEADS, t, 1), F32)],
        scratch_shapes=[pltpu.VMEM((rows, LANES), F32), pltpu.VMEM((edge_rows, LANES), F32), pltpu.VMEM((rows, LANES), F32),
                        pltpu.VMEM((edge_rows, LANES), F32), pltpu.VMEM((rows, LANES), F32), pltpu.VMEM((rows, LANES), F32),
                        pltpu.VMEM((rows, 1), F32)],
        compiler_params=_params(),
    )(proj, proj, proj, proj, proj, slopes)


def _dil_mix(outs, lses, proj, z_col0):
    t = proj.shape[0]
    tm = 512
    zcb = z_col0 // LANES
    ng = len(outs)

    def body(*refs):
        o_refs, l_refs, z_ref = refs[:ng], refs[ng:2 * ng], refs[2 * ng]
        om_ref, g_ref, lse_ref = refs[2 * ng + 1:]
        ls = [r[0] for r in l_refs]
        mx = functools.reduce(jnp.maximum, ls)
        es = [jnp.exp(l - mx) for l in ls]
        tot = functools.reduce(jnp.add, es)
        o = functools.reduce(jnp.add, [(e / tot) * r[...].astype(F32) for e, r in zip(es, o_refs)])
        z = z_ref[...].astype(F32)
        om_ref[...] = o.astype(BF16)
        g_ref[...] = (o * (z * _sigmoid(z))).astype(BF16)
        lse_ref[0] = mx + jnp.log(tot)

    tile = pl.BlockSpec((tm, LANES), lambda i, h: (i, h))
    col = pl.BlockSpec((1, tm, 1), lambda i, h: (h, i, 0))
    return _pcall(
        body, name="dil_mix", grid=(t // tm, DIL_HEADS),
        in_specs=[tile] * ng + [col] * ng + [pl.BlockSpec((tm, LANES), lambda i, h: (i, zcb + h))],
        out_specs=[tile, tile, col],
        out_shape=[SDS((t, D_MODEL), BF16), SDS((t, D_MODEL), BF16), SDS((DIL_HEADS, t, 1), F32)],
        compiler_params=_params(),
    )(*outs, *lses, proj)


def _dil_bwd(proj, do, lse, delta, slopes, grp, dil, name):
    t = proj.shape[0]
    rows = DIL_BLOCK * DIL_UNITS
    edge_rows = DIL_BLOCK * dil
    nbk = DIL_UNITS // dil
    nsb = t // rows
    last_edge = t // edge_rows - 1
    qc, kc_, vc_ = grp * DIL_HEADS, 3 * DIL_HEADS + grp * DIL_HEADS, 6 * DIL_HEADS + grp * DIL_HEADS

    def body(q_ref, qn_ref, kp_ref, kc_ref, vp_ref, vc_ref, do_ref, don_ref, l_ref, ln_ref, d_ref, dn_ref, sl_ref,
             dqkv_ref, qf, qnf, kpf, kcf, vpf, vcf, dof, donf, dqf, dkf, dvf):
        m = pl.program_id(1)
        for src, dst in ((q_ref, qf), (qn_ref, qnf), (kp_ref, kpf), (kc_ref, kcf), (vp_ref, vpf), (vc_ref, vcf),
                         (do_ref, dof), (don_ref, donf)):
            dst[...] = src[...].astype(F32)
        slope = sl_ref[0]
        unit = lax.broadcasted_iota(jnp.int32, (DIL_UNITS, 1, 1), 0)
        has_prev = (unit >= dil) | (m > 0)
        has_next = (unit < DIL_UNITS - dil) | (m < nsb - 1)
        b16 = lambda x: x.astype(BF16)
        q, kc, vc, dout = (b16(_gather_units(x, dil)) for x in (qf, kcf, vcf, dof))
        kp, vp = b16(_gather_units(kcf, dil, -1, kpf)), b16(_gather_units(vcf, dil, -1, vpf))
        qn, don = b16(_gather_units(qf, dil, 1, qnf)), b16(_gather_units(dof, dil, 1, donf))
        lrow, drow = _gather_units(l_ref, dil, lead=(0,)), _gather_units(d_ref, dil, lead=(0,))
        lnrow = _gather_units(l_ref, dil, 1, ln_ref, lead=(0,))
        dnrow = _gather_units(d_ref, dil, 1, dn_ref, lead=(0,))
        bias_p, ok_p = _dil_bias(slope, True)
        bias_c, ok_c = _dil_bias(slope, False)
        sp = jnp.where(ok_p & has_prev, _dot(q, kp, BNT) * DIL_SCALE - bias_p, NEG_INF)
        sc = jnp.where(ok_c, _dot(q, kc, BNT) * DIL_SCALE - bias_c, NEG_INF)
        pp = jnp.exp(sp - lrow)
        pc = jnp.exp(sc - lrow)
        dsp = b16(pp * (_dot(dout, vp, BNT) - drow))
        dsc = b16(pc * (_dot(dout, vc, BNT) - drow))
        _scatter_units(dqf, (_dot(dsp, kp, BNN) + _dot(dsc, kc, BNN)) * DIL_SCALE, dil)
        sn = jnp.where(ok_p & has_next, _dot(qn, kc, BNT) * DIL_SCALE - bias_p, NEG_INF)
        pn = jnp.exp(sn - lnrow)
        dsn = b16(pn * (_dot(don, vc, BNT) - dnrow))
        _scatter_units(dkf, (_dot(dsc, q, BTN) + _dot(dsn, qn, BTN)) * DIL_SCALE, dil)
        _scatter_units(dvf, _dot(b16(pc), dout, BTN) + _dot(b16(pn), don, BTN), dil)
        for s, src in enumerate((dqf, dkf, dvf)):
            dqkv_ref[s] = src[...].astype(BF16)

    prev_i = lambda m: jnp.maximum(m * nbk - 1, 0)
    next_i = lambda m: jnp.minimum((m + 1) * nbk, last_edge)
    cur = lambda col: pl.BlockSpec((rows, LANES), lambda h, m: (m, col + h))
    edge = lambda col, f: pl.BlockSpec((edge_rows, LANES), lambda h, m: (f(m), col + h))
    colcur = pl.BlockSpec((1, rows, 1), lambda h, m: (h, m, 0))
    colnext = pl.BlockSpec((1, edge_rows, 1), lambda h, m: (h, next_i(m), 0))
    out_blk = pl.BlockSpec((3, rows, LANES), lambda h, m: (0, m, h))
    big, small = pltpu.VMEM((rows, LANES), F32), pltpu.VMEM((edge_rows, LANES), F32)
    return _pcall(
        body, name=name, grid=(DIL_HEADS, nsb),
        in_specs=[cur(qc), edge(qc, next_i), edge(kc_, prev_i), cur(kc_), edge(vc_, prev_i), cur(vc_),
                  cur(0), edge(0, next_i), colcur, colnext, colcur, colnext,
                  pl.BlockSpec((1, 1, 1), lambda h, m: (h, 0, 0))],
        out_specs=out_blk, out_shape=SDS((3, t, D_MODEL), BF16),
        scratch_shapes=[big, small, small, big, small, big, big, small, big, big, big],
        compiler_params=_params(),
    )(proj, proj, proj, proj, proj, proj, do, do, lse, lse, delta, delta, slopes)


def _mesh_pos():
    x, y, c = lax.axis_index("x"), lax.axis_index("y"), lax.axis_index("c")
    return x, y, c


def _peer(pos, k):
    x, y, c = pos
    px = 1 - x if k & 4 else x
    py = 1 - y if k & 2 else y
    pc = 1 - c if k & 1 else c
    return (px, py, pc), 4 * px + 2 * py + pc


N_CHIPS = 4
CHIP_FLIPS = ((1, 0), (0, 1), (1, 1))


def _other_chips(x, y):
    return [(1 - x if fx else x, 1 - y if fy else y) for fx, fy in CHIP_FLIPS]


def _all_gather(arrays):
    n = len(arrays)
    per = 2 * N_CHIPS - 1
    hbm = pl.BlockSpec(memory_space=pltpu.HBM)

    def body(*refs):
        ins, outs = refs[:n], refs[n:2 * n]
        send_sems, recv_sems, local_sems = refs[2 * n:]
        x, y, c = _mesh_pos()
        sibling = (x, y, 1 - c)
        chips = _other_chips(x, y)
        block = lambda px, py, pc: 4 * px + 2 * py + pc

        def copy(w, k, src, blk, to):
            return pltpu.make_async_remote_copy(
                src_ref=src, dst_ref=outs[w].at[blk], send_sem=send_sems.at[w * per + k],
                recv_sem=recv_sems.at[w * per + k], device_id=to, device_id_type=MESH)

        local, started = [], []
        for w in range(n):
            cp = pltpu.make_async_copy(ins[w], outs[w].at[block(x, y, c)], local_sems.at[w])
            cp.start()
            local.append(cp)
            started.append(copy(w, 0, ins[w], block(x, y, c), sibling))
            for j, (px, py) in enumerate(chips):
                started.append(copy(w, 1 + j, ins[w], block(x, y, c), (px, py, c)))
        for cp in started:
            cp.start()
        for j, (px, py) in enumerate(chips):
            for w in range(n):
                copy(w, 1 + j, ins[w], block(px, py, c), sibling).wait_recv()
                cp = copy(w, 4 + j, outs[w].at[block(px, py, c)], block(px, py, c), sibling)
                cp.start()
                started.append(cp)
        for w in range(n):
            copy(w, 0, ins[w], block(x, y, 1 - c), sibling).wait_recv()
            for j, (px, py) in enumerate(chips):
                copy(w, 4 + j, ins[w], block(px, py, 1 - c), sibling).wait_recv()
        for cp in started:
            cp.wait_send()
        for cp in local:
            cp.wait()

    return _pcall(
        body, name="all_gather_weights", in_specs=[hbm] * n, out_specs=[hbm] * n,
        out_shape=[SDS((N_DEV,) + a.shape, a.dtype) for a in arrays],
        scratch_shapes=[pltpu.SemaphoreType.DMA((n * per,)), pltpu.SemaphoreType.DMA((n * per,)),
                        pltpu.SemaphoreType.DMA((n,))],
    )(*arrays)


def _core_exchange(slabs):
    n = len(slabs)
    hbm = pl.BlockSpec(memory_space=pltpu.HBM)

    def body(*refs):
        ins, outs = refs[:n], refs[n:2 * n]
        send_sems, recv_sems = refs[2 * n:]
        x, y, c = _mesh_pos()
        copies = [pltpu.make_async_remote_copy(
            src_ref=ins[w].at[pl.ds(0, N_CHIPS), 1 - c], dst_ref=outs[w], send_sem=send_sems.at[w],
            recv_sem=recv_sems.at[w], device_id=(x, y, 1 - c), device_id_type=MESH) for w in range(n)]
        for cp in copies:
            cp.start()
        for cp in copies:
            cp.wait_recv()
        for cp in copies:
            cp.wait_send()

    return _pcall(
        body, name="grads_core_exchange", in_specs=[hbm] * n, out_specs=[hbm] * n,
        out_shape=[SDS((N_CHIPS,) + a.shape[2:], a.dtype) for a in slabs],
        scratch_shapes=[pltpu.SemaphoreType.DMA((n,)), pltpu.SemaphoreType.DMA((n,))],
    )(*slabs)


def _chip_exchange(partials):
    n = len(partials)
    per = N_CHIPS - 1
    hbm = pl.BlockSpec(memory_space=pltpu.HBM)

    def body(*refs):
        ins, outs = refs[:n], refs[n:2 * n]
        send_sems, recv_sems, local_sems = refs[2 * n:]
        x, y, c = _mesh_pos()
        mine = 2 * x + y
        local, sends, recvs = [], [], []
        for w in range(n):
            cp = pltpu.make_async_copy(ins[w].at[mine], outs[w].at[mine], local_sems.at[w])
            cp.start()
            local.append(cp)
            for j, (px, py) in enumerate(_other_chips(x, y)):
                theirs = 2 * px + py
                sems = dict(send_sem=send_sems.at[w * per + j], recv_sem=recv_sems.at[w * per + j],
                            device_id=(px, py, c), device_id_type=MESH)
                cp = pltpu.make_async_remote_copy(src_ref=ins[w].at[theirs], dst_ref=outs[w].at[mine], **sems)
                cp.start()
                sends.append(cp)
                recvs.append(pltpu.make_async_remote_copy(src_ref=ins[w].at[theirs], dst_ref=outs[w].at[theirs], **sems))
        for cp in recvs:
            cp.wait_recv()
        for cp in sends:
            cp.wait_send()
        for cp in local:
            cp.wait()

    return _pcall(
        body, name="grads_chip_exchange", in_specs=[hbm] * n, out_specs=[hbm] * n,
        out_shape=[SDS(a.shape, a.dtype) for a in partials],
        scratch_shapes=[pltpu.SemaphoreType.DMA((n * per,)), pltpu.SemaphoreType.DMA((n * per,)),
                        pltpu.SemaphoreType.DMA((n,))],
    )(*partials)


def _core_sum(slabs, from_sibling, core, name):
    _, _, r, c = slabs.shape
    tr = min(r, 256)

    def body(core_ref, a_ref, b_ref, o_ref):
        o_ref[...] = (a_ref[0].astype(F32) + b_ref[...].astype(F32)).astype(BF16)

    return _pcall(
        body, name=name,
        grid_spec=pltpu.PrefetchScalarGridSpec(
            num_scalar_prefetch=1, grid=(N_CHIPS, r // tr),
            in_specs=[pl.BlockSpec((1, 1, tr, c), lambda g, i, core: (g, core[0], i, 0)),
                      pl.BlockSpec((1, tr, c), lambda g, i, core: (g, i, 0))],
            out_specs=pl.BlockSpec((1, tr, c), lambda g, i, core: (g, i, 0))),
        out_shape=SDS((N_CHIPS, r, c), BF16), compiler_params=_params(),
    )(core, slabs, from_sibling)


def _adam_math(w, g, m, v):
    m = ADAM_B1 * m + (1.0 - ADAM_B1) * g
    v = ADAM_B2 * v + (1.0 - ADAM_B2) * (g * g)
    m_hat = m / (1.0 - ADAM_B1 ** ADAM_STEP)
    v_hat = v / (1.0 - ADAM_B2 ** ADAM_STEP)
    delta = -ADAM_LR * (m_hat / (jnp.sqrt(v_hat) + ADAM_EPS) + ADAM_WD * w)
    return delta, m, v


def _adamw(recv, w, m, v, name):
    r, c = w.shape
    tr = min(r, 128)
    n_parts = recv.shape[0]

    def body(g_ref, w_ref, m_ref, v_ref, go_ref, d_ref, mo_ref, vo_ref):
        g = g_ref[0].astype(F32)
        for s in range(1, n_parts):
            g = g + g_ref[s].astype(F32)
        delta, mn, vn = _adam_math(w_ref[...], g, m_ref[...], v_ref[...])
        go_ref[...] = g
        d_ref[...] = delta
        mo_ref[...] = mn
        vo_ref[...] = vn

    blk = pl.BlockSpec((tr, c), lambda i: (i, 0))
    return _pcall(
        body, name=name, grid=(r // tr,),
        in_specs=[pl.BlockSpec((n_parts, tr, c), lambda i: (0, i, 0)), blk, blk, blk],
        out_specs=[blk] * 4, out_shape=[SDS((r, c), F32)] * 4, compiler_params=_params(),
    )(recv, w, m, v)


VEC_ROWS = 32


def _small_allreduce_adamw(vec, w, m, v):
    def body(vec_ref, w_ref, m_ref, v_ref, g_ref, d_ref, mo_ref, vo_ref, gath, send_sems, recv_sems):
        pos = _mesh_pos()
        me = 4 * pos[0] + 2 * pos[1] + pos[2]
        sends, recvs = [], []
        for k in range(1, N_DEV):
            peer, peer_idx = _peer(pos, k)
            cp = pltpu.make_async_remote_copy(src_ref=vec_ref, dst_ref=gath.at[me], send_sem=send_sems.at[k - 1],
                                              recv_sem=recv_sems.at[k - 1], device_id=peer, device_id_type=MESH)
            cp.start()
            sends.append(cp)
            recvs.append(pltpu.make_async_remote_copy(src_ref=vec_ref, dst_ref=gath.at[peer_idx],
                                                      send_sem=send_sems.at[k - 1], recv_sem=recv_sems.at[k - 1],
                                                      device_id=peer, device_id_type=MESH))
        gath[me] = vec_ref[...]
        for cp in recvs:
            cp.wait_recv()
        for cp in sends:
            cp.wait_send()
        tot = gath[0]
        for s in range(1, N_DEV):
            tot = tot + gath[s]
        rowi = lax.broadcasted_iota(jnp.int32, (8, LANES), 0)
        mine = jnp.sum(jnp.where(rowi == me, tot[16:24, :], 0.0), axis=0, keepdims=True)
        g = jnp.concatenate([tot[0:16, :], jnp.broadcast_to(mine, (8, LANES)), tot[24:32, :]], axis=0)
        delta, mn, vn = _adam_math(w_ref[...], g, m_ref[...], v_ref[...])
        g_ref[...] = g
        d_ref[...] = delta
        mo_ref[...] = mn
        vo_ref[...] = vn

    vm = pl.BlockSpec(memory_space=pltpu.VMEM)
    return _pcall(
        body, name="small_allreduce_adamw", in_specs=[vm] * 4, out_specs=[vm] * 4,
        out_shape=[SDS((VEC_ROWS, LANES), F32)] * 4,
        scratch_shapes=[pltpu.VMEM((N_DEV, VEC_ROWS, LANES), F32), pltpu.SemaphoreType.DMA((N_DEV - 1,)),
                        pltpu.SemaphoreType.DMA((N_DEV - 1,))],
        compiler_params=pltpu.CompilerParams(has_side_effects=True),
    )(vec, w, m, v)


def _cols_to_slabs(a):
    r, c8 = a.shape
    return a.reshape(r, N_DEV, c8 // N_DEV).transpose(1, 0, 2)


def _slabs_to_cols(a):
    n, r, c = a.shape
    return a.transpose(1, 0, 2).reshape(r, n * c)


def _rows8(vec):
    return vec.reshape(-1, LANES)


def _pad_rows(a, rows):
    return jnp.pad(a, ((0, rows - a.shape[0]), (0, LANES - a.shape[1])))


def kernel(x, p, fox_norm, fox_w_in, fox_b_f, fox_w_out, dil_norm, dil_w_in, dil_w_out, ple_w_up, ple_w_gate, final_norm, loss_target, m_fox_norm, m_fox_w_in, m_fox_b_f, m_fox_w_out, m_dil_norm, m_dil_w_in, m_dil_w_out, m_ple_w_up, m_ple_w_gate, m_final_norm, v_fox_norm, v_fox_w_in, v_fox_b_f, v_fox_w_out, v_dil_norm, v_dil_w_in, v_dil_w_out, v_ple_w_up, v_ple_w_gate, v_final_norm):
    t = x.shape[1]
    d = D_MODEL
    xs, tgt = x[0], loss_target[0]
    p0, p1 = p[0, 0], p[1, 0]
    fox_cols = fox_w_in.shape[2]
    ple_dim = ple_w_up.shape[1]

    shards = [fox_w_in[0].astype(BF16), fox_w_out[0].astype(BF16), dil_w_in[0].astype(BF16),
              dil_w_out[0].astype(BF16), ple_w_up.reshape(-1, LANES).astype(BF16),
              ple_w_gate.reshape(-1, d).astype(BF16), dil_norm]
    gw = _all_gather(shards)
    w_fox_in = _slabs_to_cols(gw[0])
    w_fox_main = w_fox_in[:, :4 * d]
    w_fox_f = jnp.pad(w_fox_in[:, 4 * d:], ((0, 0), (0, LANES - FOX_HEADS)))
    w_fox_out = gw[1].reshape(d, d)
    w_dil_in = _slabs_to_cols(gw[2])
    w_dil_out = gw[3].reshape(d, d)
    w_up = gw[4].reshape(N_DEV, 2, ple_dim, LANES).transpose(1, 2, 0, 3).reshape(2, ple_dim, d)
    w_gate = gw[5].reshape(N_DEV, 2, d // N_DEV, d).transpose(1, 0, 2, 3).reshape(2, d, d)
    dil_norm_full = gw[6].reshape(1, d)
    b_pad = jnp.pad(fox_b_f, ((0, 0), (0, LANES - FOX_HEADS)))

    n0, r0 = _rms_fwd(xs, fox_norm, "rms_fox")
    proj0 = _mm([n0], w_fox_main, "nn", "fox_in_proj", tiles=IN_PROJ_TILES)
    projf = _mm([n0], w_fox_f, "nn", "fox_gate_proj", tiles=IN_PROJ_TILES, out_dtype=F32)
    c_all = _fox_gate_fwd(projf, b_pad)
    qaug_fwd = _fox_aug(c_all, c_all, 1.0, 0.0, 0, FOX_AUG, "fox_aug_q_fwd")
    kaug = _fox_aug(c_all, c_all, -1.0, 0.0, FOX_AUG, 0, "fox_aug_k")
    o0, g0, lse0 = _fox_fwd(proj0, qaug_fwd, kaug)
    h1 = _mm_residual(g0, w_fox_out, "nn", xs, "fox_out_proj")
    h2, u0, a0 = _ple_fwd(h1, p0, w_up[0], w_gate[0], "ple0_fwd")

    n1, r1 = _rms_fwd(h2, dil_norm_full, "rms_dil")
    proj1 = _mm([n1], w_dil_in, "nn", "dil_in_proj", tiles=IN_PROJ_TILES)
    n_heads = len(DIL_PATTERN) * DIL_HEADS
    slopes = 2.0 ** (-ALIBI_MAX_EXP * jnp.arange(1, n_heads + 1, dtype=F32) / n_heads)
    dil_o, dil_lse, dil_slopes = [], [], []
    for grp, (_, dil) in enumerate(DIL_PATTERN):
        sl = (slopes[grp * DIL_HEADS:(grp + 1) * DIL_HEADS] * dil).reshape(DIL_HEADS, 1, 1)
        og, lg = _dil_fwd(proj1, sl, grp, dil, f"dil_attn_fwd_{grp}")
        dil_o.append(og)
        dil_lse.append(lg)
        dil_slopes.append(sl)
    z1_col0 = 9 * d
    o1, g1, lse1 = _dil_mix(dil_o, dil_lse, proj1, z1_col0)
    h3 = _mm_residual(g1, w_dil_out, "nn", h2, "dil_out_proj")
    h4, u1, a1 = _ple_fwd(h3, p1, w_up[1], w_gate[1], "ple1_fwd")

    dh4, d_final_norm, loss_part = _final_bwd(h4, final_norm.reshape(1, d), tgt)

    du1, da1 = _ple_bwd_elem(dh4, u1, a1, "ple1_bwd_elem")
    dw_up1 = _dw(p1, du1, "ple1_dw_up")
    dw_gate1 = _dw(h3, da1, "ple1_dw_gate")
    dh3 = _mm_residual(da1, w_gate[1], "nt", dh4, "ple1_dh")

    dw_dil_out = _dw(g1, dh3, "dil_dw_out")
    do1, dz1, delta1 = _mm_gate_bwd(dh3, w_dil_out, proj1, z1_col0, o1, DIL_HEADS, "dil_dgate")
    n_grp = len(DIL_PATTERN)
    dqkv = [_dil_bwd(proj1, do1, lse1, delta1, dil_slopes[grp], grp, dil, f"dil_attn_bwd_{grp}")
            for grp, (_, dil) in enumerate(DIL_PATTERN)]
    dw_cols = [_dw(n1, dqkv[grp], f"dil_dw_in_{kind}{grp}", sub=kind) for kind in range(3) for grp in range(n_grp)]
    dw_dil_in = jnp.concatenate(dw_cols + [_dw(n1, dz1, "dil_dw_in_z")], axis=1)
    group_major = lambda kb: jnp.where(kb < 3 * n_grp, (kb % 3) * n_grp + kb // 3, kb)
    dh2, d_dil_norm = _mm_in_bwd(dqkv + [dz1], w_dil_in, h2, dil_norm_full, r1, dh3, "dil_dx", w_kmap=group_major)

    du0, da0 = _ple_bwd_elem(dh2, u0, a0, "ple0_bwd_elem")
    dw_up0 = _dw(p0, du0, "ple0_dw_up")
    dw_gate0 = _dw(h1, da0, "ple0_dw_gate")
    dh1 = _mm_residual(da0, w_gate[0], "nt", dh2, "ple0_dh")

    dw_fox_out = _dw(g0, dh1, "fox_dw_out")
    do0, dz0, delta0 = _mm_gate_bwd(dh1, w_fox_out, proj0, 3 * d, o0, FOX_HEADS, "fox_dgate")
    head_cols = lambda a: jnp.pad(a, ((0, 0), (0, LANES - FOX_HEADS)))
    lse_cols = head_cols(lse0.reshape(FOX_HEADS, t).T)
    delta_cols = head_cols(delta0.reshape(FOX_HEADS, t).T)
    qaug_bwd = _fox_aug(c_all, lse_cols, 1.0, -1.0, 0, FOX_AUG, "fox_aug_q_bwd")
    doaug = _fox_aug(delta_cols, delta_cols, -1.0, 0.0, 0, None, "fox_aug_do")
    dq0, dk0, dv0, dck_wide, dcq = _fox_bwd(proj0, do0, qaug_bwd, kaug, doaug)
    dc_query, dc_key = _fox_unpack_dc(dck_wide, dcq)
    df, d_b_f = _fox_gate_bwd(projf, b_pad, head_cols(dc_query), head_cols(dc_key))
    dproj0 = [dq0, dk0, dv0, dz0]
    dw_fox_parts = [_dw(n0, dpart, f"fox_dw_in_{s}") for s, dpart in enumerate(dproj0)]
    dw_fox_f = _dw(n0, df, "fox_dw_gate")
    dn0_f = _mm([df], w_fox_f, "nt", "fox_dx_gate", out_dtype=F32)
    grad_x, d_fox_norm = _mm_in_bwd(dproj0, w_fox_main, xs, fox_norm, r0, dh1, "fox_dx", more=dn0_f)

    dw_fox_in = jnp.concatenate(dw_fox_parts + [dw_fox_f[:, :FOX_HEADS]], axis=1)
    slabs = [_cols_to_slabs(dw_fox_in), dw_fox_out.reshape(N_DEV, d // N_DEV, d), _cols_to_slabs(dw_dil_in),
             dw_dil_out.reshape(N_DEV, d // N_DEV, d),
             jnp.stack([dw_up0, dw_up1]).reshape(2, ple_dim, N_DEV, LANES).transpose(2, 0, 1, 3).reshape(N_DEV, -1, LANES),
             jnp.stack([dw_gate0, dw_gate1]).reshape(2, N_DEV, d // N_DEV, d).transpose(1, 0, 2, 3).reshape(N_DEV, -1, d)]
    names = ["fox_w_in", "fox_w_out", "dil_w_in", "dil_w_out", "ple_w_up", "ple_w_gate"]
    slabs = [s.reshape((N_CHIPS, 2) + s.shape[1:]) for s in slabs]
    from_sibling = _core_exchange(slabs)
    core = lax.axis_index("c").astype(jnp.int32).reshape(1)
    chip_sums = [_core_sum(s, f, core, "core_sum_" + nm) for s, f, nm in zip(slabs, from_sibling, names)]
    recv = _chip_exchange(chip_sums)
    big = [(fox_w_in, m_fox_w_in, v_fox_w_in), (fox_w_out, m_fox_w_out, v_fox_w_out),
           (dil_w_in, m_dil_w_in, v_dil_w_in), (dil_w_out, m_dil_w_out, v_dil_w_out),
           (ple_w_up, m_ple_w_up, v_ple_w_up), (ple_w_gate, m_ple_w_gate, v_ple_w_gate)]
    upd = {}
    for rv, (w, m, v), nm in zip(recv, big, names):
        shp2 = rv.shape[1:]
        res = _adamw(rv, w.reshape(shp2), m.reshape(shp2), v.reshape(shp2), "adamw_" + nm)
        upd[nm] = [a.reshape(w.shape) for a in res]

    loss_row = jnp.where(jnp.arange(LANES) == 0, loss_part, 0.0)
    vec = jnp.concatenate([_rows8(d_fox_norm), _rows8(d_final_norm), _rows8(d_dil_norm), d_b_f, loss_row,
                           jnp.zeros((VEC_ROWS - 26, LANES), F32)], axis=0)

    def small_pack(a_fox_norm, a_final_norm, a_dil_norm, a_b_f):
        return jnp.concatenate([_rows8(a_fox_norm), _rows8(a_final_norm), _pad_rows(a_dil_norm, 8),
                                _pad_rows(a_b_f, 8)], axis=0)

    sg, sd, sm, sv = _small_allreduce_adamw(
        vec, small_pack(fox_norm, final_norm, dil_norm, fox_b_f),
        small_pack(m_fox_norm, m_final_norm, m_dil_norm, m_fox_b_f),
        small_pack(v_fox_norm, v_final_norm, v_dil_norm, v_fox_b_f))

    def small_unpack(a):
        return {"fox_norm": a[0:8].reshape(1, d), "final_norm": a[8:16].reshape(d), "dil_norm": a[16:17],
                "fox_b_f": a[24:25, :FOX_HEADS]}

    loss = sg[25, 0]
    order = ["fox_norm", "fox_w_in", "fox_b_f", "fox_w_out", "dil_norm", "dil_w_in", "dil_w_out", "ple_w_up",
             "ple_w_gate", "final_norm"]
    out = [loss, grad_x[None]]
    for idx, small in enumerate((sg, sd, sm, sv)):
        sp = small_unpack(small)
        out += [sp[nm] if nm in sp else upd[nm][idx] for nm in order]
    return tuple(out)
```

```python
import functools

import jax
import jax.numpy as jnp
from jax import lax
from jax.experimental import pallas as pl
from jax.experimental.pallas import tpu as pltpu

F32 = jnp.float32
BF16 = jnp.bfloat16
SDS = jax.ShapeDtypeStruct

D_MODEL = 1024
N_DEV = 8
LANES = 128
FOX_HEADS = 16
FOX_HEAD_DIM = 64
FOX_PAIRS = FOX_HEADS // 2
DIL_HEADS = 8
DIL_BLOCK = 128
DIL_PATTERN = ((128, 1), (512, 4), (2048, 16))
ALIBI_MAX_EXP = 8.0
RMS_EPS = 1e-6
ADAM_LR, ADAM_B1, ADAM_B2, ADAM_EPS, ADAM_WD, ADAM_STEP = 0.001, 0.9, 0.999, 1e-08, 0.01, 10
VMEM_LIMIT = 48 * 1024 * 1024
NEG_INF = float("-inf")

NN = (((1,), (0,)), ((), ()))
NT = (((1,), (1,)), ((), ()))
TN = (((0,), (0,)), ((), ()))
MESH = pl.DeviceIdType.MESH


def _pcall(body, **kw):
    return pl.pallas_call(body, **kw)


def _params(**kw):
    return pltpu.CompilerParams(vmem_limit_bytes=VMEM_LIMIT, **kw)


def _dot(a, b, dims):
    return lax.dot_general(a, b, dims, preferred_element_type=F32)


def _sigmoid(x):
    return 1.0 / (1.0 + jnp.exp(-x))


def _mm(a_parts, b, mode, name, tiles=(512, 1024, 1024), extras=(), outs=None, epilogue=None, out_dtype=BF16,
        b_kmap=None, b_sub=None):
    na = len(a_parts)
    stack = [a.shape[0] if a.ndim == 3 else 1 for a in a_parts]
    first = [sum(stack[:s]) for s in range(na)]
    if mode == "tn":
        k_part, m = a_parts[0].shape
        n = b.shape[-1]
    else:
        m, k_part = a_parts[0].shape[-2:]
        n = b.shape[1] if mode == "nn" else b.shape[0]
    tm, tn, tk = min(tiles[0], m), min(tiles[1], n), min(tiles[2], k_part)
    kb = k_part // tk
    nk = sum(stack) * kb
    grid = (m // tm, n // tn, nk)
    b_kmap = b_kmap or (lambda k: k)

    in_specs = []
    for s in range(na):
        if mode == "tn":
            in_specs.append(pl.BlockSpec((tk, tm), lambda i, j, k: (k, i)))
            continue

        def rel(k, s=s):
            return jnp.clip(k - first[s] * kb, 0, stack[s] * kb - 1)

        if a_parts[s].ndim == 3:
            in_specs.append(pl.BlockSpec((None, tm, tk), lambda i, j, k, rel=rel: (rel(k) // kb, i, rel(k) % kb)))
        else:
            in_specs.append(pl.BlockSpec((tm, tk), lambda i, j, k, rel=rel: (i, rel(k))))
    if mode == "nt":
        in_specs.append(pl.BlockSpec((tn, tk), lambda i, j, k: (j, b_kmap(k))))
    elif b_sub is not None:
        in_specs.append(pl.BlockSpec((None, tk, tn), lambda i, j, k: (b_sub, k, j)))
    else:
        in_specs.append(pl.BlockSpec((tk, tn), lambda i, j, k: (b_kmap(k), j)))
    for _, blk, imap in extras:
        in_specs.append(pl.BlockSpec(blk, imap))
    if outs is None:
        outs = [(SDS((m, n), out_dtype), (tm, tn), lambda i, j, k: (i, j))]
    out_specs = [pl.BlockSpec(blk, imap) for _, blk, imap in outs]
    ne, no = len(extras), len(outs)
    dims = {"nn": NN, "nt": NT, "tn": TN}[mode]

    def finish(res, e_refs, o_refs, i):
        if epilogue is None:
            o_refs[0][...] = res.astype(o_refs[0].dtype)
        else:
            epilogue(res, e_refs, o_refs, i)

    def body(*refs):
        a_refs = refs[:na]
        b_ref = refs[na]
        e_refs = refs[na + 1:na + 1 + ne]
        o_refs = refs[na + 1 + ne:na + 1 + ne + no]
        i, k = pl.program_id(0), pl.program_id(2)
        if nk == 1:
            finish(_dot(a_refs[0][...].astype(BF16), b_ref[...].astype(BF16), dims), e_refs, o_refs, i)
            return
        acc = refs[-1]

        @pl.when(k == 0)
        def _():
            acc[...] = jnp.zeros_like(acc)

        def step(a_ref):
            acc[...] += _dot(a_ref[...].astype(BF16), b_ref[...].astype(BF16), dims)

        for s in range(na):
            if na == 1:
                step(a_refs[0])
            else:
                in_use = (k >= first[s] * kb) & (k < (first[s] + stack[s]) * kb)
                pl.when(in_use)(functools.partial(step, a_refs[s]))

        @pl.when(k == nk - 1)
        def _():
            finish(acc[...], e_refs, o_refs, i)

    res = _pcall(
        body, name=name, grid=grid, in_specs=in_specs, out_specs=out_specs,
        out_shape=[o[0] for o in outs], scratch_shapes=[] if nk == 1 else [pltpu.VMEM((tm, tn), F32)],
        compiler_params=_params(dimension_semantics=("arbitrary", "arbitrary", "arbitrary")),
    )(*a_parts, b, *[e[0] for e in extras])
    return res[0] if len(res) == 1 else res


IN_PROJ_TILES = (1024, 1024, 1024)
DW_TILES = (1024, 1024, 512)


def _dw(x, dy, name, sub=None):
    return _mm([x], dy, "tn", name, tiles=DW_TILES, b_sub=sub)


def _add_extra_epilogue(acc, e_refs, o_refs, i):
    o_refs[0][...] = acc + e_refs[0][...]


def _mm_residual(a, b, mode, res, name):
    m = a.shape[0]
    n = b.shape[1] if mode == "nn" else b.shape[0]
    tm, tn = 512, 1024
    return _mm([a], b, mode, name, tiles=(tm, tn, 1024),
               extras=[(res, (tm, tn), lambda i, j, k: (i, j))],
               outs=[(SDS((m, n), F32), (tm, tn), lambda i, j, k: (i, j))],
               epilogue=_add_extra_epilogue)


def _rms_fwd(h, g, name):
    t, d = h.shape
    tm = 512

    def body(h_ref, g_ref, n_ref, r_ref):
        x = h_ref[...]
        r = lax.rsqrt(jnp.mean(x * x, axis=-1, keepdims=True) + RMS_EPS)
        n_ref[...] = ((x * r) * g_ref[...]).astype(BF16)
        r_ref[...] = r

    return _pcall(
        body, name=name, grid=(t // tm,),
        in_specs=[pl.BlockSpec((tm, d), lambda i: (i, 0)), pl.BlockSpec((1, d), lambda i: (0, 0))],
        out_specs=[pl.BlockSpec((tm, d), lambda i: (i, 0)), pl.BlockSpec((tm, 1), lambda i: (i, 0))],
        out_shape=[SDS((t, d), BF16), SDS((t, 1), F32)],
        compiler_params=_params(),
    )(h, g)


def _rms_bwd_rows(dn, x, g, r):
    xhat = x * r
    dxhat = dn * g
    dx = r * (dxhat - xhat * jnp.mean(dxhat * xhat, axis=-1, keepdims=True))
    dg = jnp.sum(dn * xhat, axis=0, keepdims=True)
    return dx, dg


def _mm_in_bwd(d_parts, w, h, g, r, dres, name, more=None, w_kmap=None):
    t = h.shape[0]
    tm = 512
    tk = D_MODEL
    row = lambda i, j, k: (i, 0)
    extras = [(h, (tm, D_MODEL), row), (g, (1, D_MODEL), lambda i, j, k: (0, 0)), (r, (tm, 1), row),
              (dres, (tm, D_MODEL), row)]
    if more is not None:
        extras.append((more, (tm, D_MODEL), row))

    def epilogue(acc, e_refs, o_refs, i):
        dn = acc if more is None else acc + e_refs[4][...]
        dx, dg = _rms_bwd_rows(dn, e_refs[0][...], e_refs[1][...], e_refs[2][...])
        o_refs[0][...] = e_refs[3][...] + dx

        @pl.when(i == 0)
        def _():
            o_refs[1][...] = dg

        @pl.when(i > 0)
        def _():
            o_refs[1][...] += dg

    return _mm(d_parts, w, "nt", name, tiles=(tm, D_MODEL, tk), extras=extras,
               outs=[(SDS((t, D_MODEL), F32), (tm, D_MODEL), row),
                     (SDS((1, D_MODEL), F32), (1, D_MODEL), lambda i, j, k: (0, 0))],
               epilogue=epilogue, b_kmap=w_kmap)


def _final_bwd(h, g, tgt):
    t, d = h.shape
    tm = 256

    def body(h_ref, g_ref, t_ref, dh_ref, dg_ref, loss_ref):
        i = pl.program_id(0)
        x = h_ref[...]
        gg = g_ref[...]
        r = lax.rsqrt(jnp.mean(x * x, axis=-1, keepdims=True) + RMS_EPS)
        err = (x * r) * gg - t_ref[...]
        part = 0.5 * jnp.sum(jnp.mean(err * err, axis=-1, keepdims=True), axis=0, keepdims=True)
        dx, dg = _rms_bwd_rows(err * (1.0 / d), x, gg, r)
        dh_ref[...] = dx

        @pl.when(i == 0)
        def _():
            dg_ref[...] = dg
            loss_ref[...] = jnp.broadcast_to(part, loss_ref.shape)

        @pl.when(i > 0)
        def _():
            dg_ref[...] += dg
            loss_ref[...] += jnp.broadcast_to(part, loss_ref.shape)

    return _pcall(
        body, name="final_norm_loss", grid=(t // tm,),
        in_specs=[pl.BlockSpec((tm, d), lambda i: (i, 0)), pl.BlockSpec((1, d), lambda i: (0, 0)),
                  pl.BlockSpec((tm, d), lambda i: (i, 0))],
        out_specs=[pl.BlockSpec((tm, d), lambda i: (i, 0)), pl.BlockSpec((1, d), lambda i: (0, 0)),
                   pl.BlockSpec((1, LANES), lambda i: (0, 0))],
        out_shape=[SDS((t, d), F32), SDS((1, d), F32), SDS((1, LANES), F32)],
        compiler_params=_params(),
    )(h, g, tgt)


GATE_ROWS = 256


def _split3(x):
    hi = x.astype(BF16)
    r1 = x - hi.astype(F32)
    mid = r1.astype(BF16)
    lo = (r1 - mid.astype(F32)).astype(BF16)
    return hi, mid, lo


def _tri_sum(x, upper):
    rows = x.shape[0]
    ri = lax.broadcasted_iota(jnp.int32, (rows, rows), 0)
    ci = lax.broadcasted_iota(jnp.int32, (rows, rows), 1)
    tri = jnp.where((ri <= ci) if upper else (ri >= ci), 1.0, 0.0).astype(BF16)
    hi, mid, lo = _split3(x)
    return _dot(tri, hi, NN) + _dot(tri, mid, NN) + _dot(tri, lo, NN)


def _log_sigmoid(x):
    return jnp.minimum(x, 0.0) - jnp.log1p(jnp.exp(-jnp.abs(x)))


def _fox_gate_fwd(projf, bpad):
    t = projf.shape[0]
    tb = GATE_ROWS

    def body(x_ref, b_ref, c_ref, carry):
        i = pl.program_id(0)

        @pl.when(i == 0)
        def _():
            carry[...] = jnp.zeros_like(carry)

        c_ref[...] = _tri_sum(_log_sigmoid(x_ref[...] + b_ref[...]), upper=False) + carry[...]
        carry[...] = c_ref[pl.ds(tb - 1, 1), :]

    return _pcall(
        body, name="fox_gate_fwd", grid=(t // tb,),
        in_specs=[pl.BlockSpec((tb, LANES), lambda i: (i, 0)), pl.BlockSpec((1, LANES), lambda i: (0, 0))],
        out_specs=pl.BlockSpec((tb, LANES), lambda i: (i, 0)),
        out_shape=SDS((t, LANES), F32), scratch_shapes=[pltpu.VMEM((1, LANES), F32)],
        compiler_params=_params(),
    )(projf, bpad)


def _fox_gate_bwd(projf, bpad, dc_query, dc_key):
    t = projf.shape[0]
    tb = GATE_ROWS
    nb = t // tb

    def body(x_ref, b_ref, dcq_ref, dck_ref, df_ref, db_ref, carry, buf):
        i = pl.program_id(0)

        @pl.when(i == 0)
        def _():
            carry[...] = jnp.zeros_like(carry)

        buf[...] = _tri_sum(dcq_ref[...] - dck_ref[...], upper=True) + carry[...]
        carry[...] = buf[pl.ds(0, 1), :]
        df = buf[...] * _sigmoid(-(x_ref[...] + b_ref[...]))
        df_ref[...] = df.astype(BF16)
        part = jnp.sum(df, axis=0, keepdims=True)

        @pl.when(i == 0)
        def _():
            db_ref[...] = part

        @pl.when(i > 0)
        def _():
            db_ref[...] += part

    rev = lambda i: (nb - 1 - i, 0)
    return _pcall(
        body, name="fox_gate_bwd", grid=(nb,),
        in_specs=[pl.BlockSpec((tb, LANES), rev), pl.BlockSpec((1, LANES), lambda i: (0, 0)),
                  pl.BlockSpec((tb, LANES), rev), pl.BlockSpec((tb, LANES), rev)],
        out_specs=[pl.BlockSpec((tb, LANES), rev), pl.BlockSpec((1, LANES), lambda i: (0, 0))],
        out_shape=[SDS((t, LANES), BF16), SDS((1, LANES), F32)],
        scratch_shapes=[pltpu.VMEM((1, LANES), F32), pltpu.VMEM((tb, LANES), F32)],
        compiler_params=_params(),
    )(projf, bpad, dc_query, dc_key)


FOX_TQ = 1024
FOX_TQ_FWD = 1024
FOX_SCALE = FOX_HEAD_DIM ** -0.5


def _low_lanes(shape):
    return lax.broadcasted_iota(jnp.int32, shape, len(shape) - 1) < FOX_HEAD_DIM


FOX_AUG = 3
FOX_CHAIN = 256
FOX_SUM_ROWS = 8


def _top_rows(shape):
    return lax.broadcasted_iota(jnp.int32, shape, 0) < FOX_HEAD_DIM


def _fox_aug(a, b, sign_a, sign_b, piece_entry, ones_entry, name):
    t = a.shape[0]
    tb = 512

    def body(a_ref, b_ref, o_ref):
        x = sign_a * a_ref[...]
        if sign_b != 0.0:
            x = x + sign_b * b_ref[...]
        head = lax.broadcasted_iota(jnp.int32, (LANES, D_MODEL), 0)
        col = lax.broadcasted_iota(jnp.int32, (LANES, D_MODEL), 1)
        base = (head // 2) * LANES + (1 - head % 2) * FOX_HEAD_DIM + piece_entry
        acc = jnp.zeros((tb, D_MODEL), F32)
        for e, piece in enumerate(_split3(x)):
            place = jnp.where((head < FOX_HEADS) & (col == base + e), 1.0, 0.0).astype(BF16)
            acc = acc + _dot(piece, place, NN)
        if ones_entry is not None:
            ent = lax.broadcasted_iota(jnp.int32, (1, D_MODEL), 1) % FOX_HEAD_DIM
            acc = acc + jnp.where((ent >= ones_entry) & (ent < ones_entry + FOX_AUG), 1.0, 0.0)
        o_ref[...] = acc.astype(BF16)

    blk = pl.BlockSpec((tb, LANES), lambda i: (i, 0))
    return _pcall(
        body, name=name, grid=(t // tb,), in_specs=[blk, blk],
        out_specs=pl.BlockSpec((tb, D_MODEL), lambda i: (i, 0)), out_shape=SDS((t, D_MODEL), BF16),
        compiler_params=_params(),
    )(a, b)


def _fox_unpack_dc(dck_wide, dcq):
    t = dck_wide.shape[0]
    dck = dck_wide.reshape(t, FOX_PAIRS, 2, FOX_HEAD_DIM)[:, :, ::-1, 0].reshape(t, FOX_HEADS)
    return dcq[:, :, 0, :].reshape(FOX_HEADS, t).T, dck


def _causal_steps(nq, key_major):
    if key_major:
        pairs = [(i, j) for j in range(nq) for i in range(j, nq)]
    else:
        pairs = [(i, j) for i in range(nq) for j in range(i + 1)]
    return (jnp.asarray([p[0] for p in pairs], jnp.int32), jnp.asarray([p[1] for p in pairs], jnp.int32))


def _pair_operand(low, own, other, hh):
    return jnp.where(low, own, other) if hh == 0 else jnp.where(low, other, own)


def _fox_fwd(proj, qaug, kaug):
    t = proj.shape[0]
    tq = tk = min(FOX_TQ_FWD, t)
    nq = t // tq
    cb = D_MODEL // LANES
    half = min(FOX_CHAIN, tq)

    i_tab, j_tab = _causal_steps(nq, key_major=False)

    def body(i_ref, j_ref, q_ref, k_ref, v_ref, z_ref, qa_ref, ka_ref, o_ref, g_ref, lse_ref, m_s, l_s, acc_s):
        step = pl.program_id(1)
        i, j = i_ref[step], j_ref[step]

        @pl.when(j == 0)
        def _():
            m_s[...] = jnp.full_like(m_s, NEG_INF)
            l_s[...] = jnp.zeros_like(l_s)
            acc_s[...] = jnp.zeros_like(acc_s)

        low = _low_lanes((tq, LANES))
        top = _top_rows((LANES, tq))

        def update(masked):
            qs = q_ref[...] * FOX_SCALE
            qa, k, ka, v = qa_ref[...], k_ref[...], ka_ref[...], v_ref[...]
            if masked:
                causal = (lax.broadcasted_iota(jnp.int32, (tk, tq), 0) <= lax.broadcasted_iota(jnp.int32, (tk, tq), 1))
            one = jnp.ones_like(v)
            chains = [(hh, slice(c * half, (c + 1) * half)) for hh in range(2) for c in range(tq // half)]
            qh = [_pair_operand(low, qs, qa, hh) for hh in range(2)]
            kh = [_pair_operand(low, k, ka, hh) for hh in range(2)]
            vh = [_pair_operand(low, v, one, hh) for hh in range(2)]
            scores = [_dot(kh[hh], qh[hh][cols, :], NT) for hh, cols in chains]
            for (hh, cols), s in zip(chains, scores):
                if masked:
                    s = jnp.where(causal[:, cols], s, NEG_INF)
                m_prev = m_s[hh, :, cols]
                m_new = jnp.maximum(m_prev, jnp.max(s, axis=0, keepdims=True))
                alpha = jnp.exp(m_prev - m_new)
                pv = _dot(vh[hh], jnp.exp(s - m_new).astype(BF16), TN)
                sums = pv[FOX_HEAD_DIM:FOX_HEAD_DIM + FOX_SUM_ROWS, :] if hh == 0 else pv[0:FOX_SUM_ROWS, :]
                l_s[hh, :, cols] = alpha * l_s[hh, :, cols] + sums
                m_s[hh, :, cols] = m_new
                own = top[:, cols] if hh == 0 else jnp.logical_not(top[:, cols])
                acc_s[:, cols] = jnp.where(own, acc_s[:, cols] * alpha + pv, acc_s[:, cols])

        pl.when(j < i)(functools.partial(update, False))
        pl.when(j == i)(functools.partial(update, True))

        @pl.when(j == i)
        def _():
            o = (acc_s[...] / jnp.where(top, l_s[0, 0:1, :], l_s[1, 0:1, :])).T
            z = z_ref[...].astype(F32)
            o_ref[...] = o.astype(BF16)
            g_ref[...] = (o * (z * _sigmoid(z))).astype(BF16)
            for hh in range(2):
                lse_ref[hh] = m_s[hh] + jnp.log(l_s[hh, 0:1, :])

    qblk = lambda col: pl.BlockSpec((tq, LANES), lambda h, s, it, jt: (it[s], col + h))
    kblk = lambda col: pl.BlockSpec((tk, LANES), lambda h, s, it, jt: (jt[s], col + h))
    return _pcall(
        body, name="fox_attn_fwd",
        grid_spec=pltpu.PrefetchScalarGridSpec(
            num_scalar_prefetch=2, grid=(FOX_PAIRS, i_tab.shape[0]),
            in_specs=[qblk(0), kblk(cb), kblk(2 * cb), qblk(3 * cb), qblk(0), kblk(0)],
            out_specs=[qblk(0), qblk(0), pl.BlockSpec((2, 1, tq), lambda h, s, it, jt: (h, 0, it[s]))],
            scratch_shapes=[pltpu.VMEM((2, 1, tq), F32), pltpu.VMEM((2, FOX_SUM_ROWS, tq), F32),
                            pltpu.VMEM((LANES, tq), F32)]),
        out_shape=[SDS((t, D_MODEL), BF16), SDS((t, D_MODEL), BF16), SDS((FOX_HEADS, 1, t), F32)],
        compiler_params=_params(dimension_semantics=("arbitrary", "arbitrary")),
    )(i_tab, j_tab, proj, proj, proj, proj, qaug, kaug)


def _fox_bwd(proj, do, qaug, kaug, doaug):
    t = proj.shape[0]
    tq = tk = min(FOX_TQ, t)
    nq = t // tq
    cb = D_MODEL // LANES

    i_tab, j_tab = _causal_steps(nq, key_major=True)

    def body(i_ref, j_ref, q_ref, k_ref, v_ref, do_ref, qa_ref, ka_ref, da_ref,
             dq_ref, dk_ref, dv_ref, dck_ref, dcq_ref, dq_acc, dcq_acc, dk_acc, dks_acc, dv_acc):
        step = pl.program_id(1)
        i, j = i_ref[step], j_ref[step]
        low = _low_lanes((tq, LANES))
        top = _top_rows((LANES, tq))

        @pl.when(i == j)
        def _():
            dk_acc[...] = jnp.zeros_like(dk_acc)
            dks_acc[...] = jnp.zeros_like(dks_acc)
            dv_acc[...] = jnp.zeros_like(dv_acc)

        def update(masked):
            qs = q_ref[...] * FOX_SCALE
            k, v, dout = k_ref[...], v_ref[...], do_ref[...]
            qa, ka, da = qa_ref[...], ka_ref[...], da_ref[...]
            lane = lax.broadcasted_iota(jnp.int32, (tk, LANES), 1)
            vone = jnp.where((lane & (FOX_HEAD_DIM - 1)) < FOX_AUG, 1.0, 0.0).astype(v.dtype)
            one = jnp.ones_like(k)
            if masked:
                causal = (lax.broadcasted_iota(jnp.int32, (tk, tq), 0) <= lax.broadcasted_iota(jnp.int32, (tk, tq), 1))
            parts = []
            scores = [_dot(_pair_operand(low, k, ka, hh), _pair_operand(low, qs, qa, hh), NT) for hh in range(2)]
            dps = [_dot(_pair_operand(low, v, vone, hh), _pair_operand(low, dout, da, hh), NT) for hh in range(2)]
            for hh in range(2):
                s = scores[hh]
                if masked:
                    s = jnp.where(causal, s, NEG_INF)
                p = jnp.exp(s)
                ds = p * dps[hh]
                pb = p.astype(BF16)
                dsb = ds.astype(BF16)
                parts.append((_dot(pb, dout, NN),
                              _dot(dsb, _pair_operand(low, qs, one, hh), NN),
                              _dot(_pair_operand(low, k, one, hh), dsb, TN)))
            dv_acc[...] += jnp.where(low, parts[0][0], parts[1][0])
            dk_acc[...] += jnp.where(low, parts[0][1], parts[1][1])
            dks_acc[...] += jnp.where(low, parts[1][1], parts[0][1])
            dq_t = jnp.where(top, parts[0][2], parts[1][2]) * FOX_SCALE
            sum_a = parts[0][2][FOX_HEAD_DIM:FOX_HEAD_DIM + FOX_SUM_ROWS, :]
            sum_b = parts[1][2][0:FOX_SUM_ROWS, :]

            @pl.when(j == 0)
            def _():
                dq_acc[i] = dq_t
                dcq_acc[0, i] = sum_a
                dcq_acc[1, i] = sum_b

            @pl.when(j > 0)
            def _():
                dq_acc[i] += dq_t
                dcq_acc[0, i] += sum_a
                dcq_acc[1, i] += sum_b

        pl.when(i > j)(functools.partial(update, False))
        pl.when(i == j)(functools.partial(update, True))

        @pl.when(i == nq - 1)
        def _():
            dk_ref[...] = dk_acc[...].astype(BF16)
            dv_ref[...] = dv_acc[...].astype(BF16)
            dck_ref[...] = dks_acc[...]

        @pl.when((i == nq - 1) & (j == nq - 1))
        def _():
            for blk in range(nq):
                dq_ref[blk * tq:(blk + 1) * tq, :] = dq_acc[blk].T.astype(BF16)
            dcq_ref[...] = dcq_acc[...]

    qblk = lambda col: pl.BlockSpec((tq, LANES), lambda h, s, it, jt: (it[s], col + h))
    kblk = lambda col: pl.BlockSpec((tk, LANES), lambda h, s, it, jt: (jt[s], col + h))
    return _pcall(
        body, name="fox_attn_bwd",
        grid_spec=pltpu.PrefetchScalarGridSpec(
            num_scalar_prefetch=2, grid=(FOX_PAIRS, i_tab.shape[0]),
            in_specs=[qblk(0), kblk(cb), kblk(2 * cb), qblk(0), qblk(0), kblk(0), qblk(0)],
            out_specs=[pl.BlockSpec((t, LANES), lambda h, s, it, jt: (0, h)), kblk(0), kblk(0), kblk(0),
                       pl.BlockSpec((2, nq, FOX_SUM_ROWS, tq), lambda h, s, it, jt: (h, 0, 0, 0))],
            scratch_shapes=[pltpu.VMEM((nq, LANES, tq), F32), pltpu.VMEM((2, nq, FOX_SUM_ROWS, tq), F32),
                            pltpu.VMEM((tk, LANES), F32), pltpu.VMEM((tk, LANES), F32), pltpu.VMEM((tk, LANES), F32)]),
        out_shape=[SDS((t, D_MODEL), BF16), SDS((t, D_MODEL), BF16), SDS((t, D_MODEL), BF16),
                   SDS((t, D_MODEL), F32), SDS((FOX_HEADS, nq, FOX_SUM_ROWS, tq), F32)],
        compiler_params=_params(dimension_semantics=("arbitrary", "arbitrary")),
    )(i_tab, j_tab, proj, proj, proj, do, qaug, kaug, doaug)


def _mm_gate_bwd(dh, w_out, z_src, z_col0, o, heads, name):
    t = dh.shape[0]
    tm = 512
    row = lambda i, j, k: (i, 0)
    zcb = z_col0 // D_MODEL

    def epilogue(acc, e_refs, o_refs, i):
        z = e_refs[0][...].astype(F32)
        ov = e_refs[1][...].astype(F32)
        sg = _sigmoid(z)
        dout = acc * (z * sg)
        o_refs[0][...] = dout.astype(BF16)
        o_refs[1][...] = (acc * ov * (sg * (1.0 + z * (1.0 - sg)))).astype(BF16)
        prod = dout * ov
        for cbk in range(D_MODEL // LANES):
            seg = prod[:, cbk * LANES:(cbk + 1) * LANES]
            tot = jnp.sum(seg, axis=-1, keepdims=True)
            if heads == D_MODEL // LANES:
                o_refs[2][cbk] = tot
            else:
                lo = jnp.sum(jnp.where(_low_lanes(seg.shape), seg, 0.0), axis=-1, keepdims=True)
                o_refs[2][2 * cbk] = lo
                o_refs[2][2 * cbk + 1] = tot - lo

    return _mm([dh], w_out, "nt", name, tiles=(tm, D_MODEL, D_MODEL),
               extras=[(z_src, (tm, D_MODEL), lambda i, j, k: (i, zcb)), (o, (tm, D_MODEL), row)],
               outs=[(SDS((t, D_MODEL), BF16), (tm, D_MODEL), row), (SDS((t, D_MODEL), BF16), (tm, D_MODEL), row),
                     (SDS((heads, t, 1), F32), (heads, tm, 1), lambda i, j, k: (0, i, 0))],
               epilogue=epilogue)


def _ple_fwd(h, pin, w_up, w_gate, name):
    t = h.shape[0]
    tm = 512
    pd = pin.shape[1]

    def body(h_ref, p_ref, wu_ref, wg_ref, hn_ref, u_ref, a_ref):
        h = h_ref[...]
        u = _dot(p_ref[...].astype(BF16), wu_ref[...], NN)
        a = _dot(h.astype(BF16), wg_ref[...], NN)
        hn_ref[...] = h + u * _sigmoid(a)
        u_ref[...] = u.astype(BF16)
        a_ref[...] = a.astype(BF16)

    rows = pl.BlockSpec((tm, D_MODEL), lambda i: (i, 0))
    return _pcall(
        body, name=name, grid=(t // tm,),
        in_specs=[rows, pl.BlockSpec((tm, pd), lambda i: (i, 0)),
                  pl.BlockSpec((pd, D_MODEL), lambda i: (0, 0)), pl.BlockSpec((D_MODEL, D_MODEL), lambda i: (0, 0))],
        out_specs=[rows, rows, rows],
        out_shape=[SDS((t, D_MODEL), F32), SDS((t, D_MODEL), BF16), SDS((t, D_MODEL), BF16)],
        compiler_params=_params(),
    )(h, pin, w_up, w_gate)


def _ple_bwd_elem(dh, u, a, name):
    t = dh.shape[0]
    tm = 512

    def body(dh_ref, u_ref, a_ref, du_ref, da_ref):
        g = dh_ref[...]
        s = _sigmoid(a_ref[...].astype(F32))
        du_ref[...] = (g * s).astype(BF16)
        da_ref[...] = (g * u_ref[...].astype(F32) * (s * (1.0 - s))).astype(BF16)

    blk = pl.BlockSpec((tm, D_MODEL), lambda i: (i, 0))
    return _pcall(
        body, name=name, grid=(t // tm,), in_specs=[blk, blk, blk], out_specs=[blk, blk],
        out_shape=[SDS((t, D_MODEL), BF16), SDS((t, D_MODEL), BF16)], compiler_params=_params(),
    )(dh, u, a)


DIL_SCALE = LANES ** -0.5


def _dil_masks():
    ii = lax.broadcasted_iota(jnp.int32, (DIL_BLOCK, DIL_BLOCK), 0)
    jj = lax.broadcasted_iota(jnp.int32, (DIL_BLOCK, DIL_BLOCK), 1)
    return ii, jj


DIL_UNITS = 16
BNT = (((2,), (2,)), ((0,), (0,)))
BNN = (((2,), (1,)), ((0,), (0,)))
BTN = (((1,), (1,)), ((0,), (0,)))


def _dil_units(dil):
    return [(b, r) for b in range(DIL_UNITS // dil) for r in range(dil)]


def _unit_rows(b, r, dil):
    return pl.ds(b * DIL_BLOCK * dil + r, DIL_BLOCK, stride=dil)


def _gather_units(cur, dil, shift=0, edge=None, lead=()):
    nbk = DIL_UNITS // dil
    parts = []
    for b, r in _dil_units(dil):
        bb = b + shift
        if 0 <= bb < nbk:
            parts.append(cur[lead + (_unit_rows(bb, r, dil), slice(None))])
        else:
            parts.append(edge[lead + (pl.ds(r, DIL_BLOCK, stride=dil), slice(None))])
    return jnp.stack(parts)


def _scatter_units(dst, val, dil):
    for u, (b, r) in enumerate(_dil_units(dil)):
        dst[_unit_rows(b, r, dil), :] = val[u]


def _dil_bias(slope, prev):
    ii, jj = _dil_masks()
    dist = (DIL_BLOCK + ii - jj) if prev else (ii - jj)
    return (slope * dist.astype(F32))[None], ((jj >= ii) if prev else (jj <= ii))[None]


def _dil_fwd(proj, slopes, grp, dil, name):
    t = proj.shape[0]
    rows = DIL_BLOCK * DIL_UNITS
    edge_rows = DIL_BLOCK * dil
    nbk = DIL_UNITS // dil
    nsb = t // rows
    qc, kc_, vc_ = grp * DIL_HEADS, 3 * DIL_HEADS + grp * DIL_HEADS, 6 * DIL_HEADS + grp * DIL_HEADS

    def body(q_ref, kp_ref, kc_ref, vp_ref, vc_ref, sl_ref, o_ref, lse_ref, qf, kpf, kcf, vpf, vcf, of, lf):
        m = pl.program_id(1)
        for src, dst in ((q_ref, qf), (kp_ref, kpf), (kc_ref, kcf), (vp_ref, vpf), (vc_ref, vcf)):
            dst[...] = src[...].astype(F32)
        slope = sl_ref[0]
        unit = lax.broadcasted_iota(jnp.int32, (DIL_UNITS, 1, 1), 0)
        has_prev = (unit >= dil) | (m > 0)
        q = _gather_units(qf, dil).astype(BF16)
        kc, vc = _gather_units(kcf, dil).astype(BF16), _gather_units(vcf, dil).astype(BF16)
        kp, vp = _gather_units(kcf, dil, -1, kpf).astype(BF16), _gather_units(vcf, dil, -1, vpf).astype(BF16)
        bias_p, ok_p = _dil_bias(slope, True)
        bias_c, ok_c = _dil_bias(slope, False)
        sp = jnp.where(ok_p & has_prev, _dot(q, kp, BNT) * DIL_SCALE - bias_p, NEG_INF)
        sc = jnp.where(ok_c, _dot(q, kc, BNT) * DIL_SCALE - bias_c, NEG_INF)
        mx = jnp.maximum(jnp.max(sp, axis=-1, keepdims=True), jnp.max(sc, axis=-1, keepdims=True))
        pp = jnp.exp(sp - mx)
        pc = jnp.exp(sc - mx)
        l = jnp.sum(pp, axis=-1, keepdims=True) + jnp.sum(pc, axis=-1, keepdims=True)
        o = (_dot(pp.astype(BF16), vp, BNN) + _dot(pc.astype(BF16), vc, BNN)) / l
        _scatter_units(of, o, dil)
        _scatter_units(lf, mx + jnp.log(l), dil)
        o_ref[...] = of[...].astype(BF16)
        lse_ref[0] = lf[...]

    cur = lambda col: pl.BlockSpec((rows, LANES), lambda h, m: (m, col + h))
    prev = lambda col: pl.BlockSpec((edge_rows, LANES), lambda h, m: (jnp.maximum(m * nbk - 1, 0), col + h))
    return _pcall(
        body, name=name, grid=(DIL_HEADS, nsb),
        in_specs=[cur(qc), prev(kc_), cur(kc_), prev(vc_), cur(vc_), pl.BlockSpec((1, 1, 1), lambda h, m: (h, 0, 0))],
        out_specs=[pl.BlockSpec((rows, LANES), lambda h, m: (m, h)), pl.BlockSpec((1, rows, 1), lambda h, m: (h, m, 0))],
        out_shape=[SDS((t, D_MODEL), BF16), SDS((DIL_HEADS, t, 1), F32)],
        scratch_shapes=[pltpu.VMEM((rows, LANES), F32), pltpu.VMEM((edge_rows, LANES), F32), pltpu.VMEM((rows, LANES), F32),
                        pltpu.VMEM((edge_rows, LANES), F32), pltpu.VMEM((rows, LANES), F32), pltpu.VMEM((rows, LANES), F32),
                        pltpu.VMEM((rows, 1), F32)],
        compiler_params=_params(),
    )(proj, proj, proj, proj, proj, slopes)


def _dil_mix(outs, lses, proj, z_col0):
    t = proj.shape[0]
    tm = 512
    zcb = z_col0 // LANES
    ng = len(outs)

    def body(*refs):
        o_refs, l_refs, z_ref = refs[:ng], refs[ng:2 * ng], refs[2 * ng]
        om_ref, g_ref, lse_ref = refs[2 * ng + 1:]
        ls = [r[0] for r in l_refs]
        mx = functools.reduce(jnp.maximum, ls)
        es = [jnp.exp(l - mx) for l in ls]
        tot = functools.reduce(jnp.add, es)
        o = functools.reduce(jnp.add, [(e / tot) * r[...].astype(F32) for e, r in zip(es, o_refs)])
        z = z_ref[...].astype(F32)
        om_ref[...] = o.astype(BF16)
        g_ref[...] = (o * (z * _sigmoid(z))).astype(BF16)
        lse_ref[0] = mx + jnp.log(tot)

    tile = pl.BlockSpec((tm, LANES), lambda i, h: (i, h))
    col = pl.BlockSpec((1, tm, 1), lambda i, h: (h, i, 0))
    return _pcall(
        body, name="dil_mix", grid=(t // tm, DIL_HEADS),
        in_specs=[tile] * ng + [col] * ng + [pl.BlockSpec((tm, LANES), lambda i, h: (i, zcb + h))],
        out_specs=[tile, tile, col],
        out_shape=[SDS((t, D_MODEL), BF16), SDS((t, D_MODEL), BF16), SDS((DIL_HEADS, t, 1), F32)],
        compiler_params=_params(),
    )(*outs, *lses, proj)


def _dil_bwd(proj, do, lse, delta, slopes, grp, dil, name):
    t = proj.shape[0]
    rows = DIL_BLOCK * DIL_UNITS
    edge_rows = DIL_BLOCK * dil
    nbk = DIL_UNITS // dil
    nsb = t // rows
    last_edge = t // edge_rows - 1
    qc, kc_, vc_ = grp * DIL_HEADS, 3 * DIL_HEADS + grp * DIL_HEADS, 6 * DIL_HEADS + grp * DIL_HEADS

    def body(q_ref, qn_ref, kp_ref, kc_ref, vp_ref, vc_ref, do_ref, don_ref, l_ref, ln_ref, d_ref, dn_ref, sl_ref,
             dqkv_ref, qf, qnf, kpf, kcf, vpf, vcf, dof, donf, dqf, dkf, dvf):
        m = pl.program_id(1)
        for src, dst in ((q_ref, qf), (qn_ref, qnf), (kp_ref, kpf), (kc_ref, kcf), (vp_ref, vpf), (vc_ref, vcf),
                         (do_ref, dof), (don_ref, donf)):
            dst[...] = src[...].astype(F32)
        slope = sl_ref[0]
        unit = lax.broadcasted_iota(jnp.int32, (DIL_UNITS, 1, 1), 0)
        has_prev = (unit >= dil) | (m > 0)
        has_next = (unit < DIL_UNITS - dil) | (m < nsb - 1)
        b16 = lambda x: x.astype(BF16)
        q, kc, vc, dout = (b16(_gather_units(x, dil)) for x in (qf, kcf, vcf, dof))
        kp, vp = b16(_gather_units(kcf, dil, -1, kpf)), b16(_gather_units(vcf, dil, -1, vpf))
        qn, don = b16(_gather_units(qf, dil, 1, qnf)), b16(_gather_units(dof, dil, 1, donf))
        lrow, drow = _gather_units(l_ref, dil, lead=(0,)), _gather_units(d_ref, dil, lead=(0,))
        lnrow = _gather_units(l_ref, dil, 1, ln_ref, lead=(0,))
        dnrow = _gather_units(d_ref, dil, 1, dn_ref, lead=(0,))
        bias_p, ok_p = _dil_bias(slope, True)
        bias_c, ok_c = _dil_bias(slope, False)
        sp = jnp.where(ok_p & has_prev, _dot(q, kp, BNT) * DIL_SCALE - bias_p, NEG_INF)
        sc = jnp.where(ok_c, _dot(q, kc, BNT) * DIL_SCALE - bias_c, NEG_INF)
        pp = jnp.exp(sp - lrow)
        pc = jnp.exp(sc - lrow)
        dsp = b16(pp * (_dot(dout, vp, BNT) - drow))
        dsc = b16(pc * (_dot(dout, vc, BNT) - drow))
        _scatter_units(dqf, (_dot(dsp, kp, BNN) + _dot(dsc, kc, BNN)) * DIL_SCALE, dil)
        sn = jnp.where(ok_p & has_next, _dot(qn, kc, BNT) * DIL_SCALE - bias_p, NEG_INF)
        pn = jnp.exp(sn - lnrow)
        dsn = b16(pn * (_dot(don, vc, BNT) - dnrow))
        _scatter_units(dkf, (_dot(dsc, q, BTN) + _dot(dsn, qn, BTN)) * DIL_SCALE, dil)
        _scatter_units(dvf, _dot(b16(pc), dout, BTN) + _dot(b16(pn), don, BTN), dil)
        for s, src in enumerate((dqf, dkf, dvf)):
            dqkv_ref[s] = src[...].astype(BF16)

    prev_i = lambda m: jnp.maximum(m * nbk - 1, 0)
    next_i = lambda m: jnp.minimum((m + 1) * nbk, last_edge)
    cur = lambda col: pl.BlockSpec((rows, LANES), lambda h, m: (m, col + h))
    edge = lambda col, f: pl.BlockSpec((edge_rows, LANES), lambda h, m: (f(m), col + h))
    colcur = pl.BlockSpec((1, rows, 1), lambda h, m: (h, m, 0))
    colnext = pl.BlockSpec((1, edge_rows, 1), lambda h, m: (h, next_i(m), 0))
    out_blk = pl.BlockSpec((3, rows, LANES), lambda h, m: (0, m, h))
    big, small = pltpu.VMEM((rows, LANES), F32), pltpu.VMEM((edge_rows, LANES), F32)
    return _pcall(
        body, name=name, grid=(DIL_HEADS, nsb),
        in_specs=[cur(qc), edge(qc, next_i), edge(kc_, prev_i), cur(kc_), edge(vc_, prev_i), cur(vc_),
                  cur(0), edge(0, next_i), colcur, colnext, colcur, colnext,
                  pl.BlockSpec((1, 1, 1), lambda h, m: (h, 0, 0))],
        out_specs=out_blk, out_shape=SDS((3, t, D_MODEL), BF16),
        scratch_shapes=[big, small, small, big, small, big, big, small, big, big, big],
        compiler_params=_params(),
    )(proj, proj, proj, proj, proj, proj, do, do, lse, lse, delta, delta, slopes)


def _mesh_pos():
    x, y, c = lax.axis_index("x"), lax.axis_index("y"), lax.axis_index("c")
    return x, y, c


def _peer(pos, k):
    x, y, c = pos
    px = 1 - x if k & 4 else x
    py = 1 - y if k & 2 else y
    pc = 1 - c if k & 1 else c
    return (px, py, pc), 4 * px + 2 * py + pc


N_CHIPS = 4
CHIP_FLIPS = ((1, 0), (0, 1), (1, 1))


def _other_chips(x, y):
    return [(1 - x if fx else x, 1 - y if fy else y) for fx, fy in CHIP_FLIPS]


def _all_gather(arrays):
    n = len(arrays)
    per = 2 * N_CHIPS - 1
    hbm = pl.BlockSpec(memory_space=pltpu.HBM)

    def body(*refs):
        ins, outs = refs[:n], refs[n:2 * n]
        send_sems, recv_sems, local_sems = refs[2 * n:]
        x, y, c = _mesh_pos()
        sibling = (x, y, 1 - c)
        chips = _other_chips(x, y)
        block = lambda px, py, pc: 4 * px + 2 * py + pc

        def copy(w, k, src, blk, to):
            return pltpu.make_async_remote_copy(
                src_ref=src, dst_ref=outs[w].at[blk], send_sem=send_sems.at[w * per + k],
                recv_sem=recv_sems.at[w * per + k], device_id=to, device_id_type=MESH)

        local, started = [], []
        for w in range(n):
            cp = pltpu.make_async_copy(ins[w], outs[w].at[block(x, y, c)], local_sems.at[w])
            cp.start()
            local.append(cp)
            started.append(copy(w, 0, ins[w], block(x, y, c), sibling))
            for j, (px, py) in enumerate(chips):
                started.append(copy(w, 1 + j, ins[w], block(x, y, c), (px, py, c)))
        for cp in started:
            cp.start()
        for j, (px, py) in enumerate(chips):
            for w in range(n):
                copy(w, 1 + j, ins[w], block(px, py, c), sibling).wait_recv()
                cp = copy(w, 4 + j, outs[w].at[block(px, py, c)], block(px, py, c), sibling)
                cp.start()
                started.append(cp)
        for w in range(n):
            copy(w, 0, ins[w], block(x, y, 1 - c), sibling).wait_recv()
            for j, (px, py) in enumerate(chips):
                copy(w, 4 + j, ins[w], block(px, py, 1 - c), sibling).wait_recv()
        for cp in started:
            cp.wait_send()
        for cp in local:
            cp.wait()

    return _pcall(
        body, name="all_gather_weights", in_specs=[hbm] * n, out_specs=[hbm] * n,
        out_shape=[SDS((N_DEV,) + a.shape, a.dtype) for a in arrays],
        scratch_shapes=[pltpu.SemaphoreType.DMA((n * per,)), pltpu.SemaphoreType.DMA((n * per,)),
                        pltpu.SemaphoreType.DMA((n,))],
    )(*arrays)


def _core_exchange(slabs):
    n = len(slabs)
    hbm = pl.BlockSpec(memory_space=pltpu.HBM)

    def body(*refs):
        ins, outs = refs[:n], refs[n:2 * n]
        send_sems, recv_sems = refs[2 * n:]
        x, y, c = _mesh_pos()
        copies = [pltpu.make_async_remote_copy(
            src_ref=ins[w].at[pl.ds(0, N_CHIPS), 1 - c], dst_ref=outs[w], send_sem=send_sems.at[w],
            recv_sem=recv_sems.at[w], device_id=(x, y, 1 - c), device_id_type=MESH) for w in range(n)]
        for cp in copies:
            cp.start()
        for cp in copies:
            cp.wait_recv()
        for cp in copies:
            cp.wait_send()

    return _pcall(
        body, name="grads_core_exchange", in_specs=[hbm] * n, out_specs=[hbm] * n,
        out_shape=[SDS((N_CHIPS,) + a.shape[2:], a.dtype) for a in slabs],
        scratch_shapes=[pltpu.SemaphoreType.DMA((n,)), pltpu.SemaphoreType.DMA((n,))],
    )(*slabs)


def _chip_exchange(partials):
    n = len(partials)
    per = N_CHIPS - 1
    hbm = pl.BlockSpec(memory_space=pltpu.HBM)

    def body(*refs):
        ins, outs = refs[:n], refs[n:2 * n]
        send_sems, recv_sems, local_sems = refs[2 * n:]
        x, y, c = _mesh_pos()
        mine = 2 * x + y
        local, sends, recvs = [], [], []
        for w in range(n):
            cp = pltpu.make_async_copy(ins[w].at[mine], outs[w].at[mine], local_sems.at[w])
            cp.start()
            local.append(cp)
            for j, (px, py) in enumerate(_other_chips(x, y)):
                theirs = 2 * px + py
                sems = dict(send_sem=send_sems.at[w * per + j], recv_sem=recv_sems.at[w * per + j],
                            device_id=(px, py, c), device_id_type=MESH)
                cp = pltpu.make_async_remote_copy(src_ref=ins[w].at[theirs], dst_ref=outs[w].at[mine], **sems)
                cp.start()
                sends.append(cp)
                recvs.append(pltpu.make_async_remote_copy(src_ref=ins[w].at[theirs], dst_ref=outs[w].at[theirs], **sems))
        for cp in recvs:
            cp.wait_recv()
        for cp in sends:
            cp.wait_send()
        for cp in local:
            cp.wait()

    return _pcall(
        body, name="grads_chip_exchange", in_specs=[hbm] * n, out_specs=[hbm] * n,
        out_shape=[SDS(a.shape, a.dtype) for a in partials],
        scratch_shapes=[pltpu.SemaphoreType.DMA((n * per,)), pltpu.SemaphoreType.DMA((n * per,)),
                        pltpu.SemaphoreType.DMA((n,))],
    )(*partials)


def _core_sum(slabs, from_sibling, core, name):
    _, _, r, c = slabs.shape
    tr = min(r, 256)

    def body(core_ref, a_ref, b_ref, o_ref):
        o_ref[...] = (a_ref[0].astype(F32) + b_ref[...].astype(F32)).astype(BF16)

    return _pcall(
        body, name=name,
        grid_spec=pltpu.PrefetchScalarGridSpec(
            num_scalar_prefetch=1, grid=(N_CHIPS, r // tr),
            in_specs=[pl.BlockSpec((1, 1, tr, c), lambda g, i, core: (g, core[0], i, 0)),
                      pl.BlockSpec((1, tr, c), lambda g, i, core: (g, i, 0))],
            out_specs=pl.BlockSpec((1, tr, c), lambda g, i, core: (g, i, 0))),
        out_shape=SDS((N_CHIPS, r, c), BF16), compiler_params=_params(),
    )(core, slabs, from_sibling)


HBM_SPEC = pl.BlockSpec(memory_space=pltpu.HBM)
SEM_SPEC = pl.BlockSpec(memory_space=pltpu.SEMAPHORE)
DATAFLOW = pltpu.SideEffectType.DATAFLOW_SIDE_EFFECTING


def _push_start(arrays, scatter, name):
    n = len(arrays)
    per = N_DEV - 1

    def body(*refs):
        srcs, lands = refs[:n], refs[n:2 * n]
        send_sems, recv_sems, token = refs[2 * n], refs[2 * n + 1], refs[-1]
        pos = _mesh_pos()
        me = 4 * pos[0] + 2 * pos[1] + pos[2]
        for w in range(n):
            for k in range(1, N_DEV):
                peer, peer_idx = _peer(pos, k)
                pltpu.make_async_remote_copy(
                    src_ref=srcs[w].at[peer_idx] if scatter else srcs[w], dst_ref=lands[w].at[me],
                    send_sem=send_sems.at[w * per + k - 1], recv_sem=recv_sems.at[w * per + k - 1],
                    device_id=peer, device_id_type=MESH).start()
        token[...] = jnp.zeros_like(token)

    land_shapes = [a.shape if scatter else (N_DEV,) + a.shape for a in arrays]
    in_hbm = lambda a: pltpu.with_memory_space_constraint(a, pltpu.HBM)
    lands = [in_hbm(lax.empty(s, a.dtype)) for s, a in zip(land_shapes, arrays)]
    sems = pltpu.SemaphoreType.DMA((n * per,))
    res = _pcall(
        body, name=name,
        out_shape=(sems, sems, *[pltpu.HBM(a.shape, a.dtype) for a in arrays],
                   *[pltpu.HBM(s, a.dtype) for s, a in zip(land_shapes, arrays)], SDS((8, LANES), F32)),
        in_specs=[HBM_SPEC] * (2 * n),
        out_specs=(SEM_SPEC, SEM_SPEC, *[HBM_SPEC] * (2 * n), pl.BlockSpec(memory_space=pltpu.VMEM)),
        input_output_aliases={i: 2 + i for i in range(2 * n)},
        compiler_params=pltpu.CompilerParams(has_side_effects=DATAFLOW),
    )(*[in_hbm(a) for a in arrays], *lands)
    return res[0], res[1], list(res[2:2 + n]), list(res[2 + n:2 + 2 * n]), res[-1]


def _push_wait(send_sems, recv_sems, arrays, lands, after, scatter, name):
    n = len(arrays)
    per = N_DEV - 1

    def body(*refs):
        srcs, lands_ = refs[:n], refs[n:2 * n]
        send_sems_, recv_sems_ = refs[2 * n], refs[2 * n + 1]
        pos = _mesh_pos()
        for w in range(n):
            for k in range(1, N_DEV):
                peer, peer_idx = _peer(pos, k)
                cp = pltpu.make_async_remote_copy(
                    src_ref=srcs[w].at[peer_idx] if scatter else srcs[w], dst_ref=lands_[w].at[peer_idx],
                    send_sem=send_sems_.at[w * per + k - 1], recv_sem=recv_sems_.at[w * per + k - 1],
                    device_id=peer, device_id_type=MESH)
                cp.wait_send()
                cp.wait_recv()

    res = _pcall(
        body, name=name,
        out_shape=(*[pltpu.HBM(a.shape, a.dtype) for a in arrays], *[pltpu.HBM(l.shape, l.dtype) for l in lands]),
        in_specs=[HBM_SPEC] * (2 * n) + [SEM_SPEC, SEM_SPEC, pl.BlockSpec(memory_space=pl.ANY)],
        out_specs=[HBM_SPEC] * (2 * n), input_output_aliases={i: i for i in range(2 * n)},
        compiler_params=pltpu.CompilerParams(has_side_effects=DATAFLOW),
    )(*arrays, *lands, send_sems, recv_sems, after)
    return list(res[n:])


def _fill_own(land, own):
    me = 4 * lax.axis_index("x") + 2 * lax.axis_index("y") + lax.axis_index("c")
    return lax.dynamic_update_slice(land, own[None], (me,) + (0,) * own.ndim)


def _adam_math(w, g, m, v):
    m = ADAM_B1 * m + (1.0 - ADAM_B1) * g
    v = ADAM_B2 * v + (1.0 - ADAM_B2) * (g * g)
    m_hat = m / (1.0 - ADAM_B1 ** ADAM_STEP)
    v_hat = v / (1.0 - ADAM_B2 ** ADAM_STEP)
    delta = -ADAM_LR * (m_hat / (jnp.sqrt(v_hat) + ADAM_EPS) + ADAM_WD * w)
    return delta, m, v


def _adamw(recv, w, m, v, name):
    r, c = w.shape
    tr = min(r, 128)
    n_parts = recv.shape[0]

    def body(g_ref, w_ref, m_ref, v_ref, go_ref, d_ref, mo_ref, vo_ref):
        g = g_ref[0].astype(F32)
        for s in range(1, n_parts):
            g = g + g_ref[s].astype(F32)
        delta, mn, vn = _adam_math(w_ref[...], g, m_ref[...], v_ref[...])
        go_ref[...] = g
        d_ref[...] = delta
        mo_ref[...] = mn
        vo_ref[...] = vn

    blk = pl.BlockSpec((tr, c), lambda i: (i, 0))
    return _pcall(
        body, name=name, grid=(r // tr,),
        in_specs=[pl.BlockSpec((n_parts, tr, c), lambda i: (0, i, 0)), blk, blk, blk],
        out_specs=[blk] * 4, out_shape=[SDS((r, c), F32)] * 4, compiler_params=_params(),
    )(recv, w, m, v)


VEC_ROWS = 32


def _small_allreduce_adamw(vec, w, m, v):
    def body(vec_ref, w_ref, m_ref, v_ref, g_ref, d_ref, mo_ref, vo_ref, gath, send_sems, recv_sems):
        pos = _mesh_pos()
        me = 4 * pos[0] + 2 * pos[1] + pos[2]
        sends, recvs = [], []
        for k in range(1, N_DEV):
            peer, peer_idx = _peer(pos, k)
            cp = pltpu.make_async_remote_copy(src_ref=vec_ref, dst_ref=gath.at[me], send_sem=send_sems.at[k - 1],
                                              recv_sem=recv_sems.at[k - 1], device_id=peer, device_id_type=MESH)
            cp.start()
            sends.append(cp)
            recvs.append(pltpu.make_async_remote_copy(src_ref=vec_ref, dst_ref=gath.at[peer_idx],
                                                      send_sem=send_sems.at[k - 1], recv_sem=recv_sems.at[k - 1],
                                                      device_id=peer, device_id_type=MESH))
        gath[me] = vec_ref[...]
        for cp in recvs:
            cp.wait_recv()
        for cp in sends:
            cp.wait_send()
        tot = gath[0]
        for s in range(1, N_DEV):
            tot = tot + gath[s]
        rowi = lax.broadcasted_iota(jnp.int32, (8, LANES), 0)
        mine = jnp.sum(jnp.where(rowi == me, tot[16:24, :], 0.0), axis=0, keepdims=True)
        g = jnp.concatenate([tot[0:16, :], jnp.broadcast_to(mine, (8, LANES)), tot[24:32, :]], axis=0)
        delta, mn, vn = _adam_math(w_ref[...], g, m_ref[...], v_ref[...])
        g_ref[...] = g
        d_ref[...] = delta
        mo_ref[...] = mn
        vo_ref[...] = vn

    vm = pl.BlockSpec(memory_space=pltpu.VMEM)
    return _pcall(
        body, name="small_allreduce_adamw", in_specs=[vm] * 4, out_specs=[vm] * 4,
        out_shape=[SDS((VEC_ROWS, LANES), F32)] * 4,
        scratch_shapes=[pltpu.VMEM((N_DEV, VEC_ROWS, LANES), F32), pltpu.SemaphoreType.DMA((N_DEV - 1,)),
                        pltpu.SemaphoreType.DMA((N_DEV - 1,))],
        compiler_params=pltpu.CompilerParams(has_side_effects=True),
    )(vec, w, m, v)


def _cols_to_slabs(a):
    r, c8 = a.shape
    return a.reshape(r, N_DEV, c8 // N_DEV).transpose(1, 0, 2)


def _slabs_to_cols(a):
    n, r, c = a.shape
    return a.transpose(1, 0, 2).reshape(r, n * c)


def _rows8(vec):
    return vec.reshape(-1, LANES)


def _pad_rows(a, rows):
    return jnp.pad(a, ((0, rows - a.shape[0]), (0, LANES - a.shape[1])))


def kernel(x, p, fox_norm, fox_w_in, fox_b_f, fox_w_out, dil_norm, dil_w_in, dil_w_out, ple_w_up, ple_w_gate, final_norm, loss_target, m_fox_norm, m_fox_w_in, m_fox_b_f, m_fox_w_out, m_dil_norm, m_dil_w_in, m_dil_w_out, m_ple_w_up, m_ple_w_gate, m_final_norm, v_fox_norm, v_fox_w_in, v_fox_b_f, v_fox_w_out, v_dil_norm, v_dil_w_in, v_dil_w_out, v_ple_w_up, v_ple_w_gate, v_final_norm):
    t = x.shape[1]
    d = D_MODEL
    xs, tgt = x[0], loss_target[0]
    p0, p1 = p[0, 0], p[1, 0]
    fox_cols = fox_w_in.shape[2]
    ple_dim = ple_w_up.shape[1]

    later = [dil_w_in[0].astype(BF16), dil_w_out[0].astype(BF16), ple_w_up.reshape(-1, LANES).astype(BF16),
             ple_w_gate.reshape(-1, d).astype(BF16), dil_norm]
    push = _push_start(later, False, "gather_later_start")
    gw = _all_gather([fox_w_in[0].astype(BF16), fox_w_out[0].astype(BF16)])
    w_fox_in = _slabs_to_cols(gw[0])
    w_fox_main = w_fox_in[:, :4 * d]
    w_fox_f = jnp.pad(w_fox_in[:, 4 * d:], ((0, 0), (0, LANES - FOX_HEADS)))
    w_fox_out = gw[1].reshape(d, d)
    b_pad = jnp.pad(fox_b_f, ((0, 0), (0, LANES - FOX_HEADS)))

    n0, r0 = _rms_fwd(xs, fox_norm + push[4][0:1, 0:1], "rms_fox")
    proj0 = _mm([n0], w_fox_main, "nn", "fox_in_proj", tiles=IN_PROJ_TILES)
    projf = _mm([n0], w_fox_f, "nn", "fox_gate_proj", tiles=IN_PROJ_TILES, out_dtype=F32)
    c_all = _fox_gate_fwd(projf, b_pad)
    qaug_fwd = _fox_aug(c_all, c_all, 1.0, 0.0, 0, FOX_AUG, "fox_aug_q_fwd")
    kaug = _fox_aug(c_all, c_all, -1.0, 0.0, FOX_AUG, 0, "fox_aug_k")
    o0, g0, lse0 = _fox_fwd(proj0, qaug_fwd, kaug)
    h1 = _mm_residual(g0, w_fox_out, "nn", xs, "fox_out_proj")

    landed = _push_wait(push[0], push[1], push[2], push[3], h1, False, "gather_later_wait")
    gl = [_fill_own(zone, own) for zone, own in zip(landed, later)]
    w_dil_in = _slabs_to_cols(gl[0])
    w_dil_out = gl[1].reshape(d, d)
    w_up = gl[2].reshape(N_DEV, 2, ple_dim, LANES).transpose(1, 2, 0, 3).reshape(2, ple_dim, d)
    w_gate = gl[3].reshape(N_DEV, 2, d // N_DEV, d).transpose(1, 0, 2, 3).reshape(2, d, d)
    dil_norm_full = gl[4].reshape(1, d)
    h2, u0, a0 = _ple_fwd(h1, p0, w_up[0], w_gate[0], "ple0_fwd")

    n1, r1 = _rms_fwd(h2, dil_norm_full, "rms_dil")
    proj1 = _mm([n1], w_dil_in, "nn", "dil_in_proj", tiles=IN_PROJ_TILES)
    n_heads = len(DIL_PATTERN) * DIL_HEADS
    slopes = 2.0 ** (-ALIBI_MAX_EXP * jnp.arange(1, n_heads + 1, dtype=F32) / n_heads)
    dil_o, dil_lse, dil_slopes = [], [], []
    for grp, (_, dil) in enumerate(DIL_PATTERN):
        sl = (slopes[grp * DIL_HEADS:(grp + 1) * DIL_HEADS] * dil).reshape(DIL_HEADS, 1, 1)
        og, lg = _dil_fwd(proj1, sl, grp, dil, f"dil_attn_fwd_{grp}")
        dil_o.append(og)
        dil_lse.append(lg)
        dil_slopes.append(sl)
    z1_col0 = 9 * d
    o1, g1, lse1 = _dil_mix(dil_o, dil_lse, proj1, z1_col0)
    h3 = _mm_residual(g1, w_dil_out, "nn", h2, "dil_out_proj")
    h4, u1, a1 = _ple_fwd(h3, p1, w_up[1], w_gate[1], "ple1_fwd")

    dh4, d_final_norm, loss_part = _final_bwd(h4, final_norm.reshape(1, d), tgt)

    du1, da1 = _ple_bwd_elem(dh4, u1, a1, "ple1_bwd_elem")
    dw_up1 = _dw(p1, du1, "ple1_dw_up")
    dw_gate1 = _dw(h3, da1, "ple1_dw_gate")
    dh3 = _mm_residual(da1, w_gate[1], "nt", dh4, "ple1_dh")

    dw_dil_out = _dw(g1, dh3, "dil_dw_out")
    do1, dz1, delta1 = _mm_gate_bwd(dh3, w_dil_out, proj1, z1_col0, o1, DIL_HEADS, "dil_dgate")
    n_grp = len(DIL_PATTERN)
    dqkv = [_dil_bwd(proj1, do1, lse1, delta1, dil_slopes[grp], grp, dil, f"dil_attn_bwd_{grp}")
            for grp, (_, dil) in enumerate(DIL_PATTERN)]
    dw_cols = [_dw(n1, dqkv[grp], f"dil_dw_in_{kind}{grp}", sub=kind) for kind in range(3) for grp in range(n_grp)]
    dw_dil_in = jnp.concatenate(dw_cols + [_dw(n1, dz1, "dil_dw_in_z")], axis=1)
    group_major = lambda kb: jnp.where(kb < 3 * n_grp, (kb % 3) * n_grp + kb // 3, kb)
    dh2, d_dil_norm = _mm_in_bwd(dqkv + [dz1], w_dil_in, h2, dil_norm_full, r1, dh3, "dil_dx", w_kmap=group_major)

    du0, da0 = _ple_bwd_elem(dh2, u0, a0, "ple0_bwd_elem")
    dw_up0 = _dw(p0, du0, "ple0_dw_up")
    dw_gate0 = _dw(h1, da0, "ple0_dw_gate")
    dh1 = _mm_residual(da0, w_gate[0], "nt", dh2, "ple0_dh")

    dw_fox_out = _dw(g0, dh1, "fox_dw_out")
    do0, dz0, delta0 = _mm_gate_bwd(dh1, w_fox_out, proj0, 3 * d, o0, FOX_HEADS, "fox_dgate")
    head_cols = lambda a: jnp.pad(a, ((0, 0), (0, LANES - FOX_HEADS)))
    lse_cols = head_cols(lse0.reshape(FOX_HEADS, t).T)
    delta_cols = head_cols(delta0.reshape(FOX_HEADS, t).T)
    qaug_bwd = _fox_aug(c_all, lse_cols, 1.0, -1.0, 0, FOX_AUG, "fox_aug_q_bwd")
    doaug = _fox_aug(delta_cols, delta_cols, -1.0, 0.0, 0, None, "fox_aug_do")
    dq0, dk0, dv0, dck_wide, dcq = _fox_bwd(proj0, do0, qaug_bwd, kaug, doaug)
    dc_query, dc_key = _fox_unpack_dc(dck_wide, dcq)
    df, d_b_f = _fox_gate_bwd(projf, b_pad, head_cols(dc_query), head_cols(dc_key))
    dproj0 = [dq0, dk0, dv0, dz0]
    dw_fox_parts = [_dw(n0, dpart, f"fox_dw_in_{s}") for s, dpart in enumerate(dproj0)]
    dw_fox_f = _dw(n0, df, "fox_dw_gate")
    dn0_f = _mm([df], w_fox_f, "nt", "fox_dx_gate", out_dtype=F32)
    grad_x, d_fox_norm = _mm_in_bwd(dproj0, w_fox_main, xs, fox_norm, r0, dh1, "fox_dx", more=dn0_f)

    dw_fox_in = jnp.concatenate(dw_fox_parts + [dw_fox_f[:, :FOX_HEADS]], axis=1)
    slabs = [_cols_to_slabs(dw_fox_in), dw_fox_out.reshape(N_DEV, d // N_DEV, d), _cols_to_slabs(dw_dil_in),
             dw_dil_out.reshape(N_DEV, d // N_DEV, d),
             jnp.stack([dw_up0, dw_up1]).reshape(2, ple_dim, N_DEV, LANES).transpose(2, 0, 1, 3).reshape(N_DEV, -1, LANES),
             jnp.stack([dw_gate0, dw_gate1]).reshape(2, N_DEV, d // N_DEV, d).transpose(1, 0, 2, 3).reshape(N_DEV, -1, d)]
    names = ["fox_w_in", "fox_w_out", "dil_w_in", "dil_w_out", "ple_w_up", "ple_w_gate"]
    slabs = [s.reshape((N_CHIPS, 2) + s.shape[1:]) for s in slabs]
    from_sibling = _core_exchange(slabs)
    core = lax.axis_index("c").astype(jnp.int32).reshape(1)
    chip_sums = [_core_sum(s, f, core, "core_sum_" + nm) for s, f, nm in zip(slabs, from_sibling, names)]
    recv = _chip_exchange(chip_sums)
    big = [(fox_w_in, m_fox_w_in, v_fox_w_in), (fox_w_out, m_fox_w_out, v_fox_w_out),
           (dil_w_in, m_dil_w_in, v_dil_w_in), (dil_w_out, m_dil_w_out, v_dil_w_out),
           (ple_w_up, m_ple_w_up, v_ple_w_up), (ple_w_gate, m_ple_w_gate, v_ple_w_gate)]
    upd = {}
    for rv, (w, m, v), nm in zip(recv, big, names):
        shp2 = rv.shape[1:]
        res = _adamw(rv, w.reshape(shp2), m.reshape(shp2), v.reshape(shp2), "adamw_" + nm)
        upd[nm] = [a.reshape(w.shape) for a in res]

    loss_row = jnp.where(jnp.arange(LANES) == 0, loss_part, 0.0)
    vec = jnp.concatenate([_rows8(d_fox_norm), _rows8(d_final_norm), _rows8(d_dil_norm), d_b_f, loss_row,
                           jnp.zeros((VEC_ROWS - 26, LANES), F32)], axis=0)

    def small_pack(a_fox_norm, a_final_norm, a_dil_norm, a_b_f):
        return jnp.concatenate([_rows8(a_fox_norm), _rows8(a_final_norm), _pad_rows(a_dil_norm, 8),
                                _pad_rows(a_b_f, 8)], axis=0)

    sg, sd, sm, sv = _small_allreduce_adamw(
        vec, small_pack(fox_norm, final_norm, dil_norm, fox_b_f),
        small_pack(m_fox_norm, m_final_norm, m_dil_norm, m_fox_b_f),
        small_pack(v_fox_norm, v_final_norm, v_dil_norm, v_fox_b_f))

    def small_unpack(a):
        return {"fox_norm": a[0:8].reshape(1, d), "final_norm": a[8:16].reshape(d), "dil_norm": a[16:17],
                "fox_b_f": a[24:25, :FOX_HEADS]}

    loss = sg[25, 0]
    order = ["fox_norm", "fox_w_in", "fox_b_f", "fox_w_out", "dil_norm", "dil_w_in", "dil_w_out", "ple_w_up",
             "ple_w_gate", "final_norm"]
    out = [loss, grad_x[None]]
    for idx, small in enumerate((sg, sd, sm, sv)):
        sp = small_unpack(small)
        out += [sp[nm] if nm in sp else upd[nm][idx] for nm in order]
    return tuple(out)
```

```python
import functools

import jax
import jax.numpy as jnp
from jax import lax
from jax.experimental import pallas as pl
from jax.experimental.pallas import tpu as pltpu

F32 = jnp.float32
BF16 = jnp.bfloat16
SDS = jax.ShapeDtypeStruct

D_MODEL = 1024
N_DEV = 8
LANES = 128
FOX_HEADS = 16
FOX_HEAD_DIM = 64
FOX_PAIRS = FOX_HEADS // 2
DIL_HEADS = 8
DIL_BLOCK = 128
DIL_PATTERN = ((128, 1), (512, 4), (2048, 16))
ALIBI_MAX_EXP = 8.0
RMS_EPS = 1e-6
ADAM_LR, ADAM_B1, ADAM_B2, ADAM_EPS, ADAM_WD, ADAM_STEP = 0.001, 0.9, 0.999, 1e-08, 0.01, 10
VMEM_LIMIT = 48 * 1024 * 1024
NEG_INF = float("-inf")

NN = (((1,), (0,)), ((), ()))
NT = (((1,), (1,)), ((), ()))
TN = (((0,), (0,)), ((), ()))
MESH = pl.DeviceIdType.MESH


def _pcall(body, **kw):
    return pl.pallas_call(body, **kw)


def _params(**kw):
    return pltpu.CompilerParams(vmem_limit_bytes=VMEM_LIMIT, **kw)


def _dot(a, b, dims):
    return lax.dot_general(a, b, dims, preferred_element_type=F32)


def _sigmoid(x):
    return 1.0 / (1.0 + jnp.exp(-x))


def _mm(a_parts, b, mode, name, tiles=(512, 1024, 1024), extras=(), outs=None, epilogue=None, out_dtype=BF16,
        b_kmap=None, b_sub=None):
    na = len(a_parts)
    stack = [a.shape[0] if a.ndim == 3 else 1 for a in a_parts]
    first = [sum(stack[:s]) for s in range(na)]
    if mode == "tn":
        k_part, m = a_parts[0].shape
        n = b.shape[-1]
    else:
        m, k_part = a_parts[0].shape[-2:]
        n = b.shape[1] if mode == "nn" else b.shape[0]
    tm, tn, tk = min(tiles[0], m), min(tiles[1], n), min(tiles[2], k_part)
    kb = k_part // tk
    nk = sum(stack) * kb
    grid = (m // tm, n // tn, nk)
    b_kmap = b_kmap or (lambda k: k)

    in_specs = []
    for s in range(na):
        if mode == "tn":
            in_specs.append(pl.BlockSpec((tk, tm), lambda i, j, k: (k, i)))
            continue

        def rel(k, s=s):
            return jnp.clip(k - first[s] * kb, 0, stack[s] * kb - 1)

        if a_parts[s].ndim == 3:
            in_specs.append(pl.BlockSpec((None, tm, tk), lambda i, j, k, rel=rel: (rel(k) // kb, i, rel(k) % kb)))
        else:
            in_specs.append(pl.BlockSpec((tm, tk), lambda i, j, k, rel=rel: (i, rel(k))))
    if mode == "nt":
        in_specs.append(pl.BlockSpec((tn, tk), lambda i, j, k: (j, b_kmap(k))))
    elif b_sub is not None:
        in_specs.append(pl.BlockSpec((None, tk, tn), lambda i, j, k: (b_sub, k, j)))
    else:
        in_specs.append(pl.BlockSpec((tk, tn), lambda i, j, k: (b_kmap(k), j)))
    for _, blk, imap in extras:
        in_specs.append(pl.BlockSpec(blk, imap))
    if outs is None:
        outs = [(SDS((m, n), out_dtype), (tm, tn), lambda i, j, k: (i, j))]
    out_specs = [pl.BlockSpec(blk, imap) for _, blk, imap in outs]
    ne, no = len(extras), len(outs)
    dims = {"nn": NN, "nt": NT, "tn": TN}[mode]

    def finish(res, e_refs, o_refs, i):
        if epilogue is None:
            o_refs[0][...] = res.astype(o_refs[0].dtype)
        else:
            epilogue(res, e_refs, o_refs, i)

    def body(*refs):
        a_refs = refs[:na]
        b_ref = refs[na]
        e_refs = refs[na + 1:na + 1 + ne]
        o_refs = refs[na + 1 + ne:na + 1 + ne + no]
        i, k = pl.program_id(0), pl.program_id(2)
        if nk == 1:
            finish(_dot(a_refs[0][...].astype(BF16), b_ref[...].astype(BF16), dims), e_refs, o_refs, i)
            return
        acc = refs[-1]

        @pl.when(k == 0)
        def _():
            acc[...] = jnp.zeros_like(acc)

        def step(a_ref):
            acc[...] += _dot(a_ref[...].astype(BF16), b_ref[...].astype(BF16), dims)

        for s in range(na):
            if na == 1:
                step(a_refs[0])
            else:
                in_use = (k >= first[s] * kb) & (k < (first[s] + stack[s]) * kb)
                pl.when(in_use)(functools.partial(step, a_refs[s]))

        @pl.when(k == nk - 1)
        def _():
            finish(acc[...], e_refs, o_refs, i)

    res = _pcall(
        body, name=name, grid=grid, in_specs=in_specs, out_specs=out_specs,
        out_shape=[o[0] for o in outs], scratch_shapes=[] if nk == 1 else [pltpu.VMEM((tm, tn), F32)],
        compiler_params=_params(dimension_semantics=("arbitrary", "arbitrary", "arbitrary")),
    )(*a_parts, b, *[e[0] for e in extras])
    return res[0] if len(res) == 1 else res


IN_PROJ_TILES = (1024, 1024, 1024)
DW_TILES = (1024, 1024, 512)


def _dw(x, dy, name, sub=None):
    return _mm([x], dy, "tn", name, tiles=DW_TILES, b_sub=sub)


def _add_extra_epilogue(acc, e_refs, o_refs, i):
    o_refs[0][...] = acc + e_refs[0][...]


def _mm_residual(a, b, mode, res, name):
    m = a.shape[0]
    n = b.shape[1] if mode == "nn" else b.shape[0]
    tm, tn = 512, 1024
    return _mm([a], b, mode, name, tiles=(tm, tn, 1024),
               extras=[(res, (tm, tn), lambda i, j, k: (i, j))],
               outs=[(SDS((m, n), F32), (tm, tn), lambda i, j, k: (i, j))],
               epilogue=_add_extra_epilogue)


def _rms_fwd(h, g, name):
    t, d = h.shape
    tm = 512

    def body(h_ref, g_ref, n_ref, r_ref):
        x = h_ref[...]
        r = lax.rsqrt(jnp.mean(x * x, axis=-1, keepdims=True) + RMS_EPS)
        n_ref[...] = ((x * r) * g_ref[...]).astype(BF16)
        r_ref[...] = r

    return _pcall(
        body, name=name, grid=(t // tm,),
        in_specs=[pl.BlockSpec((tm, d), lambda i: (i, 0)), pl.BlockSpec((1, d), lambda i: (0, 0))],
        out_specs=[pl.BlockSpec((tm, d), lambda i: (i, 0)), pl.BlockSpec((tm, 1), lambda i: (i, 0))],
        out_shape=[SDS((t, d), BF16), SDS((t, 1), F32)],
        compiler_params=_params(),
    )(h, g)


def _rms_bwd_rows(dn, x, g, r):
    xhat = x * r
    dxhat = dn * g
    dx = r * (dxhat - xhat * jnp.mean(dxhat * xhat, axis=-1, keepdims=True))
    dg = jnp.sum(dn * xhat, axis=0, keepdims=True)
    return dx, dg


def _mm_in_bwd(d_parts, w, h, g, r, dres, name, more=None, w_kmap=None):
    t = h.shape[0]
    tm = 512
    tk = D_MODEL
    row = lambda i, j, k: (i, 0)
    extras = [(h, (tm, D_MODEL), row), (g, (1, D_MODEL), lambda i, j, k: (0, 0)), (r, (tm, 1), row),
              (dres, (tm, D_MODEL), row)]
    if more is not None:
        extras.append((more, (tm, D_MODEL), row))

    def epilogue(acc, e_refs, o_refs, i):
        dn = acc if more is None else acc + e_refs[4][...]
        dx, dg = _rms_bwd_rows(dn, e_refs[0][...], e_refs[1][...], e_refs[2][...])
        o_refs[0][...] = e_refs[3][...] + dx

        @pl.when(i == 0)
        def _():
            o_refs[1][...] = dg

        @pl.when(i > 0)
        def _():
            o_refs[1][...] += dg

    return _mm(d_parts, w, "nt", name, tiles=(tm, D_MODEL, tk), extras=extras,
               outs=[(SDS((t, D_MODEL), F32), (tm, D_MODEL), row),
                     (SDS((1, D_MODEL), F32), (1, D_MODEL), lambda i, j, k: (0, 0))],
               epilogue=epilogue, b_kmap=w_kmap)


def _final_bwd(h, g, tgt):
    t, d = h.shape
    tm = 256

    def body(h_ref, g_ref, t_ref, dh_ref, dg_ref, loss_ref):
        i = pl.program_id(0)
        x = h_ref[...]
        gg = g_ref[...]
        r = lax.rsqrt(jnp.mean(x * x, axis=-1, keepdims=True) + RMS_EPS)
        err = (x * r) * gg - t_ref[...]
        part = 0.5 * jnp.sum(jnp.mean(err * err, axis=-1, keepdims=True), axis=0, keepdims=True)
        dx, dg = _rms_bwd_rows(err * (1.0 / d), x, gg, r)
        dh_ref[...] = dx

        @pl.when(i == 0)
        def _():
            dg_ref[...] = dg
            loss_ref[...] = jnp.broadcast_to(part, loss_ref.shape)

        @pl.when(i > 0)
        def _():
            dg_ref[...] += dg
            loss_ref[...] += jnp.broadcast_to(part, loss_ref.shape)

    return _pcall(
        body, name="final_norm_loss", grid=(t // tm,),
        in_specs=[pl.BlockSpec((tm, d), lambda i: (i, 0)), pl.BlockSpec((1, d), lambda i: (0, 0)),
                  pl.BlockSpec((tm, d), lambda i: (i, 0))],
        out_specs=[pl.BlockSpec((tm, d), lambda i: (i, 0)), pl.BlockSpec((1, d), lambda i: (0, 0)),
                   pl.BlockSpec((1, LANES), lambda i: (0, 0))],
        out_shape=[SDS((t, d), F32), SDS((1, d), F32), SDS((1, LANES), F32)],
        compiler_params=_params(),
    )(h, g, tgt)


GATE_ROWS = 256


def _split3(x):
    hi = x.astype(BF16)
    r1 = x - hi.astype(F32)
    mid = r1.astype(BF16)
    lo = (r1 - mid.astype(F32)).astype(BF16)
    return hi, mid, lo


def _tri_sum(x, upper):
    rows = x.shape[0]
    ri = lax.broadcasted_iota(jnp.int32, (rows, rows), 0)
    ci = lax.broadcasted_iota(jnp.int32, (rows, rows), 1)
    tri = jnp.where((ri <= ci) if upper else (ri >= ci), 1.0, 0.0).astype(BF16)
    hi, mid, lo = _split3(x)
    return _dot(tri, hi, NN) + _dot(tri, mid, NN) + _dot(tri, lo, NN)


def _log_sigmoid(x):
    return jnp.minimum(x, 0.0) - jnp.log1p(jnp.exp(-jnp.abs(x)))


def _fox_gate_fwd(projf, bpad):
    t = projf.shape[0]
    tb = GATE_ROWS

    def body(x_ref, b_ref, c_ref, carry):
        i = pl.program_id(0)

        @pl.when(i == 0)
        def _():
            carry[...] = jnp.zeros_like(carry)

        c_ref[...] = _tri_sum(_log_sigmoid(x_ref[...] + b_ref[...]), upper=False) + carry[...]
        carry[...] = c_ref[pl.ds(tb - 1, 1), :]

    return _pcall(
        body, name="fox_gate_fwd", grid=(t // tb,),
        in_specs=[pl.BlockSpec((tb, LANES), lambda i: (i, 0)), pl.BlockSpec((1, LANES), lambda i: (0, 0))],
        out_specs=pl.BlockSpec((tb, LANES), lambda i: (i, 0)),
        out_shape=SDS((t, LANES), F32), scratch_shapes=[pltpu.VMEM((1, LANES), F32)],
        compiler_params=_params(),
    )(projf, bpad)


def _fox_gate_bwd(projf, bpad, dc_query, dc_key):
    t = projf.shape[0]
    tb = GATE_ROWS
    nb = t // tb

    def body(x_ref, b_ref, dcq_ref, dck_ref, df_ref, db_ref, carry, buf):
        i = pl.program_id(0)

        @pl.when(i == 0)
        def _():
            carry[...] = jnp.zeros_like(carry)

        buf[...] = _tri_sum(dcq_ref[...] - dck_ref[...], upper=True) + carry[...]
        carry[...] = buf[pl.ds(0, 1), :]
        df = buf[...] * _sigmoid(-(x_ref[...] + b_ref[...]))
        df_ref[...] = df.astype(BF16)
        part = jnp.sum(df, axis=0, keepdims=True)

        @pl.when(i == 0)
        def _():
            db_ref[...] = part

        @pl.when(i > 0)
        def _():
            db_ref[...] += part

    rev = lambda i: (nb - 1 - i, 0)
    return _pcall(
        body, name="fox_gate_bwd", grid=(nb,),
        in_specs=[pl.BlockSpec((tb, LANES), rev), pl.BlockSpec((1, LANES), lambda i: (0, 0)),
                  pl.BlockSpec((tb, LANES), rev), pl.BlockSpec((tb, LANES), rev)],
        out_specs=[pl.BlockSpec((tb, LANES), rev), pl.BlockSpec((1, LANES), lambda i: (0, 0))],
        out_shape=[SDS((t, LANES), BF16), SDS((1, LANES), F32)],
        scratch_shapes=[pltpu.VMEM((1, LANES), F32), pltpu.VMEM((tb, LANES), F32)],
        compiler_params=_params(),
    )(projf, bpad, dc_query, dc_key)


FOX_TQ = 1024
FOX_TQ_FWD = 1024
FOX_SCALE = FOX_HEAD_DIM ** -0.5


def _low_lanes(shape):
    return lax.broadcasted_iota(jnp.int32, shape, len(shape) - 1) < FOX_HEAD_DIM


FOX_AUG = 3
FOX_CHAIN = 256
FOX_SUM_ROWS = 8


def _top_rows(shape):
    return lax.broadcasted_iota(jnp.int32, shape, 0) < FOX_HEAD_DIM


def _fox_aug(a, b, sign_a, sign_b, piece_entry, ones_entry, name):
    t = a.shape[0]
    tb = 512

    def body(a_ref, b_ref, o_ref):
        x = sign_a * a_ref[...]
        if sign_b != 0.0:
            x = x + sign_b * b_ref[...]
        head = lax.broadcasted_iota(jnp.int32, (LANES, D_MODEL), 0)
        col = lax.broadcasted_iota(jnp.int32, (LANES, D_MODEL), 1)
        base = (head // 2) * LANES + (1 - head % 2) * FOX_HEAD_DIM + piece_entry
        acc = jnp.zeros((tb, D_MODEL), F32)
        for e, piece in enumerate(_split3(x)):
            place = jnp.where((head < FOX_HEADS) & (col == base + e), 1.0, 0.0).astype(BF16)
            acc = acc + _dot(piece, place, NN)
        if ones_entry is not None:
            ent = lax.broadcasted_iota(jnp.int32, (1, D_MODEL), 1) % FOX_HEAD_DIM
            acc = acc + jnp.where((ent >= ones_entry) & (ent < ones_entry + FOX_AUG), 1.0, 0.0)
        o_ref[...] = acc.astype(BF16)

    blk = pl.BlockSpec((tb, LANES), lambda i: (i, 0))
    return _pcall(
        body, name=name, grid=(t // tb,), in_specs=[blk, blk],
        out_specs=pl.BlockSpec((tb, D_MODEL), lambda i: (i, 0)), out_shape=SDS((t, D_MODEL), BF16),
        compiler_params=_params(),
    )(a, b)


def _fox_unpack_dc(dck_wide, dcq):
    t = dck_wide.shape[0]
    dck = dck_wide.reshape(t, FOX_PAIRS, 2, FOX_HEAD_DIM)[:, :, ::-1, 0].reshape(t, FOX_HEADS)
    return dcq[:, :, 0, :].reshape(FOX_HEADS, t).T, dck


def _causal_steps(nq, key_major):
    if key_major:
        pairs = [(i, j) for j in range(nq) for i in range(j, nq)]
    else:
        pairs = [(i, j) for i in range(nq) for j in range(i + 1)]
    return (jnp.asarray([p[0] for p in pairs], jnp.int32), jnp.asarray([p[1] for p in pairs], jnp.int32))


def _pair_operand(low, own, other, hh):
    return jnp.where(low, own, other) if hh == 0 else jnp.where(low, other, own)


def _fox_fwd(proj, qaug, kaug):
    t = proj.shape[0]
    tq = tk = min(FOX_TQ_FWD, t)
    nq = t // tq
    cb = D_MODEL // LANES
    half = min(FOX_CHAIN, tq)

    i_tab, j_tab = _causal_steps(nq, key_major=False)

    def body(i_ref, j_ref, q_ref, k_ref, v_ref, z_ref, qa_ref, ka_ref, o_ref, g_ref, lse_ref, m_s, l_s, acc_s):
        step = pl.program_id(1)
        i, j = i_ref[step], j_ref[step]

        @pl.when(j == 0)
        def _():
            m_s[...] = jnp.full_like(m_s, NEG_INF)
            l_s[...] = jnp.zeros_like(l_s)
            acc_s[...] = jnp.zeros_like(acc_s)

        low = _low_lanes((tq, LANES))
        top = _top_rows((LANES, tq))

        def update(masked):
            qs = q_ref[...] * FOX_SCALE
            qa, k, ka, v = qa_ref[...], k_ref[...], ka_ref[...], v_ref[...]
            if masked:
                causal = (lax.broadcasted_iota(jnp.int32, (tk, tq), 0) <= lax.broadcasted_iota(jnp.int32, (tk, tq), 1))
            one = jnp.ones_like(v)
            chains = [(hh, slice(c * half, (c + 1) * half)) for hh in range(2) for c in range(tq // half)]
            qh = [_pair_operand(low, qs, qa, hh) for hh in range(2)]
            kh = [_pair_operand(low, k, ka, hh) for hh in range(2)]
            vh = [_pair_operand(low, v, one, hh) for hh in range(2)]
            scores = [_dot(kh[hh], qh[hh][cols, :], NT) for hh, cols in chains]
            for (hh, cols), s in zip(chains, scores):
                if masked:
                    s = jnp.where(causal[:, cols], s, NEG_INF)
                m_prev = m_s[hh, :, cols]
                m_new = jnp.maximum(m_prev, jnp.max(s, axis=0, keepdims=True))
                alpha = jnp.exp(m_prev - m_new)
                pv = _dot(vh[hh], jnp.exp(s - m_new).astype(BF16), TN)
                sums = pv[FOX_HEAD_DIM:FOX_HEAD_DIM + FOX_SUM_ROWS, :] if hh == 0 else pv[0:FOX_SUM_ROWS, :]
                l_s[hh, :, cols] = alpha * l_s[hh, :, cols] + sums
                m_s[hh, :, cols] = m_new
                own = top[:, cols] if hh == 0 else jnp.logical_not(top[:, cols])
                acc_s[:, cols] = jnp.where(own, acc_s[:, cols] * alpha + pv, acc_s[:, cols])

        pl.when(j < i)(functools.partial(update, False))
        pl.when(j == i)(functools.partial(update, True))

        @pl.when(j == i)
        def _():
            o = (acc_s[...] / jnp.where(top, l_s[0, 0:1, :], l_s[1, 0:1, :])).T
            z = z_ref[...].astype(F32)
            o_ref[...] = o.astype(BF16)
            g_ref[...] = (o * (z * _sigmoid(z))).astype(BF16)
            for hh in range(2):
                lse_ref[hh] = m_s[hh] + jnp.log(l_s[hh, 0:1, :])

    qblk = lambda col: pl.BlockSpec((tq, LANES), lambda h, s, it, jt: (it[s], col + h))
    kblk = lambda col: pl.BlockSpec((tk, LANES), lambda h, s, it, jt: (jt[s], col + h))
    return _pcall(
        body, name="fox_attn_fwd",
        grid_spec=pltpu.PrefetchScalarGridSpec(
            num_scalar_prefetch=2, grid=(FOX_PAIRS, i_tab.shape[0]),
            in_specs=[qblk(0), kblk(cb), kblk(2 * cb), qblk(3 * cb), qblk(0), kblk(0)],
            out_specs=[qblk(0), qblk(0), pl.BlockSpec((2, 1, tq), lambda h, s, it, jt: (h, 0, it[s]))],
            scratch_shapes=[pltpu.VMEM((2, 1, tq), F32), pltpu.VMEM((2, FOX_SUM_ROWS, tq), F32),
                            pltpu.VMEM((LANES, tq), F32)]),
        out_shape=[SDS((t, D_MODEL), BF16), SDS((t, D_MODEL), BF16), SDS((FOX_HEADS, 1, t), F32)],
        compiler_params=_params(dimension_semantics=("arbitrary", "arbitrary")),
    )(i_tab, j_tab, proj, proj, proj, proj, qaug, kaug)


def _fox_bwd(proj, do, qaug, kaug, doaug):
    t = proj.shape[0]
    tq = tk = min(FOX_TQ, t)
    nq = t // tq
    cb = D_MODEL // LANES

    i_tab, j_tab = _causal_steps(nq, key_major=True)

    def body(i_ref, j_ref, q_ref, k_ref, v_ref, do_ref, qa_ref, ka_ref, da_ref,
             dq_ref, dk_ref, dv_ref, dck_ref, dcq_ref, dq_acc, dcq_acc, dk_acc, dks_acc, dv_acc):
        step = pl.program_id(1)
        i, j = i_ref[step], j_ref[step]
        low = _low_lanes((tq, LANES))
        top = _top_rows((LANES, tq))

        @pl.when(i == j)
        def _():
            dk_acc[...] = jnp.zeros_like(dk_acc)
            dks_acc[...] = jnp.zeros_like(dks_acc)
            dv_acc[...] = jnp.zeros_like(dv_acc)

        def update(masked):
            qs = q_ref[...] * FOX_SCALE
            k, v, dout = k_ref[...], v_ref[...], do_ref[...]
            qa, ka, da = qa_ref[...], ka_ref[...], da_ref[...]
            lane = lax.broadcasted_iota(jnp.int32, (tk, LANES), 1)
            vone = jnp.where((lane & (FOX_HEAD_DIM - 1)) < FOX_AUG, 1.0, 0.0).astype(v.dtype)
            one = jnp.ones_like(k)
            if masked:
                causal = (lax.broadcasted_iota(jnp.int32, (tk, tq), 0) <= lax.broadcasted_iota(jnp.int32, (tk, tq), 1))
            parts = []
            scores = [_dot(_pair_operand(low, k, ka, hh), _pair_operand(low, qs, qa, hh), NT) for hh in range(2)]
            dps = [_dot(_pair_operand(low, v, vone, hh), _pair_operand(low, dout, da, hh), NT) for hh in range(2)]
            for hh in range(2):
                s = scores[hh]
                if masked:
                    s = jnp.where(causal, s, NEG_INF)
                p = jnp.exp(s)
                ds = p * dps[hh]
                pb = p.astype(BF16)
                dsb = ds.astype(BF16)
                parts.append((_dot(pb, dout, NN),
                              _dot(dsb, _pair_operand(low, qs, one, hh), NN),
                              _dot(_pair_operand(low, k, one, hh), dsb, TN)))
            dv_acc[...] += jnp.where(low, parts[0][0], parts[1][0])
            dk_acc[...] += jnp.where(low, parts[0][1], parts[1][1])
            dks_acc[...] += jnp.where(low, parts[1][1], parts[0][1])
            dq_t = jnp.where(top, parts[0][2], parts[1][2]) * FOX_SCALE
            sum_a = parts[0][2][FOX_HEAD_DIM:FOX_HEAD_DIM + FOX_SUM_ROWS, :]
            sum_b = parts[1][2][0:FOX_SUM_ROWS, :]

            @pl.when(j == 0)
            def _():
                dq_acc[i] = dq_t
                dcq_acc[0, i] = sum_a
                dcq_acc[1, i] = sum_b

            @pl.when(j > 0)
            def _():
                dq_acc[i] += dq_t
                dcq_acc[0, i] += sum_a
                dcq_acc[1, i] += sum_b

        pl.when(i > j)(functools.partial(update, False))
        pl.when(i == j)(functools.partial(update, True))

        @pl.when(i == nq - 1)
        def _():
            dk_ref[...] = dk_acc[...].astype(BF16)
            dv_ref[...] = dv_acc[...].astype(BF16)
            dck_ref[...] = dks_acc[...]

        @pl.when((i == nq - 1) & (j == nq - 1))
        def _():
            for blk in range(nq):
                dq_ref[blk * tq:(blk + 1) * tq, :] = dq_acc[blk].T.astype(BF16)
            dcq_ref[...] = dcq_acc[...]

    qblk = lambda col: pl.BlockSpec((tq, LANES), lambda h, s, it, jt: (it[s], col + h))
    kblk = lambda col: pl.BlockSpec((tk, LANES), lambda h, s, it, jt: (jt[s], col + h))
    return _pcall(
        body, name="fox_attn_bwd",
        grid_spec=pltpu.PrefetchScalarGridSpec(
            num_scalar_prefetch=2, grid=(FOX_PAIRS, i_tab.shape[0]),
            in_specs=[qblk(0), kblk(cb), kblk(2 * cb), qblk(0), qblk(0), kblk(0), qblk(0)],
            out_specs=[pl.BlockSpec((t, LANES), lambda h, s, it, jt: (0, h)), kblk(0), kblk(0), kblk(0),
                       pl.BlockSpec((2, nq, FOX_SUM_ROWS, tq), lambda h, s, it, jt: (h, 0, 0, 0))],
            scratch_shapes=[pltpu.VMEM((nq, LANES, tq), F32), pltpu.VMEM((2, nq, FOX_SUM_ROWS, tq), F32),
                            pltpu.VMEM((tk, LANES), F32), pltpu.VMEM((tk, LANES), F32), pltpu.VMEM((tk, LANES), F32)]),
        out_shape=[SDS((t, D_MODEL), BF16), SDS((t, D_MODEL), BF16), SDS((t, D_MODEL), BF16),
                   SDS((t, D_MODEL), F32), SDS((FOX_HEADS, nq, FOX_SUM_ROWS, tq), F32)],
        compiler_params=_params(dimension_semantics=("arbitrary", "arbitrary")),
    )(i_tab, j_tab, proj, proj, proj, do, qaug, kaug, doaug)


def _mm_gate_bwd(dh, w_out, z_src, z_col0, o, heads, name):
    t = dh.shape[0]
    tm = 512
    row = lambda i, j, k: (i, 0)
    zcb = z_col0 // D_MODEL

    def epilogue(acc, e_refs, o_refs, i):
        z = e_refs[0][...].astype(F32)
        ov = e_refs[1][...].astype(F32)
        sg = _sigmoid(z)
        dout = acc * (z * sg)
        o_refs[0][...] = dout.astype(BF16)
        o_refs[1][...] = (acc * ov * (sg * (1.0 + z * (1.0 - sg)))).astype(BF16)
        prod = dout * ov
        for cbk in range(D_MODEL // LANES):
            seg = prod[:, cbk * LANES:(cbk + 1) * LANES]
            tot = jnp.sum(seg, axis=-1, keepdims=True)
            if heads == D_MODEL // LANES:
                o_refs[2][cbk] = tot
            else:
                lo = jnp.sum(jnp.where(_low_lanes(seg.shape), seg, 0.0), axis=-1, keepdims=True)
                o_refs[2][2 * cbk] = lo
                o_refs[2][2 * cbk + 1] = tot - lo

    return _mm([dh], w_out, "nt", name, tiles=(tm, D_MODEL, D_MODEL),
               extras=[(z_src, (tm, D_MODEL), lambda i, j, k: (i, zcb)), (o, (tm, D_MODEL), row)],
               outs=[(SDS((t, D_MODEL), BF16), (tm, D_MODEL), row), (SDS((t, D_MODEL), BF16), (tm, D_MODEL), row),
                     (SDS((heads, t, 1), F32), (heads, tm, 1), lambda i, j, k: (0, i, 0))],
               epilogue=epilogue)


def _ple_fwd(h, pin, w_up, w_gate, name):
    t = h.shape[0]
    tm = 512
    pd = pin.shape[1]

    def body(h_ref, p_ref, wu_ref, wg_ref, hn_ref, u_ref, a_ref):
        h = h_ref[...]
        u = _dot(p_ref[...].astype(BF16), wu_ref[...], NN)
        a = _dot(h.astype(BF16), wg_ref[...], NN)
        hn_ref[...] = h + u * _sigmoid(a)
        u_ref[...] = u.astype(BF16)
        a_ref[...] = a.astype(BF16)

    rows = pl.BlockSpec((tm, D_MODEL), lambda i: (i, 0))
    return _pcall(
        body, name=name, grid=(t // tm,),
        in_specs=[rows, pl.BlockSpec((tm, pd), lambda i: (i, 0)),
                  pl.BlockSpec((pd, D_MODEL), lambda i: (0, 0)), pl.BlockSpec((D_MODEL, D_MODEL), lambda i: (0, 0))],
        out_specs=[rows, rows, rows],
        out_shape=[SDS((t, D_MODEL), F32), SDS((t, D_MODEL), BF16), SDS((t, D_MODEL), BF16)],
        compiler_params=_params(),
    )(h, pin, w_up, w_gate)


def _ple_bwd_elem(dh, u, a, name):
    t = dh.shape[0]
    tm = 512

    def body(dh_ref, u_ref, a_ref, du_ref, da_ref):
        g = dh_ref[...]
        s = _sigmoid(a_ref[...].astype(F32))
        du_ref[...] = (g * s).astype(BF16)
        da_ref[...] = (g * u_ref[...].astype(F32) * (s * (1.0 - s))).astype(BF16)

    blk = pl.BlockSpec((tm, D_MODEL), lambda i: (i, 0))
    return _pcall(
        body, name=name, grid=(t // tm,), in_specs=[blk, blk, blk], out_specs=[blk, blk],
        out_shape=[SDS((t, D_MODEL), BF16), SDS((t, D_MODEL), BF16)], compiler_params=_params(),
    )(dh, u, a)


DIL_SCALE = LANES ** -0.5


def _dil_masks():
    ii = lax.broadcasted_iota(jnp.int32, (DIL_BLOCK, DIL_BLOCK), 0)
    jj = lax.broadcasted_iota(jnp.int32, (DIL_BLOCK, DIL_BLOCK), 1)
    return ii, jj


DIL_UNITS = 16
BNT = (((2,), (2,)), ((0,), (0,)))
BNN = (((2,), (1,)), ((0,), (0,)))
BTN = (((1,), (1,)), ((0,), (0,)))


def _dil_units(dil):
    return [(b, r) for b in range(DIL_UNITS // dil) for r in range(dil)]


def _unit_rows(b, r, dil):
    return pl.ds(b * DIL_BLOCK * dil + r, DIL_BLOCK, stride=dil)


def _gather_units(cur, dil, shift=0, edge=None, lead=()):
    nbk = DIL_UNITS // dil
    parts = []
    for b, r in _dil_units(dil):
        bb = b + shift
        if 0 <= bb < nbk:
            parts.append(cur[lead + (_unit_rows(bb, r, dil), slice(None))])
        else:
            parts.append(edge[lead + (pl.ds(r, DIL_BLOCK, stride=dil), slice(None))])
    return jnp.stack(parts)


def _scatter_units(dst, val, dil):
    for u, (b, r) in enumerate(_dil_units(dil)):
        dst[_unit_rows(b, r, dil), :] = val[u]


def _dil_bias(slope, prev):
    ii, jj = _dil_masks()
    dist = (DIL_BLOCK + ii - jj) if prev else (ii - jj)
    return (slope * dist.astype(F32))[None], ((jj >= ii) if prev else (jj <= ii))[None]


def _dil_fwd(proj, slopes, grp, dil, name):
    t = proj.shape[0]
    rows = DIL_BLOCK * DIL_UNITS
    edge_rows = DIL_BLOCK * dil
    nbk = DIL_UNITS // dil
    nsb = t // rows
    qc, kc_, vc_ = grp * DIL_HEADS, 3 * DIL_HEADS + grp * DIL_HEADS, 6 * DIL_HEADS + grp * DIL_HEADS

    def body(q_ref, kp_ref, kc_ref, vp_ref, vc_ref, sl_ref, o_ref, lse_ref, qf, kpf, kcf, vpf, vcf, of, lf):
        m = pl.program_id(1)
        for src, dst in ((q_ref, qf), (kp_ref, kpf), (kc_ref, kcf), (vp_ref, vpf), (vc_ref, vcf)):
            dst[...] = src[...].astype(F32)
        slope = sl_ref[0]
        unit = lax.broadcasted_iota(jnp.int32, (DIL_UNITS, 1, 1), 0)
        has_prev = (unit >= dil) | (m > 0)
        q = _gather_units(qf, dil).astype(BF16)
        kc, vc = _gather_units(kcf, dil).astype(BF16), _gather_units(vcf, dil).astype(BF16)
        kp, vp = _gather_units(kcf, dil, -1, kpf).astype(BF16), _gather_units(vcf, dil, -1, vpf).astype(BF16)
        bias_p, ok_p = _dil_bias(slope, True)
        bias_c, ok_c = _dil_bias(slope, False)
        sp = jnp.where(ok_p & has_prev, _dot(q, kp, BNT) * DIL_SCALE - bias_p, NEG_INF)
        sc = jnp.where(ok_c, _dot(q, kc, BNT) * DIL_SCALE - bias_c, NEG_INF)
        mx = jnp.maximum(jnp.max(sp, axis=-1, keepdims=True), jnp.max(sc, axis=-1, keepdims=True))
        pp = jnp.exp(sp - mx)
        pc = jnp.exp(sc - mx)
        l = jnp.sum(pp, axis=-1, keepdims=True) + jnp.sum(pc, axis=-1, keepdims=True)
        o = (_dot(pp.astype(BF16), vp, BNN) + _dot(pc.astype(BF16), vc, BNN)) / l
        _scatter_units(of, o, dil)
        _scatter_units(lf, mx + jnp.log(l), dil)
        o_ref[...] = of[...].astype(BF16)
        lse_ref[0] = lf[...]

    cur = lambda col: pl.BlockSpec((rows, LANES), lambda h, m: (m, col + h))
    prev = lambda col: pl.BlockSpec((edge_rows, LANES), lambda h, m: (jnp.maximum(m * nbk - 1, 0), col + h))
    return _pcall(
        body, name=name, grid=(DIL_HEADS, nsb),
        in_specs=[cur(qc), prev(kc_), cur(kc_), prev(vc_), cur(vc_), pl.BlockSpec((1, 1, 1), lambda h, m: (h, 0, 0))],
        out_specs=[pl.BlockSpec((rows, LANES), lambda h, m: (m, h)), pl.BlockSpec((1, rows, 1), lambda h, m: (h, m, 0))],
        out_shape=[SDS((t, D_MODEL), BF16), SDS((DIL_HEADS, t, 1), F32)],
        scratch_shapes=[pltpu.VMEM((rows, LANES), F32), pltpu.VMEM((edge_rows, LANES), F32), pltpu.VMEM((rows, LANES), F32),
                        pltpu.VMEM((edge_rows, LANES), F32), pltpu.VMEM((rows, LANES), F32), pltpu.VMEM((rows, LANES), F32),
                        pltpu.VMEM((rows, 1), F32)],
        compiler_params=_params(),
    )(proj, proj, proj, proj, proj, slopes)


def _dil_mix(outs, lses, proj, z_col0):
    t = proj.shape[0]
    tm = 512
    zcb = z_col0 // LANES
    ng = len(outs)

    def body(*refs):
        o_refs, l_refs, z_ref = refs[:ng], refs[ng:2 * ng], refs[2 * ng]
        om_ref, g_ref, lse_ref = refs[2 * ng + 1:]
        ls = [r[0] for r in l_refs]
        mx = functools.reduce(jnp.maximum, ls)
        es = [jnp.exp(l - mx) for l in ls]
        tot = functools.reduce(jnp.add, es)
        o = functools.reduce(jnp.add, [(e / tot) * r[...].astype(F32) for e, r in zip(es, o_refs)])
        z = z_ref[...].astype(F32)
        om_ref[...] = o.astype(BF16)
        g_ref[...] = (o * (z * _sigmoid(z))).astype(BF16)
        lse_ref[0] = mx + jnp.log(tot)

    tile = pl.BlockSpec((tm, LANES), lambda i, h: (i, h))
    col = pl.BlockSpec((1, tm, 1), lambda i, h: (h, i, 0))
    return _pcall(
        body, name="dil_mix", grid=(t // tm, DIL_HEADS),
        in_specs=[tile] * ng + [col] * ng + [pl.BlockSpec((tm, LANES), lambda i, h: (i, zcb + h))],
        out_specs=[tile, tile, col],
        out_shape=[SDS((t, D_MODEL), BF16), SDS((t, D_MODEL), BF16), SDS((DIL_HEADS, t, 1), F32)],
        compiler_params=_params(),
    )(*outs, *lses, proj)


def _dil_bwd(proj, do, lse, delta, slopes, grp, dil, name):
    t = proj.shape[0]
    rows = DIL_BLOCK * DIL_UNITS
    edge_rows = DIL_BLOCK * dil
    nbk = DIL_UNITS // dil
    nsb = t // rows
    last_edge = t // edge_rows - 1
    qc, kc_, vc_ = grp * DIL_HEADS, 3 * DIL_HEADS + grp * DIL_HEADS, 6 * DIL_HEADS + grp * DIL_HEADS

    def body(q_ref, qn_ref, kp_ref, kc_ref, vp_ref, vc_ref, do_ref, don_ref, l_ref, ln_ref, d_ref, dn_ref, sl_ref,
             dqkv_ref, qf, qnf, kpf, kcf, vpf, vcf, dof, donf, dqf, dkf, dvf):
        m = pl.program_id(1)
        for src, dst in ((q_ref, qf), (qn_ref, qnf), (kp_ref, kpf), (kc_ref, kcf), (vp_ref, vpf), (vc_ref, vcf),
                         (do_ref, dof), (don_ref, donf)):
            dst[...] = src[...].astype(F32)
        slope = sl_ref[0]
        unit = lax.broadcasted_iota(jnp.int32, (DIL_UNITS, 1, 1), 0)
        has_prev = (unit >= dil) | (m > 0)
        has_next = (unit < DIL_UNITS - dil) | (m < nsb - 1)
        b16 = lambda x: x.astype(BF16)
        q, kc, vc, dout = (b16(_gather_units(x, dil)) for x in (qf, kcf, vcf, dof))
        kp, vp = b16(_gather_units(kcf, dil, -1, kpf)), b16(_gather_units(vcf, dil, -1, vpf))
        qn, don = b16(_gather_units(qf, dil, 1, qnf)), b16(_gather_units(dof, dil, 1, donf))
        lrow, drow = _gather_units(l_ref, dil, lead=(0,)), _gather_units(d_ref, dil, lead=(0,))
        lnrow = _gather_units(l_ref, dil, 1, ln_ref, lead=(0,))
        dnrow = _gather_units(d_ref, dil, 1, dn_ref, lead=(0,))
        bias_p, ok_p = _dil_bias(slope, True)
        bias_c, ok_c = _dil_bias(slope, False)
        sp = jnp.where(ok_p & has_prev, _dot(q, kp, BNT) * DIL_SCALE - bias_p, NEG_INF)
        sc = jnp.where(ok_c, _dot(q, kc, BNT) * DIL_SCALE - bias_c, NEG_INF)
        pp = jnp.exp(sp - lrow)
        pc = jnp.exp(sc - lrow)
        dsp = b16(pp * (_dot(dout, vp, BNT) - drow))
        dsc = b16(pc * (_dot(dout, vc, BNT) - drow))
        _scatter_units(dqf, (_dot(dsp, kp, BNN) + _dot(dsc, kc, BNN)) * DIL_SCALE, dil)
        sn = jnp.where(ok_p & has_next, _dot(qn, kc, BNT) * DIL_SCALE - bias_p, NEG_INF)
        pn = jnp.exp(sn - lnrow)
        dsn = b16(pn * (_dot(don, vc, BNT) - dnrow))
        _scatter_units(dkf, (_dot(dsc, q, BTN) + _dot(dsn, qn, BTN)) * DIL_SCALE, dil)
        _scatter_units(dvf, _dot(b16(pc), dout, BTN) + _dot(b16(pn), don, BTN), dil)
        for s, src in enumerate((dqf, dkf, dvf)):
            dqkv_ref[s] = src[...].astype(BF16)

    prev_i = lambda m: jnp.maximum(m * nbk - 1, 0)
    next_i = lambda m: jnp.minimum((m + 1) * nbk, last_edge)
    cur = lambda col: pl.BlockSpec((rows, LANES), lambda h, m: (m, col + h))
    edge = lambda col, f: pl.BlockSpec((edge_rows, LANES), lambda h, m: (f(m), col + h))
    colcur = pl.BlockSpec((1, rows, 1), lambda h, m: (h, m, 0))
    colnext = pl.BlockSpec((1, edge_rows, 1), lambda h, m: (h, next_i(m), 0))
    out_blk = pl.BlockSpec((3, rows, LANES), lambda h, m: (0, m, h))
    big, small = pltpu.VMEM((rows, LANES), F32), pltpu.VMEM((edge_rows, LANES), F32)
    return _pcall(
        body, name=name, grid=(DIL_HEADS, nsb),
        in_specs=[cur(qc), edge(qc, next_i), edge(kc_, prev_i), cur(kc_), edge(vc_, prev_i), cur(vc_),
                  cur(0), edge(0, next_i), colcur, colnext, colcur, colnext,
                  pl.BlockSpec((1, 1, 1), lambda h, m: (h, 0, 0))],
        out_specs=out_blk, out_shape=SDS((3, t, D_MODEL), BF16),
        scratch_shapes=[big, small, small, big, small, big, big, small, big, big, big],
        compiler_params=_params(),
    )(proj, proj, proj, proj, proj, proj, do, do, lse, lse, delta, delta, slopes)


def _mesh_pos():
    x, y, c = lax.axis_index("x"), lax.axis_index("y"), lax.axis_index("c")
    return x, y, c


def _peer(pos, k):
    x, y, c = pos
    px = 1 - x if k & 4 else x
    py = 1 - y if k & 2 else y
    pc = 1 - c if k & 1 else c
    return (px, py, pc), 4 * px + 2 * py + pc


N_CHIPS = 4
CHIP_FLIPS = ((1, 0), (0, 1), (1, 1))


def _other_chips(x, y):
    return [(1 - x if fx else x, 1 - y if fy else y) for fx, fy in CHIP_FLIPS]


def _all_gather(arrays):
    n = len(arrays)
    per = 2 * N_CHIPS - 1
    hbm = pl.BlockSpec(memory_space=pltpu.HBM)

    def body(*refs):
        ins, outs = refs[:n], refs[n:2 * n]
        send_sems, recv_sems, local_sems = refs[2 * n:]
        x, y, c = _mesh_pos()
        sibling = (x, y, 1 - c)
        chips = _other_chips(x, y)
        block = lambda px, py, pc: 4 * px + 2 * py + pc

        def copy(w, k, src, blk, to):
            return pltpu.make_async_remote_copy(
                src_ref=src, dst_ref=outs[w].at[blk], send_sem=send_sems.at[w * per + k],
                recv_sem=recv_sems.at[w * per + k], device_id=to, device_id_type=MESH)

        local, started = [], []
        for w in range(n):
            cp = pltpu.make_async_copy(ins[w], outs[w].at[block(x, y, c)], local_sems.at[w])
            cp.start()
            local.append(cp)
            started.append(copy(w, 0, ins[w], block(x, y, c), sibling))
            for j, (px, py) in enumerate(chips):
                started.append(copy(w, 1 + j, ins[w], block(x, y, c), (px, py, c)))
        for cp in started:
            cp.start()
        for j, (px, py) in enumerate(chips):
            for w in range(n):
                copy(w, 1 + j, ins[w], block(px, py, c), sibling).wait_recv()
                cp = copy(w, 4 + j, outs[w].at[block(px, py, c)], block(px, py, c), sibling)
                cp.start()
                started.append(cp)
        for w in range(n):
            copy(w, 0, ins[w], block(x, y, 1 - c), sibling).wait_recv()
            for j, (px, py) in enumerate(chips):
                copy(w, 4 + j, ins[w], block(px, py, 1 - c), sibling).wait_recv()
        for cp in started:
            cp.wait_send()
        for cp in local:
            cp.wait()

    return _pcall(
        body, name="all_gather_weights", in_specs=[hbm] * n, out_specs=[hbm] * n,
        out_shape=[SDS((N_DEV,) + a.shape, a.dtype) for a in arrays],
        scratch_shapes=[pltpu.SemaphoreType.DMA((n * per,)), pltpu.SemaphoreType.DMA((n * per,)),
                        pltpu.SemaphoreType.DMA((n,))],
    )(*arrays)


def _core_exchange(slabs):
    n = len(slabs)
    hbm = pl.BlockSpec(memory_space=pltpu.HBM)

    def body(*refs):
        ins, outs = refs[:n], refs[n:2 * n]
        send_sems, recv_sems = refs[2 * n:]
        x, y, c = _mesh_pos()
        copies = [pltpu.make_async_remote_copy(
            src_ref=ins[w].at[pl.ds(0, N_CHIPS), 1 - c], dst_ref=outs[w], send_sem=send_sems.at[w],
            recv_sem=recv_sems.at[w], device_id=(x, y, 1 - c), device_id_type=MESH) for w in range(n)]
        for cp in copies:
            cp.start()
        for cp in copies:
            cp.wait_recv()
        for cp in copies:
            cp.wait_send()

    return _pcall(
        body, name="grads_core_exchange", in_specs=[hbm] * n, out_specs=[hbm] * n,
        out_shape=[SDS((N_CHIPS,) + a.shape[2:], a.dtype) for a in slabs],
        scratch_shapes=[pltpu.SemaphoreType.DMA((n,)), pltpu.SemaphoreType.DMA((n,))],
    )(*slabs)


def _chip_exchange(partials):
    n = len(partials)
    per = N_CHIPS - 1
    hbm = pl.BlockSpec(memory_space=pltpu.HBM)

    def body(*refs):
        ins, outs = refs[:n], refs[n:2 * n]
        send_sems, recv_sems, local_sems = refs[2 * n:]
        x, y, c = _mesh_pos()
        mine = 2 * x + y
        local, sends, recvs = [], [], []
        for w in range(n):
            cp = pltpu.make_async_copy(ins[w].at[mine], outs[w].at[mine], local_sems.at[w])
            cp.start()
            local.append(cp)
            for j, (px, py) in enumerate(_other_chips(x, y)):
                theirs = 2 * px + py
                sems = dict(send_sem=send_sems.at[w * per + j], recv_sem=recv_sems.at[w * per + j],
                            device_id=(px, py, c), device_id_type=MESH)
                cp = pltpu.make_async_remote_copy(src_ref=ins[w].at[theirs], dst_ref=outs[w].at[mine], **sems)
                cp.start()
                sends.append(cp)
                recvs.append(pltpu.make_async_remote_copy(src_ref=ins[w].at[theirs], dst_ref=outs[w].at[theirs], **sems))
        for cp in recvs:
            cp.wait_recv()
        for cp in sends:
            cp.wait_send()
        for cp in local:
            cp.wait()

    return _pcall(
        body, name="grads_chip_exchange", in_specs=[hbm] * n, out_specs=[hbm] * n,
        out_shape=[SDS(a.shape, a.dtype) for a in partials],
        scratch_shapes=[pltpu.SemaphoreType.DMA((n * per,)), pltpu.SemaphoreType.DMA((n * per,)),
                        pltpu.SemaphoreType.DMA((n,))],
    )(*partials)


def _core_sum(slabs, from_sibling, core, name):
    _, _, r, c = slabs.shape
    tr = min(r, 256)

    def body(core_ref, a_ref, b_ref, o_ref):
        o_ref[...] = (a_ref[0].astype(F32) + b_ref[...].astype(F32)).astype(BF16)

    return _pcall(
        body, name=name,
        grid_spec=pltpu.PrefetchScalarGridSpec(
            num_scalar_prefetch=1, grid=(N_CHIPS, r // tr),
            in_specs=[pl.BlockSpec((1, 1, tr, c), lambda g, i, core: (g, core[0], i, 0)),
                      pl.BlockSpec((1, tr, c), lambda g, i, core: (g, i, 0))],
            out_specs=pl.BlockSpec((1, tr, c), lambda g, i, core: (g, i, 0))),
        out_shape=SDS((N_CHIPS, r, c), BF16), compiler_params=_params(),
    )(core, slabs, from_sibling)


HBM_SPEC = pl.BlockSpec(memory_space=pltpu.HBM)
SEM_SPEC = pl.BlockSpec(memory_space=pltpu.SEMAPHORE)
DATAFLOW = pltpu.SideEffectType.DATAFLOW_SIDE_EFFECTING


def _push_start(arrays, scatter, name):
    n = len(arrays)
    per = N_DEV - 1

    def body(*refs):
        srcs, lands = refs[:n], refs[n:2 * n]
        send_sems, recv_sems, token = refs[2 * n], refs[2 * n + 1], refs[-1]
        pos = _mesh_pos()
        me = 4 * pos[0] + 2 * pos[1] + pos[2]
        for w in range(n):
            for k in range(1, N_DEV):
                peer, peer_idx = _peer(pos, k)
                pltpu.make_async_remote_copy(
                    src_ref=srcs[w].at[peer_idx] if scatter else srcs[w], dst_ref=lands[w].at[me],
                    send_sem=send_sems.at[w * per + k - 1], recv_sem=recv_sems.at[w * per + k - 1],
                    device_id=peer, device_id_type=MESH).start()
        token[...] = jnp.zeros_like(token)

    land_shapes = [a.shape if scatter else (N_DEV,) + a.shape for a in arrays]
    in_hbm = lambda a: pltpu.with_memory_space_constraint(a, pltpu.HBM)
    lands = [in_hbm(lax.empty(s, a.dtype)) for s, a in zip(land_shapes, arrays)]
    sems = pltpu.SemaphoreType.DMA((n * per,))
    res = _pcall(
        body, name=name,
        out_shape=(sems, sems, *[pltpu.HBM(a.shape, a.dtype) for a in arrays],
                   *[pltpu.HBM(s, a.dtype) for s, a in zip(land_shapes, arrays)], SDS((8, LANES), F32)),
        in_specs=[HBM_SPEC] * (2 * n),
        out_specs=(SEM_SPEC, SEM_SPEC, *[HBM_SPEC] * (2 * n), pl.BlockSpec(memory_space=pltpu.VMEM)),
        input_output_aliases={i: 2 + i for i in range(2 * n)},
        compiler_params=pltpu.CompilerParams(has_side_effects=DATAFLOW),
    )(*[in_hbm(a) for a in arrays], *lands)
    return res[0], res[1], list(res[2:2 + n]), list(res[2 + n:2 + 2 * n]), res[-1]


def _push_wait(send_sems, recv_sems, arrays, lands, after, scatter, name):
    n = len(arrays)
    per = N_DEV - 1

    def body(*refs):
        srcs, lands_ = refs[:n], refs[n:2 * n]
        send_sems_, recv_sems_ = refs[2 * n], refs[2 * n + 1]
        pos = _mesh_pos()
        for w in range(n):
            for k in range(1, N_DEV):
                peer, peer_idx = _peer(pos, k)
                cp = pltpu.make_async_remote_copy(
                    src_ref=srcs[w].at[peer_idx] if scatter else srcs[w], dst_ref=lands_[w].at[peer_idx],
                    send_sem=send_sems_.at[w * per + k - 1], recv_sem=recv_sems_.at[w * per + k - 1],
                    device_id=peer, device_id_type=MESH)
                cp.wait_send()
                cp.wait_recv()

    res = _pcall(
        body, name=name,
        out_shape=(*[pltpu.HBM(a.shape, a.dtype) for a in arrays], *[pltpu.HBM(l.shape, l.dtype) for l in lands]),
        in_specs=[HBM_SPEC] * (2 * n) + [SEM_SPEC, SEM_SPEC, pl.BlockSpec(memory_space=pl.ANY)],
        out_specs=[HBM_SPEC] * (2 * n), input_output_aliases={i: i for i in range(2 * n)},
        compiler_params=pltpu.CompilerParams(has_side_effects=DATAFLOW),
    )(*arrays, *lands, send_sems, recv_sems, after)
    return list(res[n:])


def _fill_own(land, own):
    me = 4 * lax.axis_index("x") + 2 * lax.axis_index("y") + lax.axis_index("c")
    return lax.dynamic_update_slice(land, own[None], (me,) + (0,) * own.ndim)


def _adam_math(w, g, m, v):
    m = ADAM_B1 * m + (1.0 - ADAM_B1) * g
    v = ADAM_B2 * v + (1.0 - ADAM_B2) * (g * g)
    m_hat = m / (1.0 - ADAM_B1 ** ADAM_STEP)
    v_hat = v / (1.0 - ADAM_B2 ** ADAM_STEP)
    delta = -ADAM_LR * (m_hat / (jnp.sqrt(v_hat) + ADAM_EPS) + ADAM_WD * w)
    return delta, m, v


def _adamw(recv, w, m, v, name):
    r, c = w.shape
    tr = min(r, 128)
    n_parts = recv.shape[0]

    def body(g_ref, w_ref, m_ref, v_ref, go_ref, d_ref, mo_ref, vo_ref):
        g = g_ref[0].astype(F32)
        for s in range(1, n_parts):
            g = g + g_ref[s].astype(F32)
        delta, mn, vn = _adam_math(w_ref[...], g, m_ref[...], v_ref[...])
        go_ref[...] = g
        d_ref[...] = delta
        mo_ref[...] = mn
        vo_ref[...] = vn

    blk = pl.BlockSpec((tr, c), lambda i: (i, 0))
    return _pcall(
        body, name=name, grid=(r // tr,),
        in_specs=[pl.BlockSpec((n_parts, tr, c), lambda i: (0, i, 0)), blk, blk, blk],
        out_specs=[blk] * 4, out_shape=[SDS((r, c), F32)] * 4, compiler_params=_params(),
    )(recv, w, m, v)


VEC_ROWS = 32


def _small_allreduce_adamw(vec, w, m, v):
    def body(vec_ref, w_ref, m_ref, v_ref, g_ref, d_ref, mo_ref, vo_ref, gath, send_sems, recv_sems):
        pos = _mesh_pos()
        me = 4 * pos[0] + 2 * pos[1] + pos[2]
        sends, recvs = [], []
        for k in range(1, N_DEV):
            peer, peer_idx = _peer(pos, k)
            cp = pltpu.make_async_remote_copy(src_ref=vec_ref, dst_ref=gath.at[me], send_sem=send_sems.at[k - 1],
                                              recv_sem=recv_sems.at[k - 1], device_id=peer, device_id_type=MESH)
            cp.start()
            sends.append(cp)
            recvs.append(pltpu.make_async_remote_copy(src_ref=vec_ref, dst_ref=gath.at[peer_idx],
                                                      send_sem=send_sems.at[k - 1], recv_sem=recv_sems.at[k - 1],
                                                      device_id=peer, device_id_type=MESH))
        gath[me] = vec_ref[...]
        for cp in recvs:
            cp.wait_recv()
        for cp in sends:
            cp.wait_send()
        tot = gath[0]
        for s in range(1, N_DEV):
            tot = tot + gath[s]
        rowi = lax.broadcasted_iota(jnp.int32, (8, LANES), 0)
        mine = jnp.sum(jnp.where(rowi == me, tot[16:24, :], 0.0), axis=0, keepdims=True)
        g = jnp.concatenate([tot[0:16, :], jnp.broadcast_to(mine, (8, LANES)), tot[24:32, :]], axis=0)
        delta, mn, vn = _adam_math(w_ref[...], g, m_ref[...], v_ref[...])
        g_ref[...] = g
        d_ref[...] = delta
        mo_ref[...] = mn
        vo_ref[...] = vn

    vm = pl.BlockSpec(memory_space=pltpu.VMEM)
    return _pcall(
        body, name="small_allreduce_adamw", in_specs=[vm] * 4, out_specs=[vm] * 4,
        out_shape=[SDS((VEC_ROWS, LANES), F32)] * 4,
        scratch_shapes=[pltpu.VMEM((N_DEV, VEC_ROWS, LANES), F32), pltpu.SemaphoreType.DMA((N_DEV - 1,)),
                        pltpu.SemaphoreType.DMA((N_DEV - 1,))],
        compiler_params=pltpu.CompilerParams(has_side_effects=True),
    )(vec, w, m, v)


def _cols_to_slabs(a):
    r, c8 = a.shape
    return a.reshape(r, N_DEV, c8 // N_DEV).transpose(1, 0, 2)


def _slabs_to_cols(a):
    n, r, c = a.shape
    return a.transpose(1, 0, 2).reshape(r, n * c)


def _rows8(vec):
    return vec.reshape(-1, LANES)


def _pad_rows(a, rows):
    return jnp.pad(a, ((0, rows - a.shape[0]), (0, LANES - a.shape[1])))


def kernel(x, p, fox_norm, fox_w_in, fox_b_f, fox_w_out, dil_norm, dil_w_in, dil_w_out, ple_w_up, ple_w_gate, final_norm, loss_target, m_fox_norm, m_fox_w_in, m_fox_b_f, m_fox_w_out, m_dil_norm, m_dil_w_in, m_dil_w_out, m_ple_w_up, m_ple_w_gate, m_final_norm, v_fox_norm, v_fox_w_in, v_fox_b_f, v_fox_w_out, v_dil_norm, v_dil_w_in, v_dil_w_out, v_ple_w_up, v_ple_w_gate, v_final_norm):
    t = x.shape[1]
    d = D_MODEL
    xs, tgt = x[0], loss_target[0]
    p0, p1 = p[0, 0], p[1, 0]
    fox_cols = fox_w_in.shape[2]
    ple_dim = ple_w_up.shape[1]

    later = [dil_w_in[0].astype(BF16), dil_w_out[0].astype(BF16), ple_w_up.reshape(-1, LANES).astype(BF16),
             ple_w_gate.reshape(-1, d).astype(BF16), dil_norm]
    push = _push_start(later, False, "gather_later_start")
    gw = _all_gather([fox_w_in[0].astype(BF16), fox_w_out[0].astype(BF16)])
    w_fox_in = _slabs_to_cols(gw[0])
    w_fox_main = w_fox_in[:, :4 * d]
    w_fox_f = jnp.pad(w_fox_in[:, 4 * d:], ((0, 0), (0, LANES - FOX_HEADS)))
    w_fox_out = gw[1].reshape(d, d)
    b_pad = jnp.pad(fox_b_f, ((0, 0), (0, LANES - FOX_HEADS)))

    n0, r0 = _rms_fwd(xs, fox_norm + push[4][0:1, 0:1], "rms_fox")
    proj0 = _mm([n0], w_fox_main, "nn", "fox_in_proj", tiles=IN_PROJ_TILES)
    projf = _mm([n0], w_fox_f, "nn", "fox_gate_proj", tiles=IN_PROJ_TILES, out_dtype=F32)
    c_all = _fox_gate_fwd(projf, b_pad)
    qaug_fwd = _fox_aug(c_all, c_all, 1.0, 0.0, 0, FOX_AUG, "fox_aug_q_fwd")
    kaug = _fox_aug(c_all, c_all, -1.0, 0.0, FOX_AUG, 0, "fox_aug_k")
    o0, g0, lse0 = _fox_fwd(proj0, qaug_fwd, kaug)
    h1 = _mm_residual(g0, w_fox_out, "nn", xs, "fox_out_proj")

    landed = _push_wait(push[0], push[1], push[2], push[3], h1, False, "gather_later_wait")
    gl = [_fill_own(zone, own) for zone, own in zip(landed, later)]
    w_dil_in = _slabs_to_cols(gl[0])
    w_dil_out = gl[1].reshape(d, d)
    w_up = gl[2].reshape(N_DEV, 2, ple_dim, LANES).transpose(1, 2, 0, 3).reshape(2, ple_dim, d)
    w_gate = gl[3].reshape(N_DEV, 2, d // N_DEV, d).transpose(1, 0, 2, 3).reshape(2, d, d)
    dil_norm_full = gl[4].reshape(1, d)
    h2, u0, a0 = _ple_fwd(h1, p0, w_up[0], w_gate[0], "ple0_fwd")

    n1, r1 = _rms_fwd(h2, dil_norm_full, "rms_dil")
    proj1 = _mm([n1], w_dil_in, "nn", "dil_in_proj", tiles=IN_PROJ_TILES)
    n_heads = len(DIL_PATTERN) * DIL_HEADS
    slopes = 2.0 ** (-ALIBI_MAX_EXP * jnp.arange(1, n_heads + 1, dtype=F32) / n_heads)
    dil_o, dil_lse, dil_slopes = [], [], []
    for grp, (_, dil) in enumerate(DIL_PATTERN):
        sl = (slopes[grp * DIL_HEADS:(grp + 1) * DIL_HEADS] * dil).reshape(DIL_HEADS, 1, 1)
        og, lg = _dil_fwd(proj1, sl, grp, dil, f"dil_attn_fwd_{grp}")
        dil_o.append(og)
        dil_lse.append(lg)
        dil_slopes.append(sl)
    z1_col0 = 9 * d
    o1, g1, lse1 = _dil_mix(dil_o, dil_lse, proj1, z1_col0)
    h3 = _mm_residual(g1, w_dil_out, "nn", h2, "dil_out_proj")
    h4, u1, a1 = _ple_fwd(h3, p1, w_up[1], w_gate[1], "ple1_fwd")

    dh4, d_final_norm, loss_part = _final_bwd(h4, final_norm.reshape(1, d), tgt)

    du1, da1 = _ple_bwd_elem(dh4, u1, a1, "ple1_bwd_elem")
    dw_up1 = _dw(p1, du1, "ple1_dw_up")
    dw_gate1 = _dw(h3, da1, "ple1_dw_gate")
    dh3 = _mm_residual(da1, w_gate[1], "nt", dh4, "ple1_dh")

    dw_dil_out = _dw(g1, dh3, "dil_dw_out")
    do1, dz1, delta1 = _mm_gate_bwd(dh3, w_dil_out, proj1, z1_col0, o1, DIL_HEADS, "dil_dgate")
    n_grp = len(DIL_PATTERN)
    dqkv = [_dil_bwd(proj1, do1, lse1, delta1, dil_slopes[grp], grp, dil, f"dil_attn_bwd_{grp}")
            for grp, (_, dil) in enumerate(DIL_PATTERN)]
    dw_cols = [_dw(n1, dqkv[grp], f"dil_dw_in_{kind}{grp}", sub=kind) for kind in range(3) for grp in range(n_grp)]
    dw_dil_in = jnp.concatenate(dw_cols + [_dw(n1, dz1, "dil_dw_in_z")], axis=1)
    dil_slabs = [_cols_to_slabs(dw_dil_in), dw_dil_out.reshape(N_DEV, d // N_DEV, d)]
    dil_push = _push_start(dil_slabs, True, "scatter_dil_start")
    group_major = lambda kb: jnp.where(kb < 3 * n_grp, (kb % 3) * n_grp + kb // 3, kb)
    dh2, d_dil_norm = _mm_in_bwd(dqkv + [dz1], w_dil_in, h2, dil_norm_full + dil_push[4][0:1, 0:1], r1, dh3, "dil_dx",
                                 w_kmap=group_major)

    du0, da0 = _ple_bwd_elem(dh2, u0, a0, "ple0_bwd_elem")
    dw_up0 = _dw(p0, du0, "ple0_dw_up")
    dw_gate0 = _dw(h1, da0, "ple0_dw_gate")
    dh1 = _mm_residual(da0, w_gate[0], "nt", dh2, "ple0_dh")

    dw_fox_out = _dw(g0, dh1, "fox_dw_out")
    do0, dz0, delta0 = _mm_gate_bwd(dh1, w_fox_out, proj0, 3 * d, o0, FOX_HEADS, "fox_dgate")
    head_cols = lambda a: jnp.pad(a, ((0, 0), (0, LANES - FOX_HEADS)))
    lse_cols = head_cols(lse0.reshape(FOX_HEADS, t).T)
    delta_cols = head_cols(delta0.reshape(FOX_HEADS, t).T)
    qaug_bwd = _fox_aug(c_all, lse_cols, 1.0, -1.0, 0, FOX_AUG, "fox_aug_q_bwd")
    doaug = _fox_aug(delta_cols, delta_cols, -1.0, 0.0, 0, None, "fox_aug_do")
    dq0, dk0, dv0, dck_wide, dcq = _fox_bwd(proj0, do0, qaug_bwd, kaug, doaug)
    dc_query, dc_key = _fox_unpack_dc(dck_wide, dcq)
    df, d_b_f = _fox_gate_bwd(projf, b_pad, head_cols(dc_query), head_cols(dc_key))
    dproj0 = [dq0, dk0, dv0, dz0]
    dw_fox_parts = [_dw(n0, dpart, f"fox_dw_in_{s}") for s, dpart in enumerate(dproj0)]
    dw_fox_f = _dw(n0, df, "fox_dw_gate")
    dn0_f = _mm([df], w_fox_f, "nt", "fox_dx_gate", out_dtype=F32)
    grad_x, d_fox_norm = _mm_in_bwd(dproj0, w_fox_main, xs, fox_norm, r0, dh1, "fox_dx", more=dn0_f)

    dw_fox_in = jnp.concatenate(dw_fox_parts + [dw_fox_f[:, :FOX_HEADS]], axis=1)
    slabs = [_cols_to_slabs(dw_fox_in), dw_fox_out.reshape(N_DEV, d // N_DEV, d),
             jnp.stack([dw_up0, dw_up1]).reshape(2, ple_dim, N_DEV, LANES).transpose(2, 0, 1, 3).reshape(N_DEV, -1, LANES),
             jnp.stack([dw_gate0, dw_gate1]).reshape(2, N_DEV, d // N_DEV, d).transpose(1, 0, 2, 3).reshape(N_DEV, -1, d)]
    names = ["fox_w_in", "fox_w_out", "ple_w_up", "ple_w_gate"]
    slabs = [s.reshape((N_CHIPS, 2) + s.shape[1:]) for s in slabs]
    from_sibling = _core_exchange(slabs)
    core = lax.axis_index("c").astype(jnp.int32).reshape(1)
    chip_sums = [_core_sum(s, f, core, "core_sum_" + nm) for s, f, nm in zip(slabs, from_sibling, names)]
    recv = _chip_exchange(chip_sums)
    dil_landed = _push_wait(dil_push[0], dil_push[1], dil_push[2], dil_push[3], grad_x, True, "scatter_dil_wait")
    me = 4 * lax.axis_index("x") + 2 * lax.axis_index("y") + lax.axis_index("c")
    recv = list(recv) + [_fill_own(zone, lax.dynamic_index_in_dim(own, me, 0, keepdims=False))
                         for zone, own in zip(dil_landed, dil_slabs)]
    names += ["dil_w_in", "dil_w_out"]
    big = [(fox_w_in, m_fox_w_in, v_fox_w_in), (fox_w_out, m_fox_w_out, v_fox_w_out),
           (ple_w_up, m_ple_w_up, v_ple_w_up), (ple_w_gate, m_ple_w_gate, v_ple_w_gate),
           (dil_w_in, m_dil_w_in, v_dil_w_in), (dil_w_out, m_dil_w_out, v_dil_w_out)]
    upd = {}
    for rv, (w, m, v), nm in zip(recv, big, names):
        shp2 = rv.shape[1:]
        res = _adamw(rv, w.reshape(shp2), m.reshape(shp2), v.reshape(shp2), "adamw_" + nm)
        upd[nm] = [a.reshape(w.shape) for a in res]

    loss_row = jnp.where(jnp.arange(LANES) == 0, loss_part, 0.0)
    vec = jnp.concatenate([_rows8(d_fox_norm), _rows8(d_final_norm), _rows8(d_dil_norm), d_b_f, loss_row,
                           jnp.zeros((VEC_ROWS - 26, LANES), F32)], axis=0)

    def small_pack(a_fox_norm, a_final_norm, a_dil_norm, a_b_f):
        return jnp.concatenate([_rows8(a_fox_norm), _rows8(a_final_norm), _pad_rows(a_dil_norm, 8),
                                _pad_rows(a_b_f, 8)], axis=0)

    sg, sd, sm, sv = _small_allreduce_adamw(
        vec, small_pack(fox_norm, final_norm, dil_norm, fox_b_f),
        small_pack(m_fox_norm, m_final_norm, m_dil_norm, m_fox_b_f),
        small_pack(v_fox_norm, v_final_norm, v_dil_norm, v_fox_b_f))

    def small_unpack(a):
        return {"fox_norm": a[0:8].reshape(1, d), "final_norm": a[8:16].reshape(d), "dil_norm": a[16:17],
                "fox_b_f": a[24:25, :FOX_HEADS]}

    loss = sg[25, 0]
    order = ["fox_norm", "fox_w_in", "fox_b_f", "fox_w_out", "dil_norm", "dil_w_in", "dil_w_out", "ple_w_up",
             "ple_w_gate", "final_norm"]
    out = [loss, grad_x[None]]
    for idx, small in enumerate((sg, sd, sm, sv)):
        sp = small_unpack(small)
        out += [sp[nm] if nm in sp else upd[nm][idx] for nm in order]
    return tuple(out)
```

```python
import functools

import jax
import jax.numpy as jnp
from jax import lax
from jax.experimental import pallas as pl
from jax.experimental.pallas import tpu as pltpu

F32 = jnp.float32
BF16 = jnp.bfloat16
SDS = jax.ShapeDtypeStruct

D_MODEL = 1024
N_DEV = 8
LANES = 128
FOX_HEADS = 16
FOX_HEAD_DIM = 64
FOX_PAIRS = FOX_HEADS // 2
DIL_HEADS = 8
DIL_BLOCK = 128
DIL_PATTERN = ((128, 1), (512, 4), (2048, 16))
ALIBI_MAX_EXP = 8.0
RMS_EPS = 1e-6
ADAM_LR, ADAM_B1, ADAM_B2, ADAM_EPS, ADAM_WD, ADAM_STEP = 0.001, 0.9, 0.999, 1e-08, 0.01, 10
VMEM_LIMIT = 48 * 1024 * 1024
NEG_INF = float("-inf")

NN = (((1,), (0,)), ((), ()))
NT = (((1,), (1,)), ((), ()))
TN = (((0,), (0,)), ((), ()))
MESH = pl.DeviceIdType.MESH


def _pcall(body, **kw):
    return pl.pallas_call(body, **kw)


def _params(**kw):
    return pltpu.CompilerParams(vmem_limit_bytes=VMEM_LIMIT, **kw)


def _dot(a, b, dims):
    return lax.dot_general(a, b, dims, preferred_element_type=F32)


def _sigmoid(x):
    return 1.0 / (1.0 + jnp.exp(-x))


def _mm(a_parts, b, mode, name, tiles=(512, 1024, 1024), extras=(), outs=None, epilogue=None, out_dtype=BF16,
        b_kmap=None, b_sub=None):
    na = len(a_parts)
    stack = [a.shape[0] if a.ndim == 3 else 1 for a in a_parts]
    first = [sum(stack[:s]) for s in range(na)]
    if mode == "tn":
        k_part, m = a_parts[0].shape
        n = b.shape[-1]
    else:
        m, k_part = a_parts[0].shape[-2:]
        n = b.shape[1] if mode == "nn" else b.shape[0]
    tm, tn, tk = min(tiles[0], m), min(tiles[1], n), min(tiles[2], k_part)
    kb = k_part // tk
    nk = sum(stack) * kb
    grid = (m // tm, n // tn, nk)
    b_kmap = b_kmap or (lambda k: k)

    in_specs = []
    for s in range(na):
        if mode == "tn":
            in_specs.append(pl.BlockSpec((tk, tm), lambda i, j, k: (k, i)))
            continue

        def rel(k, s=s):
            return jnp.clip(k - first[s] * kb, 0, stack[s] * kb - 1)

        if a_parts[s].ndim == 3:
            in_specs.append(pl.BlockSpec((None, tm, tk), lambda i, j, k, rel=rel: (rel(k) // kb, i, rel(k) % kb)))
        else:
            in_specs.append(pl.BlockSpec((tm, tk), lambda i, j, k, rel=rel: (i, rel(k))))
    if mode == "nt":
        in_specs.append(pl.BlockSpec((tn, tk), lambda i, j, k: (j, b_kmap(k))))
    elif b_sub is not None:
        in_specs.append(pl.BlockSpec((None, tk, tn), lambda i, j, k: (b_sub, k, j)))
    else:
        in_specs.append(pl.BlockSpec((tk, tn), lambda i, j, k: (b_kmap(k), j)))
    for _, blk, imap in extras:
        in_specs.append(pl.BlockSpec(blk, imap))
    if outs is None:
        outs = [(SDS((m, n), out_dtype), (tm, tn), lambda i, j, k: (i, j))]
    out_specs = [pl.BlockSpec(blk, imap) for _, blk, imap in outs]
    ne, no = len(extras), len(outs)
    dims = {"nn": NN, "nt": NT, "tn": TN}[mode]

    def finish(res, e_refs, o_refs, i):
        if epilogue is None:
            o_refs[0][...] = res.astype(o_refs[0].dtype)
        else:
            epilogue(res, e_refs, o_refs, i)

    def body(*refs):
        a_refs = refs[:na]
        b_ref = refs[na]
        e_refs = refs[na + 1:na + 1 + ne]
        o_refs = refs[na + 1 + ne:na + 1 + ne + no]
        i, k = pl.program_id(0), pl.program_id(2)
        if nk == 1:
            finish(_dot(a_refs[0][...].astype(BF16), b_ref[...].astype(BF16), dims), e_refs, o_refs, i)
            return
        acc = refs[-1]

        @pl.when(k == 0)
        def _():
            acc[...] = jnp.zeros_like(acc)

        def step(a_ref):
            acc[...] += _dot(a_ref[...].astype(BF16), b_ref[...].astype(BF16), dims)

        for s in range(na):
            if na == 1:
                step(a_refs[0])
            else:
                in_use = (k >= first[s] * kb) & (k < (first[s] + stack[s]) * kb)
                pl.when(in_use)(functools.partial(step, a_refs[s]))

        @pl.when(k == nk - 1)
        def _():
            finish(acc[...], e_refs, o_refs, i)

    res = _pcall(
        body, name=name, grid=grid, in_specs=in_specs, out_specs=out_specs,
        out_shape=[o[0] for o in outs], scratch_shapes=[] if nk == 1 else [pltpu.VMEM((tm, tn), F32)],
        compiler_params=_params(dimension_semantics=("arbitrary", "arbitrary", "arbitrary")),
    )(*a_parts, b, *[e[0] for e in extras])
    return res[0] if len(res) == 1 else res


IN_PROJ_TILES = (1024, 1024, 1024)
DW_TILES = (1024, 1024, 1024)


def _dw(x, dy, name, sub=None):
    return _mm([x], dy, "tn", name, tiles=DW_TILES, b_sub=sub)


def _add_extra_epilogue(acc, e_refs, o_refs, i):
    o_refs[0][...] = acc + e_refs[0][...]


def _mm_residual(a, b, mode, res, name):
    m = a.shape[0]
    n = b.shape[1] if mode == "nn" else b.shape[0]
    tm, tn = 512, 1024
    return _mm([a], b, mode, name, tiles=(tm, tn, 1024),
               extras=[(res, (tm, tn), lambda i, j, k: (i, j))],
               outs=[(SDS((m, n), F32), (tm, tn), lambda i, j, k: (i, j))],
               epilogue=_add_extra_epilogue)


def _rms_fwd(h, g, name):
    t, d = h.shape
    tm = 512

    def body(h_ref, g_ref, n_ref, r_ref):
        x = h_ref[...]
        r = lax.rsqrt(jnp.mean(x * x, axis=-1, keepdims=True) + RMS_EPS)
        n_ref[...] = ((x * r) * g_ref[...]).astype(BF16)
        r_ref[...] = r

    return _pcall(
        body, name=name, grid=(t // tm,),
        in_specs=[pl.BlockSpec((tm, d), lambda i: (i, 0)), pl.BlockSpec((1, d), lambda i: (0, 0))],
        out_specs=[pl.BlockSpec((tm, d), lambda i: (i, 0)), pl.BlockSpec((tm, 1), lambda i: (i, 0))],
        out_shape=[SDS((t, d), BF16), SDS((t, 1), F32)],
        compiler_params=_params(),
    )(h, g)


def _rms_bwd_rows(dn, x, g, r):
    xhat = x * r
    dxhat = dn * g
    dx = r * (dxhat - xhat * jnp.mean(dxhat * xhat, axis=-1, keepdims=True))
    dg = jnp.sum(dn * xhat, axis=0, keepdims=True)
    return dx, dg


def _mm_in_bwd(d_parts, w, h, g, r, dres, name, more=None, w_kmap=None):
    t = h.shape[0]
    tm = 512
    tk = D_MODEL
    row = lambda i, j, k: (i, 0)
    extras = [(h, (tm, D_MODEL), row), (g, (1, D_MODEL), lambda i, j, k: (0, 0)), (r, (tm, 1), row),
              (dres, (tm, D_MODEL), row)]
    if more is not None:
        extras.append((more, (tm, D_MODEL), row))

    def epilogue(acc, e_refs, o_refs, i):
        dn = acc if more is None else acc + e_refs[4][...]
        dx, dg = _rms_bwd_rows(dn, e_refs[0][...], e_refs[1][...], e_refs[2][...])
        o_refs[0][...] = e_refs[3][...] + dx

        @pl.when(i == 0)
        def _():
            o_refs[1][...] = dg

        @pl.when(i > 0)
        def _():
            o_refs[1][...] += dg

    return _mm(d_parts, w, "nt", name, tiles=(tm, D_MODEL, tk), extras=extras,
               outs=[(SDS((t, D_MODEL), F32), (tm, D_MODEL), row),
                     (SDS((1, D_MODEL), F32), (1, D_MODEL), lambda i, j, k: (0, 0))],
               epilogue=epilogue, b_kmap=w_kmap)


def _final_bwd(h, g, tgt):
    t, d = h.shape
    tm = 256

    def body(h_ref, g_ref, t_ref, dh_ref, dg_ref, loss_ref):
        i = pl.program_id(0)
        x = h_ref[...]
        gg = g_ref[...]
        r = lax.rsqrt(jnp.mean(x * x, axis=-1, keepdims=True) + RMS_EPS)
        err = (x * r) * gg - t_ref[...]
        part = 0.5 * jnp.sum(jnp.mean(err * err, axis=-1, keepdims=True), axis=0, keepdims=True)
        dx, dg = _rms_bwd_rows(err * (1.0 / d), x, gg, r)
        dh_ref[...] = dx

        @pl.when(i == 0)
        def _():
            dg_ref[...] = dg
            loss_ref[...] = jnp.broadcast_to(part, loss_ref.shape)

        @pl.when(i > 0)
        def _():
            dg_ref[...] += dg
            loss_ref[...] += jnp.broadcast_to(part, loss_ref.shape)

    return _pcall(
        body, name="final_norm_loss", grid=(t // tm,),
        in_specs=[pl.BlockSpec((tm, d), lambda i: (i, 0)), pl.BlockSpec((1, d), lambda i: (0, 0)),
                  pl.BlockSpec((tm, d), lambda i: (i, 0))],
        out_specs=[pl.BlockSpec((tm, d), lambda i: (i, 0)), pl.BlockSpec((1, d), lambda i: (0, 0)),
                   pl.BlockSpec((1, LANES), lambda i: (0, 0))],
        out_shape=[SDS((t, d), F32), SDS((1, d), F32), SDS((1, LANES), F32)],
        compiler_params=_params(),
    )(h, g, tgt)


GATE_ROWS = 256


def _split3(x):
    hi = x.astype(BF16)
    r1 = x - hi.astype(F32)
    mid = r1.astype(BF16)
    lo = (r1 - mid.astype(F32)).astype(BF16)
    return hi, mid, lo


def _tri_sum(x, upper):
    rows = x.shape[0]
    ri = lax.broadcasted_iota(jnp.int32, (rows, rows), 0)
    ci = lax.broadcasted_iota(jnp.int32, (rows, rows), 1)
    tri = jnp.where((ri <= ci) if upper else (ri >= ci), 1.0, 0.0).astype(BF16)
    hi, mid, lo = _split3(x)
    return _dot(tri, hi, NN) + _dot(tri, mid, NN) + _dot(tri, lo, NN)


def _log_sigmoid(x):
    return jnp.minimum(x, 0.0) - jnp.log1p(jnp.exp(-jnp.abs(x)))


def _fox_gate_fwd(projf, bpad):
    t = projf.shape[0]
    tb = GATE_ROWS

    def body(x_ref, b_ref, c_ref, carry):
        i = pl.program_id(0)

        @pl.when(i == 0)
        def _():
            carry[...] = jnp.zeros_like(carry)

        c_ref[...] = _tri_sum(_log_sigmoid(x_ref[...] + b_ref[...]), upper=False) + carry[...]
        carry[...] = c_ref[pl.ds(tb - 1, 1), :]

    return _pcall(
        body, name="fox_gate_fwd", grid=(t // tb,),
        in_specs=[pl.BlockSpec((tb, LANES), lambda i: (i, 0)), pl.BlockSpec((1, LANES), lambda i: (0, 0))],
        out_specs=pl.BlockSpec((tb, LANES), lambda i: (i, 0)),
        out_shape=SDS((t, LANES), F32), scratch_shapes=[pltpu.VMEM((1, LANES), F32)],
        compiler_params=_params(),
    )(projf, bpad)


def _fox_gate_bwd(projf, bpad, dc_query, dc_key):
    t = projf.shape[0]
    tb = GATE_ROWS
    nb = t // tb

    def body(x_ref, b_ref, dcq_ref, dck_ref, df_ref, db_ref, carry, buf):
        i = pl.program_id(0)

        @pl.when(i == 0)
        def _():
            carry[...] = jnp.zeros_like(carry)

        buf[...] = _tri_sum(dcq_ref[...] - dck_ref[...], upper=True) + carry[...]
        carry[...] = buf[pl.ds(0, 1), :]
        df = buf[...] * _sigmoid(-(x_ref[...] + b_ref[...]))
        df_ref[...] = df.astype(BF16)
        part = jnp.sum(df, axis=0, keepdims=True)

        @pl.when(i == 0)
        def _():
            db_ref[...] = part

        @pl.when(i > 0)
        def _():
            db_ref[...] += part

    rev = lambda i: (nb - 1 - i, 0)
    return _pcall(
        body, name="fox_gate_bwd", grid=(nb,),
        in_specs=[pl.BlockSpec((tb, LANES), rev), pl.BlockSpec((1, LANES), lambda i: (0, 0)),
                  pl.BlockSpec((tb, LANES), rev), pl.BlockSpec((tb, LANES), rev)],
        out_specs=[pl.BlockSpec((tb, LANES), rev), pl.BlockSpec((1, LANES), lambda i: (0, 0))],
        out_shape=[SDS((t, LANES), BF16), SDS((1, LANES), F32)],
        scratch_shapes=[pltpu.VMEM((1, LANES), F32), pltpu.VMEM((tb, LANES), F32)],
        compiler_params=_params(),
    )(projf, bpad, dc_query, dc_key)


FOX_TQ = 1024
FOX_TQ_FWD = 1024
FOX_SCALE = FOX_HEAD_DIM ** -0.5


def _low_lanes(shape):
    return lax.broadcasted_iota(jnp.int32, shape, len(shape) - 1) < FOX_HEAD_DIM


FOX_AUG = 3
FOX_CHAIN = 256
FOX_SUM_ROWS = 8


def _top_rows(shape):
    return lax.broadcasted_iota(jnp.int32, shape, 0) < FOX_HEAD_DIM


def _fox_aug(a, b, sign_a, sign_b, piece_entry, ones_entry, name):
    t = a.shape[0]
    tb = 512

    def body(a_ref, b_ref, o_ref):
        x = sign_a * a_ref[...]
        if sign_b != 0.0:
            x = x + sign_b * b_ref[...]
        head = lax.broadcasted_iota(jnp.int32, (LANES, D_MODEL), 0)
        col = lax.broadcasted_iota(jnp.int32, (LANES, D_MODEL), 1)
        base = (head // 2) * LANES + (1 - head % 2) * FOX_HEAD_DIM + piece_entry
        acc = jnp.zeros((tb, D_MODEL), F32)
        for e, piece in enumerate(_split3(x)):
            place = jnp.where((head < FOX_HEADS) & (col == base + e), 1.0, 0.0).astype(BF16)
            acc = acc + _dot(piece, place, NN)
        if ones_entry is not None:
            ent = lax.broadcasted_iota(jnp.int32, (1, D_MODEL), 1) % FOX_HEAD_DIM
            acc = acc + jnp.where((ent >= ones_entry) & (ent < ones_entry + FOX_AUG), 1.0, 0.0)
        o_ref[...] = acc.astype(BF16)

    blk = pl.BlockSpec((tb, LANES), lambda i: (i, 0))
    return _pcall(
        body, name=name, grid=(t // tb,), in_specs=[blk, blk],
        out_specs=pl.BlockSpec((tb, D_MODEL), lambda i: (i, 0)), out_shape=SDS((t, D_MODEL), BF16),
        compiler_params=_params(),
    )(a, b)


def _fox_unpack_dc(dck_wide, dcq):
    t = dck_wide.shape[0]
    dck = dck_wide.reshape(t, FOX_PAIRS, 2, FOX_HEAD_DIM)[:, :, ::-1, 0].reshape(t, FOX_HEADS)
    return dcq[:, :, 0, :].reshape(FOX_HEADS, t).T, dck


def _causal_steps(nq, key_major):
    if key_major:
        pairs = [(i, j) for j in range(nq) for i in range(j, nq)]
    else:
        pairs = [(i, j) for i in range(nq) for j in range(i + 1)]
    return (jnp.asarray([p[0] for p in pairs], jnp.int32), jnp.asarray([p[1] for p in pairs], jnp.int32))


def _pair_operand(low, own, other, hh):
    return jnp.where(low, own, other) if hh == 0 else jnp.where(low, other, own)


def _fox_fwd(proj, qaug, kaug):
    t = proj.shape[0]
    tq = tk = min(FOX_TQ_FWD, t)
    nq = t // tq
    cb = D_MODEL // LANES
    half = min(FOX_CHAIN, tq)

    i_tab, j_tab = _causal_steps(nq, key_major=False)

    def body(i_ref, j_ref, q_ref, k_ref, v_ref, z_ref, qa_ref, ka_ref, o_ref, g_ref, lse_ref, m_s, l_s, acc_s):
        step = pl.program_id(1)
        i, j = i_ref[step], j_ref[step]

        @pl.when(j == 0)
        def _():
            m_s[...] = jnp.full_like(m_s, NEG_INF)
            l_s[...] = jnp.zeros_like(l_s)
            acc_s[...] = jnp.zeros_like(acc_s)

        low = _low_lanes((tq, LANES))
        top = _top_rows((LANES, tq))

        def update(masked):
            qs = q_ref[...] * FOX_SCALE
            qa, k, ka, v = qa_ref[...], k_ref[...], ka_ref[...], v_ref[...]
            if masked:
                causal = (lax.broadcasted_iota(jnp.int32, (tk, tq), 0) <= lax.broadcasted_iota(jnp.int32, (tk, tq), 1))
            one = jnp.ones_like(v)
            chains = [(hh, slice(c * half, (c + 1) * half)) for hh in range(2) for c in range(tq // half)]
            qh = [_pair_operand(low, qs, qa, hh) for hh in range(2)]
            kh = [_pair_operand(low, k, ka, hh) for hh in range(2)]
            vh = [_pair_operand(low, v, one, hh) for hh in range(2)]
            keys = lambda cols: slice(0, cols.stop) if masked else slice(None)
            scores = [_dot(kh[hh][keys(cols), :], qh[hh][cols, :], NT) for hh, cols in chains]
            for (hh, cols), s in zip(chains, scores):
                if masked:
                    s = jnp.where(causal[keys(cols), cols], s, NEG_INF)
                m_prev = m_s[hh, :, cols]
                m_new = jnp.maximum(m_prev, jnp.max(s, axis=0, keepdims=True))
                alpha = jnp.exp(m_prev - m_new)
                pv = _dot(vh[hh][keys(cols), :], jnp.exp(s - m_new).astype(BF16), TN)
                sums = pv[FOX_HEAD_DIM:FOX_HEAD_DIM + FOX_SUM_ROWS, :] if hh == 0 else pv[0:FOX_SUM_ROWS, :]
                l_s[hh, :, cols] = alpha * l_s[hh, :, cols] + sums
                m_s[hh, :, cols] = m_new
                own = top[:, cols] if hh == 0 else jnp.logical_not(top[:, cols])
                acc_s[:, cols] = jnp.where(own, acc_s[:, cols] * alpha + pv, acc_s[:, cols])

        pl.when(j < i)(functools.partial(update, False))
        pl.when(j == i)(functools.partial(update, True))

        @pl.when(j == i)
        def _():
            o = (acc_s[...] / jnp.where(top, l_s[0, 0:1, :], l_s[1, 0:1, :])).T
            z = z_ref[...].astype(F32)
            o_ref[...] = o.astype(BF16)
            g_ref[...] = (o * (z * _sigmoid(z))).astype(BF16)
            for hh in range(2):
                lse_ref[hh] = m_s[hh] + jnp.log(l_s[hh, 0:1, :])

    qblk = lambda col: pl.BlockSpec((tq, LANES), lambda h, s, it, jt: (it[s], col + h))
    kblk = lambda col: pl.BlockSpec((tk, LANES), lambda h, s, it, jt: (jt[s], col + h))
    return _pcall(
        body, name="fox_attn_fwd",
        grid_spec=pltpu.PrefetchScalarGridSpec(
            num_scalar_prefetch=2, grid=(FOX_PAIRS, i_tab.shape[0]),
            in_specs=[qblk(0), kblk(cb), kblk(2 * cb), qblk(3 * cb), qblk(0), kblk(0)],
            out_specs=[qblk(0), qblk(0), pl.BlockSpec((2, 1, tq), lambda h, s, it, jt: (h, 0, it[s]))],
            scratch_shapes=[pltpu.VMEM((2, 1, tq), F32), pltpu.VMEM((2, FOX_SUM_ROWS, tq), F32),
                            pltpu.VMEM((LANES, tq), F32)]),
        out_shape=[SDS((t, D_MODEL), BF16), SDS((t, D_MODEL), BF16), SDS((FOX_HEADS, 1, t), F32)],
        compiler_params=_params(dimension_semantics=("arbitrary", "arbitrary")),
    )(i_tab, j_tab, proj, proj, proj, proj, qaug, kaug)


def _fox_bwd(proj, do, qaug, kaug, doaug):
    t = proj.shape[0]
    tq = tk = min(FOX_TQ, t)
    nq = t // tq
    cb = D_MODEL // LANES
    half = min(FOX_CHAIN, tq)

    i_tab, j_tab = _causal_steps(nq, key_major=True)

    def body(i_ref, j_ref, q_ref, k_ref, v_ref, do_ref, qa_ref, ka_ref, da_ref,
             dq_ref, dk_ref, dv_ref, dck_ref, dcq_ref, dq_acc, dcq_acc, dk_acc, dks_acc, dv_acc):
        step = pl.program_id(1)
        i, j = i_ref[step], j_ref[step]
        low = _low_lanes((tq, LANES))
        top = _top_rows((LANES, tq))

        @pl.when(i == j)
        def _():
            dk_acc[...] = jnp.zeros_like(dk_acc)
            dks_acc[...] = jnp.zeros_like(dks_acc)
            dv_acc[...] = jnp.zeros_like(dv_acc)

        def update(masked):
            qs = q_ref[...] * FOX_SCALE
            k, v, dout = k_ref[...], v_ref[...], do_ref[...]
            qa, ka, da = qa_ref[...], ka_ref[...], da_ref[...]
            lane = lax.broadcasted_iota(jnp.int32, (tk, LANES), 1)
            vone = jnp.where((lane & (FOX_HEAD_DIM - 1)) < FOX_AUG, 1.0, 0.0).astype(v.dtype)
            one = jnp.ones_like(k)
            if masked:
                causal = (lax.broadcasted_iota(jnp.int32, (tk, tq), 0) <= lax.broadcasted_iota(jnp.int32, (tk, tq), 1))
            parts = []
            kh = [_pair_operand(low, k, ka, hh) for hh in range(2)]
            qh = [_pair_operand(low, qs, qa, hh) for hh in range(2)]
            vh = [_pair_operand(low, v, vone, hh) for hh in range(2)]
            doh = [_pair_operand(low, dout, da, hh) for hh in range(2)]
            q1 = [_pair_operand(low, qs, one, hh) for hh in range(2)]
            k1 = [_pair_operand(low, k, one, hh) for hh in range(2)]

            def tile(hh, keys, cols, s, dp):
                if masked:
                    s = jnp.where(causal[keys, cols], s, NEG_INF)
                p = jnp.exp(s)
                pb, dsb = p.astype(BF16), (p * dp).astype(BF16)
                return (_dot(pb, dout[cols, :], NN), _dot(dsb, q1[hh][cols, :], NN), _dot(k1[hh][keys, :], dsb, TN))

            if not masked:
                scores = [_dot(kh[hh], qh[hh], NT) for hh in range(2)]
                dps = [_dot(vh[hh], doh[hh], NT) for hh in range(2)]
                everything = slice(None)
                parts = [tile(hh, everything, everything, scores[hh], dps[hh]) for hh in range(2)]
            else:
                for hh in range(2):
                    dv_h, dk_h, dq_h = jnp.zeros((tk, LANES), F32), jnp.zeros((tk, LANES), F32), []
                    for c in range(tq // half):
                        cols, keys = slice(c * half, (c + 1) * half), slice(0, (c + 1) * half)
                        dv_c, dk_c, dq_c = tile(hh, keys, cols, _dot(kh[hh][keys, :], qh[hh][cols, :], NT),
                                                _dot(vh[hh][keys, :], doh[hh][cols, :], NT))
                        below = ((0, tk - keys.stop), (0, 0))
                        dv_h, dk_h = dv_h + jnp.pad(dv_c, below), dk_h + jnp.pad(dk_c, below)
                        dq_h.append(dq_c)
                    parts.append((dv_h, dk_h, jnp.concatenate(dq_h, axis=1)))
            dv_acc[...] += jnp.where(low, parts[0][0], parts[1][0])
            dk_acc[...] += jnp.where(low, parts[0][1], parts[1][1])
            dks_acc[...] += jnp.where(low, parts[1][1], parts[0][1])
            dq_t = jnp.where(top, parts[0][2], parts[1][2]) * FOX_SCALE
            sum_a = parts[0][2][FOX_HEAD_DIM:FOX_HEAD_DIM + FOX_SUM_ROWS, :]
            sum_b = parts[1][2][0:FOX_SUM_ROWS, :]

            @pl.when(j == 0)
            def _():
                dq_acc[i] = dq_t
                dcq_acc[0, i] = sum_a
                dcq_acc[1, i] = sum_b

            @pl.when(j > 0)
            def _():
                dq_acc[i] += dq_t
                dcq_acc[0, i] += sum_a
                dcq_acc[1, i] += sum_b

        pl.when(i > j)(functools.partial(update, False))
        pl.when(i == j)(functools.partial(update, True))

        @pl.when(i == nq - 1)
        def _():
            dk_ref[...] = dk_acc[...].astype(BF16)
            dv_ref[...] = dv_acc[...].astype(BF16)
            dck_ref[...] = dks_acc[...]

        @pl.when((i == nq - 1) & (j == nq - 1))
        def _():
            for blk in range(nq):
                dq_ref[blk * tq:(blk + 1) * tq, :] = dq_acc[blk].T.astype(BF16)
            dcq_ref[...] = dcq_acc[...]

    qblk = lambda col: pl.BlockSpec((tq, LANES), lambda h, s, it, jt: (it[s], col + h))
    kblk = lambda col: pl.BlockSpec((tk, LANES), lambda h, s, it, jt: (jt[s], col + h))
    return _pcall(
        body, name="fox_attn_bwd",
        grid_spec=pltpu.PrefetchScalarGridSpec(
            num_scalar_prefetch=2, grid=(FOX_PAIRS, i_tab.shape[0]),
            in_specs=[qblk(0), kblk(cb), kblk(2 * cb), qblk(0), qblk(0), kblk(0), qblk(0)],
            out_specs=[pl.BlockSpec((t, LANES), lambda h, s, it, jt: (0, h)), kblk(0), kblk(0), kblk(0),
                       pl.BlockSpec((2, nq, FOX_SUM_ROWS, tq), lambda h, s, it, jt: (h, 0, 0, 0))],
            scratch_shapes=[pltpu.VMEM((nq, LANES, tq), F32), pltpu.VMEM((2, nq, FOX_SUM_ROWS, tq), F32),
                            pltpu.VMEM((tk, LANES), F32), pltpu.VMEM((tk, LANES), F32), pltpu.VMEM((tk, LANES), F32)]),
        out_shape=[SDS((t, D_MODEL), BF16), SDS((t, D_MODEL), BF16), SDS((t, D_MODEL), BF16),
                   SDS((t, D_MODEL), F32), SDS((FOX_HEADS, nq, FOX_SUM_ROWS, tq), F32)],
        compiler_params=_params(dimension_semantics=("arbitrary", "arbitrary")),
    )(i_tab, j_tab, proj, proj, proj, do, qaug, kaug, doaug)


def _mm_gate_bwd(dh, w_out, z_src, z_col0, o, heads, name):
    t = dh.shape[0]
    tm = 512
    row = lambda i, j, k: (i, 0)
    zcb = z_col0 // D_MODEL

    def epilogue(acc, e_refs, o_refs, i):
        z = e_refs[0][...].astype(F32)
        ov = e_refs[1][...].astype(F32)
        sg = _sigmoid(z)
        dout = acc * (z * sg)
        o_refs[0][...] = dout.astype(BF16)
        o_refs[1][...] = (acc * ov * (sg * (1.0 + z * (1.0 - sg)))).astype(BF16)
        prod = dout * ov
        for cbk in range(D_MODEL // LANES):
            seg = prod[:, cbk * LANES:(cbk + 1) * LANES]
            tot = jnp.sum(seg, axis=-1, keepdims=True)
            if heads == D_MODEL // LANES:
                o_refs[2][cbk] = tot
            else:
                lo = jnp.sum(jnp.where(_low_lanes(seg.shape), seg, 0.0), axis=-1, keepdims=True)
                o_refs[2][2 * cbk] = lo
                o_refs[2][2 * cbk + 1] = tot - lo

    return _mm([dh], w_out, "nt", name, tiles=(tm, D_MODEL, D_MODEL),
               extras=[(z_src, (tm, D_MODEL), lambda i, j, k: (i, zcb)), (o, (tm, D_MODEL), row)],
               outs=[(SDS((t, D_MODEL), BF16), (tm, D_MODEL), row), (SDS((t, D_MODEL), BF16), (tm, D_MODEL), row),
                     (SDS((heads, t, 1), F32), (heads, tm, 1), lambda i, j, k: (0, i, 0))],
               epilogue=epilogue)


def _ple_fwd(h, pin, w_up, w_gate, name):
    t = h.shape[0]
    tm = 512
    pd = pin.shape[1]

    def body(h_ref, p_ref, wu_ref, wg_ref, hn_ref, u_ref, a_ref):
        h = h_ref[...]
        u = _dot(p_ref[...].astype(BF16), wu_ref[...], NN)
        a = _dot(h.astype(BF16), wg_ref[...], NN)
        hn_ref[...] = h + u * _sigmoid(a)
        u_ref[...] = u.astype(BF16)
        a_ref[...] = a.astype(BF16)

    rows = pl.BlockSpec((tm, D_MODEL), lambda i: (i, 0))
    return _pcall(
        body, name=name, grid=(t // tm,),
        in_specs=[rows, pl.BlockSpec((tm, pd), lambda i: (i, 0)),
                  pl.BlockSpec((pd, D_MODEL), lambda i: (0, 0)), pl.BlockSpec((D_MODEL, D_MODEL), lambda i: (0, 0))],
        out_specs=[rows, rows, rows],
        out_shape=[SDS((t, D_MODEL), F32), SDS((t, D_MODEL), BF16), SDS((t, D_MODEL), BF16)],
        compiler_params=_params(),
    )(h, pin, w_up, w_gate)


def _ple_bwd_elem(dh, u, a, name):
    t = dh.shape[0]
    tm = 512

    def body(dh_ref, u_ref, a_ref, du_ref, da_ref):
        g = dh_ref[...]
        s = _sigmoid(a_ref[...].astype(F32))
        du_ref[...] = (g * s).astype(BF16)
        da_ref[...] = (g * u_ref[...].astype(F32) * (s * (1.0 - s))).astype(BF16)

    blk = pl.BlockSpec((tm, D_MODEL), lambda i: (i, 0))
    return _pcall(
        body, name=name, grid=(t // tm,), in_specs=[blk, blk, blk], out_specs=[blk, blk],
        out_shape=[SDS((t, D_MODEL), BF16), SDS((t, D_MODEL), BF16)], compiler_params=_params(),
    )(dh, u, a)


DIL_SCALE = LANES ** -0.5


def _dil_masks():
    ii = lax.broadcasted_iota(jnp.int32, (DIL_BLOCK, DIL_BLOCK), 0)
    jj = lax.broadcasted_iota(jnp.int32, (DIL_BLOCK, DIL_BLOCK), 1)
    return ii, jj


DIL_UNITS = 16
BNT = (((2,), (2,)), ((0,), (0,)))
BNN = (((2,), (1,)), ((0,), (0,)))
BTN = (((1,), (1,)), ((0,), (0,)))


def _dil_units(dil):
    return [(b, r) for b in range(DIL_UNITS // dil) for r in range(dil)]


def _unit_rows(b, r, dil):
    return pl.ds(b * DIL_BLOCK * dil + r, DIL_BLOCK, stride=dil)


def _gather_units(cur, dil, shift=0, edge=None, lead=()):
    nbk = DIL_UNITS // dil
    parts = []
    for b, r in _dil_units(dil):
        bb = b + shift
        if 0 <= bb < nbk:
            parts.append(cur[lead + (_unit_rows(bb, r, dil), slice(None))])
        else:
            parts.append(edge[lead + (pl.ds(r, DIL_BLOCK, stride=dil), slice(None))])
    return jnp.stack(parts)


def _scatter_units(dst, val, dil):
    for u, (b, r) in enumerate(_dil_units(dil)):
        dst[_unit_rows(b, r, dil), :] = val[u]


def _dil_bias(slope, prev):
    ii, jj = _dil_masks()
    dist = (DIL_BLOCK + ii - jj) if prev else (ii - jj)
    return (slope * dist.astype(F32))[None], ((jj >= ii) if prev else (jj <= ii))[None]


def _dil_fwd(proj, slopes, grp, dil, name):
    t = proj.shape[0]
    rows = DIL_BLOCK * DIL_UNITS
    edge_rows = DIL_BLOCK * dil
    nbk = DIL_UNITS // dil
    nsb = t // rows
    qc, kc_, vc_ = grp * DIL_HEADS, 3 * DIL_HEADS + grp * DIL_HEADS, 6 * DIL_HEADS + grp * DIL_HEADS

    def body(q_ref, kp_ref, kc_ref, vp_ref, vc_ref, sl_ref, o_ref, lse_ref, qf, kpf, kcf, vpf, vcf, of, lf):
        m = pl.program_id(1)
        for src, dst in ((q_ref, qf), (kp_ref, kpf), (kc_ref, kcf), (vp_ref, vpf), (vc_ref, vcf)):
            dst[...] = src[...].astype(F32)
        slope = sl_ref[0]
        unit = lax.broadcasted_iota(jnp.int32, (DIL_UNITS, 1, 1), 0)
        has_prev = (unit >= dil) | (m > 0)
        q = _gather_units(qf, dil).astype(BF16)
        kc, vc = _gather_units(kcf, dil).astype(BF16), _gather_units(vcf, dil).astype(BF16)
        kp, vp = _gather_units(kcf, dil, -1, kpf).astype(BF16), _gather_units(vcf, dil, -1, vpf).astype(BF16)
        bias_p, ok_p = _dil_bias(slope, True)
        bias_c, ok_c = _dil_bias(slope, False)
        sp = jnp.where(ok_p & has_prev, _dot(q, kp, BNT) * DIL_SCALE - bias_p, NEG_INF)
        sc = jnp.where(ok_c, _dot(q, kc, BNT) * DIL_SCALE - bias_c, NEG_INF)
        mx = jnp.maximum(jnp.max(sp, axis=-1, keepdims=True), jnp.max(sc, axis=-1, keepdims=True))
        pp = jnp.exp(sp - mx)
        pc = jnp.exp(sc - mx)
        l = jnp.sum(pp, axis=-1, keepdims=True) + jnp.sum(pc, axis=-1, keepdims=True)
        o = (_dot(pp.astype(BF16), vp, BNN) + _dot(pc.astype(BF16), vc, BNN)) / l
        _scatter_units(of, o, dil)
        _scatter_units(lf, mx + jnp.log(l), dil)
        o_ref[...] = of[...].astype(BF16)
        lse_ref[0] = lf[...]

    cur = lambda col: pl.BlockSpec((rows, LANES), lambda h, m: (m, col + h))
    prev = lambda col: pl.BlockSpec((edge_rows, LANES), lambda h, m: (jnp.maximum(m * nbk - 1, 0), col + h))
    return _pcall(
        body, name=name, grid=(DIL_HEADS, nsb),
        in_specs=[cur(qc), prev(kc_), cur(kc_), prev(vc_), cur(vc_), pl.BlockSpec((1, 1, 1), lambda h, m: (h, 0, 0))],
        out_specs=[pl.BlockSpec((rows, LANES), lambda h, m: (m, h)), pl.BlockSpec((1, rows, 1), lambda h, m: (h, m, 0))],
        out_shape=[SDS((t, D_MODEL), BF16), SDS((DIL_HEADS, t, 1), F32)],
        scratch_shapes=[pltpu.VMEM((rows, LANES), F32), pltpu.VMEM((edge_rows, LANES), F32), pltpu.VMEM((rows, LANES), F32),
                        pltpu.VMEM((edge_rows, LANES), F32), pltpu.VMEM((rows, LANES), F32), pltpu.VMEM((rows, LANES), F32),
                        pltpu.VMEM((rows, 1), F32)],
        compiler_params=_params(),
    )(proj, proj, proj, proj, proj, slopes)


def _dil_mix(outs, lses, proj, z_col0):
    t = proj.shape[0]
    tm = 512
    zcb = z_col0 // LANES
    ng = len(outs)

    def body(*refs):
        o_refs, l_refs, z_ref = refs[:ng], refs[ng:2 * ng], refs[2 * ng]
        om_ref, g_ref, lse_ref = refs[2 * ng + 1:]
        ls = [r[0] for r in l_refs]
        mx = functools.reduce(jnp.maximum, ls)
        es = [jnp.exp(l - mx) for l in ls]
        tot = functools.reduce(jnp.add, es)
        o = functools.reduce(jnp.add, [(e / tot) * r[...].astype(F32) for e, r in zip(es, o_refs)])
        z = z_ref[...].astype(F32)
        om_ref[...] = o.astype(BF16)
        g_ref[...] = (o * (z * _sigmoid(z))).astype(BF16)
        lse_ref[0] = mx + jnp.log(tot)

    tile = pl.BlockSpec((tm, LANES), lambda i, h: (i, h))
    col = pl.BlockSpec((1, tm, 1), lambda i, h: (h, i, 0))
    return _pcall(
        body, name="dil_mix", grid=(t // tm, DIL_HEADS),
        in_specs=[tile] * ng + [col] * ng + [pl.BlockSpec((tm, LANES), lambda i, h: (i, zcb + h))],
        out_specs=[tile, tile, col],
        out_shape=[SDS((t, D_MODEL), BF16), SDS((t, D_MODEL), BF16), SDS((DIL_HEADS, t, 1), F32)],
        compiler_params=_params(),
    )(*outs, *lses, proj)


def _dil_bwd(proj, do, lse, delta, slopes, grp, dil, name):
    t = proj.shape[0]
    rows = DIL_BLOCK * DIL_UNITS
    edge_rows = DIL_BLOCK * dil
    nbk = DIL_UNITS // dil
    nsb = t // rows
    last_edge = t // edge_rows - 1
    qc, kc_, vc_ = grp * DIL_HEADS, 3 * DIL_HEADS + grp * DIL_HEADS, 6 * DIL_HEADS + grp * DIL_HEADS

    def body(q_ref, qn_ref, kp_ref, kc_ref, vp_ref, vc_ref, do_ref, don_ref, l_ref, ln_ref, d_ref, dn_ref, sl_ref,
             dqkv_ref, qf, qnf, kpf, kcf, vpf, vcf, dof, donf, dqf, dkf, dvf):
        m = pl.program_id(1)
        for src, dst in ((q_ref, qf), (qn_ref, qnf), (kp_ref, kpf), (kc_ref, kcf), (vp_ref, vpf), (vc_ref, vcf),
                         (do_ref, dof), (don_ref, donf)):
            dst[...] = src[...].astype(F32)
        slope = sl_ref[0]
        unit = lax.broadcasted_iota(jnp.int32, (DIL_UNITS, 1, 1), 0)
        has_prev = (unit >= dil) | (m > 0)
        has_next = (unit < DIL_UNITS - dil) | (m < nsb - 1)
        b16 = lambda x: x.astype(BF16)
        q, kc, vc, dout = (b16(_gather_units(x, dil)) for x in (qf, kcf, vcf, dof))
        kp, vp = b16(_gather_units(kcf, dil, -1, kpf)), b16(_gather_units(vcf, dil, -1, vpf))
        qn, don = b16(_gather_units(qf, dil, 1, qnf)), b16(_gather_units(dof, dil, 1, donf))
        lrow, drow = _gather_units(l_ref, dil, lead=(0,)), _gather_units(d_ref, dil, lead=(0,))
        lnrow = _gather_units(l_ref, dil, 1, ln_ref, lead=(0,))
        dnrow = _gather_units(d_ref, dil, 1, dn_ref, lead=(0,))
        bias_p, ok_p = _dil_bias(slope, True)
        bias_c, ok_c = _dil_bias(slope, False)
        sp = jnp.where(ok_p & has_prev, _dot(q, kp, BNT) * DIL_SCALE - bias_p, NEG_INF)
        sc = jnp.where(ok_c, _dot(q, kc, BNT) * DIL_SCALE - bias_c, NEG_INF)
        pp = jnp.exp(sp - lrow)
        pc = jnp.exp(sc - lrow)
        dsp = b16(pp * (_dot(dout, vp, BNT) - drow))
        dsc = b16(pc * (_dot(dout, vc, BNT) - drow))
        _scatter_units(dqf, (_dot(dsp, kp, BNN) + _dot(dsc, kc, BNN)) * DIL_SCALE, dil)
        sn = jnp.where(ok_p & has_next, _dot(qn, kc, BNT) * DIL_SCALE - bias_p, NEG_INF)
        pn = jnp.exp(sn - lnrow)
        dsn = b16(pn * (_dot(don, vc, BNT) - dnrow))
        _scatter_units(dkf, (_dot(dsc, q, BTN) + _dot(dsn, qn, BTN)) * DIL_SCALE, dil)
        _scatter_units(dvf, _dot(b16(pc), dout, BTN) + _dot(b16(pn), don, BTN), dil)
        for s, src in enumerate((dqf, dkf, dvf)):
            dqkv_ref[s] = src[...].astype(BF16)

    prev_i = lambda m: jnp.maximum(m * nbk - 1, 0)
    next_i = lambda m: jnp.minimum((m + 1) * nbk, last_edge)
    cur = lambda col: pl.BlockSpec((rows, LANES), lambda h, m: (m, col + h))
    edge = lambda col, f: pl.BlockSpec((edge_rows, LANES), lambda h, m: (f(m), col + h))
    colcur = pl.BlockSpec((1, rows, 1), lambda h, m: (h, m, 0))
    colnext = pl.BlockSpec((1, edge_rows, 1), lambda h, m: (h, next_i(m), 0))
    out_blk = pl.BlockSpec((3, rows, LANES), lambda h, m: (0, m, h))
    big, small = pltpu.VMEM((rows, LANES), F32), pltpu.VMEM((edge_rows, LANES), F32)
    return _pcall(
        body, name=name, grid=(DIL_HEADS, nsb),
        in_specs=[cur(qc), edge(qc, next_i), edge(kc_, prev_i), cur(kc_), edge(vc_, prev_i), cur(vc_),
                  cur(0), edge(0, next_i), colcur, colnext, colcur, colnext,
                  pl.BlockSpec((1, 1, 1), lambda h, m: (h, 0, 0))],
        out_specs=out_blk, out_shape=SDS((3, t, D_MODEL), BF16),
        scratch_shapes=[big, small, small, big, small, big, big, small, big, big, big],
        compiler_params=_params(),
    )(proj, proj, proj, proj, proj, proj, do, do, lse, lse, delta, delta, slopes)


def _mesh_pos():
    x, y, c = lax.axis_index("x"), lax.axis_index("y"), lax.axis_index("c")
    return x, y, c


def _peer(pos, k):
    x, y, c = pos
    px = 1 - x if k & 4 else x
    py = 1 - y if k & 2 else y
    pc = 1 - c if k & 1 else c
    return (px, py, pc), 4 * px + 2 * py + pc


N_CHIPS = 4
CHIP_FLIPS = ((1, 0), (0, 1), (1, 1))


def _other_chips(x, y):
    return [(1 - x if fx else x, 1 - y if fy else y) for fx, fy in CHIP_FLIPS]


def _all_gather(arrays):
    n = len(arrays)
    per = 2 * N_CHIPS - 1
    hbm = pl.BlockSpec(memory_space=pltpu.HBM)

    def body(*refs):
        ins, outs = refs[:n], refs[n:2 * n]
        send_sems, recv_sems, local_sems = refs[2 * n:]
        x, y, c = _mesh_pos()
        sibling = (x, y, 1 - c)
        chips = _other_chips(x, y)
        block = lambda px, py, pc: 4 * px + 2 * py + pc

        def copy(w, k, src, blk, to):
            return pltpu.make_async_remote_copy(
                src_ref=src, dst_ref=outs[w].at[blk], send_sem=send_sems.at[w * per + k],
                recv_sem=recv_sems.at[w * per + k], device_id=to, device_id_type=MESH)

        local, started = [], []
        for w in range(n):
            cp = pltpu.make_async_copy(ins[w], outs[w].at[block(x, y, c)], local_sems.at[w])
            cp.start()
            local.append(cp)
            started.append(copy(w, 0, ins[w], block(x, y, c), sibling))
            for j, (px, py) in enumerate(chips):
                started.append(copy(w, 1 + j, ins[w], block(x, y, c), (px, py, c)))
        for cp in started:
            cp.start()
        for j, (px, py) in enumerate(chips):
            for w in range(n):
                copy(w, 1 + j, ins[w], block(px, py, c), sibling).wait_recv()
                cp = copy(w, 4 + j, outs[w].at[block(px, py, c)], block(px, py, c), sibling)
                cp.start()
                started.append(cp)
        for w in range(n):
            copy(w, 0, ins[w], block(x, y, 1 - c), sibling).wait_recv()
            for j, (px, py) in enumerate(chips):
                copy(w, 4 + j, ins[w], block(px, py, 1 - c), sibling).wait_recv()
        for cp in started:
            cp.wait_send()
        for cp in local:
            cp.wait()

    return _pcall(
        body, name="all_gather_weights", in_specs=[hbm] * n, out_specs=[hbm] * n,
        out_shape=[SDS((N_DEV,) + a.shape, a.dtype) for a in arrays],
        scratch_shapes=[pltpu.SemaphoreType.DMA((n * per,)), pltpu.SemaphoreType.DMA((n * per,)),
                        pltpu.SemaphoreType.DMA((n,))],
    )(*arrays)


def _core_exchange(slabs):
    n = len(slabs)
    hbm = pl.BlockSpec(memory_space=pltpu.HBM)

    def body(*refs):
        ins, outs = refs[:n], refs[n:2 * n]
        send_sems, recv_sems = refs[2 * n:]
        x, y, c = _mesh_pos()
        copies = [pltpu.make_async_remote_copy(
            src_ref=ins[w].at[pl.ds(0, N_CHIPS), 1 - c], dst_ref=outs[w], send_sem=send_sems.at[w],
            recv_sem=recv_sems.at[w], device_id=(x, y, 1 - c), device_id_type=MESH) for w in range(n)]
        for cp in copies:
            cp.start()
        for cp in copies:
            cp.wait_recv()
        for cp in copies:
            cp.wait_send()

    return _pcall(
        body, name="grads_core_exchange", in_specs=[hbm] * n, out_specs=[hbm] * n,
        out_shape=[SDS((N_CHIPS,) + a.shape[2:], a.dtype) for a in slabs],
        scratch_shapes=[pltpu.SemaphoreType.DMA((n,)), pltpu.SemaphoreType.DMA((n,))],
    )(*slabs)


def _chip_exchange(partials):
    n = len(partials)
    per = N_CHIPS - 1
    hbm = pl.BlockSpec(memory_space=pltpu.HBM)

    def body(*refs):
        ins, outs = refs[:n], refs[n:2 * n]
        send_sems, recv_sems, local_sems = refs[2 * n:]
        x, y, c = _mesh_pos()
        mine = 2 * x + y
        local, sends, recvs = [], [], []
        for w in range(n):
            cp = pltpu.make_async_copy(ins[w].at[mine], outs[w].at[mine], local_sems.at[w])
            cp.start()
            local.append(cp)
            for j, (px, py) in enumerate(_other_chips(x, y)):
                theirs = 2 * px + py
                sems = dict(send_sem=send_sems.at[w * per + j], recv_sem=recv_sems.at[w * per + j],
                            device_id=(px, py, c), device_id_type=MESH)
                cp = pltpu.make_async_remote_copy(src_ref=ins[w].at[theirs], dst_ref=outs[w].at[mine], **sems)
                cp.start()
                sends.append(cp)
                recvs.append(pltpu.make_async_remote_copy(src_ref=ins[w].at[theirs], dst_ref=outs[w].at[theirs], **sems))
        for cp in recvs:
            cp.wait_recv()
        for cp in sends:
            cp.wait_send()
        for cp in local:
            cp.wait()

    return _pcall(
        body, name="grads_chip_exchange", in_specs=[hbm] * n, out_specs=[hbm] * n,
        out_shape=[SDS(a.shape, a.dtype) for a in partials],
        scratch_shapes=[pltpu.SemaphoreType.DMA((n * per,)), pltpu.SemaphoreType.DMA((n * per,)),
                        pltpu.SemaphoreType.DMA((n,))],
    )(*partials)


def _core_sum(slabs, from_sibling, core, name):
    _, _, r, c = slabs.shape
    tr = min(r, 256)

    def body(core_ref, a_ref, b_ref, o_ref):
        o_ref[...] = (a_ref[0].astype(F32) + b_ref[...].astype(F32)).astype(BF16)

    return _pcall(
        body, name=name,
        grid_spec=pltpu.PrefetchScalarGridSpec(
            num_scalar_prefetch=1, grid=(N_CHIPS, r // tr),
            in_specs=[pl.BlockSpec((1, 1, tr, c), lambda g, i, core: (g, core[0], i, 0)),
                      pl.BlockSpec((1, tr, c), lambda g, i, core: (g, i, 0))],
            out_specs=pl.BlockSpec((1, tr, c), lambda g, i, core: (g, i, 0))),
        out_shape=SDS((N_CHIPS, r, c), BF16), compiler_params=_params(),
    )(core, slabs, from_sibling)


HBM_SPEC = pl.BlockSpec(memory_space=pltpu.HBM)
SEM_SPEC = pl.BlockSpec(memory_space=pltpu.SEMAPHORE)
DATAFLOW = pltpu.SideEffectType.DATAFLOW_SIDE_EFFECTING


def _push_start(arrays, scatter, name):
    n = len(arrays)
    per = N_DEV - 1

    def body(*refs):
        srcs, lands = refs[:n], refs[n:2 * n]
        send_sems, recv_sems, token = refs[2 * n], refs[2 * n + 1], refs[-1]
        pos = _mesh_pos()
        me = 4 * pos[0] + 2 * pos[1] + pos[2]
        for w in range(n):
            for k in range(1, N_DEV):
                peer, peer_idx = _peer(pos, k)
                pltpu.make_async_remote_copy(
                    src_ref=srcs[w].at[peer_idx] if scatter else srcs[w], dst_ref=lands[w].at[me],
                    send_sem=send_sems.at[w * per + k - 1], recv_sem=recv_sems.at[w * per + k - 1],
                    device_id=peer, device_id_type=MESH).start()
        token[...] = jnp.zeros_like(token)

    land_shapes = [a.shape if scatter else (N_DEV,) + a.shape for a in arrays]
    in_hbm = lambda a: pltpu.with_memory_space_constraint(a, pltpu.HBM)
    lands = [in_hbm(lax.empty(s, a.dtype)) for s, a in zip(land_shapes, arrays)]
    sems = pltpu.SemaphoreType.DMA((n * per,))
    res = _pcall(
        body, name=name,
        out_shape=(sems, sems, *[pltpu.HBM(a.shape, a.dtype) for a in arrays],
                   *[pltpu.HBM(s, a.dtype) for s, a in zip(land_shapes, arrays)], SDS((8, LANES), F32)),
        in_specs=[HBM_SPEC] * (2 * n),
        out_specs=(SEM_SPEC, SEM_SPEC, *[HBM_SPEC] * (2 * n), pl.BlockSpec(memory_space=pltpu.VMEM)),
        input_output_aliases={i: 2 + i for i in range(2 * n)},
        compiler_params=pltpu.CompilerParams(has_side_effects=DATAFLOW),
    )(*[in_hbm(a) for a in arrays], *lands)
    return res[0], res[1], list(res[2:2 + n]), list(res[2 + n:2 + 2 * n]), res[-1]


def _push_wait(send_sems, recv_sems, arrays, lands, after, scatter, name):
    n = len(arrays)
    per = N_DEV - 1

    def body(*refs):
        srcs, lands_ = refs[:n], refs[n:2 * n]
        send_sems_, recv_sems_ = refs[2 * n], refs[2 * n + 1]
        pos = _mesh_pos()
        for w in range(n):
            for k in range(1, N_DEV):
                peer, peer_idx = _peer(pos, k)
                cp = pltpu.make_async_remote_copy(
                    src_ref=srcs[w].at[peer_idx] if scatter else srcs[w], dst_ref=lands_[w].at[peer_idx],
                    send_sem=send_sems_.at[w * per + k - 1], recv_sem=recv_sems_.at[w * per + k - 1],
                    device_id=peer, device_id_type=MESH)
                cp.wait_send()
                cp.wait_recv()

    res = _pcall(
        body, name=name,
        out_shape=(*[pltpu.HBM(a.shape, a.dtype) for a in arrays], *[pltpu.HBM(l.shape, l.dtype) for l in lands]),
        in_specs=[HBM_SPEC] * (2 * n) + [SEM_SPEC, SEM_SPEC, pl.BlockSpec(memory_space=pl.ANY)],
        out_specs=[HBM_SPEC] * (2 * n), input_output_aliases={i: i for i in range(2 * n)},
        compiler_params=pltpu.CompilerParams(has_side_effects=DATAFLOW),
    )(*arrays, *lands, send_sems, recv_sems, after)
    return list(res[n:])


def _fill_own(land, own):
    me = 4 * lax.axis_index("x") + 2 * lax.axis_index("y") + lax.axis_index("c")
    return lax.dynamic_update_slice(land, own[None], (me,) + (0,) * own.ndim)


def _adam_math(w, g, m, v):
    m = ADAM_B1 * m + (1.0 - ADAM_B1) * g
    v = ADAM_B2 * v + (1.0 - ADAM_B2) * (g * g)
    m_hat = m / (1.0 - ADAM_B1 ** ADAM_STEP)
    v_hat = v / (1.0 - ADAM_B2 ** ADAM_STEP)
    delta = -ADAM_LR * (m_hat / (jnp.sqrt(v_hat) + ADAM_EPS) + ADAM_WD * w)
    return delta, m, v


def _adamw(recv, w, m, v, name):
    r, c = w.shape
    tr = min(r, 128)
    n_parts = recv.shape[0]

    def body(g_ref, w_ref, m_ref, v_ref, go_ref, d_ref, mo_ref, vo_ref):
        g = g_ref[0].astype(F32)
        for s in range(1, n_parts):
            g = g + g_ref[s].astype(F32)
        delta, mn, vn = _adam_math(w_ref[...], g, m_ref[...], v_ref[...])
        go_ref[...] = g
        d_ref[...] = delta
        mo_ref[...] = mn
        vo_ref[...] = vn

    blk = pl.BlockSpec((tr, c), lambda i: (i, 0))
    return _pcall(
        body, name=name, grid=(r // tr,),
        in_specs=[pl.BlockSpec((n_parts, tr, c), lambda i: (0, i, 0)), blk, blk, blk],
        out_specs=[blk] * 4, out_shape=[SDS((r, c), F32)] * 4, compiler_params=_params(),
    )(recv, w, m, v)


VEC_ROWS = 32


def _small_allreduce_adamw(vec, w, m, v):
    def body(vec_ref, w_ref, m_ref, v_ref, g_ref, d_ref, mo_ref, vo_ref, gath, send_sems, recv_sems):
        pos = _mesh_pos()
        me = 4 * pos[0] + 2 * pos[1] + pos[2]
        sends, recvs = [], []
        for k in range(1, N_DEV):
            peer, peer_idx = _peer(pos, k)
            cp = pltpu.make_async_remote_copy(src_ref=vec_ref, dst_ref=gath.at[me], send_sem=send_sems.at[k - 1],
                                              recv_sem=recv_sems.at[k - 1], device_id=peer, device_id_type=MESH)
            cp.start()
            sends.append(cp)
            recvs.append(pltpu.make_async_remote_copy(src_ref=vec_ref, dst_ref=gath.at[peer_idx],
                                                      send_sem=send_sems.at[k - 1], recv_sem=recv_sems.at[k - 1],
                                                      device_id=peer, device_id_type=MESH))
        gath[me] = vec_ref[...]
        for cp in recvs:
            cp.wait_recv()
        for cp in sends:
            cp.wait_send()
        tot = gath[0]
        for s in range(1, N_DEV):
            tot = tot + gath[s]
        rowi = lax.broadcasted_iota(jnp.int32, (8, LANES), 0)
        mine = jnp.sum(jnp.where(rowi == me, tot[16:24, :], 0.0), axis=0, keepdims=True)
        g = jnp.concatenate([tot[0:16, :], jnp.broadcast_to(mine, (8, LANES)), tot[24:32, :]], axis=0)
        delta, mn, vn = _adam_math(w_ref[...], g, m_ref[...], v_ref[...])
        g_ref[...] = g
        d_ref[...] = delta
        mo_ref[...] = mn
        vo_ref[...] = vn

    vm = pl.BlockSpec(memory_space=pltpu.VMEM)
    return _pcall(
        body, name="small_allreduce_adamw", in_specs=[vm] * 4, out_specs=[vm] * 4,
        out_shape=[SDS((VEC_ROWS, LANES), F32)] * 4,
        scratch_shapes=[pltpu.VMEM((N_DEV, VEC_ROWS, LANES), F32), pltpu.SemaphoreType.DMA((N_DEV - 1,)),
                        pltpu.SemaphoreType.DMA((N_DEV - 1,))],
        compiler_params=pltpu.CompilerParams(has_side_effects=True),
    )(vec, w, m, v)


def _cols_to_slabs(a):
    r, c8 = a.shape
    return a.reshape(r, N_DEV, c8 // N_DEV).transpose(1, 0, 2)


def _slabs_to_cols(a):
    n, r, c = a.shape
    return a.transpose(1, 0, 2).reshape(r, n * c)


def _rows8(vec):
    return vec.reshape(-1, LANES)


def _pad_rows(a, rows):
    return jnp.pad(a, ((0, rows - a.shape[0]), (0, LANES - a.shape[1])))


def kernel(x, p, fox_norm, fox_w_in, fox_b_f, fox_w_out, dil_norm, dil_w_in, dil_w_out, ple_w_up, ple_w_gate, final_norm, loss_target, m_fox_norm, m_fox_w_in, m_fox_b_f, m_fox_w_out, m_dil_norm, m_dil_w_in, m_dil_w_out, m_ple_w_up, m_ple_w_gate, m_final_norm, v_fox_norm, v_fox_w_in, v_fox_b_f, v_fox_w_out, v_dil_norm, v_dil_w_in, v_dil_w_out, v_ple_w_up, v_ple_w_gate, v_final_norm):
    t = x.shape[1]
    d = D_MODEL
    xs, tgt = x[0], loss_target[0]
    p0, p1 = p[0, 0], p[1, 0]
    fox_cols = fox_w_in.shape[2]
    ple_dim = ple_w_up.shape[1]

    later = [dil_w_in[0].astype(BF16), dil_w_out[0].astype(BF16), ple_w_up.reshape(-1, LANES).astype(BF16),
             ple_w_gate.reshape(-1, d).astype(BF16), dil_norm]
    push = _push_start(later, False, "gather_later_start")
    gw = _all_gather([fox_w_in[0].astype(BF16), fox_w_out[0].astype(BF16)])
    w_fox_in = _slabs_to_cols(gw[0])
    w_fox_main = w_fox_in[:, :4 * d]
    w_fox_f = jnp.pad(w_fox_in[:, 4 * d:], ((0, 0), (0, LANES - FOX_HEADS)))
    w_fox_out = gw[1].reshape(d, d)
    b_pad = jnp.pad(fox_b_f, ((0, 0), (0, LANES - FOX_HEADS)))

    n0, r0 = _rms_fwd(xs, fox_norm + push[4][0:1, 0:1], "rms_fox")
    proj0 = _mm([n0], w_fox_main, "nn", "fox_in_proj", tiles=IN_PROJ_TILES)
    projf = _mm([n0], w_fox_f, "nn", "fox_gate_proj", tiles=IN_PROJ_TILES, out_dtype=F32)
    c_all = _fox_gate_fwd(projf, b_pad)
    qaug_fwd = _fox_aug(c_all, c_all, 1.0, 0.0, 0, FOX_AUG, "fox_aug_q_fwd")
    kaug = _fox_aug(c_all, c_all, -1.0, 0.0, FOX_AUG, 0, "fox_aug_k")
    o0, g0, lse0 = _fox_fwd(proj0, qaug_fwd, kaug)
    h1 = _mm_residual(g0, w_fox_out, "nn", xs, "fox_out_proj")

    landed = _push_wait(push[0], push[1], push[2], push[3], h1, False, "gather_later_wait")
    gl = [_fill_own(zone, own) for zone, own in zip(landed, later)]
    w_dil_in = _slabs_to_cols(gl[0])
    w_dil_out = gl[1].reshape(d, d)
    w_up = gl[2].reshape(N_DEV, 2, ple_dim, LANES).transpose(1, 2, 0, 3).reshape(2, ple_dim, d)
    w_gate = gl[3].reshape(N_DEV, 2, d // N_DEV, d).transpose(1, 0, 2, 3).reshape(2, d, d)
    dil_norm_full = gl[4].reshape(1, d)
    h2, u0, a0 = _ple_fwd(h1, p0, w_up[0], w_gate[0], "ple0_fwd")

    n1, r1 = _rms_fwd(h2, dil_norm_full, "rms_dil")
    proj1 = _mm([n1], w_dil_in, "nn", "dil_in_proj", tiles=IN_PROJ_TILES)
    n_heads = len(DIL_PATTERN) * DIL_HEADS
    slopes = 2.0 ** (-ALIBI_MAX_EXP * jnp.arange(1, n_heads + 1, dtype=F32) / n_heads)
    dil_o, dil_lse, dil_slopes = [], [], []
    for grp, (_, dil) in enumerate(DIL_PATTERN):
        sl = (slopes[grp * DIL_HEADS:(grp + 1) * DIL_HEADS] * dil).reshape(DIL_HEADS, 1, 1)
        og, lg = _dil_fwd(proj1, sl, grp, dil, f"dil_attn_fwd_{grp}")
        dil_o.append(og)
        dil_lse.append(lg)
        dil_slopes.append(sl)
    z1_col0 = 9 * d
    o1, g1, lse1 = _dil_mix(dil_o, dil_lse, proj1, z1_col0)
    h3 = _mm_residual(g1, w_dil_out, "nn", h2, "dil_out_proj")
    h4, u1, a1 = _ple_fwd(h3, p1, w_up[1], w_gate[1], "ple1_fwd")

    dh4, d_final_norm, loss_part = _final_bwd(h4, final_norm.reshape(1, d), tgt)

    du1, da1 = _ple_bwd_elem(dh4, u1, a1, "ple1_bwd_elem")
    dw_up1 = _dw(p1, du1, "ple1_dw_up")
    dw_gate1 = _dw(h3, da1, "ple1_dw_gate")
    dh3 = _mm_residual(da1, w_gate[1], "nt", dh4, "ple1_dh")

    dw_dil_out = _dw(g1, dh3, "dil_dw_out")
    do1, dz1, delta1 = _mm_gate_bwd(dh3, w_dil_out, proj1, z1_col0, o1, DIL_HEADS, "dil_dgate")
    n_grp = len(DIL_PATTERN)
    dqkv = [_dil_bwd(proj1, do1, lse1, delta1, dil_slopes[grp], grp, dil, f"dil_attn_bwd_{grp}")
            for grp, (_, dil) in enumerate(DIL_PATTERN)]
    dw_cols = [_dw(n1, dqkv[grp], f"dil_dw_in_{kind}{grp}", sub=kind) for kind in range(3) for grp in range(n_grp)]
    dw_dil_in = jnp.concatenate(dw_cols + [_dw(n1, dz1, "dil_dw_in_z")], axis=1)
    dil_slabs = [_cols_to_slabs(dw_dil_in), dw_dil_out.reshape(N_DEV, d // N_DEV, d)]
    dil_push = _push_start(dil_slabs, True, "scatter_dil_start")
    group_major = lambda kb: jnp.where(kb < 3 * n_grp, (kb % 3) * n_grp + kb // 3, kb)
    dh2, d_dil_norm = _mm_in_bwd(dqkv + [dz1], w_dil_in, h2, dil_norm_full + dil_push[4][0:1, 0:1], r1, dh3, "dil_dx",
                                 w_kmap=group_major)

    du0, da0 = _ple_bwd_elem(dh2, u0, a0, "ple0_bwd_elem")
    dw_up0 = _dw(p0, du0, "ple0_dw_up")
    dw_gate0 = _dw(h1, da0, "ple0_dw_gate")
    dh1 = _mm_residual(da0, w_gate[0], "nt", dh2, "ple0_dh")

    dw_fox_out = _dw(g0, dh1, "fox_dw_out")
    do0, dz0, delta0 = _mm_gate_bwd(dh1, w_fox_out, proj0, 3 * d, o0, FOX_HEADS, "fox_dgate")
    head_cols = lambda a: jnp.pad(a, ((0, 0), (0, LANES - FOX_HEADS)))
    lse_cols = head_cols(lse0.reshape(FOX_HEADS, t).T)
    delta_cols = head_cols(delta0.reshape(FOX_HEADS, t).T)
    qaug_bwd = _fox_aug(c_all, lse_cols, 1.0, -1.0, 0, FOX_AUG, "fox_aug_q_bwd")
    doaug = _fox_aug(delta_cols, delta_cols, -1.0, 0.0, 0, None, "fox_aug_do")
    dq0, dk0, dv0, dck_wide, dcq = _fox_bwd(proj0, do0, qaug_bwd, kaug, doaug)
    dc_query, dc_key = _fox_unpack_dc(dck_wide, dcq)
    df, d_b_f = _fox_gate_bwd(projf, b_pad, head_cols(dc_query), head_cols(dc_key))
    dproj0 = [dq0, dk0, dv0, dz0]
    dw_fox_parts = [_dw(n0, dpart, f"fox_dw_in_{s}") for s, dpart in enumerate(dproj0)]
    dw_fox_f = _dw(n0, df, "fox_dw_gate")
    dn0_f = _mm([df], w_fox_f, "nt", "fox_dx_gate", out_dtype=F32)
    grad_x, d_fox_norm = _mm_in_bwd(dproj0, w_fox_main, xs, fox_norm, r0, dh1, "fox_dx", more=dn0_f)

    dw_fox_in = jnp.concatenate(dw_fox_parts + [dw_fox_f[:, :FOX_HEADS]], axis=1)
    slabs = [_cols_to_slabs(dw_fox_in), dw_fox_out.reshape(N_DEV, d // N_DEV, d),
             jnp.stack([dw_up0, dw_up1]).reshape(2, ple_dim, N_DEV, LANES).transpose(2, 0, 1, 3).reshape(N_DEV, -1, LANES),
             jnp.stack([dw_gate0, dw_gate1]).reshape(2, N_DEV, d // N_DEV, d).transpose(1, 0, 2, 3).reshape(N_DEV, -1, d)]
    names = ["fox_w_in", "fox_w_out", "ple_w_up", "ple_w_gate"]
    slabs = [s.reshape((N_CHIPS, 2) + s.shape[1:]) for s in slabs]
    from_sibling = _core_exchange(slabs)
    core = lax.axis_index("c").astype(jnp.int32).reshape(1)
    chip_sums = [_core_sum(s, f, core, "core_sum_" + nm) for s, f, nm in zip(slabs, from_sibling, names)]
    recv = _chip_exchange(chip_sums)
    dil_landed = _push_wait(dil_push[0], dil_push[1], dil_push[2], dil_push[3], grad_x, True, "scatter_dil_wait")
    me = 4 * lax.axis_index("x") + 2 * lax.axis_index("y") + lax.axis_index("c")
    recv = list(recv) + [_fill_own(zone, lax.dynamic_index_in_dim(own, me, 0, keepdims=False))
                         for zone, own in zip(dil_landed, dil_slabs)]
    names += ["dil_w_in", "dil_w_out"]
    big = [(fox_w_in, m_fox_w_in, v_fox_w_in), (fox_w_out, m_fox_w_out, v_fox_w_out),
           (ple_w_up, m_ple_w_up, v_ple_w_up), (ple_w_gate, m_ple_w_gate, v_ple_w_gate),
           (dil_w_in, m_dil_w_in, v_dil_w_in), (dil_w_out, m_dil_w_out, v_dil_w_out)]
    upd = {}
    for rv, (w, m, v), nm in zip(recv, big, names):
        shp2 = rv.shape[1:]
        res = _adamw(rv, w.reshape(shp2), m.reshape(shp2), v.reshape(shp2), "adamw_" + nm)
        upd[nm] = [a.reshape(w.shape) for a in res]

    loss_row = jnp.where(jnp.arange(LANES) == 0, loss_part, 0.0)
    vec = jnp.concatenate([_rows8(d_fox_norm), _rows8(d_final_norm), _rows8(d_dil_norm), d_b_f, loss_row,
                           jnp.zeros((VEC_ROWS - 26, LANES), F32)], axis=0)

    def small_pack(a_fox_norm, a_final_norm, a_dil_norm, a_b_f):
        return jnp.concatenate([_rows8(a_fox_norm), _rows8(a_final_norm), _pad_rows(a_dil_norm, 8),
                                _pad_rows(a_b_f, 8)], axis=0)

    sg, sd, sm, sv = _small_allreduce_adamw(
        vec, small_pack(fox_norm, final_norm, dil_norm, fox_b_f),
        small_pack(m_fox_norm, m_final_norm, m_dil_norm, m_fox_b_f),
        small_pack(v_fox_norm, v_final_norm, v_dil_norm, v_fox_b_f))

    def small_unpack(a):
        return {"fox_norm": a[0:8].reshape(1, d), "final_norm": a[8:16].reshape(d), "dil_norm": a[16:17],
                "fox_b_f": a[24:25, :FOX_HEADS]}

    loss = sg[25, 0]
    order = ["fox_norm", "fox_w_in", "fox_b_f", "fox_w_out", "dil_norm", "dil_w_in", "dil_w_out", "ple_w_up",
             "ple_w_gate", "final_norm"]
    out = [loss, grad_x[None]]
    for idx, small in enumerate((sg, sd, sm, sv)):
        sp = small_unpack(small)
        out += [sp[nm] if nm in sp else upd[nm][idx] for nm in order]
    return tuple(out)
```

```python
import functools

import jax
import jax.numpy as jnp
from jax import lax
from jax.experimental import pallas as pl
from jax.experimental.pallas import tpu as pltpu

F32 = jnp.float32
BF16 = jnp.bfloat16
SDS = jax.ShapeDtypeStruct

D_MODEL = 1024
N_DEV = 8
LANES = 128
FOX_HEADS = 16
FOX_HEAD_DIM = 64
FOX_PAIRS = FOX_HEADS // 2
DIL_HEADS = 8
DIL_BLOCK = 128
DIL_PATTERN = ((128, 1), (512, 4), (2048, 16))
ALIBI_MAX_EXP = 8.0
RMS_EPS = 1e-6
ADAM_LR, ADAM_B1, ADAM_B2, ADAM_EPS, ADAM_WD, ADAM_STEP = 0.001, 0.9, 0.999, 1e-08, 0.01, 10
VMEM_LIMIT = 48 * 1024 * 1024
NEG_INF = float("-inf")

NN = (((1,), (0,)), ((), ()))
NT = (((1,), (1,)), ((), ()))
TN = (((0,), (0,)), ((), ()))
MESH = pl.DeviceIdType.MESH


def _pcall(body, **kw):
    return pl.pallas_call(body, **kw)


def _params(**kw):
    return pltpu.CompilerParams(vmem_limit_bytes=VMEM_LIMIT, **kw)


def _dot(a, b, dims):
    return lax.dot_general(a, b, dims, preferred_element_type=F32)


def _sigmoid(x):
    return 1.0 / (1.0 + jnp.exp(-x))


def _mm(a_parts, b, mode, name, tiles=(512, 1024, 1024), extras=(), outs=None, epilogue=None, out_dtype=BF16,
        b_kmap=None, b_sub=None):
    na = len(a_parts)
    stack = [a.shape[0] if a.ndim == 3 else 1 for a in a_parts]
    first = [sum(stack[:s]) for s in range(na)]
    if mode == "tn":
        k_part, m = a_parts[0].shape
        n = b.shape[-1]
    else:
        m, k_part = a_parts[0].shape[-2:]
        n = b.shape[1] if mode == "nn" else b.shape[0]
    tm, tn, tk = min(tiles[0], m), min(tiles[1], n), min(tiles[2], k_part)
    kb = k_part // tk
    nk = sum(stack) * kb
    grid = (m // tm, n // tn, nk)
    b_kmap = b_kmap or (lambda k: k)

    in_specs = []
    for s in range(na):
        if mode == "tn":
            in_specs.append(pl.BlockSpec((tk, tm), lambda i, j, k: (k, i)))
            continue

        def rel(k, s=s):
            return jnp.clip(k - first[s] * kb, 0, stack[s] * kb - 1)

        if a_parts[s].ndim == 3:
            in_specs.append(pl.BlockSpec((None, tm, tk), lambda i, j, k, rel=rel: (rel(k) // kb, i, rel(k) % kb)))
        else:
            in_specs.append(pl.BlockSpec((tm, tk), lambda i, j, k, rel=rel: (i, rel(k))))
    if mode == "nt":
        in_specs.append(pl.BlockSpec((tn, tk), lambda i, j, k: (j, b_kmap(k))))
    elif b_sub is not None:
        in_specs.append(pl.BlockSpec((None, tk, tn), lambda i, j, k: (b_sub, k, j)))
    else:
        in_specs.append(pl.BlockSpec((tk, tn), lambda i, j, k: (b_kmap(k), j)))
    for _, blk, imap in extras:
        in_specs.append(pl.BlockSpec(blk, imap))
    if outs is None:
        outs = [(SDS((m, n), out_dtype), (tm, tn), lambda i, j, k: (i, j))]
    out_specs = [pl.BlockSpec(blk, imap) for _, blk, imap in outs]
    ne, no = len(extras), len(outs)
    dims = {"nn": NN, "nt": NT, "tn": TN}[mode]

    def finish(res, e_refs, o_refs, i):
        if epilogue is None:
            o_refs[0][...] = res.astype(o_refs[0].dtype)
        else:
            epilogue(res, e_refs, o_refs, i)

    def body(*refs):
        a_refs = refs[:na]
        b_ref = refs[na]
        e_refs = refs[na + 1:na + 1 + ne]
        o_refs = refs[na + 1 + ne:na + 1 + ne + no]
        i, k = pl.program_id(0), pl.program_id(2)
        if nk == 1:
            finish(_dot(a_refs[0][...].astype(BF16), b_ref[...].astype(BF16), dims), e_refs, o_refs, i)
            return
        acc = refs[-1]

        @pl.when(k == 0)
        def _():
            acc[...] = jnp.zeros_like(acc)

        def step(a_ref):
            acc[...] += _dot(a_ref[...].astype(BF16), b_ref[...].astype(BF16), dims)

        for s in range(na):
            if na == 1:
                step(a_refs[0])
            else:
                in_use = (k >= first[s] * kb) & (k < (first[s] + stack[s]) * kb)
                pl.when(in_use)(functools.partial(step, a_refs[s]))

        @pl.when(k == nk - 1)
        def _():
            finish(acc[...], e_refs, o_refs, i)

    res = _pcall(
        body, name=name, grid=grid, in_specs=in_specs, out_specs=out_specs,
        out_shape=[o[0] for o in outs], scratch_shapes=[] if nk == 1 else [pltpu.VMEM((tm, tn), F32)],
        compiler_params=_params(dimension_semantics=("arbitrary", "arbitrary", "arbitrary")),
    )(*a_parts, b, *[e[0] for e in extras])
    return res[0] if len(res) == 1 else res


IN_PROJ_TILES = (1024, 1024, 1024)
DW_TILES = (1024, 1024, 1024)


def _dw(x, dy, name, sub=None):
    return _mm([x], dy, "tn", name, tiles=DW_TILES, b_sub=sub)


def _add_extra_epilogue(acc, e_refs, o_refs, i):
    o_refs[0][...] = acc + e_refs[0][...]


def _mm_residual(a, b, mode, res, name):
    m = a.shape[0]
    n = b.shape[1] if mode == "nn" else b.shape[0]
    tm, tn = 512, 1024
    return _mm([a], b, mode, name, tiles=(tm, tn, 1024),
               extras=[(res, (tm, tn), lambda i, j, k: (i, j))],
               outs=[(SDS((m, n), F32), (tm, tn), lambda i, j, k: (i, j))],
               epilogue=_add_extra_epilogue)


def _rms_fwd(h, g, name):
    t, d = h.shape
    tm = 512

    def body(h_ref, g_ref, n_ref, r_ref):
        x = h_ref[...]
        r = lax.rsqrt(jnp.mean(x * x, axis=-1, keepdims=True) + RMS_EPS)
        n_ref[...] = ((x * r) * g_ref[...]).astype(BF16)
        r_ref[...] = r

    return _pcall(
        body, name=name, grid=(t // tm,),
        in_specs=[pl.BlockSpec((tm, d), lambda i: (i, 0)), pl.BlockSpec((1, d), lambda i: (0, 0))],
        out_specs=[pl.BlockSpec((tm, d), lambda i: (i, 0)), pl.BlockSpec((tm, 1), lambda i: (i, 0))],
        out_shape=[SDS((t, d), BF16), SDS((t, 1), F32)],
        compiler_params=_params(),
    )(h, g)


def _rms_bwd_rows(dn, x, g, r):
    xhat = x * r
    dxhat = dn * g
    dx = r * (dxhat - xhat * jnp.mean(dxhat * xhat, axis=-1, keepdims=True))
    dg = jnp.sum(dn * xhat, axis=0, keepdims=True)
    return dx, dg


def _mm_in_bwd(d_parts, w, h, g, r, dres, name, more=None, w_kmap=None):
    t = h.shape[0]
    tm = 512
    tk = D_MODEL
    row = lambda i, j, k: (i, 0)
    extras = [(h, (tm, D_MODEL), row), (g, (1, D_MODEL), lambda i, j, k: (0, 0)), (r, (tm, 1), row),
              (dres, (tm, D_MODEL), row)]
    if more is not None:
        extras.append((more, (tm, D_MODEL), row))

    def epilogue(acc, e_refs, o_refs, i):
        dn = acc if more is None else acc + e_refs[4][...]
        dx, dg = _rms_bwd_rows(dn, e_refs[0][...], e_refs[1][...], e_refs[2][...])
        o_refs[0][...] = e_refs[3][...] + dx

        @pl.when(i == 0)
        def _():
            o_refs[1][...] = dg

        @pl.when(i > 0)
        def _():
            o_refs[1][...] += dg

    return _mm(d_parts, w, "nt", name, tiles=(tm, D_MODEL, tk), extras=extras,
               outs=[(SDS((t, D_MODEL), F32), (tm, D_MODEL), row),
                     (SDS((1, D_MODEL), F32), (1, D_MODEL), lambda i, j, k: (0, 0))],
               epilogue=epilogue, b_kmap=w_kmap)


def _final_bwd(h, g, tgt):
    t, d = h.shape
    tm = 256

    def body(h_ref, g_ref, t_ref, dh_ref, dg_ref, loss_ref):
        i = pl.program_id(0)
        x = h_ref[...]
        gg = g_ref[...]
        r = lax.rsqrt(jnp.mean(x * x, axis=-1, keepdims=True) + RMS_EPS)
        err = (x * r) * gg - t_ref[...]
        part = 0.5 * jnp.sum(jnp.mean(err * err, axis=-1, keepdims=True), axis=0, keepdims=True)
        dx, dg = _rms_bwd_rows(err * (1.0 / d), x, gg, r)
        dh_ref[...] = dx

        @pl.when(i == 0)
        def _():
            dg_ref[...] = dg
            loss_ref[...] = jnp.broadcast_to(part, loss_ref.shape)

        @pl.when(i > 0)
        def _():
            dg_ref[...] += dg
            loss_ref[...] += jnp.broadcast_to(part, loss_ref.shape)

    return _pcall(
        body, name="final_norm_loss", grid=(t // tm,),
        in_specs=[pl.BlockSpec((tm, d), lambda i: (i, 0)), pl.BlockSpec((1, d), lambda i: (0, 0)),
                  pl.BlockSpec((tm, d), lambda i: (i, 0))],
        out_specs=[pl.BlockSpec((tm, d), lambda i: (i, 0)), pl.BlockSpec((1, d), lambda i: (0, 0)),
                   pl.BlockSpec((1, LANES), lambda i: (0, 0))],
        out_shape=[SDS((t, d), F32), SDS((1, d), F32), SDS((1, LANES), F32)],
        compiler_params=_params(),
    )(h, g, tgt)


GATE_ROWS = 256


def _split3(x):
    hi = x.astype(BF16)
    r1 = x - hi.astype(F32)
    mid = r1.astype(BF16)
    lo = (r1 - mid.astype(F32)).astype(BF16)
    return hi, mid, lo


def _tri_sum(x, upper):
    rows = x.shape[0]
    ri = lax.broadcasted_iota(jnp.int32, (rows, rows), 0)
    ci = lax.broadcasted_iota(jnp.int32, (rows, rows), 1)
    tri = jnp.where((ri <= ci) if upper else (ri >= ci), 1.0, 0.0).astype(BF16)
    hi, mid, lo = _split3(x)
    return _dot(tri, hi, NN) + _dot(tri, mid, NN) + _dot(tri, lo, NN)


def _log_sigmoid(x):
    return jnp.minimum(x, 0.0) - jnp.log1p(jnp.exp(-jnp.abs(x)))


def _fox_gate_fwd(projf, bpad):
    t = projf.shape[0]
    tb = GATE_ROWS

    def body(x_ref, b_ref, c_ref, carry):
        i = pl.program_id(0)

        @pl.when(i == 0)
        def _():
            carry[...] = jnp.zeros_like(carry)

        c_ref[...] = _tri_sum(_log_sigmoid(x_ref[...] + b_ref[...]), upper=False) + carry[...]
        carry[...] = c_ref[pl.ds(tb - 1, 1), :]

    return _pcall(
        body, name="fox_gate_fwd", grid=(t // tb,),
        in_specs=[pl.BlockSpec((tb, LANES), lambda i: (i, 0)), pl.BlockSpec((1, LANES), lambda i: (0, 0))],
        out_specs=pl.BlockSpec((tb, LANES), lambda i: (i, 0)),
        out_shape=SDS((t, LANES), F32), scratch_shapes=[pltpu.VMEM((1, LANES), F32)],
        compiler_params=_params(),
    )(projf, bpad)


def _fox_gate_bwd(projf, bpad, dc_query, dc_key_wide):
    t = projf.shape[0]
    tb = GATE_ROWS
    nb = t // tb

    def body(x_ref, b_ref, dcq_ref, dck_ref, df_ref, db_ref, carry, buf):
        i = pl.program_id(0)

        @pl.when(i == 0)
        def _():
            carry[...] = jnp.zeros_like(carry)

        src = lax.broadcasted_iota(jnp.int32, (D_MODEL, LANES), 0)
        head = lax.broadcasted_iota(jnp.int32, (D_MODEL, LANES), 1)
        pick = jnp.where((head < FOX_HEADS) & (src == (head // 2) * LANES + (1 - head % 2) * FOX_HEAD_DIM), 1.0, 0.0)
        pick = pick.astype(BF16)
        dc_key = sum(_dot(piece, pick, NN) for piece in _split3(dck_ref[...]))
        buf[...] = _tri_sum(dcq_ref[...] - dc_key, upper=True) + carry[...]
        carry[...] = buf[pl.ds(0, 1), :]
        df = buf[...] * _sigmoid(-(x_ref[...] + b_ref[...]))
        df_ref[...] = df.astype(BF16)
        part = jnp.sum(df, axis=0, keepdims=True)

        @pl.when(i == 0)
        def _():
            db_ref[...] = part

        @pl.when(i > 0)
        def _():
            db_ref[...] += part

    rev = lambda i: (nb - 1 - i, 0)
    return _pcall(
        body, name="fox_gate_bwd", grid=(nb,),
        in_specs=[pl.BlockSpec((tb, LANES), rev), pl.BlockSpec((1, LANES), lambda i: (0, 0)),
                  pl.BlockSpec((tb, LANES), rev), pl.BlockSpec((tb, D_MODEL), rev)],
        out_specs=[pl.BlockSpec((tb, LANES), rev), pl.BlockSpec((1, LANES), lambda i: (0, 0))],
        out_shape=[SDS((t, LANES), BF16), SDS((1, LANES), F32)],
        scratch_shapes=[pltpu.VMEM((1, LANES), F32), pltpu.VMEM((tb, LANES), F32)],
        compiler_params=_params(),
    )(projf, bpad, dc_query, dc_key_wide)


FOX_TQ = 1024
FOX_TQ_FWD = 1024
FOX_SCALE = FOX_HEAD_DIM ** -0.5


def _low_lanes(shape):
    return lax.broadcasted_iota(jnp.int32, shape, len(shape) - 1) < FOX_HEAD_DIM


FOX_AUG = 3
FOX_CHAIN = 256
FOX_SUM_ROWS = 8


def _top_rows(shape):
    return lax.broadcasted_iota(jnp.int32, shape, 0) < FOX_HEAD_DIM


def _fox_aug(a, b, sign_a, sign_b, piece_entry, ones_entry, name):
    t = a.shape[0]
    tb = 512

    def body(a_ref, b_ref, o_ref):
        x = sign_a * a_ref[...]
        if sign_b != 0.0:
            x = x + sign_b * b_ref[...]
        head = lax.broadcasted_iota(jnp.int32, (LANES, D_MODEL), 0)
        col = lax.broadcasted_iota(jnp.int32, (LANES, D_MODEL), 1)
        base = (head // 2) * LANES + (1 - head % 2) * FOX_HEAD_DIM + piece_entry
        acc = jnp.zeros((tb, D_MODEL), F32)
        for e, piece in enumerate(_split3(x)):
            place = jnp.where((head < FOX_HEADS) & (col == base + e), 1.0, 0.0).astype(BF16)
            acc = acc + _dot(piece, place, NN)
        if ones_entry is not None:
            ent = lax.broadcasted_iota(jnp.int32, (1, D_MODEL), 1) % FOX_HEAD_DIM
            acc = acc + jnp.where((ent >= ones_entry) & (ent < ones_entry + FOX_AUG), 1.0, 0.0)
        o_ref[...] = acc.astype(BF16)

    blk = pl.BlockSpec((tb, LANES), lambda i: (i, 0))
    return _pcall(
        body, name=name, grid=(t // tb,), in_specs=[blk, blk],
        out_specs=pl.BlockSpec((tb, D_MODEL), lambda i: (i, 0)), out_shape=SDS((t, D_MODEL), BF16),
        compiler_params=_params(),
    )(a, b)


def _causal_steps(nq, key_major):
    if key_major:
        pairs = [(i, j) for j in range(nq) for i in range(j, nq)]
    else:
        pairs = [(i, j) for i in range(nq) for j in range(i + 1)]
    return (jnp.asarray([p[0] for p in pairs], jnp.int32), jnp.asarray([p[1] for p in pairs], jnp.int32))


def _pair_operand(low, own, other, hh):
    return jnp.where(low, own, other) if hh == 0 else jnp.where(low, other, own)


def _fox_fwd(proj, qaug, kaug):
    t = proj.shape[0]
    tq = tk = min(FOX_TQ_FWD, t)
    nq = t // tq
    cb = D_MODEL // LANES
    half = min(FOX_CHAIN, tq)

    i_tab, j_tab = _causal_steps(nq, key_major=False)

    def body(i_ref, j_ref, q_ref, k_ref, v_ref, z_ref, qa_ref, ka_ref, o_ref, g_ref, lse_ref, m_s, l_s, acc_s):
        step = pl.program_id(1)
        i, j = i_ref[step], j_ref[step]

        @pl.when(j == 0)
        def _():
            m_s[...] = jnp.full_like(m_s, NEG_INF)
            l_s[...] = jnp.zeros_like(l_s)
            acc_s[...] = jnp.zeros_like(acc_s)

        low = _low_lanes((tq, LANES))
        top = _top_rows((LANES, tq))

        def update(masked):
            qs = q_ref[...] * FOX_SCALE
            qa, k, ka, v = qa_ref[...], k_ref[...], ka_ref[...], v_ref[...]
            if masked:
                causal = (lax.broadcasted_iota(jnp.int32, (tk, tq), 0) <= lax.broadcasted_iota(jnp.int32, (tk, tq), 1))
            one = jnp.ones_like(v)
            chains = [(hh, slice(c * half, (c + 1) * half)) for hh in range(2) for c in range(tq // half)]
            qh = [_pair_operand(low, qs, qa, hh) for hh in range(2)]
            kh = [_pair_operand(low, k, ka, hh) for hh in range(2)]
            vh = [_pair_operand(low, v, one, hh) for hh in range(2)]
            keys = lambda cols: slice(0, cols.stop) if masked else slice(None)
            scores = [_dot(kh[hh][keys(cols), :], qh[hh][cols, :], NT) for hh, cols in chains]
            for (hh, cols), s in zip(chains, scores):
                if masked:
                    s = jnp.where(causal[keys(cols), cols], s, NEG_INF)
                m_prev = m_s[hh, :, cols]
                m_new = jnp.maximum(m_prev, jnp.max(s, axis=0, keepdims=True))
                alpha = jnp.exp(m_prev - m_new)
                pv = _dot(vh[hh][keys(cols), :], jnp.exp(s - m_new).astype(BF16), TN)
                sums = pv[FOX_HEAD_DIM:FOX_HEAD_DIM + FOX_SUM_ROWS, :] if hh == 0 else pv[0:FOX_SUM_ROWS, :]
                l_s[hh, :, cols] = alpha * l_s[hh, :, cols] + sums
                m_s[hh, :, cols] = m_new
                own = top[:, cols] if hh == 0 else jnp.logical_not(top[:, cols])
                acc_s[:, cols] = jnp.where(own, acc_s[:, cols] * alpha + pv, acc_s[:, cols])

        pl.when(j < i)(functools.partial(update, False))
        pl.when(j == i)(functools.partial(update, True))

        @pl.when(j == i)
        def _():
            o = (acc_s[...] / jnp.where(top, l_s[0, 0:1, :], l_s[1, 0:1, :])).T
            z = z_ref[...].astype(F32)
            o_ref[...] = o.astype(BF16)
            g_ref[...] = (o * (z * _sigmoid(z))).astype(BF16)
            for hh in range(2):
                lse_ref[hh] = m_s[hh] + jnp.log(l_s[hh, 0:1, :])

    qblk = lambda col: pl.BlockSpec((tq, LANES), lambda h, s, it, jt: (it[s], col + h))
    kblk = lambda col: pl.BlockSpec((tk, LANES), lambda h, s, it, jt: (jt[s], col + h))
    return _pcall(
        body, name="fox_attn_fwd",
        grid_spec=pltpu.PrefetchScalarGridSpec(
            num_scalar_prefetch=2, grid=(FOX_PAIRS, i_tab.shape[0]),
            in_specs=[qblk(0), kblk(cb), kblk(2 * cb), qblk(3 * cb), qblk(0), kblk(0)],
            out_specs=[qblk(0), qblk(0), pl.BlockSpec((2, 1, tq), lambda h, s, it, jt: (h, 0, it[s]))],
            scratch_shapes=[pltpu.VMEM((2, 1, tq), F32), pltpu.VMEM((2, FOX_SUM_ROWS, tq), F32),
                            pltpu.VMEM((LANES, tq), F32)]),
        out_shape=[SDS((t, D_MODEL), BF16), SDS((t, D_MODEL), BF16), SDS((FOX_HEADS, 1, t), F32)],
        compiler_params=_params(dimension_semantics=("arbitrary", "arbitrary")),
    )(i_tab, j_tab, proj, proj, proj, proj, qaug, kaug)


def _fox_bwd(proj, do, qaug, kaug, doaug):
    t = proj.shape[0]
    tq = tk = min(FOX_TQ, t)
    nq = t // tq
    cb = D_MODEL // LANES
    half = min(FOX_CHAIN, tq)

    i_tab, j_tab = _causal_steps(nq, key_major=True)

    def body(i_ref, j_ref, q_ref, k_ref, v_ref, do_ref, qa_ref, ka_ref, da_ref,
             dq_ref, dk_ref, dv_ref, dck_ref, dcq_ref, dq_acc, dcq_acc, dk_acc, dks_acc, dv_acc):
        step = pl.program_id(1)
        i, j = i_ref[step], j_ref[step]
        low = _low_lanes((tq, LANES))
        top = _top_rows((LANES, tq))

        @pl.when(i == j)
        def _():
            dk_acc[...] = jnp.zeros_like(dk_acc)
            dks_acc[...] = jnp.zeros_like(dks_acc)
            dv_acc[...] = jnp.zeros_like(dv_acc)

        def update(masked):
            qs = q_ref[...] * FOX_SCALE
            k, v, dout = k_ref[...], v_ref[...], do_ref[...]
            qa, ka, da = qa_ref[...], ka_ref[...], da_ref[...]
            lane = lax.broadcasted_iota(jnp.int32, (tk, LANES), 1)
            vone = jnp.where((lane & (FOX_HEAD_DIM - 1)) < FOX_AUG, 1.0, 0.0).astype(v.dtype)
            one = jnp.ones_like(k)
            if masked:
                causal = (lax.broadcasted_iota(jnp.int32, (tk, tq), 0) <= lax.broadcasted_iota(jnp.int32, (tk, tq), 1))
            parts = []
            kh = [_pair_operand(low, k, ka, hh) for hh in range(2)]
            qh = [_pair_operand(low, qs, qa, hh) for hh in range(2)]
            vh = [_pair_operand(low, v, vone, hh) for hh in range(2)]
            doh = [_pair_operand(low, dout, da, hh) for hh in range(2)]
            q1 = [_pair_operand(low, qs, one, hh) for hh in range(2)]
            k1 = [_pair_operand(low, k, one, hh) for hh in range(2)]

            def tile(hh, keys, cols, s, dp):
                if masked:
                    s = jnp.where(causal[keys, cols], s, NEG_INF)
                p = jnp.exp(s)
                pb, dsb = p.astype(BF16), (p * dp).astype(BF16)
                return (_dot(pb, dout[cols, :], NN), _dot(dsb, q1[hh][cols, :], NN), _dot(k1[hh][keys, :], dsb, TN))

            if not masked:
                scores = [_dot(kh[hh], qh[hh], NT) for hh in range(2)]
                dps = [_dot(vh[hh], doh[hh], NT) for hh in range(2)]
                everything = slice(None)
                parts = [tile(hh, everything, everything, scores[hh], dps[hh]) for hh in range(2)]
            else:
                for hh in range(2):
                    dv_h, dk_h, dq_h = jnp.zeros((tk, LANES), F32), jnp.zeros((tk, LANES), F32), []
                    for c in range(tq // half):
                        cols, keys = slice(c * half, (c + 1) * half), slice(0, (c + 1) * half)
                        dv_c, dk_c, dq_c = tile(hh, keys, cols, _dot(kh[hh][keys, :], qh[hh][cols, :], NT),
                                                _dot(vh[hh][keys, :], doh[hh][cols, :], NT))
                        below = ((0, tk - keys.stop), (0, 0))
                        dv_h, dk_h = dv_h + jnp.pad(dv_c, below), dk_h + jnp.pad(dk_c, below)
                        dq_h.append(dq_c)
                    parts.append((dv_h, dk_h, jnp.concatenate(dq_h, axis=1)))
            dv_acc[...] += jnp.where(low, parts[0][0], parts[1][0])
            dk_acc[...] += jnp.where(low, parts[0][1], parts[1][1])
            dks_acc[...] += jnp.where(low, parts[1][1], parts[0][1])
            dq_t = jnp.where(top, parts[0][2], parts[1][2]) * FOX_SCALE
            sum_a = parts[0][2][FOX_HEAD_DIM:FOX_HEAD_DIM + FOX_SUM_ROWS, :]
            sum_b = parts[1][2][0:FOX_SUM_ROWS, :]

            @pl.when(j == 0)
            def _():
                dq_acc[i] = dq_t
                dcq_acc[0, i] = sum_a
                dcq_acc[1, i] = sum_b

            @pl.when(j > 0)
            def _():
                dq_acc[i] += dq_t
                dcq_acc[0, i] += sum_a
                dcq_acc[1, i] += sum_b

        pl.when(i > j)(functools.partial(update, False))
        pl.when(i == j)(functools.partial(update, True))

        @pl.when(i == nq - 1)
        def _():
            dk_ref[...] = dk_acc[...].astype(BF16)
            dv_ref[...] = dv_acc[...].astype(BF16)
            dck_ref[...] = dks_acc[...]

        @pl.when((i == nq - 1) & (j == nq - 1))
        def _():
            for blk in range(nq):
                dq_ref[blk * tq:(blk + 1) * tq, :] = dq_acc[blk].T.astype(BF16)
            dcq_ref[...] = dcq_acc[...]

    qblk = lambda col: pl.BlockSpec((tq, LANES), lambda h, s, it, jt: (it[s], col + h))
    kblk = lambda col: pl.BlockSpec((tk, LANES), lambda h, s, it, jt: (jt[s], col + h))
    return _pcall(
        body, name="fox_attn_bwd",
        grid_spec=pltpu.PrefetchScalarGridSpec(
            num_scalar_prefetch=2, grid=(FOX_PAIRS, i_tab.shape[0]),
            in_specs=[qblk(0), kblk(cb), kblk(2 * cb), qblk(0), qblk(0), kblk(0), qblk(0)],
            out_specs=[pl.BlockSpec((t, LANES), lambda h, s, it, jt: (0, h)), kblk(0), kblk(0), kblk(0),
                       pl.BlockSpec((2, nq, FOX_SUM_ROWS, tq), lambda h, s, it, jt: (h, 0, 0, 0))],
            scratch_shapes=[pltpu.VMEM((nq, LANES, tq), F32), pltpu.VMEM((2, nq, FOX_SUM_ROWS, tq), F32),
                            pltpu.VMEM((tk, LANES), F32), pltpu.VMEM((tk, LANES), F32), pltpu.VMEM((tk, LANES), F32)]),
        out_shape=[SDS((t, D_MODEL), BF16), SDS((t, D_MODEL), BF16), SDS((t, D_MODEL), BF16),
                   SDS((t, D_MODEL), F32), SDS((FOX_HEADS, nq, FOX_SUM_ROWS, tq), F32)],
        compiler_params=_params(dimension_semantics=("arbitrary", "arbitrary")),
    )(i_tab, j_tab, proj, proj, proj, do, qaug, kaug, doaug)


def _mm_gate_bwd(dh, w_out, z_src, z_col0, o, heads, name):
    t = dh.shape[0]
    tm = 512
    row = lambda i, j, k: (i, 0)
    zcb = z_col0 // D_MODEL
    per_block = heads == D_MODEL // LANES

    def epilogue(acc, e_refs, o_refs, i):
        z = e_refs[0][...].astype(F32)
        ov = e_refs[1][...].astype(F32)
        sg = _sigmoid(z)
        dout = acc * (z * sg)
        o_refs[0][...] = dout.astype(BF16)
        o_refs[1][...] = (acc * ov * (sg * (1.0 + z * (1.0 - sg)))).astype(BF16)
        prod = dout * ov
        lane = lax.broadcasted_iota(jnp.int32, (tm, LANES), 1)
        cols = jnp.zeros((tm, LANES), F32)
        for cbk in range(D_MODEL // LANES):
            seg = prod[:, cbk * LANES:(cbk + 1) * LANES]
            tot = jnp.sum(seg, axis=-1, keepdims=True)
            if per_block:
                o_refs[2][cbk] = tot
            else:
                lo = jnp.sum(jnp.where(_low_lanes(seg.shape), seg, 0.0), axis=-1, keepdims=True)
                cols = jnp.where(lane == 2 * cbk, lo, jnp.where(lane == 2 * cbk + 1, tot - lo, cols))
        if not per_block:
            o_refs[2][...] = cols

    delta_out = ((SDS((heads, t, 1), F32), (heads, tm, 1), lambda i, j, k: (0, i, 0)) if per_block
                 else (SDS((t, LANES), F32), (tm, LANES), row))
    return _mm([dh], w_out, "nt", name, tiles=(tm, D_MODEL, D_MODEL),
               extras=[(z_src, (tm, D_MODEL), lambda i, j, k: (i, zcb)), (o, (tm, D_MODEL), row)],
               outs=[(SDS((t, D_MODEL), BF16), (tm, D_MODEL), row), (SDS((t, D_MODEL), BF16), (tm, D_MODEL), row),
                     delta_out],
               epilogue=epilogue)


def _ple_fwd(h, pin, w_up, w_gate, name):
    t = h.shape[0]
    tm = 512
    pd = pin.shape[1]

    def body(h_ref, p_ref, wu_ref, wg_ref, hn_ref, u_ref, a_ref):
        h = h_ref[...]
        u = _dot(p_ref[...].astype(BF16), wu_ref[...], NN)
        a = _dot(h.astype(BF16), wg_ref[...], NN)
        hn_ref[...] = h + u * _sigmoid(a)
        u_ref[...] = u.astype(BF16)
        a_ref[...] = a.astype(BF16)

    rows = pl.BlockSpec((tm, D_MODEL), lambda i: (i, 0))
    return _pcall(
        body, name=name, grid=(t // tm,),
        in_specs=[rows, pl.BlockSpec((tm, pd), lambda i: (i, 0)),
                  pl.BlockSpec((pd, D_MODEL), lambda i: (0, 0)), pl.BlockSpec((D_MODEL, D_MODEL), lambda i: (0, 0))],
        out_specs=[rows, rows, rows],
        out_shape=[SDS((t, D_MODEL), F32), SDS((t, D_MODEL), BF16), SDS((t, D_MODEL), BF16)],
        compiler_params=_params(),
    )(h, pin, w_up, w_gate)


def _ple_bwd_elem(dh, u, a, name):
    t = dh.shape[0]
    tm = 512

    def body(dh_ref, u_ref, a_ref, du_ref, da_ref):
        g = dh_ref[...]
        s = _sigmoid(a_ref[...].astype(F32))
        du_ref[...] = (g * s).astype(BF16)
        da_ref[...] = (g * u_ref[...].astype(F32) * (s * (1.0 - s))).astype(BF16)

    blk = pl.BlockSpec((tm, D_MODEL), lambda i: (i, 0))
    return _pcall(
        body, name=name, grid=(t // tm,), in_specs=[blk, blk, blk], out_specs=[blk, blk],
        out_shape=[SDS((t, D_MODEL), BF16), SDS((t, D_MODEL), BF16)], compiler_params=_params(),
    )(dh, u, a)


DIL_SCALE = LANES ** -0.5


def _dil_masks():
    ii = lax.broadcasted_iota(jnp.int32, (DIL_BLOCK, DIL_BLOCK), 0)
    jj = lax.broadcasted_iota(jnp.int32, (DIL_BLOCK, DIL_BLOCK), 1)
    return ii, jj


DIL_UNITS = 16
BNT = (((2,), (2,)), ((0,), (0,)))
BNN = (((2,), (1,)), ((0,), (0,)))
BTN = (((1,), (1,)), ((0,), (0,)))


def _dil_units(dil):
    return [(b, r) for b in range(DIL_UNITS // dil) for r in range(dil)]


def _unit_rows(b, r, dil):
    return pl.ds(b * DIL_BLOCK * dil + r, DIL_BLOCK, stride=dil)


def _gather_units(cur, dil, shift=0, edge=None, lead=()):
    nbk = DIL_UNITS // dil
    parts = []
    for b, r in _dil_units(dil):
        bb = b + shift
        if 0 <= bb < nbk:
            parts.append(cur[lead + (_unit_rows(bb, r, dil), slice(None))])
        else:
            parts.append(edge[lead + (pl.ds(r, DIL_BLOCK, stride=dil), slice(None))])
    return jnp.stack(parts)


def _scatter_units(dst, val, dil):
    for u, (b, r) in enumerate(_dil_units(dil)):
        dst[_unit_rows(b, r, dil), :] = val[u]


def _dil_bias(slope, prev):
    ii, jj = _dil_masks()
    dist = (DIL_BLOCK + ii - jj) if prev else (ii - jj)
    return (slope * dist.astype(F32))[None], ((jj >= ii) if prev else (jj <= ii))[None]


def _dil_fwd(proj, slopes, grp, dil, name):
    t = proj.shape[0]
    rows = DIL_BLOCK * DIL_UNITS
    edge_rows = DIL_BLOCK * dil
    nbk = DIL_UNITS // dil
    nsb = t // rows
    qc, kc_, vc_ = grp * DIL_HEADS, 3 * DIL_HEADS + grp * DIL_HEADS, 6 * DIL_HEADS + grp * DIL_HEADS

    def body(q_ref, kp_ref, kc_ref, vp_ref, vc_ref, sl_ref, o_ref, lse_ref, qf, kpf, kcf, vpf, vcf, of, lf):
        m = pl.program_id(1)
        for src, dst in ((q_ref, qf), (kp_ref, kpf), (kc_ref, kcf), (vp_ref, vpf), (vc_ref, vcf)):
            dst[...] = src[...].astype(F32)
        slope = sl_ref[0]
        unit = lax.broadcasted_iota(jnp.int32, (DIL_UNITS, 1, 1), 0)
        has_prev = (unit >= dil) | (m > 0)
        q = _gather_units(qf, dil).astype(BF16)
        kc, vc = _gather_units(kcf, dil).astype(BF16), _gather_units(vcf, dil).astype(BF16)
        kp, vp = _gather_units(kcf, dil, -1, kpf).astype(BF16), _gather_units(vcf, dil, -1, vpf).astype(BF16)
        bias_p, ok_p = _dil_bias(slope, True)
        bias_c, ok_c = _dil_bias(slope, False)
        sp = jnp.where(ok_p & has_prev, _dot(q, kp, BNT) * DIL_SCALE - bias_p, NEG_INF)
        sc = jnp.where(ok_c, _dot(q, kc, BNT) * DIL_SCALE - bias_c, NEG_INF)
        mx = jnp.maximum(jnp.max(sp, axis=-1, keepdims=True), jnp.max(sc, axis=-1, keepdims=True))
        pp = jnp.exp(sp - mx)
        pc = jnp.exp(sc - mx)
        l = jnp.sum(pp, axis=-1, keepdims=True) + jnp.sum(pc, axis=-1, keepdims=True)
        o = (_dot(pp.astype(BF16), vp, BNN) + _dot(pc.astype(BF16), vc, BNN)) / l
        _scatter_units(of, o, dil)
        _scatter_units(lf, mx + jnp.log(l), dil)
        o_ref[...] = of[...].astype(BF16)
        lse_ref[0] = lf[...]

    cur = lambda col: pl.BlockSpec((rows, LANES), lambda h, m: (m, col + h))
    prev = lambda col: pl.BlockSpec((edge_rows, LANES), lambda h, m: (jnp.maximum(m * nbk - 1, 0), col + h))
    return _pcall(
        body, name=name, grid=(DIL_HEADS, nsb),
        in_specs=[cur(qc), prev(kc_), cur(kc_), prev(vc_), cur(vc_), pl.BlockSpec((1, 1, 1), lambda h, m: (h, 0, 0))],
        out_specs=[pl.BlockSpec((rows, LANES), lambda h, m: (m, h)), pl.BlockSpec((1, rows, 1), lambda h, m: (h, m, 0))],
        out_shape=[SDS((t, D_MODEL), BF16), SDS((DIL_HEADS, t, 1), F32)],
        scratch_shapes=[pltpu.VMEM((rows, LANES), F32), pltpu.VMEM((edge_rows, LANES), F32), pltpu.VMEM((rows, LANES), F32),
                        pltpu.VMEM((edge_rows, LANES), F32), pltpu.VMEM((rows, LANES), F32), pltpu.VMEM((rows, LANES), F32),
                        pltpu.VMEM((rows, 1), F32)],
        compiler_params=_params(),
    )(proj, proj, proj, proj, proj, slopes)


def _dil_mix(outs, lses, proj, z_col0):
    t = proj.shape[0]
    tm = 512
    zcb = z_col0 // LANES
    ng = len(outs)

    def body(*refs):
        o_refs, l_refs, z_ref = refs[:ng], refs[ng:2 * ng], refs[2 * ng]
        om_ref, g_ref, lse_ref = refs[2 * ng + 1:]
        ls = [r[0] for r in l_refs]
        mx = functools.reduce(jnp.maximum, ls)
        es = [jnp.exp(l - mx) for l in ls]
        tot = functools.reduce(jnp.add, es)
        o = functools.reduce(jnp.add, [(e / tot) * r[...].astype(F32) for e, r in zip(es, o_refs)])
        z = z_ref[...].astype(F32)
        om_ref[...] = o.astype(BF16)
        g_ref[...] = (o * (z * _sigmoid(z))).astype(BF16)
        lse_ref[0] = mx + jnp.log(tot)

    tile = pl.BlockSpec((tm, LANES), lambda i, h: (i, h))
    col = pl.BlockSpec((1, tm, 1), lambda i, h: (h, i, 0))
    return _pcall(
        body, name="dil_mix", grid=(t // tm, DIL_HEADS),
        in_specs=[tile] * ng + [col] * ng + [pl.BlockSpec((tm, LANES), lambda i, h: (i, zcb + h))],
        out_specs=[tile, tile, col],
        out_shape=[SDS((t, D_MODEL), BF16), SDS((t, D_MODEL), BF16), SDS((DIL_HEADS, t, 1), F32)],
        compiler_params=_params(),
    )(*outs, *lses, proj)


def _dil_bwd(proj, do, lse, delta, slopes, grp, dil, name):
    t = proj.shape[0]
    rows = DIL_BLOCK * DIL_UNITS
    edge_rows = DIL_BLOCK * dil
    nbk = DIL_UNITS // dil
    nsb = t // rows
    last_edge = t // edge_rows - 1
    qc, kc_, vc_ = grp * DIL_HEADS, 3 * DIL_HEADS + grp * DIL_HEADS, 6 * DIL_HEADS + grp * DIL_HEADS

    def body(q_ref, qn_ref, kp_ref, kc_ref, vp_ref, vc_ref, do_ref, don_ref, l_ref, ln_ref, d_ref, dn_ref, sl_ref,
             dqkv_ref, qf, qnf, kpf, kcf, vpf, vcf, dof, donf, dqf, dkf, dvf):
        m = pl.program_id(1)
        for src, dst in ((q_ref, qf), (qn_ref, qnf), (kp_ref, kpf), (kc_ref, kcf), (vp_ref, vpf), (vc_ref, vcf),
                         (do_ref, dof), (don_ref, donf)):
            dst[...] = src[...].astype(F32)
        slope = sl_ref[0]
        unit = lax.broadcasted_iota(jnp.int32, (DIL_UNITS, 1, 1), 0)
        has_prev = (unit >= dil) | (m > 0)
        has_next = (unit < DIL_UNITS - dil) | (m < nsb - 1)
        b16 = lambda x: x.astype(BF16)
        q, kc, vc, dout = (b16(_gather_units(x, dil)) for x in (qf, kcf, vcf, dof))
        kp, vp = b16(_gather_units(kcf, dil, -1, kpf)), b16(_gather_units(vcf, dil, -1, vpf))
        qn, don = b16(_gather_units(qf, dil, 1, qnf)), b16(_gather_units(dof, dil, 1, donf))
        lrow, drow = _gather_units(l_ref, dil, lead=(0,)), _gather_units(d_ref, dil, lead=(0,))
        lnrow = _gather_units(l_ref, dil, 1, ln_ref, lead=(0,))
        dnrow = _gather_units(d_ref, dil, 1, dn_ref, lead=(0,))
        bias_p, ok_p = _dil_bias(slope, True)
        bias_c, ok_c = _dil_bias(slope, False)
        sp = jnp.where(ok_p & has_prev, _dot(q, kp, BNT) * DIL_SCALE - bias_p, NEG_INF)
        sc = jnp.where(ok_c, _dot(q, kc, BNT) * DIL_SCALE - bias_c, NEG_INF)
        pp = jnp.exp(sp - lrow)
        pc = jnp.exp(sc - lrow)
        dsp = b16(pp * (_dot(dout, vp, BNT) - drow))
        dsc = b16(pc * (_dot(dout, vc, BNT) - drow))
        _scatter_units(dqf, (_dot(dsp, kp, BNN) + _dot(dsc, kc, BNN)) * DIL_SCALE, dil)
        sn = jnp.where(ok_p & has_next, _dot(qn, kc, BNT) * DIL_SCALE - bias_p, NEG_INF)
        pn = jnp.exp(sn - lnrow)
        dsn = b16(pn * (_dot(don, vc, BNT) - dnrow))
        _scatter_units(dkf, (_dot(dsc, q, BTN) + _dot(dsn, qn, BTN)) * DIL_SCALE, dil)
        _scatter_units(dvf, _dot(b16(pc), dout, BTN) + _dot(b16(pn), don, BTN), dil)
        for s, src in enumerate((dqf, dkf, dvf)):
            dqkv_ref[s] = src[...].astype(BF16)

    prev_i = lambda m: jnp.maximum(m * nbk - 1, 0)
    next_i = lambda m: jnp.minimum((m + 1) * nbk, last_edge)
    cur = lambda col: pl.BlockSpec((rows, LANES), lambda h, m: (m, col + h))
    edge = lambda col, f: pl.BlockSpec((edge_rows, LANES), lambda h, m: (f(m), col + h))
    colcur = pl.BlockSpec((1, rows, 1), lambda h, m: (h, m, 0))
    colnext = pl.BlockSpec((1, edge_rows, 1), lambda h, m: (h, next_i(m), 0))
    out_blk = pl.BlockSpec((3, rows, LANES), lambda h, m: (0, m, h))
    big, small = pltpu.VMEM((rows, LANES), F32), pltpu.VMEM((edge_rows, LANES), F32)
    return _pcall(
        body, name=name, grid=(DIL_HEADS, nsb),
        in_specs=[cur(qc), edge(qc, next_i), edge(kc_, prev_i), cur(kc_), edge(vc_, prev_i), cur(vc_),
                  cur(0), edge(0, next_i), colcur, colnext, colcur, colnext,
                  pl.BlockSpec((1, 1, 1), lambda h, m: (h, 0, 0))],
        out_specs=out_blk, out_shape=SDS((3, t, D_MODEL), BF16),
        scratch_shapes=[big, small, small, big, small, big, big, small, big, big, big],
        compiler_params=_params(),
    )(proj, proj, proj, proj, proj, proj, do, do, lse, lse, delta, delta, slopes)


def _mesh_pos():
    x, y, c = lax.axis_index("x"), lax.axis_index("y"), lax.axis_index("c")
    return x, y, c


def _peer(pos, k):
    x, y, c = pos
    px = 1 - x if k & 4 else x
    py = 1 - y if k & 2 else y
    pc = 1 - c if k & 1 else c
    return (px, py, pc), 4 * px + 2 * py + pc


N_CHIPS = 4
CHIP_FLIPS = ((1, 0), (0, 1), (1, 1))


def _other_chips(x, y):
    return [(1 - x if fx else x, 1 - y if fy else y) for fx, fy in CHIP_FLIPS]


def _all_gather(arrays):
    n = len(arrays)
    per = 2 * N_CHIPS - 1
    hbm = pl.BlockSpec(memory_space=pltpu.HBM)

    def body(*refs):
        ins, outs = refs[:n], refs[n:2 * n]
        send_sems, recv_sems, local_sems = refs[2 * n:]
        x, y, c = _mesh_pos()
        sibling = (x, y, 1 - c)
        chips = _other_chips(x, y)
        block = lambda px, py, pc: 4 * px + 2 * py + pc

        def copy(w, k, src, blk, to):
            return pltpu.make_async_remote_copy(
                src_ref=src, dst_ref=outs[w].at[blk], send_sem=send_sems.at[w * per + k],
                recv_sem=recv_sems.at[w * per + k], device_id=to, device_id_type=MESH)

        local, started = [], []
        for w in range(n):
            cp = pltpu.make_async_copy(ins[w], outs[w].at[block(x, y, c)], local_sems.at[w])
            cp.start()
            local.append(cp)
            started.append(copy(w, 0, ins[w], block(x, y, c), sibling))
            for j, (px, py) in enumerate(chips):
                started.append(copy(w, 1 + j, ins[w], block(x, y, c), (px, py, c)))
        for cp in started:
            cp.start()
        for j, (px, py) in enumerate(chips):
            for w in range(n):
                copy(w, 1 + j, ins[w], block(px, py, c), sibling).wait_recv()
                cp = copy(w, 4 + j, outs[w].at[block(px, py, c)], block(px, py, c), sibling)
                cp.start()
                started.append(cp)
        for w in range(n):
            copy(w, 0, ins[w], block(x, y, 1 - c), sibling).wait_recv()
            for j, (px, py) in enumerate(chips):
                copy(w, 4 + j, ins[w], block(px, py, 1 - c), sibling).wait_recv()
        for cp in started:
            cp.wait_send()
        for cp in local:
            cp.wait()

    return _pcall(
        body, name="all_gather_weights", in_specs=[hbm] * n, out_specs=[hbm] * n,
        out_shape=[SDS((N_DEV,) + a.shape, a.dtype) for a in arrays],
        scratch_shapes=[pltpu.SemaphoreType.DMA((n * per,)), pltpu.SemaphoreType.DMA((n * per,)),
                        pltpu.SemaphoreType.DMA((n,))],
    )(*arrays)


HBM_SPEC = pl.BlockSpec(memory_space=pltpu.HBM)
SEM_SPEC = pl.BlockSpec(memory_space=pltpu.SEMAPHORE)
DATAFLOW = pltpu.SideEffectType.DATAFLOW_SIDE_EFFECTING


def _push_start(arrays, scatter, name):
    n = len(arrays)
    per = N_DEV - 1

    def body(*refs):
        srcs, lands = refs[:n], refs[n:2 * n]
        send_sems, recv_sems, token = refs[2 * n], refs[2 * n + 1], refs[-1]
        pos = _mesh_pos()
        me = 4 * pos[0] + 2 * pos[1] + pos[2]
        for w in range(n):
            for k in range(1, N_DEV):
                peer, peer_idx = _peer(pos, k)
                pltpu.make_async_remote_copy(
                    src_ref=srcs[w].at[peer_idx] if scatter else srcs[w], dst_ref=lands[w].at[me],
                    send_sem=send_sems.at[w * per + k - 1], recv_sem=recv_sems.at[w * per + k - 1],
                    device_id=peer, device_id_type=MESH).start()
        token[...] = jnp.zeros_like(token)

    land_shapes = [a.shape if scatter else (N_DEV,) + a.shape for a in arrays]
    in_hbm = lambda a: pltpu.with_memory_space_constraint(a, pltpu.HBM)
    lands = [in_hbm(lax.empty(s, a.dtype)) for s, a in zip(land_shapes, arrays)]
    sems = pltpu.SemaphoreType.DMA((n * per,))
    res = _pcall(
        body, name=name,
        out_shape=(sems, sems, *[pltpu.HBM(a.shape, a.dtype) for a in arrays],
                   *[pltpu.HBM(s, a.dtype) for s, a in zip(land_shapes, arrays)], SDS((8, LANES), F32)),
        in_specs=[HBM_SPEC] * (2 * n),
        out_specs=(SEM_SPEC, SEM_SPEC, *[HBM_SPEC] * (2 * n), pl.BlockSpec(memory_space=pltpu.VMEM)),
        input_output_aliases={i: 2 + i for i in range(2 * n)},
        compiler_params=pltpu.CompilerParams(has_side_effects=DATAFLOW),
    )(*[in_hbm(a) for a in arrays], *lands)
    return res[0], res[1], list(res[2:2 + n]), list(res[2 + n:2 + 2 * n]), res[-1]


def _push_wait(send_sems, recv_sems, arrays, lands, after, scatter, name):
    n = len(arrays)
    per = N_DEV - 1

    def body(*refs):
        srcs, lands_ = refs[:n], refs[n:2 * n]
        send_sems_, recv_sems_ = refs[2 * n], refs[2 * n + 1]
        pos = _mesh_pos()
        for w in range(n):
            for k in range(1, N_DEV):
                peer, peer_idx = _peer(pos, k)
                cp = pltpu.make_async_remote_copy(
                    src_ref=srcs[w].at[peer_idx] if scatter else srcs[w], dst_ref=lands_[w].at[peer_idx],
                    send_sem=send_sems_.at[w * per + k - 1], recv_sem=recv_sems_.at[w * per + k - 1],
                    device_id=peer, device_id_type=MESH)
                cp.wait_send()
                cp.wait_recv()

    res = _pcall(
        body, name=name,
        out_shape=(*[pltpu.HBM(a.shape, a.dtype) for a in arrays], *[pltpu.HBM(l.shape, l.dtype) for l in lands]),
        in_specs=[HBM_SPEC] * (2 * n) + [SEM_SPEC, SEM_SPEC, pl.BlockSpec(memory_space=pl.ANY)],
        out_specs=[HBM_SPEC] * (2 * n), input_output_aliases={i: i for i in range(2 * n)},
        compiler_params=pltpu.CompilerParams(has_side_effects=DATAFLOW),
    )(*arrays, *lands, send_sems, recv_sems, after)
    return list(res[n:])


def _fill_own(land, own):
    me = 4 * lax.axis_index("x") + 2 * lax.axis_index("y") + lax.axis_index("c")
    return lax.dynamic_update_slice(land, own[None], (me,) + (0,) * own.ndim)


def _adam_math(w, g, m, v):
    m = ADAM_B1 * m + (1.0 - ADAM_B1) * g
    v = ADAM_B2 * v + (1.0 - ADAM_B2) * (g * g)
    m_hat = m / (1.0 - ADAM_B1 ** ADAM_STEP)
    v_hat = v / (1.0 - ADAM_B2 ** ADAM_STEP)
    delta = -ADAM_LR * (m_hat / (jnp.sqrt(v_hat) + ADAM_EPS) + ADAM_WD * w)
    return delta, m, v


def _adamw(recv, w, m, v, name):
    n_parts, r, c = recv.shape
    layers, rows_per_layer, _ = w.shape
    tr = min(rows_per_layer, 128)
    per_layer = rows_per_layer // tr

    def body(g_ref, w_ref, m_ref, v_ref, go_ref, d_ref, mo_ref, vo_ref):
        g = g_ref[0].astype(F32)
        for s in range(1, n_parts):
            g = g + g_ref[s].astype(F32)
        delta, mn, vn = _adam_math(w_ref[...], g, m_ref[...], v_ref[...])
        go_ref[...] = g
        d_ref[...] = delta
        mo_ref[...] = mn
        vo_ref[...] = vn

    blk = pl.BlockSpec((None, tr, c), lambda i: (i // per_layer, i % per_layer, 0))
    return _pcall(
        body, name=name, grid=(r // tr,),
        in_specs=[pl.BlockSpec((n_parts, tr, c), lambda i: (0, i, 0)), blk, blk, blk],
        out_specs=[blk] * 4, out_shape=[SDS(w.shape, F32)] * 4, compiler_params=_params(),
    )(recv, w, m, v)


VEC_ROWS = 32


def _small_allreduce_adamw(vec, w, m, v):
    def body(vec_ref, w_ref, m_ref, v_ref, g_ref, d_ref, mo_ref, vo_ref, gath, send_sems, recv_sems):
        pos = _mesh_pos()
        me = 4 * pos[0] + 2 * pos[1] + pos[2]
        sends, recvs = [], []
        for k in range(1, N_DEV):
            peer, peer_idx = _peer(pos, k)
            cp = pltpu.make_async_remote_copy(src_ref=vec_ref, dst_ref=gath.at[me], send_sem=send_sems.at[k - 1],
                                              recv_sem=recv_sems.at[k - 1], device_id=peer, device_id_type=MESH)
            cp.start()
            sends.append(cp)
            recvs.append(pltpu.make_async_remote_copy(src_ref=vec_ref, dst_ref=gath.at[peer_idx],
                                                      send_sem=send_sems.at[k - 1], recv_sem=recv_sems.at[k - 1],
                                                      device_id=peer, device_id_type=MESH))
        gath[me] = vec_ref[...]
        for cp in recvs:
            cp.wait_recv()
        for cp in sends:
            cp.wait_send()
        tot = gath[0]
        for s in range(1, N_DEV):
            tot = tot + gath[s]
        rowi = lax.broadcasted_iota(jnp.int32, (8, LANES), 0)
        mine = jnp.sum(jnp.where(rowi == me, tot[16:24, :], 0.0), axis=0, keepdims=True)
        g = jnp.concatenate([tot[0:16, :], jnp.broadcast_to(mine, (8, LANES)), tot[24:32, :]], axis=0)
        delta, mn, vn = _adam_math(w_ref[...], g, m_ref[...], v_ref[...])
        g_ref[...] = g
        d_ref[...] = delta
        mo_ref[...] = mn
        vo_ref[...] = vn

    vm = pl.BlockSpec(memory_space=pltpu.VMEM)
    return _pcall(
        body, name="small_allreduce_adamw", in_specs=[vm] * 4, out_specs=[vm] * 4,
        out_shape=[SDS((VEC_ROWS, LANES), F32)] * 4,
        scratch_shapes=[pltpu.VMEM((N_DEV, VEC_ROWS, LANES), F32), pltpu.SemaphoreType.DMA((N_DEV - 1,)),
                        pltpu.SemaphoreType.DMA((N_DEV - 1,))],
        compiler_params=pltpu.CompilerParams(has_side_effects=True),
    )(vec, w, m, v)


def _cols_to_slabs(a):
    r, c8 = a.shape
    return a.reshape(r, N_DEV, c8 // N_DEV).transpose(1, 0, 2)


def _slabs_to_cols(a):
    n, r, c = a.shape
    return a.transpose(1, 0, 2).reshape(r, n * c)


def _rows8(vec):
    return vec.reshape(-1, LANES)


def _pad_rows(a, rows):
    return jnp.pad(a, ((0, rows - a.shape[0]), (0, LANES - a.shape[1])))


def kernel(x, p, fox_norm, fox_w_in, fox_b_f, fox_w_out, dil_norm, dil_w_in, dil_w_out, ple_w_up, ple_w_gate, final_norm, loss_target, m_fox_norm, m_fox_w_in, m_fox_b_f, m_fox_w_out, m_dil_norm, m_dil_w_in, m_dil_w_out, m_ple_w_up, m_ple_w_gate, m_final_norm, v_fox_norm, v_fox_w_in, v_fox_b_f, v_fox_w_out, v_dil_norm, v_dil_w_in, v_dil_w_out, v_ple_w_up, v_ple_w_gate, v_final_norm):
    t = x.shape[1]
    d = D_MODEL
    xs, tgt = x[0], loss_target[0]
    p0, p1 = p[0, 0], p[1, 0]
    fox_cols = fox_w_in.shape[2]
    ple_dim = ple_w_up.shape[1]

    later = [dil_w_in[0].astype(BF16), dil_w_out[0].astype(BF16), ple_w_up.reshape(-1, LANES).astype(BF16),
             ple_w_gate.reshape(-1, d).astype(BF16), dil_norm]
    push = _push_start(later, False, "gather_later_start")
    gw = _all_gather([fox_w_in[0].astype(BF16), fox_w_out[0].astype(BF16)])
    w_fox_in = _slabs_to_cols(gw[0])
    w_fox_main = w_fox_in[:, :4 * d]
    w_fox_f = jnp.pad(w_fox_in[:, 4 * d:], ((0, 0), (0, LANES - FOX_HEADS)))
    w_fox_out = gw[1].reshape(d, d)
    b_pad = jnp.pad(fox_b_f, ((0, 0), (0, LANES - FOX_HEADS)))

    n0, r0 = _rms_fwd(xs, fox_norm + push[4][0:1, 0:1], "rms_fox")
    proj0 = _mm([n0], w_fox_main, "nn", "fox_in_proj", tiles=IN_PROJ_TILES)
    projf = _mm([n0], w_fox_f, "nn", "fox_gate_proj", tiles=IN_PROJ_TILES, out_dtype=F32)
    c_all = _fox_gate_fwd(projf, b_pad)
    qaug_fwd = _fox_aug(c_all, c_all, 1.0, 0.0, 0, FOX_AUG, "fox_aug_q_fwd")
    kaug = _fox_aug(c_all, c_all, -1.0, 0.0, FOX_AUG, 0, "fox_aug_k")
    o0, g0, lse0 = _fox_fwd(proj0, qaug_fwd, kaug)
    h1 = _mm_residual(g0, w_fox_out, "nn", xs, "fox_out_proj")

    landed = _push_wait(push[0], push[1], push[2], push[3], h1, False, "gather_later_wait")
    gl = [_fill_own(zone, own) for zone, own in zip(landed, later)]
    w_dil_in = _slabs_to_cols(gl[0])
    w_dil_out = gl[1].reshape(d, d)
    w_up = gl[2].reshape(N_DEV, 2, ple_dim, LANES).transpose(1, 2, 0, 3).reshape(2, ple_dim, d)
    w_gate = gl[3].reshape(N_DEV, 2, d // N_DEV, d).transpose(1, 0, 2, 3).reshape(2, d, d)
    dil_norm_full = gl[4].reshape(1, d)
    h2, u0, a0 = _ple_fwd(h1, p0, w_up[0], w_gate[0], "ple0_fwd")

    n1, r1 = _rms_fwd(h2, dil_norm_full, "rms_dil")
    proj1 = _mm([n1], w_dil_in, "nn", "dil_in_proj", tiles=IN_PROJ_TILES)
    n_heads = len(DIL_PATTERN) * DIL_HEADS
    slopes = 2.0 ** (-ALIBI_MAX_EXP * jnp.arange(1, n_heads + 1, dtype=F32) / n_heads)
    dil_o, dil_lse, dil_slopes = [], [], []
    for grp, (_, dil) in enumerate(DIL_PATTERN):
        sl = (slopes[grp * DIL_HEADS:(grp + 1) * DIL_HEADS] * dil).reshape(DIL_HEADS, 1, 1)
        og, lg = _dil_fwd(proj1, sl, grp, dil, f"dil_attn_fwd_{grp}")
        dil_o.append(og)
        dil_lse.append(lg)
        dil_slopes.append(sl)
    z1_col0 = 9 * d
    o1, g1, lse1 = _dil_mix(dil_o, dil_lse, proj1, z1_col0)
    h3 = _mm_residual(g1, w_dil_out, "nn", h2, "dil_out_proj")
    h4, u1, a1 = _ple_fwd(h3, p1, w_up[1], w_gate[1], "ple1_fwd")

    dh4, d_final_norm, loss_part = _final_bwd(h4, final_norm.reshape(1, d), tgt)

    du1, da1 = _ple_bwd_elem(dh4, u1, a1, "ple1_bwd_elem")
    dw_up1 = _dw(p1, du1, "ple1_dw_up")
    dw_gate1 = _dw(h3, da1, "ple1_dw_gate")
    dh3 = _mm_residual(da1, w_gate[1], "nt", dh4, "ple1_dh")

    dw_dil_out = _dw(g1, dh3, "dil_dw_out")
    do1, dz1, delta1 = _mm_gate_bwd(dh3, w_dil_out, proj1, z1_col0, o1, DIL_HEADS, "dil_dgate")
    n_grp = len(DIL_PATTERN)
    dqkv = [_dil_bwd(proj1, do1, lse1, delta1, dil_slopes[grp], grp, dil, f"dil_attn_bwd_{grp}")
            for grp, (_, dil) in enumerate(DIL_PATTERN)]
    dw_cols = [_dw(n1, dqkv[grp], f"dil_dw_in_{kind}{grp}", sub=kind) for kind in range(3) for grp in range(n_grp)]
    dw_dil_in = jnp.concatenate(dw_cols + [_dw(n1, dz1, "dil_dw_in_z")], axis=1)
    row_slabs = lambda a: a.reshape(N_DEV, a.shape[0] // N_DEV, a.shape[1])
    dil_slabs = [_cols_to_slabs(dw_dil_in), row_slabs(dw_dil_out), _cols_to_slabs(dw_up1), row_slabs(dw_gate1)]
    dil_push = _push_start(dil_slabs, True, "scatter_dil_start")
    group_major = lambda kb: jnp.where(kb < 3 * n_grp, (kb % 3) * n_grp + kb // 3, kb)
    dh2, d_dil_norm = _mm_in_bwd(dqkv + [dz1], w_dil_in, h2, dil_norm_full + dil_push[4][0:1, 0:1], r1, dh3, "dil_dx",
                                 w_kmap=group_major)

    du0, da0 = _ple_bwd_elem(dh2, u0, a0, "ple0_bwd_elem")
    dw_up0 = _dw(p0, du0, "ple0_dw_up")
    dw_gate0 = _dw(h1, da0, "ple0_dw_gate")
    dh1 = _mm_residual(da0, w_gate[0], "nt", dh2, "ple0_dh")

    dw_fox_out = _dw(g0, dh1, "fox_dw_out")
    do0, dz0, delta0 = _mm_gate_bwd(dh1, w_fox_out, proj0, 3 * d, o0, FOX_HEADS, "fox_dgate")
    head_cols = lambda a: jnp.pad(a, ((0, 0), (0, LANES - FOX_HEADS)))
    lse_cols = head_cols(lse0.reshape(FOX_HEADS, t).T)
    qaug_bwd = _fox_aug(c_all, lse_cols, 1.0, -1.0, 0, FOX_AUG, "fox_aug_q_bwd")
    doaug = _fox_aug(delta0, delta0, -1.0, 0.0, 0, None, "fox_aug_do")
    dq0, dk0, dv0, dck_wide, dcq = _fox_bwd(proj0, do0, qaug_bwd, kaug, doaug)
    dc_query = head_cols(dcq[:, :, 0, :].reshape(FOX_HEADS, t).T)
    df, d_b_f = _fox_gate_bwd(projf, b_pad, dc_query, dck_wide)
    dproj0 = [dq0, dk0, dv0, dz0]
    dw_fox_parts = [_dw(n0, dpart, f"fox_dw_in_{s}") for s, dpart in enumerate(dproj0)]
    dw_fox_f = _dw(n0, df, "fox_dw_gate")
    dn0_f = _mm([df], w_fox_f, "nt", "fox_dx_gate", out_dtype=F32)
    dw_fox_in = jnp.concatenate(dw_fox_parts + [dw_fox_f[:, :FOX_HEADS]], axis=1)
    fox_slabs = [_cols_to_slabs(dw_fox_in), row_slabs(dw_fox_out), _cols_to_slabs(dw_up0), row_slabs(dw_gate0)]
    fox_push = _push_start(fox_slabs, True, "scatter_fox_start")
    grad_x, d_fox_norm = _mm_in_bwd(dproj0, w_fox_main, xs, fox_norm + fox_push[4][0:1, 0:1], r0, dh1, "fox_dx",
                                    more=dn0_f)

    me = 4 * lax.axis_index("x") + 2 * lax.axis_index("y") + lax.axis_index("c")

    def landed(push, slabs, name):
        zones = _push_wait(push[0], push[1], push[2], push[3], grad_x, True, name)
        return [_fill_own(zone, lax.dynamic_index_in_dim(own, me, 0, keepdims=False)) for zone, own in zip(zones, slabs)]

    g_dil_in, g_dil_out, g_up1, g_gate1 = landed(dil_push, dil_slabs, "scatter_dil_wait")
    g_fox_in, g_fox_out, g_up0, g_gate0 = landed(fox_push, fox_slabs, "scatter_fox_wait")
    upd = {"fox_w_in": _adamw(g_fox_in, fox_w_in, m_fox_w_in, v_fox_w_in, "adamw_fox_w_in"),
           "fox_w_out": _adamw(g_fox_out, fox_w_out, m_fox_w_out, v_fox_w_out, "adamw_fox_w_out"),
           "dil_w_in": _adamw(g_dil_in, dil_w_in, m_dil_w_in, v_dil_w_in, "adamw_dil_w_in"),
           "dil_w_out": _adamw(g_dil_out, dil_w_out, m_dil_w_out, v_dil_w_out, "adamw_dil_w_out")}
    for nm, grads, params in (("ple_w_up", (g_up0, g_up1), (ple_w_up, m_ple_w_up, v_ple_w_up)),
                              ("ple_w_gate", (g_gate0, g_gate1), (ple_w_gate, m_ple_w_gate, v_ple_w_gate))):
        layers = [_adamw(g, *[a[l:l + 1] for a in params], f"adamw_{nm}_{l}") for l, g in enumerate(grads)]
        upd[nm] = [jnp.concatenate([layers[0][k], layers[1][k]], axis=0) for k in range(4)]

    loss_row = jnp.where(jnp.arange(LANES) == 0, loss_part, 0.0)
    vec = jnp.concatenate([_rows8(d_fox_norm), _rows8(d_final_norm), _rows8(d_dil_norm), d_b_f, loss_row,
                           jnp.zeros((VEC_ROWS - 26, LANES), F32)], axis=0)

    def small_pack(a_fox_norm, a_final_norm, a_dil_norm, a_b_f):
        return jnp.concatenate([_rows8(a_fox_norm), _rows8(a_final_norm), _pad_rows(a_dil_norm, 8),
                                _pad_rows(a_b_f, 8)], axis=0)

    sg, sd, sm, sv = _small_allreduce_adamw(
        vec, small_pack(fox_norm, final_norm, dil_norm, fox_b_f),
        small_pack(m_fox_norm, m_final_norm, m_dil_norm, m_fox_b_f),
        small_pack(v_fox_norm, v_final_norm, v_dil_norm, v_fox_b_f))

    def small_unpack(a):
        return {"fox_norm": a[0:8].reshape(1, d), "final_norm": a[8:16].reshape(d), "dil_norm": a[16:17],
                "fox_b_f": a[24:25, :FOX_HEADS]}

    loss = sg[25, 0]
    order = ["fox_norm", "fox_w_in", "fox_b_f", "fox_w_out", "dil_norm", "dil_w_in", "dil_w_out", "ple_w_up",
             "ple_w_gate", "final_norm"]
    out = [loss, grad_x[None]]
    for idx, small in enumerate((sg, sd, sm, sv)):
        sp = small_unpack(small)
        out += [sp[nm] if nm in sp else upd[nm][idx] for nm in order]
    return tuple(out)
```

```python
import functools

import jax
import jax.numpy as jnp
from jax import lax
from jax.experimental import pallas as pl
from jax.experimental.pallas import tpu as pltpu

F32 = jnp.float32
BF16 = jnp.bfloat16
SDS = jax.ShapeDtypeStruct

D_MODEL = 1024
N_DEV = 8
LANES = 128
FOX_HEADS = 16
FOX_HEAD_DIM = 64
FOX_PAIRS = FOX_HEADS // 2
DIL_HEADS = 8
DIL_BLOCK = 128
DIL_PATTERN = ((128, 1), (512, 4), (2048, 16))
ALIBI_MAX_EXP = 8.0
RMS_EPS = 1e-6
ADAM_LR, ADAM_B1, ADAM_B2, ADAM_EPS, ADAM_WD, ADAM_STEP = 0.001, 0.9, 0.999, 1e-08, 0.01, 10
VMEM_LIMIT = 48 * 1024 * 1024
NEG_INF = float("-inf")

NN = (((1,), (0,)), ((), ()))
NT = (((1,), (1,)), ((), ()))
TN = (((0,), (0,)), ((), ()))
MESH = pl.DeviceIdType.MESH


def _pcall(body, **kw):
    return pl.pallas_call(body, **kw)


def _params(**kw):
    return pltpu.CompilerParams(vmem_limit_bytes=VMEM_LIMIT, **kw)


def _dot(a, b, dims):
    return lax.dot_general(a, b, dims, preferred_element_type=F32)


def _sigmoid(x):
    return 1.0 / (1.0 + jnp.exp(-x))


def _mm(a_parts, b, mode, name, tiles=(512, 1024, 1024), extras=(), outs=None, epilogue=None, out_dtype=BF16,
        b_kmap=None, b_sub=None):
    na = len(a_parts)
    stack = [a.shape[0] if a.ndim == 3 else 1 for a in a_parts]
    first = [sum(stack[:s]) for s in range(na)]
    if mode == "tn":
        k_part, m = a_parts[0].shape
        n = b.shape[-1]
    else:
        m, k_part = a_parts[0].shape[-2:]
        n = b.shape[1] if mode == "nn" else b.shape[0]
    tm, tn, tk = min(tiles[0], m), min(tiles[1], n), min(tiles[2], k_part)
    kb = k_part // tk
    nk = sum(stack) * kb
    grid = (m // tm, n // tn, nk)
    b_kmap = b_kmap or (lambda k: k)

    in_specs = []
    for s in range(na):
        if mode == "tn":
            in_specs.append(pl.BlockSpec((tk, tm), lambda i, j, k: (k, i)))
            continue

        def rel(k, s=s):
            return jnp.clip(k - first[s] * kb, 0, stack[s] * kb - 1)

        if a_parts[s].ndim == 3:
            in_specs.append(pl.BlockSpec((None, tm, tk), lambda i, j, k, rel=rel: (rel(k) // kb, i, rel(k) % kb)))
        else:
            in_specs.append(pl.BlockSpec((tm, tk), lambda i, j, k, rel=rel: (i, rel(k))))
    if mode == "nt":
        in_specs.append(pl.BlockSpec((tn, tk), lambda i, j, k: (j, b_kmap(k))))
    elif b_sub is not None:
        in_specs.append(pl.BlockSpec((None, tk, tn), lambda i, j, k: (b_sub, k, j)))
    else:
        in_specs.append(pl.BlockSpec((tk, tn), lambda i, j, k: (b_kmap(k), j)))
    for _, blk, imap in extras:
        in_specs.append(pl.BlockSpec(blk, imap))
    if outs is None:
        outs = [(SDS((m, n), out_dtype), (tm, tn), lambda i, j, k: (i, j))]
    out_specs = [pl.BlockSpec(blk, imap) for _, blk, imap in outs]
    ne, no = len(extras), len(outs)
    dims = {"nn": NN, "nt": NT, "tn": TN}[mode]

    def finish(res, e_refs, o_refs, i):
        if epilogue is None:
            o_refs[0][...] = res.astype(o_refs[0].dtype)
        else:
            epilogue(res, e_refs, o_refs, i)

    def body(*refs):
        a_refs = refs[:na]
        b_ref = refs[na]
        e_refs = refs[na + 1:na + 1 + ne]
        o_refs = refs[na + 1 + ne:na + 1 + ne + no]
        i, k = pl.program_id(0), pl.program_id(2)
        if nk == 1:
            finish(_dot(a_refs[0][...].astype(BF16), b_ref[...].astype(BF16), dims), e_refs, o_refs, i)
            return
        acc = refs[-1]

        @pl.when(k == 0)
        def _():
            acc[...] = jnp.zeros_like(acc)

        def step(a_ref):
            acc[...] += _dot(a_ref[...].astype(BF16), b_ref[...].astype(BF16), dims)

        for s in range(na):
            if na == 1:
                step(a_refs[0])
            else:
                in_use = (k >= first[s] * kb) & (k < (first[s] + stack[s]) * kb)
                pl.when(in_use)(functools.partial(step, a_refs[s]))

        @pl.when(k == nk - 1)
        def _():
            finish(acc[...], e_refs, o_refs, i)

    res = _pcall(
        body, name=name, grid=grid, in_specs=in_specs, out_specs=out_specs,
        out_shape=[o[0] for o in outs], scratch_shapes=[] if nk == 1 else [pltpu.VMEM((tm, tn), F32)],
        compiler_params=_params(dimension_semantics=("arbitrary", "arbitrary", "arbitrary")),
    )(*a_parts, b, *[e[0] for e in extras])
    return res[0] if len(res) == 1 else res


IN_PROJ_TILES = (1024, 1024, 1024)
DW_TILES = (1024, 1024, 1024)


def _dw(x, dy, name, sub=None):
    return _mm([x], dy, "tn", name, tiles=DW_TILES, b_sub=sub)


def _add_extra_epilogue(acc, e_refs, o_refs, i):
    o_refs[0][...] = acc + e_refs[0][...]


def _mm_residual(a, b, mode, res, name):
    m = a.shape[0]
    n = b.shape[1] if mode == "nn" else b.shape[0]
    tm, tn = 512, 1024
    return _mm([a], b, mode, name, tiles=(tm, tn, 1024),
               extras=[(res, (tm, tn), lambda i, j, k: (i, j))],
               outs=[(SDS((m, n), F32), (tm, tn), lambda i, j, k: (i, j))],
               epilogue=_add_extra_epilogue)


def _rms_fwd(h, g, name):
    t, d = h.shape
    tm = 512

    def body(h_ref, g_ref, n_ref, r_ref):
        x = h_ref[...]
        r = lax.rsqrt(jnp.mean(x * x, axis=-1, keepdims=True) + RMS_EPS)
        n_ref[...] = ((x * r) * g_ref[...]).astype(BF16)
        r_ref[...] = r

    return _pcall(
        body, name=name, grid=(t // tm,),
        in_specs=[pl.BlockSpec((tm, d), lambda i: (i, 0)), pl.BlockSpec((1, d), lambda i: (0, 0))],
        out_specs=[pl.BlockSpec((tm, d), lambda i: (i, 0)), pl.BlockSpec((tm, 1), lambda i: (i, 0))],
        out_shape=[SDS((t, d), BF16), SDS((t, 1), F32)],
        compiler_params=_params(),
    )(h, g)


def _rms_bwd_rows(dn, x, g, r):
    xhat = x * r
    dxhat = dn * g
    dx = r * (dxhat - xhat * jnp.mean(dxhat * xhat, axis=-1, keepdims=True))
    dg = jnp.sum(dn * xhat, axis=0, keepdims=True)
    return dx, dg


def _mm_in_bwd(d_parts, w, h, g, r, dres, name, more=None, w_kmap=None):
    t = h.shape[0]
    tm = 512
    tk = D_MODEL
    row = lambda i, j, k: (i, 0)
    extras = [(h, (tm, D_MODEL), row), (g, (1, D_MODEL), lambda i, j, k: (0, 0)), (r, (tm, 1), row),
              (dres, (tm, D_MODEL), row)]
    if more is not None:
        extras += [(more[0], (tm, more[0].shape[1]), row), (more[1], more[1].shape, lambda i, j, k: (0, 0))]

    def epilogue(acc, e_refs, o_refs, i):
        dn = acc if more is None else acc + _dot(e_refs[4][...], e_refs[5][...], NT)
        dx, dg = _rms_bwd_rows(dn, e_refs[0][...], e_refs[1][...], e_refs[2][...])
        o_refs[0][...] = e_refs[3][...] + dx

        @pl.when(i == 0)
        def _():
            o_refs[1][...] = dg

        @pl.when(i > 0)
        def _():
            o_refs[1][...] += dg

    return _mm(d_parts, w, "nt", name, tiles=(tm, D_MODEL, tk), extras=extras,
               outs=[(SDS((t, D_MODEL), F32), (tm, D_MODEL), row),
                     (SDS((1, D_MODEL), F32), (1, D_MODEL), lambda i, j, k: (0, 0))],
               epilogue=epilogue, b_kmap=w_kmap)


def _final_bwd(h, g, tgt):
    t, d = h.shape
    tm = 256

    def body(h_ref, g_ref, t_ref, dh_ref, dg_ref, loss_ref):
        i = pl.program_id(0)
        x = h_ref[...]
        gg = g_ref[...]
        r = lax.rsqrt(jnp.mean(x * x, axis=-1, keepdims=True) + RMS_EPS)
        err = (x * r) * gg - t_ref[...]
        part = 0.5 * jnp.sum(jnp.mean(err * err, axis=-1, keepdims=True), axis=0, keepdims=True)
        dx, dg = _rms_bwd_rows(err * (1.0 / d), x, gg, r)
        dh_ref[...] = dx

        @pl.when(i == 0)
        def _():
            dg_ref[...] = dg
            loss_ref[...] = jnp.broadcast_to(part, loss_ref.shape)

        @pl.when(i > 0)
        def _():
            dg_ref[...] += dg
            loss_ref[...] += jnp.broadcast_to(part, loss_ref.shape)

    return _pcall(
        body, name="final_norm_loss", grid=(t // tm,),
        in_specs=[pl.BlockSpec((tm, d), lambda i: (i, 0)), pl.BlockSpec((1, d), lambda i: (0, 0)),
                  pl.BlockSpec((tm, d), lambda i: (i, 0))],
        out_specs=[pl.BlockSpec((tm, d), lambda i: (i, 0)), pl.BlockSpec((1, d), lambda i: (0, 0)),
                   pl.BlockSpec((1, LANES), lambda i: (0, 0))],
        out_shape=[SDS((t, d), F32), SDS((1, d), F32), SDS((1, LANES), F32)],
        compiler_params=_params(),
    )(h, g, tgt)


GATE_ROWS = 256


def _split3(x):
    hi = x.astype(BF16)
    r1 = x - hi.astype(F32)
    mid = r1.astype(BF16)
    lo = (r1 - mid.astype(F32)).astype(BF16)
    return hi, mid, lo


def _tri_sum(x, upper):
    rows = x.shape[0]
    ri = lax.broadcasted_iota(jnp.int32, (rows, rows), 0)
    ci = lax.broadcasted_iota(jnp.int32, (rows, rows), 1)
    tri = jnp.where((ri <= ci) if upper else (ri >= ci), 1.0, 0.0).astype(BF16)
    hi, mid, lo = _split3(x)
    return _dot(tri, hi, NN) + _dot(tri, mid, NN) + _dot(tri, lo, NN)


def _log_sigmoid(x):
    return jnp.minimum(x, 0.0) - jnp.log1p(jnp.exp(-jnp.abs(x)))


def _fox_gate_fwd(projf, bpad):
    t = projf.shape[0]
    tb = GATE_ROWS

    def body(x_ref, b_ref, c_ref, carry):
        i = pl.program_id(0)

        @pl.when(i == 0)
        def _():
            carry[...] = jnp.zeros_like(carry)

        c_ref[...] = _tri_sum(_log_sigmoid(x_ref[...] + b_ref[...]), upper=False) + carry[...]
        carry[...] = c_ref[pl.ds(tb - 1, 1), :]

    return _pcall(
        body, name="fox_gate_fwd", grid=(t // tb,),
        in_specs=[pl.BlockSpec((tb, LANES), lambda i: (i, 0)), pl.BlockSpec((1, LANES), lambda i: (0, 0))],
        out_specs=pl.BlockSpec((tb, LANES), lambda i: (i, 0)),
        out_shape=SDS((t, LANES), F32), scratch_shapes=[pltpu.VMEM((1, LANES), F32)],
        compiler_params=_params(),
    )(projf, bpad)


def _fox_gate_bwd(projf, bpad, dc_query, dc_key_wide):
    t = projf.shape[0]
    tb = GATE_ROWS
    nb = t // tb

    def body(x_ref, b_ref, dcq_ref, dck_ref, df_ref, db_ref, carry, buf):
        i = pl.program_id(0)

        @pl.when(i == 0)
        def _():
            carry[...] = jnp.zeros_like(carry)

        src = lax.broadcasted_iota(jnp.int32, (D_MODEL, LANES), 0)
        head = lax.broadcasted_iota(jnp.int32, (D_MODEL, LANES), 1)
        pick = jnp.where((head < FOX_HEADS) & (src == (head // 2) * LANES + (1 - head % 2) * FOX_HEAD_DIM), 1.0, 0.0)
        pick = pick.astype(BF16)
        dc_key = sum(_dot(piece, pick, NN) for piece in _split3(dck_ref[...]))
        buf[...] = _tri_sum(dcq_ref[...] - dc_key, upper=True) + carry[...]
        carry[...] = buf[pl.ds(0, 1), :]
        df = buf[...] * _sigmoid(-(x_ref[...] + b_ref[...]))
        df_ref[...] = df.astype(BF16)
        part = jnp.sum(df, axis=0, keepdims=True)

        @pl.when(i == 0)
        def _():
            db_ref[...] = part

        @pl.when(i > 0)
        def _():
            db_ref[...] += part

    rev = lambda i: (nb - 1 - i, 0)
    return _pcall(
        body, name="fox_gate_bwd", grid=(nb,),
        in_specs=[pl.BlockSpec((tb, LANES), rev), pl.BlockSpec((1, LANES), lambda i: (0, 0)),
                  pl.BlockSpec((tb, LANES), rev), pl.BlockSpec((tb, D_MODEL), rev)],
        out_specs=[pl.BlockSpec((tb, LANES), rev), pl.BlockSpec((1, LANES), lambda i: (0, 0))],
        out_shape=[SDS((t, LANES), BF16), SDS((1, LANES), F32)],
        scratch_shapes=[pltpu.VMEM((1, LANES), F32), pltpu.VMEM((tb, LANES), F32)],
        compiler_params=_params(),
    )(projf, bpad, dc_query, dc_key_wide)


FOX_TQ = 1024
FOX_TQ_FWD = 1024
FOX_SCALE = FOX_HEAD_DIM ** -0.5


def _low_lanes(shape):
    return lax.broadcasted_iota(jnp.int32, shape, len(shape) - 1) < FOX_HEAD_DIM


FOX_AUG = 3
FOX_CHAIN = 256
FOX_SUM_ROWS = 8


def _top_rows(shape):
    return lax.broadcasted_iota(jnp.int32, shape, 0) < FOX_HEAD_DIM


def _fox_aug(a, b, sign_a, sign_b, piece_entry, ones_entry, name):
    t = a.shape[0]
    tb = 512

    def body(a_ref, b_ref, o_ref):
        x = sign_a * a_ref[...]
        if sign_b != 0.0:
            x = x + sign_b * b_ref[...]
        head = lax.broadcasted_iota(jnp.int32, (LANES, D_MODEL), 0)
        col = lax.broadcasted_iota(jnp.int32, (LANES, D_MODEL), 1)
        base = (head // 2) * LANES + (1 - head % 2) * FOX_HEAD_DIM + piece_entry
        acc = jnp.zeros((tb, D_MODEL), F32)
        for e, piece in enumerate(_split3(x)):
            place = jnp.where((head < FOX_HEADS) & (col == base + e), 1.0, 0.0).astype(BF16)
            acc = acc + _dot(piece, place, NN)
        if ones_entry is not None:
            ent = lax.broadcasted_iota(jnp.int32, (1, D_MODEL), 1) % FOX_HEAD_DIM
            acc = acc + jnp.where((ent >= ones_entry) & (ent < ones_entry + FOX_AUG), 1.0, 0.0)
        o_ref[...] = acc.astype(BF16)

    blk = pl.BlockSpec((tb, LANES), lambda i: (i, 0))
    return _pcall(
        body, name=name, grid=(t // tb,), in_specs=[blk, blk],
        out_specs=pl.BlockSpec((tb, D_MODEL), lambda i: (i, 0)), out_shape=SDS((t, D_MODEL), BF16),
        compiler_params=_params(),
    )(a, b)


def _causal_steps(nq, key_major):
    if key_major:
        pairs = [(i, j) for j in range(nq) for i in range(j, nq)]
    else:
        pairs = [(i, j) for i in range(nq) for j in range(i + 1)]
    return (jnp.asarray([p[0] for p in pairs], jnp.int32), jnp.asarray([p[1] for p in pairs], jnp.int32))


def _pair_operand(low, own, other, hh):
    return jnp.where(low, own, other) if hh == 0 else jnp.where(low, other, own)


def _fox_fwd(proj, qaug, kaug):
    t = proj.shape[0]
    tq = tk = min(FOX_TQ_FWD, t)
    nq = t // tq
    cb = D_MODEL // LANES
    half = min(FOX_CHAIN, tq)

    i_tab, j_tab = _causal_steps(nq, key_major=False)

    def body(i_ref, j_ref, q_ref, k_ref, v_ref, z_ref, qa_ref, ka_ref, o_ref, g_ref, lse_ref, m_s, l_s, acc_s):
        step = pl.program_id(1)
        i, j = i_ref[step], j_ref[step]

        @pl.when(j == 0)
        def _():
            m_s[...] = jnp.full_like(m_s, NEG_INF)
            l_s[...] = jnp.zeros_like(l_s)
            acc_s[...] = jnp.zeros_like(acc_s)

        low = _low_lanes((tq, LANES))
        top = _top_rows((LANES, tq))

        def update(masked):
            qs = q_ref[...] * FOX_SCALE
            qa, k, ka, v = qa_ref[...], k_ref[...], ka_ref[...], v_ref[...]
            if masked:
                causal = (lax.broadcasted_iota(jnp.int32, (tk, tq), 0) <= lax.broadcasted_iota(jnp.int32, (tk, tq), 1))
            one = jnp.ones_like(v)
            chains = [(hh, slice(c * half, (c + 1) * half)) for hh in range(2) for c in range(tq // half)]
            qh = [_pair_operand(low, qs, qa, hh) for hh in range(2)]
            kh = [_pair_operand(low, k, ka, hh) for hh in range(2)]
            vh = [_pair_operand(low, v, one, hh) for hh in range(2)]
            keys = lambda cols: slice(0, cols.stop) if masked else slice(None)
            scores = [_dot(kh[hh][keys(cols), :], qh[hh][cols, :], NT) for hh, cols in chains]
            for (hh, cols), s in zip(chains, scores):
                if masked:
                    s = jnp.where(causal[keys(cols), cols], s, NEG_INF)
                m_prev = m_s[hh, :, cols]
                m_new = jnp.maximum(m_prev, jnp.max(s, axis=0, keepdims=True))
                alpha = jnp.exp(m_prev - m_new)
                pv = _dot(vh[hh][keys(cols), :], jnp.exp(s - m_new).astype(BF16), TN)
                sums = pv[FOX_HEAD_DIM:FOX_HEAD_DIM + FOX_SUM_ROWS, :] if hh == 0 else pv[0:FOX_SUM_ROWS, :]
                l_s[hh, :, cols] = alpha * l_s[hh, :, cols] + sums
                m_s[hh, :, cols] = m_new
                own = top[:, cols] if hh == 0 else jnp.logical_not(top[:, cols])
                acc_s[:, cols] = jnp.where(own, acc_s[:, cols] * alpha + pv, acc_s[:, cols])

        pl.when(j < i)(functools.partial(update, False))
        pl.when(j == i)(functools.partial(update, True))

        @pl.when(j == i)
        def _():
            o = (acc_s[...] / jnp.where(top, l_s[0, 0:1, :], l_s[1, 0:1, :])).T
            z = z_ref[...].astype(F32)
            o_ref[...] = o.astype(BF16)
            g_ref[...] = (o * (z * _sigmoid(z))).astype(BF16)
            for hh in range(2):
                lse_ref[hh] = m_s[hh] + jnp.log(l_s[hh, 0:1, :])

    qblk = lambda col: pl.BlockSpec((tq, LANES), lambda h, s, it, jt: (it[s], col + h))
    kblk = lambda col: pl.BlockSpec((tk, LANES), lambda h, s, it, jt: (jt[s], col + h))
    return _pcall(
        body, name="fox_attn_fwd",
        grid_spec=pltpu.PrefetchScalarGridSpec(
            num_scalar_prefetch=2, grid=(FOX_PAIRS, i_tab.shape[0]),
            in_specs=[qblk(0), kblk(cb), kblk(2 * cb), qblk(3 * cb), qblk(0), kblk(0)],
            out_specs=[qblk(0), qblk(0), pl.BlockSpec((2, 1, tq), lambda h, s, it, jt: (h, 0, it[s]))],
            scratch_shapes=[pltpu.VMEM((2, 1, tq), F32), pltpu.VMEM((2, FOX_SUM_ROWS, tq), F32),
                            pltpu.VMEM((LANES, tq), F32)]),
        out_shape=[SDS((t, D_MODEL), BF16), SDS((t, D_MODEL), BF16), SDS((FOX_HEADS, 1, t), F32)],
        compiler_params=_params(dimension_semantics=("arbitrary", "arbitrary")),
    )(i_tab, j_tab, proj, proj, proj, proj, qaug, kaug)


def _fox_bwd(proj, do, qaug, kaug, doaug):
    t = proj.shape[0]
    tq = tk = min(FOX_TQ, t)
    nq = t // tq
    cb = D_MODEL // LANES
    half = min(FOX_CHAIN, tq)

    i_tab, j_tab = _causal_steps(nq, key_major=True)

    def body(i_ref, j_ref, q_ref, k_ref, v_ref, do_ref, qa_ref, ka_ref, da_ref,
             dq_ref, dk_ref, dv_ref, dck_ref, dcq_ref, dq_acc, dcq_acc, dk_acc, dks_acc, dv_acc):
        step = pl.program_id(1)
        i, j = i_ref[step], j_ref[step]
        low = _low_lanes((tq, LANES))
        top = _top_rows((LANES, tq))

        @pl.when(i == j)
        def _():
            dk_acc[...] = jnp.zeros_like(dk_acc)
            dks_acc[...] = jnp.zeros_like(dks_acc)
            dv_acc[...] = jnp.zeros_like(dv_acc)

        def update(masked):
            qs = q_ref[...] * FOX_SCALE
            k, v, dout = k_ref[...], v_ref[...], do_ref[...]
            qa, ka, da = qa_ref[...], ka_ref[...], da_ref[...]
            lane = lax.broadcasted_iota(jnp.int32, (tk, LANES), 1)
            vone = jnp.where((lane & (FOX_HEAD_DIM - 1)) < FOX_AUG, 1.0, 0.0).astype(v.dtype)
            one = jnp.ones_like(k)
            if masked:
                causal = (lax.broadcasted_iota(jnp.int32, (tk, tq), 0) <= lax.broadcasted_iota(jnp.int32, (tk, tq), 1))
            parts = []
            kh = [_pair_operand(low, k, ka, hh) for hh in range(2)]
            qh = [_pair_operand(low, qs, qa, hh) for hh in range(2)]
            vh = [_pair_operand(low, v, vone, hh) for hh in range(2)]
            doh = [_pair_operand(low, dout, da, hh) for hh in range(2)]
            q1 = [_pair_operand(low, qs, one, hh) for hh in range(2)]
            k1 = [_pair_operand(low, k, one, hh) for hh in range(2)]

            def tile(hh, keys, cols, s, dp):
                if masked:
                    s = jnp.where(causal[keys, cols], s, NEG_INF)
                p = jnp.exp(s)
                pb, dsb = p.astype(BF16), (p * dp).astype(BF16)
                return (_dot(pb, dout[cols, :], NN), _dot(dsb, q1[hh][cols, :], NN), _dot(k1[hh][keys, :], dsb, TN))

            if not masked:
                scores = [_dot(kh[hh], qh[hh], NT) for hh in range(2)]
                dps = [_dot(vh[hh], doh[hh], NT) for hh in range(2)]
                everything = slice(None)
                parts = [tile(hh, everything, everything, scores[hh], dps[hh]) for hh in range(2)]
            else:
                for hh in range(2):
                    dv_h, dk_h, dq_h = jnp.zeros((tk, LANES), F32), jnp.zeros((tk, LANES), F32), []
                    for c in range(tq // half):
                        cols, keys = slice(c * half, (c + 1) * half), slice(0, (c + 1) * half)
                        dv_c, dk_c, dq_c = tile(hh, keys, cols, _dot(kh[hh][keys, :], qh[hh][cols, :], NT),
                                                _dot(vh[hh][keys, :], doh[hh][cols, :], NT))
                        below = ((0, tk - keys.stop), (0, 0))
                        dv_h, dk_h = dv_h + jnp.pad(dv_c, below), dk_h + jnp.pad(dk_c, below)
                        dq_h.append(dq_c)
                    parts.append((dv_h, dk_h, jnp.concatenate(dq_h, axis=1)))
            dv_acc[...] += jnp.where(low, parts[0][0], parts[1][0])
            dk_acc[...] += jnp.where(low, parts[0][1], parts[1][1])
            dks_acc[...] += jnp.where(low, parts[1][1], parts[0][1])
            dq_t = jnp.where(top, parts[0][2], parts[1][2]) * FOX_SCALE
            sum_a = parts[0][2][FOX_HEAD_DIM:FOX_HEAD_DIM + FOX_SUM_ROWS, :]
            sum_b = parts[1][2][0:FOX_SUM_ROWS, :]

            @pl.when(j == 0)
            def _():
                dq_acc[i] = dq_t
                dcq_acc[0, i] = sum_a
                dcq_acc[1, i] = sum_b

            @pl.when(j > 0)
            def _():
                dq_acc[i] += dq_t
                dcq_acc[0, i] += sum_a
                dcq_acc[1, i] += sum_b

        pl.when(i > j)(functools.partial(update, False))
        pl.when(i == j)(functools.partial(update, True))

        @pl.when(i == nq - 1)
        def _():
            dk_ref[...] = dk_acc[...].astype(BF16)
            dv_ref[...] = dv_acc[...].astype(BF16)
            dck_ref[...] = dks_acc[...]

        @pl.when((i == nq - 1) & (j == nq - 1))
        def _():
            for blk in range(nq):
                dq_ref[blk * tq:(blk + 1) * tq, :] = dq_acc[blk].T.astype(BF16)
            dcq_ref[...] = dcq_acc[...]

    qblk = lambda col: pl.BlockSpec((tq, LANES), lambda h, s, it, jt: (it[s], col + h))
    kblk = lambda col: pl.BlockSpec((tk, LANES), lambda h, s, it, jt: (jt[s], col + h))
    return _pcall(
        body, name="fox_attn_bwd",
        grid_spec=pltpu.PrefetchScalarGridSpec(
            num_scalar_prefetch=2, grid=(FOX_PAIRS, i_tab.shape[0]),
            in_specs=[qblk(0), kblk(cb), kblk(2 * cb), qblk(0), qblk(0), kblk(0), qblk(0)],
            out_specs=[pl.BlockSpec((t, LANES), lambda h, s, it, jt: (0, h)), kblk(0), kblk(0), kblk(0),
                       pl.BlockSpec((2, nq, FOX_SUM_ROWS, tq), lambda h, s, it, jt: (h, 0, 0, 0))],
            scratch_shapes=[pltpu.VMEM((nq, LANES, tq), F32), pltpu.VMEM((2, nq, FOX_SUM_ROWS, tq), F32),
                            pltpu.VMEM((tk, LANES), F32), pltpu.VMEM((tk, LANES), F32), pltpu.VMEM((tk, LANES), F32)]),
        out_shape=[SDS((t, D_MODEL), BF16), SDS((t, D_MODEL), BF16), SDS((t, D_MODEL), BF16),
                   SDS((t, D_MODEL), F32), SDS((FOX_HEADS, nq, FOX_SUM_ROWS, tq), F32)],
        compiler_params=_params(dimension_semantics=("arbitrary", "arbitrary")),
    )(i_tab, j_tab, proj, proj, proj, do, qaug, kaug, doaug)


def _mm_gate_bwd(dh, w_out, z_src, z_col0, o, heads, name):
    t = dh.shape[0]
    tm = 512
    row = lambda i, j, k: (i, 0)
    zcb = z_col0 // D_MODEL
    per_block = heads == D_MODEL // LANES

    def epilogue(acc, e_refs, o_refs, i):
        z = e_refs[0][...].astype(F32)
        ov = e_refs[1][...].astype(F32)
        sg = _sigmoid(z)
        dout = acc * (z * sg)
        o_refs[0][...] = dout.astype(BF16)
        o_refs[1][...] = (acc * ov * (sg * (1.0 + z * (1.0 - sg)))).astype(BF16)
        prod = dout * ov
        lane = lax.broadcasted_iota(jnp.int32, (tm, LANES), 1)
        cols = jnp.zeros((tm, LANES), F32)
        for cbk in range(D_MODEL // LANES):
            seg = prod[:, cbk * LANES:(cbk + 1) * LANES]
            tot = jnp.sum(seg, axis=-1, keepdims=True)
            if per_block:
                o_refs[2][cbk] = tot
            else:
                lo = jnp.sum(jnp.where(_low_lanes(seg.shape), seg, 0.0), axis=-1, keepdims=True)
                cols = jnp.where(lane == 2 * cbk, lo, jnp.where(lane == 2 * cbk + 1, tot - lo, cols))
        if not per_block:
            o_refs[2][...] = cols

    delta_out = ((SDS((heads, t, 1), F32), (heads, tm, 1), lambda i, j, k: (0, i, 0)) if per_block
                 else (SDS((t, LANES), F32), (tm, LANES), row))
    return _mm([dh], w_out, "nt", name, tiles=(tm, D_MODEL, D_MODEL),
               extras=[(z_src, (tm, D_MODEL), lambda i, j, k: (i, zcb)), (o, (tm, D_MODEL), row)],
               outs=[(SDS((t, D_MODEL), BF16), (tm, D_MODEL), row), (SDS((t, D_MODEL), BF16), (tm, D_MODEL), row),
                     delta_out],
               epilogue=epilogue)


def _ple_fwd(h, pin, w_up, w_gate, name):
    t = h.shape[0]
    tm = 512
    pd = pin.shape[1]

    def body(h_ref, p_ref, wu_ref, wg_ref, hn_ref, u_ref, a_ref):
        h = h_ref[...]
        u = _dot(p_ref[...].astype(BF16), wu_ref[...], NN)
        a = _dot(h.astype(BF16), wg_ref[...], NN)
        hn_ref[...] = h + u * _sigmoid(a)
        u_ref[...] = u.astype(BF16)
        a_ref[...] = a.astype(BF16)

    rows = pl.BlockSpec((tm, D_MODEL), lambda i: (i, 0))
    return _pcall(
        body, name=name, grid=(t // tm,),
        in_specs=[rows, pl.BlockSpec((tm, pd), lambda i: (i, 0)),
                  pl.BlockSpec((pd, D_MODEL), lambda i: (0, 0)), pl.BlockSpec((D_MODEL, D_MODEL), lambda i: (0, 0))],
        out_specs=[rows, rows, rows],
        out_shape=[SDS((t, D_MODEL), F32), SDS((t, D_MODEL), BF16), SDS((t, D_MODEL), BF16)],
        compiler_params=_params(),
    )(h, pin, w_up, w_gate)


def _ple_bwd_elem(dh, u, a, name):
    t = dh.shape[0]
    tm = 512

    def body(dh_ref, u_ref, a_ref, du_ref, da_ref):
        g = dh_ref[...]
        s = _sigmoid(a_ref[...].astype(F32))
        du_ref[...] = (g * s).astype(BF16)
        da_ref[...] = (g * u_ref[...].astype(F32) * (s * (1.0 - s))).astype(BF16)

    blk = pl.BlockSpec((tm, D_MODEL), lambda i: (i, 0))
    return _pcall(
        body, name=name, grid=(t // tm,), in_specs=[blk, blk, blk], out_specs=[blk, blk],
        out_shape=[SDS((t, D_MODEL), BF16), SDS((t, D_MODEL), BF16)], compiler_params=_params(),
    )(dh, u, a)


DIL_SCALE = LANES ** -0.5


def _dil_masks():
    ii = lax.broadcasted_iota(jnp.int32, (DIL_BLOCK, DIL_BLOCK), 0)
    jj = lax.broadcasted_iota(jnp.int32, (DIL_BLOCK, DIL_BLOCK), 1)
    return ii, jj


DIL_UNITS = 16
BNT = (((2,), (2,)), ((0,), (0,)))
BNN = (((2,), (1,)), ((0,), (0,)))
BTN = (((1,), (1,)), ((0,), (0,)))


def _dil_units(dil):
    return [(b, r) for b in range(DIL_UNITS // dil) for r in range(dil)]


DIL_MAX_STRIDE = 4


def _pre(dil):
    return max(dil // DIL_MAX_STRIDE, 1)


def _stage_in(src, dst, staging, dil, lead=()):
    n, pre = dst.shape[0], _pre(dil)
    if pre == 1:
        dst[...] = src[lead + (slice(None), slice(None))].astype(F32)
        return
    if src.dtype == F32:
        staging = src
    else:
        staging[...] = src[...].astype(F32)
    for q in range(pre):
        dst[q * (n // pre):(q + 1) * (n // pre), :] = staging[lead + (pl.ds(q, n // pre, stride=pre), slice(None))]


def _stage_out(src, dst, staging, dil, lead=()):
    n, pre = src.shape[0], _pre(dil)
    if pre == 1:
        dst[lead + (slice(None), slice(None))] = src[...].astype(dst.dtype)
        return
    out = dst if dst.dtype == F32 else staging
    for q in range(pre):
        out[(lead if dst.dtype == F32 else ()) + (pl.ds(q, n // pre, stride=pre), slice(None))] = \
            src[q * (n // pre):(q + 1) * (n // pre), :]
    if dst.dtype != F32:
        dst[...] = staging[...].astype(dst.dtype)


def _unit_rows(b, r, dil, n):
    pre = _pre(dil)
    return pl.ds((r % pre) * (n // pre) + (b * DIL_BLOCK * dil + r) // pre, DIL_BLOCK, stride=dil // pre)


def _gather_units(cur, dil, shift=0, edge=None):
    nbk = DIL_UNITS // dil
    parts = []
    for b, r in _dil_units(dil):
        bb = b + shift
        if 0 <= bb < nbk:
            parts.append(cur[_unit_rows(bb, r, dil, cur.shape[0]), :])
        else:
            parts.append(edge[_unit_rows(0, r, dil, edge.shape[0]), :])
    return jnp.stack(parts)


def _scatter_units(dst, val, dil):
    for u, (b, r) in enumerate(_dil_units(dil)):
        dst[_unit_rows(b, r, dil, dst.shape[0]), :] = val[u]


def _dil_bias(slope, prev):
    ii, jj = _dil_masks()
    dist = (DIL_BLOCK + ii - jj) if prev else (ii - jj)
    return (slope * dist.astype(F32))[None], ((jj >= ii) if prev else (jj <= ii))[None]


def _dil_fwd(proj, slopes, grp, dil, name):
    t = proj.shape[0]
    rows = DIL_BLOCK * DIL_UNITS
    edge_rows = DIL_BLOCK * dil
    nbk = DIL_UNITS // dil
    nsb = t // rows
    qc, kc_, vc_ = grp * DIL_HEADS, 3 * DIL_HEADS + grp * DIL_HEADS, 6 * DIL_HEADS + grp * DIL_HEADS

    def body(q_ref, kp_ref, kc_ref, vp_ref, vc_ref, sl_ref, o_ref, lse_ref, qf, kpf, kcf, vpf, vcf, of, lf, staging):
        m = pl.program_id(1)
        for src, dst in ((q_ref, qf), (kp_ref, kpf), (kc_ref, kcf), (vp_ref, vpf), (vc_ref, vcf)):
            _stage_in(src, dst, staging, dil)
        slope = sl_ref[0]
        unit = lax.broadcasted_iota(jnp.int32, (DIL_UNITS, 1, 1), 0)
        has_prev = (unit >= dil) | (m > 0)
        q = _gather_units(qf, dil).astype(BF16)
        kc, vc = _gather_units(kcf, dil).astype(BF16), _gather_units(vcf, dil).astype(BF16)
        kp, vp = _gather_units(kcf, dil, -1, kpf).astype(BF16), _gather_units(vcf, dil, -1, vpf).astype(BF16)
        bias_p, ok_p = _dil_bias(slope, True)
        bias_c, ok_c = _dil_bias(slope, False)
        sp = jnp.where(ok_p & has_prev, _dot(q, kp, BNT) * DIL_SCALE - bias_p, NEG_INF)
        sc = jnp.where(ok_c, _dot(q, kc, BNT) * DIL_SCALE - bias_c, NEG_INF)
        mx = jnp.maximum(jnp.max(sp, axis=-1, keepdims=True), jnp.max(sc, axis=-1, keepdims=True))
        pp = jnp.exp(sp - mx)
        pc = jnp.exp(sc - mx)
        l = jnp.sum(pp, axis=-1, keepdims=True) + jnp.sum(pc, axis=-1, keepdims=True)
        o = (_dot(pp.astype(BF16), vp, BNN) + _dot(pc.astype(BF16), vc, BNN)) / l
        _scatter_units(of, o, dil)
        _scatter_units(lf, mx + jnp.log(l), dil)
        _stage_out(of, o_ref, staging, dil)
        _stage_out(lf, lse_ref, staging, dil, lead=(0,))

    cur = lambda col: pl.BlockSpec((rows, LANES), lambda h, m: (m, col + h))
    prev = lambda col: pl.BlockSpec((edge_rows, LANES), lambda h, m: (jnp.maximum(m * nbk - 1, 0), col + h))
    return _pcall(
        body, name=name, grid=(DIL_HEADS, nsb),
        in_specs=[cur(qc), prev(kc_), cur(kc_), prev(vc_), cur(vc_), pl.BlockSpec((1, 1, 1), lambda h, m: (h, 0, 0))],
        out_specs=[pl.BlockSpec((rows, LANES), lambda h, m: (m, h)), pl.BlockSpec((1, rows, 1), lambda h, m: (h, m, 0))],
        out_shape=[SDS((t, D_MODEL), BF16), SDS((DIL_HEADS, t, 1), F32)],
        scratch_shapes=[pltpu.VMEM((rows, LANES), F32), pltpu.VMEM((edge_rows, LANES), F32), pltpu.VMEM((rows, LANES), F32),
                        pltpu.VMEM((edge_rows, LANES), F32), pltpu.VMEM((rows, LANES), F32), pltpu.VMEM((rows, LANES), F32),
                        pltpu.VMEM((rows, 1), F32), pltpu.VMEM((rows, LANES), F32)],
        compiler_params=_params(),
    )(proj, proj, proj, proj, proj, slopes)


def _dil_mix(outs, lses, proj, z_col0):
    t = proj.shape[0]
    tm = 512
    zcb = z_col0 // LANES
    ng = len(outs)

    def body(*refs):
        o_refs, l_refs, z_ref = refs[:ng], refs[ng:2 * ng], refs[2 * ng]
        om_ref, g_ref, lse_ref = refs[2 * ng + 1:]
        ls = [r[0] for r in l_refs]
        mx = functools.reduce(jnp.maximum, ls)
        es = [jnp.exp(l - mx) for l in ls]
        tot = functools.reduce(jnp.add, es)
        o = functools.reduce(jnp.add, [(e / tot) * r[...].astype(F32) for e, r in zip(es, o_refs)])
        z = z_ref[...].astype(F32)
        om_ref[...] = o.astype(BF16)
        g_ref[...] = (o * (z * _sigmoid(z))).astype(BF16)
        lse_ref[0] = mx + jnp.log(tot)

    tile = pl.BlockSpec((tm, LANES), lambda i, h: (i, h))
    col = pl.BlockSpec((1, tm, 1), lambda i, h: (h, i, 0))
    return _pcall(
        body, name="dil_mix", grid=(t // tm, DIL_HEADS),
        in_specs=[tile] * ng + [col] * ng + [pl.BlockSpec((tm, LANES), lambda i, h: (i, zcb + h))],
        out_specs=[tile, tile, col],
        out_shape=[SDS((t, D_MODEL), BF16), SDS((t, D_MODEL), BF16), SDS((DIL_HEADS, t, 1), F32)],
        compiler_params=_params(),
    )(*outs, *lses, proj)


def _dil_bwd(proj, do, lse, delta, slopes, grp, dil, name):
    t = proj.shape[0]
    rows = DIL_BLOCK * DIL_UNITS
    edge_rows = DIL_BLOCK * dil
    nbk = DIL_UNITS // dil
    nsb = t // rows
    last_edge = t // edge_rows - 1
    qc, kc_, vc_ = grp * DIL_HEADS, 3 * DIL_HEADS + grp * DIL_HEADS, 6 * DIL_HEADS + grp * DIL_HEADS

    def body(q_ref, qn_ref, kp_ref, kc_ref, vp_ref, vc_ref, do_ref, don_ref, l_ref, ln_ref, d_ref, dn_ref, sl_ref,
             dqkv_ref, qf, qnf, kpf, kcf, vpf, vcf, dof, donf, dqf, dkf, dvf, staging, lf, lnf, df, dnf):
        m = pl.program_id(1)
        for src, dst in ((q_ref, qf), (qn_ref, qnf), (kp_ref, kpf), (kc_ref, kcf), (vp_ref, vpf), (vc_ref, vcf),
                         (do_ref, dof), (don_ref, donf)):
            _stage_in(src, dst, staging, dil)
        for src, dst in ((l_ref, lf), (ln_ref, lnf), (d_ref, df), (dn_ref, dnf)):
            _stage_in(src, dst, staging, dil, lead=(0,))
        slope = sl_ref[0]
        unit = lax.broadcasted_iota(jnp.int32, (DIL_UNITS, 1, 1), 0)
        has_prev = (unit >= dil) | (m > 0)
        has_next = (unit < DIL_UNITS - dil) | (m < nsb - 1)
        b16 = lambda x: x.astype(BF16)
        q, kc, vc, dout = (b16(_gather_units(x, dil)) for x in (qf, kcf, vcf, dof))
        kp, vp = b16(_gather_units(kcf, dil, -1, kpf)), b16(_gather_units(vcf, dil, -1, vpf))
        qn, don = b16(_gather_units(qf, dil, 1, qnf)), b16(_gather_units(dof, dil, 1, donf))
        lrow, drow = _gather_units(lf, dil), _gather_units(df, dil)
        lnrow, dnrow = _gather_units(lf, dil, 1, lnf), _gather_units(df, dil, 1, dnf)
        bias_p, ok_p = _dil_bias(slope, True)
        bias_c, ok_c = _dil_bias(slope, False)
        sp = jnp.where(ok_p & has_prev, _dot(q, kp, BNT) * DIL_SCALE - bias_p, NEG_INF)
        sc = jnp.where(ok_c, _dot(q, kc, BNT) * DIL_SCALE - bias_c, NEG_INF)
        pp = jnp.exp(sp - lrow)
        pc = jnp.exp(sc - lrow)
        dsp = b16(pp * (_dot(dout, vp, BNT) - drow))
        dsc = b16(pc * (_dot(dout, vc, BNT) - drow))
        _scatter_units(dqf, (_dot(dsp, kp, BNN) + _dot(dsc, kc, BNN)) * DIL_SCALE, dil)
        sn = jnp.where(ok_p & has_next, _dot(qn, kc, BNT) * DIL_SCALE - bias_p, NEG_INF)
        pn = jnp.exp(sn - lnrow)
        dsn = b16(pn * (_dot(don, vc, BNT) - dnrow))
        _scatter_units(dkf, (_dot(dsc, q, BTN) + _dot(dsn, qn, BTN)) * DIL_SCALE, dil)
        _scatter_units(dvf, _dot(b16(pc), dout, BTN) + _dot(b16(pn), don, BTN), dil)
        for s, src in enumerate((dqf, dkf, dvf)):
            _stage_out(src, dqkv_ref.at[s], staging, dil)

    prev_i = lambda m: jnp.maximum(m * nbk - 1, 0)
    next_i = lambda m: jnp.minimum((m + 1) * nbk, last_edge)
    cur = lambda col: pl.BlockSpec((rows, LANES), lambda h, m: (m, col + h))
    edge = lambda col, f: pl.BlockSpec((edge_rows, LANES), lambda h, m: (f(m), col + h))
    colcur = pl.BlockSpec((1, rows, 1), lambda h, m: (h, m, 0))
    colnext = pl.BlockSpec((1, edge_rows, 1), lambda h, m: (h, next_i(m), 0))
    out_blk = pl.BlockSpec((3, rows, LANES), lambda h, m: (0, m, h))
    big, small = pltpu.VMEM((rows, LANES), F32), pltpu.VMEM((edge_rows, LANES), F32)
    return _pcall(
        body, name=name, grid=(DIL_HEADS, nsb),
        in_specs=[cur(qc), edge(qc, next_i), edge(kc_, prev_i), cur(kc_), edge(vc_, prev_i), cur(vc_),
                  cur(0), edge(0, next_i), colcur, colnext, colcur, colnext,
                  pl.BlockSpec((1, 1, 1), lambda h, m: (h, 0, 0))],
        out_specs=out_blk, out_shape=SDS((3, t, D_MODEL), BF16),
        scratch_shapes=[big, small, small, big, small, big, big, small, big, big, big, big,
                        pltpu.VMEM((rows, 1), F32), pltpu.VMEM((edge_rows, 1), F32),
                        pltpu.VMEM((rows, 1), F32), pltpu.VMEM((edge_rows, 1), F32)],
        compiler_params=_params(),
    )(proj, proj, proj, proj, proj, proj, do, do, lse, lse, delta, delta, slopes)


def _mesh_pos():
    x, y, c = lax.axis_index("x"), lax.axis_index("y"), lax.axis_index("c")
    return x, y, c


def _peer(pos, k):
    x, y, c = pos
    px = 1 - x if k & 4 else x
    py = 1 - y if k & 2 else y
    pc = 1 - c if k & 1 else c
    return (px, py, pc), 4 * px + 2 * py + pc


N_CHIPS = 4
CHIP_FLIPS = ((1, 0), (0, 1), (1, 1))


def _other_chips(x, y):
    return [(1 - x if fx else x, 1 - y if fy else y) for fx, fy in CHIP_FLIPS]


def _all_gather(arrays):
    n = len(arrays)
    per = 2 * N_CHIPS - 1
    hbm = pl.BlockSpec(memory_space=pltpu.HBM)

    def body(*refs):
        ins, outs = refs[:n], refs[n:2 * n]
        send_sems, recv_sems, local_sems = refs[2 * n:]
        x, y, c = _mesh_pos()
        sibling = (x, y, 1 - c)
        chips = _other_chips(x, y)
        block = lambda px, py, pc: 4 * px + 2 * py + pc

        def copy(w, k, src, blk, to):
            return pltpu.make_async_remote_copy(
                src_ref=src, dst_ref=outs[w].at[blk], send_sem=send_sems.at[w * per + k],
                recv_sem=recv_sems.at[w * per + k], device_id=to, device_id_type=MESH)

        local, started = [], []
        for w in range(n):
            cp = pltpu.make_async_copy(ins[w], outs[w].at[block(x, y, c)], local_sems.at[w])
            cp.start()
            local.append(cp)
            started.append(copy(w, 0, ins[w], block(x, y, c), sibling))
            for j, (px, py) in enumerate(chips):
                started.append(copy(w, 1 + j, ins[w], block(x, y, c), (px, py, c)))
        for cp in started:
            cp.start()
        for j, (px, py) in enumerate(chips):
            for w in range(n):
                copy(w, 1 + j, ins[w], block(px, py, c), sibling).wait_recv()
                cp = copy(w, 4 + j, outs[w].at[block(px, py, c)], block(px, py, c), sibling)
                cp.start()
                started.append(cp)
        for w in range(n):
            copy(w, 0, ins[w], block(x, y, 1 - c), sibling).wait_recv()
            for j, (px, py) in enumerate(chips):
                copy(w, 4 + j, ins[w], block(px, py, 1 - c), sibling).wait_recv()
        for cp in started:
            cp.wait_send()
        for cp in local:
            cp.wait()

    return _pcall(
        body, name="all_gather_weights", in_specs=[hbm] * n, out_specs=[hbm] * n,
        out_shape=[SDS((N_DEV,) + a.shape, a.dtype) for a in arrays],
        scratch_shapes=[pltpu.SemaphoreType.DMA((n * per,)), pltpu.SemaphoreType.DMA((n * per,)),
                        pltpu.SemaphoreType.DMA((n,))],
    )(*arrays)


HBM_SPEC = pl.BlockSpec(memory_space=pltpu.HBM)
SEM_SPEC = pl.BlockSpec(memory_space=pltpu.SEMAPHORE)
DATAFLOW = pltpu.SideEffectType.DATAFLOW_SIDE_EFFECTING


def _push_start(arrays, scatter, name):
    n = len(arrays)
    per = N_DEV - 1

    def body(*refs):
        srcs, lands = refs[:n], refs[n:2 * n]
        send_sems, recv_sems, token = refs[2 * n], refs[2 * n + 1], refs[-1]
        pos = _mesh_pos()
        me = 4 * pos[0] + 2 * pos[1] + pos[2]
        for w in range(n):
            for k in range(1, N_DEV):
                peer, peer_idx = _peer(pos, k)
                pltpu.make_async_remote_copy(
                    src_ref=srcs[w].at[peer_idx] if scatter else srcs[w], dst_ref=lands[w].at[me],
                    send_sem=send_sems.at[w * per + k - 1], recv_sem=recv_sems.at[w * per + k - 1],
                    device_id=peer, device_id_type=MESH).start()
        token[...] = jnp.zeros_like(token)

    land_shapes = [a.shape if scatter else (N_DEV,) + a.shape for a in arrays]
    in_hbm = lambda a: pltpu.with_memory_space_constraint(a, pltpu.HBM)
    lands = [in_hbm(lax.empty(s, a.dtype)) for s, a in zip(land_shapes, arrays)]
    sems = pltpu.SemaphoreType.DMA((n * per,))
    res = _pcall(
        body, name=name,
        out_shape=(sems, sems, *[pltpu.HBM(a.shape, a.dtype) for a in arrays],
                   *[pltpu.HBM(s, a.dtype) for s, a in zip(land_shapes, arrays)], SDS((8, LANES), F32)),
        in_specs=[HBM_SPEC] * (2 * n),
        out_specs=(SEM_SPEC, SEM_SPEC, *[HBM_SPEC] * (2 * n), pl.BlockSpec(memory_space=pltpu.VMEM)),
        input_output_aliases={i: 2 + i for i in range(2 * n)},
        compiler_params=pltpu.CompilerParams(has_side_effects=DATAFLOW),
    )(*[in_hbm(a) for a in arrays], *lands)
    return res[0], res[1], list(res[2:2 + n]), list(res[2 + n:2 + 2 * n]), res[-1]


def _push_wait(send_sems, recv_sems, arrays, lands, after, scatter, name):
    n = len(arrays)
    per = N_DEV - 1

    def body(*refs):
        srcs, lands_ = refs[:n], refs[n:2 * n]
        send_sems_, recv_sems_ = refs[2 * n], refs[2 * n + 1]
        pos = _mesh_pos()
        for w in range(n):
            for k in range(1, N_DEV):
                peer, peer_idx = _peer(pos, k)
                cp = pltpu.make_async_remote_copy(
                    src_ref=srcs[w].at[peer_idx] if scatter else srcs[w], dst_ref=lands_[w].at[peer_idx],
                    send_sem=send_sems_.at[w * per + k - 1], recv_sem=recv_sems_.at[w * per + k - 1],
                    device_id=peer, device_id_type=MESH)
                cp.wait_send()
                cp.wait_recv()

    res = _pcall(
        body, name=name,
        out_shape=(*[pltpu.HBM(a.shape, a.dtype) for a in arrays], *[pltpu.HBM(l.shape, l.dtype) for l in lands]),
        in_specs=[HBM_SPEC] * (2 * n) + [SEM_SPEC, SEM_SPEC, pl.BlockSpec(memory_space=pl.ANY)],
        out_specs=[HBM_SPEC] * (2 * n), input_output_aliases={i: i for i in range(2 * n)},
        compiler_params=pltpu.CompilerParams(has_side_effects=DATAFLOW),
    )(*arrays, *lands, send_sems, recv_sems, after)
    return list(res[n:])


def _fill_own(land, own):
    me = 4 * lax.axis_index("x") + 2 * lax.axis_index("y") + lax.axis_index("c")
    return lax.dynamic_update_slice(land, own[None], (me,) + (0,) * own.ndim)


def _adam_math(w, g, m, v):
    m = ADAM_B1 * m + (1.0 - ADAM_B1) * g
    v = ADAM_B2 * v + (1.0 - ADAM_B2) * (g * g)
    m_hat = m / (1.0 - ADAM_B1 ** ADAM_STEP)
    v_hat = v / (1.0 - ADAM_B2 ** ADAM_STEP)
    delta = -ADAM_LR * (m_hat / (jnp.sqrt(v_hat) + ADAM_EPS) + ADAM_WD * w)
    return delta, m, v


def _adamw(recv, w, m, v, name):
    n_parts, r, c = recv.shape
    layers, rows_per_layer, _ = w.shape
    tr = min(rows_per_layer, 128)
    per_layer = rows_per_layer // tr

    def body(g_ref, w_ref, m_ref, v_ref, go_ref, d_ref, mo_ref, vo_ref):
        g = g_ref[0].astype(F32)
        for s in range(1, n_parts):
            g = g + g_ref[s].astype(F32)
        delta, mn, vn = _adam_math(w_ref[...], g, m_ref[...], v_ref[...])
        go_ref[...] = g
        d_ref[...] = delta
        mo_ref[...] = mn
        vo_ref[...] = vn

    blk = pl.BlockSpec((None, tr, c), lambda i: (i // per_layer, i % per_layer, 0))
    return _pcall(
        body, name=name, grid=(r // tr,),
        in_specs=[pl.BlockSpec((n_parts, tr, c), lambda i: (0, i, 0)), blk, blk, blk],
        out_specs=[blk] * 4, out_shape=[SDS(w.shape, F32)] * 4, compiler_params=_params(),
    )(recv, w, m, v)


VEC_ROWS = 32


def _small_allreduce_adamw(vec, w, m, v):
    def body(vec_ref, w_ref, m_ref, v_ref, g_ref, d_ref, mo_ref, vo_ref, gath, send_sems, recv_sems):
        pos = _mesh_pos()
        me = 4 * pos[0] + 2 * pos[1] + pos[2]
        sends, recvs = [], []
        for k in range(1, N_DEV):
            peer, peer_idx = _peer(pos, k)
            cp = pltpu.make_async_remote_copy(src_ref=vec_ref, dst_ref=gath.at[me], send_sem=send_sems.at[k - 1],
                                              recv_sem=recv_sems.at[k - 1], device_id=peer, device_id_type=MESH)
            cp.start()
            sends.append(cp)
            recvs.append(pltpu.make_async_remote_copy(src_ref=vec_ref, dst_ref=gath.at[peer_idx],
                                                      send_sem=send_sems.at[k - 1], recv_sem=recv_sems.at[k - 1],
                                                      device_id=peer, device_id_type=MESH))
        gath[me] = vec_ref[...]
        for cp in recvs:
            cp.wait_recv()
        for cp in sends:
            cp.wait_send()
        tot = gath[0]
        for s in range(1, N_DEV):
            tot = tot + gath[s]
        rowi = lax.broadcasted_iota(jnp.int32, (8, LANES), 0)
        mine = jnp.sum(jnp.where(rowi == me, tot[16:24, :], 0.0), axis=0, keepdims=True)
        g = jnp.concatenate([tot[0:16, :], jnp.broadcast_to(mine, (8, LANES)), tot[24:32, :]], axis=0)
        delta, mn, vn = _adam_math(w_ref[...], g, m_ref[...], v_ref[...])
        g_ref[...] = g
        d_ref[...] = delta
        mo_ref[...] = mn
        vo_ref[...] = vn

    vm = pl.BlockSpec(memory_space=pltpu.VMEM)
    return _pcall(
        body, name="small_allreduce_adamw", in_specs=[vm] * 4, out_specs=[vm] * 4,
        out_shape=[SDS((VEC_ROWS, LANES), F32)] * 4,
        scratch_shapes=[pltpu.VMEM((N_DEV, VEC_ROWS, LANES), F32), pltpu.SemaphoreType.DMA((N_DEV - 1,)),
                        pltpu.SemaphoreType.DMA((N_DEV - 1,))],
        compiler_params=pltpu.CompilerParams(has_side_effects=True),
    )(vec, w, m, v)


def _cols_to_slabs(a):
    r, c8 = a.shape
    return a.reshape(r, N_DEV, c8 // N_DEV).transpose(1, 0, 2)


def _slabs_to_cols(a):
    n, r, c = a.shape
    return a.transpose(1, 0, 2).reshape(r, n * c)


def _rows8(vec):
    return vec.reshape(-1, LANES)


def _pad_rows(a, rows):
    return jnp.pad(a, ((0, rows - a.shape[0]), (0, LANES - a.shape[1])))


def kernel(x, p, fox_norm, fox_w_in, fox_b_f, fox_w_out, dil_norm, dil_w_in, dil_w_out, ple_w_up, ple_w_gate, final_norm, loss_target, m_fox_norm, m_fox_w_in, m_fox_b_f, m_fox_w_out, m_dil_norm, m_dil_w_in, m_dil_w_out, m_ple_w_up, m_ple_w_gate, m_final_norm, v_fox_norm, v_fox_w_in, v_fox_b_f, v_fox_w_out, v_dil_norm, v_dil_w_in, v_dil_w_out, v_ple_w_up, v_ple_w_gate, v_final_norm):
    t = x.shape[1]
    d = D_MODEL
    xs, tgt = x[0], loss_target[0]
    p0, p1 = p[0, 0], p[1, 0]
    fox_cols = fox_w_in.shape[2]
    ple_dim = ple_w_up.shape[1]

    later = [dil_w_in[0].astype(BF16), dil_w_out[0].astype(BF16), ple_w_up.reshape(-1, LANES).astype(BF16),
             ple_w_gate.reshape(-1, d).astype(BF16), dil_norm]
    push = _push_start(later, False, "gather_later_start")
    gw = _all_gather([fox_w_in[0].astype(BF16), fox_w_out[0].astype(BF16)])
    w_fox_in = _slabs_to_cols(gw[0])
    w_fox_main = w_fox_in[:, :4 * d]
    w_fox_f = jnp.pad(w_fox_in[:, 4 * d:], ((0, 0), (0, LANES - FOX_HEADS)))
    w_fox_out = gw[1].reshape(d, d)
    b_pad = jnp.pad(fox_b_f, ((0, 0), (0, LANES - FOX_HEADS)))

    n0, r0 = _rms_fwd(xs, fox_norm + push[4][0:1, 0:1], "rms_fox")
    proj0 = _mm([n0], w_fox_main, "nn", "fox_in_proj", tiles=IN_PROJ_TILES)
    projf = _mm([n0], w_fox_f, "nn", "fox_gate_proj", tiles=IN_PROJ_TILES, out_dtype=F32)
    c_all = _fox_gate_fwd(projf, b_pad)
    qaug_fwd = _fox_aug(c_all, c_all, 1.0, 0.0, 0, FOX_AUG, "fox_aug_q_fwd")
    kaug = _fox_aug(c_all, c_all, -1.0, 0.0, FOX_AUG, 0, "fox_aug_k")
    o0, g0, lse0 = _fox_fwd(proj0, qaug_fwd, kaug)
    h1 = _mm_residual(g0, w_fox_out, "nn", xs, "fox_out_proj")

    landed = _push_wait(push[0], push[1], push[2], push[3], h1, False, "gather_later_wait")
    gl = [_fill_own(zone, own) for zone, own in zip(landed, later)]
    w_dil_in = _slabs_to_cols(gl[0])
    w_dil_out = gl[1].reshape(d, d)
    w_up = gl[2].reshape(N_DEV, 2, ple_dim, LANES).transpose(1, 2, 0, 3).reshape(2, ple_dim, d)
    w_gate = gl[3].reshape(N_DEV, 2, d // N_DEV, d).transpose(1, 0, 2, 3).reshape(2, d, d)
    dil_norm_full = gl[4].reshape(1, d)
    h2, u0, a0 = _ple_fwd(h1, p0, w_up[0], w_gate[0], "ple0_fwd")

    n1, r1 = _rms_fwd(h2, dil_norm_full, "rms_dil")
    proj1 = _mm([n1], w_dil_in, "nn", "dil_in_proj", tiles=IN_PROJ_TILES)
    n_heads = len(DIL_PATTERN) * DIL_HEADS
    slopes = 2.0 ** (-ALIBI_MAX_EXP * jnp.arange(1, n_heads + 1, dtype=F32) / n_heads)
    dil_o, dil_lse, dil_slopes = [], [], []
    for grp, (_, dil) in enumerate(DIL_PATTERN):
        sl = (slopes[grp * DIL_HEADS:(grp + 1) * DIL_HEADS] * dil).reshape(DIL_HEADS, 1, 1)
        og, lg = _dil_fwd(proj1, sl, grp, dil, f"dil_attn_fwd_{grp}")
        dil_o.append(og)
        dil_lse.append(lg)
        dil_slopes.append(sl)
    z1_col0 = 9 * d
    o1, g1, lse1 = _dil_mix(dil_o, dil_lse, proj1, z1_col0)
    h3 = _mm_residual(g1, w_dil_out, "nn", h2, "dil_out_proj")
    h4, u1, a1 = _ple_fwd(h3, p1, w_up[1], w_gate[1], "ple1_fwd")

    dh4, d_final_norm, loss_part = _final_bwd(h4, final_norm.reshape(1, d), tgt)

    du1, da1 = _ple_bwd_elem(dh4, u1, a1, "ple1_bwd_elem")
    dw_up1 = _dw(p1, du1, "ple1_dw_up")
    dw_gate1 = _dw(h3, da1, "ple1_dw_gate")
    dh3 = _mm_residual(da1, w_gate[1], "nt", dh4, "ple1_dh")

    dw_dil_out = _dw(g1, dh3, "dil_dw_out")
    do1, dz1, delta1 = _mm_gate_bwd(dh3, w_dil_out, proj1, z1_col0, o1, DIL_HEADS, "dil_dgate")
    n_grp = len(DIL_PATTERN)
    dqkv = [_dil_bwd(proj1, do1, lse1, delta1, dil_slopes[grp], grp, dil, f"dil_attn_bwd_{grp}")
            for grp, (_, dil) in enumerate(DIL_PATTERN)]
    dw_cols = [_dw(n1, dqkv[grp], f"dil_dw_in_{kind}{grp}", sub=kind) for kind in range(3) for grp in range(n_grp)]
    dw_dil_in = jnp.concatenate(dw_cols + [_dw(n1, dz1, "dil_dw_in_z")], axis=1)
    row_slabs = lambda a: a.reshape(N_DEV, a.shape[0] // N_DEV, a.shape[1])
    dil_slabs = [_cols_to_slabs(dw_dil_in), row_slabs(dw_dil_out), _cols_to_slabs(dw_up1), row_slabs(dw_gate1)]
    dil_push = _push_start(dil_slabs, True, "scatter_dil_start")
    group_major = lambda kb: jnp.where(kb < 3 * n_grp, (kb % 3) * n_grp + kb // 3, kb)
    dh2, d_dil_norm = _mm_in_bwd(dqkv + [dz1], w_dil_in, h2, dil_norm_full + dil_push[4][0:1, 0:1], r1, dh3, "dil_dx",
                                 w_kmap=group_major)

    du0, da0 = _ple_bwd_elem(dh2, u0, a0, "ple0_bwd_elem")
    dw_up0 = _dw(p0, du0, "ple0_dw_up")
    dw_gate0 = _dw(h1, da0, "ple0_dw_gate")
    dh1 = _mm_residual(da0, w_gate[0], "nt", dh2, "ple0_dh")

    dw_fox_out = _dw(g0, dh1, "fox_dw_out")
    do0, dz0, delta0 = _mm_gate_bwd(dh1, w_fox_out, proj0, 3 * d, o0, FOX_HEADS, "fox_dgate")
    head_cols = lambda a: jnp.pad(a, ((0, 0), (0, LANES - FOX_HEADS)))
    lse_cols = head_cols(lse0.reshape(FOX_HEADS, t).T)
    qaug_bwd = _fox_aug(c_all, lse_cols, 1.0, -1.0, 0, FOX_AUG, "fox_aug_q_bwd")
    doaug = _fox_aug(delta0, delta0, -1.0, 0.0, 0, None, "fox_aug_do")
    dq0, dk0, dv0, dck_wide, dcq = _fox_bwd(proj0, do0, qaug_bwd, kaug, doaug)
    dc_query = head_cols(dcq[:, :, 0, :].reshape(FOX_HEADS, t).T)
    df, d_b_f = _fox_gate_bwd(projf, b_pad, dc_query, dck_wide)
    dproj0 = [dq0, dk0, dv0, dz0]
    dw_fox_parts = [_dw(n0, dpart, f"fox_dw_in_{s}") for s, dpart in enumerate(dproj0)]
    dw_fox_f = _dw(n0, df, "fox_dw_gate")
    dw_fox_in = jnp.concatenate(dw_fox_parts + [dw_fox_f[:, :FOX_HEADS]], axis=1)
    fox_slabs = [_cols_to_slabs(dw_fox_in), row_slabs(dw_fox_out), _cols_to_slabs(dw_up0), row_slabs(dw_gate0)]
    fox_push = _push_start(fox_slabs, True, "scatter_fox_start")
    grad_x, d_fox_norm = _mm_in_bwd(dproj0, w_fox_main, xs, fox_norm + fox_push[4][0:1, 0:1], r0, dh1, "fox_dx",
                                    more=(df, w_fox_f))

    me = 4 * lax.axis_index("x") + 2 * lax.axis_index("y") + lax.axis_index("c")

    def landed(push, slabs, name):
        zones = _push_wait(push[0], push[1], push[2], push[3], grad_x, True, name)
        return [_fill_own(zone, lax.dynamic_index_in_dim(own, me, 0, keepdims=False)) for zone, own in zip(zones, slabs)]

    g_dil_in, g_dil_out, g_up1, g_gate1 = landed(dil_push, dil_slabs, "scatter_dil_wait")
    g_fox_in, g_fox_out, g_up0, g_gate0 = landed(fox_push, fox_slabs, "scatter_fox_wait")
    upd = {"fox_w_in": _adamw(g_fox_in, fox_w_in, m_fox_w_in, v_fox_w_in, "adamw_fox_w_in"),
           "fox_w_out": _adamw(g_fox_out, fox_w_out, m_fox_w_out, v_fox_w_out, "adamw_fox_w_out"),
           "dil_w_in": _adamw(g_dil_in, dil_w_in, m_dil_w_in, v_dil_w_in, "adamw_dil_w_in"),
           "dil_w_out": _adamw(g_dil_out, dil_w_out, m_dil_w_out, v_dil_w_out, "adamw_dil_w_out")}
    for nm, grads, params in (("ple_w_up", (g_up0, g_up1), (ple_w_up, m_ple_w_up, v_ple_w_up)),
                              ("ple_w_gate", (g_gate0, g_gate1), (ple_w_gate, m_ple_w_gate, v_ple_w_gate))):
        layers = [_adamw(g, *[a[l:l + 1] for a in params], f"adamw_{nm}_{l}") for l, g in enumerate(grads)]
        upd[nm] = [jnp.concatenate([layers[0][k], layers[1][k]], axis=0) for k in range(4)]

    loss_row = jnp.where(jnp.arange(LANES) == 0, loss_part, 0.0)
    vec = jnp.concatenate([_rows8(d_fox_norm), _rows8(d_final_norm), _rows8(d_dil_norm), d_b_f, loss_row,
                           jnp.zeros((VEC_ROWS - 26, LANES), F32)], axis=0)

    def small_pack(a_fox_norm, a_final_norm, a_dil_norm, a_b_f):
        return jnp.concatenate([_rows8(a_fox_norm), _rows8(a_final_norm), _pad_rows(a_dil_norm, 8),
                                _pad_rows(a_b_f, 8)], axis=0)

    sg, sd, sm, sv = _small_allreduce_adamw(
        vec, small_pack(fox_norm, final_norm, dil_norm, fox_b_f),
        small_pack(m_fox_norm, m_final_norm, m_dil_norm, m_fox_b_f),
        small_pack(v_fox_norm, v_final_norm, v_dil_norm, v_fox_b_f))

    def small_unpack(a):
        return {"fox_norm": a[0:8].reshape(1, d), "final_norm": a[8:16].reshape(d), "dil_norm": a[16:17],
                "fox_b_f": a[24:25, :FOX_HEADS]}

    loss = sg[25, 0]
    order = ["fox_norm", "fox_w_in", "fox_b_f", "fox_w_out", "dil_norm", "dil_w_in", "dil_w_out", "ple_w_up",
             "ple_w_gate", "final_norm"]
    out = [loss, grad_x[None]]
    for idx, small in enumerate((sg, sd, sm, sv)):
        sp = small_unpack(small)
        out += [sp[nm] if nm in sp else upd[nm][idx] for nm in order]
    return tuple(out)
```

```python
import functools

import jax
import jax.numpy as jnp
from jax import lax
from jax.experimental import pallas as pl
from jax.experimental.pallas import tpu as pltpu

F32 = jnp.float32
BF16 = jnp.bfloat16
SDS = jax.ShapeDtypeStruct

D_MODEL = 1024
N_DEV = 8
LANES = 128
FOX_HEADS = 16
FOX_HEAD_DIM = 64
FOX_PAIRS = FOX_HEADS // 2
DIL_HEADS = 8
DIL_BLOCK = 128
DIL_PATTERN = ((128, 1), (512, 4), (2048, 16))
ALIBI_MAX_EXP = 8.0
RMS_EPS = 1e-6
ADAM_LR, ADAM_B1, ADAM_B2, ADAM_EPS, ADAM_WD, ADAM_STEP = 0.001, 0.9, 0.999, 1e-08, 0.01, 10
VMEM_LIMIT = 48 * 1024 * 1024
NEG_INF = float("-inf")

NN = (((1,), (0,)), ((), ()))
NT = (((1,), (1,)), ((), ()))
TN = (((0,), (0,)), ((), ()))
MESH = pl.DeviceIdType.MESH


def _pcall(body, **kw):
    return pl.pallas_call(body, **kw)


def _params(**kw):
    return pltpu.CompilerParams(vmem_limit_bytes=VMEM_LIMIT, **kw)


def _dot(a, b, dims):
    return lax.dot_general(a, b, dims, preferred_element_type=F32)


def _sigmoid(x):
    return 1.0 / (1.0 + jnp.exp(-x))


def _mm(a_parts, b, mode, name, tiles=(512, 1024, 1024), extras=(), outs=None, epilogue=None, out_dtype=BF16,
        b_kmap=None, b_sub=None):
    na = len(a_parts)
    stack = [a.shape[0] if a.ndim == 3 else 1 for a in a_parts]
    first = [sum(stack[:s]) for s in range(na)]
    if mode == "tn":
        k_part, m = a_parts[0].shape
        n = b.shape[-1]
    else:
        m, k_part = a_parts[0].shape[-2:]
        n = b.shape[1] if mode == "nn" else b.shape[0]
    tm, tn, tk = min(tiles[0], m), min(tiles[1], n), min(tiles[2], k_part)
    kb = k_part // tk
    nk = sum(stack) * kb
    grid = (m // tm, n // tn, nk)
    b_kmap = b_kmap or (lambda k: k)

    in_specs = []
    for s in range(na):
        if mode == "tn":
            in_specs.append(pl.BlockSpec((tk, tm), lambda i, j, k: (k, i)))
            continue

        def rel(k, s=s):
            return jnp.clip(k - first[s] * kb, 0, stack[s] * kb - 1)

        if a_parts[s].ndim == 3:
            in_specs.append(pl.BlockSpec((None, tm, tk), lambda i, j, k, rel=rel: (rel(k) // kb, i, rel(k) % kb)))
        else:
            in_specs.append(pl.BlockSpec((tm, tk), lambda i, j, k, rel=rel: (i, rel(k))))
    if mode == "nt":
        in_specs.append(pl.BlockSpec((tn, tk), lambda i, j, k: (j, b_kmap(k))))
    elif b_sub is not None:
        in_specs.append(pl.BlockSpec((None, tk, tn), lambda i, j, k: (b_sub, k, j)))
    else:
        in_specs.append(pl.BlockSpec((tk, tn), lambda i, j, k: (b_kmap(k), j)))
    for _, blk, imap in extras:
        in_specs.append(pl.BlockSpec(blk, imap))
    if outs is None:
        outs = [(SDS((m, n), out_dtype), (tm, tn), lambda i, j, k: (i, j))]
    out_specs = [pl.BlockSpec(blk, imap) for _, blk, imap in outs]
    ne, no = len(extras), len(outs)
    dims = {"nn": NN, "nt": NT, "tn": TN}[mode]

    def finish(res, e_refs, o_refs, i):
        if epilogue is None:
            o_refs[0][...] = res.astype(o_refs[0].dtype)
        else:
            epilogue(res, e_refs, o_refs, i)

    def body(*refs):
        a_refs = refs[:na]
        b_ref = refs[na]
        e_refs = refs[na + 1:na + 1 + ne]
        o_refs = refs[na + 1 + ne:na + 1 + ne + no]
        i, k = pl.program_id(0), pl.program_id(2)
        if nk == 1:
            finish(_dot(a_refs[0][...].astype(BF16), b_ref[...].astype(BF16), dims), e_refs, o_refs, i)
            return
        acc = refs[-1]

        @pl.when(k == 0)
        def _():
            acc[...] = jnp.zeros_like(acc)

        def step(a_ref):
            acc[...] += _dot(a_ref[...].astype(BF16), b_ref[...].astype(BF16), dims)

        for s in range(na):
            if na == 1:
                step(a_refs[0])
            else:
                in_use = (k >= first[s] * kb) & (k < (first[s] + stack[s]) * kb)
                pl.when(in_use)(functools.partial(step, a_refs[s]))

        @pl.when(k == nk - 1)
        def _():
            finish(acc[...], e_refs, o_refs, i)

    res = _pcall(
        body, name=name, grid=grid, in_specs=in_specs, out_specs=out_specs,
        out_shape=[o[0] for o in outs], scratch_shapes=[] if nk == 1 else [pltpu.VMEM((tm, tn), F32)],
        compiler_params=_params(dimension_semantics=("arbitrary", "arbitrary", "arbitrary")),
    )(*a_parts, b, *[e[0] for e in extras])
    return res[0] if len(res) == 1 else res


IN_PROJ_TILES = (1024, 1024, 1024)
DW_TILES = (1024, 1024, 1024)


def _dw(x, dy, name, sub=None):
    return _mm([x], dy, "tn", name, tiles=DW_TILES, b_sub=sub)


def _add_extra_epilogue(acc, e_refs, o_refs, i):
    o_refs[0][...] = acc + e_refs[0][...]


def _mm_residual(a, b, mode, res, name):
    m = a.shape[0]
    n = b.shape[1] if mode == "nn" else b.shape[0]
    tm, tn = 512, 1024
    return _mm([a], b, mode, name, tiles=(tm, tn, 1024),
               extras=[(res, (tm, tn), lambda i, j, k: (i, j))],
               outs=[(SDS((m, n), F32), (tm, tn), lambda i, j, k: (i, j))],
               epilogue=_add_extra_epilogue)


def _rms_fwd(h, g, name):
    t, d = h.shape
    tm = 512

    def body(h_ref, g_ref, n_ref, r_ref):
        x = h_ref[...]
        r = lax.rsqrt(jnp.mean(x * x, axis=-1, keepdims=True) + RMS_EPS)
        n_ref[...] = ((x * r) * g_ref[...]).astype(BF16)
        r_ref[...] = r

    return _pcall(
        body, name=name, grid=(t // tm,),
        in_specs=[pl.BlockSpec((tm, d), lambda i: (i, 0)), pl.BlockSpec((1, d), lambda i: (0, 0))],
        out_specs=[pl.BlockSpec((tm, d), lambda i: (i, 0)), pl.BlockSpec((tm, 1), lambda i: (i, 0))],
        out_shape=[SDS((t, d), BF16), SDS((t, 1), F32)],
        compiler_params=_params(),
    )(h, g)


def _rms_bwd_rows(dn, x, g, r):
    xhat = x * r
    dxhat = dn * g
    dx = r * (dxhat - xhat * jnp.mean(dxhat * xhat, axis=-1, keepdims=True))
    dg = jnp.sum(dn * xhat, axis=0, keepdims=True)
    return dx, dg


def _mm_in_bwd(d_parts, w, h, g, r, dres, name, more=None, w_kmap=None):
    t = h.shape[0]
    tm = 512
    tk = D_MODEL
    row = lambda i, j, k: (i, 0)
    extras = [(h, (tm, D_MODEL), row), (g, (1, D_MODEL), lambda i, j, k: (0, 0)), (r, (tm, 1), row),
              (dres, (tm, D_MODEL), row)]
    if more is not None:
        extras += [(more[0], (tm, more[0].shape[1]), row), (more[1], more[1].shape, lambda i, j, k: (0, 0))]

    def epilogue(acc, e_refs, o_refs, i):
        dn = acc if more is None else acc + _dot(e_refs[4][...], e_refs[5][...], NT)
        dx, dg = _rms_bwd_rows(dn, e_refs[0][...], e_refs[1][...], e_refs[2][...])
        o_refs[0][...] = e_refs[3][...] + dx

        @pl.when(i == 0)
        def _():
            o_refs[1][...] = dg

        @pl.when(i > 0)
        def _():
            o_refs[1][...] += dg

    return _mm(d_parts, w, "nt", name, tiles=(tm, D_MODEL, tk), extras=extras,
               outs=[(SDS((t, D_MODEL), F32), (tm, D_MODEL), row),
                     (SDS((1, D_MODEL), F32), (1, D_MODEL), lambda i, j, k: (0, 0))],
               epilogue=epilogue, b_kmap=w_kmap)


def _final_bwd(h, g, tgt):
    t, d = h.shape
    tm = 256

    def body(h_ref, g_ref, t_ref, dh_ref, dg_ref, loss_ref):
        i = pl.program_id(0)
        x = h_ref[...]
        gg = g_ref[...]
        r = lax.rsqrt(jnp.mean(x * x, axis=-1, keepdims=True) + RMS_EPS)
        err = (x * r) * gg - t_ref[...]
        part = 0.5 * jnp.sum(jnp.mean(err * err, axis=-1, keepdims=True), axis=0, keepdims=True)
        dx, dg = _rms_bwd_rows(err * (1.0 / d), x, gg, r)
        dh_ref[...] = dx

        @pl.when(i == 0)
        def _():
            dg_ref[...] = dg
            loss_ref[...] = jnp.broadcast_to(part, loss_ref.shape)

        @pl.when(i > 0)
        def _():
            dg_ref[...] += dg
            loss_ref[...] += jnp.broadcast_to(part, loss_ref.shape)

    return _pcall(
        body, name="final_norm_loss", grid=(t // tm,),
        in_specs=[pl.BlockSpec((tm, d), lambda i: (i, 0)), pl.BlockSpec((1, d), lambda i: (0, 0)),
                  pl.BlockSpec((tm, d), lambda i: (i, 0))],
        out_specs=[pl.BlockSpec((tm, d), lambda i: (i, 0)), pl.BlockSpec((1, d), lambda i: (0, 0)),
                   pl.BlockSpec((1, LANES), lambda i: (0, 0))],
        out_shape=[SDS((t, d), F32), SDS((1, d), F32), SDS((1, LANES), F32)],
        compiler_params=_params(),
    )(h, g, tgt)


GATE_ROWS = 256


def _split3(x):
    hi = x.astype(BF16)
    r1 = x - hi.astype(F32)
    mid = r1.astype(BF16)
    lo = (r1 - mid.astype(F32)).astype(BF16)
    return hi, mid, lo


def _tri_sum(x, upper):
    rows = x.shape[0]
    ri = lax.broadcasted_iota(jnp.int32, (rows, rows), 0)
    ci = lax.broadcasted_iota(jnp.int32, (rows, rows), 1)
    tri = jnp.where((ri <= ci) if upper else (ri >= ci), 1.0, 0.0).astype(BF16)
    hi, mid, lo = _split3(x)
    return _dot(tri, hi, NN) + _dot(tri, mid, NN) + _dot(tri, lo, NN)


def _log_sigmoid(x):
    return jnp.minimum(x, 0.0) - jnp.log1p(jnp.exp(-jnp.abs(x)))


def _fox_gate_fwd(projf, bpad):
    t = projf.shape[0]
    tb = GATE_ROWS

    def body(x_ref, b_ref, c_ref, carry):
        i = pl.program_id(0)

        @pl.when(i == 0)
        def _():
            carry[...] = jnp.zeros_like(carry)

        c_ref[...] = _tri_sum(_log_sigmoid(x_ref[...] + b_ref[...]), upper=False) + carry[...]
        carry[...] = c_ref[pl.ds(tb - 1, 1), :]

    return _pcall(
        body, name="fox_gate_fwd", grid=(t // tb,),
        in_specs=[pl.BlockSpec((tb, LANES), lambda i: (i, 0)), pl.BlockSpec((1, LANES), lambda i: (0, 0))],
        out_specs=pl.BlockSpec((tb, LANES), lambda i: (i, 0)),
        out_shape=SDS((t, LANES), F32), scratch_shapes=[pltpu.VMEM((1, LANES), F32)],
        compiler_params=_params(),
    )(projf, bpad)


def _fox_gate_bwd(projf, bpad, dc_query, dc_key_wide):
    t = projf.shape[0]
    tb = GATE_ROWS
    nb = t // tb

    def body(x_ref, b_ref, dcq_ref, dck_ref, df_ref, db_ref, carry, buf):
        i = pl.program_id(0)

        @pl.when(i == 0)
        def _():
            carry[...] = jnp.zeros_like(carry)

        src = lax.broadcasted_iota(jnp.int32, (D_MODEL, LANES), 0)
        head = lax.broadcasted_iota(jnp.int32, (D_MODEL, LANES), 1)
        pick = jnp.where((head < FOX_HEADS) & (src == (head // 2) * LANES + (1 - head % 2) * FOX_HEAD_DIM), 1.0, 0.0)
        pick = pick.astype(BF16)
        dc_key = sum(_dot(piece, pick, NN) for piece in _split3(dck_ref[...]))
        buf[...] = _tri_sum(dcq_ref[...] - dc_key, upper=True) + carry[...]
        carry[...] = buf[pl.ds(0, 1), :]
        df = buf[...] * _sigmoid(-(x_ref[...] + b_ref[...]))
        df_ref[...] = df.astype(BF16)
        part = jnp.sum(df, axis=0, keepdims=True)

        @pl.when(i == 0)
        def _():
            db_ref[...] = part

        @pl.when(i > 0)
        def _():
            db_ref[...] += part

    rev = lambda i: (nb - 1 - i, 0)
    return _pcall(
        body, name="fox_gate_bwd", grid=(nb,),
        in_specs=[pl.BlockSpec((tb, LANES), rev), pl.BlockSpec((1, LANES), lambda i: (0, 0)),
                  pl.BlockSpec((tb, LANES), rev), pl.BlockSpec((tb, D_MODEL), rev)],
        out_specs=[pl.BlockSpec((tb, LANES), rev), pl.BlockSpec((1, LANES), lambda i: (0, 0))],
        out_shape=[SDS((t, LANES), BF16), SDS((1, LANES), F32)],
        scratch_shapes=[pltpu.VMEM((1, LANES), F32), pltpu.VMEM((tb, LANES), F32)],
        compiler_params=_params(),
    )(projf, bpad, dc_query, dc_key_wide)


FOX_TQ = 1024
FOX_TQ_FWD = 1024
FOX_SCALE = FOX_HEAD_DIM ** -0.5


def _low_lanes(shape):
    return lax.broadcasted_iota(jnp.int32, shape, len(shape) - 1) < FOX_HEAD_DIM


FOX_AUG = 3
FOX_CHAIN = 256
FOX_SUM_ROWS = 8


def _top_rows(shape):
    return lax.broadcasted_iota(jnp.int32, shape, 0) < FOX_HEAD_DIM


def _fox_aug(a, b, sign_a, sign_b, piece_entry, ones_entry, name):
    t = a.shape[0]
    tb = 512

    def body(a_ref, b_ref, o_ref):
        x = sign_a * a_ref[...]
        if sign_b != 0.0:
            x = x + sign_b * b_ref[...]
        head = lax.broadcasted_iota(jnp.int32, (LANES, D_MODEL), 0)
        col = lax.broadcasted_iota(jnp.int32, (LANES, D_MODEL), 1)
        base = (head // 2) * LANES + (1 - head % 2) * FOX_HEAD_DIM + piece_entry
        acc = jnp.zeros((tb, D_MODEL), F32)
        for e, piece in enumerate(_split3(x)):
            place = jnp.where((head < FOX_HEADS) & (col == base + e), 1.0, 0.0).astype(BF16)
            acc = acc + _dot(piece, place, NN)
        if ones_entry is not None:
            ent = lax.broadcasted_iota(jnp.int32, (1, D_MODEL), 1) % FOX_HEAD_DIM
            acc = acc + jnp.where((ent >= ones_entry) & (ent < ones_entry + FOX_AUG), 1.0, 0.0)
        o_ref[...] = acc.astype(BF16)

    blk = pl.BlockSpec((tb, LANES), lambda i: (i, 0))
    return _pcall(
        body, name=name, grid=(t // tb,), in_specs=[blk, blk],
        out_specs=pl.BlockSpec((tb, D_MODEL), lambda i: (i, 0)), out_shape=SDS((t, D_MODEL), BF16),
        compiler_params=_params(),
    )(a, b)


def _causal_steps(nq, key_major):
    if key_major:
        pairs = [(i, j) for j in range(nq) for i in range(j, nq)]
    else:
        pairs = [(i, j) for i in range(nq) for j in range(i + 1)]
    return (jnp.asarray([p[0] for p in pairs], jnp.int32), jnp.asarray([p[1] for p in pairs], jnp.int32))


def _pair_operand(low, own, other, hh):
    return jnp.where(low, own, other) if hh == 0 else jnp.where(low, other, own)


def _fox_fwd(proj, qaug, kaug):
    t = proj.shape[0]
    tq = tk = min(FOX_TQ_FWD, t)
    nq = t // tq
    cb = D_MODEL // LANES
    half = min(FOX_CHAIN, tq)

    i_tab, j_tab = _causal_steps(nq, key_major=False)

    def body(i_ref, j_ref, q_ref, k_ref, v_ref, z_ref, qa_ref, ka_ref, o_ref, g_ref, lse_ref, m_s, l_s, acc_s):
        step = pl.program_id(1)
        i, j = i_ref[step], j_ref[step]

        @pl.when(j == 0)
        def _():
            m_s[...] = jnp.full_like(m_s, NEG_INF)
            l_s[...] = jnp.zeros_like(l_s)
            acc_s[...] = jnp.zeros_like(acc_s)

        low = _low_lanes((tq, LANES))
        top = _top_rows((LANES, tq))

        def update(masked):
            qs = q_ref[...] * FOX_SCALE
            qa, k, ka, v = qa_ref[...], k_ref[...], ka_ref[...], v_ref[...]
            if masked:
                causal = (lax.broadcasted_iota(jnp.int32, (tk, tq), 0) <= lax.broadcasted_iota(jnp.int32, (tk, tq), 1))
            one = jnp.ones_like(v)
            chains = [(hh, slice(c * half, (c + 1) * half)) for hh in range(2) for c in range(tq // half)]
            qh = [_pair_operand(low, qs, qa, hh) for hh in range(2)]
            kh = [_pair_operand(low, k, ka, hh) for hh in range(2)]
            vh = [_pair_operand(low, v, one, hh) for hh in range(2)]
            keys = lambda cols: slice(0, cols.stop) if masked else slice(None)
            scores = [_dot(kh[hh][keys(cols), :], qh[hh][cols, :], NT) for hh, cols in chains]
            for (hh, cols), s in zip(chains, scores):
                if masked:
                    s = jnp.where(causal[keys(cols), cols], s, NEG_INF)
                m_prev = m_s[hh, :, cols]
                m_new = jnp.maximum(m_prev, jnp.max(s, axis=0, keepdims=True))
                alpha = jnp.exp(m_prev - m_new)
                pv = _dot(vh[hh][keys(cols), :], jnp.exp(s - m_new).astype(BF16), TN)
                sums = pv[FOX_HEAD_DIM:FOX_HEAD_DIM + FOX_SUM_ROWS, :] if hh == 0 else pv[0:FOX_SUM_ROWS, :]
                l_s[hh, :, cols] = alpha * l_s[hh, :, cols] + sums
                m_s[hh, :, cols] = m_new
                own = top[:, cols] if hh == 0 else jnp.logical_not(top[:, cols])
                acc_s[:, cols] = jnp.where(own, acc_s[:, cols] * alpha + pv, acc_s[:, cols])

        pl.when(j < i)(functools.partial(update, False))
        pl.when(j == i)(functools.partial(update, True))

        @pl.when(j == i)
        def _():
            o = (acc_s[...] / jnp.where(top, l_s[0, 0:1, :], l_s[1, 0:1, :])).T
            z = z_ref[...].astype(F32)
            o_ref[...] = o.astype(BF16)
            g_ref[...] = (o * (z * _sigmoid(z))).astype(BF16)
            for hh in range(2):
                lse_ref[hh] = m_s[hh] + jnp.log(l_s[hh, 0:1, :])

    qblk = lambda col: pl.BlockSpec((tq, LANES), lambda h, s, it, jt: (it[s], col + h))
    kblk = lambda col: pl.BlockSpec((tk, LANES), lambda h, s, it, jt: (jt[s], col + h))
    return _pcall(
        body, name="fox_attn_fwd",
        grid_spec=pltpu.PrefetchScalarGridSpec(
            num_scalar_prefetch=2, grid=(FOX_PAIRS, i_tab.shape[0]),
            in_specs=[qblk(0), kblk(cb), kblk(2 * cb), qblk(3 * cb), qblk(0), kblk(0)],
            out_specs=[qblk(0), qblk(0), pl.BlockSpec((2, 1, tq), lambda h, s, it, jt: (h, 0, it[s]))],
            scratch_shapes=[pltpu.VMEM((2, 1, tq), F32), pltpu.VMEM((2, FOX_SUM_ROWS, tq), F32),
                            pltpu.VMEM((LANES, tq), F32)]),
        out_shape=[SDS((t, D_MODEL), BF16), SDS((t, D_MODEL), BF16), SDS((FOX_HEADS, 1, t), F32)],
        compiler_params=_params(dimension_semantics=("arbitrary", "arbitrary")),
    )(i_tab, j_tab, proj, proj, proj, proj, qaug, kaug)


def _fox_bwd(proj, do, qaug, kaug, doaug):
    t = proj.shape[0]
    tq = tk = min(FOX_TQ, t)
    nq = t // tq
    cb = D_MODEL // LANES
    half = min(FOX_CHAIN, tq)

    i_tab, j_tab = _causal_steps(nq, key_major=True)

    def body(i_ref, j_ref, q_ref, k_ref, v_ref, do_ref, qa_ref, ka_ref, da_ref,
             dq_ref, dk_ref, dv_ref, dck_ref, dcq_ref, dq_acc, dcq_acc, dk_acc, dks_acc, dv_acc):
        step = pl.program_id(1)
        i, j = i_ref[step], j_ref[step]
        low = _low_lanes((tq, LANES))
        top = _top_rows((LANES, tq))

        @pl.when(i == j)
        def _():
            dk_acc[...] = jnp.zeros_like(dk_acc)
            dks_acc[...] = jnp.zeros_like(dks_acc)
            dv_acc[...] = jnp.zeros_like(dv_acc)

        def update(masked):
            qs = q_ref[...] * FOX_SCALE
            k, v, dout = k_ref[...], v_ref[...], do_ref[...]
            qa, ka, da = qa_ref[...], ka_ref[...], da_ref[...]
            lane = lax.broadcasted_iota(jnp.int32, (tk, LANES), 1)
            vone = jnp.where((lane & (FOX_HEAD_DIM - 1)) < FOX_AUG, 1.0, 0.0).astype(v.dtype)
            one = jnp.ones_like(k)
            if masked:
                causal = (lax.broadcasted_iota(jnp.int32, (tk, tq), 0) <= lax.broadcasted_iota(jnp.int32, (tk, tq), 1))
            parts = []
            kh = [_pair_operand(low, k, ka, hh) for hh in range(2)]
            qh = [_pair_operand(low, qs, qa, hh) for hh in range(2)]
            vh = [_pair_operand(low, v, vone, hh) for hh in range(2)]
            doh = [_pair_operand(low, dout, da, hh) for hh in range(2)]
            q1 = [_pair_operand(low, qs, one, hh) for hh in range(2)]
            k1 = [_pair_operand(low, k, one, hh) for hh in range(2)]

            def tile(hh, keys, cols, s, dp):
                if masked:
                    s = jnp.where(causal[keys, cols], s, NEG_INF)
                p = jnp.exp(s)
                pb, dsb = p.astype(BF16), (p * dp).astype(BF16)
                return (_dot(pb, dout[cols, :], NN), _dot(dsb, q1[hh][cols, :], NN), _dot(k1[hh][keys, :], dsb, TN))

            if not masked:
                scores = [_dot(kh[hh], qh[hh], NT) for hh in range(2)]
                dps = [_dot(vh[hh], doh[hh], NT) for hh in range(2)]
                everything = slice(None)
                parts = [tile(hh, everything, everything, scores[hh], dps[hh]) for hh in range(2)]
            else:
                for hh in range(2):
                    dv_h, dk_h, dq_h = jnp.zeros((tk, LANES), F32), jnp.zeros((tk, LANES), F32), []
                    for c in range(tq // half):
                        cols, keys = slice(c * half, (c + 1) * half), slice(0, (c + 1) * half)
                        dv_c, dk_c, dq_c = tile(hh, keys, cols, _dot(kh[hh][keys, :], qh[hh][cols, :], NT),
                                                _dot(vh[hh][keys, :], doh[hh][cols, :], NT))
                        below = ((0, tk - keys.stop), (0, 0))
                        dv_h, dk_h = dv_h + jnp.pad(dv_c, below), dk_h + jnp.pad(dk_c, below)
                        dq_h.append(dq_c)
                    parts.append((dv_h, dk_h, jnp.concatenate(dq_h, axis=1)))
            dv_acc[...] += jnp.where(low, parts[0][0], parts[1][0])
            dk_acc[...] += jnp.where(low, parts[0][1], parts[1][1])
            dks_acc[...] += jnp.where(low, parts[1][1], parts[0][1])
            dq_t = jnp.where(top, parts[0][2], parts[1][2]) * FOX_SCALE
            sum_a = parts[0][2][FOX_HEAD_DIM:FOX_HEAD_DIM + FOX_SUM_ROWS, :]
            sum_b = parts[1][2][0:FOX_SUM_ROWS, :]

            @pl.when(j == 0)
            def _():
                dq_acc[i] = dq_t
                dcq_acc[0, i] = sum_a
                dcq_acc[1, i] = sum_b

            @pl.when(j > 0)
            def _():
                dq_acc[i] += dq_t
                dcq_acc[0, i] += sum_a
                dcq_acc[1, i] += sum_b

        pl.when(i > j)(functools.partial(update, False))
        pl.when(i == j)(functools.partial(update, True))

        @pl.when(i == nq - 1)
        def _():
            dk_ref[...] = dk_acc[...].astype(BF16)
            dv_ref[...] = dv_acc[...].astype(BF16)
            dck_ref[...] = dks_acc[...]

        @pl.when((i == nq - 1) & (j == nq - 1))
        def _():
            for blk in range(nq):
                dq_ref[blk * tq:(blk + 1) * tq, :] = dq_acc[blk].T.astype(BF16)
            dcq_ref[...] = dcq_acc[...]

    qblk = lambda col: pl.BlockSpec((tq, LANES), lambda h, s, it, jt: (it[s], col + h))
    kblk = lambda col: pl.BlockSpec((tk, LANES), lambda h, s, it, jt: (jt[s], col + h))
    return _pcall(
        body, name="fox_attn_bwd",
        grid_spec=pltpu.PrefetchScalarGridSpec(
            num_scalar_prefetch=2, grid=(FOX_PAIRS, i_tab.shape[0]),
            in_specs=[qblk(0), kblk(cb), kblk(2 * cb), qblk(0), qblk(0), kblk(0), qblk(0)],
            out_specs=[pl.BlockSpec((t, LANES), lambda h, s, it, jt: (0, h)), kblk(0), kblk(0), kblk(0),
                       pl.BlockSpec((2, nq, FOX_SUM_ROWS, tq), lambda h, s, it, jt: (h, 0, 0, 0))],
            scratch_shapes=[pltpu.VMEM((nq, LANES, tq), F32), pltpu.VMEM((2, nq, FOX_SUM_ROWS, tq), F32),
                            pltpu.VMEM((tk, LANES), F32), pltpu.VMEM((tk, LANES), F32), pltpu.VMEM((tk, LANES), F32)]),
        out_shape=[SDS((t, D_MODEL), BF16), SDS((t, D_MODEL), BF16), SDS((t, D_MODEL), BF16),
                   SDS((t, D_MODEL), F32), SDS((FOX_HEADS, nq, FOX_SUM_ROWS, tq), F32)],
        compiler_params=_params(dimension_semantics=("arbitrary", "arbitrary")),
    )(i_tab, j_tab, proj, proj, proj, do, qaug, kaug, doaug)


def _mm_gate_bwd(dh, w_out, z_src, z_col0, o, heads, name):
    t = dh.shape[0]
    tm = 512
    row = lambda i, j, k: (i, 0)
    zcb = z_col0 // D_MODEL
    per_block = heads == D_MODEL // LANES

    def epilogue(acc, e_refs, o_refs, i):
        z = e_refs[0][...].astype(F32)
        ov = e_refs[1][...].astype(F32)
        sg = _sigmoid(z)
        dout = acc * (z * sg)
        o_refs[0][...] = dout.astype(BF16)
        o_refs[1][...] = (acc * ov * (sg * (1.0 + z * (1.0 - sg)))).astype(BF16)
        prod = dout * ov
        lane = lax.broadcasted_iota(jnp.int32, (tm, LANES), 1)
        cols = jnp.zeros((tm, LANES), F32)
        for cbk in range(D_MODEL // LANES):
            seg = prod[:, cbk * LANES:(cbk + 1) * LANES]
            tot = jnp.sum(seg, axis=-1, keepdims=True)
            if per_block:
                cols = jnp.where(lane == cbk, tot, cols)
            else:
                lo = jnp.sum(jnp.where(_low_lanes(seg.shape), seg, 0.0), axis=-1, keepdims=True)
                cols = jnp.where(lane == 2 * cbk, lo, jnp.where(lane == 2 * cbk + 1, tot - lo, cols))
        o_refs[2][...] = cols

    delta_out = (SDS((t, LANES), F32), (tm, LANES), row)
    return _mm([dh], w_out, "nt", name, tiles=(tm, D_MODEL, D_MODEL),
               extras=[(z_src, (tm, D_MODEL), lambda i, j, k: (i, zcb)), (o, (tm, D_MODEL), row)],
               outs=[(SDS((t, D_MODEL), BF16), (tm, D_MODEL), row), (SDS((t, D_MODEL), BF16), (tm, D_MODEL), row),
                     delta_out],
               epilogue=epilogue)


def _ple_fwd(h, pin, w_up, w_gate, name):
    t = h.shape[0]
    tm = 512
    pd = pin.shape[1]

    def body(h_ref, p_ref, wu_ref, wg_ref, hn_ref, u_ref, a_ref):
        h = h_ref[...]
        u = _dot(p_ref[...].astype(BF16), wu_ref[...], NN)
        a = _dot(h.astype(BF16), wg_ref[...], NN)
        hn_ref[...] = h + u * _sigmoid(a)
        u_ref[...] = u.astype(BF16)
        a_ref[...] = a.astype(BF16)

    rows = pl.BlockSpec((tm, D_MODEL), lambda i: (i, 0))
    return _pcall(
        body, name=name, grid=(t // tm,),
        in_specs=[rows, pl.BlockSpec((tm, pd), lambda i: (i, 0)),
                  pl.BlockSpec((pd, D_MODEL), lambda i: (0, 0)), pl.BlockSpec((D_MODEL, D_MODEL), lambda i: (0, 0))],
        out_specs=[rows, rows, rows],
        out_shape=[SDS((t, D_MODEL), F32), SDS((t, D_MODEL), BF16), SDS((t, D_MODEL), BF16)],
        compiler_params=_params(),
    )(h, pin, w_up, w_gate)


def _ple_bwd_elem(dh, u, a, name):
    t = dh.shape[0]
    tm = 512

    def body(dh_ref, u_ref, a_ref, du_ref, da_ref):
        g = dh_ref[...]
        s = _sigmoid(a_ref[...].astype(F32))
        du_ref[...] = (g * s).astype(BF16)
        da_ref[...] = (g * u_ref[...].astype(F32) * (s * (1.0 - s))).astype(BF16)

    blk = pl.BlockSpec((tm, D_MODEL), lambda i: (i, 0))
    return _pcall(
        body, name=name, grid=(t // tm,), in_specs=[blk, blk, blk], out_specs=[blk, blk],
        out_shape=[SDS((t, D_MODEL), BF16), SDS((t, D_MODEL), BF16)], compiler_params=_params(),
    )(dh, u, a)


DIL_SCALE = LANES ** -0.5


def _dil_masks():
    ii = lax.broadcasted_iota(jnp.int32, (DIL_BLOCK, DIL_BLOCK), 0)
    jj = lax.broadcasted_iota(jnp.int32, (DIL_BLOCK, DIL_BLOCK), 1)
    return ii, jj


DIL_UNITS = 16
BNT = (((2,), (2,)), ((0,), (0,)))
BNN = (((2,), (1,)), ((0,), (0,)))
BTN = (((1,), (1,)), ((0,), (0,)))


def _dil_units(dil):
    return [(b, r) for b in range(DIL_UNITS // dil) for r in range(dil)]


DIL_MAX_STRIDE = 4


def _pre(dil):
    return max(dil // DIL_MAX_STRIDE, 1)


def _stage_in(src, dst, staging, dil, lead=()):
    n, pre = dst.shape[0], _pre(dil)
    if pre == 1:
        dst[...] = src[lead + (slice(None), slice(None))].astype(F32)
        return
    if src.dtype == F32:
        staging = src
    else:
        staging[...] = src[...].astype(F32)
    for q in range(pre):
        dst[q * (n // pre):(q + 1) * (n // pre), :] = staging[lead + (pl.ds(q, n // pre, stride=pre), slice(None))]


def _stage_out(src, dst, staging, dil, lead=()):
    n, pre = src.shape[0], _pre(dil)
    if pre == 1:
        dst[lead + (slice(None), slice(None))] = src[...].astype(dst.dtype)
        return
    out = dst if dst.dtype == F32 else staging
    for q in range(pre):
        out[(lead if dst.dtype == F32 else ()) + (pl.ds(q, n // pre, stride=pre), slice(None))] = \
            src[q * (n // pre):(q + 1) * (n // pre), :]
    if dst.dtype != F32:
        dst[...] = staging[...].astype(dst.dtype)


def _unit_rows(b, r, dil, n):
    pre = _pre(dil)
    return pl.ds((r % pre) * (n // pre) + (b * DIL_BLOCK * dil + r) // pre, DIL_BLOCK, stride=dil // pre)


def _gather_units(cur, dil, shift=0, edge=None):
    nbk = DIL_UNITS // dil
    parts = []
    for b, r in _dil_units(dil):
        bb = b + shift
        if 0 <= bb < nbk:
            parts.append(cur[_unit_rows(bb, r, dil, cur.shape[0]), :])
        else:
            parts.append(edge[_unit_rows(0, r, dil, edge.shape[0]), :])
    return jnp.stack(parts)


def _scatter_units(dst, val, dil):
    for u, (b, r) in enumerate(_dil_units(dil)):
        dst[_unit_rows(b, r, dil, dst.shape[0]), :] = val[u]


def _dil_bias(slope, prev):
    ii, jj = _dil_masks()
    dist = (DIL_BLOCK + ii - jj) if prev else (ii - jj)
    return (slope * dist.astype(F32))[None], ((jj >= ii) if prev else (jj <= ii))[None]


def _dil_fwd(proj, slopes, grp, dil, name):
    t = proj.shape[0]
    rows = DIL_BLOCK * DIL_UNITS
    edge_rows = DIL_BLOCK * dil
    nbk = DIL_UNITS // dil
    nsb = t // rows
    qc, kc_, vc_ = grp * DIL_HEADS, 3 * DIL_HEADS + grp * DIL_HEADS, 6 * DIL_HEADS + grp * DIL_HEADS

    def body(q_ref, kp_ref, kc_ref, vp_ref, vc_ref, sl_ref, o_ref, lse_ref, qf, kpf, kcf, vpf, vcf, of, lf, staging,
             lnat):
        m, h = pl.program_id(0), pl.program_id(1)
        for src, dst in ((q_ref, qf), (kp_ref, kpf), (kc_ref, kcf), (vp_ref, vpf), (vc_ref, vcf)):
            _stage_in(src, dst, staging, dil)
        slope = sl_ref[0]
        unit = lax.broadcasted_iota(jnp.int32, (DIL_UNITS, 1, 1), 0)
        has_prev = (unit >= dil) | (m > 0)
        q = _gather_units(qf, dil).astype(BF16)
        kc, vc = _gather_units(kcf, dil).astype(BF16), _gather_units(vcf, dil).astype(BF16)
        kp, vp = _gather_units(kcf, dil, -1, kpf).astype(BF16), _gather_units(vcf, dil, -1, vpf).astype(BF16)
        bias_p, ok_p = _dil_bias(slope, True)
        bias_c, ok_c = _dil_bias(slope, False)
        sp = jnp.where(ok_p & has_prev, _dot(q, kp, BNT) * DIL_SCALE - bias_p, NEG_INF)
        sc = jnp.where(ok_c, _dot(q, kc, BNT) * DIL_SCALE - bias_c, NEG_INF)
        mx = jnp.maximum(jnp.max(sp, axis=-1, keepdims=True), jnp.max(sc, axis=-1, keepdims=True))
        pp = jnp.exp(sp - mx)
        pc = jnp.exp(sc - mx)
        l = jnp.sum(pp, axis=-1, keepdims=True) + jnp.sum(pc, axis=-1, keepdims=True)
        o = (_dot(pp.astype(BF16), vp, BNN) + _dot(pc.astype(BF16), vc, BNN)) / l
        _scatter_units(of, o, dil)
        _scatter_units(lf, mx + jnp.log(l), dil)
        _stage_out(of, o_ref, staging, dil)
        _stage_out(lf, lnat, staging, dil)

        @pl.when(h == 0)
        def _():
            lse_ref[...] = jnp.zeros_like(lse_ref)

        lane = lax.broadcasted_iota(jnp.int32, (rows, LANES), 1)
        lse_ref[...] = jnp.where(lane == h, lnat[...], lse_ref[...])

    cur = lambda col: pl.BlockSpec((rows, LANES), lambda m, h: (m, col + h))
    prev = lambda col: pl.BlockSpec((edge_rows, LANES), lambda m, h: (jnp.maximum(m * nbk - 1, 0), col + h))
    return _pcall(
        body, name=name, grid=(nsb, DIL_HEADS),
        in_specs=[cur(qc), prev(kc_), cur(kc_), prev(vc_), cur(vc_), pl.BlockSpec((1, 1, 1), lambda m, h: (h, 0, 0))],
        out_specs=[pl.BlockSpec((rows, LANES), lambda m, h: (m, h)), pl.BlockSpec((rows, LANES), lambda m, h: (m, 0))],
        out_shape=[SDS((t, D_MODEL), BF16), SDS((t, LANES), F32)],
        scratch_shapes=[pltpu.VMEM((rows, LANES), F32), pltpu.VMEM((edge_rows, LANES), F32), pltpu.VMEM((rows, LANES), F32),
                        pltpu.VMEM((edge_rows, LANES), F32), pltpu.VMEM((rows, LANES), F32), pltpu.VMEM((rows, LANES), F32),
                        pltpu.VMEM((rows, 1), F32), pltpu.VMEM((rows, LANES), F32), pltpu.VMEM((rows, 1), F32)],
        compiler_params=_params(dimension_semantics=("arbitrary", "arbitrary")),
    )(proj, proj, proj, proj, proj, slopes)


def _dil_mix(outs, lses, proj, z_col0):
    t = proj.shape[0]
    tm = 512
    zcb = z_col0 // D_MODEL
    ng = len(outs)

    def body(*refs):
        o_refs, l_refs, z_ref = refs[:ng], refs[ng:2 * ng], refs[2 * ng]
        om_ref, g_ref, lse_ref = refs[2 * ng + 1:]
        ls = [r[...] for r in l_refs]
        mx = functools.reduce(jnp.maximum, ls)
        es = [jnp.exp(l - mx) for l in ls]
        tot = functools.reduce(jnp.add, es)
        head = lax.broadcasted_iota(jnp.int32, (LANES, D_MODEL), 0)
        col = lax.broadcasted_iota(jnp.int32, (LANES, D_MODEL), 1)
        spread = jnp.where((head < DIL_HEADS) & (col // LANES == head), 1.0, 0.0).astype(BF16)
        widen = lambda w: sum(_dot(piece, spread, NN) for piece in _split3(w))
        o = functools.reduce(jnp.add, [widen(e / tot) * r[...].astype(F32) for e, r in zip(es, o_refs)])
        z = z_ref[...].astype(F32)
        om_ref[...] = o.astype(BF16)
        g_ref[...] = (o * (z * _sigmoid(z))).astype(BF16)
        lse_ref[...] = mx + jnp.log(tot)

    rows = pl.BlockSpec((tm, D_MODEL), lambda i: (i, 0))
    lanes = pl.BlockSpec((tm, LANES), lambda i: (i, 0))
    return _pcall(
        body, name="dil_mix", grid=(t // tm,),
        in_specs=[rows] * ng + [lanes] * ng + [pl.BlockSpec((tm, D_MODEL), lambda i: (i, zcb))],
        out_specs=[rows, rows, lanes],
        out_shape=[SDS((t, D_MODEL), BF16), SDS((t, D_MODEL), BF16), SDS((t, LANES), F32)],
        compiler_params=_params(),
    )(*outs, *lses, proj)


def _dil_bwd(proj, do, lse, delta, slopes, grp, dil, name):
    t = proj.shape[0]
    rows = DIL_BLOCK * DIL_UNITS
    edge_rows = DIL_BLOCK * dil
    nbk = DIL_UNITS // dil
    nsb = t // rows
    last_edge = t // edge_rows - 1
    qc, kc_, vc_ = grp * DIL_HEADS, 3 * DIL_HEADS + grp * DIL_HEADS, 6 * DIL_HEADS + grp * DIL_HEADS

    def body(q_ref, qn_ref, kp_ref, kc_ref, vp_ref, vc_ref, do_ref, don_ref, l_ref, ln_ref, d_ref, dn_ref, sl_ref,
             dqkv_ref, qf, qnf, kpf, kcf, vpf, vcf, dof, donf, dqf, dkf, dvf, staging, lf, lnf, df, dnf, nat, nat_edge):
        h, m = pl.program_id(0), pl.program_id(1)
        for src, dst in ((q_ref, qf), (qn_ref, qnf), (kp_ref, kpf), (kc_ref, kcf), (vp_ref, vpf), (vc_ref, vcf),
                         (do_ref, dof), (don_ref, donf)):
            _stage_in(src, dst, staging, dil)
        for src, col, dst in ((l_ref, nat, lf), (ln_ref, nat_edge, lnf), (d_ref, nat, df), (dn_ref, nat_edge, dnf)):
            lane = lax.broadcasted_iota(jnp.int32, src.shape, 1)
            col[...] = jnp.sum(jnp.where(lane == h, src[...], 0.0), axis=-1, keepdims=True)
            _stage_in(col, dst, staging, dil)
        slope = sl_ref[0]
        unit = lax.broadcasted_iota(jnp.int32, (DIL_UNITS, 1, 1), 0)
        has_prev = (unit >= dil) | (m > 0)
        has_next = (unit < DIL_UNITS - dil) | (m < nsb - 1)
        b16 = lambda x: x.astype(BF16)
        q, kc, vc, dout = (b16(_gather_units(x, dil)) for x in (qf, kcf, vcf, dof))
        kp, vp = b16(_gather_units(kcf, dil, -1, kpf)), b16(_gather_units(vcf, dil, -1, vpf))
        qn, don = b16(_gather_units(qf, dil, 1, qnf)), b16(_gather_units(dof, dil, 1, donf))
        lrow, drow = _gather_units(lf, dil), _gather_units(df, dil)
        lnrow, dnrow = _gather_units(lf, dil, 1, lnf), _gather_units(df, dil, 1, dnf)
        bias_p, ok_p = _dil_bias(slope, True)
        bias_c, ok_c = _dil_bias(slope, False)
        sp = jnp.where(ok_p & has_prev, _dot(q, kp, BNT) * DIL_SCALE - bias_p, NEG_INF)
        sc = jnp.where(ok_c, _dot(q, kc, BNT) * DIL_SCALE - bias_c, NEG_INF)
        pp = jnp.exp(sp - lrow)
        pc = jnp.exp(sc - lrow)
        dsp = b16(pp * (_dot(dout, vp, BNT) - drow))
        dsc = b16(pc * (_dot(dout, vc, BNT) - drow))
        _scatter_units(dqf, (_dot(dsp, kp, BNN) + _dot(dsc, kc, BNN)) * DIL_SCALE, dil)
        sn = jnp.where(ok_p & has_next, _dot(qn, kc, BNT) * DIL_SCALE - bias_p, NEG_INF)
        pn = jnp.exp(sn - lnrow)
        dsn = b16(pn * (_dot(don, vc, BNT) - dnrow))
        _scatter_units(dkf, (_dot(dsc, q, BTN) + _dot(dsn, qn, BTN)) * DIL_SCALE, dil)
        _scatter_units(dvf, _dot(b16(pc), dout, BTN) + _dot(b16(pn), don, BTN), dil)
        for s, src in enumerate((dqf, dkf, dvf)):
            _stage_out(src, dqkv_ref.at[s], staging, dil)

    prev_i = lambda m: jnp.maximum(m * nbk - 1, 0)
    next_i = lambda m: jnp.minimum((m + 1) * nbk, last_edge)
    cur = lambda col: pl.BlockSpec((rows, LANES), lambda h, m: (m, col + h))
    edge = lambda col, f: pl.BlockSpec((edge_rows, LANES), lambda h, m: (f(m), col + h))
    colcur = pl.BlockSpec((rows, LANES), lambda h, m: (m, 0))
    colnext = pl.BlockSpec((edge_rows, LANES), lambda h, m: (next_i(m), 0))
    out_blk = pl.BlockSpec((3, rows, LANES), lambda h, m: (0, m, h))
    big, small = pltpu.VMEM((rows, LANES), F32), pltpu.VMEM((edge_rows, LANES), F32)
    return _pcall(
        body, name=name, grid=(DIL_HEADS, nsb),
        in_specs=[cur(qc), edge(qc, next_i), edge(kc_, prev_i), cur(kc_), edge(vc_, prev_i), cur(vc_),
                  cur(0), edge(0, next_i), colcur, colnext, colcur, colnext,
                  pl.BlockSpec((1, 1, 1), lambda h, m: (h, 0, 0))],
        out_specs=out_blk, out_shape=SDS((3, t, D_MODEL), BF16),
        scratch_shapes=[big, small, small, big, small, big, big, small, big, big, big, big,
                        pltpu.VMEM((rows, 1), F32), pltpu.VMEM((edge_rows, 1), F32),
                        pltpu.VMEM((rows, 1), F32), pltpu.VMEM((edge_rows, 1), F32),
                        pltpu.VMEM((rows, 1), F32), pltpu.VMEM((edge_rows, 1), F32)],
        compiler_params=_params(),
    )(proj, proj, proj, proj, proj, proj, do, do, lse, lse, delta, delta, slopes)


def _mesh_pos():
    x, y, c = lax.axis_index("x"), lax.axis_index("y"), lax.axis_index("c")
    return x, y, c


def _peer(pos, k):
    x, y, c = pos
    px = 1 - x if k & 4 else x
    py = 1 - y if k & 2 else y
    pc = 1 - c if k & 1 else c
    return (px, py, pc), 4 * px + 2 * py + pc


N_CHIPS = 4
CHIP_FLIPS = ((1, 0), (0, 1), (1, 1))


def _other_chips(x, y):
    return [(1 - x if fx else x, 1 - y if fy else y) for fx, fy in CHIP_FLIPS]


def _all_gather(arrays):
    n = len(arrays)
    per = 2 * N_CHIPS - 1
    hbm = pl.BlockSpec(memory_space=pltpu.HBM)

    def body(*refs):
        ins, outs = refs[:n], refs[n:2 * n]
        send_sems, recv_sems, local_sems = refs[2 * n:]
        x, y, c = _mesh_pos()
        sibling = (x, y, 1 - c)
        chips = _other_chips(x, y)
        block = lambda px, py, pc: 4 * px + 2 * py + pc

        def copy(w, k, src, blk, to):
            return pltpu.make_async_remote_copy(
                src_ref=src, dst_ref=outs[w].at[blk], send_sem=send_sems.at[w * per + k],
                recv_sem=recv_sems.at[w * per + k], device_id=to, device_id_type=MESH)

        local, started = [], []
        for w in range(n):
            cp = pltpu.make_async_copy(ins[w], outs[w].at[block(x, y, c)], local_sems.at[w])
            cp.start()
            local.append(cp)
            started.append(copy(w, 0, ins[w], block(x, y, c), sibling))
            for j, (px, py) in enumerate(chips):
                started.append(copy(w, 1 + j, ins[w], block(x, y, c), (px, py, c)))
        for cp in started:
            cp.start()
        for j, (px, py) in enumerate(chips):
            for w in range(n):
                copy(w, 1 + j, ins[w], block(px, py, c), sibling).wait_recv()
                cp = copy(w, 4 + j, outs[w].at[block(px, py, c)], block(px, py, c), sibling)
                cp.start()
                started.append(cp)
        for w in range(n):
            copy(w, 0, ins[w], block(x, y, 1 - c), sibling).wait_recv()
            for j, (px, py) in enumerate(chips):
                copy(w, 4 + j, ins[w], block(px, py, 1 - c), sibling).wait_recv()
        for cp in started:
            cp.wait_send()
        for cp in local:
            cp.wait()

    return _pcall(
        body, name="all_gather_weights", in_specs=[hbm] * n, out_specs=[hbm] * n,
        out_shape=[SDS((N_DEV,) + a.shape, a.dtype) for a in arrays],
        scratch_shapes=[pltpu.SemaphoreType.DMA((n * per,)), pltpu.SemaphoreType.DMA((n * per,)),
                        pltpu.SemaphoreType.DMA((n,))],
    )(*arrays)


HBM_SPEC = pl.BlockSpec(memory_space=pltpu.HBM)
SEM_SPEC = pl.BlockSpec(memory_space=pltpu.SEMAPHORE)
DATAFLOW = pltpu.SideEffectType.DATAFLOW_SIDE_EFFECTING


def _push_start(arrays, scatter, name):
    n = len(arrays)
    per = N_DEV - 1

    def body(*refs):
        srcs, lands = refs[:n], refs[n:2 * n]
        send_sems, recv_sems, token = refs[2 * n], refs[2 * n + 1], refs[-1]
        pos = _mesh_pos()
        me = 4 * pos[0] + 2 * pos[1] + pos[2]
        for w in range(n):
            for k in range(1, N_DEV):
                peer, peer_idx = _peer(pos, k)
                pltpu.make_async_remote_copy(
                    src_ref=srcs[w].at[peer_idx] if scatter else srcs[w], dst_ref=lands[w].at[me],
                    send_sem=send_sems.at[w * per + k - 1], recv_sem=recv_sems.at[w * per + k - 1],
                    device_id=peer, device_id_type=MESH).start()
        token[...] = jnp.zeros_like(token)

    land_shapes = [a.shape if scatter else (N_DEV,) + a.shape for a in arrays]
    in_hbm = lambda a: pltpu.with_memory_space_constraint(a, pltpu.HBM)
    lands = [in_hbm(lax.empty(s, a.dtype)) for s, a in zip(land_shapes, arrays)]
    sems = pltpu.SemaphoreType.DMA((n * per,))
    res = _pcall(
        body, name=name,
        out_shape=(sems, sems, *[pltpu.HBM(a.shape, a.dtype) for a in arrays],
                   *[pltpu.HBM(s, a.dtype) for s, a in zip(land_shapes, arrays)], SDS((8, LANES), F32)),
        in_specs=[HBM_SPEC] * (2 * n),
        out_specs=(SEM_SPEC, SEM_SPEC, *[HBM_SPEC] * (2 * n), pl.BlockSpec(memory_space=pltpu.VMEM)),
        input_output_aliases={i: 2 + i for i in range(2 * n)},
        compiler_params=pltpu.CompilerParams(has_side_effects=DATAFLOW),
    )(*[in_hbm(a) for a in arrays], *lands)
    return res[0], res[1], list(res[2:2 + n]), list(res[2 + n:2 + 2 * n]), res[-1]


def _push_wait(send_sems, recv_sems, arrays, lands, after, scatter, name):
    n = len(arrays)
    per = N_DEV - 1

    def body(*refs):
        srcs, lands_ = refs[:n], refs[n:2 * n]
        send_sems_, recv_sems_ = refs[2 * n], refs[2 * n + 1]
        pos = _mesh_pos()
        for w in range(n):
            for k in range(1, N_DEV):
                peer, peer_idx = _peer(pos, k)
                cp = pltpu.make_async_remote_copy(
                    src_ref=srcs[w].at[peer_idx] if scatter else srcs[w], dst_ref=lands_[w].at[peer_idx],
                    send_sem=send_sems_.at[w * per + k - 1], recv_sem=recv_sems_.at[w * per + k - 1],
                    device_id=peer, device_id_type=MESH)
                cp.wait_send()
                cp.wait_recv()

    res = _pcall(
        body, name=name,
        out_shape=(*[pltpu.HBM(a.shape, a.dtype) for a in arrays], *[pltpu.HBM(l.shape, l.dtype) for l in lands]),
        in_specs=[HBM_SPEC] * (2 * n) + [SEM_SPEC, SEM_SPEC, pl.BlockSpec(memory_space=pl.ANY)],
        out_specs=[HBM_SPEC] * (2 * n), input_output_aliases={i: i for i in range(2 * n)},
        compiler_params=pltpu.CompilerParams(has_side_effects=DATAFLOW),
    )(*arrays, *lands, send_sems, recv_sems, after)
    return list(res[n:])


def _fill_own(land, own):
    me = 4 * lax.axis_index("x") + 2 * lax.axis_index("y") + lax.axis_index("c")
    return lax.dynamic_update_slice(land, own[None], (me,) + (0,) * own.ndim)


def _adam_math(w, g, m, v):
    m = ADAM_B1 * m + (1.0 - ADAM_B1) * g
    v = ADAM_B2 * v + (1.0 - ADAM_B2) * (g * g)
    m_hat = m / (1.0 - ADAM_B1 ** ADAM_STEP)
    v_hat = v / (1.0 - ADAM_B2 ** ADAM_STEP)
    delta = -ADAM_LR * (m_hat / (jnp.sqrt(v_hat) + ADAM_EPS) + ADAM_WD * w)
    return delta, m, v


def _adamw(recv, w, m, v, name):
    n_parts, r, c = recv.shape
    layers, rows_per_layer, _ = w.shape
    tr = min(rows_per_layer, 128)
    per_layer = rows_per_layer // tr

    def body(g_ref, w_ref, m_ref, v_ref, go_ref, d_ref, mo_ref, vo_ref):
        g = g_ref[0].astype(F32)
        for s in range(1, n_parts):
            g = g + g_ref[s].astype(F32)
        delta, mn, vn = _adam_math(w_ref[...], g, m_ref[...], v_ref[...])
        go_ref[...] = g
        d_ref[...] = delta
        mo_ref[...] = mn
        vo_ref[...] = vn

    blk = pl.BlockSpec((None, tr, c), lambda i: (i // per_layer, i % per_layer, 0))
    return _pcall(
        body, name=name, grid=(r // tr,),
        in_specs=[pl.BlockSpec((n_parts, tr, c), lambda i: (0, i, 0)), blk, blk, blk],
        out_specs=[blk] * 4, out_shape=[SDS(w.shape, F32)] * 4, compiler_params=_params(),
    )(recv, w, m, v)


VEC_ROWS = 32


def _small_allreduce_adamw(vec, w, m, v):
    def body(vec_ref, w_ref, m_ref, v_ref, g_ref, d_ref, mo_ref, vo_ref, gath, send_sems, recv_sems):
        pos = _mesh_pos()
        me = 4 * pos[0] + 2 * pos[1] + pos[2]
        sends, recvs = [], []
        for k in range(1, N_DEV):
            peer, peer_idx = _peer(pos, k)
            cp = pltpu.make_async_remote_copy(src_ref=vec_ref, dst_ref=gath.at[me], send_sem=send_sems.at[k - 1],
                                              recv_sem=recv_sems.at[k - 1], device_id=peer, device_id_type=MESH)
            cp.start()
            sends.append(cp)
            recvs.append(pltpu.make_async_remote_copy(src_ref=vec_ref, dst_ref=gath.at[peer_idx],
                                                      send_sem=send_sems.at[k - 1], recv_sem=recv_sems.at[k - 1],
                                                      device_id=peer, device_id_type=MESH))
        gath[me] = vec_ref[...]
        for cp in recvs:
            cp.wait_recv()
        for cp in sends:
            cp.wait_send()
        tot = gath[0]
        for s in range(1, N_DEV):
            tot = tot + gath[s]
        rowi = lax.broadcasted_iota(jnp.int32, (8, LANES), 0)
        mine = jnp.sum(jnp.where(rowi == me, tot[16:24, :], 0.0), axis=0, keepdims=True)
        g = jnp.concatenate([tot[0:16, :], jnp.broadcast_to(mine, (8, LANES)), tot[24:32, :]], axis=0)
        delta, mn, vn = _adam_math(w_ref[...], g, m_ref[...], v_ref[...])
        g_ref[...] = g
        d_ref[...] = delta
        mo_ref[...] = mn
        vo_ref[...] = vn

    vm = pl.BlockSpec(memory_space=pltpu.VMEM)
    return _pcall(
        body, name="small_allreduce_adamw", in_specs=[vm] * 4, out_specs=[vm] * 4,
        out_shape=[SDS((VEC_ROWS, LANES), F32)] * 4,
        scratch_shapes=[pltpu.VMEM((N_DEV, VEC_ROWS, LANES), F32), pltpu.SemaphoreType.DMA((N_DEV - 1,)),
                        pltpu.SemaphoreType.DMA((N_DEV - 1,))],
        compiler_params=pltpu.CompilerParams(has_side_effects=True),
    )(vec, w, m, v)


def _cols_to_slabs(a):
    r, c8 = a.shape
    return a.reshape(r, N_DEV, c8 // N_DEV).transpose(1, 0, 2)


def _slabs_to_cols(a):
    n, r, c = a.shape
    return a.transpose(1, 0, 2).reshape(r, n * c)


def _rows8(vec):
    return vec.reshape(-1, LANES)


def _pad_rows(a, rows):
    return jnp.pad(a, ((0, rows - a.shape[0]), (0, LANES - a.shape[1])))


def kernel(x, p, fox_norm, fox_w_in, fox_b_f, fox_w_out, dil_norm, dil_w_in, dil_w_out, ple_w_up, ple_w_gate, final_norm, loss_target, m_fox_norm, m_fox_w_in, m_fox_b_f, m_fox_w_out, m_dil_norm, m_dil_w_in, m_dil_w_out, m_ple_w_up, m_ple_w_gate, m_final_norm, v_fox_norm, v_fox_w_in, v_fox_b_f, v_fox_w_out, v_dil_norm, v_dil_w_in, v_dil_w_out, v_ple_w_up, v_ple_w_gate, v_final_norm):
    t = x.shape[1]
    d = D_MODEL
    xs, tgt = x[0], loss_target[0]
    p0, p1 = p[0, 0], p[1, 0]
    fox_cols = fox_w_in.shape[2]
    ple_dim = ple_w_up.shape[1]

    later = [dil_w_in[0].astype(BF16), dil_w_out[0].astype(BF16), ple_w_up.reshape(-1, LANES).astype(BF16),
             ple_w_gate.reshape(-1, d).astype(BF16), dil_norm]
    push = _push_start(later, False, "gather_later_start")
    gw = _all_gather([fox_w_in[0].astype(BF16), fox_w_out[0].astype(BF16)])
    w_fox_in = _slabs_to_cols(gw[0])
    w_fox_main = w_fox_in[:, :4 * d]
    w_fox_f = jnp.pad(w_fox_in[:, 4 * d:], ((0, 0), (0, LANES - FOX_HEADS)))
    w_fox_out = gw[1].reshape(d, d)
    b_pad = jnp.pad(fox_b_f, ((0, 0), (0, LANES - FOX_HEADS)))

    n0, r0 = _rms_fwd(xs, fox_norm + push[4][0:1, 0:1], "rms_fox")
    proj0 = _mm([n0], w_fox_main, "nn", "fox_in_proj", tiles=IN_PROJ_TILES)
    projf = _mm([n0], w_fox_f, "nn", "fox_gate_proj", tiles=IN_PROJ_TILES, out_dtype=F32)
    c_all = _fox_gate_fwd(projf, b_pad)
    qaug_fwd = _fox_aug(c_all, c_all, 1.0, 0.0, 0, FOX_AUG, "fox_aug_q_fwd")
    kaug = _fox_aug(c_all, c_all, -1.0, 0.0, FOX_AUG, 0, "fox_aug_k")
    o0, g0, lse0 = _fox_fwd(proj0, qaug_fwd, kaug)
    h1 = _mm_residual(g0, w_fox_out, "nn", xs, "fox_out_proj")

    landed = _push_wait(push[0], push[1], push[2], push[3], h1, False, "gather_later_wait")
    gl = [_fill_own(zone, own) for zone, own in zip(landed, later)]
    w_dil_in = _slabs_to_cols(gl[0])
    w_dil_out = gl[1].reshape(d, d)
    w_up = gl[2].reshape(N_DEV, 2, ple_dim, LANES).transpose(1, 2, 0, 3).reshape(2, ple_dim, d)
    w_gate = gl[3].reshape(N_DEV, 2, d // N_DEV, d).transpose(1, 0, 2, 3).reshape(2, d, d)
    dil_norm_full = gl[4].reshape(1, d)
    h2, u0, a0 = _ple_fwd(h1, p0, w_up[0], w_gate[0], "ple0_fwd")

    n1, r1 = _rms_fwd(h2, dil_norm_full, "rms_dil")
    proj1 = _mm([n1], w_dil_in, "nn", "dil_in_proj", tiles=IN_PROJ_TILES)
    n_heads = len(DIL_PATTERN) * DIL_HEADS
    slopes = 2.0 ** (-ALIBI_MAX_EXP * jnp.arange(1, n_heads + 1, dtype=F32) / n_heads)
    dil_o, dil_lse, dil_slopes = [], [], []
    for grp, (_, dil) in enumerate(DIL_PATTERN):
        sl = (slopes[grp * DIL_HEADS:(grp + 1) * DIL_HEADS] * dil).reshape(DIL_HEADS, 1, 1)
        og, lg = _dil_fwd(proj1, sl, grp, dil, f"dil_attn_fwd_{grp}")
        dil_o.append(og)
        dil_lse.append(lg)
        dil_slopes.append(sl)
    z1_col0 = 9 * d
    o1, g1, lse1 = _dil_mix(dil_o, dil_lse, proj1, z1_col0)
    h3 = _mm_residual(g1, w_dil_out, "nn", h2, "dil_out_proj")
    h4, u1, a1 = _ple_fwd(h3, p1, w_up[1], w_gate[1], "ple1_fwd")

    dh4, d_final_norm, loss_part = _final_bwd(h4, final_norm.reshape(1, d), tgt)

    du1, da1 = _ple_bwd_elem(dh4, u1, a1, "ple1_bwd_elem")
    dw_up1 = _dw(p1, du1, "ple1_dw_up")
    dw_gate1 = _dw(h3, da1, "ple1_dw_gate")
    dh3 = _mm_residual(da1, w_gate[1], "nt", dh4, "ple1_dh")

    dw_dil_out = _dw(g1, dh3, "dil_dw_out")
    do1, dz1, delta1 = _mm_gate_bwd(dh3, w_dil_out, proj1, z1_col0, o1, DIL_HEADS, "dil_dgate")
    n_grp = len(DIL_PATTERN)
    dqkv = [_dil_bwd(proj1, do1, lse1, delta1, dil_slopes[grp], grp, dil, f"dil_attn_bwd_{grp}")
            for grp, (_, dil) in enumerate(DIL_PATTERN)]
    dw_cols = [_dw(n1, dqkv[grp], f"dil_dw_in_{kind}{grp}", sub=kind) for kind in range(3) for grp in range(n_grp)]
    dw_dil_in = jnp.concatenate(dw_cols + [_dw(n1, dz1, "dil_dw_in_z")], axis=1)
    row_slabs = lambda a: a.reshape(N_DEV, a.shape[0] // N_DEV, a.shape[1])
    dil_slabs = [_cols_to_slabs(dw_dil_in), row_slabs(dw_dil_out), _cols_to_slabs(dw_up1), row_slabs(dw_gate1)]
    dil_push = _push_start(dil_slabs, True, "scatter_dil_start")
    group_major = lambda kb: jnp.where(kb < 3 * n_grp, (kb % 3) * n_grp + kb // 3, kb)
    dh2, d_dil_norm = _mm_in_bwd(dqkv + [dz1], w_dil_in, h2, dil_norm_full + dil_push[4][0:1, 0:1], r1, dh3, "dil_dx",
                                 w_kmap=group_major)

    du0, da0 = _ple_bwd_elem(dh2, u0, a0, "ple0_bwd_elem")
    dw_up0 = _dw(p0, du0, "ple0_dw_up")
    dw_gate0 = _dw(h1, da0, "ple0_dw_gate")
    dh1 = _mm_residual(da0, w_gate[0], "nt", dh2, "ple0_dh")

    dw_fox_out = _dw(g0, dh1, "fox_dw_out")
    do0, dz0, delta0 = _mm_gate_bwd(dh1, w_fox_out, proj0, 3 * d, o0, FOX_HEADS, "fox_dgate")
    head_cols = lambda a: jnp.pad(a, ((0, 0), (0, LANES - FOX_HEADS)))
    lse_cols = head_cols(lse0.reshape(FOX_HEADS, t).T)
    qaug_bwd = _fox_aug(c_all, lse_cols, 1.0, -1.0, 0, FOX_AUG, "fox_aug_q_bwd")
    doaug = _fox_aug(delta0, delta0, -1.0, 0.0, 0, None, "fox_aug_do")
    dq0, dk0, dv0, dck_wide, dcq = _fox_bwd(proj0, do0, qaug_bwd, kaug, doaug)
    dc_query = head_cols(dcq[:, :, 0, :].reshape(FOX_HEADS, t).T)
    df, d_b_f = _fox_gate_bwd(projf, b_pad, dc_query, dck_wide)
    dproj0 = [dq0, dk0, dv0, dz0]
    dw_fox_parts = [_dw(n0, dpart, f"fox_dw_in_{s}") for s, dpart in enumerate(dproj0)]
    dw_fox_f = _dw(n0, df, "fox_dw_gate")
    dw_fox_in = jnp.concatenate(dw_fox_parts + [dw_fox_f[:, :FOX_HEADS]], axis=1)
    fox_slabs = [_cols_to_slabs(dw_fox_in), row_slabs(dw_fox_out), _cols_to_slabs(dw_up0), row_slabs(dw_gate0)]
    fox_push = _push_start(fox_slabs, True, "scatter_fox_start")
    grad_x, d_fox_norm = _mm_in_bwd(dproj0, w_fox_main, xs, fox_norm + fox_push[4][0:1, 0:1], r0, dh1, "fox_dx",
                                    more=(df, w_fox_f))

    me = 4 * lax.axis_index("x") + 2 * lax.axis_index("y") + lax.axis_index("c")

    def landed(push, slabs, name):
        zones = _push_wait(push[0], push[1], push[2], push[3], grad_x, True, name)
        return [_fill_own(zone, lax.dynamic_index_in_dim(own, me, 0, keepdims=False)) for zone, own in zip(zones, slabs)]

    g_dil_in, g_dil_out, g_up1, g_gate1 = landed(dil_push, dil_slabs, "scatter_dil_wait")
    g_fox_in, g_fox_out, g_up0, g_gate0 = landed(fox_push, fox_slabs, "scatter_fox_wait")
    upd = {"fox_w_in": _adamw(g_fox_in, fox_w_in, m_fox_w_in, v_fox_w_in, "adamw_fox_w_in"),
           "fox_w_out": _adamw(g_fox_out, fox_w_out, m_fox_w_out, v_fox_w_out, "adamw_fox_w_out"),
           "dil_w_in": _adamw(g_dil_in, dil_w_in, m_dil_w_in, v_dil_w_in, "adamw_dil_w_in"),
           "dil_w_out": _adamw(g_dil_out, dil_w_out, m_dil_w_out, v_dil_w_out, "adamw_dil_w_out")}
    for nm, grads, params in (("ple_w_up", (g_up0, g_up1), (ple_w_up, m_ple_w_up, v_ple_w_up)),
                              ("ple_w_gate", (g_gate0, g_gate1), (ple_w_gate, m_ple_w_gate, v_ple_w_gate))):
        layers = [_adamw(g, *[a[l:l + 1] for a in params], f"adamw_{nm}_{l}") for l, g in enumerate(grads)]
        upd[nm] = [jnp.concatenate([layers[0][k], layers[1][k]], axis=0) for k in range(4)]

    loss_row = jnp.where(jnp.arange(LANES) == 0, loss_part, 0.0)
    vec = jnp.concatenate([_rows8(d_fox_norm), _rows8(d_final_norm), _rows8(d_dil_norm), d_b_f, loss_row,
                           jnp.zeros((VEC_ROWS - 26, LANES), F32)], axis=0)

    def small_pack(a_fox_norm, a_final_norm, a_dil_norm, a_b_f):
        return jnp.concatenate([_rows8(a_fox_norm), _rows8(a_final_norm), _pad_rows(a_dil_norm, 8),
                                _pad_rows(a_b_f, 8)], axis=0)

    sg, sd, sm, sv = _small_allreduce_adamw(
        vec, small_pack(fox_norm, final_norm, dil_norm, fox_b_f),
        small_pack(m_fox_norm, m_final_norm, m_dil_norm, m_fox_b_f),
        small_pack(v_fox_norm, v_final_norm, v_dil_norm, v_fox_b_f))

    def small_unpack(a):
        return {"fox_norm": a[0:8].reshape(1, d), "final_norm": a[8:16].reshape(d), "dil_norm": a[16:17],
                "fox_b_f": a[24:25, :FOX_HEADS]}

    loss = sg[25, 0]
    order = ["fox_norm", "fox_w_in", "fox_b_f", "fox_w_out", "dil_norm", "dil_w_in", "dil_w_out", "ple_w_up",
             "ple_w_gate", "final_norm"]
    out = [loss, grad_x[None]]
    for idx, small in enumerate((sg, sd, sm, sv)):
        sp = small_unpack(small)
        out += [sp[nm] if nm in sp else upd[nm][idx] for nm in order]
    return tuple(out)
```

```python
import functools

import jax
import jax.numpy as jnp
from jax import lax
from jax.experimental import pallas as pl
from jax.experimental.pallas import tpu as pltpu

F32 = jnp.float32
BF16 = jnp.bfloat16
SDS = jax.ShapeDtypeStruct

D_MODEL = 1024
N_DEV = 8
LANES = 128
FOX_HEADS = 16
FOX_HEAD_DIM = 64
FOX_PAIRS = FOX_HEADS // 2
DIL_HEADS = 8
DIL_BLOCK = 128
DIL_PATTERN = ((128, 1), (512, 4), (2048, 16))
ALIBI_MAX_EXP = 8.0
RMS_EPS = 1e-6
ADAM_LR, ADAM_B1, ADAM_B2, ADAM_EPS, ADAM_WD, ADAM_STEP = 0.001, 0.9, 0.999, 1e-08, 0.01, 10
VMEM_LIMIT = 48 * 1024 * 1024
NEG_INF = float("-inf")

NN = (((1,), (0,)), ((), ()))
NT = (((1,), (1,)), ((), ()))
TN = (((0,), (0,)), ((), ()))
MESH = pl.DeviceIdType.MESH


def _pcall(body, **kw):
    return pl.pallas_call(body, **kw)


def _params(**kw):
    return pltpu.CompilerParams(vmem_limit_bytes=VMEM_LIMIT, **kw)


def _dot(a, b, dims):
    return lax.dot_general(a, b, dims, preferred_element_type=F32)


def _sigmoid(x):
    return 1.0 / (1.0 + jnp.exp(-x))


def _mm(a_parts, b, mode, name, tiles=(512, 1024, 1024), extras=(), outs=None, epilogue=None, out_dtype=BF16,
        b_kmap=None, b_sub=None):
    na = len(a_parts)
    stack = [a.shape[0] if a.ndim == 3 else 1 for a in a_parts]
    first = [sum(stack[:s]) for s in range(na)]
    if mode == "tn":
        k_part, m = a_parts[0].shape
        n = b.shape[-1]
    else:
        m, k_part = a_parts[0].shape[-2:]
        n = b.shape[1] if mode == "nn" else b.shape[0]
    tm, tn, tk = min(tiles[0], m), min(tiles[1], n), min(tiles[2], k_part)
    kb = k_part // tk
    nk = sum(stack) * kb
    grid = (m // tm, n // tn, nk)
    b_kmap = b_kmap or (lambda k: k)

    in_specs = []
    for s in range(na):
        if mode == "tn":
            in_specs.append(pl.BlockSpec((tk, tm), lambda i, j, k: (k, i)))
            continue

        def rel(k, s=s):
            return jnp.clip(k - first[s] * kb, 0, stack[s] * kb - 1)

        if a_parts[s].ndim == 3:
            in_specs.append(pl.BlockSpec((None, tm, tk), lambda i, j, k, rel=rel: (rel(k) // kb, i, rel(k) % kb)))
        else:
            in_specs.append(pl.BlockSpec((tm, tk), lambda i, j, k, rel=rel: (i, rel(k))))
    if mode == "nt":
        in_specs.append(pl.BlockSpec((tn, tk), lambda i, j, k: (j, b_kmap(k))))
    elif b_sub is not None:
        in_specs.append(pl.BlockSpec((None, tk, tn), lambda i, j, k: (b_sub, k, j)))
    else:
        in_specs.append(pl.BlockSpec((tk, tn), lambda i, j, k: (b_kmap(k), j)))
    for _, blk, imap in extras:
        in_specs.append(pl.BlockSpec(blk, imap))
    if outs is None:
        outs = [(SDS((m, n), out_dtype), (tm, tn), lambda i, j, k: (i, j))]
    out_specs = [pl.BlockSpec(blk, imap) for _, blk, imap in outs]
    ne, no = len(extras), len(outs)
    dims = {"nn": NN, "nt": NT, "tn": TN}[mode]

    def finish(res, e_refs, o_refs, i):
        if epilogue is None:
            o_refs[0][...] = res.astype(o_refs[0].dtype)
        else:
            epilogue(res, e_refs, o_refs, i)

    def body(*refs):
        a_refs = refs[:na]
        b_ref = refs[na]
        e_refs = refs[na + 1:na + 1 + ne]
        o_refs = refs[na + 1 + ne:na + 1 + ne + no]
        i, k = pl.program_id(0), pl.program_id(2)
        if nk == 1:
            finish(_dot(a_refs[0][...].astype(BF16), b_ref[...].astype(BF16), dims), e_refs, o_refs, i)
            return
        acc = refs[-1]

        @pl.when(k == 0)
        def _():
            acc[...] = jnp.zeros_like(acc)

        def step(a_ref):
            acc[...] += _dot(a_ref[...].astype(BF16), b_ref[...].astype(BF16), dims)

        for s in range(na):
            if na == 1:
                step(a_refs[0])
            else:
                in_use = (k >= first[s] * kb) & (k < (first[s] + stack[s]) * kb)
                pl.when(in_use)(functools.partial(step, a_refs[s]))

        @pl.when(k == nk - 1)
        def _():
            finish(acc[...], e_refs, o_refs, i)

    res = _pcall(
        body, name=name, grid=grid, in_specs=in_specs, out_specs=out_specs,
        out_shape=[o[0] for o in outs], scratch_shapes=[] if nk == 1 else [pltpu.VMEM((tm, tn), F32)],
        compiler_params=_params(dimension_semantics=("arbitrary", "arbitrary", "arbitrary")),
    )(*a_parts, b, *[e[0] for e in extras])
    return res[0] if len(res) == 1 else res


IN_PROJ_TILES = (1024, 1024, 1024)
DW_TILES = (1024, 1024, 1024)


def _dw(x, dy, name, sub=None):
    return _mm([x], dy, "tn", name, tiles=DW_TILES, b_sub=sub)


def _add_extra_epilogue(acc, e_refs, o_refs, i):
    o_refs[0][...] = acc + e_refs[0][...]


def _mm_residual(a, b, mode, res, name):
    m = a.shape[0]
    n = b.shape[1] if mode == "nn" else b.shape[0]
    tm, tn = 512, 1024
    return _mm([a], b, mode, name, tiles=(tm, tn, 1024),
               extras=[(res, (tm, tn), lambda i, j, k: (i, j))],
               outs=[(SDS((m, n), F32), (tm, tn), lambda i, j, k: (i, j))],
               epilogue=_add_extra_epilogue)


def _rms_fwd(h, g, name):
    t, d = h.shape
    tm = 512

    def body(h_ref, g_ref, n_ref, r_ref):
        x = h_ref[...]
        r = lax.rsqrt(jnp.mean(x * x, axis=-1, keepdims=True) + RMS_EPS)
        n_ref[...] = ((x * r) * g_ref[...]).astype(BF16)
        r_ref[...] = r

    return _pcall(
        body, name=name, grid=(t // tm,),
        in_specs=[pl.BlockSpec((tm, d), lambda i: (i, 0)), pl.BlockSpec((1, d), lambda i: (0, 0))],
        out_specs=[pl.BlockSpec((tm, d), lambda i: (i, 0)), pl.BlockSpec((tm, 1), lambda i: (i, 0))],
        out_shape=[SDS((t, d), BF16), SDS((t, 1), F32)],
        compiler_params=_params(),
    )(h, g)


def _rms_bwd_rows(dn, x, g, r):
    xhat = x * r
    dxhat = dn * g
    dx = r * (dxhat - xhat * jnp.mean(dxhat * xhat, axis=-1, keepdims=True))
    dg = jnp.sum(dn * xhat, axis=0, keepdims=True)
    return dx, dg


def _mm_in_bwd(d_parts, w, h, g, r, dres, name, more=None, w_kmap=None):
    t = h.shape[0]
    tm = 512
    tk = D_MODEL
    row = lambda i, j, k: (i, 0)
    extras = [(h, (tm, D_MODEL), row), (g, (1, D_MODEL), lambda i, j, k: (0, 0)), (r, (tm, 1), row),
              (dres, (tm, D_MODEL), row)]
    if more is not None:
        extras += [(more[0], (tm, more[0].shape[1]), row), (more[1], more[1].shape, lambda i, j, k: (0, 0))]

    def epilogue(acc, e_refs, o_refs, i):
        dn = acc if more is None else acc + _dot(e_refs[4][...], e_refs[5][...], NT)
        dx, dg = _rms_bwd_rows(dn, e_refs[0][...], e_refs[1][...], e_refs[2][...])
        o_refs[0][...] = e_refs[3][...] + dx

        @pl.when(i == 0)
        def _():
            o_refs[1][...] = dg

        @pl.when(i > 0)
        def _():
            o_refs[1][...] += dg

    return _mm(d_parts, w, "nt", name, tiles=(tm, D_MODEL, tk), extras=extras,
               outs=[(SDS((t, D_MODEL), F32), (tm, D_MODEL), row),
                     (SDS((1, D_MODEL), F32), (1, D_MODEL), lambda i, j, k: (0, 0))],
               epilogue=epilogue, b_kmap=w_kmap)


def _final_bwd(h, g, tgt):
    t, d = h.shape
    tm = 256

    def body(h_ref, g_ref, t_ref, dh_ref, dg_ref, loss_ref):
        i = pl.program_id(0)
        x = h_ref[...]
        gg = g_ref[...]
        r = lax.rsqrt(jnp.mean(x * x, axis=-1, keepdims=True) + RMS_EPS)
        err = (x * r) * gg - t_ref[...]
        part = 0.5 * jnp.sum(jnp.mean(err * err, axis=-1, keepdims=True), axis=0, keepdims=True)
        dx, dg = _rms_bwd_rows(err * (1.0 / d), x, gg, r)
        dh_ref[...] = dx

        @pl.when(i == 0)
        def _():
            dg_ref[...] = dg
            loss_ref[...] = jnp.broadcast_to(part, loss_ref.shape)

        @pl.when(i > 0)
        def _():
            dg_ref[...] += dg
            loss_ref[...] += jnp.broadcast_to(part, loss_ref.shape)

    return _pcall(
        body, name="final_norm_loss", grid=(t // tm,),
        in_specs=[pl.BlockSpec((tm, d), lambda i: (i, 0)), pl.BlockSpec((1, d), lambda i: (0, 0)),
                  pl.BlockSpec((tm, d), lambda i: (i, 0))],
        out_specs=[pl.BlockSpec((tm, d), lambda i: (i, 0)), pl.BlockSpec((1, d), lambda i: (0, 0)),
                   pl.BlockSpec((1, LANES), lambda i: (0, 0))],
        out_shape=[SDS((t, d), F32), SDS((1, d), F32), SDS((1, LANES), F32)],
        compiler_params=_params(),
    )(h, g, tgt)


GATE_ROWS = 256


def _split3(x):
    hi = x.astype(BF16)
    r1 = x - hi.astype(F32)
    mid = r1.astype(BF16)
    lo = (r1 - mid.astype(F32)).astype(BF16)
    return hi, mid, lo


def _tri_sum(x, upper):
    rows = x.shape[0]
    ri = lax.broadcasted_iota(jnp.int32, (rows, rows), 0)
    ci = lax.broadcasted_iota(jnp.int32, (rows, rows), 1)
    tri = jnp.where((ri <= ci) if upper else (ri >= ci), 1.0, 0.0).astype(BF16)
    hi, mid, lo = _split3(x)
    return _dot(tri, hi, NN) + _dot(tri, mid, NN) + _dot(tri, lo, NN)


def _log_sigmoid(x):
    return jnp.minimum(x, 0.0) - jnp.log1p(jnp.exp(-jnp.abs(x)))


def _fox_gate_fwd(projf, bpad):
    t = projf.shape[0]
    tb = GATE_ROWS

    def body(x_ref, b_ref, c_ref, carry):
        i = pl.program_id(0)

        @pl.when(i == 0)
        def _():
            carry[...] = jnp.zeros_like(carry)

        c_ref[...] = _tri_sum(_log_sigmoid(x_ref[...] + b_ref[...]), upper=False) + carry[...]
        carry[...] = c_ref[pl.ds(tb - 1, 1), :]

    return _pcall(
        body, name="fox_gate_fwd", grid=(t // tb,),
        in_specs=[pl.BlockSpec((tb, LANES), lambda i: (i, 0)), pl.BlockSpec((1, LANES), lambda i: (0, 0))],
        out_specs=pl.BlockSpec((tb, LANES), lambda i: (i, 0)),
        out_shape=SDS((t, LANES), F32), scratch_shapes=[pltpu.VMEM((1, LANES), F32)],
        compiler_params=_params(),
    )(projf, bpad)


def _fox_gate_bwd(projf, bpad, dc_query, dc_key_wide):
    t = projf.shape[0]
    tb = GATE_ROWS
    nb = t // tb

    def body(x_ref, b_ref, dcq_ref, dck_ref, df_ref, db_ref, carry, buf):
        i = pl.program_id(0)

        @pl.when(i == 0)
        def _():
            carry[...] = jnp.zeros_like(carry)

        src = lax.broadcasted_iota(jnp.int32, (D_MODEL, LANES), 0)
        head = lax.broadcasted_iota(jnp.int32, (D_MODEL, LANES), 1)
        pick = jnp.where((head < FOX_HEADS) & (src == (head // 2) * LANES + (1 - head % 2) * FOX_HEAD_DIM), 1.0, 0.0)
        pick = pick.astype(BF16)
        dc_key = sum(_dot(piece, pick, NN) for piece in _split3(dck_ref[...]))
        buf[...] = _tri_sum(dcq_ref[...] - dc_key, upper=True) + carry[...]
        carry[...] = buf[pl.ds(0, 1), :]
        df = buf[...] * _sigmoid(-(x_ref[...] + b_ref[...]))
        df_ref[...] = df.astype(BF16)
        part = jnp.sum(df, axis=0, keepdims=True)

        @pl.when(i == 0)
        def _():
            db_ref[...] = part

        @pl.when(i > 0)
        def _():
            db_ref[...] += part

    rev = lambda i: (nb - 1 - i, 0)
    return _pcall(
        body, name="fox_gate_bwd", grid=(nb,),
        in_specs=[pl.BlockSpec((tb, LANES), rev), pl.BlockSpec((1, LANES), lambda i: (0, 0)),
                  pl.BlockSpec((tb, LANES), rev), pl.BlockSpec((tb, D_MODEL), rev)],
        out_specs=[pl.BlockSpec((tb, LANES), rev), pl.BlockSpec((1, LANES), lambda i: (0, 0))],
        out_shape=[SDS((t, LANES), BF16), SDS((1, LANES), F32)],
        scratch_shapes=[pltpu.VMEM((1, LANES), F32), pltpu.VMEM((tb, LANES), F32)],
        compiler_params=_params(),
    )(projf, bpad, dc_query, dc_key_wide)


FOX_TQ = 1024
FOX_TQ_FWD = 1024
FOX_SCALE = FOX_HEAD_DIM ** -0.5


def _low_lanes(shape):
    return lax.broadcasted_iota(jnp.int32, shape, len(shape) - 1) < FOX_HEAD_DIM


FOX_AUG = 3
FOX_CHAIN = 256
FOX_SUM_ROWS = 8


def _top_rows(shape):
    return lax.broadcasted_iota(jnp.int32, shape, 0) < FOX_HEAD_DIM


def _fox_aug(a, b, sign_a, sign_b, piece_entry, ones_entry, name):
    t = a.shape[0]
    tb = 512

    def body(a_ref, b_ref, o_ref):
        x = sign_a * a_ref[...]
        if sign_b != 0.0:
            x = x + sign_b * b_ref[...]
        head = lax.broadcasted_iota(jnp.int32, (LANES, D_MODEL), 0)
        col = lax.broadcasted_iota(jnp.int32, (LANES, D_MODEL), 1)
        base = (head // 2) * LANES + (1 - head % 2) * FOX_HEAD_DIM + piece_entry
        acc = jnp.zeros((tb, D_MODEL), F32)
        for e, piece in enumerate(_split3(x)):
            place = jnp.where((head < FOX_HEADS) & (col == base + e), 1.0, 0.0).astype(BF16)
            acc = acc + _dot(piece, place, NN)
        if ones_entry is not None:
            ent = lax.broadcasted_iota(jnp.int32, (1, D_MODEL), 1) % FOX_HEAD_DIM
            acc = acc + jnp.where((ent >= ones_entry) & (ent < ones_entry + FOX_AUG), 1.0, 0.0)
        o_ref[...] = acc.astype(BF16)

    blk = pl.BlockSpec((tb, LANES), lambda i: (i, 0))
    return _pcall(
        body, name=name, grid=(t // tb,), in_specs=[blk, blk],
        out_specs=pl.BlockSpec((tb, D_MODEL), lambda i: (i, 0)), out_shape=SDS((t, D_MODEL), BF16),
        compiler_params=_params(),
    )(a, b)


def _causal_steps(nq, key_major):
    if key_major:
        pairs = [(i, j) for j in range(nq) for i in range(j, nq)]
    else:
        pairs = [(i, j) for i in range(nq) for j in range(i + 1)]
    return (jnp.asarray([p[0] for p in pairs], jnp.int32), jnp.asarray([p[1] for p in pairs], jnp.int32))


def _pair_operand(low, own, other, hh):
    return jnp.where(low, own, other) if hh == 0 else jnp.where(low, other, own)


def _fox_fwd(proj, qaug, kaug):
    t = proj.shape[0]
    tq = tk = min(FOX_TQ_FWD, t)
    nq = t // tq
    cb = D_MODEL // LANES
    half = min(FOX_CHAIN, tq)

    i_tab, j_tab = _causal_steps(nq, key_major=False)

    def body(i_ref, j_ref, q_ref, k_ref, v_ref, z_ref, qa_ref, ka_ref, o_ref, g_ref, lse_ref, m_s, l_s, acc_s):
        step = pl.program_id(1)
        i, j = i_ref[step], j_ref[step]

        @pl.when(j == 0)
        def _():
            m_s[...] = jnp.full_like(m_s, NEG_INF)
            l_s[...] = jnp.zeros_like(l_s)
            acc_s[...] = jnp.zeros_like(acc_s)

        low = _low_lanes((tq, LANES))
        top = _top_rows((LANES, tq))

        def update(masked):
            qs = q_ref[...] * FOX_SCALE
            qa, k, ka, v = qa_ref[...], k_ref[...], ka_ref[...], v_ref[...]
            if masked:
                causal = (lax.broadcasted_iota(jnp.int32, (tk, tq), 0) <= lax.broadcasted_iota(jnp.int32, (tk, tq), 1))
            one = jnp.ones_like(v)
            chains = [(hh, slice(c * half, (c + 1) * half)) for hh in range(2) for c in range(tq // half)]
            qh = [_pair_operand(low, qs, qa, hh) for hh in range(2)]
            kh = [_pair_operand(low, k, ka, hh) for hh in range(2)]
            vh = [_pair_operand(low, v, one, hh) for hh in range(2)]
            keys = lambda cols: slice(0, cols.stop) if masked else slice(None)
            scores = [_dot(kh[hh][keys(cols), :], qh[hh][cols, :], NT) for hh, cols in chains]
            for (hh, cols), s in zip(chains, scores):
                if masked:
                    s = jnp.where(causal[keys(cols), cols], s, NEG_INF)
                m_prev = m_s[hh, :, cols]
                m_new = jnp.maximum(m_prev, jnp.max(s, axis=0, keepdims=True))
                alpha = jnp.exp(m_prev - m_new)
                pv = _dot(vh[hh][keys(cols), :], jnp.exp(s - m_new).astype(BF16), TN)
                sums = pv[FOX_HEAD_DIM:FOX_HEAD_DIM + FOX_SUM_ROWS, :] if hh == 0 else pv[0:FOX_SUM_ROWS, :]
                l_s[hh, :, cols] = alpha * l_s[hh, :, cols] + sums
                m_s[hh, :, cols] = m_new
                own = top[:, cols] if hh == 0 else jnp.logical_not(top[:, cols])
                acc_s[:, cols] = jnp.where(own, acc_s[:, cols] * alpha + pv, acc_s[:, cols])

        pl.when(j < i)(functools.partial(update, False))
        pl.when(j == i)(functools.partial(update, True))

        @pl.when(j == i)
        def _():
            o = (acc_s[...] / jnp.where(top, l_s[0, 0:1, :], l_s[1, 0:1, :])).T
            z = z_ref[...].astype(F32)
            o_ref[...] = o.astype(BF16)
            g_ref[...] = (o * (z * _sigmoid(z))).astype(BF16)
            for hh in range(2):
                lse_ref[hh] = m_s[hh] + jnp.log(l_s[hh, 0:1, :])

    qblk = lambda col: pl.BlockSpec((tq, LANES), lambda h, s, it, jt: (it[s], col + h))
    kblk = lambda col: pl.BlockSpec((tk, LANES), lambda h, s, it, jt: (jt[s], col + h))
    return _pcall(
        body, name="fox_attn_fwd",
        grid_spec=pltpu.PrefetchScalarGridSpec(
            num_scalar_prefetch=2, grid=(FOX_PAIRS, i_tab.shape[0]),
            in_specs=[qblk(0), kblk(cb), kblk(2 * cb), qblk(3 * cb), qblk(0), kblk(0)],
            out_specs=[qblk(0), qblk(0), pl.BlockSpec((2, 1, tq), lambda h, s, it, jt: (h, 0, it[s]))],
            scratch_shapes=[pltpu.VMEM((2, 1, tq), F32), pltpu.VMEM((2, FOX_SUM_ROWS, tq), F32),
                            pltpu.VMEM((LANES, tq), F32)]),
        out_shape=[SDS((t, D_MODEL), BF16), SDS((t, D_MODEL), BF16), SDS((FOX_HEADS, 1, t), F32)],
        compiler_params=_params(dimension_semantics=("arbitrary", "arbitrary")),
    )(i_tab, j_tab, proj, proj, proj, proj, qaug, kaug)


def _fox_bwd(proj, do, qaug, kaug, doaug):
    t = proj.shape[0]
    tq = tk = min(FOX_TQ, t)
    nq = t // tq
    cb = D_MODEL // LANES
    half = min(FOX_CHAIN, tq)

    i_tab, j_tab = _causal_steps(nq, key_major=True)

    def body(i_ref, j_ref, q_ref, k_ref, v_ref, do_ref, qa_ref, ka_ref, da_ref,
             dq_ref, dk_ref, dv_ref, dck_ref, dcq_ref, dq_acc, dcq_acc, dk_acc, dks_acc, dv_acc):
        step = pl.program_id(1)
        i, j = i_ref[step], j_ref[step]
        low = _low_lanes((tq, LANES))
        top = _top_rows((LANES, tq))

        @pl.when(i == j)
        def _():
            dk_acc[...] = jnp.zeros_like(dk_acc)
            dks_acc[...] = jnp.zeros_like(dks_acc)
            dv_acc[...] = jnp.zeros_like(dv_acc)

        def update(masked):
            qs = q_ref[...] * FOX_SCALE
            k, v, dout = k_ref[...], v_ref[...], do_ref[...]
            qa, ka, da = qa_ref[...], ka_ref[...], da_ref[...]
            lane = lax.broadcasted_iota(jnp.int32, (tk, LANES), 1)
            vone = jnp.where((lane & (FOX_HEAD_DIM - 1)) < FOX_AUG, 1.0, 0.0).astype(v.dtype)
            one = jnp.ones_like(k)
            if masked:
                causal = (lax.broadcasted_iota(jnp.int32, (tk, tq), 0) <= lax.broadcasted_iota(jnp.int32, (tk, tq), 1))
            parts = []
            kh = [_pair_operand(low, k, ka, hh) for hh in range(2)]
            qh = [_pair_operand(low, qs, qa, hh) for hh in range(2)]
            vh = [_pair_operand(low, v, vone, hh) for hh in range(2)]
            doh = [_pair_operand(low, dout, da, hh) for hh in range(2)]
            q1 = [_pair_operand(low, qs, one, hh) for hh in range(2)]
            k1 = [_pair_operand(low, k, one, hh) for hh in range(2)]

            def tile(hh, keys, cols, s, dp):
                if masked:
                    s = jnp.where(causal[keys, cols], s, NEG_INF)
                p = jnp.exp(s)
                pb, dsb = p.astype(BF16), (p * dp).astype(BF16)
                return (_dot(pb, dout[cols, :], NN), _dot(dsb, q1[hh][cols, :], NN), _dot(k1[hh][keys, :], dsb, TN))

            if not masked:
                scores = [_dot(kh[hh], qh[hh], NT) for hh in range(2)]
                dps = [_dot(vh[hh], doh[hh], NT) for hh in range(2)]
                everything = slice(None)
                parts = [tile(hh, everything, everything, scores[hh], dps[hh]) for hh in range(2)]
            else:
                for hh in range(2):
                    dv_h, dk_h, dq_h = jnp.zeros((tk, LANES), F32), jnp.zeros((tk, LANES), F32), []
                    for c in range(tq // half):
                        cols, keys = slice(c * half, (c + 1) * half), slice(0, (c + 1) * half)
                        dv_c, dk_c, dq_c = tile(hh, keys, cols, _dot(kh[hh][keys, :], qh[hh][cols, :], NT),
                                                _dot(vh[hh][keys, :], doh[hh][cols, :], NT))
                        below = ((0, tk - keys.stop), (0, 0))
                        dv_h, dk_h = dv_h + jnp.pad(dv_c, below), dk_h + jnp.pad(dk_c, below)
                        dq_h.append(dq_c)
                    parts.append((dv_h, dk_h, jnp.concatenate(dq_h, axis=1)))
            dv_acc[...] += jnp.where(low, parts[0][0], parts[1][0])
            dk_acc[...] += jnp.where(low, parts[0][1], parts[1][1])
            dks_acc[...] += jnp.where(low, parts[1][1], parts[0][1])
            dq_t = jnp.where(top, parts[0][2], parts[1][2]) * FOX_SCALE
            sum_a = parts[0][2][FOX_HEAD_DIM:FOX_HEAD_DIM + FOX_SUM_ROWS, :]
            sum_b = parts[1][2][0:FOX_SUM_ROWS, :]

            @pl.when(j == 0)
            def _():
                dq_acc[i] = dq_t
                dcq_acc[0, i] = sum_a
                dcq_acc[1, i] = sum_b

            @pl.when(j > 0)
            def _():
                dq_acc[i] += dq_t
                dcq_acc[0, i] += sum_a
                dcq_acc[1, i] += sum_b

        pl.when(i > j)(functools.partial(update, False))
        pl.when(i == j)(functools.partial(update, True))

        @pl.when(i == nq - 1)
        def _():
            dk_ref[...] = dk_acc[...].astype(BF16)
            dv_ref[...] = dv_acc[...].astype(BF16)
            dck_ref[...] = dks_acc[...]

        @pl.when((i == nq - 1) & (j == nq - 1))
        def _():
            for blk in range(nq):
                dq_ref[blk * tq:(blk + 1) * tq, :] = dq_acc[blk].T.astype(BF16)
            dcq_ref[...] = dcq_acc[...]

    qblk = lambda col: pl.BlockSpec((tq, LANES), lambda h, s, it, jt: (it[s], col + h))
    kblk = lambda col: pl.BlockSpec((tk, LANES), lambda h, s, it, jt: (jt[s], col + h))
    return _pcall(
        body, name="fox_attn_bwd",
        grid_spec=pltpu.PrefetchScalarGridSpec(
            num_scalar_prefetch=2, grid=(FOX_PAIRS, i_tab.shape[0]),
            in_specs=[qblk(0), kblk(cb), kblk(2 * cb), qblk(0), qblk(0), kblk(0), qblk(0)],
            out_specs=[pl.BlockSpec((t, LANES), lambda h, s, it, jt: (0, h)), kblk(0), kblk(0), kblk(0),
                       pl.BlockSpec((2, nq, FOX_SUM_ROWS, tq), lambda h, s, it, jt: (h, 0, 0, 0))],
            scratch_shapes=[pltpu.VMEM((nq, LANES, tq), F32), pltpu.VMEM((2, nq, FOX_SUM_ROWS, tq), F32),
                            pltpu.VMEM((tk, LANES), F32), pltpu.VMEM((tk, LANES), F32), pltpu.VMEM((tk, LANES), F32)]),
        out_shape=[SDS((t, D_MODEL), BF16), SDS((t, D_MODEL), BF16), SDS((t, D_MODEL), BF16),
                   SDS((t, D_MODEL), F32), SDS((FOX_HEADS, nq, FOX_SUM_ROWS, tq), F32)],
        compiler_params=_params(dimension_semantics=("arbitrary", "arbitrary")),
    )(i_tab, j_tab, proj, proj, proj, do, qaug, kaug, doaug)


def _mm_gate_bwd(dh, w_out, z_src, z_col0, o, heads, name):
    t = dh.shape[0]
    tm = 512
    row = lambda i, j, k: (i, 0)
    zcb = z_col0 // D_MODEL
    per_block = heads == D_MODEL // LANES

    def epilogue(acc, e_refs, o_refs, i):
        z = e_refs[0][...].astype(F32)
        ov = e_refs[1][...].astype(F32)
        sg = _sigmoid(z)
        dout = acc * (z * sg)
        o_refs[0][...] = dout.astype(BF16)
        o_refs[1][...] = (acc * ov * (sg * (1.0 + z * (1.0 - sg)))).astype(BF16)
        prod = dout * ov
        lane = lax.broadcasted_iota(jnp.int32, (tm, LANES), 1)
        cols = jnp.zeros((tm, LANES), F32)
        for cbk in range(D_MODEL // LANES):
            seg = prod[:, cbk * LANES:(cbk + 1) * LANES]
            tot = jnp.sum(seg, axis=-1, keepdims=True)
            if per_block:
                o_refs[2][cbk] = tot
            else:
                lo = jnp.sum(jnp.where(_low_lanes(seg.shape), seg, 0.0), axis=-1, keepdims=True)
                cols = jnp.where(lane == 2 * cbk, lo, jnp.where(lane == 2 * cbk + 1, tot - lo, cols))
        if not per_block:
            o_refs[2][...] = cols

    delta_out = ((SDS((heads, t, 1), F32), (heads, tm, 1), lambda i, j, k: (0, i, 0)) if per_block
                 else (SDS((t, LANES), F32), (tm, LANES), row))
    return _mm([dh], w_out, "nt", name, tiles=(tm, D_MODEL, D_MODEL),
               extras=[(z_src, (tm, D_MODEL), lambda i, j, k: (i, zcb)), (o, (tm, D_MODEL), row)],
               outs=[(SDS((t, D_MODEL), BF16), (tm, D_MODEL), row), (SDS((t, D_MODEL), BF16), (tm, D_MODEL), row),
                     delta_out],
               epilogue=epilogue)


def _ple_fwd(h, pin, w_up, w_gate, name):
    t = h.shape[0]
    tm = 512
    pd = pin.shape[1]

    def body(h_ref, p_ref, wu_ref, wg_ref, hn_ref, u_ref, a_ref):
        h = h_ref[...]
        u = _dot(p_ref[...].astype(BF16), wu_ref[...], NN)
        a = _dot(h.astype(BF16), wg_ref[...], NN)
        hn_ref[...] = h + u * _sigmoid(a)
        u_ref[...] = u.astype(BF16)
        a_ref[...] = a.astype(BF16)

    rows = pl.BlockSpec((tm, D_MODEL), lambda i: (i, 0))
    return _pcall(
        body, name=name, grid=(t // tm,),
        in_specs=[rows, pl.BlockSpec((tm, pd), lambda i: (i, 0)),
                  pl.BlockSpec((pd, D_MODEL), lambda i: (0, 0)), pl.BlockSpec((D_MODEL, D_MODEL), lambda i: (0, 0))],
        out_specs=[rows, rows, rows],
        out_shape=[SDS((t, D_MODEL), F32), SDS((t, D_MODEL), BF16), SDS((t, D_MODEL), BF16)],
        compiler_params=_params(),
    )(h, pin, w_up, w_gate)


def _ple_bwd_elem(dh, u, a, name):
    t = dh.shape[0]
    tm = 512

    def body(dh_ref, u_ref, a_ref, du_ref, da_ref):
        g = dh_ref[...]
        s = _sigmoid(a_ref[...].astype(F32))
        du_ref[...] = (g * s).astype(BF16)
        da_ref[...] = (g * u_ref[...].astype(F32) * (s * (1.0 - s))).astype(BF16)

    blk = pl.BlockSpec((tm, D_MODEL), lambda i: (i, 0))
    return _pcall(
        body, name=name, grid=(t // tm,), in_specs=[blk, blk, blk], out_specs=[blk, blk],
        out_shape=[SDS((t, D_MODEL), BF16), SDS((t, D_MODEL), BF16)], compiler_params=_params(),
    )(dh, u, a)


DIL_SCALE = LANES ** -0.5


def _dil_masks():
    ii = lax.broadcasted_iota(jnp.int32, (DIL_BLOCK, DIL_BLOCK), 0)
    jj = lax.broadcasted_iota(jnp.int32, (DIL_BLOCK, DIL_BLOCK), 1)
    return ii, jj


DIL_UNITS = 16
BNT = (((2,), (2,)), ((0,), (0,)))
BNN = (((2,), (1,)), ((0,), (0,)))
BTN = (((1,), (1,)), ((0,), (0,)))


def _dil_units(dil):
    return [(b, r) for b in range(DIL_UNITS // dil) for r in range(dil)]


DIL_MAX_STRIDE = 4


def _pre(dil):
    return max(dil // DIL_MAX_STRIDE, 1)


def _stage_in(src, dst, staging, dil, lead=()):
    n, pre = dst.shape[0], _pre(dil)
    if pre == 1:
        dst[...] = src[lead + (slice(None), slice(None))].astype(F32)
        return
    if src.dtype == F32:
        staging = src
    else:
        staging[...] = src[...].astype(F32)
    for q in range(pre):
        dst[q * (n // pre):(q + 1) * (n // pre), :] = staging[lead + (pl.ds(q, n // pre, stride=pre), slice(None))]


def _stage_out(src, dst, staging, dil, lead=()):
    n, pre = src.shape[0], _pre(dil)
    if pre == 1:
        dst[lead + (slice(None), slice(None))] = src[...].astype(dst.dtype)
        return
    out = dst if dst.dtype == F32 else staging
    for q in range(pre):
        out[(lead if dst.dtype == F32 else ()) + (pl.ds(q, n // pre, stride=pre), slice(None))] = \
            src[q * (n // pre):(q + 1) * (n // pre), :]
    if dst.dtype != F32:
        dst[...] = staging[...].astype(dst.dtype)


def _unit_rows(b, r, dil, n):
    pre = _pre(dil)
    return pl.ds((r % pre) * (n // pre) + (b * DIL_BLOCK * dil + r) // pre, DIL_BLOCK, stride=dil // pre)


def _gather_units(cur, dil, shift=0, edge=None):
    nbk = DIL_UNITS // dil
    parts = []
    for b, r in _dil_units(dil):
        bb = b + shift
        if 0 <= bb < nbk:
            parts.append(cur[_unit_rows(bb, r, dil, cur.shape[0]), :])
        else:
            parts.append(edge[_unit_rows(0, r, dil, edge.shape[0]), :])
    return jnp.stack(parts)


def _scatter_units(dst, val, dil):
    for u, (b, r) in enumerate(_dil_units(dil)):
        dst[_unit_rows(b, r, dil, dst.shape[0]), :] = val[u]


def _dil_bias(slope, prev):
    ii, jj = _dil_masks()
    dist = (DIL_BLOCK + ii - jj) if prev else (ii - jj)
    return (slope * dist.astype(F32))[None], ((jj >= ii) if prev else (jj <= ii))[None]


def _dil_fwd(proj, slopes, grp, dil, name):
    t = proj.shape[0]
    rows = DIL_BLOCK * DIL_UNITS
    edge_rows = DIL_BLOCK * dil
    nbk = DIL_UNITS // dil
    nsb = t // rows
    qc, kc_, vc_ = grp * DIL_HEADS, 3 * DIL_HEADS + grp * DIL_HEADS, 6 * DIL_HEADS + grp * DIL_HEADS

    def body(q_ref, kp_ref, kc_ref, vp_ref, vc_ref, sl_ref, o_ref, lse_ref, qf, kpf, kcf, vpf, vcf, of, lf, staging,
             lnat):
        m, h = pl.program_id(0), pl.program_id(1)
        for src, dst in ((q_ref, qf), (kp_ref, kpf), (kc_ref, kcf), (vp_ref, vpf), (vc_ref, vcf)):
            _stage_in(src, dst, staging, dil)
        slope = sl_ref[0]
        unit = lax.broadcasted_iota(jnp.int32, (DIL_UNITS, 1, 1), 0)
        has_prev = (unit >= dil) | (m > 0)
        q = _gather_units(qf, dil).astype(BF16)
        kc, vc = _gather_units(kcf, dil).astype(BF16), _gather_units(vcf, dil).astype(BF16)
        kp, vp = _gather_units(kcf, dil, -1, kpf).astype(BF16), _gather_units(vcf, dil, -1, vpf).astype(BF16)
        bias_p, ok_p = _dil_bias(slope, True)
        bias_c, ok_c = _dil_bias(slope, False)
        sp = jnp.where(ok_p & has_prev, _dot(q, kp, BNT) * DIL_SCALE - bias_p, NEG_INF)
        sc = jnp.where(ok_c, _dot(q, kc, BNT) * DIL_SCALE - bias_c, NEG_INF)
        mx = jnp.maximum(jnp.max(sp, axis=-1, keepdims=True), jnp.max(sc, axis=-1, keepdims=True))
        pp = jnp.exp(sp - mx)
        pc = jnp.exp(sc - mx)
        l = jnp.sum(pp, axis=-1, keepdims=True) + jnp.sum(pc, axis=-1, keepdims=True)
        o = (_dot(pp.astype(BF16), vp, BNN) + _dot(pc.astype(BF16), vc, BNN)) / l
        _scatter_units(of, o, dil)
        _scatter_units(lf, mx + jnp.log(l), dil)
        _stage_out(of, o_ref, staging, dil)
        _stage_out(lf, lnat, staging, dil)

        @pl.when(h == 0)
        def _():
            lse_ref[...] = jnp.zeros_like(lse_ref)

        lane = lax.broadcasted_iota(jnp.int32, (rows, LANES), 1)
        lse_ref[...] = jnp.where(lane == h, lnat[...], lse_ref[...])

    cur = lambda col: pl.BlockSpec((rows, LANES), lambda m, h: (m, col + h))
    prev = lambda col: pl.BlockSpec((edge_rows, LANES), lambda m, h: (jnp.maximum(m * nbk - 1, 0), col + h))
    return _pcall(
        body, name=name, grid=(nsb, DIL_HEADS),
        in_specs=[cur(qc), prev(kc_), cur(kc_), prev(vc_), cur(vc_), pl.BlockSpec((1, 1, 1), lambda m, h: (h, 0, 0))],
        out_specs=[pl.BlockSpec((rows, LANES), lambda m, h: (m, h)), pl.BlockSpec((rows, LANES), lambda m, h: (m, 0))],
        out_shape=[SDS((t, D_MODEL), BF16), SDS((t, LANES), F32)],
        scratch_shapes=[pltpu.VMEM((rows, LANES), F32), pltpu.VMEM((edge_rows, LANES), F32), pltpu.VMEM((rows, LANES), F32),
                        pltpu.VMEM((edge_rows, LANES), F32), pltpu.VMEM((rows, LANES), F32), pltpu.VMEM((rows, LANES), F32),
                        pltpu.VMEM((rows, 1), F32), pltpu.VMEM((rows, LANES), F32), pltpu.VMEM((rows, 1), F32)],
        compiler_params=_params(dimension_semantics=("arbitrary", "arbitrary")),
    )(proj, proj, proj, proj, proj, slopes)


def _dil_mix(outs, lses, proj, z_col0):
    t = proj.shape[0]
    tm = 512
    zcb = z_col0 // D_MODEL
    ng = len(outs)

    def body(*refs):
        o_refs, l_refs, z_ref = refs[:ng], refs[ng:2 * ng], refs[2 * ng]
        om_ref, g_ref, lse_ref = refs[2 * ng + 1:]
        ls = [r[...] for r in l_refs]
        mx = functools.reduce(jnp.maximum, ls)
        es = [jnp.exp(l - mx) for l in ls]
        tot = functools.reduce(jnp.add, es)
        head = lax.broadcasted_iota(jnp.int32, (LANES, D_MODEL), 0)
        col = lax.broadcasted_iota(jnp.int32, (LANES, D_MODEL), 1)
        spread = jnp.where((head < DIL_HEADS) & (col // LANES == head), 1.0, 0.0).astype(BF16)
        widen = lambda w: sum(_dot(piece, spread, NN) for piece in _split3(w))
        o = functools.reduce(jnp.add, [widen(e / tot) * r[...].astype(F32) for e, r in zip(es, o_refs)])
        z = z_ref[...].astype(F32)
        om_ref[...] = o.astype(BF16)
        g_ref[...] = (o * (z * _sigmoid(z))).astype(BF16)
        joint = mx + jnp.log(tot)
        lane = lax.broadcasted_iota(jnp.int32, joint.shape, 1)
        for hd in range(DIL_HEADS):
            lse_ref[hd] = jnp.sum(jnp.where(lane == hd, joint, 0.0), axis=-1, keepdims=True)

    rows = pl.BlockSpec((tm, D_MODEL), lambda i: (i, 0))
    lanes = pl.BlockSpec((tm, LANES), lambda i: (i, 0))
    return _pcall(
        body, name="dil_mix", grid=(t // tm,),
        in_specs=[rows] * ng + [lanes] * ng + [pl.BlockSpec((tm, D_MODEL), lambda i: (i, zcb))],
        out_specs=[rows, rows, pl.BlockSpec((DIL_HEADS, tm, 1), lambda i: (0, i, 0))],
        out_shape=[SDS((t, D_MODEL), BF16), SDS((t, D_MODEL), BF16), SDS((DIL_HEADS, t, 1), F32)],
        compiler_params=_params(),
    )(*outs, *lses, proj)


def _dil_bwd(proj, do, lse, delta, slopes, grp, dil, name):
    t = proj.shape[0]
    rows = DIL_BLOCK * DIL_UNITS
    edge_rows = DIL_BLOCK * dil
    nbk = DIL_UNITS // dil
    nsb = t // rows
    last_edge = t // edge_rows - 1
    qc, kc_, vc_ = grp * DIL_HEADS, 3 * DIL_HEADS + grp * DIL_HEADS, 6 * DIL_HEADS + grp * DIL_HEADS

    def body(q_ref, qn_ref, kp_ref, kc_ref, vp_ref, vc_ref, do_ref, don_ref, l_ref, ln_ref, d_ref, dn_ref, sl_ref,
             dqkv_ref, qf, qnf, kpf, kcf, vpf, vcf, dof, donf, dqf, dkf, dvf, staging, lf, lnf, df, dnf):
        m = pl.program_id(1)
        for src, dst in ((q_ref, qf), (qn_ref, qnf), (kp_ref, kpf), (kc_ref, kcf), (vp_ref, vpf), (vc_ref, vcf),
                         (do_ref, dof), (don_ref, donf)):
            _stage_in(src, dst, staging, dil)
        for src, dst in ((l_ref, lf), (ln_ref, lnf), (d_ref, df), (dn_ref, dnf)):
            _stage_in(src, dst, staging, dil, lead=(0,))
        slope = sl_ref[0]
        unit = lax.broadcasted_iota(jnp.int32, (DIL_UNITS, 1, 1), 0)
        has_prev = (unit >= dil) | (m > 0)
        has_next = (unit < DIL_UNITS - dil) | (m < nsb - 1)
        b16 = lambda x: x.astype(BF16)
        q, kc, vc, dout = (b16(_gather_units(x, dil)) for x in (qf, kcf, vcf, dof))
        kp, vp = b16(_gather_units(kcf, dil, -1, kpf)), b16(_gather_units(vcf, dil, -1, vpf))
        qn, don = b16(_gather_units(qf, dil, 1, qnf)), b16(_gather_units(dof, dil, 1, donf))
        lrow, drow = _gather_units(lf, dil), _gather_units(df, dil)
        lnrow, dnrow = _gather_units(lf, dil, 1, lnf), _gather_units(df, dil, 1, dnf)
        bias_p, ok_p = _dil_bias(slope, True)
        bias_c, ok_c = _dil_bias(slope, False)
        sp = jnp.where(ok_p & has_prev, _dot(q, kp, BNT) * DIL_SCALE - bias_p, NEG_INF)
        sc = jnp.where(ok_c, _dot(q, kc, BNT) * DIL_SCALE - bias_c, NEG_INF)
        pp = jnp.exp(sp - lrow)
        pc = jnp.exp(sc - lrow)
        dsp = b16(pp * (_dot(dout, vp, BNT) - drow))
        dsc = b16(pc * (_dot(dout, vc, BNT) - drow))
        _scatter_units(dqf, (_dot(dsp, kp, BNN) + _dot(dsc, kc, BNN)) * DIL_SCALE, dil)
        sn = jnp.where(ok_p & has_next, _dot(qn, kc, BNT) * DIL_SCALE - bias_p, NEG_INF)
        pn = jnp.exp(sn - lnrow)
        dsn = b16(pn * (_dot(don, vc, BNT) - dnrow))
        _scatter_units(dkf, (_dot(dsc, q, BTN) + _dot(dsn, qn, BTN)) * DIL_SCALE, dil)
        _scatter_units(dvf, _dot(b16(pc), dout, BTN) + _dot(b16(pn), don, BTN), dil)
        for s, src in enumerate((dqf, dkf, dvf)):
            _stage_out(src, dqkv_ref.at[s], staging, dil)

    prev_i = lambda m: jnp.maximum(m * nbk - 1, 0)
    next_i = lambda m: jnp.minimum((m + 1) * nbk, last_edge)
    cur = lambda col: pl.BlockSpec((rows, LANES), lambda h, m: (m, col + h))
    edge = lambda col, f: pl.BlockSpec((edge_rows, LANES), lambda h, m: (f(m), col + h))
    colcur = pl.BlockSpec((1, rows, 1), lambda h, m: (h, m, 0))
    colnext = pl.BlockSpec((1, edge_rows, 1), lambda h, m: (h, next_i(m), 0))
    out_blk = pl.BlockSpec((3, rows, LANES), lambda h, m: (0, m, h))
    big, small = pltpu.VMEM((rows, LANES), F32), pltpu.VMEM((edge_rows, LANES), F32)
    return _pcall(
        body, name=name, grid=(DIL_HEADS, nsb),
        in_specs=[cur(qc), edge(qc, next_i), edge(kc_, prev_i), cur(kc_), edge(vc_, prev_i), cur(vc_),
                  cur(0), edge(0, next_i), colcur, colnext, colcur, colnext,
                  pl.BlockSpec((1, 1, 1), lambda h, m: (h, 0, 0))],
        out_specs=out_blk, out_shape=SDS((3, t, D_MODEL), BF16),
        scratch_shapes=[big, small, small, big, small, big, big, small, big, big, big, big,
                        pltpu.VMEM((rows, 1), F32), pltpu.VMEM((edge_rows, 1), F32),
                        pltpu.VMEM((rows, 1), F32), pltpu.VMEM((edge_rows, 1), F32)],
        compiler_params=_params(),
    )(proj, proj, proj, proj, proj, proj, do, do, lse, lse, delta, delta, slopes)


def _mesh_pos():
    x, y, c = lax.axis_index("x"), lax.axis_index("y"), lax.axis_index("c")
    return x, y, c


def _peer(pos, k):
    x, y, c = pos
    px = 1 - x if k & 4 else x
    py = 1 - y if k & 2 else y
    pc = 1 - c if k & 1 else c
    return (px, py, pc), 4 * px + 2 * py + pc


N_CHIPS = 4
CHIP_FLIPS = ((1, 0), (0, 1), (1, 1))


def _other_chips(x, y):
    return [(1 - x if fx else x, 1 - y if fy else y) for fx, fy in CHIP_FLIPS]


def _all_gather(arrays):
    n = len(arrays)
    per = 2 * N_CHIPS - 1
    hbm = pl.BlockSpec(memory_space=pltpu.HBM)

    def body(*refs):
        ins, outs = refs[:n], refs[n:2 * n]
        send_sems, recv_sems, local_sems = refs[2 * n:]
        x, y, c = _mesh_pos()
        sibling = (x, y, 1 - c)
        chips = _other_chips(x, y)
        block = lambda px, py, pc: 4 * px + 2 * py + pc

        def copy(w, k, src, blk, to):
            return pltpu.make_async_remote_copy(
                src_ref=src, dst_ref=outs[w].at[blk], send_sem=send_sems.at[w * per + k],
                recv_sem=recv_sems.at[w * per + k], device_id=to, device_id_type=MESH)

        local, started = [], []
        for w in range(n):
            cp = pltpu.make_async_copy(ins[w], outs[w].at[block(x, y, c)], local_sems.at[w])
            cp.start()
            local.append(cp)
            started.append(copy(w, 0, ins[w], block(x, y, c), sibling))
            for j, (px, py) in enumerate(chips):
                started.append(copy(w, 1 + j, ins[w], block(x, y, c), (px, py, c)))
        for cp in started:
            cp.start()
        for j, (px, py) in enumerate(chips):
            for w in range(n):
                copy(w, 1 + j, ins[w], block(px, py, c), sibling).wait_recv()
                cp = copy(w, 4 + j, outs[w].at[block(px, py, c)], block(px, py, c), sibling)
                cp.start()
                started.append(cp)
        for w in range(n):
            copy(w, 0, ins[w], block(x, y, 1 - c), sibling).wait_recv()
            for j, (px, py) in enumerate(chips):
                copy(w, 4 + j, ins[w], block(px, py, 1 - c), sibling).wait_recv()
        for cp in started:
            cp.wait_send()
        for cp in local:
            cp.wait()

    return _pcall(
        body, name="all_gather_weights", in_specs=[hbm] * n, out_specs=[hbm] * n,
        out_shape=[SDS((N_DEV,) + a.shape, a.dtype) for a in arrays],
        scratch_shapes=[pltpu.SemaphoreType.DMA((n * per,)), pltpu.SemaphoreType.DMA((n * per,)),
                        pltpu.SemaphoreType.DMA((n,))],
    )(*arrays)


HBM_SPEC = pl.BlockSpec(memory_space=pltpu.HBM)
SEM_SPEC = pl.BlockSpec(memory_space=pltpu.SEMAPHORE)
DATAFLOW = pltpu.SideEffectType.DATAFLOW_SIDE_EFFECTING


def _push_start(arrays, scatter, name):
    n = len(arrays)
    per = N_DEV - 1

    def body(*refs):
        srcs, lands = refs[:n], refs[n:2 * n]
        send_sems, recv_sems, token = refs[2 * n], refs[2 * n + 1], refs[-1]
        pos = _mesh_pos()
        me = 4 * pos[0] + 2 * pos[1] + pos[2]
        for w in range(n):
            for k in range(1, N_DEV):
                peer, peer_idx = _peer(pos, k)
                pltpu.make_async_remote_copy(
                    src_ref=srcs[w].at[peer_idx] if scatter else srcs[w], dst_ref=lands[w].at[me],
                    send_sem=send_sems.at[w * per + k - 1], recv_sem=recv_sems.at[w * per + k - 1],
                    device_id=peer, device_id_type=MESH).start()
        token[...] = jnp.zeros_like(token)

    land_shapes = [a.shape if scatter else (N_DEV,) + a.shape for a in arrays]
    in_hbm = lambda a: pltpu.with_memory_space_constraint(a, pltpu.HBM)
    lands = [in_hbm(lax.empty(s, a.dtype)) for s, a in zip(land_shapes, arrays)]
    sems = pltpu.SemaphoreType.DMA((n * per,))
    res = _pcall(
        body, name=name,
        out_shape=(sems, sems, *[pltpu.HBM(a.shape, a.dtype) for a in arrays],
                   *[pltpu.HBM(s, a.dtype) for s, a in zip(land_shapes, arrays)], SDS((8, LANES), F32)),
        in_specs=[HBM_SPEC] * (2 * n),
        out_specs=(SEM_SPEC, SEM_SPEC, *[HBM_SPEC] * (2 * n), pl.BlockSpec(memory_space=pltpu.VMEM)),
        input_output_aliases={i: 2 + i for i in range(2 * n)},
        compiler_params=pltpu.CompilerParams(has_side_effects=DATAFLOW),
    )(*[in_hbm(a) for a in arrays], *lands)
    return res[0], res[1], list(res[2:2 + n]), list(res[2 + n:2 + 2 * n]), res[-1]


def _push_wait(send_sems, recv_sems, arrays, lands, after, scatter, name):
    n = len(arrays)
    per = N_DEV - 1

    def body(*refs):
        srcs, lands_ = refs[:n], refs[n:2 * n]
        send_sems_, recv_sems_ = refs[2 * n], refs[2 * n + 1]
        pos = _mesh_pos()
        for w in range(n):
            for k in range(1, N_DEV):
                peer, peer_idx = _peer(pos, k)
                cp = pltpu.make_async_remote_copy(
                    src_ref=srcs[w].at[peer_idx] if scatter else srcs[w], dst_ref=lands_[w].at[peer_idx],
                    send_sem=send_sems_.at[w * per + k - 1], recv_sem=recv_sems_.at[w * per + k - 1],
                    device_id=peer, device_id_type=MESH)
                cp.wait_send()
                cp.wait_recv()

    res = _pcall(
        body, name=name,
        out_shape=(*[pltpu.HBM(a.shape, a.dtype) for a in arrays], *[pltpu.HBM(l.shape, l.dtype) for l in lands]),
        in_specs=[HBM_SPEC] * (2 * n) + [SEM_SPEC, SEM_SPEC, pl.BlockSpec(memory_space=pl.ANY)],
        out_specs=[HBM_SPEC] * (2 * n), input_output_aliases={i: i for i in range(2 * n)},
        compiler_params=pltpu.CompilerParams(has_side_effects=DATAFLOW),
    )(*arrays, *lands, send_sems, recv_sems, after)
    return list(res[n:])


def _fill_own(land, own):
    me = 4 * lax.axis_index("x") + 2 * lax.axis_index("y") + lax.axis_index("c")
    return lax.dynamic_update_slice(land, own[None], (me,) + (0,) * own.ndim)


def _adam_math(w, g, m, v):
    m = ADAM_B1 * m + (1.0 - ADAM_B1) * g
    v = ADAM_B2 * v + (1.0 - ADAM_B2) * (g * g)
    m_hat = m / (1.0 - ADAM_B1 ** ADAM_STEP)
    v_hat = v / (1.0 - ADAM_B2 ** ADAM_STEP)
    delta = -ADAM_LR * (m_hat / (jnp.sqrt(v_hat) + ADAM_EPS) + ADAM_WD * w)
    return delta, m, v


def _adamw(recv, w, m, v, name):
    n_parts, r, c = recv.shape
    layers, rows_per_layer, _ = w.shape
    tr = min(rows_per_layer, 128)
    per_layer = rows_per_layer // tr

    def body(g_ref, w_ref, m_ref, v_ref, go_ref, d_ref, mo_ref, vo_ref):
        g = g_ref[0].astype(F32)
        for s in range(1, n_parts):
            g = g + g_ref[s].astype(F32)
        delta, mn, vn = _adam_math(w_ref[...], g, m_ref[...], v_ref[...])
        go_ref[...] = g
        d_ref[...] = delta
        mo_ref[...] = mn
        vo_ref[...] = vn

    blk = pl.BlockSpec((None, tr, c), lambda i: (i // per_layer, i % per_layer, 0))
    return _pcall(
        body, name=name, grid=(r // tr,),
        in_specs=[pl.BlockSpec((n_parts, tr, c), lambda i: (0, i, 0)), blk, blk, blk],
        out_specs=[blk] * 4, out_shape=[SDS(w.shape, F32)] * 4, compiler_params=_params(),
    )(recv, w, m, v)


VEC_ROWS = 32


def _small_allreduce_adamw(vec, w, m, v):
    def body(vec_ref, w_ref, m_ref, v_ref, g_ref, d_ref, mo_ref, vo_ref, gath, send_sems, recv_sems):
        pos = _mesh_pos()
        me = 4 * pos[0] + 2 * pos[1] + pos[2]
        sends, recvs = [], []
        for k in range(1, N_DEV):
            peer, peer_idx = _peer(pos, k)
            cp = pltpu.make_async_remote_copy(src_ref=vec_ref, dst_ref=gath.at[me], send_sem=send_sems.at[k - 1],
                                              recv_sem=recv_sems.at[k - 1], device_id=peer, device_id_type=MESH)
            cp.start()
            sends.append(cp)
            recvs.append(pltpu.make_async_remote_copy(src_ref=vec_ref, dst_ref=gath.at[peer_idx],
                                                      send_sem=send_sems.at[k - 1], recv_sem=recv_sems.at[k - 1],
                                                      device_id=peer, device_id_type=MESH))
        gath[me] = vec_ref[...]
        for cp in recvs:
            cp.wait_recv()
        for cp in sends:
            cp.wait_send()
        tot = gath[0]
        for s in range(1, N_DEV):
            tot = tot + gath[s]
        rowi = lax.broadcasted_iota(jnp.int32, (8, LANES), 0)
        mine = jnp.sum(jnp.where(rowi == me, tot[16:24, :], 0.0), axis=0, keepdims=True)
        g = jnp.concatenate([tot[0:16, :], jnp.broadcast_to(mine, (8, LANES)), tot[24:32, :]], axis=0)
        delta, mn, vn = _adam_math(w_ref[...], g, m_ref[...], v_ref[...])
        g_ref[...] = g
        d_ref[...] = delta
        mo_ref[...] = mn
        vo_ref[...] = vn

    vm = pl.BlockSpec(memory_space=pltpu.VMEM)
    return _pcall(
        body, name="small_allreduce_adamw", in_specs=[vm] * 4, out_specs=[vm] * 4,
        out_shape=[SDS((VEC_ROWS, LANES), F32)] * 4,
        scratch_shapes=[pltpu.VMEM((N_DEV, VEC_ROWS, LANES), F32), pltpu.SemaphoreType.DMA((N_DEV - 1,)),
                        pltpu.SemaphoreType.DMA((N_DEV - 1,))],
        compiler_params=pltpu.CompilerParams(has_side_effects=True),
    )(vec, w, m, v)


def _cols_to_slabs(a):
    r, c8 = a.shape
    return a.reshape(r, N_DEV, c8 // N_DEV).transpose(1, 0, 2)


def _slabs_to_cols(a):
    n, r, c = a.shape
    return a.transpose(1, 0, 2).reshape(r, n * c)


def _rows8(vec):
    return vec.reshape(-1, LANES)


def _pad_rows(a, rows):
    return jnp.pad(a, ((0, rows - a.shape[0]), (0, LANES - a.shape[1])))


def kernel(x, p, fox_norm, fox_w_in, fox_b_f, fox_w_out, dil_norm, dil_w_in, dil_w_out, ple_w_up, ple_w_gate, final_norm, loss_target, m_fox_norm, m_fox_w_in, m_fox_b_f, m_fox_w_out, m_dil_norm, m_dil_w_in, m_dil_w_out, m_ple_w_up, m_ple_w_gate, m_final_norm, v_fox_norm, v_fox_w_in, v_fox_b_f, v_fox_w_out, v_dil_norm, v_dil_w_in, v_dil_w_out, v_ple_w_up, v_ple_w_gate, v_final_norm):
    t = x.shape[1]
    d = D_MODEL
    xs, tgt = x[0], loss_target[0]
    p0, p1 = p[0, 0], p[1, 0]
    fox_cols = fox_w_in.shape[2]
    ple_dim = ple_w_up.shape[1]

    later = [dil_w_in[0].astype(BF16), dil_w_out[0].astype(BF16), ple_w_up.reshape(-1, LANES).astype(BF16),
             ple_w_gate.reshape(-1, d).astype(BF16), dil_norm]
    push = _push_start(later, False, "gather_later_start")
    gw = _all_gather([fox_w_in[0].astype(BF16), fox_w_out[0].astype(BF16)])
    w_fox_in = _slabs_to_cols(gw[0])
    w_fox_main = w_fox_in[:, :4 * d]
    w_fox_f = jnp.pad(w_fox_in[:, 4 * d:], ((0, 0), (0, LANES - FOX_HEADS)))
    w_fox_out = gw[1].reshape(d, d)
    b_pad = jnp.pad(fox_b_f, ((0, 0), (0, LANES - FOX_HEADS)))

    n0, r0 = _rms_fwd(xs, fox_norm + push[4][0:1, 0:1], "rms_fox")
    proj0 = _mm([n0], w_fox_main, "nn", "fox_in_proj", tiles=IN_PROJ_TILES)
    projf = _mm([n0], w_fox_f, "nn", "fox_gate_proj", tiles=IN_PROJ_TILES, out_dtype=F32)
    c_all = _fox_gate_fwd(projf, b_pad)
    qaug_fwd = _fox_aug(c_all, c_all, 1.0, 0.0, 0, FOX_AUG, "fox_aug_q_fwd")
    kaug = _fox_aug(c_all, c_all, -1.0, 0.0, FOX_AUG, 0, "fox_aug_k")
    o0, g0, lse0 = _fox_fwd(proj0, qaug_fwd, kaug)
    h1 = _mm_residual(g0, w_fox_out, "nn", xs, "fox_out_proj")

    landed = _push_wait(push[0], push[1], push[2], push[3], h1, False, "gather_later_wait")
    gl = [_fill_own(zone, own) for zone, own in zip(landed, later)]
    w_dil_in = _slabs_to_cols(gl[0])
    w_dil_out = gl[1].reshape(d, d)
    w_up = gl[2].reshape(N_DEV, 2, ple_dim, LANES).transpose(1, 2, 0, 3).reshape(2, ple_dim, d)
    w_gate = gl[3].reshape(N_DEV, 2, d // N_DEV, d).transpose(1, 0, 2, 3).reshape(2, d, d)
    dil_norm_full = gl[4].reshape(1, d)
    h2, u0, a0 = _ple_fwd(h1, p0, w_up[0], w_gate[0], "ple0_fwd")

    n1, r1 = _rms_fwd(h2, dil_norm_full, "rms_dil")
    proj1 = _mm([n1], w_dil_in, "nn", "dil_in_proj", tiles=IN_PROJ_TILES)
    n_heads = len(DIL_PATTERN) * DIL_HEADS
    slopes = 2.0 ** (-ALIBI_MAX_EXP * jnp.arange(1, n_heads + 1, dtype=F32) / n_heads)
    dil_o, dil_lse, dil_slopes = [], [], []
    for grp, (_, dil) in enumerate(DIL_PATTERN):
        sl = (slopes[grp * DIL_HEADS:(grp + 1) * DIL_HEADS] * dil).reshape(DIL_HEADS, 1, 1)
        og, lg = _dil_fwd(proj1, sl, grp, dil, f"dil_attn_fwd_{grp}")
        dil_o.append(og)
        dil_lse.append(lg)
        dil_slopes.append(sl)
    z1_col0 = 9 * d
    o1, g1, lse1 = _dil_mix(dil_o, dil_lse, proj1, z1_col0)
    h3 = _mm_residual(g1, w_dil_out, "nn", h2, "dil_out_proj")
    h4, u1, a1 = _ple_fwd(h3, p1, w_up[1], w_gate[1], "ple1_fwd")

    dh4, d_final_norm, loss_part = _final_bwd(h4, final_norm.reshape(1, d), tgt)

    du1, da1 = _ple_bwd_elem(dh4, u1, a1, "ple1_bwd_elem")
    dw_up1 = _dw(p1, du1, "ple1_dw_up")
    dw_gate1 = _dw(h3, da1, "ple1_dw_gate")
    dh3 = _mm_residual(da1, w_gate[1], "nt", dh4, "ple1_dh")

    dw_dil_out = _dw(g1, dh3, "dil_dw_out")
    do1, dz1, delta1 = _mm_gate_bwd(dh3, w_dil_out, proj1, z1_col0, o1, DIL_HEADS, "dil_dgate")
    n_grp = len(DIL_PATTERN)
    dqkv = [_dil_bwd(proj1, do1, lse1, delta1, dil_slopes[grp], grp, dil, f"dil_attn_bwd_{grp}")
            for grp, (_, dil) in enumerate(DIL_PATTERN)]
    dw_cols = [_dw(n1, dqkv[grp], f"dil_dw_in_{kind}{grp}", sub=kind) for kind in range(3) for grp in range(n_grp)]
    dw_dil_in = jnp.concatenate(dw_cols + [_dw(n1, dz1, "dil_dw_in_z")], axis=1)
    row_slabs = lambda a: a.reshape(N_DEV, a.shape[0] // N_DEV, a.shape[1])
    dil_slabs = [_cols_to_slabs(dw_dil_in), row_slabs(dw_dil_out), _cols_to_slabs(dw_up1), row_slabs(dw_gate1)]
    dil_push = _push_start(dil_slabs, True, "scatter_dil_start")
    group_major = lambda kb: jnp.where(kb < 3 * n_grp, (kb % 3) * n_grp + kb // 3, kb)
    dh2, d_dil_norm = _mm_in_bwd(dqkv + [dz1], w_dil_in, h2, dil_norm_full + dil_push[4][0:1, 0:1], r1, dh3, "dil_dx",
                                 w_kmap=group_major)

    du0, da0 = _ple_bwd_elem(dh2, u0, a0, "ple0_bwd_elem")
    dw_up0 = _dw(p0, du0, "ple0_dw_up")
    dw_gate0 = _dw(h1, da0, "ple0_dw_gate")
    dh1 = _mm_residual(da0, w_gate[0], "nt", dh2, "ple0_dh")

    dw_fox_out = _dw(g0, dh1, "fox_dw_out")
    do0, dz0, delta0 = _mm_gate_bwd(dh1, w_fox_out, proj0, 3 * d, o0, FOX_HEADS, "fox_dgate")
    head_cols = lambda a: jnp.pad(a, ((0, 0), (0, LANES - FOX_HEADS)))
    lse_cols = head_cols(lse0.reshape(FOX_HEADS, t).T)
    qaug_bwd = _fox_aug(c_all, lse_cols, 1.0, -1.0, 0, FOX_AUG, "fox_aug_q_bwd")
    doaug = _fox_aug(delta0, delta0, -1.0, 0.0, 0, None, "fox_aug_do")
    dq0, dk0, dv0, dck_wide, dcq = _fox_bwd(proj0, do0, qaug_bwd, kaug, doaug)
    dc_query = head_cols(dcq[:, :, 0, :].reshape(FOX_HEADS, t).T)
    df, d_b_f = _fox_gate_bwd(projf, b_pad, dc_query, dck_wide)
    dproj0 = [dq0, dk0, dv0, dz0]
    dw_fox_parts = [_dw(n0, dpart, f"fox_dw_in_{s}") for s, dpart in enumerate(dproj0)]
    dw_fox_f = _dw(n0, df, "fox_dw_gate")
    dw_fox_in = jnp.concatenate(dw_fox_parts + [dw_fox_f[:, :FOX_HEADS]], axis=1)
    fox_slabs = [_cols_to_slabs(dw_fox_in), row_slabs(dw_fox_out), _cols_to_slabs(dw_up0), row_slabs(dw_gate0)]
    fox_push = _push_start(fox_slabs, True, "scatter_fox_start")
    grad_x, d_fox_norm = _mm_in_bwd(dproj0, w_fox_main, xs, fox_norm + fox_push[4][0:1, 0:1], r0, dh1, "fox_dx",
                                    more=(df, w_fox_f))

    me = 4 * lax.axis_index("x") + 2 * lax.axis_index("y") + lax.axis_index("c")

    def landed(push, slabs, name):
        zones = _push_wait(push[0], push[1], push[2], push[3], grad_x, True, name)
        return [_fill_own(zone, lax.dynamic_index_in_dim(own, me, 0, keepdims=False)) for zone, own in zip(zones, slabs)]

    g_dil_in, g_dil_out, g_up1, g_gate1 = landed(dil_push, dil_slabs, "scatter_dil_wait")
    g_fox_in, g_fox_out, g_up0, g_gate0 = landed(fox_push, fox_slabs, "scatter_fox_wait")
    upd = {"fox_w_in": _adamw(g_fox_in, fox_w_in, m_fox_w_in, v_fox_w_in, "adamw_fox_w_in"),
           "fox_w_out": _adamw(g_fox_out, fox_w_out, m_fox_w_out, v_fox_w_out, "adamw_fox_w_out"),
           "dil_w_in": _adamw(g_dil_in, dil_w_in, m_dil_w_in, v_dil_w_in, "adamw_dil_w_in"),
           "dil_w_out": _adamw(g_dil_out, dil_w_out, m_dil_w_out, v_dil_w_out, "adamw_dil_w_out")}
    for nm, grads, params in (("ple_w_up", (g_up0, g_up1), (ple_w_up, m_ple_w_up, v_ple_w_up)),
                              ("ple_w_gate", (g_gate0, g_gate1), (ple_w_gate, m_ple_w_gate, v_ple_w_gate))):
        layers = [_adamw(g, *[a[l:l + 1] for a in params], f"adamw_{nm}_{l}") for l, g in enumerate(grads)]
        upd[nm] = [jnp.concatenate([layers[0][k], layers[1][k]], axis=0) for k in range(4)]

    loss_row = jnp.where(jnp.arange(LANES) == 0, loss_part, 0.0)
    vec = jnp.concatenate([_rows8(d_fox_norm), _rows8(d_final_norm), _rows8(d_dil_norm), d_b_f, loss_row,
                           jnp.zeros((VEC_ROWS - 26, LANES), F32)], axis=0)

    def small_pack(a_fox_norm, a_final_norm, a_dil_norm, a_b_f):
        return jnp.concatenate([_rows8(a_fox_norm), _rows8(a_final_norm), _pad_rows(a_dil_norm, 8),
                                _pad_rows(a_b_f, 8)], axis=0)

    sg, sd, sm, sv = _small_allreduce_adamw(
        vec, small_pack(fox_norm, final_norm, dil_norm, fox_b_f),
        small_pack(m_fox_norm, m_final_norm, m_dil_norm, m_fox_b_f),
        small_pack(v_fox_norm, v_final_norm, v_dil_norm, v_fox_b_f))

    def small_unpack(a):
        return {"fox_norm": a[0:8].reshape(1, d), "final_norm": a[8:16].reshape(d), "dil_norm": a[16:17],
                "fox_b_f": a[24:25, :FOX_HEADS]}

    loss = sg[25, 0]
    order = ["fox_norm", "fox_w_in", "fox_b_f", "fox_w_out", "dil_norm", "dil_w_in", "dil_w_out", "ple_w_up",
             "ple_w_gate", "final_norm"]
    out = [loss, grad_x[None]]
    for idx, small in enumerate((sg, sd, sm, sv)):
        sp = small_unpack(small)
        out += [sp[nm] if nm in sp else upd[nm][idx] for nm in order]
    return tuple(out)
```

```python
import functools

import jax
import jax.numpy as jnp
from jax import lax
from jax.experimental import pallas as pl
from jax.experimental.pallas import tpu as pltpu

F32 = jnp.float32
BF16 = jnp.bfloat16
SDS = jax.ShapeDtypeStruct

D_MODEL = 1024
N_DEV = 8
LANES = 128
FOX_HEADS = 16
FOX_HEAD_DIM = 64
FOX_PAIRS = FOX_HEADS // 2
DIL_HEADS = 8
DIL_BLOCK = 128
DIL_PATTERN = ((128, 1), (512, 4), (2048, 16))
ALIBI_MAX_EXP = 8.0
RMS_EPS = 1e-6
ADAM_LR, ADAM_B1, ADAM_B2, ADAM_EPS, ADAM_WD, ADAM_STEP = 0.001, 0.9, 0.999, 1e-08, 0.01, 10
VMEM_LIMIT = 48 * 1024 * 1024
NEG_INF = float("-inf")

NN = (((1,), (0,)), ((), ()))
NT = (((1,), (1,)), ((), ()))
TN = (((0,), (0,)), ((), ()))
MESH = pl.DeviceIdType.MESH


def _pcall(body, **kw):
    return pl.pallas_call(body, **kw)


def _params(**kw):
    return pltpu.CompilerParams(vmem_limit_bytes=VMEM_LIMIT, **kw)


def _dot(a, b, dims):
    return lax.dot_general(a, b, dims, preferred_element_type=F32)


def _sigmoid(x):
    return 1.0 / (1.0 + jnp.exp(-x))


def _mm(a_parts, b, mode, name, tiles=(512, 1024, 1024), extras=(), outs=None, epilogue=None, out_dtype=BF16,
        b_kmap=None, b_sub=None):
    na = len(a_parts)
    stack = [a.shape[0] if a.ndim == 3 else 1 for a in a_parts]
    first = [sum(stack[:s]) for s in range(na)]
    if mode == "tn":
        k_part, m = a_parts[0].shape
        n = b.shape[-1]
    else:
        m, k_part = a_parts[0].shape[-2:]
        n = b.shape[1] if mode == "nn" else b.shape[0]
    tm, tn, tk = min(tiles[0], m), min(tiles[1], n), min(tiles[2], k_part)
    kb = k_part // tk
    nk = sum(stack) * kb
    grid = (m // tm, n // tn, nk)
    b_kmap = b_kmap or (lambda k: k)

    in_specs = []
    for s in range(na):
        if mode == "tn":
            in_specs.append(pl.BlockSpec((tk, tm), lambda i, j, k: (k, i)))
            continue

        def rel(k, s=s):
            return jnp.clip(k - first[s] * kb, 0, stack[s] * kb - 1)

        if a_parts[s].ndim == 3:
            in_specs.append(pl.BlockSpec((None, tm, tk), lambda i, j, k, rel=rel: (rel(k) // kb, i, rel(k) % kb)))
        else:
            in_specs.append(pl.BlockSpec((tm, tk), lambda i, j, k, rel=rel: (i, rel(k))))
    if mode == "nt":
        in_specs.append(pl.BlockSpec((tn, tk), lambda i, j, k: (j, b_kmap(k))))
    elif b_sub is not None:
        in_specs.append(pl.BlockSpec((None, tk, tn), lambda i, j, k: (b_sub, k, j)))
    else:
        in_specs.append(pl.BlockSpec((tk, tn), lambda i, j, k: (b_kmap(k), j)))
    for _, blk, imap in extras:
        in_specs.append(pl.BlockSpec(blk, imap))
    if outs is None:
        outs = [(SDS((m, n), out_dtype), (tm, tn), lambda i, j, k: (i, j))]
    out_specs = [pl.BlockSpec(blk, imap) for _, blk, imap in outs]
    ne, no = len(extras), len(outs)
    dims = {"nn": NN, "nt": NT, "tn": TN}[mode]

    def finish(res, e_refs, o_refs, i):
        if epilogue is None:
            o_refs[0][...] = res.astype(o_refs[0].dtype)
        else:
            epilogue(res, e_refs, o_refs, i)

    def body(*refs):
        a_refs = refs[:na]
        b_ref = refs[na]
        e_refs = refs[na + 1:na + 1 + ne]
        o_refs = refs[na + 1 + ne:na + 1 + ne + no]
        i, k = pl.program_id(0), pl.program_id(2)
        if nk == 1:
            finish(_dot(a_refs[0][...].astype(BF16), b_ref[...].astype(BF16), dims), e_refs, o_refs, i)
            return
        acc = refs[-1]

        @pl.when(k == 0)
        def _():
            acc[...] = jnp.zeros_like(acc)

        def step(a_ref):
            acc[...] += _dot(a_ref[...].astype(BF16), b_ref[...].astype(BF16), dims)

        for s in range(na):
            if na == 1:
                step(a_refs[0])
            else:
                in_use = (k >= first[s] * kb) & (k < (first[s] + stack[s]) * kb)
                pl.when(in_use)(functools.partial(step, a_refs[s]))

        @pl.when(k == nk - 1)
        def _():
            finish(acc[...], e_refs, o_refs, i)

    res = _pcall(
        body, name=name, grid=grid, in_specs=in_specs, out_specs=out_specs,
        out_shape=[o[0] for o in outs], scratch_shapes=[] if nk == 1 else [pltpu.VMEM((tm, tn), F32)],
        compiler_params=_params(dimension_semantics=("arbitrary", "arbitrary", "arbitrary")),
    )(*a_parts, b, *[e[0] for e in extras])
    return res[0] if len(res) == 1 else res


IN_PROJ_TILES = (1024, 1024, 1024)
DW_TILES = (1024, 1024, 1024)


def _dw(x, dy, name, sub=None):
    return _mm([x], dy, "tn", name, tiles=DW_TILES, b_sub=sub)


def _add_extra_epilogue(acc, e_refs, o_refs, i):
    o_refs[0][...] = acc + e_refs[0][...]


def _mm_residual(a, b, mode, res, name):
    m = a.shape[0]
    n = b.shape[1] if mode == "nn" else b.shape[0]
    tm, tn = 512, 1024
    return _mm([a], b, mode, name, tiles=(tm, tn, 1024),
               extras=[(res, (tm, tn), lambda i, j, k: (i, j))],
               outs=[(SDS((m, n), F32), (tm, tn), lambda i, j, k: (i, j))],
               epilogue=_add_extra_epilogue)


def _rms_fwd(h, g, name):
    t, d = h.shape
    tm = 512

    def body(h_ref, g_ref, n_ref, r_ref):
        x = h_ref[...]
        r = lax.rsqrt(jnp.mean(x * x, axis=-1, keepdims=True) + RMS_EPS)
        n_ref[...] = ((x * r) * g_ref[...]).astype(BF16)
        r_ref[...] = r

    return _pcall(
        body, name=name, grid=(t // tm,),
        in_specs=[pl.BlockSpec((tm, d), lambda i: (i, 0)), pl.BlockSpec((1, d), lambda i: (0, 0))],
        out_specs=[pl.BlockSpec((tm, d), lambda i: (i, 0)), pl.BlockSpec((tm, 1), lambda i: (i, 0))],
        out_shape=[SDS((t, d), BF16), SDS((t, 1), F32)],
        compiler_params=_params(),
    )(h, g)


def _rms_bwd_rows(dn, x, g, r):
    xhat = x * r
    dxhat = dn * g
    dx = r * (dxhat - xhat * jnp.mean(dxhat * xhat, axis=-1, keepdims=True))
    dg = jnp.sum(dn * xhat, axis=0, keepdims=True)
    return dx, dg


def _mm_in_bwd(d_parts, w, h, g, r, dres, name, more=None, w_kmap=None):
    t = h.shape[0]
    tm = 1024
    row = lambda i, j, k: (i, 0)
    extras = []
    if more is not None:
        extras = [(more[0], (tm, more[0].shape[1]), row), (more[1], more[1].shape, lambda i, j, k: (0, 0))]

    def epilogue(acc, e_refs, o_refs, i):
        o_refs[0][...] = acc if more is None else acc + _dot(e_refs[0][...], e_refs[1][...], NT)

    dn = _mm(d_parts, w, "nt", name, tiles=(tm, D_MODEL, D_MODEL), extras=extras,
             outs=[(SDS((t, D_MODEL), F32), (tm, D_MODEL), row)], epilogue=epilogue, b_kmap=w_kmap)

    tr = 256

    def body(dn_ref, h_ref, g_ref, r_ref, dres_ref, dh_ref, dg_ref):
        i = pl.program_id(0)
        dx, dg = _rms_bwd_rows(dn_ref[...], h_ref[...], g_ref[...], r_ref[...])
        dh_ref[...] = dres_ref[...] + dx

        @pl.when(i == 0)
        def _():
            dg_ref[...] = dg

        @pl.when(i > 0)
        def _():
            dg_ref[...] += dg

    rows = pl.BlockSpec((tr, D_MODEL), lambda i: (i, 0))
    gain = pl.BlockSpec((1, D_MODEL), lambda i: (0, 0))
    return _pcall(
        body, name=name + "_norm", grid=(t // tr,),
        in_specs=[rows, rows, gain, pl.BlockSpec((tr, 1), lambda i: (i, 0)), rows], out_specs=[rows, gain],
        out_shape=[SDS((t, D_MODEL), F32), SDS((1, D_MODEL), F32)], compiler_params=_params(),
    )(dn, h, g, r, dres)


def _final_bwd(h, g, tgt):
    t, d = h.shape
    tm = 256

    def body(h_ref, g_ref, t_ref, dh_ref, dg_ref, loss_ref):
        i = pl.program_id(0)
        x = h_ref[...]
        gg = g_ref[...]
        r = lax.rsqrt(jnp.mean(x * x, axis=-1, keepdims=True) + RMS_EPS)
        err = (x * r) * gg - t_ref[...]
        part = 0.5 * jnp.sum(jnp.mean(err * err, axis=-1, keepdims=True), axis=0, keepdims=True)
        dx, dg = _rms_bwd_rows(err * (1.0 / d), x, gg, r)
        dh_ref[...] = dx

        @pl.when(i == 0)
        def _():
            dg_ref[...] = dg
            loss_ref[...] = jnp.broadcast_to(part, loss_ref.shape)

        @pl.when(i > 0)
        def _():
            dg_ref[...] += dg
            loss_ref[...] += jnp.broadcast_to(part, loss_ref.shape)

    return _pcall(
        body, name="final_norm_loss", grid=(t // tm,),
        in_specs=[pl.BlockSpec((tm, d), lambda i: (i, 0)), pl.BlockSpec((1, d), lambda i: (0, 0)),
                  pl.BlockSpec((tm, d), lambda i: (i, 0))],
        out_specs=[pl.BlockSpec((tm, d), lambda i: (i, 0)), pl.BlockSpec((1, d), lambda i: (0, 0)),
                   pl.BlockSpec((1, LANES), lambda i: (0, 0))],
        out_shape=[SDS((t, d), F32), SDS((1, d), F32), SDS((1, LANES), F32)],
        compiler_params=_params(),
    )(h, g, tgt)


GATE_ROWS = 256


def _split3(x):
    hi = x.astype(BF16)
    r1 = x - hi.astype(F32)
    mid = r1.astype(BF16)
    lo = (r1 - mid.astype(F32)).astype(BF16)
    return hi, mid, lo


def _tri_sum(x, upper):
    rows = x.shape[0]
    ri = lax.broadcasted_iota(jnp.int32, (rows, rows), 0)
    ci = lax.broadcasted_iota(jnp.int32, (rows, rows), 1)
    tri = jnp.where((ri <= ci) if upper else (ri >= ci), 1.0, 0.0).astype(BF16)
    hi, mid, lo = _split3(x)
    return _dot(tri, hi, NN) + _dot(tri, mid, NN) + _dot(tri, lo, NN)


def _log_sigmoid(x):
    return jnp.minimum(x, 0.0) - jnp.log1p(jnp.exp(-jnp.abs(x)))


def _fox_gate_fwd(projf, bpad):
    t = projf.shape[0]
    tb = GATE_ROWS

    def body(x_ref, b_ref, c_ref, carry):
        i = pl.program_id(0)

        @pl.when(i == 0)
        def _():
            carry[...] = jnp.zeros_like(carry)

        c_ref[...] = _tri_sum(_log_sigmoid(x_ref[...] + b_ref[...]), upper=False) + carry[...]
        carry[...] = c_ref[pl.ds(tb - 1, 1), :]

    return _pcall(
        body, name="fox_gate_fwd", grid=(t // tb,),
        in_specs=[pl.BlockSpec((tb, LANES), lambda i: (i, 0)), pl.BlockSpec((1, LANES), lambda i: (0, 0))],
        out_specs=pl.BlockSpec((tb, LANES), lambda i: (i, 0)),
        out_shape=SDS((t, LANES), F32), scratch_shapes=[pltpu.VMEM((1, LANES), F32)],
        compiler_params=_params(),
    )(projf, bpad)


def _fox_gate_bwd(projf, bpad, dc_query, dc_key_wide):
    t = projf.shape[0]
    tb = GATE_ROWS
    nb = t // tb

    def body(x_ref, b_ref, dcq_ref, dck_ref, df_ref, db_ref, carry, buf):
        i = pl.program_id(0)

        @pl.when(i == 0)
        def _():
            carry[...] = jnp.zeros_like(carry)

        src = lax.broadcasted_iota(jnp.int32, (D_MODEL, LANES), 0)
        head = lax.broadcasted_iota(jnp.int32, (D_MODEL, LANES), 1)
        pick = jnp.where((head < FOX_HEADS) & (src == (head // 2) * LANES + (1 - head % 2) * FOX_HEAD_DIM), 1.0, 0.0)
        pick = pick.astype(BF16)
        dc_key = sum(_dot(piece, pick, NN) for piece in _split3(dck_ref[...]))
        buf[...] = _tri_sum(dcq_ref[...] - dc_key, upper=True) + carry[...]
        carry[...] = buf[pl.ds(0, 1), :]
        df = buf[...] * _sigmoid(-(x_ref[...] + b_ref[...]))
        df_ref[...] = df.astype(BF16)
        part = jnp.sum(df, axis=0, keepdims=True)

        @pl.when(i == 0)
        def _():
            db_ref[...] = part

        @pl.when(i > 0)
        def _():
            db_ref[...] += part

    rev = lambda i: (nb - 1 - i, 0)
    return _pcall(
        body, name="fox_gate_bwd", grid=(nb,),
        in_specs=[pl.BlockSpec((tb, LANES), rev), pl.BlockSpec((1, LANES), lambda i: (0, 0)),
                  pl.BlockSpec((tb, LANES), rev), pl.BlockSpec((tb, D_MODEL), rev)],
        out_specs=[pl.BlockSpec((tb, LANES), rev), pl.BlockSpec((1, LANES), lambda i: (0, 0))],
        out_shape=[SDS((t, LANES), BF16), SDS((1, LANES), F32)],
        scratch_shapes=[pltpu.VMEM((1, LANES), F32), pltpu.VMEM((tb, LANES), F32)],
        compiler_params=_params(),
    )(projf, bpad, dc_query, dc_key_wide)


FOX_TQ = 1024
FOX_TQ_FWD = 1024
FOX_SCALE = FOX_HEAD_DIM ** -0.5


def _low_lanes(shape):
    return lax.broadcasted_iota(jnp.int32, shape, len(shape) - 1) < FOX_HEAD_DIM


FOX_AUG = 3
FOX_CHAIN = 256
FOX_SUM_ROWS = 8


def _top_rows(shape):
    return lax.broadcasted_iota(jnp.int32, shape, 0) < FOX_HEAD_DIM


def _fox_aug(a, b, sign_a, sign_b, piece_entry, ones_entry, name):
    t = a.shape[0]
    tb = 512

    def body(a_ref, b_ref, o_ref):
        x = sign_a * a_ref[...]
        if sign_b != 0.0:
            x = x + sign_b * b_ref[...]
        head = lax.broadcasted_iota(jnp.int32, (LANES, D_MODEL), 0)
        col = lax.broadcasted_iota(jnp.int32, (LANES, D_MODEL), 1)
        base = (head // 2) * LANES + (1 - head % 2) * FOX_HEAD_DIM + piece_entry
        acc = jnp.zeros((tb, D_MODEL), F32)
        for e, piece in enumerate(_split3(x)):
            place = jnp.where((head < FOX_HEADS) & (col == base + e), 1.0, 0.0).astype(BF16)
            acc = acc + _dot(piece, place, NN)
        if ones_entry is not None:
            ent = lax.broadcasted_iota(jnp.int32, (1, D_MODEL), 1) % FOX_HEAD_DIM
            acc = acc + jnp.where((ent >= ones_entry) & (ent < ones_entry + FOX_AUG), 1.0, 0.0)
        o_ref[...] = acc.astype(BF16)

    blk = pl.BlockSpec((tb, LANES), lambda i: (i, 0))
    return _pcall(
        body, name=name, grid=(t // tb,), in_specs=[blk, blk],
        out_specs=pl.BlockSpec((tb, D_MODEL), lambda i: (i, 0)), out_shape=SDS((t, D_MODEL), BF16),
        compiler_params=_params(),
    )(a, b)


def _causal_steps(nq, key_major):
    if key_major:
        pairs = [(i, j) for j in range(nq) for i in range(j, nq)]
    else:
        pairs = [(i, j) for i in range(nq) for j in range(i + 1)]
    return (jnp.asarray([p[0] for p in pairs], jnp.int32), jnp.asarray([p[1] for p in pairs], jnp.int32))


def _pair_operand(low, own, other, hh):
    return jnp.where(low, own, other) if hh == 0 else jnp.where(low, other, own)


def _fox_fwd(proj, qaug, kaug):
    t = proj.shape[0]
    tq = tk = min(FOX_TQ_FWD, t)
    nq = t // tq
    cb = D_MODEL // LANES
    half = min(FOX_CHAIN, tq)

    i_tab, j_tab = _causal_steps(nq, key_major=False)

    def body(i_ref, j_ref, q_ref, k_ref, v_ref, z_ref, qa_ref, ka_ref, o_ref, g_ref, lse_ref, m_s, l_s, acc_s):
        step = pl.program_id(1)
        i, j = i_ref[step], j_ref[step]

        @pl.when(j == 0)
        def _():
            m_s[...] = jnp.full_like(m_s, NEG_INF)
            l_s[...] = jnp.zeros_like(l_s)
            acc_s[...] = jnp.zeros_like(acc_s)

        low = _low_lanes((tq, LANES))
        top = _top_rows((LANES, tq))

        def update(masked):
            qs = q_ref[...] * FOX_SCALE
            qa, k, ka, v = qa_ref[...], k_ref[...], ka_ref[...], v_ref[...]
            if masked:
                causal = (lax.broadcasted_iota(jnp.int32, (tk, tq), 0) <= lax.broadcasted_iota(jnp.int32, (tk, tq), 1))
            one = jnp.ones_like(v)
            chains = [(hh, slice(c * half, (c + 1) * half)) for hh in range(2) for c in range(tq // half)]
            qh = [_pair_operand(low, qs, qa, hh) for hh in range(2)]
            kh = [_pair_operand(low, k, ka, hh) for hh in range(2)]
            vh = [_pair_operand(low, v, one, hh) for hh in range(2)]
            keys = lambda cols: slice(0, cols.stop) if masked else slice(None)
            scores = [_dot(kh[hh][keys(cols), :], qh[hh][cols, :], NT) for hh, cols in chains]
            for (hh, cols), s in zip(chains, scores):
                if masked:
                    s = jnp.where(causal[keys(cols), cols], s, NEG_INF)
                m_prev = m_s[hh, :, cols]
                m_new = jnp.maximum(m_prev, jnp.max(s, axis=0, keepdims=True))
                alpha = jnp.exp(m_prev - m_new)
                pv = _dot(vh[hh][keys(cols), :], jnp.exp(s - m_new).astype(BF16), TN)
                sums = pv[FOX_HEAD_DIM:FOX_HEAD_DIM + FOX_SUM_ROWS, :] if hh == 0 else pv[0:FOX_SUM_ROWS, :]
                l_s[hh, :, cols] = alpha * l_s[hh, :, cols] + sums
                m_s[hh, :, cols] = m_new
                own = top[:, cols] if hh == 0 else jnp.logical_not(top[:, cols])
                acc_s[:, cols] = jnp.where(own, acc_s[:, cols] * alpha + pv, acc_s[:, cols])

        pl.when(j < i)(functools.partial(update, False))
        pl.when(j == i)(functools.partial(update, True))

        @pl.when(j == i)
        def _():
            o = (acc_s[...] / jnp.where(top, l_s[0, 0:1, :], l_s[1, 0:1, :])).T
            z = z_ref[...].astype(F32)
            o_ref[...] = o.astype(BF16)
            g_ref[...] = (o * (z * _sigmoid(z))).astype(BF16)
            for hh in range(2):
                lse_ref[hh] = m_s[hh] + jnp.log(l_s[hh, 0:1, :])

    qblk = lambda col: pl.BlockSpec((tq, LANES), lambda h, s, it, jt: (it[s], col + h))
    kblk = lambda col: pl.BlockSpec((tk, LANES), lambda h, s, it, jt: (jt[s], col + h))
    return _pcall(
        body, name="fox_attn_fwd",
        grid_spec=pltpu.PrefetchScalarGridSpec(
            num_scalar_prefetch=2, grid=(FOX_PAIRS, i_tab.shape[0]),
            in_specs=[qblk(0), kblk(cb), kblk(2 * cb), qblk(3 * cb), qblk(0), kblk(0)],
            out_specs=[qblk(0), qblk(0), pl.BlockSpec((2, 1, tq), lambda h, s, it, jt: (h, 0, it[s]))],
            scratch_shapes=[pltpu.VMEM((2, 1, tq), F32), pltpu.VMEM((2, FOX_SUM_ROWS, tq), F32),
                            pltpu.VMEM((LANES, tq), F32)]),
        out_shape=[SDS((t, D_MODEL), BF16), SDS((t, D_MODEL), BF16), SDS((FOX_HEADS, 1, t), F32)],
        compiler_params=_params(dimension_semantics=("arbitrary", "arbitrary")),
    )(i_tab, j_tab, proj, proj, proj, proj, qaug, kaug)


def _fox_bwd(proj, do, qaug, kaug, doaug):
    t = proj.shape[0]
    tq = tk = min(FOX_TQ, t)
    nq = t // tq
    cb = D_MODEL // LANES
    half = min(FOX_CHAIN, tq)

    i_tab, j_tab = _causal_steps(nq, key_major=True)

    def body(i_ref, j_ref, q_ref, k_ref, v_ref, do_ref, qa_ref, ka_ref, da_ref,
             dq_ref, dk_ref, dv_ref, dck_ref, dcq_ref, dq_acc, dcq_acc, dk_acc, dks_acc, dv_acc):
        step = pl.program_id(1)
        i, j = i_ref[step], j_ref[step]
        low = _low_lanes((tq, LANES))
        top = _top_rows((LANES, tq))

        @pl.when(i == j)
        def _():
            dk_acc[...] = jnp.zeros_like(dk_acc)
            dks_acc[...] = jnp.zeros_like(dks_acc)
            dv_acc[...] = jnp.zeros_like(dv_acc)

        def update(masked):
            qs = q_ref[...] * FOX_SCALE
            k, v, dout = k_ref[...], v_ref[...], do_ref[...]
            qa, ka, da = qa_ref[...], ka_ref[...], da_ref[...]
            lane = lax.broadcasted_iota(jnp.int32, (tk, LANES), 1)
            vone = jnp.where((lane & (FOX_HEAD_DIM - 1)) < FOX_AUG, 1.0, 0.0).astype(v.dtype)
            one = jnp.ones_like(k)
            if masked:
                causal = (lax.broadcasted_iota(jnp.int32, (tk, tq), 0) <= lax.broadcasted_iota(jnp.int32, (tk, tq), 1))
            parts = []
            kh = [_pair_operand(low, k, ka, hh) for hh in range(2)]
            qh = [_pair_operand(low, qs, qa, hh) for hh in range(2)]
            vh = [_pair_operand(low, v, vone, hh) for hh in range(2)]
            doh = [_pair_operand(low, dout, da, hh) for hh in range(2)]
            q1 = [_pair_operand(low, qs, one, hh) for hh in range(2)]
            k1 = [_pair_operand(low, k, one, hh) for hh in range(2)]

            def tile(hh, keys, cols, s, dp):
                if masked:
                    s = jnp.where(causal[keys, cols], s, NEG_INF)
                p = jnp.exp(s)
                pb, dsb = p.astype(BF16), (p * dp).astype(BF16)
                return (_dot(pb, dout[cols, :], NN), _dot(dsb, q1[hh][cols, :], NN), _dot(k1[hh][keys, :], dsb, TN))

            if not masked:
                scores = [_dot(kh[hh], qh[hh], NT) for hh in range(2)]
                dps = [_dot(vh[hh], doh[hh], NT) for hh in range(2)]
                everything = slice(None)
                parts = [tile(hh, everything, everything, scores[hh], dps[hh]) for hh in range(2)]
            else:
                for hh in range(2):
                    dv_h, dk_h, dq_h = jnp.zeros((tk, LANES), F32), jnp.zeros((tk, LANES), F32), []
                    for c in range(tq // half):
                        cols, keys = slice(c * half, (c + 1) * half), slice(0, (c + 1) * half)
                        dv_c, dk_c, dq_c = tile(hh, keys, cols, _dot(kh[hh][keys, :], qh[hh][cols, :], NT),
                                                _dot(vh[hh][keys, :], doh[hh][cols, :], NT))
                        below = ((0, tk - keys.stop), (0, 0))
                        dv_h, dk_h = dv_h + jnp.pad(dv_c, below), dk_h + jnp.pad(dk_c, below)
                        dq_h.append(dq_c)
                    parts.append((dv_h, dk_h, jnp.concatenate(dq_h, axis=1)))
            dv_acc[...] += jnp.where(low, parts[0][0], parts[1][0])
            dk_acc[...] += jnp.where(low, parts[0][1], parts[1][1])
            dks_acc[...] += jnp.where(low, parts[1][1], parts[0][1])
            dq_t = jnp.where(top, parts[0][2], parts[1][2]) * FOX_SCALE
            sum_a = parts[0][2][FOX_HEAD_DIM:FOX_HEAD_DIM + FOX_SUM_ROWS, :]
            sum_b = parts[1][2][0:FOX_SUM_ROWS, :]

            @pl.when(j == 0)
            def _():
                dq_acc[i] = dq_t
                dcq_acc[0, i] = sum_a
                dcq_acc[1, i] = sum_b

            @pl.when(j > 0)
            def _():
                dq_acc[i] += dq_t
                dcq_acc[0, i] += sum_a
                dcq_acc[1, i] += sum_b

        pl.when(i > j)(functools.partial(update, False))
        pl.when(i == j)(functools.partial(update, True))

        @pl.when(i == nq - 1)
        def _():
            dk_ref[...] = dk_acc[...].astype(BF16)
            dv_ref[...] = dv_acc[...].astype(BF16)
            dck_ref[...] = dks_acc[...]

        @pl.when((i == nq - 1) & (j == nq - 1))
        def _():
            for blk in range(nq):
                dq_ref[blk * tq:(blk + 1) * tq, :] = dq_acc[blk].T.astype(BF16)
            dcq_ref[...] = dcq_acc[...]

    qblk = lambda col: pl.BlockSpec((tq, LANES), lambda h, s, it, jt: (it[s], col + h))
    kblk = lambda col: pl.BlockSpec((tk, LANES), lambda h, s, it, jt: (jt[s], col + h))
    return _pcall(
        body, name="fox_attn_bwd",
        grid_spec=pltpu.PrefetchScalarGridSpec(
            num_scalar_prefetch=2, grid=(FOX_PAIRS, i_tab.shape[0]),
            in_specs=[qblk(0), kblk(cb), kblk(2 * cb), qblk(0), qblk(0), kblk(0), qblk(0)],
            out_specs=[pl.BlockSpec((t, LANES), lambda h, s, it, jt: (0, h)), kblk(0), kblk(0), kblk(0),
                       pl.BlockSpec((2, nq, FOX_SUM_ROWS, tq), lambda h, s, it, jt: (h, 0, 0, 0))],
            scratch_shapes=[pltpu.VMEM((nq, LANES, tq), F32), pltpu.VMEM((2, nq, FOX_SUM_ROWS, tq), F32),
                            pltpu.VMEM((tk, LANES), F32), pltpu.VMEM((tk, LANES), F32), pltpu.VMEM((tk, LANES), F32)]),
        out_shape=[SDS((t, D_MODEL), BF16), SDS((t, D_MODEL), BF16), SDS((t, D_MODEL), BF16),
                   SDS((t, D_MODEL), F32), SDS((FOX_HEADS, nq, FOX_SUM_ROWS, tq), F32)],
        compiler_params=_params(dimension_semantics=("arbitrary", "arbitrary")),
    )(i_tab, j_tab, proj, proj, proj, do, qaug, kaug, doaug)


def _mm_gate_bwd(dh, w_out, z_src, z_col0, o, heads, name):
    t = dh.shape[0]
    tm = 512
    row = lambda i, j, k: (i, 0)
    zcb = z_col0 // D_MODEL
    per_block = heads == D_MODEL // LANES

    def epilogue(acc, e_refs, o_refs, i):
        z = e_refs[0][...].astype(F32)
        ov = e_refs[1][...].astype(F32)
        sg = _sigmoid(z)
        dout = acc * (z * sg)
        o_refs[0][...] = dout.astype(BF16)
        o_refs[1][...] = (acc * ov * (sg * (1.0 + z * (1.0 - sg)))).astype(BF16)
        prod = dout * ov
        lane = lax.broadcasted_iota(jnp.int32, (tm, LANES), 1)
        cols = jnp.zeros((tm, LANES), F32)
        for cbk in range(D_MODEL // LANES):
            seg = prod[:, cbk * LANES:(cbk + 1) * LANES]
            tot = jnp.sum(seg, axis=-1, keepdims=True)
            if per_block:
                o_refs[2][cbk] = tot
            else:
                lo = jnp.sum(jnp.where(_low_lanes(seg.shape), seg, 0.0), axis=-1, keepdims=True)
                cols = jnp.where(lane == 2 * cbk, lo, jnp.where(lane == 2 * cbk + 1, tot - lo, cols))
        if not per_block:
            o_refs[2][...] = cols

    delta_out = ((SDS((heads, t, 1), F32), (heads, tm, 1), lambda i, j, k: (0, i, 0)) if per_block
                 else (SDS((t, LANES), F32), (tm, LANES), row))
    return _mm([dh], w_out, "nt", name, tiles=(tm, D_MODEL, D_MODEL),
               extras=[(z_src, (tm, D_MODEL), lambda i, j, k: (i, zcb)), (o, (tm, D_MODEL), row)],
               outs=[(SDS((t, D_MODEL), BF16), (tm, D_MODEL), row), (SDS((t, D_MODEL), BF16), (tm, D_MODEL), row),
                     delta_out],
               epilogue=epilogue)


def _ple_fwd(h, pin, w_up, w_gate, name):
    t = h.shape[0]
    tm = 512
    pd = pin.shape[1]

    def body(h_ref, p_ref, wu_ref, wg_ref, hn_ref, u_ref, a_ref):
        h = h_ref[...]
        u = _dot(p_ref[...].astype(BF16), wu_ref[...], NN)
        a = _dot(h.astype(BF16), wg_ref[...], NN)
        hn_ref[...] = h + u * _sigmoid(a)
        u_ref[...] = u.astype(BF16)
        a_ref[...] = a.astype(BF16)

    rows = pl.BlockSpec((tm, D_MODEL), lambda i: (i, 0))
    return _pcall(
        body, name=name, grid=(t // tm,),
        in_specs=[rows, pl.BlockSpec((tm, pd), lambda i: (i, 0)),
                  pl.BlockSpec((pd, D_MODEL), lambda i: (0, 0)), pl.BlockSpec((D_MODEL, D_MODEL), lambda i: (0, 0))],
        out_specs=[rows, rows, rows],
        out_shape=[SDS((t, D_MODEL), F32), SDS((t, D_MODEL), BF16), SDS((t, D_MODEL), BF16)],
        compiler_params=_params(),
    )(h, pin, w_up, w_gate)


def _ple_bwd_elem(dh, u, a, name):
    t = dh.shape[0]
    tm = 512

    def body(dh_ref, u_ref, a_ref, du_ref, da_ref):
        g = dh_ref[...]
        s = _sigmoid(a_ref[...].astype(F32))
        du_ref[...] = (g * s).astype(BF16)
        da_ref[...] = (g * u_ref[...].astype(F32) * (s * (1.0 - s))).astype(BF16)

    blk = pl.BlockSpec((tm, D_MODEL), lambda i: (i, 0))
    return _pcall(
        body, name=name, grid=(t // tm,), in_specs=[blk, blk, blk], out_specs=[blk, blk],
        out_shape=[SDS((t, D_MODEL), BF16), SDS((t, D_MODEL), BF16)], compiler_params=_params(),
    )(dh, u, a)


DIL_SCALE = LANES ** -0.5


def _dil_masks():
    ii = lax.broadcasted_iota(jnp.int32, (DIL_BLOCK, DIL_BLOCK), 0)
    jj = lax.broadcasted_iota(jnp.int32, (DIL_BLOCK, DIL_BLOCK), 1)
    return ii, jj


DIL_UNITS = 16
BNT = (((2,), (2,)), ((0,), (0,)))
BNN = (((2,), (1,)), ((0,), (0,)))
BTN = (((1,), (1,)), ((0,), (0,)))


def _dil_units(dil):
    return [(b, r) for b in range(DIL_UNITS // dil) for r in range(dil)]


DIL_MAX_STRIDE = 4


def _pre(dil):
    return max(dil // DIL_MAX_STRIDE, 1)


def _stage_in(src, dst, staging, dil, lead=()):
    n, pre = dst.shape[0], _pre(dil)
    if pre == 1:
        dst[...] = src[lead + (slice(None), slice(None))].astype(F32)
        return
    if src.dtype == F32:
        staging = src
    else:
        staging[...] = src[...].astype(F32)
    for q in range(pre):
        dst[q * (n // pre):(q + 1) * (n // pre), :] = staging[lead + (pl.ds(q, n // pre, stride=pre), slice(None))]


def _stage_out(src, dst, staging, dil, lead=()):
    n, pre = src.shape[0], _pre(dil)
    if pre == 1:
        dst[lead + (slice(None), slice(None))] = src[...].astype(dst.dtype)
        return
    out = dst if dst.dtype == F32 else staging
    for q in range(pre):
        out[(lead if dst.dtype == F32 else ()) + (pl.ds(q, n // pre, stride=pre), slice(None))] = \
            src[q * (n // pre):(q + 1) * (n // pre), :]
    if dst.dtype != F32:
        dst[...] = staging[...].astype(dst.dtype)


def _unit_rows(b, r, dil, n):
    pre = _pre(dil)
    return pl.ds((r % pre) * (n // pre) + (b * DIL_BLOCK * dil + r) // pre, DIL_BLOCK, stride=dil // pre)


def _gather_units(cur, dil, shift=0, edge=None):
    nbk = DIL_UNITS // dil
    parts = []
    for b, r in _dil_units(dil):
        bb = b + shift
        if 0 <= bb < nbk:
            parts.append(cur[_unit_rows(bb, r, dil, cur.shape[0]), :])
        else:
            parts.append(edge[_unit_rows(0, r, dil, edge.shape[0]), :])
    return jnp.stack(parts)


def _scatter_units(dst, val, dil):
    for u, (b, r) in enumerate(_dil_units(dil)):
        dst[_unit_rows(b, r, dil, dst.shape[0]), :] = val[u]


def _dil_bias(slope, prev):
    ii, jj = _dil_masks()
    dist = (DIL_BLOCK + ii - jj) if prev else (ii - jj)
    return (slope * dist.astype(F32))[None], ((jj >= ii) if prev else (jj <= ii))[None]


def _dil_fwd(proj, slopes, grp, dil, name):
    t = proj.shape[0]
    rows = DIL_BLOCK * DIL_UNITS
    edge_rows = DIL_BLOCK * dil
    nbk = DIL_UNITS // dil
    nsb = t // rows
    qc, kc_, vc_ = grp * DIL_HEADS, 3 * DIL_HEADS + grp * DIL_HEADS, 6 * DIL_HEADS + grp * DIL_HEADS

    def body(q_ref, kp_ref, kc_ref, vp_ref, vc_ref, sl_ref, o_ref, lse_ref, qf, kpf, kcf, vpf, vcf, of, lf, staging,
             lnat):
        m, h = pl.program_id(0), pl.program_id(1)
        for src, dst in ((q_ref, qf), (kp_ref, kpf), (kc_ref, kcf), (vp_ref, vpf), (vc_ref, vcf)):
            _stage_in(src, dst, staging, dil)
        slope = sl_ref[0]
        unit = lax.broadcasted_iota(jnp.int32, (DIL_UNITS, 1, 1), 0)
        has_prev = (unit >= dil) | (m > 0)
        q = _gather_units(qf, dil).astype(BF16)
        kc, vc = _gather_units(kcf, dil).astype(BF16), _gather_units(vcf, dil).astype(BF16)
        kp, vp = _gather_units(kcf, dil, -1, kpf).astype(BF16), _gather_units(vcf, dil, -1, vpf).astype(BF16)
        bias_p, ok_p = _dil_bias(slope, True)
        bias_c, ok_c = _dil_bias(slope, False)
        sp = jnp.where(ok_p & has_prev, _dot(q, kp, BNT) * DIL_SCALE - bias_p, NEG_INF)
        sc = jnp.where(ok_c, _dot(q, kc, BNT) * DIL_SCALE - bias_c, NEG_INF)
        mx = jnp.maximum(jnp.max(sp, axis=-1, keepdims=True), jnp.max(sc, axis=-1, keepdims=True))
        pp = jnp.exp(sp - mx)
        pc = jnp.exp(sc - mx)
        l = jnp.sum(pp, axis=-1, keepdims=True) + jnp.sum(pc, axis=-1, keepdims=True)
        o = (_dot(pp.astype(BF16), vp, BNN) + _dot(pc.astype(BF16), vc, BNN)) / l
        _scatter_units(of, o, dil)
        _scatter_units(lf, mx + jnp.log(l), dil)
        _stage_out(of, o_ref, staging, dil)
        _stage_out(lf, lnat, staging, dil)

        @pl.when(h == 0)
        def _():
            lse_ref[...] = jnp.zeros_like(lse_ref)

        lane = lax.broadcasted_iota(jnp.int32, (rows, LANES), 1)
        lse_ref[...] = jnp.where(lane == h, lnat[...], lse_ref[...])

    cur = lambda col: pl.BlockSpec((rows, LANES), lambda m, h: (m, col + h))
    prev = lambda col: pl.BlockSpec((edge_rows, LANES), lambda m, h: (jnp.maximum(m * nbk - 1, 0), col + h))
    return _pcall(
        body, name=name, grid=(nsb, DIL_HEADS),
        in_specs=[cur(qc), prev(kc_), cur(kc_), prev(vc_), cur(vc_), pl.BlockSpec((1, 1, 1), lambda m, h: (h, 0, 0))],
        out_specs=[pl.BlockSpec((rows, LANES), lambda m, h: (m, h)), pl.BlockSpec((rows, LANES), lambda m, h: (m, 0))],
        out_shape=[SDS((t, D_MODEL), BF16), SDS((t, LANES), F32)],
        scratch_shapes=[pltpu.VMEM((rows, LANES), F32), pltpu.VMEM((edge_rows, LANES), F32), pltpu.VMEM((rows, LANES), F32),
                        pltpu.VMEM((edge_rows, LANES), F32), pltpu.VMEM((rows, LANES), F32), pltpu.VMEM((rows, LANES), F32),
                        pltpu.VMEM((rows, 1), F32), pltpu.VMEM((rows, LANES), F32), pltpu.VMEM((rows, 1), F32)],
        compiler_params=_params(dimension_semantics=("arbitrary", "arbitrary")),
    )(proj, proj, proj, proj, proj, slopes)


def _dil_mix(outs, lses, proj, z_col0):
    t = proj.shape[0]
    tm = 512
    zcb = z_col0 // D_MODEL
    ng = len(outs)

    def body(*refs):
        o_refs, l_refs, z_ref = refs[:ng], refs[ng:2 * ng], refs[2 * ng]
        om_ref, g_ref, lse_ref = refs[2 * ng + 1:]
        ls = [r[...] for r in l_refs]
        mx = functools.reduce(jnp.maximum, ls)
        es = [jnp.exp(l - mx) for l in ls]
        tot = functools.reduce(jnp.add, es)
        head = lax.broadcasted_iota(jnp.int32, (LANES, D_MODEL), 0)
        col = lax.broadcasted_iota(jnp.int32, (LANES, D_MODEL), 1)
        spread = jnp.where((head < DIL_HEADS) & (col // LANES == head), 1.0, 0.0).astype(BF16)
        widen = lambda w: sum(_dot(piece, spread, NN) for piece in _split3(w))
        o = functools.reduce(jnp.add, [widen(e / tot) * r[...].astype(F32) for e, r in zip(es, o_refs)])
        z = z_ref[...].astype(F32)
        om_ref[...] = o.astype(BF16)
        g_ref[...] = (o * (z * _sigmoid(z))).astype(BF16)
        joint = mx + jnp.log(tot)
        lane = lax.broadcasted_iota(jnp.int32, joint.shape, 1)
        for hd in range(DIL_HEADS):
            lse_ref[hd] = jnp.sum(jnp.where(lane == hd, joint, 0.0), axis=-1, keepdims=True)

    rows = pl.BlockSpec((tm, D_MODEL), lambda i: (i, 0))
    lanes = pl.BlockSpec((tm, LANES), lambda i: (i, 0))
    return _pcall(
        body, name="dil_mix", grid=(t // tm,),
        in_specs=[rows] * ng + [lanes] * ng + [pl.BlockSpec((tm, D_MODEL), lambda i: (i, zcb))],
        out_specs=[rows, rows, pl.BlockSpec((DIL_HEADS, tm, 1), lambda i: (0, i, 0))],
        out_shape=[SDS((t, D_MODEL), BF16), SDS((t, D_MODEL), BF16), SDS((DIL_HEADS, t, 1), F32)],
        compiler_params=_params(),
    )(*outs, *lses, proj)


def _dil_bwd(proj, do, lse, delta, slopes, grp, dil, name):
    t = proj.shape[0]
    rows = DIL_BLOCK * DIL_UNITS
    edge_rows = DIL_BLOCK * dil
    nbk = DIL_UNITS // dil
    nsb = t // rows
    last_edge = t // edge_rows - 1
    qc, kc_, vc_ = grp * DIL_HEADS, 3 * DIL_HEADS + grp * DIL_HEADS, 6 * DIL_HEADS + grp * DIL_HEADS

    def body(q_ref, qn_ref, kp_ref, kc_ref, vp_ref, vc_ref, do_ref, don_ref, l_ref, ln_ref, d_ref, dn_ref, sl_ref,
             dqkv_ref, qf, qnf, kpf, kcf, vpf, vcf, dof, donf, dqf, dkf, dvf, staging, lf, lnf, df, dnf):
        m = pl.program_id(1)
        for src, dst in ((q_ref, qf), (qn_ref, qnf), (kp_ref, kpf), (kc_ref, kcf), (vp_ref, vpf), (vc_ref, vcf),
                         (do_ref, dof), (don_ref, donf)):
            _stage_in(src, dst, staging, dil)
        for src, dst in ((l_ref, lf), (ln_ref, lnf), (d_ref, df), (dn_ref, dnf)):
            _stage_in(src, dst, staging, dil, lead=(0,))
        slope = sl_ref[0]
        unit = lax.broadcasted_iota(jnp.int32, (DIL_UNITS, 1, 1), 0)
        has_prev = (unit >= dil) | (m > 0)
        has_next = (unit < DIL_UNITS - dil) | (m < nsb - 1)
        b16 = lambda x: x.astype(BF16)
        q, kc, vc, dout = (b16(_gather_units(x, dil)) for x in (qf, kcf, vcf, dof))
        kp, vp = b16(_gather_units(kcf, dil, -1, kpf)), b16(_gather_units(vcf, dil, -1, vpf))
        qn, don = b16(_gather_units(qf, dil, 1, qnf)), b16(_gather_units(dof, dil, 1, donf))
        lrow, drow = _gather_units(lf, dil), _gather_units(df, dil)
        lnrow, dnrow = _gather_units(lf, dil, 1, lnf), _gather_units(df, dil, 1, dnf)
        bias_p, ok_p = _dil_bias(slope, True)
        bias_c, ok_c = _dil_bias(slope, False)
        sp = jnp.where(ok_p & has_prev, _dot(q, kp, BNT) * DIL_SCALE - bias_p, NEG_INF)
        sc = jnp.where(ok_c, _dot(q, kc, BNT) * DIL_SCALE - bias_c, NEG_INF)
        pp = jnp.exp(sp - lrow)
        pc = jnp.exp(sc - lrow)
        dsp = b16(pp * (_dot(dout, vp, BNT) - drow))
        dsc = b16(pc * (_dot(dout, vc, BNT) - drow))
        _scatter_units(dqf, (_dot(dsp, kp, BNN) + _dot(dsc, kc, BNN)) * DIL_SCALE, dil)
        sn = jnp.where(ok_p & has_next, _dot(qn, kc, BNT) * DIL_SCALE - bias_p, NEG_INF)
        pn = jnp.exp(sn - lnrow)
        dsn = b16(pn * (_dot(don, vc, BNT) - dnrow))
        _scatter_units(dkf, (_dot(dsc, q, BTN) + _dot(dsn, qn, BTN)) * DIL_SCALE, dil)
        _scatter_units(dvf, _dot(b16(pc), dout, BTN) + _dot(b16(pn), don, BTN), dil)
        for s, src in enumerate((dqf, dkf, dvf)):
            _stage_out(src, dqkv_ref.at[s], staging, dil)

    prev_i = lambda m: jnp.maximum(m * nbk - 1, 0)
    next_i = lambda m: jnp.minimum((m + 1) * nbk, last_edge)
    cur = lambda col: pl.BlockSpec((rows, LANES), lambda h, m: (m, col + h))
    edge = lambda col, f: pl.BlockSpec((edge_rows, LANES), lambda h, m: (f(m), col + h))
    colcur = pl.BlockSpec((1, rows, 1), lambda h, m: (h, m, 0))
    colnext = pl.BlockSpec((1, edge_rows, 1), lambda h, m: (h, next_i(m), 0))
    out_blk = pl.BlockSpec((3, rows, LANES), lambda h, m: (0, m, h))
    big, small = pltpu.VMEM((rows, LANES), F32), pltpu.VMEM((edge_rows, LANES), F32)
    return _pcall(
        body, name=name, grid=(DIL_HEADS, nsb),
        in_specs=[cur(qc), edge(qc, next_i), edge(kc_, prev_i), cur(kc_), edge(vc_, prev_i), cur(vc_),
                  cur(0), edge(0, next_i), colcur, colnext, colcur, colnext,
                  pl.BlockSpec((1, 1, 1), lambda h, m: (h, 0, 0))],
        out_specs=out_blk, out_shape=SDS((3, t, D_MODEL), BF16),
        scratch_shapes=[big, small, small, big, small, big, big, small, big, big, big, big,
                        pltpu.VMEM((rows, 1), F32), pltpu.VMEM((edge_rows, 1), F32),
                        pltpu.VMEM((rows, 1), F32), pltpu.VMEM((edge_rows, 1), F32)],
        compiler_params=_params(),
    )(proj, proj, proj, proj, proj, proj, do, do, lse, lse, delta, delta, slopes)


def _mesh_pos():
    x, y, c = lax.axis_index("x"), lax.axis_index("y"), lax.axis_index("c")
    return x, y, c


def _peer(pos, k):
    x, y, c = pos
    px = 1 - x if k & 4 else x
    py = 1 - y if k & 2 else y
    pc = 1 - c if k & 1 else c
    return (px, py, pc), 4 * px + 2 * py + pc


N_CHIPS = 4
CHIP_FLIPS = ((1, 0), (0, 1), (1, 1))


def _other_chips(x, y):
    return [(1 - x if fx else x, 1 - y if fy else y) for fx, fy in CHIP_FLIPS]


def _all_gather(arrays):
    n = len(arrays)
    per = 2 * N_CHIPS - 1
    hbm = pl.BlockSpec(memory_space=pltpu.HBM)

    def body(*refs):
        ins, outs = refs[:n], refs[n:2 * n]
        send_sems, recv_sems, local_sems = refs[2 * n:]
        x, y, c = _mesh_pos()
        sibling = (x, y, 1 - c)
        chips = _other_chips(x, y)
        block = lambda px, py, pc: 4 * px + 2 * py + pc

        def copy(w, k, src, blk, to):
            return pltpu.make_async_remote_copy(
                src_ref=src, dst_ref=outs[w].at[blk], send_sem=send_sems.at[w * per + k],
                recv_sem=recv_sems.at[w * per + k], device_id=to, device_id_type=MESH)

        local, started = [], []
        for w in range(n):
            cp = pltpu.make_async_copy(ins[w], outs[w].at[block(x, y, c)], local_sems.at[w])
            cp.start()
            local.append(cp)
            started.append(copy(w, 0, ins[w], block(x, y, c), sibling))
            for j, (px, py) in enumerate(chips):
                started.append(copy(w, 1 + j, ins[w], block(x, y, c), (px, py, c)))
        for cp in started:
            cp.start()
        for j, (px, py) in enumerate(chips):
            for w in range(n):
                copy(w, 1 + j, ins[w], block(px, py, c), sibling).wait_recv()
                cp = copy(w, 4 + j, outs[w].at[block(px, py, c)], block(px, py, c), sibling)
                cp.start()
                started.append(cp)
        for w in range(n):
            copy(w, 0, ins[w], block(x, y, 1 - c), sibling).wait_recv()
            for j, (px, py) in enumerate(chips):
                copy(w, 4 + j, ins[w], block(px, py, 1 - c), sibling).wait_recv()
        for cp in started:
            cp.wait_send()
        for cp in local:
            cp.wait()

    return _pcall(
        body, name="all_gather_weights", in_specs=[hbm] * n, out_specs=[hbm] * n,
        out_shape=[SDS((N_DEV,) + a.shape, a.dtype) for a in arrays],
        scratch_shapes=[pltpu.SemaphoreType.DMA((n * per,)), pltpu.SemaphoreType.DMA((n * per,)),
                        pltpu.SemaphoreType.DMA((n,))],
    )(*arrays)


HBM_SPEC = pl.BlockSpec(memory_space=pltpu.HBM)
SEM_SPEC = pl.BlockSpec(memory_space=pltpu.SEMAPHORE)
DATAFLOW = pltpu.SideEffectType.DATAFLOW_SIDE_EFFECTING


def _push_start(arrays, scatter, name):
    n = len(arrays)
    per = N_DEV - 1

    def body(*refs):
        srcs, lands = refs[:n], refs[n:2 * n]
        send_sems, recv_sems, token = refs[2 * n], refs[2 * n + 1], refs[-1]
        pos = _mesh_pos()
        me = 4 * pos[0] + 2 * pos[1] + pos[2]
        for w in range(n):
            for k in range(1, N_DEV):
                peer, peer_idx = _peer(pos, k)
                pltpu.make_async_remote_copy(
                    src_ref=srcs[w].at[peer_idx] if scatter else srcs[w], dst_ref=lands[w].at[me],
                    send_sem=send_sems.at[w * per + k - 1], recv_sem=recv_sems.at[w * per + k - 1],
                    device_id=peer, device_id_type=MESH).start()
        token[...] = jnp.zeros_like(token)

    land_shapes = [a.shape if scatter else (N_DEV,) + a.shape for a in arrays]
    in_hbm = lambda a: pltpu.with_memory_space_constraint(a, pltpu.HBM)
    lands = [in_hbm(lax.empty(s, a.dtype)) for s, a in zip(land_shapes, arrays)]
    sems = pltpu.SemaphoreType.DMA((n * per,))
    res = _pcall(
        body, name=name,
        out_shape=(sems, sems, *[pltpu.HBM(a.shape, a.dtype) for a in arrays],
                   *[pltpu.HBM(s, a.dtype) for s, a in zip(land_shapes, arrays)], SDS((8, LANES), F32)),
        in_specs=[HBM_SPEC] * (2 * n),
        out_specs=(SEM_SPEC, SEM_SPEC, *[HBM_SPEC] * (2 * n), pl.BlockSpec(memory_space=pltpu.VMEM)),
        input_output_aliases={i: 2 + i for i in range(2 * n)},
        compiler_params=pltpu.CompilerParams(has_side_effects=DATAFLOW),
    )(*[in_hbm(a) for a in arrays], *lands)
    return res[0], res[1], list(res[2:2 + n]), list(res[2 + n:2 + 2 * n]), res[-1]


def _push_wait(send_sems, recv_sems, arrays, lands, after, scatter, name):
    n = len(arrays)
    per = N_DEV - 1

    def body(*refs):
        srcs, lands_ = refs[:n], refs[n:2 * n]
        send_sems_, recv_sems_ = refs[2 * n], refs[2 * n + 1]
        pos = _mesh_pos()
        for w in range(n):
            for k in range(1, N_DEV):
                peer, peer_idx = _peer(pos, k)
                cp = pltpu.make_async_remote_copy(
                    src_ref=srcs[w].at[peer_idx] if scatter else srcs[w], dst_ref=lands_[w].at[peer_idx],
                    send_sem=send_sems_.at[w * per + k - 1], recv_sem=recv_sems_.at[w * per + k - 1],
                    device_id=peer, device_id_type=MESH)
                cp.wait_send()
                cp.wait_recv()

    res = _pcall(
        body, name=name,
        out_shape=(*[pltpu.HBM(a.shape, a.dtype) for a in arrays], *[pltpu.HBM(l.shape, l.dtype) for l in lands]),
        in_specs=[HBM_SPEC] * (2 * n) + [SEM_SPEC, SEM_SPEC, pl.BlockSpec(memory_space=pl.ANY)],
        out_specs=[HBM_SPEC] * (2 * n), input_output_aliases={i: i for i in range(2 * n)},
        compiler_params=pltpu.CompilerParams(has_side_effects=DATAFLOW),
    )(*arrays, *lands, send_sems, recv_sems, after)
    return list(res[n:])


def _fill_own(land, own):
    me = 4 * lax.axis_index("x") + 2 * lax.axis_index("y") + lax.axis_index("c")
    return lax.dynamic_update_slice(land, own[None], (me,) + (0,) * own.ndim)


def _adam_math(w, g, m, v):
    m = ADAM_B1 * m + (1.0 - ADAM_B1) * g
    v = ADAM_B2 * v + (1.0 - ADAM_B2) * (g * g)
    m_hat = m / (1.0 - ADAM_B1 ** ADAM_STEP)
    v_hat = v / (1.0 - ADAM_B2 ** ADAM_STEP)
    delta = -ADAM_LR * (m_hat / (jnp.sqrt(v_hat) + ADAM_EPS) + ADAM_WD * w)
    return delta, m, v


def _adamw(recv, w, m, v, name):
    n_parts, r, c = recv.shape
    layers, rows_per_layer, _ = w.shape
    tr = min(rows_per_layer, 128)
    per_layer = rows_per_layer // tr

    def body(g_ref, w_ref, m_ref, v_ref, go_ref, d_ref, mo_ref, vo_ref):
        g = g_ref[0].astype(F32)
        for s in range(1, n_parts):
            g = g + g_ref[s].astype(F32)
        delta, mn, vn = _adam_math(w_ref[...], g, m_ref[...], v_ref[...])
        go_ref[...] = g
        d_ref[...] = delta
        mo_ref[...] = mn
        vo_ref[...] = vn

    blk = pl.BlockSpec((None, tr, c), lambda i: (i // per_layer, i % per_layer, 0))
    return _pcall(
        body, name=name, grid=(r // tr,),
        in_specs=[pl.BlockSpec((n_parts, tr, c), lambda i: (0, i, 0)), blk, blk, blk],
        out_specs=[blk] * 4, out_shape=[SDS(w.shape, F32)] * 4, compiler_params=_params(),
    )(recv, w, m, v)


VEC_ROWS = 32


def _small_allreduce_adamw(vec, w, m, v):
    def body(vec_ref, w_ref, m_ref, v_ref, g_ref, d_ref, mo_ref, vo_ref, gath, send_sems, recv_sems):
        pos = _mesh_pos()
        me = 4 * pos[0] + 2 * pos[1] + pos[2]
        sends, recvs = [], []
        for k in range(1, N_DEV):
            peer, peer_idx = _peer(pos, k)
            cp = pltpu.make_async_remote_copy(src_ref=vec_ref, dst_ref=gath.at[me], send_sem=send_sems.at[k - 1],
                                              recv_sem=recv_sems.at[k - 1], device_id=peer, device_id_type=MESH)
            cp.start()
            sends.append(cp)
            recvs.append(pltpu.make_async_remote_copy(src_ref=vec_ref, dst_ref=gath.at[peer_idx],
                                                      send_sem=send_sems.at[k - 1], recv_sem=recv_sems.at[k - 1],
                                                      device_id=peer, device_id_type=MESH))
        gath[me] = vec_ref[...]
        for cp in recvs:
            cp.wait_recv()
        for cp in sends:
            cp.wait_send()
        tot = gath[0]
        for s in range(1, N_DEV):
            tot = tot + gath[s]
        rowi = lax.broadcasted_iota(jnp.int32, (8, LANES), 0)
        mine = jnp.sum(jnp.where(rowi == me, tot[16:24, :], 0.0), axis=0, keepdims=True)
        g = jnp.concatenate([tot[0:16, :], jnp.broadcast_to(mine, (8, LANES)), tot[24:32, :]], axis=0)
        delta, mn, vn = _adam_math(w_ref[...], g, m_ref[...], v_ref[...])
        g_ref[...] = g
        d_ref[...] = delta
        mo_ref[...] = mn
        vo_ref[...] = vn

    vm = pl.BlockSpec(memory_space=pltpu.VMEM)
    return _pcall(
        body, name="small_allreduce_adamw", in_specs=[vm] * 4, out_specs=[vm] * 4,
        out_shape=[SDS((VEC_ROWS, LANES), F32)] * 4,
        scratch_shapes=[pltpu.VMEM((N_DEV, VEC_ROWS, LANES), F32), pltpu.SemaphoreType.DMA((N_DEV - 1,)),
                        pltpu.SemaphoreType.DMA((N_DEV - 1,))],
        compiler_params=pltpu.CompilerParams(has_side_effects=True),
    )(vec, w, m, v)


def _cols_to_slabs(a):
    r, c8 = a.shape
    return a.reshape(r, N_DEV, c8 // N_DEV).transpose(1, 0, 2)


def _slabs_to_cols(a):
    n, r, c = a.shape
    return a.transpose(1, 0, 2).reshape(r, n * c)


def _rows8(vec):
    return vec.reshape(-1, LANES)


def _pad_rows(a, rows):
    return jnp.pad(a, ((0, rows - a.shape[0]), (0, LANES - a.shape[1])))


def kernel(x, p, fox_norm, fox_w_in, fox_b_f, fox_w_out, dil_norm, dil_w_in, dil_w_out, ple_w_up, ple_w_gate, final_norm, loss_target, m_fox_norm, m_fox_w_in, m_fox_b_f, m_fox_w_out, m_dil_norm, m_dil_w_in, m_dil_w_out, m_ple_w_up, m_ple_w_gate, m_final_norm, v_fox_norm, v_fox_w_in, v_fox_b_f, v_fox_w_out, v_dil_norm, v_dil_w_in, v_dil_w_out, v_ple_w_up, v_ple_w_gate, v_final_norm):
    t = x.shape[1]
    d = D_MODEL
    xs, tgt = x[0], loss_target[0]
    p0, p1 = p[0, 0], p[1, 0]
    fox_cols = fox_w_in.shape[2]
    ple_dim = ple_w_up.shape[1]

    later = [dil_w_in[0].astype(BF16), dil_w_out[0].astype(BF16), ple_w_up.reshape(-1, LANES).astype(BF16),
             ple_w_gate.reshape(-1, d).astype(BF16), dil_norm]
    push = _push_start(later, False, "gather_later_start")
    gw = _all_gather([fox_w_in[0].astype(BF16), fox_w_out[0].astype(BF16)])
    w_fox_in = _slabs_to_cols(gw[0])
    w_fox_main = w_fox_in[:, :4 * d]
    w_fox_f = jnp.pad(w_fox_in[:, 4 * d:], ((0, 0), (0, LANES - FOX_HEADS)))
    w_fox_out = gw[1].reshape(d, d)
    b_pad = jnp.pad(fox_b_f, ((0, 0), (0, LANES - FOX_HEADS)))

    n0, r0 = _rms_fwd(xs, fox_norm + push[4][0:1, 0:1], "rms_fox")
    proj0 = _mm([n0], w_fox_main, "nn", "fox_in_proj", tiles=IN_PROJ_TILES)
    projf = _mm([n0], w_fox_f, "nn", "fox_gate_proj", tiles=IN_PROJ_TILES, out_dtype=F32)
    c_all = _fox_gate_fwd(projf, b_pad)
    qaug_fwd = _fox_aug(c_all, c_all, 1.0, 0.0, 0, FOX_AUG, "fox_aug_q_fwd")
    kaug = _fox_aug(c_all, c_all, -1.0, 0.0, FOX_AUG, 0, "fox_aug_k")
    o0, g0, lse0 = _fox_fwd(proj0, qaug_fwd, kaug)
    h1 = _mm_residual(g0, w_fox_out, "nn", xs, "fox_out_proj")

    landed = _push_wait(push[0], push[1], push[2], push[3], h1, False, "gather_later_wait")
    gl = [_fill_own(zone, own) for zone, own in zip(landed, later)]
    w_dil_in = _slabs_to_cols(gl[0])
    w_dil_out = gl[1].reshape(d, d)
    w_up = gl[2].reshape(N_DEV, 2, ple_dim, LANES).transpose(1, 2, 0, 3).reshape(2, ple_dim, d)
    w_gate = gl[3].reshape(N_DEV, 2, d // N_DEV, d).transpose(1, 0, 2, 3).reshape(2, d, d)
    dil_norm_full = gl[4].reshape(1, d)
    h2, u0, a0 = _ple_fwd(h1, p0, w_up[0], w_gate[0], "ple0_fwd")

    n1, r1 = _rms_fwd(h2, dil_norm_full, "rms_dil")
    proj1 = _mm([n1], w_dil_in, "nn", "dil_in_proj", tiles=IN_PROJ_TILES)
    n_heads = len(DIL_PATTERN) * DIL_HEADS
    slopes = 2.0 ** (-ALIBI_MAX_EXP * jnp.arange(1, n_heads + 1, dtype=F32) / n_heads)
    dil_o, dil_lse, dil_slopes = [], [], []
    for grp, (_, dil) in enumerate(DIL_PATTERN):
        sl = (slopes[grp * DIL_HEADS:(grp + 1) * DIL_HEADS] * dil).reshape(DIL_HEADS, 1, 1)
        og, lg = _dil_fwd(proj1, sl, grp, dil, f"dil_attn_fwd_{grp}")
        dil_o.append(og)
        dil_lse.append(lg)
        dil_slopes.append(sl)
    z1_col0 = 9 * d
    o1, g1, lse1 = _dil_mix(dil_o, dil_lse, proj1, z1_col0)
    h3 = _mm_residual(g1, w_dil_out, "nn", h2, "dil_out_proj")
    h4, u1, a1 = _ple_fwd(h3, p1, w_up[1], w_gate[1], "ple1_fwd")

    dh4, d_final_norm, loss_part = _final_bwd(h4, final_norm.reshape(1, d), tgt)

    du1, da1 = _ple_bwd_elem(dh4, u1, a1, "ple1_bwd_elem")
    dw_up1 = _dw(p1, du1, "ple1_dw_up")
    dw_gate1 = _dw(h3, da1, "ple1_dw_gate")
    dh3 = _mm_residual(da1, w_gate[1], "nt", dh4, "ple1_dh")

    dw_dil_out = _dw(g1, dh3, "dil_dw_out")
    do1, dz1, delta1 = _mm_gate_bwd(dh3, w_dil_out, proj1, z1_col0, o1, DIL_HEADS, "dil_dgate")
    n_grp = len(DIL_PATTERN)
    dqkv = [_dil_bwd(proj1, do1, lse1, delta1, dil_slopes[grp], grp, dil, f"dil_attn_bwd_{grp}")
            for grp, (_, dil) in enumerate(DIL_PATTERN)]
    dw_cols = [_dw(n1, dqkv[grp], f"dil_dw_in_{kind}{grp}", sub=kind) for kind in range(3) for grp in range(n_grp)]
    dw_dil_in = jnp.concatenate(dw_cols + [_dw(n1, dz1, "dil_dw_in_z")], axis=1)
    row_slabs = lambda a: a.reshape(N_DEV, a.shape[0] // N_DEV, a.shape[1])
    dil_slabs = [_cols_to_slabs(dw_dil_in), row_slabs(dw_dil_out), _cols_to_slabs(dw_up1), row_slabs(dw_gate1)]
    dil_push = _push_start(dil_slabs, True, "scatter_dil_start")
    group_major = lambda kb: jnp.where(kb < 3 * n_grp, (kb % 3) * n_grp + kb // 3, kb)
    dh2, d_dil_norm = _mm_in_bwd(dqkv + [dz1], w_dil_in, h2, dil_norm_full + dil_push[4][0:1, 0:1], r1, dh3, "dil_dx",
                                 w_kmap=group_major)

    du0, da0 = _ple_bwd_elem(dh2, u0, a0, "ple0_bwd_elem")
    dw_up0 = _dw(p0, du0, "ple0_dw_up")
    dw_gate0 = _dw(h1, da0, "ple0_dw_gate")
    dh1 = _mm_residual(da0, w_gate[0], "nt", dh2, "ple0_dh")

    dw_fox_out = _dw(g0, dh1, "fox_dw_out")
    do0, dz0, delta0 = _mm_gate_bwd(dh1, w_fox_out, proj0, 3 * d, o0, FOX_HEADS, "fox_dgate")
    head_cols = lambda a: jnp.pad(a, ((0, 0), (0, LANES - FOX_HEADS)))
    lse_cols = head_cols(lse0.reshape(FOX_HEADS, t).T)
    qaug_bwd = _fox_aug(c_all, lse_cols, 1.0, -1.0, 0, FOX_AUG, "fox_aug_q_bwd")
    doaug = _fox_aug(delta0, delta0, -1.0, 0.0, 0, None, "fox_aug_do")
    dq0, dk0, dv0, dck_wide, dcq = _fox_bwd(proj0, do0, qaug_bwd, kaug, doaug)
    dc_query = head_cols(dcq[:, :, 0, :].reshape(FOX_HEADS, t).T)
    df, d_b_f = _fox_gate_bwd(projf, b_pad, dc_query, dck_wide)
    dproj0 = [dq0, dk0, dv0, dz0]
    dw_fox_parts = [_dw(n0, dpart, f"fox_dw_in_{s}") for s, dpart in enumerate(dproj0)]
    dw_fox_f = _dw(n0, df, "fox_dw_gate")
    dw_fox_in = jnp.concatenate(dw_fox_parts + [dw_fox_f[:, :FOX_HEADS]], axis=1)
    fox_slabs = [_cols_to_slabs(dw_fox_in), row_slabs(dw_fox_out), _cols_to_slabs(dw_up0), row_slabs(dw_gate0)]
    fox_push = _push_start(fox_slabs, True, "scatter_fox_start")
    grad_x, d_fox_norm = _mm_in_bwd(dproj0, w_fox_main, xs, fox_norm + fox_push[4][0:1, 0:1], r0, dh1, "fox_dx",
                                    more=(df, w_fox_f))

    me = 4 * lax.axis_index("x") + 2 * lax.axis_index("y") + lax.axis_index("c")

    def landed(push, slabs, name):
        zones = _push_wait(push[0], push[1], push[2], push[3], grad_x, True, name)
        return [_fill_own(zone, lax.dynamic_index_in_dim(own, me, 0, keepdims=False)) for zone, own in zip(zones, slabs)]

    g_dil_in, g_dil_out, g_up1, g_gate1 = landed(dil_push, dil_slabs, "scatter_dil_wait")
    g_fox_in, g_fox_out, g_up0, g_gate0 = landed(fox_push, fox_slabs, "scatter_fox_wait")
    upd = {"fox_w_in": _adamw(g_fox_in, fox_w_in, m_fox_w_in, v_fox_w_in, "adamw_fox_w_in"),
           "fox_w_out": _adamw(g_fox_out, fox_w_out, m_fox_w_out, v_fox_w_out, "adamw_fox_w_out"),
           "dil_w_in": _adamw(g_dil_in, dil_w_in, m_dil_w_in, v_dil_w_in, "adamw_dil_w_in"),
           "dil_w_out": _adamw(g_dil_out, dil_w_out, m_dil_w_out, v_dil_w_out, "adamw_dil_w_out")}
    for nm, grads, params in (("ple_w_up", (g_up0, g_up1), (ple_w_up, m_ple_w_up, v_ple_w_up)),
                              ("ple_w_gate", (g_gate0, g_gate1), (ple_w_gate, m_ple_w_gate, v_ple_w_gate))):
        layers = [_adamw(g, *[a[l:l + 1] for a in params], f"adamw_{nm}_{l}") for l, g in enumerate(grads)]
        upd[nm] = [jnp.concatenate([layers[0][k], layers[1][k]], axis=0) for k in range(4)]

    loss_row = jnp.where(jnp.arange(LANES) == 0, loss_part, 0.0)
    vec = jnp.concatenate([_rows8(d_fox_norm), _rows8(d_final_norm), _rows8(d_dil_norm), d_b_f, loss_row,
                           jnp.zeros((VEC_ROWS - 26, LANES), F32)], axis=0)

    def small_pack(a_fox_norm, a_final_norm, a_dil_norm, a_b_f):
        return jnp.concatenate([_rows8(a_fox_norm), _rows8(a_final_norm), _pad_rows(a_dil_norm, 8),
                                _pad_rows(a_b_f, 8)], axis=0)

    sg, sd, sm, sv = _small_allreduce_adamw(
        vec, small_pack(fox_norm, final_norm, dil_norm, fox_b_f),
        small_pack(m_fox_norm, m_final_norm, m_dil_norm, m_fox_b_f),
        small_pack(v_fox_norm, v_final_norm, v_dil_norm, v_fox_b_f))

    def small_unpack(a):
        return {"fox_norm": a[0:8].reshape(1, d), "final_norm": a[8:16].reshape(d), "dil_norm": a[16:17],
                "fox_b_f": a[24:25, :FOX_HEADS]}

    loss = sg[25, 0]
    order = ["fox_norm", "fox_w_in", "fox_b_f", "fox_w_out", "dil_norm", "dil_w_in", "dil_w_out", "ple_w_up",
             "ple_w_gate", "final_norm"]
    out = [loss, grad_x[None]]
    for idx, small in enumerate((sg, sd, sm, sv)):
        sp = small_unpack(small)
        out += [sp[nm] if nm in sp else upd[nm][idx] for nm in order]
    return tuple(out)
```

```python
import functools

import jax
import jax.numpy as jnp
from jax import lax
from jax.experimental import pallas as pl
from jax.experimental.pallas import tpu as pltpu

F32 = jnp.float32
BF16 = jnp.bfloat16
SDS = jax.ShapeDtypeStruct

D_MODEL = 1024
N_DEV = 8
LANES = 128
FOX_HEADS = 16
FOX_HEAD_DIM = 64
FOX_PAIRS = FOX_HEADS // 2
DIL_HEADS = 8
DIL_BLOCK = 128
DIL_PATTERN = ((128, 1), (512, 4), (2048, 16))
ALIBI_MAX_EXP = 8.0
RMS_EPS = 1e-6
ADAM_LR, ADAM_B1, ADAM_B2, ADAM_EPS, ADAM_WD, ADAM_STEP = 0.001, 0.9, 0.999, 1e-08, 0.01, 10
VMEM_LIMIT = 48 * 1024 * 1024
NEG_INF = float("-inf")

NN = (((1,), (0,)), ((), ()))
NT = (((1,), (1,)), ((), ()))
TN = (((0,), (0,)), ((), ()))
MESH = pl.DeviceIdType.MESH


def _pcall(body, **kw):
    return pl.pallas_call(body, **kw)


def _params(**kw):
    return pltpu.CompilerParams(vmem_limit_bytes=VMEM_LIMIT, **kw)


def _dot(a, b, dims):
    return lax.dot_general(a, b, dims, preferred_element_type=F32)


def _sigmoid(x):
    return 1.0 / (1.0 + jnp.exp(-x))


def _mm(a_parts, b, mode, name, tiles=(512, 1024, 1024), extras=(), outs=None, epilogue=None, out_dtype=BF16,
        b_kmap=None, b_sub=None):
    na = len(a_parts)
    stack = [a.shape[0] if a.ndim == 3 else 1 for a in a_parts]
    first = [sum(stack[:s]) for s in range(na)]
    if mode == "tn":
        k_part, m = a_parts[0].shape
        n = b.shape[-1]
    else:
        m, k_part = a_parts[0].shape[-2:]
        n = b.shape[1] if mode == "nn" else b.shape[0]
    tm, tn, tk = min(tiles[0], m), min(tiles[1], n), min(tiles[2], k_part)
    kb = k_part // tk
    nk = sum(stack) * kb
    grid = (m // tm, n // tn, nk)
    b_kmap = b_kmap or (lambda k: k)

    in_specs = []
    for s in range(na):
        if mode == "tn":
            in_specs.append(pl.BlockSpec((tk, tm), lambda i, j, k: (k, i)))
            continue

        def rel(k, s=s):
            return jnp.clip(k - first[s] * kb, 0, stack[s] * kb - 1)

        if a_parts[s].ndim == 3:
            in_specs.append(pl.BlockSpec((None, tm, tk), lambda i, j, k, rel=rel: (rel(k) // kb, i, rel(k) % kb)))
        else:
            in_specs.append(pl.BlockSpec((tm, tk), lambda i, j, k, rel=rel: (i, rel(k))))
    if mode == "nt":
        in_specs.append(pl.BlockSpec((tn, tk), lambda i, j, k: (j, b_kmap(k))))
    elif b_sub is not None:
        in_specs.append(pl.BlockSpec((None, tk, tn), lambda i, j, k: (b_sub, k, j)))
    else:
        in_specs.append(pl.BlockSpec((tk, tn), lambda i, j, k: (b_kmap(k), j)))
    for _, blk, imap in extras:
        in_specs.append(pl.BlockSpec(blk, imap))
    if outs is None:
        outs = [(SDS((m, n), out_dtype), (tm, tn), lambda i, j, k: (i, j))]
    out_specs = [pl.BlockSpec(blk, imap) for _, blk, imap in outs]
    ne, no = len(extras), len(outs)
    dims = {"nn": NN, "nt": NT, "tn": TN}[mode]

    def finish(res, e_refs, o_refs, i):
        if epilogue is None:
            o_refs[0][...] = res.astype(o_refs[0].dtype)
        else:
            epilogue(res, e_refs, o_refs, i)

    def body(*refs):
        a_refs = refs[:na]
        b_ref = refs[na]
        e_refs = refs[na + 1:na + 1 + ne]
        o_refs = refs[na + 1 + ne:na + 1 + ne + no]
        i, k = pl.program_id(0), pl.program_id(2)
        if nk == 1:
            finish(_dot(a_refs[0][...].astype(BF16), b_ref[...].astype(BF16), dims), e_refs, o_refs, i)
            return
        acc = refs[-1]

        @pl.when(k == 0)
        def _():
            acc[...] = jnp.zeros_like(acc)

        def step(a_ref):
            acc[...] += _dot(a_ref[...].astype(BF16), b_ref[...].astype(BF16), dims)

        for s in range(na):
            if na == 1:
                step(a_refs[0])
            else:
                in_use = (k >= first[s] * kb) & (k < (first[s] + stack[s]) * kb)
                pl.when(in_use)(functools.partial(step, a_refs[s]))

        @pl.when(k == nk - 1)
        def _():
            finish(acc[...], e_refs, o_refs, i)

    res = _pcall(
        body, name=name, grid=grid, in_specs=in_specs, out_specs=out_specs,
        out_shape=[o[0] for o in outs], scratch_shapes=[] if nk == 1 else [pltpu.VMEM((tm, tn), F32)],
        compiler_params=_params(dimension_semantics=("arbitrary", "arbitrary", "arbitrary")),
    )(*a_parts, b, *[e[0] for e in extras])
    return res[0] if len(res) == 1 else res


IN_PROJ_TILES = (1024, 1024, 1024)
DW_TILES = (1024, 1024, 1024)


def _dw(x, dy, name, sub=None):
    return _mm([x], dy, "tn", name, tiles=DW_TILES, b_sub=sub)


def _add_extra_epilogue(acc, e_refs, o_refs, i):
    o_refs[0][...] = acc + e_refs[0][...]


def _mm_residual(a, b, mode, res, name):
    m = a.shape[0]
    n = b.shape[1] if mode == "nn" else b.shape[0]
    tm, tn = 512, 1024
    return _mm([a], b, mode, name, tiles=(tm, tn, 1024),
               extras=[(res, (tm, tn), lambda i, j, k: (i, j))],
               outs=[(SDS((m, n), F32), (tm, tn), lambda i, j, k: (i, j))],
               epilogue=_add_extra_epilogue)


def _rms_fwd(h, g, name):
    t, d = h.shape
    tm = 512

    def body(h_ref, g_ref, n_ref, r_ref):
        x = h_ref[...]
        r = lax.rsqrt(jnp.mean(x * x, axis=-1, keepdims=True) + RMS_EPS)
        n_ref[...] = ((x * r) * g_ref[...]).astype(BF16)
        r_ref[...] = r

    return _pcall(
        body, name=name, grid=(t // tm,),
        in_specs=[pl.BlockSpec((tm, d), lambda i: (i, 0)), pl.BlockSpec((1, d), lambda i: (0, 0))],
        out_specs=[pl.BlockSpec((tm, d), lambda i: (i, 0)), pl.BlockSpec((tm, 1), lambda i: (i, 0))],
        out_shape=[SDS((t, d), BF16), SDS((t, 1), F32)],
        compiler_params=_params(),
    )(h, g)


def _rms_bwd_rows(dn, x, g, r):
    xhat = x * r
    dxhat = dn * g
    dx = r * (dxhat - xhat * jnp.mean(dxhat * xhat, axis=-1, keepdims=True))
    dg = jnp.sum(dn * xhat, axis=0, keepdims=True)
    return dx, dg


def _mm_in_bwd(d_parts, w, h, g, r, dres, name, more=None, w_kmap=None, after=None):
    t = h.shape[0]
    tm = 1024
    row = lambda i, j, k: (i, 0)
    extras = []
    if more is not None:
        extras = [(more[0], (tm, more[0].shape[1]), row), (more[1], more[1].shape, lambda i, j, k: (0, 0))]
    if after is not None:
        extras.append((after, after.shape, lambda i, j, k: (0, 0)))

    def epilogue(acc, e_refs, o_refs, i):
        o_refs[0][...] = acc if more is None else acc + _dot(e_refs[0][...], e_refs[1][...], NT)

    dn = _mm(d_parts, w, "nt", name, tiles=(tm, D_MODEL, D_MODEL), extras=extras,
             outs=[(SDS((t, D_MODEL), F32), (tm, D_MODEL), row)], epilogue=epilogue, b_kmap=w_kmap)

    tr = 256

    def body(dn_ref, h_ref, g_ref, r_ref, dres_ref, dh_ref, dg_ref):
        i = pl.program_id(0)
        dx, dg = _rms_bwd_rows(dn_ref[...], h_ref[...], g_ref[...], r_ref[...])
        dh_ref[...] = dres_ref[...] + dx

        @pl.when(i == 0)
        def _():
            dg_ref[...] = dg

        @pl.when(i > 0)
        def _():
            dg_ref[...] += dg

    rows = pl.BlockSpec((tr, D_MODEL), lambda i: (i, 0))
    gain = pl.BlockSpec((1, D_MODEL), lambda i: (0, 0))
    return _pcall(
        body, name=name + "_norm", grid=(t // tr,),
        in_specs=[rows, rows, gain, pl.BlockSpec((tr, 1), lambda i: (i, 0)), rows], out_specs=[rows, gain],
        out_shape=[SDS((t, D_MODEL), F32), SDS((1, D_MODEL), F32)], compiler_params=_params(),
    )(dn, h, g, r, dres)


def _final_bwd(h, g, tgt):
    t, d = h.shape
    tm = 256

    def body(h_ref, g_ref, t_ref, dh_ref, dg_ref, loss_ref):
        i = pl.program_id(0)
        x = h_ref[...]
        gg = g_ref[...]
        r = lax.rsqrt(jnp.mean(x * x, axis=-1, keepdims=True) + RMS_EPS)
        err = (x * r) * gg - t_ref[...]
        part = 0.5 * jnp.sum(jnp.mean(err * err, axis=-1, keepdims=True), axis=0, keepdims=True)
        dx, dg = _rms_bwd_rows(err * (1.0 / d), x, gg, r)
        dh_ref[...] = dx

        @pl.when(i == 0)
        def _():
            dg_ref[...] = dg
            loss_ref[...] = jnp.broadcast_to(part, loss_ref.shape)

        @pl.when(i > 0)
        def _():
            dg_ref[...] += dg
            loss_ref[...] += jnp.broadcast_to(part, loss_ref.shape)

    return _pcall(
        body, name="final_norm_loss", grid=(t // tm,),
        in_specs=[pl.BlockSpec((tm, d), lambda i: (i, 0)), pl.BlockSpec((1, d), lambda i: (0, 0)),
                  pl.BlockSpec((tm, d), lambda i: (i, 0))],
        out_specs=[pl.BlockSpec((tm, d), lambda i: (i, 0)), pl.BlockSpec((1, d), lambda i: (0, 0)),
                   pl.BlockSpec((1, LANES), lambda i: (0, 0))],
        out_shape=[SDS((t, d), F32), SDS((1, d), F32), SDS((1, LANES), F32)],
        compiler_params=_params(),
    )(h, g, tgt)


GATE_ROWS = 256


def _split3(x):
    hi = x.astype(BF16)
    r1 = x - hi.astype(F32)
    mid = r1.astype(BF16)
    lo = (r1 - mid.astype(F32)).astype(BF16)
    return hi, mid, lo


def _tri_sum(x, upper):
    rows = x.shape[0]
    ri = lax.broadcasted_iota(jnp.int32, (rows, rows), 0)
    ci = lax.broadcasted_iota(jnp.int32, (rows, rows), 1)
    tri = jnp.where((ri <= ci) if upper else (ri >= ci), 1.0, 0.0).astype(BF16)
    hi, mid, lo = _split3(x)
    return _dot(tri, hi, NN) + _dot(tri, mid, NN) + _dot(tri, lo, NN)


def _log_sigmoid(x):
    return jnp.minimum(x, 0.0) - jnp.log1p(jnp.exp(-jnp.abs(x)))


def _fox_gate_fwd(projf, bpad):
    t = projf.shape[0]
    tb = GATE_ROWS

    def body(x_ref, b_ref, c_ref, carry):
        i = pl.program_id(0)

        @pl.when(i == 0)
        def _():
            carry[...] = jnp.zeros_like(carry)

        c_ref[...] = _tri_sum(_log_sigmoid(x_ref[...] + b_ref[...]), upper=False) + carry[...]
        carry[...] = c_ref[pl.ds(tb - 1, 1), :]

    return _pcall(
        body, name="fox_gate_fwd", grid=(t // tb,),
        in_specs=[pl.BlockSpec((tb, LANES), lambda i: (i, 0)), pl.BlockSpec((1, LANES), lambda i: (0, 0))],
        out_specs=pl.BlockSpec((tb, LANES), lambda i: (i, 0)),
        out_shape=SDS((t, LANES), F32), scratch_shapes=[pltpu.VMEM((1, LANES), F32)],
        compiler_params=_params(),
    )(projf, bpad)


def _fox_gate_bwd(projf, bpad, dc_query, dc_key_wide):
    t = projf.shape[0]
    tb = GATE_ROWS
    nb = t // tb

    def body(x_ref, b_ref, dcq_ref, dck_ref, df_ref, db_ref, carry, buf):
        i = pl.program_id(0)

        @pl.when(i == 0)
        def _():
            carry[...] = jnp.zeros_like(carry)

        src = lax.broadcasted_iota(jnp.int32, (D_MODEL, LANES), 0)
        head = lax.broadcasted_iota(jnp.int32, (D_MODEL, LANES), 1)
        pick = jnp.where((head < FOX_HEADS) & (src == (head // 2) * LANES + (1 - head % 2) * FOX_HEAD_DIM), 1.0, 0.0)
        pick = pick.astype(BF16)
        dc_key = sum(_dot(piece, pick, NN) for piece in _split3(dck_ref[...]))
        buf[...] = _tri_sum(dcq_ref[...] - dc_key, upper=True) + carry[...]
        carry[...] = buf[pl.ds(0, 1), :]
        df = buf[...] * _sigmoid(-(x_ref[...] + b_ref[...]))
        df_ref[...] = df.astype(BF16)
        part = jnp.sum(df, axis=0, keepdims=True)

        @pl.when(i == 0)
        def _():
            db_ref[...] = part

        @pl.when(i > 0)
        def _():
            db_ref[...] += part

    rev = lambda i: (nb - 1 - i, 0)
    return _pcall(
        body, name="fox_gate_bwd", grid=(nb,),
        in_specs=[pl.BlockSpec((tb, LANES), rev), pl.BlockSpec((1, LANES), lambda i: (0, 0)),
                  pl.BlockSpec((tb, LANES), rev), pl.BlockSpec((tb, D_MODEL), rev)],
        out_specs=[pl.BlockSpec((tb, LANES), rev), pl.BlockSpec((1, LANES), lambda i: (0, 0))],
        out_shape=[SDS((t, LANES), BF16), SDS((1, LANES), F32)],
        scratch_shapes=[pltpu.VMEM((1, LANES), F32), pltpu.VMEM((tb, LANES), F32)],
        compiler_params=_params(),
    )(projf, bpad, dc_query, dc_key_wide)


FOX_TQ = 1024
FOX_TQ_FWD = 1024
FOX_SCALE = FOX_HEAD_DIM ** -0.5


def _low_lanes(shape):
    return lax.broadcasted_iota(jnp.int32, shape, len(shape) - 1) < FOX_HEAD_DIM


FOX_AUG = 3
FOX_CHAIN = 256
FOX_SUM_ROWS = 8


def _top_rows(shape):
    return lax.broadcasted_iota(jnp.int32, shape, 0) < FOX_HEAD_DIM


def _fox_aug(a, b, sign_a, sign_b, piece_entry, ones_entry, name):
    t = a.shape[0]
    tb = 512

    def body(a_ref, b_ref, o_ref):
        x = sign_a * a_ref[...]
        if sign_b != 0.0:
            x = x + sign_b * b_ref[...]
        head = lax.broadcasted_iota(jnp.int32, (LANES, D_MODEL), 0)
        col = lax.broadcasted_iota(jnp.int32, (LANES, D_MODEL), 1)
        base = (head // 2) * LANES + (1 - head % 2) * FOX_HEAD_DIM + piece_entry
        acc = jnp.zeros((tb, D_MODEL), F32)
        for e, piece in enumerate(_split3(x)):
            place = jnp.where((head < FOX_HEADS) & (col == base + e), 1.0, 0.0).astype(BF16)
            acc = acc + _dot(piece, place, NN)
        if ones_entry is not None:
            ent = lax.broadcasted_iota(jnp.int32, (1, D_MODEL), 1) % FOX_HEAD_DIM
            acc = acc + jnp.where((ent >= ones_entry) & (ent < ones_entry + FOX_AUG), 1.0, 0.0)
        o_ref[...] = acc.astype(BF16)

    blk = pl.BlockSpec((tb, LANES), lambda i: (i, 0))
    return _pcall(
        body, name=name, grid=(t // tb,), in_specs=[blk, blk],
        out_specs=pl.BlockSpec((tb, D_MODEL), lambda i: (i, 0)), out_shape=SDS((t, D_MODEL), BF16),
        compiler_params=_params(),
    )(a, b)


def _causal_steps(nq, key_major):
    if key_major:
        pairs = [(i, j) for j in range(nq) for i in range(j, nq)]
    else:
        pairs = [(i, j) for i in range(nq) for j in range(i + 1)]
    return (jnp.asarray([p[0] for p in pairs], jnp.int32), jnp.asarray([p[1] for p in pairs], jnp.int32))


def _pair_operand(low, own, other, hh):
    return jnp.where(low, own, other) if hh == 0 else jnp.where(low, other, own)


def _fox_fwd(proj, qaug, kaug):
    t = proj.shape[0]
    tq = tk = min(FOX_TQ_FWD, t)
    nq = t // tq
    cb = D_MODEL // LANES
    half = min(FOX_CHAIN, tq)

    i_tab, j_tab = _causal_steps(nq, key_major=False)

    def body(i_ref, j_ref, q_ref, k_ref, v_ref, z_ref, qa_ref, ka_ref, o_ref, g_ref, lse_ref, m_s, l_s, acc_s):
        step = pl.program_id(1)
        i, j = i_ref[step], j_ref[step]

        @pl.when(j == 0)
        def _():
            m_s[...] = jnp.full_like(m_s, NEG_INF)
            l_s[...] = jnp.zeros_like(l_s)
            acc_s[...] = jnp.zeros_like(acc_s)

        low = _low_lanes((tq, LANES))
        top = _top_rows((LANES, tq))

        def update(masked):
            qs = q_ref[...] * FOX_SCALE
            qa, k, ka, v = qa_ref[...], k_ref[...], ka_ref[...], v_ref[...]
            if masked:
                causal = (lax.broadcasted_iota(jnp.int32, (tk, tq), 0) <= lax.broadcasted_iota(jnp.int32, (tk, tq), 1))
            one = jnp.ones_like(v)
            chains = [(hh, slice(c * half, (c + 1) * half)) for hh in range(2) for c in range(tq // half)]
            qh = [_pair_operand(low, qs, qa, hh) for hh in range(2)]
            kh = [_pair_operand(low, k, ka, hh) for hh in range(2)]
            vh = [_pair_operand(low, v, one, hh) for hh in range(2)]
            keys = lambda cols: slice(0, cols.stop) if masked else slice(None)
            scores = [_dot(kh[hh][keys(cols), :], qh[hh][cols, :], NT) for hh, cols in chains]
            for (hh, cols), s in zip(chains, scores):
                if masked:
                    s = jnp.where(causal[keys(cols), cols], s, NEG_INF)
                m_prev = m_s[hh, :, cols]
                m_new = jnp.maximum(m_prev, jnp.max(s, axis=0, keepdims=True))
                alpha = jnp.exp(m_prev - m_new)
                pv = _dot(vh[hh][keys(cols), :], jnp.exp(s - m_new).astype(BF16), TN)
                sums = pv[FOX_HEAD_DIM:FOX_HEAD_DIM + FOX_SUM_ROWS, :] if hh == 0 else pv[0:FOX_SUM_ROWS, :]
                l_s[hh, :, cols] = alpha * l_s[hh, :, cols] + sums
                m_s[hh, :, cols] = m_new
                own = top[:, cols] if hh == 0 else jnp.logical_not(top[:, cols])
                acc_s[:, cols] = jnp.where(own, acc_s[:, cols] * alpha + pv, acc_s[:, cols])

        pl.when(j < i)(functools.partial(update, False))
        pl.when(j == i)(functools.partial(update, True))

        @pl.when(j == i)
        def _():
            o = (acc_s[...] / jnp.where(top, l_s[0, 0:1, :], l_s[1, 0:1, :])).T
            z = z_ref[...].astype(F32)
            o_ref[...] = o.astype(BF16)
            g_ref[...] = (o * (z * _sigmoid(z))).astype(BF16)
            for hh in range(2):
                lse_ref[hh] = m_s[hh] + jnp.log(l_s[hh, 0:1, :])

    qblk = lambda col: pl.BlockSpec((tq, LANES), lambda h, s, it, jt: (it[s], col + h))
    kblk = lambda col: pl.BlockSpec((tk, LANES), lambda h, s, it, jt: (jt[s], col + h))
    return _pcall(
        body, name="fox_attn_fwd",
        grid_spec=pltpu.PrefetchScalarGridSpec(
            num_scalar_prefetch=2, grid=(FOX_PAIRS, i_tab.shape[0]),
            in_specs=[qblk(0), kblk(cb), kblk(2 * cb), qblk(3 * cb), qblk(0), kblk(0)],
            out_specs=[qblk(0), qblk(0), pl.BlockSpec((2, 1, tq), lambda h, s, it, jt: (h, 0, it[s]))],
            scratch_shapes=[pltpu.VMEM((2, 1, tq), F32), pltpu.VMEM((2, FOX_SUM_ROWS, tq), F32),
                            pltpu.VMEM((LANES, tq), F32)]),
        out_shape=[SDS((t, D_MODEL), BF16), SDS((t, D_MODEL), BF16), SDS((FOX_HEADS, 1, t), F32)],
        compiler_params=_params(dimension_semantics=("arbitrary", "arbitrary")),
    )(i_tab, j_tab, proj, proj, proj, proj, qaug, kaug)


def _fox_bwd(proj, do, qaug, kaug, doaug):
    t = proj.shape[0]
    tq = tk = min(FOX_TQ, t)
    nq = t // tq
    cb = D_MODEL // LANES
    half = min(FOX_CHAIN, tq)

    i_tab, j_tab = _causal_steps(nq, key_major=True)

    def body(i_ref, j_ref, q_ref, k_ref, v_ref, do_ref, qa_ref, ka_ref, da_ref,
             dq_ref, dk_ref, dv_ref, dck_ref, dcq_ref, dq_acc, dcq_acc, dk_acc, dks_acc, dv_acc):
        step = pl.program_id(1)
        i, j = i_ref[step], j_ref[step]
        low = _low_lanes((tq, LANES))
        top = _top_rows((LANES, tq))

        @pl.when(i == j)
        def _():
            dk_acc[...] = jnp.zeros_like(dk_acc)
            dks_acc[...] = jnp.zeros_like(dks_acc)
            dv_acc[...] = jnp.zeros_like(dv_acc)

        def update(masked):
            qs = q_ref[...] * FOX_SCALE
            k, v, dout = k_ref[...], v_ref[...], do_ref[...]
            qa, ka, da = qa_ref[...], ka_ref[...], da_ref[...]
            lane = lax.broadcasted_iota(jnp.int32, (tk, LANES), 1)
            vone = jnp.where((lane & (FOX_HEAD_DIM - 1)) < FOX_AUG, 1.0, 0.0).astype(v.dtype)
            one = jnp.ones_like(k)
            if masked:
                causal = (lax.broadcasted_iota(jnp.int32, (tk, tq), 0) <= lax.broadcasted_iota(jnp.int32, (tk, tq), 1))
            parts = []
            kh = [_pair_operand(low, k, ka, hh) for hh in range(2)]
            qh = [_pair_operand(low, qs, qa, hh) for hh in range(2)]
            vh = [_pair_operand(low, v, vone, hh) for hh in range(2)]
            doh = [_pair_operand(low, dout, da, hh) for hh in range(2)]
            q1 = [_pair_operand(low, qs, one, hh) for hh in range(2)]
            k1 = [_pair_operand(low, k, one, hh) for hh in range(2)]

            def tile(hh, keys, cols, s, dp):
                if masked:
                    s = jnp.where(causal[keys, cols], s, NEG_INF)
                p = jnp.exp(s)
                pb, dsb = p.astype(BF16), (p * dp).astype(BF16)
                return (_dot(pb, dout[cols, :], NN), _dot(dsb, q1[hh][cols, :], NN), _dot(k1[hh][keys, :], dsb, TN))

            if not masked:
                scores = [_dot(kh[hh], qh[hh], NT) for hh in range(2)]
                dps = [_dot(vh[hh], doh[hh], NT) for hh in range(2)]
                everything = slice(None)
                parts = [tile(hh, everything, everything, scores[hh], dps[hh]) for hh in range(2)]
            else:
                for hh in range(2):
                    dv_h, dk_h, dq_h = jnp.zeros((tk, LANES), F32), jnp.zeros((tk, LANES), F32), []
                    for c in range(tq // half):
                        cols, keys = slice(c * half, (c + 1) * half), slice(0, (c + 1) * half)
                        dv_c, dk_c, dq_c = tile(hh, keys, cols, _dot(kh[hh][keys, :], qh[hh][cols, :], NT),
                                                _dot(vh[hh][keys, :], doh[hh][cols, :], NT))
                        below = ((0, tk - keys.stop), (0, 0))
                        dv_h, dk_h = dv_h + jnp.pad(dv_c, below), dk_h + jnp.pad(dk_c, below)
                        dq_h.append(dq_c)
                    parts.append((dv_h, dk_h, jnp.concatenate(dq_h, axis=1)))
            dv_acc[...] += jnp.where(low, parts[0][0], parts[1][0])
            dk_acc[...] += jnp.where(low, parts[0][1], parts[1][1])
            dks_acc[...] += jnp.where(low, parts[1][1], parts[0][1])
            dq_t = jnp.where(top, parts[0][2], parts[1][2]) * FOX_SCALE
            sum_a = parts[0][2][FOX_HEAD_DIM:FOX_HEAD_DIM + FOX_SUM_ROWS, :]
            sum_b = parts[1][2][0:FOX_SUM_ROWS, :]

            @pl.when(j == 0)
            def _():
                dq_acc[i] = dq_t
                dcq_acc[0, i] = sum_a
                dcq_acc[1, i] = sum_b

            @pl.when(j > 0)
            def _():
                dq_acc[i] += dq_t
                dcq_acc[0, i] += sum_a
                dcq_acc[1, i] += sum_b

        pl.when(i > j)(functools.partial(update, False))
        pl.when(i == j)(functools.partial(update, True))

        @pl.when(i == nq - 1)
        def _():
            dk_ref[...] = dk_acc[...].astype(BF16)
            dv_ref[...] = dv_acc[...].astype(BF16)
            dck_ref[...] = dks_acc[...]

        @pl.when((i == nq - 1) & (j == nq - 1))
        def _():
            for blk in range(nq):
                dq_ref[blk * tq:(blk + 1) * tq, :] = dq_acc[blk].T.astype(BF16)
            dcq_ref[...] = dcq_acc[...]

    qblk = lambda col: pl.BlockSpec((tq, LANES), lambda h, s, it, jt: (it[s], col + h))
    kblk = lambda col: pl.BlockSpec((tk, LANES), lambda h, s, it, jt: (jt[s], col + h))
    return _pcall(
        body, name="fox_attn_bwd",
        grid_spec=pltpu.PrefetchScalarGridSpec(
            num_scalar_prefetch=2, grid=(FOX_PAIRS, i_tab.shape[0]),
            in_specs=[qblk(0), kblk(cb), kblk(2 * cb), qblk(0), qblk(0), kblk(0), qblk(0)],
            out_specs=[pl.BlockSpec((t, LANES), lambda h, s, it, jt: (0, h)), kblk(0), kblk(0), kblk(0),
                       pl.BlockSpec((2, nq, FOX_SUM_ROWS, tq), lambda h, s, it, jt: (h, 0, 0, 0))],
            scratch_shapes=[pltpu.VMEM((nq, LANES, tq), F32), pltpu.VMEM((2, nq, FOX_SUM_ROWS, tq), F32),
                            pltpu.VMEM((tk, LANES), F32), pltpu.VMEM((tk, LANES), F32), pltpu.VMEM((tk, LANES), F32)]),
        out_shape=[SDS((t, D_MODEL), BF16), SDS((t, D_MODEL), BF16), SDS((t, D_MODEL), BF16),
                   SDS((t, D_MODEL), F32), SDS((FOX_HEADS, nq, FOX_SUM_ROWS, tq), F32)],
        compiler_params=_params(dimension_semantics=("arbitrary", "arbitrary")),
    )(i_tab, j_tab, proj, proj, proj, do, qaug, kaug, doaug)


def _mm_gate_bwd(dh, w_out, z_src, z_col0, o, heads, name):
    t = dh.shape[0]
    tm = 512
    row = lambda i, j, k: (i, 0)
    zcb = z_col0 // D_MODEL
    per_block = heads == D_MODEL // LANES

    def epilogue(acc, e_refs, o_refs, i):
        z = e_refs[0][...].astype(F32)
        ov = e_refs[1][...].astype(F32)
        sg = _sigmoid(z)
        dout = acc * (z * sg)
        o_refs[0][...] = dout.astype(BF16)
        o_refs[1][...] = (acc * ov * (sg * (1.0 + z * (1.0 - sg)))).astype(BF16)
        prod = dout * ov
        lane = lax.broadcasted_iota(jnp.int32, (tm, LANES), 1)
        cols = jnp.zeros((tm, LANES), F32)
        for cbk in range(D_MODEL // LANES):
            seg = prod[:, cbk * LANES:(cbk + 1) * LANES]
            tot = jnp.sum(seg, axis=-1, keepdims=True)
            if per_block:
                o_refs[2][cbk] = tot
            else:
                lo = jnp.sum(jnp.where(_low_lanes(seg.shape), seg, 0.0), axis=-1, keepdims=True)
                cols = jnp.where(lane == 2 * cbk, lo, jnp.where(lane == 2 * cbk + 1, tot - lo, cols))
        if not per_block:
            o_refs[2][...] = cols

    delta_out = ((SDS((heads, t, 1), F32), (heads, tm, 1), lambda i, j, k: (0, i, 0)) if per_block
                 else (SDS((t, LANES), F32), (tm, LANES), row))
    return _mm([dh], w_out, "nt", name, tiles=(tm, D_MODEL, D_MODEL),
               extras=[(z_src, (tm, D_MODEL), lambda i, j, k: (i, zcb)), (o, (tm, D_MODEL), row)],
               outs=[(SDS((t, D_MODEL), BF16), (tm, D_MODEL), row), (SDS((t, D_MODEL), BF16), (tm, D_MODEL), row),
                     delta_out],
               epilogue=epilogue)


def _ple_fwd(h, pin, w_up, w_gate, name):
    t = h.shape[0]
    tm = 512
    pd = pin.shape[1]

    def body(h_ref, p_ref, wu_ref, wg_ref, hn_ref, u_ref, a_ref):
        h = h_ref[...]
        u = _dot(p_ref[...].astype(BF16), wu_ref[...], NN)
        a = _dot(h.astype(BF16), wg_ref[...], NN)
        hn_ref[...] = h + u * _sigmoid(a)
        u_ref[...] = u.astype(BF16)
        a_ref[...] = a.astype(BF16)

    rows = pl.BlockSpec((tm, D_MODEL), lambda i: (i, 0))
    return _pcall(
        body, name=name, grid=(t // tm,),
        in_specs=[rows, pl.BlockSpec((tm, pd), lambda i: (i, 0)),
                  pl.BlockSpec((pd, D_MODEL), lambda i: (0, 0)), pl.BlockSpec((D_MODEL, D_MODEL), lambda i: (0, 0))],
        out_specs=[rows, rows, rows],
        out_shape=[SDS((t, D_MODEL), F32), SDS((t, D_MODEL), BF16), SDS((t, D_MODEL), BF16)],
        compiler_params=_params(),
    )(h, pin, w_up, w_gate)


def _ple_bwd_elem(dh, u, a, name):
    t = dh.shape[0]
    tm = 512

    def body(dh_ref, u_ref, a_ref, du_ref, da_ref):
        g = dh_ref[...]
        s = _sigmoid(a_ref[...].astype(F32))
        du_ref[...] = (g * s).astype(BF16)
        da_ref[...] = (g * u_ref[...].astype(F32) * (s * (1.0 - s))).astype(BF16)

    blk = pl.BlockSpec((tm, D_MODEL), lambda i: (i, 0))
    return _pcall(
        body, name=name, grid=(t // tm,), in_specs=[blk, blk, blk], out_specs=[blk, blk],
        out_shape=[SDS((t, D_MODEL), BF16), SDS((t, D_MODEL), BF16)], compiler_params=_params(),
    )(dh, u, a)


DIL_SCALE = LANES ** -0.5


def _dil_masks():
    ii = lax.broadcasted_iota(jnp.int32, (DIL_BLOCK, DIL_BLOCK), 0)
    jj = lax.broadcasted_iota(jnp.int32, (DIL_BLOCK, DIL_BLOCK), 1)
    return ii, jj


DIL_UNITS = 16
BNT = (((2,), (2,)), ((0,), (0,)))
BNN = (((2,), (1,)), ((0,), (0,)))
BTN = (((1,), (1,)), ((0,), (0,)))


def _dil_units(dil):
    return [(b, r) for b in range(DIL_UNITS // dil) for r in range(dil)]


DIL_MAX_STRIDE = 4


def _pre(dil):
    return max(dil // DIL_MAX_STRIDE, 1)


def _stage_in(src, dst, staging, dil, lead=()):
    n, pre = dst.shape[0], _pre(dil)
    if pre == 1:
        dst[...] = src[lead + (slice(None), slice(None))].astype(F32)
        return
    if src.dtype == F32:
        staging = src
    else:
        staging[...] = src[...].astype(F32)
    for q in range(pre):
        dst[q * (n // pre):(q + 1) * (n // pre), :] = staging[lead + (pl.ds(q, n // pre, stride=pre), slice(None))]


def _stage_out(src, dst, staging, dil, lead=()):
    n, pre = src.shape[0], _pre(dil)
    if pre == 1:
        dst[lead + (slice(None), slice(None))] = src[...].astype(dst.dtype)
        return
    out = dst if dst.dtype == F32 else staging
    for q in range(pre):
        out[(lead if dst.dtype == F32 else ()) + (pl.ds(q, n // pre, stride=pre), slice(None))] = \
            src[q * (n // pre):(q + 1) * (n // pre), :]
    if dst.dtype != F32:
        dst[...] = staging[...].astype(dst.dtype)


def _unit_rows(b, r, dil, n):
    pre = _pre(dil)
    return pl.ds((r % pre) * (n // pre) + (b * DIL_BLOCK * dil + r) // pre, DIL_BLOCK, stride=dil // pre)


def _gather_units(cur, dil, shift=0, edge=None):
    nbk = DIL_UNITS // dil
    parts = []
    for b, r in _dil_units(dil):
        bb = b + shift
        if 0 <= bb < nbk:
            parts.append(cur[_unit_rows(bb, r, dil, cur.shape[0]), :])
        else:
            parts.append(edge[_unit_rows(0, r, dil, edge.shape[0]), :])
    return jnp.stack(parts)


def _scatter_units(dst, val, dil):
    for u, (b, r) in enumerate(_dil_units(dil)):
        dst[_unit_rows(b, r, dil, dst.shape[0]), :] = val[u]


def _dil_bias(slope, prev):
    ii, jj = _dil_masks()
    dist = (DIL_BLOCK + ii - jj) if prev else (ii - jj)
    return (slope * dist.astype(F32))[None], ((jj >= ii) if prev else (jj <= ii))[None]


def _dil_fwd(proj, slopes, grp, dil, name):
    t = proj.shape[0]
    rows = DIL_BLOCK * DIL_UNITS
    edge_rows = DIL_BLOCK * dil
    nbk = DIL_UNITS // dil
    nsb = t // rows
    qc, kc_, vc_ = grp * DIL_HEADS, 3 * DIL_HEADS + grp * DIL_HEADS, 6 * DIL_HEADS + grp * DIL_HEADS

    def body(q_ref, kp_ref, kc_ref, vp_ref, vc_ref, sl_ref, o_ref, lse_ref, qf, kpf, kcf, vpf, vcf, of, lf, staging,
             lnat):
        m, h = pl.program_id(0), pl.program_id(1)
        for src, dst in ((q_ref, qf), (kp_ref, kpf), (kc_ref, kcf), (vp_ref, vpf), (vc_ref, vcf)):
            _stage_in(src, dst, staging, dil)
        slope = sl_ref[0]
        unit = lax.broadcasted_iota(jnp.int32, (DIL_UNITS, 1, 1), 0)
        has_prev = (unit >= dil) | (m > 0)
        q = _gather_units(qf, dil).astype(BF16)
        kc, vc = _gather_units(kcf, dil).astype(BF16), _gather_units(vcf, dil).astype(BF16)
        kp, vp = _gather_units(kcf, dil, -1, kpf).astype(BF16), _gather_units(vcf, dil, -1, vpf).astype(BF16)
        bias_p, ok_p = _dil_bias(slope, True)
        bias_c, ok_c = _dil_bias(slope, False)
        sp = jnp.where(ok_p & has_prev, _dot(q, kp, BNT) * DIL_SCALE - bias_p, NEG_INF)
        sc = jnp.where(ok_c, _dot(q, kc, BNT) * DIL_SCALE - bias_c, NEG_INF)
        mx = jnp.maximum(jnp.max(sp, axis=-1, keepdims=True), jnp.max(sc, axis=-1, keepdims=True))
        pp = jnp.exp(sp - mx)
        pc = jnp.exp(sc - mx)
        l = jnp.sum(pp, axis=-1, keepdims=True) + jnp.sum(pc, axis=-1, keepdims=True)
        o = (_dot(pp.astype(BF16), vp, BNN) + _dot(pc.astype(BF16), vc, BNN)) / l
        _scatter_units(of, o, dil)
        _scatter_units(lf, mx + jnp.log(l), dil)
        _stage_out(of, o_ref, staging, dil)
        _stage_out(lf, lnat, staging, dil)

        @pl.when(h == 0)
        def _():
            lse_ref[...] = jnp.zeros_like(lse_ref)

        lane = lax.broadcasted_iota(jnp.int32, (rows, LANES), 1)
        lse_ref[...] = jnp.where(lane == h, lnat[...], lse_ref[...])

    cur = lambda col: pl.BlockSpec((rows, LANES), lambda m, h: (m, col + h))
    prev = lambda col: pl.BlockSpec((edge_rows, LANES), lambda m, h: (jnp.maximum(m * nbk - 1, 0), col + h))
    return _pcall(
        body, name=name, grid=(nsb, DIL_HEADS),
        in_specs=[cur(qc), prev(kc_), cur(kc_), prev(vc_), cur(vc_), pl.BlockSpec((1, 1, 1), lambda m, h: (h, 0, 0))],
        out_specs=[pl.BlockSpec((rows, LANES), lambda m, h: (m, h)), pl.BlockSpec((rows, LANES), lambda m, h: (m, 0))],
        out_shape=[SDS((t, D_MODEL), BF16), SDS((t, LANES), F32)],
        scratch_shapes=[pltpu.VMEM((rows, LANES), F32), pltpu.VMEM((edge_rows, LANES), F32), pltpu.VMEM((rows, LANES), F32),
                        pltpu.VMEM((edge_rows, LANES), F32), pltpu.VMEM((rows, LANES), F32), pltpu.VMEM((rows, LANES), F32),
                        pltpu.VMEM((rows, 1), F32), pltpu.VMEM((rows, LANES), F32), pltpu.VMEM((rows, 1), F32)],
        compiler_params=_params(dimension_semantics=("arbitrary", "arbitrary")),
    )(proj, proj, proj, proj, proj, slopes)


def _dil_mix(outs, lses, proj, z_col0):
    t = proj.shape[0]
    tm = 512
    zcb = z_col0 // D_MODEL
    ng = len(outs)

    def body(*refs):
        o_refs, l_refs, z_ref = refs[:ng], refs[ng:2 * ng], refs[2 * ng]
        om_ref, g_ref, lse_ref = refs[2 * ng + 1:]
        ls = [r[...] for r in l_refs]
        mx = functools.reduce(jnp.maximum, ls)
        es = [jnp.exp(l - mx) for l in ls]
        tot = functools.reduce(jnp.add, es)
        head = lax.broadcasted_iota(jnp.int32, (LANES, D_MODEL), 0)
        col = lax.broadcasted_iota(jnp.int32, (LANES, D_MODEL), 1)
        spread = jnp.where((head < DIL_HEADS) & (col // LANES == head), 1.0, 0.0).astype(BF16)
        widen = lambda w: sum(_dot(piece, spread, NN) for piece in _split3(w))
        o = functools.reduce(jnp.add, [widen(e / tot) * r[...].astype(F32) for e, r in zip(es, o_refs)])
        z = z_ref[...].astype(F32)
        om_ref[...] = o.astype(BF16)
        g_ref[...] = (o * (z * _sigmoid(z))).astype(BF16)
        joint = mx + jnp.log(tot)
        lane = lax.broadcasted_iota(jnp.int32, joint.shape, 1)
        for hd in range(DIL_HEADS):
            lse_ref[hd] = jnp.sum(jnp.where(lane == hd, joint, 0.0), axis=-1, keepdims=True)

    rows = pl.BlockSpec((tm, D_MODEL), lambda i: (i, 0))
    lanes = pl.BlockSpec((tm, LANES), lambda i: (i, 0))
    return _pcall(
        body, name="dil_mix", grid=(t // tm,),
        in_specs=[rows] * ng + [lanes] * ng + [pl.BlockSpec((tm, D_MODEL), lambda i: (i, zcb))],
        out_specs=[rows, rows, pl.BlockSpec((DIL_HEADS, tm, 1), lambda i: (0, i, 0))],
        out_shape=[SDS((t, D_MODEL), BF16), SDS((t, D_MODEL), BF16), SDS((DIL_HEADS, t, 1), F32)],
        compiler_params=_params(),
    )(*outs, *lses, proj)


def _dil_bwd(proj, do, lse, delta, slopes, grp, dil, name):
    t = proj.shape[0]
    rows = DIL_BLOCK * DIL_UNITS
    edge_rows = DIL_BLOCK * dil
    nbk = DIL_UNITS // dil
    nsb = t // rows
    last_edge = t // edge_rows - 1
    qc, kc_, vc_ = grp * DIL_HEADS, 3 * DIL_HEADS + grp * DIL_HEADS, 6 * DIL_HEADS + grp * DIL_HEADS

    def body(q_ref, qn_ref, kp_ref, kc_ref, vp_ref, vc_ref, do_ref, don_ref, l_ref, ln_ref, d_ref, dn_ref, sl_ref,
             dqkv_ref, qf, qnf, kpf, kcf, vpf, vcf, dof, donf, dqf, dkf, dvf, staging, lf, lnf, df, dnf):
        m = pl.program_id(1)
        for src, dst in ((q_ref, qf), (qn_ref, qnf), (kp_ref, kpf), (kc_ref, kcf), (vp_ref, vpf), (vc_ref, vcf),
                         (do_ref, dof), (don_ref, donf)):
            _stage_in(src, dst, staging, dil)
        for src, dst in ((l_ref, lf), (ln_ref, lnf), (d_ref, df), (dn_ref, dnf)):
            _stage_in(src, dst, staging, dil, lead=(0,))
        slope = sl_ref[0]
        unit = lax.broadcasted_iota(jnp.int32, (DIL_UNITS, 1, 1), 0)
        has_prev = (unit >= dil) | (m > 0)
        has_next = (unit < DIL_UNITS - dil) | (m < nsb - 1)
        b16 = lambda x: x.astype(BF16)
        q, kc, vc, dout = (b16(_gather_units(x, dil)) for x in (qf, kcf, vcf, dof))
        kp, vp = b16(_gather_units(kcf, dil, -1, kpf)), b16(_gather_units(vcf, dil, -1, vpf))
        qn, don = b16(_gather_units(qf, dil, 1, qnf)), b16(_gather_units(dof, dil, 1, donf))
        lrow, drow = _gather_units(lf, dil), _gather_units(df, dil)
        lnrow, dnrow = _gather_units(lf, dil, 1, lnf), _gather_units(df, dil, 1, dnf)
        bias_p, ok_p = _dil_bias(slope, True)
        bias_c, ok_c = _dil_bias(slope, False)
        sp = jnp.where(ok_p & has_prev, _dot(q, kp, BNT) * DIL_SCALE - bias_p, NEG_INF)
        sc = jnp.where(ok_c, _dot(q, kc, BNT) * DIL_SCALE - bias_c, NEG_INF)
        pp = jnp.exp(sp - lrow)
        pc = jnp.exp(sc - lrow)
        dsp = b16(pp * (_dot(dout, vp, BNT) - drow))
        dsc = b16(pc * (_dot(dout, vc, BNT) - drow))
        _scatter_units(dqf, (_dot(dsp, kp, BNN) + _dot(dsc, kc, BNN)) * DIL_SCALE, dil)
        sn = jnp.where(ok_p & has_next, _dot(qn, kc, BNT) * DIL_SCALE - bias_p, NEG_INF)
        pn = jnp.exp(sn - lnrow)
        dsn = b16(pn * (_dot(don, vc, BNT) - dnrow))
        _scatter_units(dkf, (_dot(dsc, q, BTN) + _dot(dsn, qn, BTN)) * DIL_SCALE, dil)
        _scatter_units(dvf, _dot(b16(pc), dout, BTN) + _dot(b16(pn), don, BTN), dil)
        for s, src in enumerate((dqf, dkf, dvf)):
            _stage_out(src, dqkv_ref.at[s], staging, dil)

    prev_i = lambda m: jnp.maximum(m * nbk - 1, 0)
    next_i = lambda m: jnp.minimum((m + 1) * nbk, last_edge)
    cur = lambda col: pl.BlockSpec((rows, LANES), lambda h, m: (m, col + h))
    edge = lambda col, f: pl.BlockSpec((edge_rows, LANES), lambda h, m: (f(m), col + h))
    colcur = pl.BlockSpec((1, rows, 1), lambda h, m: (h, m, 0))
    colnext = pl.BlockSpec((1, edge_rows, 1), lambda h, m: (h, next_i(m), 0))
    out_blk = pl.BlockSpec((3, rows, LANES), lambda h, m: (0, m, h))
    big, small = pltpu.VMEM((rows, LANES), F32), pltpu.VMEM((edge_rows, LANES), F32)
    return _pcall(
        body, name=name, grid=(DIL_HEADS, nsb),
        in_specs=[cur(qc), edge(qc, next_i), edge(kc_, prev_i), cur(kc_), edge(vc_, prev_i), cur(vc_),
                  cur(0), edge(0, next_i), colcur, colnext, colcur, colnext,
                  pl.BlockSpec((1, 1, 1), lambda h, m: (h, 0, 0))],
        out_specs=out_blk, out_shape=SDS((3, t, D_MODEL), BF16),
        scratch_shapes=[big, small, small, big, small, big, big, small, big, big, big, big,
                        pltpu.VMEM((rows, 1), F32), pltpu.VMEM((edge_rows, 1), F32),
                        pltpu.VMEM((rows, 1), F32), pltpu.VMEM((edge_rows, 1), F32)],
        compiler_params=_params(),
    )(proj, proj, proj, proj, proj, proj, do, do, lse, lse, delta, delta, slopes)


def _mesh_pos():
    x, y, c = lax.axis_index("x"), lax.axis_index("y"), lax.axis_index("c")
    return x, y, c


def _peer(pos, k):
    x, y, c = pos
    px = 1 - x if k & 4 else x
    py = 1 - y if k & 2 else y
    pc = 1 - c if k & 1 else c
    return (px, py, pc), 4 * px + 2 * py + pc


N_CHIPS = 4
CHIP_FLIPS = ((1, 0), (0, 1), (1, 1))


def _other_chips(x, y):
    return [(1 - x if fx else x, 1 - y if fy else y) for fx, fy in CHIP_FLIPS]


def _all_gather(arrays):
    n = len(arrays)
    per = 2 * N_CHIPS - 1
    hbm = pl.BlockSpec(memory_space=pltpu.HBM)

    def body(*refs):
        ins, outs = refs[:n], refs[n:2 * n]
        send_sems, recv_sems, local_sems = refs[2 * n:]
        x, y, c = _mesh_pos()
        sibling = (x, y, 1 - c)
        chips = _other_chips(x, y)
        block = lambda px, py, pc: 4 * px + 2 * py + pc

        def copy(w, k, src, blk, to):
            return pltpu.make_async_remote_copy(
                src_ref=src, dst_ref=outs[w].at[blk], send_sem=send_sems.at[w * per + k],
                recv_sem=recv_sems.at[w * per + k], device_id=to, device_id_type=MESH)

        local, started = [], []
        for w in range(n):
            cp = pltpu.make_async_copy(ins[w], outs[w].at[block(x, y, c)], local_sems.at[w])
            cp.start()
            local.append(cp)
            started.append(copy(w, 0, ins[w], block(x, y, c), sibling))
            for j, (px, py) in enumerate(chips):
                started.append(copy(w, 1 + j, ins[w], block(x, y, c), (px, py, c)))
        for cp in started:
            cp.start()
        for j, (px, py) in enumerate(chips):
            for w in range(n):
                copy(w, 1 + j, ins[w], block(px, py, c), sibling).wait_recv()
                cp = copy(w, 4 + j, outs[w].at[block(px, py, c)], block(px, py, c), sibling)
                cp.start()
                started.append(cp)
        for w in range(n):
            copy(w, 0, ins[w], block(x, y, 1 - c), sibling).wait_recv()
            for j, (px, py) in enumerate(chips):
                copy(w, 4 + j, ins[w], block(px, py, 1 - c), sibling).wait_recv()
        for cp in started:
            cp.wait_send()
        for cp in local:
            cp.wait()

    return _pcall(
        body, name="all_gather_weights", in_specs=[hbm] * n, out_specs=[hbm] * n,
        out_shape=[SDS((N_DEV,) + a.shape, a.dtype) for a in arrays],
        scratch_shapes=[pltpu.SemaphoreType.DMA((n * per,)), pltpu.SemaphoreType.DMA((n * per,)),
                        pltpu.SemaphoreType.DMA((n,))],
    )(*arrays)


HBM_SPEC = pl.BlockSpec(memory_space=pltpu.HBM)
SEM_SPEC = pl.BlockSpec(memory_space=pltpu.SEMAPHORE)
DATAFLOW = pltpu.SideEffectType.DATAFLOW_SIDE_EFFECTING


def _push_start(arrays, scatter, name):
    n = len(arrays)
    per = N_DEV - 1

    def body(*refs):
        srcs, lands = refs[:n], refs[n:2 * n]
        send_sems, recv_sems, token = refs[2 * n], refs[2 * n + 1], refs[-1]
        pos = _mesh_pos()
        me = 4 * pos[0] + 2 * pos[1] + pos[2]
        for w in range(n):
            for k in range(1, N_DEV):
                peer, peer_idx = _peer(pos, k)
                pltpu.make_async_remote_copy(
                    src_ref=srcs[w].at[peer_idx] if scatter else srcs[w], dst_ref=lands[w].at[me],
                    send_sem=send_sems.at[w * per + k - 1], recv_sem=recv_sems.at[w * per + k - 1],
                    device_id=peer, device_id_type=MESH).start()
        token[...] = jnp.zeros_like(token)

    land_shapes = [a.shape if scatter else (N_DEV,) + a.shape for a in arrays]
    in_hbm = lambda a: pltpu.with_memory_space_constraint(a, pltpu.HBM)
    lands = [in_hbm(lax.empty(s, a.dtype)) for s, a in zip(land_shapes, arrays)]
    sems = pltpu.SemaphoreType.DMA((n * per,))
    res = _pcall(
        body, name=name,
        out_shape=(sems, sems, *[pltpu.HBM(a.shape, a.dtype) for a in arrays],
                   *[pltpu.HBM(s, a.dtype) for s, a in zip(land_shapes, arrays)], SDS((8, LANES), F32)),
        in_specs=[HBM_SPEC] * (2 * n),
        out_specs=(SEM_SPEC, SEM_SPEC, *[HBM_SPEC] * (2 * n), pl.BlockSpec(memory_space=pltpu.VMEM)),
        input_output_aliases={i: 2 + i for i in range(2 * n)},
        compiler_params=pltpu.CompilerParams(has_side_effects=DATAFLOW),
    )(*[in_hbm(a) for a in arrays], *lands)
    return res[0], res[1], list(res[2:2 + n]), list(res[2 + n:2 + 2 * n]), res[-1]


def _push_wait(send_sems, recv_sems, arrays, lands, after, scatter, name):
    n = len(arrays)
    per = N_DEV - 1

    def body(*refs):
        srcs, lands_ = refs[:n], refs[n:2 * n]
        send_sems_, recv_sems_ = refs[2 * n], refs[2 * n + 1]
        pos = _mesh_pos()
        for w in range(n):
            for k in range(1, N_DEV):
                peer, peer_idx = _peer(pos, k)
                cp = pltpu.make_async_remote_copy(
                    src_ref=srcs[w].at[peer_idx] if scatter else srcs[w], dst_ref=lands_[w].at[peer_idx],
                    send_sem=send_sems_.at[w * per + k - 1], recv_sem=recv_sems_.at[w * per + k - 1],
                    device_id=peer, device_id_type=MESH)
                cp.wait_send()
                cp.wait_recv()

    res = _pcall(
        body, name=name,
        out_shape=(*[pltpu.HBM(a.shape, a.dtype) for a in arrays], *[pltpu.HBM(l.shape, l.dtype) for l in lands]),
        in_specs=[HBM_SPEC] * (2 * n) + [SEM_SPEC, SEM_SPEC, pl.BlockSpec(memory_space=pl.ANY)],
        out_specs=[HBM_SPEC] * (2 * n), input_output_aliases={i: i for i in range(2 * n)},
        compiler_params=pltpu.CompilerParams(has_side_effects=DATAFLOW),
    )(*arrays, *lands, send_sems, recv_sems, after)
    return list(res[n:])


def _fill_own(land, own):
    me = 4 * lax.axis_index("x") + 2 * lax.axis_index("y") + lax.axis_index("c")
    return lax.dynamic_update_slice(land, own[None], (me,) + (0,) * own.ndim)


def _adam_math(w, g, m, v):
    m = ADAM_B1 * m + (1.0 - ADAM_B1) * g
    v = ADAM_B2 * v + (1.0 - ADAM_B2) * (g * g)
    m_hat = m / (1.0 - ADAM_B1 ** ADAM_STEP)
    v_hat = v / (1.0 - ADAM_B2 ** ADAM_STEP)
    delta = -ADAM_LR * (m_hat / (jnp.sqrt(v_hat) + ADAM_EPS) + ADAM_WD * w)
    return delta, m, v


def _adamw(recv, w, m, v, name):
    n_parts, r, c = recv.shape
    layers, rows_per_layer, _ = w.shape
    tr = min(rows_per_layer, 128)
    per_layer = rows_per_layer // tr

    def body(g_ref, w_ref, m_ref, v_ref, go_ref, d_ref, mo_ref, vo_ref):
        g = g_ref[0].astype(F32)
        for s in range(1, n_parts):
            g = g + g_ref[s].astype(F32)
        delta, mn, vn = _adam_math(w_ref[...], g, m_ref[...], v_ref[...])
        go_ref[...] = g
        d_ref[...] = delta
        mo_ref[...] = mn
        vo_ref[...] = vn

    blk = pl.BlockSpec((None, tr, c), lambda i: (i // per_layer, i % per_layer, 0))
    return _pcall(
        body, name=name, grid=(r // tr,),
        in_specs=[pl.BlockSpec((n_parts, tr, c), lambda i: (0, i, 0)), blk, blk, blk],
        out_specs=[blk] * 4, out_shape=[SDS(w.shape, F32)] * 4, compiler_params=_params(),
    )(recv, w, m, v)


VEC_ROWS = 32


def _small_allreduce_adamw(vec, w, m, v):
    def body(vec_ref, w_ref, m_ref, v_ref, g_ref, d_ref, mo_ref, vo_ref, gath, send_sems, recv_sems):
        pos = _mesh_pos()
        me = 4 * pos[0] + 2 * pos[1] + pos[2]
        sends, recvs = [], []
        for k in range(1, N_DEV):
            peer, peer_idx = _peer(pos, k)
            cp = pltpu.make_async_remote_copy(src_ref=vec_ref, dst_ref=gath.at[me], send_sem=send_sems.at[k - 1],
                                              recv_sem=recv_sems.at[k - 1], device_id=peer, device_id_type=MESH)
            cp.start()
            sends.append(cp)
            recvs.append(pltpu.make_async_remote_copy(src_ref=vec_ref, dst_ref=gath.at[peer_idx],
                                                      send_sem=send_sems.at[k - 1], recv_sem=recv_sems.at[k - 1],
                                                      device_id=peer, device_id_type=MESH))
        gath[me] = vec_ref[...]
        for cp in recvs:
            cp.wait_recv()
        for cp in sends:
            cp.wait_send()
        tot = gath[0]
        for s in range(1, N_DEV):
            tot = tot + gath[s]
        rowi = lax.broadcasted_iota(jnp.int32, (8, LANES), 0)
        mine = jnp.sum(jnp.where(rowi == me, tot[16:24, :], 0.0), axis=0, keepdims=True)
        g = jnp.concatenate([tot[0:16, :], jnp.broadcast_to(mine, (8, LANES)), tot[24:32, :]], axis=0)
        delta, mn, vn = _adam_math(w_ref[...], g, m_ref[...], v_ref[...])
        g_ref[...] = g
        d_ref[...] = delta
        mo_ref[...] = mn
        vo_ref[...] = vn

    vm = pl.BlockSpec(memory_space=pltpu.VMEM)
    return _pcall(
        body, name="small_allreduce_adamw", in_specs=[vm] * 4, out_specs=[vm] * 4,
        out_shape=[SDS((VEC_ROWS, LANES), F32)] * 4,
        scratch_shapes=[pltpu.VMEM((N_DEV, VEC_ROWS, LANES), F32), pltpu.SemaphoreType.DMA((N_DEV - 1,)),
                        pltpu.SemaphoreType.DMA((N_DEV - 1,))],
        compiler_params=pltpu.CompilerParams(has_side_effects=True),
    )(vec, w, m, v)


def _cols_to_slabs(a):
    r, c8 = a.shape
    return a.reshape(r, N_DEV, c8 // N_DEV).transpose(1, 0, 2)


def _slabs_to_cols(a):
    n, r, c = a.shape
    return a.transpose(1, 0, 2).reshape(r, n * c)


def _rows8(vec):
    return vec.reshape(-1, LANES)


def _pad_rows(a, rows):
    return jnp.pad(a, ((0, rows - a.shape[0]), (0, LANES - a.shape[1])))


def kernel(x, p, fox_norm, fox_w_in, fox_b_f, fox_w_out, dil_norm, dil_w_in, dil_w_out, ple_w_up, ple_w_gate, final_norm, loss_target, m_fox_norm, m_fox_w_in, m_fox_b_f, m_fox_w_out, m_dil_norm, m_dil_w_in, m_dil_w_out, m_ple_w_up, m_ple_w_gate, m_final_norm, v_fox_norm, v_fox_w_in, v_fox_b_f, v_fox_w_out, v_dil_norm, v_dil_w_in, v_dil_w_out, v_ple_w_up, v_ple_w_gate, v_final_norm):
    t = x.shape[1]
    d = D_MODEL
    xs, tgt = x[0], loss_target[0]
    p0, p1 = p[0, 0], p[1, 0]
    fox_cols = fox_w_in.shape[2]
    ple_dim = ple_w_up.shape[1]

    later = [dil_w_in[0].astype(BF16), dil_w_out[0].astype(BF16), ple_w_up.reshape(-1, LANES).astype(BF16),
             ple_w_gate.reshape(-1, d).astype(BF16), dil_norm]
    push = _push_start(later, False, "gather_later_start")
    gw = _all_gather([fox_w_in[0].astype(BF16), fox_w_out[0].astype(BF16)])
    w_fox_in = _slabs_to_cols(gw[0])
    w_fox_main = w_fox_in[:, :4 * d]
    w_fox_f = jnp.pad(w_fox_in[:, 4 * d:], ((0, 0), (0, LANES - FOX_HEADS)))
    w_fox_out = gw[1].reshape(d, d)
    b_pad = jnp.pad(fox_b_f, ((0, 0), (0, LANES - FOX_HEADS)))

    n0, r0 = _rms_fwd(xs, fox_norm + push[4][0:1, 0:1], "rms_fox")
    proj0 = _mm([n0], w_fox_main, "nn", "fox_in_proj", tiles=IN_PROJ_TILES)
    projf = _mm([n0], w_fox_f, "nn", "fox_gate_proj", tiles=IN_PROJ_TILES, out_dtype=F32)
    c_all = _fox_gate_fwd(projf, b_pad)
    qaug_fwd = _fox_aug(c_all, c_all, 1.0, 0.0, 0, FOX_AUG, "fox_aug_q_fwd")
    kaug = _fox_aug(c_all, c_all, -1.0, 0.0, FOX_AUG, 0, "fox_aug_k")
    o0, g0, lse0 = _fox_fwd(proj0, qaug_fwd, kaug)
    h1 = _mm_residual(g0, w_fox_out, "nn", xs, "fox_out_proj")

    landed = _push_wait(push[0], push[1], push[2], push[3], h1, False, "gather_later_wait")
    gl = [_fill_own(zone, own) for zone, own in zip(landed, later)]
    w_dil_in = _slabs_to_cols(gl[0])
    w_dil_out = gl[1].reshape(d, d)
    w_up = gl[2].reshape(N_DEV, 2, ple_dim, LANES).transpose(1, 2, 0, 3).reshape(2, ple_dim, d)
    w_gate = gl[3].reshape(N_DEV, 2, d // N_DEV, d).transpose(1, 0, 2, 3).reshape(2, d, d)
    dil_norm_full = gl[4].reshape(1, d)
    h2, u0, a0 = _ple_fwd(h1, p0, w_up[0], w_gate[0], "ple0_fwd")

    n1, r1 = _rms_fwd(h2, dil_norm_full, "rms_dil")
    proj1 = _mm([n1], w_dil_in, "nn", "dil_in_proj", tiles=IN_PROJ_TILES)
    n_heads = len(DIL_PATTERN) * DIL_HEADS
    slopes = 2.0 ** (-ALIBI_MAX_EXP * jnp.arange(1, n_heads + 1, dtype=F32) / n_heads)
    dil_o, dil_lse, dil_slopes = [], [], []
    for grp, (_, dil) in enumerate(DIL_PATTERN):
        sl = (slopes[grp * DIL_HEADS:(grp + 1) * DIL_HEADS] * dil).reshape(DIL_HEADS, 1, 1)
        og, lg = _dil_fwd(proj1, sl, grp, dil, f"dil_attn_fwd_{grp}")
        dil_o.append(og)
        dil_lse.append(lg)
        dil_slopes.append(sl)
    z1_col0 = 9 * d
    o1, g1, lse1 = _dil_mix(dil_o, dil_lse, proj1, z1_col0)
    h3 = _mm_residual(g1, w_dil_out, "nn", h2, "dil_out_proj")
    h4, u1, a1 = _ple_fwd(h3, p1, w_up[1], w_gate[1], "ple1_fwd")

    dh4, d_final_norm, loss_part = _final_bwd(h4, final_norm.reshape(1, d), tgt)

    du1, da1 = _ple_bwd_elem(dh4, u1, a1, "ple1_bwd_elem")
    dw_up1 = _dw(p1, du1, "ple1_dw_up")
    dw_gate1 = _dw(h3, da1, "ple1_dw_gate")
    dh3 = _mm_residual(da1, w_gate[1], "nt", dh4, "ple1_dh")

    dw_dil_out = _dw(g1, dh3, "dil_dw_out")
    do1, dz1, delta1 = _mm_gate_bwd(dh3, w_dil_out, proj1, z1_col0, o1, DIL_HEADS, "dil_dgate")
    n_grp = len(DIL_PATTERN)
    dqkv = [_dil_bwd(proj1, do1, lse1, delta1, dil_slopes[grp], grp, dil, f"dil_attn_bwd_{grp}")
            for grp, (_, dil) in enumerate(DIL_PATTERN)]
    dw_cols = [_dw(n1, dqkv[grp], f"dil_dw_in_{kind}{grp}", sub=kind) for kind in range(3) for grp in range(n_grp)]
    dw_dil_in = jnp.concatenate(dw_cols + [_dw(n1, dz1, "dil_dw_in_z")], axis=1)
    row_slabs = lambda a: a.reshape(N_DEV, a.shape[0] // N_DEV, a.shape[1])
    dil_slabs = [_cols_to_slabs(dw_dil_in), row_slabs(dw_dil_out), _cols_to_slabs(dw_up1), row_slabs(dw_gate1)]
    dil_push = _push_start(dil_slabs, True, "scatter_dil_start")
    group_major = lambda kb: jnp.where(kb < 3 * n_grp, (kb % 3) * n_grp + kb // 3, kb)
    dh2, d_dil_norm = _mm_in_bwd(dqkv + [dz1], w_dil_in, h2, dil_norm_full, r1, dh3, "dil_dx", w_kmap=group_major,
                                 after=dil_push[4])

    du0, da0 = _ple_bwd_elem(dh2, u0, a0, "ple0_bwd_elem")
    dw_up0 = _dw(p0, du0, "ple0_dw_up")
    dw_gate0 = _dw(h1, da0, "ple0_dw_gate")
    dh1 = _mm_residual(da0, w_gate[0], "nt", dh2, "ple0_dh")

    dw_fox_out = _dw(g0, dh1, "fox_dw_out")
    do0, dz0, delta0 = _mm_gate_bwd(dh1, w_fox_out, proj0, 3 * d, o0, FOX_HEADS, "fox_dgate")
    head_cols = lambda a: jnp.pad(a, ((0, 0), (0, LANES - FOX_HEADS)))
    lse_cols = head_cols(lse0.reshape(FOX_HEADS, t).T)
    qaug_bwd = _fox_aug(c_all, lse_cols, 1.0, -1.0, 0, FOX_AUG, "fox_aug_q_bwd")
    doaug = _fox_aug(delta0, delta0, -1.0, 0.0, 0, None, "fox_aug_do")
    dq0, dk0, dv0, dck_wide, dcq = _fox_bwd(proj0, do0, qaug_bwd, kaug, doaug)
    dc_query = head_cols(dcq[:, :, 0, :].reshape(FOX_HEADS, t).T)
    df, d_b_f = _fox_gate_bwd(projf, b_pad, dc_query, dck_wide)
    dproj0 = [dq0, dk0, dv0, dz0]
    dw_fox_parts = [_dw(n0, dpart, f"fox_dw_in_{s}") for s, dpart in enumerate(dproj0)]
    dw_fox_f = _dw(n0, df, "fox_dw_gate")
    dw_fox_in = jnp.concatenate(dw_fox_parts + [dw_fox_f[:, :FOX_HEADS]], axis=1)
    fox_slabs = [_cols_to_slabs(dw_fox_in), row_slabs(dw_fox_out), _cols_to_slabs(dw_up0), row_slabs(dw_gate0)]
    fox_push = _push_start(fox_slabs, True, "scatter_fox_start")
    grad_x, d_fox_norm = _mm_in_bwd(dproj0, w_fox_main, xs, fox_norm, r0, dh1, "fox_dx", more=(df, w_fox_f),
                                    after=fox_push[4])

    me = 4 * lax.axis_index("x") + 2 * lax.axis_index("y") + lax.axis_index("c")

    def landed(push, slabs, name):
        zones = _push_wait(push[0], push[1], push[2], push[3], grad_x, True, name)
        return [_fill_own(zone, lax.dynamic_index_in_dim(own, me, 0, keepdims=False)) for zone, own in zip(zones, slabs)]

    g_dil_in, g_dil_out, g_up1, g_gate1 = landed(dil_push, dil_slabs, "scatter_dil_wait")
    g_fox_in, g_fox_out, g_up0, g_gate0 = landed(fox_push, fox_slabs, "scatter_fox_wait")
    upd = {"fox_w_in": _adamw(g_fox_in, fox_w_in, m_fox_w_in, v_fox_w_in, "adamw_fox_w_in"),
           "fox_w_out": _adamw(g_fox_out, fox_w_out, m_fox_w_out, v_fox_w_out, "adamw_fox_w_out"),
           "dil_w_in": _adamw(g_dil_in, dil_w_in, m_dil_w_in, v_dil_w_in, "adamw_dil_w_in"),
           "dil_w_out": _adamw(g_dil_out, dil_w_out, m_dil_w_out, v_dil_w_out, "adamw_dil_w_out")}
    for nm, grads, params in (("ple_w_up", (g_up0, g_up1), (ple_w_up, m_ple_w_up, v_ple_w_up)),
                              ("ple_w_gate", (g_gate0, g_gate1), (ple_w_gate, m_ple_w_gate, v_ple_w_gate))):
        layers = [_adamw(g, *[a[l:l + 1] for a in params], f"adamw_{nm}_{l}") for l, g in enumerate(grads)]
        upd[nm] = [jnp.concatenate([layers[0][k], layers[1][k]], axis=0) for k in range(4)]

    loss_row = jnp.where(jnp.arange(LANES) == 0, loss_part, 0.0)
    vec = jnp.concatenate([_rows8(d_fox_norm), _rows8(d_final_norm), _rows8(d_dil_norm), d_b_f, loss_row,
                           jnp.zeros((VEC_ROWS - 26, LANES), F32)], axis=0)

    def small_pack(a_fox_norm, a_final_norm, a_dil_norm, a_b_f):
        return jnp.concatenate([_rows8(a_fox_norm), _rows8(a_final_norm), _pad_rows(a_dil_norm, 8),
                                _pad_rows(a_b_f, 8)], axis=0)

    sg, sd, sm, sv = _small_allreduce_adamw(
        vec, small_pack(fox_norm, final_norm, dil_norm, fox_b_f),
        small_pack(m_fox_norm, m_final_norm, m_dil_norm, m_fox_b_f),
        small_pack(v_fox_norm, v_final_norm, v_dil_norm, v_fox_b_f))

    def small_unpack(a):
        return {"fox_norm": a[0:8].reshape(1, d), "final_norm": a[8:16].reshape(d), "dil_norm": a[16:17],
                "fox_b_f": a[24:25, :FOX_HEADS]}

    loss = sg[25, 0]
    order = ["fox_norm", "fox_w_in", "fox_b_f", "fox_w_out", "dil_norm", "dil_w_in", "dil_w_out", "ple_w_up",
             "ple_w_gate", "final_norm"]
    out = [loss, grad_x[None]]
    for idx, small in enumerate((sg, sd, sm, sv)):
        sp = small_unpack(small)
        out += [sp[nm] if nm in sp else upd[nm][idx] for nm in order]
    return tuple(out)
```
